```python
import jax, jax.numpy as jnp
from jax import lax
import numpy as np

D_MODEL = 1024
BATCH = 2
SEQ = 8192
DEPTH = 2

HEAD_DIM = 64
ROPE_THETA = 10000.0
Q_BLOCK = 128
NORM_EPS = 1e-6
NEG_INF = -1e30

NSA_HEADS = 8
NSA_GROUPS = 2
NSA_HPG = NSA_HEADS // NSA_GROUPS
CMP_LEN = 32
CMP_STRIDE = 16
CMP_HIDDEN = 256
SEL_LEN = 64
SEL_TOPN = 16
WINDOW = 512
FORCE_SCORE = 1e4

RET_HEADS = 4
RET_QK_DIM = 64
RET_V_DIM = 128
RET_CHUNK = 128

FOX_HEADS = 8

D_FF = 2816
N_EXPERTS = 8
TOP_K = 2
D_FF_EXPERT = 3584
N_DENSE = (DEPTH + 1) // 2
N_MOE = DEPTH // 2

NSA_Q = NSA_HEADS * HEAD_DIM
NSA_KV = NSA_GROUPS * HEAD_DIM
RET_QK = RET_HEADS * RET_QK_DIM
RET_V = RET_HEADS * RET_V_DIM
FOX_W = FOX_HEADS * HEAD_DIM
IN_SPLITS = (NSA_Q, NSA_KV, NSA_KV, NSA_KV, NSA_KV, NSA_KV, NSA_KV, 3 * NSA_HEADS,
             RET_QK, RET_QK, RET_V, RET_V,
             FOX_W, FOX_W, FOX_W, FOX_HEADS,
             3 * D_MODEL)
D_IN = sum(IN_SPLITS)

kernel_name = 'hybrid_nsa_retention_fox_moe_block'


def rms_norm(x, g):
    xf = x.astype(jnp.float32)
    y = xf * lax.rsqrt(jnp.mean(xf * xf, axis=-1, keepdims=True) + NORM_EPS)
    return (y * g.astype(jnp.float32)).astype(x.dtype)


def rope(x, pos):
    d = x.shape[-1]
    inv = ROPE_THETA ** (-jnp.arange(0, d, 2, dtype=jnp.float32) / d)
    ang = pos[:, None] * inv[None, :]
    ang = jnp.concatenate([ang, ang], axis=-1)[:, None, :]
    xf = x.astype(jnp.float32)
    x1, x2 = jnp.split(xf, 2, axis=-1)
    rot = jnp.concatenate([-x2, x1], axis=-1)
    return (xf * jnp.cos(ang) + rot * jnp.sin(ang)).astype(x.dtype)


def _masked_softmax(s, mask):
    s = jnp.where(mask, s.astype(jnp.float32), NEG_INF)
    return jnp.where(mask, jax.nn.softmax(s, axis=-1), 0.0)


def _gather_blocks(blocks, idx):
    return jax.vmap(jax.vmap(lambda bl, ix: bl[ix]))(blocks, idx)


def _compress(x, pe, w1, w2, tok_idx):
    B, _, G, dh = x.shape
    blk = x[:, tok_idx] + pe[:, None, :]
    blk = jnp.moveaxis(blk, 3, 1)
    blk = blk.reshape(B, G, tok_idx.shape[0], CMP_LEN * dh)
    return jax.nn.silu(blk @ w1) @ w2


def native_sparse_attention(q, k_cmp, v_cmp, k_sel, v_sel, k_win, v_win, gate_logits,
                            ck_pe, ck_w1, ck_w2, cv_pe, cv_w1, cv_w2):
    B, T, H, dh = q.shape
    G = NSA_GROUPS
    f32 = jnp.float32
    q, k_cmp, v_cmp, k_sel, v_sel, k_win, v_win = [a.astype(f32) for a in (q, k_cmp, v_cmp, k_sel, v_sel, k_win, v_win)]
    n_cmp = (T - CMP_LEN) // CMP_STRIDE + 1
    n_sel = T // SEL_LEN
    top_n = min(SEL_TOPN, n_sel)
    n_qb = T // Q_BLOCK
    scale = dh ** -0.5

    tok_idx = jnp.arange(n_cmp)[:, None] * CMP_STRIDE + jnp.arange(CMP_LEN)[None, :]
    kc = _compress(k_cmp, ck_pe, ck_w1, ck_w2, tok_idx)
    vc = _compress(v_cmp, cv_pe, cv_w1, cv_w2, tok_idx)
    cmp_start = jnp.arange(n_cmp) * CMP_STRIDE
    cmp_end = cmp_start + CMP_LEN - 1
    sel_start = jnp.arange(n_sel) * SEL_LEN
    overlap = ((cmp_start[:, None] < sel_start[None, :] + SEL_LEN)
               & (cmp_start[:, None] + CMP_LEN > sel_start[None, :])).astype(f32)
    sel_id = jnp.arange(n_sel)

    ks_blk = k_sel.reshape(B, n_sel, SEL_LEN, G, dh).transpose(0, 3, 1, 2, 4)
    vs_blk = v_sel.reshape(B, n_sel, SEL_LEN, G, dh).transpose(0, 3, 1, 2, 4)
    kw_pad = jnp.pad(k_win, ((0, 0), (WINDOW, 0), (0, 0), (0, 0))).transpose(0, 2, 1, 3)
    vw_pad = jnp.pad(v_win, ((0, 0), (WINDOW, 0), (0, 0), (0, 0))).transpose(0, 2, 1, 3)
    qg = q.reshape(B, T, G, NSA_HPG, dh).transpose(0, 2, 3, 1, 4)

    def block(i):
        t0 = i * Q_BLOCK
        tq = t0 + jnp.arange(Q_BLOCK)
        qb = lax.dynamic_slice_in_dim(qg, t0, Q_BLOCK, axis=3) * scale
        s = jnp.einsum('bghqd,bgnd->bghqn', qb, kc)
        p = _masked_softmax(s, cmp_end[None, :] <= tq[:, None])
        o_cmp = jnp.einsum('bghqn,bgnd->bghqd', p, vc)
        imp = jnp.einsum('bghqn,ns->bgqs', p, overlap)
        cur = (tq // SEL_LEN)[:, None]
        forced = (sel_id == 0) | (sel_id == cur) | (sel_id == cur - 1)
        score = jnp.where(forced, FORCE_SCORE, imp)
        score = jnp.where(sel_id <= cur, score, NEG_INF)
        _, idx = lax.top_k(score, top_n)
        kg = _gather_blocks(ks_blk, idx).reshape(B, G, Q_BLOCK, top_n * SEL_LEN, dh)
        vg = _gather_blocks(vs_blk, idx).reshape(B, G, Q_BLOCK, top_n * SEL_LEN, dh)
        tok = (idx[..., None] * SEL_LEN + jnp.arange(SEL_LEN)).reshape(B, G, 1, Q_BLOCK, top_n * SEL_LEN)
        s = jnp.einsum('bghqd,bgqkd->bghqk', qb, kg)
        p = _masked_softmax(s, tok <= tq[:, None])
        o_sel = jnp.einsum('bghqk,bgqkd->bghqd', p, vg)
        kw = lax.dynamic_slice_in_dim(kw_pad, t0, Q_BLOCK + WINDOW, axis=2)
        vw = lax.dynamic_slice_in_dim(vw_pad, t0, Q_BLOCK + WINDOW, axis=2)
        kpos = t0 - WINDOW + jnp.arange(Q_BLOCK + WINDOW)
        rel = tq[:, None] - kpos[None, :]
        wmask = (rel >= 0) & (rel < WINDOW) & (kpos[None, :] >= 0)
        s = jnp.einsum('bghqd,bgkd->bghqk', qb, kw)
        o_win = jnp.einsum('bghqk,bgkd->bghqd', _masked_softmax(s, wmask), vw)
        return o_cmp, o_sel, o_win

    outs = lax.map(block, jnp.arange(n_qb))
    o_cmp, o_sel, o_win = [o.transpose(1, 0, 4, 2, 3, 5).reshape(B, T, H, dh) for o in outs]
    g = jax.nn.sigmoid(gate_logits.astype(f32)).reshape(B, T, 3, H, 1)
    o = g[:, :, 0] * o_cmp + g[:, :, 1] * o_sel + g[:, :, 2] * o_win
    return o.reshape(B, T, H * dh)


def retention(q, k, v, g):
    f32 = jnp.float32
    q, k, v, g = [a.astype(f32) for a in (q, k, v, g)]
    B, T, H, dk = q.shape
    dv = v.shape[-1]
    C = RET_CHUNK
    n_c = T // C
    log_g = jnp.log(1.0 - 2.0 ** (-5.0 - jnp.arange(H, dtype=f32)))
    n = jnp.arange(C, dtype=f32)
    diff = n[:, None] - n[None, :]
    causal = diff >= 0
    decay_in = jnp.where(causal[None], jnp.exp(jnp.where(causal, diff, 0.0)[None] * log_g[:, None, None]), 0.0)
    q_decay = jnp.exp((n[None, :] + 1.0) * log_g[:, None])[None, :, :, None]
    k_decay = jnp.exp((C - 1.0 - n)[None, :] * log_g[:, None])[None, :, :, None]
    chunk_decay = jnp.exp(C * log_g)[None, :, None, None]

    def to_chunks(a):
        return a.reshape(B, n_c, C, H, a.shape[-1]).transpose(1, 0, 3, 2, 4)

    def step(state, inp):
        qc, kc, vc = inp
        inner = jnp.einsum('bhnd,bhmd->bhnm', qc, kc) * decay_in
        o = jnp.einsum('bhnm,bhme->bhne', inner, vc) + jnp.einsum('bhnd,bhde->bhne', qc * q_decay, state)
        state = state * chunk_decay + jnp.einsum('bhmd,bhme->bhde', kc * k_decay, vc)
        return state, o

    state0 = jnp.zeros((B, H, dk, dv), f32)
    _, o = lax.scan(step, state0, (to_chunks(q), to_chunks(k), to_chunks(v)))
    o = o.transpose(1, 0, 3, 2, 4).reshape(B, T, H, dv)
    mu = jnp.mean(o, axis=-1, keepdims=True)
    var = jnp.mean(jnp.square(o - mu), axis=-1, keepdims=True)
    o = (o - mu) * lax.rsqrt(var + NORM_EPS)
    return (jax.nn.silu(g) * o).reshape(B, T, H * dv)


def forgetting_attention(q, k, v, f_logit):
    f32 = jnp.float32
    B, T, H, dh = q.shape
    n_qb = T // Q_BLOCK
    scale = dh ** -0.5
    qh, kh, vh = [a.astype(f32).transpose(0, 2, 1, 3) for a in (q, k, v)]
    cum = jnp.cumsum(jax.nn.log_sigmoid(f_logit.astype(f32)), axis=1).transpose(0, 2, 1)
    kpos = jnp.arange(T)

    def block(i):
        t0 = i * Q_BLOCK
        tq = t0 + jnp.arange(Q_BLOCK)
        qb = lax.dynamic_slice_in_dim(qh, t0, Q_BLOCK, axis=2)
        cq = lax.dynamic_slice_in_dim(cum, t0, Q_BLOCK, axis=2)
        s = jnp.einsum('bhqd,bhkd->bhqk', qb, kh) * scale + (cq[..., None] - cum[:, :, None, :])
        p = _masked_softmax(s, kpos[None, :] <= tq[:, None])
        return jnp.einsum('bhqk,bhkd->bhqd', p, vh)

    o = lax.map(block, jnp.arange(n_qb))
    return o.transpose(1, 0, 3, 2, 4).reshape(B, T, H * dh)


def token_mixing(h, w_in, ck_pe, ck_w1, ck_w2, cv_pe, cv_w1, cv_w2, fox_f_bias,
                 w_read_nsa, w_read_ret, w_read_fox, w_out):
    B, T, _ = h.shape
    pos = jnp.arange(T, dtype=jnp.float32)
    points = np.cumsum(IN_SPLITS)[:-1].tolist()
    (nq, nkc, nvc, nks, nvs, nkw, nvw, ngate, rq, rk, rv, rg,
     fq, fk, fv, ff, mg) = jnp.split(h @ w_in, points, axis=-1)

    def heads(a, n_h):
        return a.reshape(B, T, n_h, -1)

    G = NSA_GROUPS
    o_nsa = native_sparse_attention(
        rope(heads(nq, NSA_HEADS), pos), rope(heads(nkc, G), pos), heads(nvc, G),
        rope(heads(nks, G), pos), heads(nvs, G), rope(heads(nkw, G), pos), heads(nvw, G), ngate,
        ck_pe, ck_w1, ck_w2, cv_pe, cv_w1, cv_w2)
    o_ret = retention(rope(heads(rq, RET_HEADS), pos),
                      rope(heads(rk, RET_HEADS), pos) * (RET_QK_DIM ** -0.5),
                      heads(rv, RET_HEADS), heads(rg, RET_HEADS))
    o_fox = forgetting_attention(heads(fq, FOX_HEADS), heads(fk, FOX_HEADS), heads(fv, FOX_HEADS),
                                 ff + fox_f_bias)
    gates = jax.nn.sigmoid(mg.astype(jnp.float32)).reshape(B, T, 3, D_MODEL)
    merged = (gates[:, :, 0] * (o_nsa @ w_read_nsa)
              + gates[:, :, 1] * (o_ret @ w_read_ret)
              + gates[:, :, 2] * (o_fox @ w_read_fox))
    return (merged @ w_out).astype(h.dtype)


def swiglu(h, w1, w3, w2):
    return (jax.nn.silu(h @ w1) * (h @ w3)) @ w2


def moe_ffn(h, w_router, w1, w3, w2):
    logits = (h @ w_router).astype(jnp.float32)
    top_v, top_i = lax.top_k(logits, TOP_K)
    top_w = jax.nn.softmax(top_v, axis=-1)
    gate = jnp.sum(jax.nn.one_hot(top_i, N_EXPERTS, dtype=jnp.float32) * top_w[..., None], axis=-2).astype(h.dtype)
    y = jnp.zeros_like(h)
    for e in range(N_EXPERTS):
        y = y + gate[..., e:e + 1] * swiglu(h, w1[e], w3[e], w2[e])
    return y


def setup_inputs(seed: int = 0) -> dict:
    key = jax.random.key(seed)
    keys = iter(jax.random.split(key, 32))

    def nrm(shape, scale):
        return jax.random.normal(next(keys), shape, jnp.float32) * scale

    D = D_MODEL
    L_dh = CMP_LEN * HEAD_DIM
    return {
        'x': nrm((BATCH, SEQ, D), 1.0),
        'c': nrm((BATCH, D), 1.0),
        'ada_w': nrm((DEPTH, D, 6 * D), 0.5 * D ** -0.5),
        'ada_b': nrm((DEPTH, 6 * D), 0.02),
        'norm_mix': 1.0 + nrm((DEPTH, D), 0.02),
        'norm_ffn': 1.0 + nrm((DEPTH, D), 0.02),
        'w_in': nrm((DEPTH, D, D_IN), D ** -0.5),
        'cmp_k_pe': nrm((DEPTH, CMP_LEN, HEAD_DIM), 0.1),
        'cmp_k_w1': nrm((DEPTH, L_dh, CMP_HIDDEN), L_dh ** -0.5),
        'cmp_k_w2': nrm((DEPTH, CMP_HIDDEN, HEAD_DIM), CMP_HIDDEN ** -0.5),
        'cmp_v_pe': nrm((DEPTH, CMP_LEN, HEAD_DIM), 0.1),
        'cmp_v_w1': nrm((DEPTH, L_dh, CMP_HIDDEN), L_dh ** -0.5),
        'cmp_v_w2': nrm((DEPTH, CMP_HIDDEN, HEAD_DIM), CMP_HIDDEN ** -0.5),
        'fox_f_bias': 3.0 + nrm((DEPTH, FOX_HEADS), 0.5),
        'w_read_nsa': nrm((DEPTH, NSA_Q, D), NSA_Q ** -0.5),
        'w_read_ret': nrm((DEPTH, RET_V, D), RET_V ** -0.5),
        'w_read_fox': nrm((DEPTH, FOX_W, D), FOX_W ** -0.5),
        'w_out': nrm((DEPTH, D, D), D ** -0.5),
        'ffn_w1': nrm((N_DENSE, D, D_FF), D ** -0.5),
        'ffn_w3': nrm((N_DENSE, D, D_FF), D ** -0.5),
        'ffn_w2': nrm((N_DENSE, D_FF, D), D_FF ** -0.5),
        'router_w': nrm((N_MOE, D, N_EXPERTS), D ** -0.5),
        'moe_w1': nrm((N_MOE, N_EXPERTS, D, D_FF_EXPERT), D ** -0.5),
        'moe_w3': nrm((N_MOE, N_EXPERTS, D, D_FF_EXPERT), D ** -0.5),
        'moe_w2': nrm((N_MOE, N_EXPERTS, D_FF_EXPERT, D), D_FF_EXPERT ** -0.5),
        'final_norm': 1.0 + nrm((D,), 0.02),
    }


def reference(x, c, ada_w, ada_b, norm_mix, norm_ffn, w_in, cmp_k_pe, cmp_k_w1, cmp_k_w2,
              cmp_v_pe, cmp_v_w1, cmp_v_w2, fox_f_bias, w_read_nsa, w_read_ret, w_read_fox, w_out,
              ffn_w1, ffn_w3, ffn_w2, router_w, moe_w1, moe_w3, moe_w2, final_norm):
    for l in range(DEPTH):
        mod = jax.nn.silu(c) @ ada_w[l] + ada_b[l]
        sh1, sc1, g1, sh2, sc2, g2 = jnp.split(mod[:, None, :], 6, axis=-1)
        h = rms_norm(x, norm_mix[l]) * (1.0 + sc1) + sh1
        x = x + g1 * token_mixing(h, w_in[l], cmp_k_pe[l], cmp_k_w1[l], cmp_k_w2[l],
                                  cmp_v_pe[l], cmp_v_w1[l], cmp_v_w2[l], fox_f_bias[l],
                                  w_read_nsa[l], w_read_ret[l], w_read_fox[l], w_out[l])
        h = rms_norm(x, norm_ffn[l]) * (1.0 + sc2) + sh2
        if l % 2 == 0:
            f = swiglu(h, ffn_w1[l // 2], ffn_w3[l // 2], ffn_w2[l // 2])
        else:
            f = moe_ffn(h, router_w[l // 2], moe_w1[l // 2], moe_w3[l // 2], moe_w2[l // 2])
        x = x + g2 * f
    return rms_norm(x, final_norm)
```

```python
import functools
import math

import jax
import jax.numpy as jnp
import numpy as np
from jax import lax
from jax.experimental import pallas as pl
from jax.experimental.pallas import tpu as pltpu

F32 = jnp.float32
BF16 = jnp.bfloat16

D_MODEL = 1024
DEPTH = 2
HEAD_DIM = 64
ROPE_THETA = 10000.0
NORM_EPS = 1e-6
NEG_INF = -1e30
REMOVED = -3e38

NSA_HEADS = 8
NSA_GROUPS = 2
NSA_HPG = NSA_HEADS // NSA_GROUPS
CMP_LEN = 32
CMP_STRIDE = 16
CMP_HIDDEN = 256
SEL_LEN = 64
SEL_TOPN = 16
WINDOW = 512
FORCE_SCORE = 1e4
NSA_QBLOCK = 128

RET_HEADS = 4
RET_QK_DIM = 64
RET_V_DIM = 128
RET_CHUNK = 128

FOX_HEADS = 8

D_FF = 2816
N_EXPERTS = 8
D_FF_EXPERT = 3584

LANES = 128
VMEM_LIMIT = 56 * 1024 * 1024

P1_NQ = 0
P1_RQ = 1024
P1_RK = 1536
P1_NKC = 1792
P1_NKS = 1920
P1_NKW = 2048
P1_COLS = 2176
P2_MG = 0
P2_RV = 3072
P2_RG = 3584
P2_FQ = 4096
P2_FK = 5120
P2_FV = 5632
P2_NVC = 6144
P2_NVS = 6272
P2_NVW = 6400
P2_SMALL = 6528
P2_COLS = 6656


def _cparams(*sem):
    return pltpu.CompilerParams(dimension_semantics=tuple(sem), vmem_limit_bytes=VMEM_LIMIT)


def _sigmoid(x):
    return 1.0 / (1.0 + jnp.exp(-x))


def _dot(a, b):
    return jnp.dot(a, b, preferred_element_type=F32)


def _dot_nt(a, b):
    return lax.dot_general(a, b, (((1,), (1,)), ((), ())), preferred_element_type=F32)


def _dot_tn(a, b):
    return lax.dot_general(a, b, (((0,), (0,)), ((), ())), preferred_element_type=F32)


def _split3(x):
    hi = x.astype(BF16)
    r1 = x - hi.astype(F32)
    mid = r1.astype(BF16)
    lo = (r1 - mid.astype(F32)).astype(BF16)
    return hi, mid, lo


def _norm_mod(x, nw, sc, sh):
    ms = jnp.mean(x * x, axis=-1, keepdims=True)
    y = x * lax.rsqrt(ms + NORM_EPS) * nw
    return y * (1.0 + sc) + sh


def _mod_kernel(c_ref, w_ref, b_ref, o_ref):
    c = c_ref[...]
    s = c * _sigmoid(c)
    o_ref[0] = _dot(s.astype(BF16), w_ref[0].astype(BF16)) + b_ref[0]


def modulation(c, ada_w, ada_b):
    B, D = c.shape
    depth = ada_w.shape[0]
    rows = 8
    c_pad = jnp.zeros((rows, D), F32).at[:B].set(c)
    out = pl.pallas_call(
        _mod_kernel,
        grid=(depth, 6),
        in_specs=[pl.BlockSpec((rows, D), lambda l, j: (0, 0)),
                  pl.BlockSpec((1, D, D), lambda l, j: (l, 0, j)),
                  pl.BlockSpec((1, 1, D), lambda l, j: (l, 0, j))],
        out_specs=pl.BlockSpec((1, rows, D), lambda l, j: (l, 0, j)),
        out_shape=jax.ShapeDtypeStruct((depth, rows, 6 * D), F32),
        compiler_params=_cparams("parallel", "parallel"),
        name="modulation",
    )(c_pad, ada_w, ada_b.reshape(depth, 1, 6 * D))
    return out[:, :B].reshape(depth, B, 6, 1, D)


def _proj_plain_kernel(x_ref, nw_ref, sc_ref, sh_ref, w_ref, o_ref, h_ref):
    @pl.when(pl.program_id(1) == 0)
    def _():
        h_ref[...] = _norm_mod(x_ref[...], nw_ref[...], sc_ref[...], sh_ref[...]).astype(BF16)

    o_ref[...] = _dot(h_ref[...], w_ref[...]).astype(o_ref.dtype)


def _proj_rope_kernel(x_ref, nw_ref, sc_ref, sh_ref, w_ref, cos_ref, sin_ref, o_ref, *, scales):
    h = _norm_mod(x_ref[...], nw_ref[...], sc_ref[...], sh_ref[...]).astype(BF16)
    y = _dot(h, w_ref[...])
    cos = cos_ref[...]
    sin = sin_ref[...]
    lane = lax.broadcasted_iota(jnp.int32, cos.shape, 1)
    first_half = (lane % HEAD_DIM) < (HEAD_DIM // 2)
    for g, scale in enumerate(scales):
        yg = y[:, g * LANES:(g + 1) * LANES]
        rot = jnp.where(first_half, pltpu.roll(yg, LANES - HEAD_DIM // 2, 1),
                        pltpu.roll(yg, HEAD_DIM // 2, 1))
        r = yg * cos + rot * sin
        if scale != 1.0:
            r = r * scale
        o_ref[:, g * LANES:(g + 1) * LANES] = r.astype(o_ref.dtype)


def _mod_specs(T, tm, sc_idx, sh_idx, nargs):
    per_b = T // tm
    if nargs == 1:
        return [pl.BlockSpec((None, None, 1, D_MODEL), lambda i: (i // per_b, sc_idx, 0, 0)),
                pl.BlockSpec((None, None, 1, D_MODEL), lambda i: (i // per_b, sh_idx, 0, 0))]
    return [pl.BlockSpec((None, None, 1, D_MODEL), lambda i, j: (i // per_b, sc_idx, 0, 0)),
            pl.BlockSpec((None, None, 1, D_MODEL), lambda i, j: (i // per_b, sh_idx, 0, 0))]


def proj_plain(x, mod_l, nw, w, T, *, tm=1024, tn=512):
    M, D = x.shape
    N = w.shape[1]
    return pl.pallas_call(
        _proj_plain_kernel,
        grid=(M // tm, N // tn),
        in_specs=[pl.BlockSpec((tm, D), lambda i, j: (i, 0)),
                  pl.BlockSpec((1, D), lambda i, j: (0, 0))]
        + _mod_specs(T, tm, 1, 0, 2)
        + [pl.BlockSpec((D, tn), lambda i, j: (0, j))],
        out_specs=pl.BlockSpec((tm, tn), lambda i, j: (i, j)),
        out_shape=jax.ShapeDtypeStruct((M, N), BF16),
        scratch_shapes=[pltpu.VMEM((tm, D), BF16)],
        compiler_params=_cparams("parallel", "arbitrary"),
        name="proj_plain",
    )(x, nw, mod_l, mod_l, w)


def proj_rope(x, mod_l, nw, w, cos, sin, scales, T, *, tm=512):
    M, D = x.shape
    N = w.shape[1]
    per_b = T // tm
    return pl.pallas_call(
        functools.partial(_proj_rope_kernel, scales=scales),
        grid=(M // tm,),
        in_specs=[pl.BlockSpec((tm, D), lambda i: (i, 0)),
                  pl.BlockSpec((1, D), lambda i: (0, 0))]
        + _mod_specs(T, tm, 1, 0, 1)
        + [pl.BlockSpec((D, N), lambda i: (0, 0)),
           pl.BlockSpec((tm, LANES), lambda i: (i % per_b, 0)),
           pl.BlockSpec((tm, LANES), lambda i: (i % per_b, 0))],
        out_specs=pl.BlockSpec((tm, N), lambda i: (i, 0)),
        out_shape=jax.ShapeDtypeStruct((M, N), BF16),
        compiler_params=_cparams("parallel"),
        name="proj_rope",
    )(x, nw, mod_l, mod_l, w, cos, sin)


def rope_tables(T):
    d = HEAD_DIM
    pos = jnp.arange(T, dtype=F32)
    inv = ROPE_THETA ** (-jnp.arange(0, d, 2, dtype=F32) / d)
    ang = pos[:, None] * inv[None, :]
    cos = jnp.cos(ang)
    sin = jnp.sin(ang)
    cos_t = jnp.concatenate([cos, cos, cos, cos], axis=-1)
    sin_t = jnp.concatenate([-sin, sin, -sin, sin], axis=-1)
    return cos_t, sin_t


def _pad_heads(w, n_heads, half_of_head):
    D = w.shape[0]
    w = w.reshape(D, n_heads, HEAD_DIM)
    z = jnp.zeros_like(w)
    halves = np.array([half_of_head(h) for h in range(n_heads)])
    lo = jnp.where(halves[None, :, None] == 0, w, z)
    hi = jnp.where(halves[None, :, None] == 1, w, z)
    return jnp.concatenate([lo, hi], axis=-1).reshape(D, n_heads * LANES)


def split_w_in(w_in):
    sizes = [512, 128, 128, 128, 128, 128, 128, 24, 256, 256, 512, 512, 512, 512, 512, 8, 3072]
    offs = np.cumsum([0] + sizes)
    (nq, nkc, nvc, nks, nvs, nkw, nvw, ngate, rq, rk, rv, rg, fq, fk, fv, ff, mg) = [
        w_in[:, offs[i]:offs[i + 1]] for i in range(len(sizes))]
    D = w_in.shape[0]
    nq_p = _pad_heads(nq, NSA_HEADS, lambda h: h // NSA_HPG)
    rq_p = _pad_heads(rq, RET_HEADS, lambda h: h % 2)
    fq_p = _pad_heads(fq, FOX_HEADS, lambda h: h % 2) * (HEAD_DIM ** -0.5)
    small = jnp.concatenate([ngate, ff, jnp.zeros((D, LANES - 32), w_in.dtype)], axis=-1)
    w1 = jnp.concatenate([nq_p, rq_p, rk, nkc, nks, nkw], axis=-1).astype(BF16)
    w2 = jnp.concatenate([mg, rv, rg, fq_p, fk, fv, nvc, nvs, nvw, small], axis=-1).astype(BF16)
    assert w1.shape[1] == P1_COLS and w2.shape[1] == P2_COLS
    return w1, w2


def p1_scales():
    s = [1.0] * (P1_COLS // LANES)
    for g in range(P1_NQ // LANES, P1_RQ // LANES):
        s[g] = HEAD_DIM ** -0.5
    for g in range(P1_RK // LANES, P1_NKC // LANES):
        s[g] = RET_QK_DIM ** -0.5
    return tuple(s)


def _compress_kernel(x_ref, pe_ref, w1_ref, w2_ref, o_ref):
    r = x_ref[...]
    half = r.shape[1]
    w1 = w1_ref[...]
    a = _dot(r, w1[:half])
    b = _dot(r, w1[half:])
    pe = _dot(pe_ref[...], w1)[0:1]
    n = a.shape[0]
    hid = a + pltpu.roll(b, n - 1, 0) + pe
    hid = hid * _sigmoid(hid)
    o_ref[...] = _dot(hid.astype(BF16), w2_ref[...]).astype(o_ref.dtype)


def compress(xr, pe, w1, w2):
    _, B, G, R, W = xr.shape
    H = w1.shape[-1]
    return pl.pallas_call(
        _compress_kernel,
        grid=(2, B, G),
        in_specs=[pl.BlockSpec((None, None, None, R, W), lambda s, b, g: (s, b, g, 0, 0)),
                  pl.BlockSpec((None, 8, 2 * W), lambda s, b, g: (s, 0, 0)),
                  pl.BlockSpec((None, 2 * W, H), lambda s, b, g: (s, 0, 0)),
                  pl.BlockSpec((None, H, HEAD_DIM), lambda s, b, g: (s, 0, 0))],
        out_specs=pl.BlockSpec((None, None, None, R, HEAD_DIM), lambda s, b, g: (s, b, g, 0, 0)),
        out_shape=jax.ShapeDtypeStruct((2, B, G, R, HEAD_DIM), BF16),
        compiler_params=_cparams("parallel", "parallel", "parallel"),
        name="nsa_compress",
    )(xr, pe, w1, w2)


def _stack_heads(q_ref, g):
    return jnp.concatenate(
        [q_ref[:, (NSA_HPG * g + hh) * LANES:(NSA_HPG * g + hh + 1) * LANES] for hh in range(NSA_HPG)],
        axis=0)


def _store_heads(o_ref, g, o, tq):
    for hh in range(NSA_HPG):
        h = NSA_HPG * g + hh
        o_ref[:, h * LANES:(h + 1) * LANES] = o[hh * tq:(hh + 1) * tq].astype(o_ref.dtype)


def _nsa_cmp_kernel(q_ref, kc_ref, vc_ref, ov_ref, o_ref, m_ref, *, tq, n_sel, top_n):
    t0 = pl.program_id(1) * tq
    kc = kc_ref[...]
    vc = vc_ref[...]
    ncp = kc.shape[0]
    nsp = ov_ref.shape[0]
    rows = NSA_HPG * tq
    n_idx = lax.broadcasted_iota(jnp.int32, (rows, ncp), 1)
    t_idx = t0 + lax.broadcasted_iota(jnp.int32, (rows, ncp), 0) % tq
    valid = (n_idx * CMP_STRIDE + (CMP_LEN - 1)) <= t_idx
    j_idx = lax.broadcasted_iota(jnp.int32, (nsp, tq), 0)
    cur = (t0 + lax.broadcasted_iota(jnp.int32, (nsp, tq), 1)) // SEL_LEN
    forced = (j_idx == 0) | (j_idx == cur) | (j_idx == cur - 1)
    j_f = j_idx.astype(F32)
    for g in range(NSA_GROUPS):
        q = _stack_heads(q_ref, g)
        s = jnp.where(valid, _dot_nt(q, kc), NEG_INF)
        m = jnp.max(s, axis=-1, keepdims=True)
        e = jnp.where(valid, jnp.exp(s - m), 0.0)
        l = jnp.sum(e, axis=-1, keepdims=True)
        p = e / jnp.where(l > 0.0, l, 1.0)
        _store_heads(o_ref, g, _dot(p.astype(BF16), vc), tq)
        psum = p[0:tq]
        for hh in range(1, NSA_HPG):
            psum = psum + p[hh * tq:(hh + 1) * tq]
        imp_t = _dot_nt(ov_ref[...], psum.astype(BF16))
        score = jnp.where(forced, FORCE_SCORE, imp_t)
        score = jnp.where(j_idx <= cur, score, NEG_INF)
        score = jnp.where(j_idx < n_sel, score, REMOVED)
        sel = jnp.zeros((nsp, tq), F32)
        for _ in range(top_n):
            mx = jnp.max(score, axis=0, keepdims=True)
            idx = jnp.min(jnp.where(score == mx, j_f, float(nsp)), axis=0, keepdims=True)
            hit = j_f == idx
            sel = jnp.where(hit, 1.0, sel)
            score = jnp.where(hit, REMOVED, score)
        sel = jnp.where(j_idx <= cur, sel, 0.0)
        m_ref[g] = sel.T.astype(m_ref.dtype)


def nsa_cmp_select(p1, kc, vc, ov_t, T):
    B = p1.shape[0]
    tq = NSA_QBLOCK
    ncp = kc.shape[1]
    nsp = ov_t.shape[0]
    n_sel = T // SEL_LEN
    return pl.pallas_call(
        functools.partial(_nsa_cmp_kernel, tq=tq, n_sel=n_sel, top_n=min(SEL_TOPN, n_sel)),
        grid=(B, T // tq),
        in_specs=[pl.BlockSpec((None, tq, NSA_HEADS * LANES), lambda b, i: (b, i, 0)),
                  pl.BlockSpec((None, ncp, LANES), lambda b, i: (b, 0, 0)),
                  pl.BlockSpec((None, ncp, LANES), lambda b, i: (b, 0, 0)),
                  pl.BlockSpec((nsp, ncp), lambda b, i: (0, 0))],
        out_specs=[pl.BlockSpec((None, tq, NSA_HEADS * LANES), lambda b, i: (b, i, 0)),
                   pl.BlockSpec((None, NSA_GROUPS, tq, nsp), lambda b, i: (b, 0, i, 0))],
        out_shape=[jax.ShapeDtypeStruct((B, T, NSA_HEADS * LANES), BF16),
                   jax.ShapeDtypeStruct((B, NSA_GROUPS, T, nsp), BF16)],
        compiler_params=_cparams("parallel", "parallel"),
        name="nsa_cmp_select",
    )(p1, kc, vc, ov_t)


def _nsa_sel_kernel(q_ref, k_ref, v_ref, m_ref, e_ref, o_ref, *, tq, tk):
    t0 = pl.program_id(1) * tq
    n_tiles = (t0 + tq + tk - 1) // tk
    rows = NSA_HPG * tq
    trow = t0 + lax.broadcasted_iota(jnp.int32, (tq, tk), 0)
    kcol = lax.broadcasted_iota(jnp.int32, (tq, tk), 1)
    for g in range(NSA_GROUPS):
        q = _stack_heads(q_ref, g)
        msel = m_ref[g]

        def body(j, carry, q=q, msel=msel):
            m, l, acc = carry
            start = pl.multiple_of(j * tk, tk)
            ks = k_ref[pl.ds(start, tk), :]
            vs = v_ref[pl.ds(start, tk), :]
            allow = _dot(msel, e_ref[:, pl.ds(start, tk)])
            ok = (allow > 0.5) & ((kcol + start) <= trow)
            bias = jnp.where(ok, 0.0, NEG_INF)
            s = _dot_nt(q, ks) + jnp.concatenate([bias] * NSA_HPG, axis=0)
            m_new = jnp.maximum(m, jnp.max(s, axis=-1, keepdims=True))
            alpha = jnp.exp(m - m_new)
            p = jnp.exp(s - m_new)
            l = alpha * l + jnp.sum(p, axis=-1, keepdims=True)
            acc = alpha * acc + _dot(p.astype(BF16), vs)
            return m_new, l, acc

        init = (jnp.full((rows, 1), NEG_INF, F32), jnp.zeros((rows, 1), F32),
                jnp.zeros((rows, LANES), F32))
        m, l, acc = lax.fori_loop(0, n_tiles, body, init)
        _store_heads(o_ref, g, acc / l, tq)


def nsa_selected(p1, p2, sel, e_mat, T, *, tk=512):
    B = p1.shape[0]
    tq = NSA_QBLOCK
    nsp = sel.shape[-1]
    return pl.pallas_call(
        functools.partial(_nsa_sel_kernel, tq=tq, tk=tk),
        grid=(B, T // tq),
        in_specs=[pl.BlockSpec((None, tq, NSA_HEADS * LANES), lambda b, i: (b, i, 0)),
                  pl.BlockSpec((None, T, LANES), lambda b, i: (b, 0, P1_NKS // LANES)),
                  pl.BlockSpec((None, T, LANES), lambda b, i: (b, 0, P2_NVS // LANES)),
                  pl.BlockSpec((None, NSA_GROUPS, tq, nsp), lambda b, i: (b, 0, i, 0)),
                  pl.BlockSpec((nsp, T), lambda b, i: (0, 0))],
        out_specs=pl.BlockSpec((None, tq, NSA_HEADS * LANES), lambda b, i: (b, i, 0)),
        out_shape=jax.ShapeDtypeStruct((B, T, NSA_HEADS * LANES), BF16),
        compiler_params=_cparams("parallel", "parallel"),
        name="nsa_selected",
    )(p1, p1, p2, sel, e_mat)


def _nsa_win_kernel(q_ref, k_ref, v_ref, o_ref, *, tq):
    t0 = pl.program_id(1) * tq
    span = WINDOW + tq
    start = pl.multiple_of(jnp.maximum(t0 - WINDOW, 0), tq)
    ks = k_ref[pl.ds(start, span), :]
    vs = v_ref[pl.ds(start, span), :]
    rows = NSA_HPG * tq
    t_idx = t0 + lax.broadcasted_iota(jnp.int32, (rows, span), 0) % tq
    kpos = start + lax.broadcasted_iota(jnp.int32, (rows, span), 1)
    ok = (kpos <= t_idx) & ((t_idx - kpos) < WINDOW)
    for g in range(NSA_GROUPS):
        q = _stack_heads(q_ref, g)
        s = jnp.where(ok, _dot_nt(q, ks), NEG_INF)
        m = jnp.max(s, axis=-1, keepdims=True)
        p = jnp.exp(s - m)
        l = jnp.sum(p, axis=-1, keepdims=True)
        _store_heads(o_ref, g, _dot(p.astype(BF16), vs) / l, tq)


def nsa_window(p1, p2, T):
    B = p1.shape[0]
    tq = NSA_QBLOCK
    return pl.pallas_call(
        functools.partial(_nsa_win_kernel, tq=tq),
        grid=(B, T // tq),
        in_specs=[pl.BlockSpec((None, tq, NSA_HEADS * LANES), lambda b, i: (b, i, 0)),
                  pl.BlockSpec((None, T, LANES), lambda b, i: (b, 0, P1_NKW // LANES)),
                  pl.BlockSpec((None, T, LANES), lambda b, i: (b, 0, P2_NVW // LANES))],
        out_specs=pl.BlockSpec((None, tq, NSA_HEADS * LANES), lambda b, i: (b, i, 0)),
        out_shape=jax.ShapeDtypeStruct((B, T, NSA_HEADS * LANES), BF16),
        compiler_params=_cparams("parallel", "parallel"),
        name="nsa_window",
    )(p1, p1, p2)


def _retention_kernel(q_ref, k_ref, v_ref, g_ref, din_ref, qd_ref, kd_ref, cd_ref, o_ref, st_ref):
    @pl.when(pl.program_id(0) == 0)
    def _():
        st_ref[...] = jnp.zeros_like(st_ref)

    B = q_ref.shape[0]
    for b in range(B):
        for h in range(RET_HEADS):
            lanes = slice(h * LANES, (h + 1) * LANES)
            qh = q_ref[b, :, lanes]
            kp = k_ref[b, :, (h // 2) * LANES:(h // 2 + 1) * LANES]
            vh = v_ref[b, :, lanes]
            st = st_ref[b, h]
            inner = _dot_nt(qh, kp) * din_ref[h]
            o = _dot(inner.astype(BF16), vh) + _dot(qh, st.astype(BF16)) * qd_ref[h]
            kd = (kp.astype(F32) * kd_ref[h]).astype(BF16)
            st_ref[b, h] = st * cd_ref[h, 0:1, :] + _dot_tn(kd, vh)
            mu = jnp.mean(o, axis=-1, keepdims=True)
            d = o - mu
            var = jnp.mean(d * d, axis=-1, keepdims=True)
            on = d * lax.rsqrt(var + NORM_EPS)
            gh = g_ref[b, :, lanes].astype(F32)
            o_ref[b, :, lanes] = (gh * _sigmoid(gh) * on).astype(o_ref.dtype)


def retention_consts():
    C = RET_CHUNK
    H = RET_HEADS
    log_g = jnp.log(1.0 - 2.0 ** (-5.0 - jnp.arange(H, dtype=F32)))
    n = jnp.arange(C, dtype=F32)
    diff = n[:, None] - n[None, :]
    causal = diff >= 0
    decay_in = jnp.where(causal[None], jnp.exp(jnp.where(causal, diff, 0.0)[None] * log_g[:, None, None]), 0.0)
    q_decay = jnp.exp((n[None, :] + 1.0) * log_g[:, None])
    k_decay = jnp.exp((C - 1.0 - n)[None, :] * log_g[:, None])
    chunk_decay = jnp.exp(C * log_g)
    qd = jnp.broadcast_to(q_decay[:, :, None], (H, C, LANES))
    kd = jnp.broadcast_to(k_decay[:, :, None], (H, C, LANES))
    cd = jnp.broadcast_to(chunk_decay[:, None, None], (H, 8, LANES))
    return decay_in, qd, kd, cd


def retention(p1, p2, consts, T):
    B = p1.shape[0]
    C = RET_CHUNK
    din, qd, kd, cd = consts
    W = RET_HEADS * LANES
    full = lambda shape: pl.BlockSpec(shape, lambda c: (0,) * len(shape))
    return pl.pallas_call(
        _retention_kernel,
        grid=(T // C,),
        in_specs=[pl.BlockSpec((B, C, W), lambda c: (0, c, P1_RQ // W)),
                  pl.BlockSpec((B, C, W // 2), lambda c: (0, c, P1_RK // (W // 2))),
                  pl.BlockSpec((B, C, W), lambda c: (0, c, P2_RV // W)),
                  pl.BlockSpec((B, C, W), lambda c: (0, c, P2_RG // W)),
                  full(din.shape), full(qd.shape), full(kd.shape), full(cd.shape)],
        out_specs=pl.BlockSpec((B, C, W), lambda c: (0, c, 0)),
        out_shape=jax.ShapeDtypeStruct((B, T, W), BF16),
        scratch_shapes=[pltpu.VMEM((B, RET_HEADS, LANES, LANES), F32)],
        compiler_params=_cparams("arbitrary"),
        name="retention",
    )(p1, p1, p2, p2, din, qd, kd, cd)


def _fox_cum_kernel(f_ref, b_ref, o_ref):
    x = f_ref[...] + b_ref[...]
    ls = jnp.minimum(x, 0.0) - jnp.log1p(jnp.exp(-jnp.abs(x)))
    R = x.shape[0]
    ki = lax.broadcasted_iota(jnp.int32, (LANES, LANES), 0)
    ji = lax.broadcasted_iota(jnp.int32, (LANES, LANES), 1)
    upper = jnp.where(ki <= ji, 1.0, 0.0).astype(BF16)
    hi, mid, lo = _split3(ls)
    rowcum = _dot(hi, upper) + _dot(mid, upper) + _dot(lo, upper)
    tot = jnp.broadcast_to(rowcum[:, LANES - 1:LANES], (R, LANES))
    ri = lax.broadcasted_iota(jnp.int32, (R, R), 0)
    ci = lax.broadcasted_iota(jnp.int32, (R, R), 1)
    lower = jnp.where(ci < ri, 1.0, 0.0).astype(BF16)
    hi, mid, lo = _split3(tot)
    offs = _dot(lower, hi) + _dot(lower, mid) + _dot(lower, lo)
    o_ref[...] = rowcum + offs


def fox_cum(f_logit, bias):
    B, H, R, _ = f_logit.shape
    return pl.pallas_call(
        _fox_cum_kernel,
        grid=(B, H),
        in_specs=[pl.BlockSpec((None, None, R, LANES), lambda b, h: (b, h, 0, 0)),
                  pl.BlockSpec((None, 1, LANES), lambda b, h: (h, 0, 0))],
        out_specs=pl.BlockSpec((None, None, R, LANES), lambda b, h: (b, h, 0, 0)),
        out_shape=jax.ShapeDtypeStruct((B, H, R, LANES), F32),
        compiler_params=_cparams("parallel", "parallel"),
        name="fox_cum",
    )(f_logit, bias)


def _fox_kernel(q_ref, k_ref, v_ref, c_ref, o_ref, *, tq):
    i = pl.program_id(2)
    tk = tq
    row = lax.broadcasted_iota(jnp.int32, (tq, tk), 0)
    col = lax.broadcasted_iota(jnp.int32, (tq, tk), 1)
    lane = lax.broadcasted_iota(jnp.int32, (tq, LANES), 1)
    outs = []
    for hh in range(2):
        q = q_ref[:, hh * LANES:(hh + 1) * LANES]

        def step(j, carry, masked, q=q, hh=hh):
            m, l, acc = carry
            start = pl.multiple_of(j * tk, tk)
            ks = k_ref[pl.ds(start, tk), :]
            vs = v_ref[pl.ds(start, tk), :]
            s = _dot_nt(q, ks) - c_ref[hh, :, pl.ds(start, tk)]
            if masked:
                s = jnp.where(col <= row, s, NEG_INF)
            m_new = jnp.maximum(m, jnp.max(s, axis=-1, keepdims=True))
            alpha = jnp.exp(m - m_new)
            p = jnp.exp(s - m_new)
            l = alpha * l + jnp.sum(p, axis=-1, keepdims=True)
            acc = alpha * acc + _dot(p.astype(BF16), vs)
            return m_new, l, acc

        init = (jnp.full((tq, 1), NEG_INF, F32), jnp.zeros((tq, 1), F32), jnp.zeros((tq, LANES), F32))
        carry = lax.fori_loop(0, i, functools.partial(step, masked=False), init)
        m, l, acc = step(i, carry, True)
        outs.append(acc / l)
    o_ref[...] = jnp.where(lane < HEAD_DIM, outs[0], outs[1]).astype(o_ref.dtype)


def fox_attention(p2, cum, T, *, tq=512):
    B = p2.shape[0]
    HP = FOX_HEADS // 2
    return pl.pallas_call(
        functools.partial(_fox_kernel, tq=tq),
        grid=(B, HP, T // tq),
        in_specs=[pl.BlockSpec((None, tq, 2 * LANES), lambda b, h, i: (b, i, P2_FQ // (2 * LANES) + h)),
                  pl.BlockSpec((None, T, LANES), lambda b, h, i: (b, 0, P2_FK // LANES + h)),
                  pl.BlockSpec((None, T, LANES), lambda b, h, i: (b, 0, P2_FV // LANES + h)),
                  pl.BlockSpec((None, None, 2, 1, T), lambda b, h, i: (b, h, 0, 0, 0))],
        out_specs=pl.BlockSpec((None, tq, LANES), lambda b, h, i: (b, i, h)),
        out_shape=jax.ShapeDtypeStruct((B, T, FOX_HEADS * HEAD_DIM), BF16),
        compiler_params=_cparams("parallel", "parallel", "parallel"),
        name="fox_attention",
    )(p2, p2, p2, cum)


def _readout_kernel(ocmp_ref, osel_ref, owin_ref, small_ref, oret_ref, ofox_ref, mg_ref, x_ref, g1_ref,
                    ex_ref, wn_ref, wr_ref, wf_ref, wo_ref, o_ref):
    W = NSA_HEADS * LANES
    gs = _sigmoid(small_ref[...].astype(F32)).astype(BF16)
    ge = _dot(gs, ex_ref[...])
    onsa = (ge[:, :W] * ocmp_ref[...].astype(F32) + ge[:, W:2 * W] * osel_ref[...].astype(F32)
            + ge[:, 2 * W:] * owin_ref[...].astype(F32))
    D = D_MODEL
    merged = (_sigmoid(mg_ref[:, :D].astype(F32)) * _dot(onsa.astype(BF16), wn_ref[...])
              + _sigmoid(mg_ref[:, D:2 * D].astype(F32)) * _dot(oret_ref[...], wr_ref[...])
              + _sigmoid(mg_ref[:, 2 * D:].astype(F32)) * _dot(ofox_ref[...], wf_ref[...]))
    y = _dot(merged.astype(BF16), wo_ref[...])
    o_ref[...] = x_ref[...] + g1_ref[...] * y


def readout(o_cmp, o_sel, o_win, p2, o_ret, o_fox, x, mod_l, ex, wn, wr, wf, wo, T, *, tm=512):
    M, D = x.shape
    per_b = T // tm
    W = NSA_HEADS * LANES
    row = lambda width, col=0: pl.BlockSpec((tm, width), lambda i: (i, col))
    full = lambda a: pl.BlockSpec(a.shape, lambda i: (0,) * a.ndim)
    return pl.pallas_call(
        _readout_kernel,
        grid=(M // tm,),
        in_specs=[row(W), row(W), row(W), row(LANES, P2_SMALL // LANES), row(512), row(512),
                  row(3 * D, 0), row(D),
                  pl.BlockSpec((None, None, 1, D), lambda i: (i // per_b, 2, 0, 0)),
                  full(ex), full(wn), full(wr), full(wf), full(wo)],
        out_specs=row(D),
        out_shape=jax.ShapeDtypeStruct((M, D), F32),
        compiler_params=_cparams("parallel"),
        name="mixer_readout",
    )(o_cmp, o_sel, o_win, p2, o_ret, o_fox, p2, x, mod_l, ex, wn, wr, wf, wo)


def nsa_gate_expand():
    ex = np.zeros((LANES, 3 * NSA_HEADS * LANES), np.float32)
    for br in range(3):
        for h in range(NSA_HEADS):
            c0 = br * NSA_HEADS * LANES + h * LANES
            ex[br * NSA_HEADS + h, c0:c0 + LANES] = 1.0
    return jnp.asarray(ex, BF16)


def pad_read_nsa(w):
    D = w.shape[1]
    w = w.reshape(NSA_HEADS, HEAD_DIM, D)
    z = jnp.zeros_like(w)
    g = (np.arange(NSA_HEADS) // NSA_HPG)[:, None, None]
    lo = jnp.where(g == 0, w, z)
    hi = jnp.where(g == 1, w, z)
    return jnp.concatenate([lo, hi], axis=1).reshape(NSA_HEADS * LANES, D).astype(BF16)


def _ffn_kernel(*refs, gated):
    if gated:
        x_ref, nw_ref, sc_ref, sh_ref, g2_ref, gate_ref, w1_ref, w3_ref, w2_ref, o_ref, h_ref, acc_ref = refs
    else:
        x_ref, nw_ref, sc_ref, sh_ref, g2_ref, w1_ref, w3_ref, w2_ref, o_ref, h_ref, acc_ref = refs
    e = pl.program_id(1)
    f = pl.program_id(2)

    @pl.when((e == 0) & (f == 0))
    def _():
        h_ref[...] = _norm_mod(x_ref[...], nw_ref[...], sc_ref[...], sh_ref[...]).astype(BF16)
        acc_ref[...] = jnp.zeros_like(acc_ref)

    h = h_ref[...]
    u = _dot(h, w1_ref[...])
    v = _dot(h, w3_ref[...])
    a = (u * _sigmoid(u) * v).astype(BF16)
    y = _dot(a, w2_ref[...])
    if gated:
        gate = gate_ref[...]
        lane = lax.broadcasted_iota(jnp.int32, gate.shape, 1)
        y = y * jnp.sum(jnp.where(lane == e, gate, 0.0), axis=-1, keepdims=True)
    acc_ref[...] += y

    @pl.when((e == pl.num_programs(1) - 1) & (f == pl.num_programs(2) - 1))
    def _():
        o_ref[...] = x_ref[...] + g2_ref[...] * acc_ref[...]


def ffn(x, mod_l, nw, w1, w3, w2, gate, T, *, tm, tf):
    M, D = x.shape
    E, _, F = w1.shape
    per_b = T // tm
    gated = gate is not None
    modspec = lambda k: pl.BlockSpec((None, None, 1, D), lambda i, e, f: (i // per_b, k, 0, 0))
    in_specs = [pl.BlockSpec((tm, D), lambda i, e, f: (i, 0)),
                pl.BlockSpec((1, D), lambda i, e, f: (0, 0)),
                modspec(4), modspec(3), modspec(5)]
    args = [x, nw, mod_l, mod_l, mod_l]
    if gated:
        in_specs.append(pl.BlockSpec((tm, LANES), lambda i, e, f: (i, 0)))
        args.append(gate)
    in_specs += [pl.BlockSpec((None, D, tf), lambda i, e, f: (e, 0, f)),
                 pl.BlockSpec((None, D, tf), lambda i, e, f: (e, 0, f)),
                 pl.BlockSpec((None, tf, D), lambda i, e, f: (e, f, 0))]
    args += [w1, w3, w2]
    return pl.pallas_call(
        functools.partial(_ffn_kernel, gated=gated),
        grid=(M // tm, E, F // tf),
        in_specs=in_specs,
        out_specs=pl.BlockSpec((tm, D), lambda i, e, f: (i, 0)),
        out_shape=jax.ShapeDtypeStruct((M, D), F32),
        scratch_shapes=[pltpu.VMEM((tm, D), BF16), pltpu.VMEM((tm, D), F32)],
        compiler_params=_cparams("parallel", "arbitrary", "arbitrary"),
        name="ffn_gated" if gated else "ffn_dense",
    )(*args)


def _router_kernel(x_ref, nw_ref, sc_ref, sh_ref, wh_ref, wl_ref, o_ref):
    h = _norm_mod(x_ref[...], nw_ref[...], sc_ref[...], sh_ref[...])
    hh = h.astype(BF16)
    hl = (h - hh.astype(F32)).astype(BF16)
    logits = _dot(hh, wh_ref[...]) + (_dot(hl, wh_ref[...]) + _dot(hh, wl_ref[...]))
    lane = lax.broadcasted_iota(jnp.int32, logits.shape, 1)
    logits = jnp.where(lane < N_EXPERTS, logits, REMOVED)
    lane_f = lane.astype(F32)
    v1 = jnp.max(logits, axis=-1, keepdims=True)
    i1 = jnp.min(jnp.where(logits == v1, lane_f, float(LANES)), axis=-1, keepdims=True)
    rest = jnp.where(lane_f == i1, REMOVED, logits)
    v2 = jnp.max(rest, axis=-1, keepdims=True)
    i2 = jnp.min(jnp.where(rest == v2, lane_f, float(LANES)), axis=-1, keepdims=True)
    e2 = jnp.exp(v2 - v1)
    w1 = 1.0 / (1.0 + e2)
    w2 = e2 / (1.0 + e2)
    o_ref[...] = jnp.where(lane_f == i1, w1, jnp.where(lane_f == i2, w2, 0.0))


def router(x, mod_l, nw, w_router, T, *, tm=512):
    M, D = x.shape
    per_b = T // tm
    wp = jnp.zeros((D, LANES), F32).at[:, :N_EXPERTS].set(w_router)
    wh = wp.astype(BF16)
    wl = (wp - wh.astype(F32)).astype(BF16)
    return pl.pallas_call(
        _router_kernel,
        grid=(M // tm,),
        in_specs=[pl.BlockSpec((tm, D), lambda i: (i, 0)),
                  pl.BlockSpec((1, D), lambda i: (0, 0))]
        + _mod_specs(T, tm, 4, 3, 1)
        + [pl.BlockSpec((D, LANES), lambda i: (0, 0)),
           pl.BlockSpec((D, LANES), lambda i: (0, 0))],
        out_specs=pl.BlockSpec((tm, LANES), lambda i: (i, 0)),
        out_shape=jax.ShapeDtypeStruct((M, LANES), F32),
        compiler_params=_cparams("parallel"),
        name="moe_router",
    )(x, nw, mod_l, mod_l, wh, wl)


def _final_norm_kernel(x_ref, w_ref, o_ref):
    x = x_ref[...]
    ms = jnp.mean(x * x, axis=-1, keepdims=True)
    o_ref[...] = x * lax.rsqrt(ms + NORM_EPS) * w_ref[...]


def final_norm(x, w, *, tm=1024):
    M, D = x.shape
    return pl.pallas_call(
        _final_norm_kernel,
        grid=(M // tm,),
        in_specs=[pl.BlockSpec((tm, D), lambda i: (i, 0)), pl.BlockSpec((1, D), lambda i: (0, 0))],
        out_specs=pl.BlockSpec((tm, D), lambda i: (i, 0)),
        out_shape=jax.ShapeDtypeStruct((M, D), F32),
        compiler_params=_cparams("parallel"),
        name="final_norm",
    )(x, w)


def nsa_constants(T):
    n_sel = T // SEL_LEN
    nsp = max(LANES, n_sel)
    ncp = T // CMP_STRIDE
    cmp_start = np.arange(ncp) * CMP_STRIDE
    sel_start = np.arange(nsp) * SEL_LEN
    ov = ((cmp_start[:, None] < sel_start[None, :] + SEL_LEN)
          & (cmp_start[:, None] + CMP_LEN > sel_start[None, :]))
    ov[(T - CMP_LEN) // CMP_STRIDE + 1:] = False
    ov[:, n_sel:] = False
    e_mat = (np.arange(T)[None, :] // SEL_LEN) == np.arange(nsp)[:, None]
    return jnp.asarray(ov.T, BF16), jnp.asarray(e_mat, BF16)


def token_mixing(x, mod_l, lw, consts, B, T):
    M = B * T
    cos_t, sin_t, ov_t, e_mat, ret_consts, ex = consts
    p1 = proj_rope(x, mod_l, lw["norm_mix"], lw["w1"], cos_t, sin_t, p1_scales(), T).reshape(B, T, P1_COLS)
    p2 = proj_plain(x, mod_l, lw["norm_mix"], lw["w2"], T).reshape(B, T, P2_COLS)

    def group_rows(a):
        return a.reshape(B, T, NSA_GROUPS, HEAD_DIM).transpose(0, 2, 1, 3).reshape(
            B, NSA_GROUPS, T // CMP_STRIDE, CMP_STRIDE * HEAD_DIM)

    xr = jnp.stack([group_rows(p1[:, :, P1_NKC:P1_NKC + LANES]), group_rows(p2[:, :, P2_NVC:P2_NVC + LANES])])
    cmp_out = compress(xr, lw["cmp_pe"], lw["cmp_w1"], lw["cmp_w2"])
    cmp_out = cmp_out.transpose(0, 1, 3, 2, 4).reshape(2, B, T // CMP_STRIDE, LANES)
    o_cmp, sel = nsa_cmp_select(p1, cmp_out[0], cmp_out[1], ov_t, T)
    o_sel = nsa_selected(p1, p2, sel, e_mat, T)
    o_win = nsa_window(p1, p2, T)

    o_ret = retention(p1, p2, ret_consts, T)

    ff = p2[:, :, P2_SMALL + 3 * NSA_HEADS:P2_SMALL + 3 * NSA_HEADS + FOX_HEADS].astype(F32)
    ff = ff.transpose(0, 2, 1).reshape(B, FOX_HEADS, T // LANES, LANES)
    cum = fox_cum(ff, lw["fox_bias"]).reshape(B, FOX_HEADS // 2, 2, 1, T)
    o_fox = fox_attention(p2, cum, T)

    return readout(o_cmp.reshape(M, -1), o_sel.reshape(M, -1), o_win.reshape(M, -1), p2.reshape(M, P2_COLS),
                   o_ret.reshape(M, -1), o_fox.reshape(M, -1), x, mod_l, ex,
                   lw["wn"], lw["wr"], lw["wf"], lw["wo"], T)


def layer_weights(l, norm_mix, w_in, cmp_k_pe, cmp_k_w1, cmp_k_w2, cmp_v_pe, cmp_v_w1, cmp_v_w2, fox_f_bias,
                  w_read_nsa, w_read_ret, w_read_fox, w_out):
    w1, w2 = split_w_in(w_in[l])
    pe = jnp.stack([cmp_k_pe[l].reshape(1, -1), cmp_v_pe[l].reshape(1, -1)])
    pe = jnp.broadcast_to(pe, (2, 8, pe.shape[-1])).astype(BF16)
    return {
        "norm_mix": norm_mix[l].reshape(1, -1),
        "w1": w1, "w2": w2,
        "cmp_pe": pe,
        "cmp_w1": jnp.stack([cmp_k_w1[l], cmp_v_w1[l]]).astype(BF16),
        "cmp_w2": jnp.stack([cmp_k_w2[l], cmp_v_w2[l]]).astype(BF16),
        "fox_bias": jnp.broadcast_to(fox_f_bias[l][:, None, None], (FOX_HEADS, 1, LANES)),
        "wn": pad_read_nsa(w_read_nsa[l]),
        "wr": w_read_ret[l].astype(BF16),
        "wf": w_read_fox[l].astype(BF16),
        "wo": w_out[l].astype(BF16),
    }


def kernel(x, c, ada_w, ada_b, norm_mix, norm_ffn, w_in, cmp_k_pe, cmp_k_w1, cmp_k_w2, cmp_v_pe, cmp_v_w1,
           cmp_v_w2, fox_f_bias, w_read_nsa, w_read_ret, w_read_fox, w_out, ffn_w1, ffn_w3, ffn_w2, router_w,
           moe_w1, moe_w3, moe_w2, final_norm_w):
    B, T, D = x.shape
    M = B * T
    depth = ada_w.shape[0]
    mod = modulation(c, ada_w, ada_b)
    cos_t, sin_t = rope_tables(T)
    ov_t, e_mat = nsa_constants(T)
    consts = (cos_t, sin_t, ov_t, e_mat, retention_consts(), nsa_gate_expand())
    xs = x.reshape(M, D)
    for l in range(depth):
        lw = layer_weights(l, norm_mix, w_in, cmp_k_pe, cmp_k_w1, cmp_k_w2, cmp_v_pe, cmp_v_w1, cmp_v_w2,
                           fox_f_bias, w_read_nsa, w_read_ret, w_read_fox, w_out)
        xs = token_mixing(xs, mod[l], lw, consts, B, T)
        nf = norm_ffn[l].reshape(1, D)
        if l % 2 == 0:
            k = l // 2
            xs = ffn(xs, mod[l], nf, ffn_w1[k][None].astype(BF16), ffn_w3[k][None].astype(BF16),
                     ffn_w2[k][None].astype(BF16), None, T, tm=512, tf=D_FF // 2)
        else:
            k = l // 2
            gate = router(xs, mod[l], nf, router_w[k], T)
            xs = ffn(xs, mod[l], nf, moe_w1[k].astype(BF16), moe_w3[k].astype(BF16), moe_w2[k].astype(BF16),
                     gate, T, tm=1024, tf=D_FF_EXPERT // 4)
    return final_norm(xs, final_norm_w.reshape(1, D)).reshape(B, T, D)
```

```python
import functools
import math

import jax
import jax.numpy as jnp
import numpy as np
from jax import lax
from jax.experimental import pallas as pl
from jax.experimental.pallas import tpu as pltpu

F32 = jnp.float32
BF16 = jnp.bfloat16

D_MODEL = 1024
DEPTH = 2
HEAD_DIM = 64
ROPE_THETA = 10000.0
NORM_EPS = 1e-6
NEG_INF = -1e30
REMOVED = -3e38

NSA_HEADS = 8
NSA_GROUPS = 2
NSA_HPG = NSA_HEADS // NSA_GROUPS
CMP_LEN = 32
CMP_STRIDE = 16
CMP_HIDDEN = 256
SEL_LEN = 64
SEL_TOPN = 16
WINDOW = 512
FORCE_SCORE = 1e4
NSA_QBLOCK = 128

RET_HEADS = 4
RET_QK_DIM = 64
RET_V_DIM = 128
RET_CHUNK = 128

FOX_HEADS = 8

D_FF = 2816
N_EXPERTS = 8
D_FF_EXPERT = 3584

LANES = 128
VMEM_LIMIT = 56 * 1024 * 1024

P1_NQ = 0
P1_RQ = 1024
P1_RK = 1536
P1_NKC = 1792
P1_NKS = 1920
P1_NKW = 2048
P1_COLS = 2176
P2_MG = 0
P2_RV = 3072
P2_RG = 3584
P2_FQ = 4096
P2_FK = 5120
P2_FV = 5632
P2_NVC = 6144
P2_NVS = 6272
P2_NVW = 6400
P2_SMALL = 6528
P2_COLS = 6656


def _cparams(*sem):
    return pltpu.CompilerParams(dimension_semantics=tuple(sem), vmem_limit_bytes=VMEM_LIMIT)


def _sigmoid(x):
    return 1.0 / (1.0 + jnp.exp(-x))


def _dot(a, b):
    return jnp.dot(a, b, preferred_element_type=F32)


def _dot_nt(a, b):
    return lax.dot_general(a, b, (((1,), (1,)), ((), ())), preferred_element_type=F32)


def _dot_tn(a, b):
    return lax.dot_general(a, b, (((0,), (0,)), ((), ())), preferred_element_type=F32)


def _split3(x):
    hi = x.astype(BF16)
    r1 = x - hi.astype(F32)
    mid = r1.astype(BF16)
    lo = (r1 - mid.astype(F32)).astype(BF16)
    return hi, mid, lo


def _norm_mod(x, nw, sc, sh):
    ms = jnp.mean(x * x, axis=-1, keepdims=True)
    y = x * lax.rsqrt(ms + NORM_EPS) * nw
    return y * (1.0 + sc) + sh


def _mod_kernel(c_ref, w_ref, b_ref, o_ref):
    c = c_ref[...]
    s = c * _sigmoid(c)
    o_ref[0] = _dot(s.astype(BF16), w_ref[0].astype(BF16)) + b_ref[0]


def modulation(c, ada_w, ada_b):
    B, D = c.shape
    depth = ada_w.shape[0]
    rows = 8
    c_pad = jnp.zeros((rows, D), F32).at[:B].set(c)
    out = pl.pallas_call(
        _mod_kernel,
        grid=(depth, 6),
        in_specs=[pl.BlockSpec((rows, D), lambda l, j: (0, 0)),
                  pl.BlockSpec((1, D, D), lambda l, j: (l, 0, j)),
                  pl.BlockSpec((1, 1, D), lambda l, j: (l, 0, j))],
        out_specs=pl.BlockSpec((1, rows, D), lambda l, j: (l, 0, j)),
        out_shape=jax.ShapeDtypeStruct((depth, rows, 6 * D), F32),
        compiler_params=_cparams("parallel", "parallel"),
        name="modulation",
    )(c_pad, ada_w, ada_b.reshape(depth, 1, 6 * D))
    return out[:, :B].reshape(depth, B, 6, 1, D)


def _proj_plain_kernel(x_ref, nw_ref, sc_ref, sh_ref, w_ref, o_ref, h_ref):
    @pl.when(pl.program_id(1) == 0)
    def _():
        h_ref[...] = _norm_mod(x_ref[...], nw_ref[...], sc_ref[...], sh_ref[...]).astype(BF16)

    o_ref[...] = _dot(h_ref[...], w_ref[...]).astype(o_ref.dtype)


def _proj_rope_kernel(x_ref, nw_ref, sc_ref, sh_ref, w_ref, cos_ref, sin_ref, o_ref, *, scales):
    h = _norm_mod(x_ref[...], nw_ref[...], sc_ref[...], sh_ref[...]).astype(BF16)
    y = _dot(h, w_ref[...])
    cos = cos_ref[...]
    sin = sin_ref[...]
    lane = lax.broadcasted_iota(jnp.int32, cos.shape, 1)
    first_half = (lane % HEAD_DIM) < (HEAD_DIM // 2)
    for g, scale in enumerate(scales):
        yg = y[:, g * LANES:(g + 1) * LANES]
        rot = jnp.where(first_half, pltpu.roll(yg, LANES - HEAD_DIM // 2, 1),
                        pltpu.roll(yg, HEAD_DIM // 2, 1))
        r = yg * cos + rot * sin
        if scale != 1.0:
            r = r * scale
        o_ref[:, g * LANES:(g + 1) * LANES] = r.astype(o_ref.dtype)


def _mod_specs(T, tm, sc_idx, sh_idx, nargs):
    per_b = T // tm
    if nargs == 1:
        return [pl.BlockSpec((None, None, 1, D_MODEL), lambda i: (i // per_b, sc_idx, 0, 0)),
                pl.BlockSpec((None, None, 1, D_MODEL), lambda i: (i // per_b, sh_idx, 0, 0))]
    return [pl.BlockSpec((None, None, 1, D_MODEL), lambda i, j: (i // per_b, sc_idx, 0, 0)),
            pl.BlockSpec((None, None, 1, D_MODEL), lambda i, j: (i // per_b, sh_idx, 0, 0))]


def proj_plain(x, mod_l, nw, w, T, *, tm=1024, tn=512):
    M, D = x.shape
    N = w.shape[1]
    return pl.pallas_call(
        _proj_plain_kernel,
        grid=(M // tm, N // tn),
        in_specs=[pl.BlockSpec((tm, D), lambda i, j: (i, 0)),
                  pl.BlockSpec((1, D), lambda i, j: (0, 0))]
        + _mod_specs(T, tm, 1, 0, 2)
        + [pl.BlockSpec((D, tn), lambda i, j: (0, j))],
        out_specs=pl.BlockSpec((tm, tn), lambda i, j: (i, j)),
        out_shape=jax.ShapeDtypeStruct((M, N), BF16),
        scratch_shapes=[pltpu.VMEM((tm, D), BF16)],
        compiler_params=_cparams("parallel", "arbitrary"),
        name="proj_plain",
    )(x, nw, mod_l, mod_l, w)


def proj_rope(x, mod_l, nw, w, cos, sin, scales, T, *, tm=512):
    M, D = x.shape
    N = w.shape[1]
    per_b = T // tm
    return pl.pallas_call(
        functools.partial(_proj_rope_kernel, scales=scales),
        grid=(M // tm,),
        in_specs=[pl.BlockSpec((tm, D), lambda i: (i, 0)),
                  pl.BlockSpec((1, D), lambda i: (0, 0))]
        + _mod_specs(T, tm, 1, 0, 1)
        + [pl.BlockSpec((D, N), lambda i: (0, 0)),
           pl.BlockSpec((tm, LANES), lambda i: (i % per_b, 0)),
           pl.BlockSpec((tm, LANES), lambda i: (i % per_b, 0))],
        out_specs=pl.BlockSpec((tm, N), lambda i: (i, 0)),
        out_shape=jax.ShapeDtypeStruct((M, N), BF16),
        compiler_params=_cparams("parallel"),
        name="proj_rope",
    )(x, nw, mod_l, mod_l, w, cos, sin)


def rope_tables(T):
    d = HEAD_DIM
    pos = jnp.arange(T, dtype=F32)
    inv = ROPE_THETA ** (-jnp.arange(0, d, 2, dtype=F32) / d)
    ang = pos[:, None] * inv[None, :]
    cos = jnp.cos(ang)
    sin = jnp.sin(ang)
    cos_t = jnp.concatenate([cos, cos, cos, cos], axis=-1)
    sin_t = jnp.concatenate([-sin, sin, -sin, sin], axis=-1)
    return cos_t, sin_t


def _pad_heads(w, n_heads, half_of_head):
    D = w.shape[0]
    w = w.reshape(D, n_heads, HEAD_DIM)
    z = jnp.zeros_like(w)
    halves = np.array([half_of_head(h) for h in range(n_heads)])
    lo = jnp.where(halves[None, :, None] == 0, w, z)
    hi = jnp.where(halves[None, :, None] == 1, w, z)
    return jnp.concatenate([lo, hi], axis=-1).reshape(D, n_heads * LANES)


def split_w_in(w_in):
    sizes = [512, 128, 128, 128, 128, 128, 128, 24, 256, 256, 512, 512, 512, 512, 512, 8, 3072]
    offs = np.cumsum([0] + sizes)
    (nq, nkc, nvc, nks, nvs, nkw, nvw, ngate, rq, rk, rv, rg, fq, fk, fv, ff, mg) = [
        w_in[:, offs[i]:offs[i + 1]] for i in range(len(sizes))]
    D = w_in.shape[0]
    nq_p = _pad_heads(nq, NSA_HEADS, lambda h: h // NSA_HPG)
    rq_p = _pad_heads(rq, RET_HEADS, lambda h: h % 2)
    fq_p = _pad_heads(fq, FOX_HEADS, lambda h: h % 2) * (HEAD_DIM ** -0.5)
    small = jnp.concatenate([ngate, ff, jnp.zeros((D, LANES - 32), w_in.dtype)], axis=-1)
    w1 = jnp.concatenate([nq_p, rq_p, rk, nkc, nks, nkw], axis=-1).astype(BF16)
    w2 = jnp.concatenate([mg, rv, rg, fq_p, fk, fv, nvc, nvs, nvw, small], axis=-1).astype(BF16)
    assert w1.shape[1] == P1_COLS and w2.shape[1] == P2_COLS
    return w1, w2


def p1_scales():
    s = [1.0] * (P1_COLS // LANES)
    for g in range(P1_NQ // LANES, P1_RQ // LANES):
        s[g] = HEAD_DIM ** -0.5
    for g in range(P1_RK // LANES, P1_NKC // LANES):
        s[g] = RET_QK_DIM ** -0.5
    return tuple(s)


def _compress_kernel(x_ref, pe_ref, w1_ref, w2_ref, o_ref):
    r = x_ref[...]
    half = r.shape[1]
    w1 = w1_ref[...]
    a = _dot(r, w1[:half])
    b = _dot(r, w1[half:])
    pe = _dot(pe_ref[...], w1)[0:1]
    n = a.shape[0]
    hid = a + pltpu.roll(b, n - 1, 0) + pe
    hid = hid * _sigmoid(hid)
    o_ref[...] = _dot(hid.astype(BF16), w2_ref[...]).astype(o_ref.dtype)


def compress(xr, pe, w1, w2):
    _, B, G, R, W = xr.shape
    H = w1.shape[-1]
    return pl.pallas_call(
        _compress_kernel,
        grid=(2, B, G),
        in_specs=[pl.BlockSpec((None, None, None, R, W), lambda s, b, g: (s, b, g, 0, 0)),
                  pl.BlockSpec((None, 8, 2 * W), lambda s, b, g: (s, 0, 0)),
                  pl.BlockSpec((None, 2 * W, H), lambda s, b, g: (s, 0, 0)),
                  pl.BlockSpec((None, H, HEAD_DIM), lambda s, b, g: (s, 0, 0))],
        out_specs=pl.BlockSpec((None, None, None, R, HEAD_DIM), lambda s, b, g: (s, b, g, 0, 0)),
        out_shape=jax.ShapeDtypeStruct((2, B, G, R, HEAD_DIM), BF16),
        compiler_params=_cparams("parallel", "parallel", "parallel"),
        name="nsa_compress",
    )(xr, pe, w1, w2)


def _stack_heads(q_ref, g):
    return jnp.concatenate(
        [q_ref[:, (NSA_HPG * g + hh) * LANES:(NSA_HPG * g + hh + 1) * LANES] for hh in range(NSA_HPG)],
        axis=0)


def _store_heads(o_ref, g, o, tq):
    for hh in range(NSA_HPG):
        h = NSA_HPG * g + hh
        o_ref[:, h * LANES:(h + 1) * LANES] = o[hh * tq:(hh + 1) * tq].astype(o_ref.dtype)


def _nsa_cmp_kernel(q_ref, kc_ref, vc_ref, ov_ref, o_ref, m_ref, *, tq, n_sel, top_n):
    t0 = pl.program_id(1) * tq
    kc = kc_ref[...]
    vc = vc_ref[...]
    ncp = kc.shape[0]
    nsp = ov_ref.shape[0]
    rows = NSA_HPG * tq
    n_idx = lax.broadcasted_iota(jnp.int32, (rows, ncp), 1)
    t_idx = t0 + lax.broadcasted_iota(jnp.int32, (rows, ncp), 0) % tq
    valid = (n_idx * CMP_STRIDE + (CMP_LEN - 1)) <= t_idx
    j_idx = lax.broadcasted_iota(jnp.int32, (nsp, tq), 0)
    cur = (t0 + lax.broadcasted_iota(jnp.int32, (nsp, tq), 1)) // SEL_LEN
    forced = (j_idx == 0) | (j_idx == cur) | (j_idx == cur - 1)
    j_f = j_idx.astype(F32)
    for g in range(NSA_GROUPS):
        q = _stack_heads(q_ref, g)
        s = jnp.where(valid, _dot_nt(q, kc), NEG_INF)
        m = jnp.max(s, axis=-1, keepdims=True)
        e = jnp.where(valid, jnp.exp(s - m), 0.0)
        l = jnp.sum(e, axis=-1, keepdims=True)
        p = e / jnp.where(l > 0.0, l, 1.0)
        _store_heads(o_ref, g, _dot(p.astype(BF16), vc), tq)
        psum = p[0:tq]
        for hh in range(1, NSA_HPG):
            psum = psum + p[hh * tq:(hh + 1) * tq]
        imp_t = _dot_nt(ov_ref[...], psum.astype(BF16))
        score = jnp.where(forced, FORCE_SCORE, imp_t)
        score = jnp.where(j_idx <= cur, score, NEG_INF)
        score = jnp.where(j_idx < n_sel, score, REMOVED)
        sel = jnp.zeros((nsp, tq), F32)
        for _ in range(top_n):
            mx = jnp.max(score, axis=0, keepdims=True)
            idx = jnp.min(jnp.where(score == mx, j_f, float(nsp)), axis=0, keepdims=True)
            hit = j_f == idx
            sel = jnp.where(hit, 1.0, sel)
            score = jnp.where(hit, REMOVED, score)
        sel = jnp.where(j_idx <= cur, sel, 0.0)
        m_ref[g] = sel.T.astype(m_ref.dtype)


def nsa_cmp_select(p1, kc, vc, ov_t, T):
    B = p1.shape[0]
    tq = NSA_QBLOCK
    ncp = kc.shape[1]
    nsp = ov_t.shape[0]
    n_sel = T // SEL_LEN
    return pl.pallas_call(
        functools.partial(_nsa_cmp_kernel, tq=tq, n_sel=n_sel, top_n=min(SEL_TOPN, n_sel)),
        grid=(B, T // tq),
        in_specs=[pl.BlockSpec((None, tq, NSA_HEADS * LANES), lambda b, i: (b, i, 0)),
                  pl.BlockSpec((None, ncp, LANES), lambda b, i: (b, 0, 0)),
                  pl.BlockSpec((None, ncp, LANES), lambda b, i: (b, 0, 0)),
                  pl.BlockSpec((nsp, ncp), lambda b, i: (0, 0))],
        out_specs=[pl.BlockSpec((None, tq, NSA_HEADS * LANES), lambda b, i: (b, i, 0)),
                   pl.BlockSpec((None, NSA_GROUPS, tq, nsp), lambda b, i: (b, 0, i, 0))],
        out_shape=[jax.ShapeDtypeStruct((B, T, NSA_HEADS * LANES), BF16),
                   jax.ShapeDtypeStruct((B, NSA_GROUPS, T, nsp), BF16)],
        compiler_params=_cparams("parallel", "parallel"),
        name="nsa_cmp_select",
    )(p1, kc, vc, ov_t)


def _nsa_sel_kernel(q_ref, k_ref, v_ref, m_ref, e_ref, o_ref, *, tq, tk):
    t0 = pl.program_id(1) * tq
    n_tiles = (t0 + tq + tk - 1) // tk
    rows = NSA_HPG * tq
    trow = t0 + lax.broadcasted_iota(jnp.int32, (tq, tk), 0)
    kcol = lax.broadcasted_iota(jnp.int32, (tq, tk), 1)
    for g in range(NSA_GROUPS):
        q = _stack_heads(q_ref, g)
        msel = m_ref[g]

        def body(j, carry, q=q, msel=msel):
            m, l, acc = carry
            start = pl.multiple_of(j * tk, tk)
            ks = k_ref[pl.ds(start, tk), :]
            vs = v_ref[pl.ds(start, tk), :]
            allow = _dot(msel, e_ref[:, pl.ds(start, tk)])
            ok = (allow > 0.5) & ((kcol + start) <= trow)
            bias = jnp.where(ok, 0.0, NEG_INF)
            s = _dot_nt(q, ks) + jnp.concatenate([bias] * NSA_HPG, axis=0)
            m_new = jnp.maximum(m, jnp.max(s, axis=-1, keepdims=True))
            alpha = jnp.exp(m - m_new)
            p = jnp.exp(s - m_new)
            l = alpha * l + jnp.sum(p, axis=-1, keepdims=True)
            acc = alpha * acc + _dot(p.astype(BF16), vs)
            return m_new, l, acc

        init = (jnp.full((rows, 1), NEG_INF, F32), jnp.zeros((rows, 1), F32),
                jnp.zeros((rows, LANES), F32))
        m, l, acc = lax.fori_loop(0, n_tiles, body, init)
        _store_heads(o_ref, g, acc / l, tq)


def nsa_selected(p1, p2, sel, e_mat, T, *, tk=512):
    B = p1.shape[0]
    tq = NSA_QBLOCK
    nsp = sel.shape[-1]
    return pl.pallas_call(
        functools.partial(_nsa_sel_kernel, tq=tq, tk=tk),
        grid=(B, T // tq),
        in_specs=[pl.BlockSpec((None, tq, NSA_HEADS * LANES), lambda b, i: (b, i, 0)),
                  pl.BlockSpec((None, T, LANES), lambda b, i: (b, 0, P1_NKS // LANES)),
                  pl.BlockSpec((None, T, LANES), lambda b, i: (b, 0, P2_NVS // LANES)),
                  pl.BlockSpec((None, NSA_GROUPS, tq, nsp), lambda b, i: (b, 0, i, 0)),
                  pl.BlockSpec((nsp, T), lambda b, i: (0, 0))],
        out_specs=pl.BlockSpec((None, tq, NSA_HEADS * LANES), lambda b, i: (b, i, 0)),
        out_shape=jax.ShapeDtypeStruct((B, T, NSA_HEADS * LANES), BF16),
        compiler_params=_cparams("parallel", "parallel"),
        name="nsa_selected",
    )(p1, p1, p2, sel, e_mat)


def _nsa_win_kernel(q_ref, k_ref, v_ref, o_ref, *, tq):
    t0 = pl.program_id(1) * tq
    span = WINDOW + tq
    start = pl.multiple_of(jnp.maximum(t0 - WINDOW, 0), tq)
    ks = k_ref[pl.ds(start, span), :]
    vs = v_ref[pl.ds(start, span), :]
    rows = NSA_HPG * tq
    t_idx = t0 + lax.broadcasted_iota(jnp.int32, (rows, span), 0) % tq
    kpos = start + lax.broadcasted_iota(jnp.int32, (rows, span), 1)
    ok = (kpos <= t_idx) & ((t_idx - kpos) < WINDOW)
    for g in range(NSA_GROUPS):
        q = _stack_heads(q_ref, g)
        s = jnp.where(ok, _dot_nt(q, ks), NEG_INF)
        m = jnp.max(s, axis=-1, keepdims=True)
        p = jnp.exp(s - m)
        l = jnp.sum(p, axis=-1, keepdims=True)
        _store_heads(o_ref, g, _dot(p.astype(BF16), vs) / l, tq)


def nsa_window(p1, p2, T):
    B = p1.shape[0]
    tq = NSA_QBLOCK
    return pl.pallas_call(
        functools.partial(_nsa_win_kernel, tq=tq),
        grid=(B, T // tq),
        in_specs=[pl.BlockSpec((None, tq, NSA_HEADS * LANES), lambda b, i: (b, i, 0)),
                  pl.BlockSpec((None, T, LANES), lambda b, i: (b, 0, P1_NKW // LANES)),
                  pl.BlockSpec((None, T, LANES), lambda b, i: (b, 0, P2_NVW // LANES))],
        out_specs=pl.BlockSpec((None, tq, NSA_HEADS * LANES), lambda b, i: (b, i, 0)),
        out_shape=jax.ShapeDtypeStruct((B, T, NSA_HEADS * LANES), BF16),
        compiler_params=_cparams("parallel", "parallel"),
        name="nsa_window",
    )(p1, p1, p2)


def _retention_kernel(q_ref, k_ref, v_ref, g_ref, din_ref, qd_ref, kd_ref, cd_ref, o_ref, st_ref):
    @pl.when(pl.program_id(0) == 0)
    def _():
        st_ref[...] = jnp.zeros_like(st_ref)

    B = q_ref.shape[0]
    for b in range(B):
        for h in range(RET_HEADS):
            lanes = slice(h * LANES, (h + 1) * LANES)
            qh = q_ref[b, :, lanes]
            kp = k_ref[b, :, (h // 2) * LANES:(h // 2 + 1) * LANES]
            vh = v_ref[b, :, lanes]
            st = st_ref[b, h]
            inner = _dot_nt(qh, kp) * din_ref[h]
            o = _dot(inner.astype(BF16), vh) + _dot(qh, st.astype(BF16)) * qd_ref[h]
            kd = (kp.astype(F32) * kd_ref[h]).astype(BF16)
            st_ref[b, h] = st * cd_ref[h, 0:1, :] + _dot_tn(kd, vh)
            mu = jnp.mean(o, axis=-1, keepdims=True)
            d = o - mu
            var = jnp.mean(d * d, axis=-1, keepdims=True)
            on = d * lax.rsqrt(var + NORM_EPS)
            gh = g_ref[b, :, lanes].astype(F32)
            o_ref[b, :, lanes] = (gh * _sigmoid(gh) * on).astype(o_ref.dtype)


def retention_consts():
    C = RET_CHUNK
    H = RET_HEADS
    log_g = jnp.log(1.0 - 2.0 ** (-5.0 - jnp.arange(H, dtype=F32)))
    n = jnp.arange(C, dtype=F32)
    diff = n[:, None] - n[None, :]
    causal = diff >= 0
    decay_in = jnp.where(causal[None], jnp.exp(jnp.where(causal, diff, 0.0)[None] * log_g[:, None, None]), 0.0)
    q_decay = jnp.exp((n[None, :] + 1.0) * log_g[:, None])
    k_decay = jnp.exp((C - 1.0 - n)[None, :] * log_g[:, None])
    chunk_decay = jnp.exp(C * log_g)
    qd = jnp.broadcast_to(q_decay[:, :, None], (H, C, LANES))
    kd = jnp.broadcast_to(k_decay[:, :, None], (H, C, LANES))
    cd = jnp.broadcast_to(chunk_decay[:, None, None], (H, 8, LANES))
    return decay_in, qd, kd, cd


def retention(p1, p2, consts, T):
    B = p1.shape[0]
    C = RET_CHUNK
    din, qd, kd, cd = consts
    W = RET_HEADS * LANES
    full = lambda shape: pl.BlockSpec(shape, lambda c: (0,) * len(shape))
    return pl.pallas_call(
        _retention_kernel,
        grid=(T // C,),
        in_specs=[pl.BlockSpec((B, C, W), lambda c: (0, c, P1_RQ // W)),
                  pl.BlockSpec((B, C, W // 2), lambda c: (0, c, P1_RK // (W // 2))),
                  pl.BlockSpec((B, C, W), lambda c: (0, c, P2_RV // W)),
                  pl.BlockSpec((B, C, W), lambda c: (0, c, P2_RG // W)),
                  full(din.shape), full(qd.shape), full(kd.shape), full(cd.shape)],
        out_specs=pl.BlockSpec((B, C, W), lambda c: (0, c, 0)),
        out_shape=jax.ShapeDtypeStruct((B, T, W), BF16),
        scratch_shapes=[pltpu.VMEM((B, RET_HEADS, LANES, LANES), F32)],
        compiler_params=_cparams("arbitrary"),
        name="retention",
    )(p1, p1, p2, p2, din, qd, kd, cd)


def _fox_cum_kernel(f_ref, b_ref, o_ref):
    x = f_ref[...] + b_ref[...]
    ls = jnp.minimum(x, 0.0) - jnp.log1p(jnp.exp(-jnp.abs(x)))
    R = x.shape[0]
    ki = lax.broadcasted_iota(jnp.int32, (LANES, LANES), 0)
    ji = lax.broadcasted_iota(jnp.int32, (LANES, LANES), 1)
    upper = jnp.where(ki <= ji, 1.0, 0.0).astype(BF16)
    hi, mid, lo = _split3(ls)
    rowcum = _dot(hi, upper) + _dot(mid, upper) + _dot(lo, upper)
    tot = jnp.broadcast_to(rowcum[:, LANES - 1:LANES], (R, LANES))
    ri = lax.broadcasted_iota(jnp.int32, (R, R), 0)
    ci = lax.broadcasted_iota(jnp.int32, (R, R), 1)
    lower = jnp.where(ci < ri, 1.0, 0.0).astype(BF16)
    hi, mid, lo = _split3(tot)
    offs = _dot(lower, hi) + _dot(lower, mid) + _dot(lower, lo)
    o_ref[...] = rowcum + offs


def fox_cum(f_logit, bias):
    B, H, R, _ = f_logit.shape
    return pl.pallas_call(
        _fox_cum_kernel,
        grid=(B, H),
        in_specs=[pl.BlockSpec((None, None, R, LANES), lambda b, h: (b, h, 0, 0)),
                  pl.BlockSpec((None, 1, LANES), lambda b, h: (h, 0, 0))],
        out_specs=pl.BlockSpec((None, None, R, LANES), lambda b, h: (b, h, 0, 0)),
        out_shape=jax.ShapeDtypeStruct((B, H, R, LANES), F32),
        compiler_params=_cparams("parallel", "parallel"),
        name="fox_cum",
    )(f_logit, bias)


def _fox_kernel(q_ref, k_ref, v_ref, c_ref, o_ref, *, tq):
    i = pl.program_id(2)
    tk = tq
    row = lax.broadcasted_iota(jnp.int32, (tq, tk), 0)
    col = lax.broadcasted_iota(jnp.int32, (tq, tk), 1)
    lane = lax.broadcasted_iota(jnp.int32, (tq, LANES), 1)
    outs = []
    for hh in range(2):
        q = q_ref[:, hh * LANES:(hh + 1) * LANES]

        def step(j, carry, masked, q=q, hh=hh):
            m, l, acc = carry
            start = pl.multiple_of(j * tk, tk)
            ks = k_ref[pl.ds(start, tk), :]
            vs = v_ref[pl.ds(start, tk), :]
            s = _dot_nt(q, ks) - c_ref[hh, :, pl.ds(start, tk)]
            if masked:
                s = jnp.where(col <= row, s, NEG_INF)
            m_new = jnp.maximum(m, jnp.max(s, axis=-1, keepdims=True))
            alpha = jnp.exp(m - m_new)
            p = jnp.exp(s - m_new)
            l = alpha * l + jnp.sum(p, axis=-1, keepdims=True)
            acc = alpha * acc + _dot(p.astype(BF16), vs)
            return m_new, l, acc

        init = (jnp.full((tq, 1), NEG_INF, F32), jnp.zeros((tq, 1), F32), jnp.zeros((tq, LANES), F32))
        carry = lax.fori_loop(0, i, functools.partial(step, masked=False), init)
        m, l, acc = step(i, carry, True)
        outs.append(acc / l)
    o_ref[...] = jnp.where(lane < HEAD_DIM, outs[0], outs[1]).astype(o_ref.dtype)


def fox_attention(p2, cum, T, *, tq=512):
    B = p2.shape[0]
    HP = FOX_HEADS // 2
    return pl.pallas_call(
        functools.partial(_fox_kernel, tq=tq),
        grid=(B, HP, T // tq),
        in_specs=[pl.BlockSpec((None, tq, 2 * LANES), lambda b, h, i: (b, i, P2_FQ // (2 * LANES) + h)),
                  pl.BlockSpec((None, T, LANES), lambda b, h, i: (b, 0, P2_FK // LANES + h)),
                  pl.BlockSpec((None, T, LANES), lambda b, h, i: (b, 0, P2_FV // LANES + h)),
                  pl.BlockSpec((None, None, 2, 1, T), lambda b, h, i: (b, h, 0, 0, 0))],
        out_specs=pl.BlockSpec((None, tq, LANES), lambda b, h, i: (b, i, h)),
        out_shape=jax.ShapeDtypeStruct((B, T, FOX_HEADS * HEAD_DIM), BF16),
        compiler_params=_cparams("parallel", "parallel", "parallel"),
        name="fox_attention",
    )(p2, p2, p2, cum)


def _readout_kernel(ocmp_ref, osel_ref, owin_ref, small_ref, oret_ref, ofox_ref, mg_ref, x_ref, g1_ref,
                    ex_ref, wn_ref, wr_ref, wf_ref, wo_ref, o_ref):
    W = NSA_HEADS * LANES
    gs = _sigmoid(small_ref[...].astype(F32)).astype(BF16)
    ge = _dot(gs, ex_ref[...])
    onsa = (ge[:, :W] * ocmp_ref[...].astype(F32) + ge[:, W:2 * W] * osel_ref[...].astype(F32)
            + ge[:, 2 * W:] * owin_ref[...].astype(F32))
    D = D_MODEL
    merged = (_sigmoid(mg_ref[:, :D].astype(F32)) * _dot(onsa.astype(BF16), wn_ref[...])
              + _sigmoid(mg_ref[:, D:2 * D].astype(F32)) * _dot(oret_ref[...], wr_ref[...])
              + _sigmoid(mg_ref[:, 2 * D:].astype(F32)) * _dot(ofox_ref[...], wf_ref[...]))
    y = _dot(merged.astype(BF16), wo_ref[...])
    o_ref[...] = x_ref[...] + g1_ref[...] * y


def readout(o_cmp, o_sel, o_win, p2, o_ret, o_fox, x, mod_l, ex, wn, wr, wf, wo, T, *, tm=512):
    M, D = x.shape
    per_b = T // tm
    W = NSA_HEADS * LANES
    row = lambda width, col=0: pl.BlockSpec((tm, width), lambda i: (i, col))
    full = lambda a: pl.BlockSpec(a.shape, lambda i: (0,) * a.ndim)
    return pl.pallas_call(
        _readout_kernel,
        grid=(M // tm,),
        in_specs=[row(W), row(W), row(W), row(LANES, P2_SMALL // LANES), row(512), row(512),
                  row(3 * D, 0), row(D),
                  pl.BlockSpec((None, None, 1, D), lambda i: (i // per_b, 2, 0, 0)),
                  full(ex), full(wn), full(wr), full(wf), full(wo)],
        out_specs=row(D),
        out_shape=jax.ShapeDtypeStruct((M, D), F32),
        compiler_params=_cparams("parallel"),
        name="mixer_readout",
    )(o_cmp, o_sel, o_win, p2, o_ret, o_fox, p2, x, mod_l, ex, wn, wr, wf, wo)


def nsa_gate_expand():
    ex = np.zeros((LANES, 3 * NSA_HEADS * LANES), np.float32)
    for br in range(3):
        for h in range(NSA_HEADS):
            c0 = br * NSA_HEADS * LANES + h * LANES
            ex[br * NSA_HEADS + h, c0:c0 + LANES] = 1.0
    return jnp.asarray(ex, BF16)


def pad_read_nsa(w):
    D = w.shape[1]
    w = w.reshape(NSA_HEADS, HEAD_DIM, D)
    z = jnp.zeros_like(w)
    g = (np.arange(NSA_HEADS) // NSA_HPG)[:, None, None]
    lo = jnp.where(g == 0, w, z)
    hi = jnp.where(g == 1, w, z)
    return jnp.concatenate([lo, hi], axis=1).reshape(NSA_HEADS * LANES, D).astype(BF16)


def _ffn_kernel(*refs, gated):
    if gated:
        x_ref, nw_ref, sc_ref, sh_ref, g2_ref, gate_ref, w1_ref, w3_ref, w2_ref, o_ref, h_ref, acc_ref = refs
    else:
        x_ref, nw_ref, sc_ref, sh_ref, g2_ref, w1_ref, w3_ref, w2_ref, o_ref, h_ref, acc_ref = refs
    e = pl.program_id(1)
    f = pl.program_id(2)

    @pl.when((e == 0) & (f == 0))
    def _():
        h_ref[...] = _norm_mod(x_ref[...], nw_ref[...], sc_ref[...], sh_ref[...]).astype(BF16)
        acc_ref[...] = jnp.zeros_like(acc_ref)

    h = h_ref[...]
    u = _dot(h, w1_ref[...])
    v = _dot(h, w3_ref[...])
    a = (u * _sigmoid(u) * v).astype(BF16)
    y = _dot(a, w2_ref[...])
    if gated:
        gate = gate_ref[...]
        lane = lax.broadcasted_iota(jnp.int32, gate.shape, 1)
        y = y * jnp.sum(jnp.where(lane == e, gate, 0.0), axis=-1, keepdims=True)
    acc_ref[...] += y

    @pl.when((e == pl.num_programs(1) - 1) & (f == pl.num_programs(2) - 1))
    def _():
        o_ref[...] = x_ref[...] + g2_ref[...] * acc_ref[...]


def ffn(x, mod_l, nw, w1, w3, w2, gate, T, *, tm, tf):
    M, D = x.shape
    E, _, F = w1.shape
    per_b = T // tm
    gated = gate is not None
    modspec = lambda k: pl.BlockSpec((None, None, 1, D), lambda i, e, f: (i // per_b, k, 0, 0))
    in_specs = [pl.BlockSpec((tm, D), lambda i, e, f: (i, 0)),
                pl.BlockSpec((1, D), lambda i, e, f: (0, 0)),
                modspec(4), modspec(3), modspec(5)]
    args = [x, nw, mod_l, mod_l, mod_l]
    if gated:
        in_specs.append(pl.BlockSpec((tm, LANES), lambda i, e, f: (i, 0)))
        args.append(gate)
    in_specs += [pl.BlockSpec((None, D, tf), lambda i, e, f: (e, 0, f)),
                 pl.BlockSpec((None, D, tf), lambda i, e, f: (e, 0, f)),
                 pl.BlockSpec((None, tf, D), lambda i, e, f: (e, f, 0))]
    args += [w1, w3, w2]
    return pl.pallas_call(
        functools.partial(_ffn_kernel, gated=gated),
        grid=(M // tm, E, F // tf),
        in_specs=in_specs,
        out_specs=pl.BlockSpec((tm, D), lambda i, e, f: (i, 0)),
        out_shape=jax.ShapeDtypeStruct((M, D), F32),
        scratch_shapes=[pltpu.VMEM((tm, D), BF16), pltpu.VMEM((tm, D), F32)],
        compiler_params=_cparams("parallel", "arbitrary", "arbitrary"),
        name="ffn_gated" if gated else "ffn_dense",
    )(*args)


MOE_TC = 512
MOE_TS = 512


def _router_kernel(x_ref, nw_ref, sc_ref, sh_ref, wh_ref, wl_ref,
                   h_ref, gate_ref, rank_ref, rank_t_ref, cstart_ref, cnt_ref, carry_ref, carry_t_ref):
    @pl.when(pl.program_id(0) == 0)
    def _():
        carry_ref[...] = jnp.zeros_like(carry_ref)
        carry_t_ref[...] = jnp.zeros_like(carry_t_ref)

    h = _norm_mod(x_ref[...], nw_ref[...], sc_ref[...], sh_ref[...])
    hh = h.astype(BF16)
    h_ref[...] = hh
    hl = (h - hh.astype(F32)).astype(BF16)
    logits = _dot(hh, wh_ref[...]) + (_dot(hl, wh_ref[...]) + _dot(hh, wl_ref[...]))
    tm = logits.shape[0]
    lane = lax.broadcasted_iota(jnp.int32, logits.shape, 1)
    logits = jnp.where(lane < N_EXPERTS, logits, REMOVED)
    lane_f = lane.astype(F32)
    v1 = jnp.max(logits, axis=-1, keepdims=True)
    i1 = jnp.min(jnp.where(logits == v1, lane_f, float(LANES)), axis=-1, keepdims=True)
    rest = jnp.where(lane_f == i1, REMOVED, logits)
    v2 = jnp.max(rest, axis=-1, keepdims=True)
    i2 = jnp.min(jnp.where(rest == v2, lane_f, float(LANES)), axis=-1, keepdims=True)
    e2 = jnp.exp(v2 - v1)
    w1 = 1.0 / (1.0 + e2)
    w2 = e2 / (1.0 + e2)
    gate_ref[...] = jnp.where(lane_f == i1, w1, jnp.where(lane_f == i2, w2, 0.0))

    sel = jnp.where((lane_f == i1) | (lane_f == i2), 1.0, 0.0)
    sel_t = sel.T[0:N_EXPERTS]
    ri = lax.broadcasted_iota(jnp.int32, (tm, tm), 0)
    ci = lax.broadcasted_iota(jnp.int32, (tm, tm), 1)
    before = jnp.where(ci < ri, 1.0, 0.0).astype(BF16)
    after = jnp.where(ri < ci, 1.0, 0.0).astype(BF16)
    carry = carry_ref[0:1, :]
    carry_t = carry_t_ref[:, 0:1]
    rank = _dot(before, sel.astype(BF16)) + carry
    rank_t = _dot(sel_t.astype(BF16), after) + carry_t
    rank_ref[...] = jnp.where(sel > 0.0, rank, -1.0)
    rank_t_ref[...] = jnp.where(sel_t > 0.0, rank_t, -1.0)
    cstart_ref[0] = carry_ref[...]
    carry_ref[...] = carry_ref[...] + jnp.sum(sel, axis=0, keepdims=True)
    carry_t_ref[...] = carry_t_ref[...] + jnp.sum(sel_t, axis=1, keepdims=True)
    cnt_ref[...] = carry_ref[...]


def router(x, mod_l, nw, w_router, T):
    M, D = x.shape
    tm = MOE_TC
    per_b = T // tm
    wp = jnp.zeros((D, LANES), F32).at[:, :N_EXPERTS].set(w_router)
    wh = wp.astype(BF16)
    wl = (wp - wh.astype(F32)).astype(BF16)
    nc = M // tm
    return pl.pallas_call(
        _router_kernel,
        grid=(nc,),
        in_specs=[pl.BlockSpec((tm, D), lambda i: (i, 0)),
                  pl.BlockSpec((1, D), lambda i: (0, 0))]
        + _mod_specs(T, tm, 4, 3, 1)
        + [pl.BlockSpec((D, LANES), lambda i: (0, 0)),
           pl.BlockSpec((D, LANES), lambda i: (0, 0))],
        out_specs=[pl.BlockSpec((tm, D), lambda i: (i, 0)),
                   pl.BlockSpec((tm, LANES), lambda i: (i, 0)),
                   pl.BlockSpec((tm, LANES), lambda i: (i, 0)),
                   pl.BlockSpec((N_EXPERTS, tm), lambda i: (0, i)),
                   pl.BlockSpec((1, 8, LANES), lambda i: (i, 0, 0)),
                   pl.BlockSpec((8, LANES), lambda i: (0, 0))],
        out_shape=[jax.ShapeDtypeStruct((M, D), BF16),
                   jax.ShapeDtypeStruct((M, LANES), F32),
                   jax.ShapeDtypeStruct((M, LANES), F32),
                   jax.ShapeDtypeStruct((N_EXPERTS, M), F32),
                   jax.ShapeDtypeStruct((nc, 8, LANES), F32),
                   jax.ShapeDtypeStruct((8, LANES), F32)],
        scratch_shapes=[pltpu.VMEM((8, LANES), F32), pltpu.VMEM((8, LANES), F32)],
        compiler_params=_cparams("arbitrary"),
        name="moe_router",
    )(x, nw, mod_l, mod_l, wh, wl)


def moe_schedule(counts, cstart, M):
    ts, tc = MOE_TS, MOE_TC
    nc = M // tc
    rt = (2 * M) // ts + N_EXPERTS
    smax = rt + N_EXPERTS * nc
    i32 = jnp.int32
    cnt = counts.astype(i32)
    cs = cstart.astype(i32)
    ce = jnp.concatenate([cs[1:], cnt[None]], axis=0)
    ntile = (cnt + ts - 1) // ts
    tile_end = jnp.cumsum(ntile)
    tile_off = tile_end - ntile
    total_tiles = tile_end[-1]
    r = jnp.arange(rt, dtype=i32)
    e_r = jnp.minimum(jnp.searchsorted(tile_end, r, side="right"), N_EXPERTS - 1).astype(i32)
    valid_r = r < total_tiles
    k_r = r - tile_off[e_r]
    lo = k_r * ts
    hi = jnp.minimum(lo + ts, cnt[e_r])
    ov = (ce[:, e_r] > lo[None]) & (cs[:, e_r] < hi[None]) & valid_r[None]
    c_lo = jnp.argmax(ov, axis=0).astype(i32)
    n_c = jnp.sum(ov, axis=0).astype(i32)
    pend = jnp.cumsum(n_c)
    pstart = pend - n_c
    n_pairs = pend[-1]
    s = jnp.minimum(jnp.arange(smax, dtype=i32), n_pairs - 1)
    g_r = jnp.minimum(jnp.searchsorted(pend, s, side="right"), rt - 1).astype(i32)
    g_c = c_lo[g_r] + (s - pstart[g_r])
    g_valid = jnp.arange(smax, dtype=i32) < n_pairs
    g_first = (s == pstart[g_r]) & g_valid
    gather = dict(r=g_r, c=g_c, e=e_r[g_r], base=k_r[g_r] * ts, first=g_first.astype(i32), valid=g_valid.astype(i32))

    row_off = tile_off * ts
    has = ce > cs
    r_lo = (row_off[None] + cs) // ts
    r_hi = (row_off[None] + ce - 1) // ts
    n_q = jnp.where(has, r_hi - r_lo + 1, 0).reshape(-1).astype(i32)
    qend = jnp.cumsum(n_q)
    qstart = qend - n_q
    n_pairs2 = qend[-1]
    s2 = jnp.minimum(jnp.arange(smax, dtype=i32), n_pairs2 - 1)
    q = jnp.minimum(jnp.searchsorted(qend, s2, side="right"), nc * N_EXPERTS - 1).astype(i32)
    c_c = q // N_EXPERTS
    c_e = q % N_EXPERTS
    c_r = r_lo.reshape(-1)[q] + (s2 - qstart[q])
    c_valid = jnp.arange(smax, dtype=i32) < n_pairs2
    tok_first = qstart.reshape(nc, N_EXPERTS)[:, 0]
    tok_last = qend.reshape(nc, N_EXPERTS)[:, -1] - 1
    c_first = (s2 == tok_first[c_c]) & c_valid
    c_last = (s2 == tok_last[c_c]) & c_valid
    combine = dict(r=c_r.astype(i32), c=c_c, e=c_e, base=(c_r * ts - row_off[c_e]).astype(i32),
                   first=c_first.astype(i32), last=c_last.astype(i32), valid=c_valid.astype(i32))
    tiles = dict(e=e_r, total=total_tiles.reshape(1).astype(i32))
    return tiles, gather, combine, rt, smax


def _moe_gather_kernel(r_s, c_s, e_s, base_s, first_s, valid_s, rank_t_ref, h_ref, o_ref):
    s = pl.program_id(0)

    @pl.when(first_s[s] == 1)
    def _():
        o_ref[...] = jnp.zeros_like(o_ref)

    @pl.when(valid_s[s] == 1)
    def _():
        ts = o_ref.shape[0]
        tc = h_ref.shape[0]
        rk = rank_t_ref[pl.ds(e_s[s], 1), :] - base_s[s].astype(F32)
        row = lax.broadcasted_iota(jnp.int32, (ts, tc), 0).astype(F32)
        onehot = jnp.where(rk == row, 1.0, 0.0).astype(BF16)
        o_ref[...] += _dot(onehot, h_ref[...]).astype(o_ref.dtype)


def moe_gather(h, rank_t, g, rt, smax):
    M, D = h.shape
    ts, tc = MOE_TS, MOE_TC
    return pl.pallas_call(
        _moe_gather_kernel,
        grid_spec=pltpu.PrefetchScalarGridSpec(
            num_scalar_prefetch=6,
            grid=(smax,),
            in_specs=[pl.BlockSpec((N_EXPERTS, tc), lambda s, r, c, *_: (0, c[s])),
                      pl.BlockSpec((tc, D), lambda s, r, c, *_: (c[s], 0))],
            out_specs=pl.BlockSpec((ts, D), lambda s, r, c, *_: (r[s], 0)),
        ),
        out_shape=jax.ShapeDtypeStruct((rt * ts, D), BF16),
        compiler_params=_cparams("arbitrary"),
        name="moe_gather",
    )(g["r"], g["c"], g["e"], g["base"], g["first"], g["valid"], rank_t, h)


def _moe_up_kernel(e_r, total, x_ref, w1_ref, w3_ref, o_ref):
    @pl.when(pl.program_id(1) < total[0])
    def _():
        x = x_ref[...]
        u = _dot(x, w1_ref[...])
        v = _dot(x, w3_ref[...])
        o_ref[...] = (u * _sigmoid(u) * v).astype(o_ref.dtype)


def moe_up(xs, w1, w3, tiles, rt, *, tf=896):
    R, D = xs.shape
    ts = MOE_TS
    F = w1.shape[-1]
    live = lambda r, total: jnp.minimum(r, total[0] - 1)
    return pl.pallas_call(
        _moe_up_kernel,
        grid_spec=pltpu.PrefetchScalarGridSpec(
            num_scalar_prefetch=2,
            grid=(F // tf, rt),
            in_specs=[pl.BlockSpec((ts, D), lambda n, r, e, total: (live(r, total), 0)),
                      pl.BlockSpec((None, D, tf), lambda n, r, e, total: (e[live(r, total)], 0, n)),
                      pl.BlockSpec((None, D, tf), lambda n, r, e, total: (e[live(r, total)], 0, n))],
            out_specs=pl.BlockSpec((ts, tf), lambda n, r, e, total: (r, n)),
        ),
        out_shape=jax.ShapeDtypeStruct((R, F), BF16),
        compiler_params=_cparams("arbitrary", "arbitrary"),
        name="moe_up",
    )(tiles["e"], tiles["total"], xs, w1, w3)


def _moe_down_kernel(e_r, total, a_ref, w2_ref, o_ref):
    @pl.when(pl.program_id(0) < total[0])
    def _():
        o_ref[...] = _dot(a_ref[...], w2_ref[...]).astype(o_ref.dtype)


def moe_down(a, w2, tiles, rt):
    R, F = a.shape
    ts = MOE_TS
    D = w2.shape[-1]
    live = lambda r, total: jnp.minimum(r, total[0] - 1)
    return pl.pallas_call(
        _moe_down_kernel,
        grid_spec=pltpu.PrefetchScalarGridSpec(
            num_scalar_prefetch=2,
            grid=(rt,),
            in_specs=[pl.BlockSpec((ts, F), lambda r, e, total: (live(r, total), 0)),
                      pl.BlockSpec((None, F, D), lambda r, e, total: (e[live(r, total)], 0, 0))],
            out_specs=pl.BlockSpec((ts, D), lambda r, e, total: (r, 0)),
        ),
        out_shape=jax.ShapeDtypeStruct((R, D), BF16),
        compiler_params=_cparams("arbitrary"),
        name="moe_down",
    )(tiles["e"], tiles["total"], a, w2)


def _moe_combine_kernel(r_s, c_s, e_s, base_s, first_s, last_s, valid_s,
                        rank_ref, gate_ref, y_ref, x_ref, g2_ref, o_ref, acc_ref):
    s = pl.program_id(0)

    @pl.when(first_s[s] == 1)
    def _():
        acc_ref[...] = jnp.zeros_like(acc_ref)

    @pl.when(valid_s[s] == 1)
    def _():
        tc = rank_ref.shape[0]
        ts = y_ref.shape[0]
        lane = lax.broadcasted_iota(jnp.int32, (tc, LANES), 1)
        mine = lane == e_s[s]
        rank_col = jnp.sum(jnp.where(mine, rank_ref[...], 0.0), axis=-1, keepdims=True) - base_s[s].astype(F32)
        gate_col = jnp.sum(jnp.where(mine, gate_ref[...], 0.0), axis=-1, keepdims=True)
        col = lax.broadcasted_iota(jnp.int32, (tc, ts), 1).astype(F32)
        onehot = jnp.where(rank_col == col, 1.0, 0.0).astype(BF16)
        acc_ref[...] += gate_col * _dot(onehot, y_ref[...])

    @pl.when(last_s[s] == 1)
    def _():
        o_ref[...] = x_ref[...] + g2_ref[...] * acc_ref[...]


def moe_combine(y, rank, gate, x, mod_l, cb, T, smax):
    M, D = x.shape
    ts, tc = MOE_TS, MOE_TC
    per_b = T // tc
    tok = lambda width: pl.BlockSpec((tc, width), lambda s, r, c, *_: (c[s], 0))
    return pl.pallas_call(
        _moe_combine_kernel,
        grid_spec=pltpu.PrefetchScalarGridSpec(
            num_scalar_prefetch=7,
            grid=(smax,),
            in_specs=[tok(LANES), tok(LANES),
                      pl.BlockSpec((ts, D), lambda s, r, c, *_: (r[s], 0)),
                      tok(D),
                      pl.BlockSpec((None, None, 1, D), lambda s, r, c, *_: (c[s] // per_b, 5, 0, 0))],
            out_specs=tok(D),
            scratch_shapes=[pltpu.VMEM((tc, D), F32)],
        ),
        out_shape=jax.ShapeDtypeStruct((M, D), F32),
        compiler_params=_cparams("arbitrary"),
        name="moe_combine",
    )(cb["r"], cb["c"], cb["e"], cb["base"], cb["first"], cb["last"], cb["valid"], rank, gate, y, x, mod_l)


def moe_ffn(x, mod_l, nw, w_router, w1, w3, w2, T):
    M = x.shape[0]
    h, gate, rank, rank_t, cstart, cnt = router(x, mod_l, nw, w_router, T)
    tiles, g, cb, rt, smax = moe_schedule(cnt[0, :N_EXPERTS], cstart[:, 0, :N_EXPERTS], M)
    xs = moe_gather(h, rank_t, g, rt, smax)
    a = moe_up(xs, w1, w3, tiles, rt)
    y = moe_down(a, w2, tiles, rt)
    return moe_combine(y, rank, gate, x, mod_l, cb, T, smax)


def _final_norm_kernel(x_ref, w_ref, o_ref):
    x = x_ref[...]
    ms = jnp.mean(x * x, axis=-1, keepdims=True)
    o_ref[...] = x * lax.rsqrt(ms + NORM_EPS) * w_ref[...]


def final_norm(x, w, *, tm=1024):
    M, D = x.shape
    return pl.pallas_call(
        _final_norm_kernel,
        grid=(M // tm,),
        in_specs=[pl.BlockSpec((tm, D), lambda i: (i, 0)), pl.BlockSpec((1, D), lambda i: (0, 0))],
        out_specs=pl.BlockSpec((tm, D), lambda i: (i, 0)),
        out_shape=jax.ShapeDtypeStruct((M, D), F32),
        compiler_params=_cparams("parallel"),
        name="final_norm",
    )(x, w)


def nsa_constants(T):
    n_sel = T // SEL_LEN
    nsp = max(LANES, n_sel)
    ncp = T // CMP_STRIDE
    cmp_start = np.arange(ncp) * CMP_STRIDE
    sel_start = np.arange(nsp) * SEL_LEN
    ov = ((cmp_start[:, None] < sel_start[None, :] + SEL_LEN)
          & (cmp_start[:, None] + CMP_LEN > sel_start[None, :]))
    ov[(T - CMP_LEN) // CMP_STRIDE + 1:] = False
    ov[:, n_sel:] = False
    e_mat = (np.arange(T)[None, :] // SEL_LEN) == np.arange(nsp)[:, None]
    return jnp.asarray(ov.T, BF16), jnp.asarray(e_mat, BF16)


def token_mixing(x, mod_l, lw, consts, B, T):
    M = B * T
    cos_t, sin_t, ov_t, e_mat, ret_consts, ex = consts
    p1 = proj_rope(x, mod_l, lw["norm_mix"], lw["w1"], cos_t, sin_t, p1_scales(), T).reshape(B, T, P1_COLS)
    p2 = proj_plain(x, mod_l, lw["norm_mix"], lw["w2"], T).reshape(B, T, P2_COLS)

    def group_rows(a):
        return a.reshape(B, T, NSA_GROUPS, HEAD_DIM).transpose(0, 2, 1, 3).reshape(
            B, NSA_GROUPS, T // CMP_STRIDE, CMP_STRIDE * HEAD_DIM)

    xr = jnp.stack([group_rows(p1[:, :, P1_NKC:P1_NKC + LANES]), group_rows(p2[:, :, P2_NVC:P2_NVC + LANES])])
    cmp_out = compress(xr, lw["cmp_pe"], lw["cmp_w1"], lw["cmp_w2"])
    cmp_out = cmp_out.transpose(0, 1, 3, 2, 4).reshape(2, B, T // CMP_STRIDE, LANES)
    o_cmp, sel = nsa_cmp_select(p1, cmp_out[0], cmp_out[1], ov_t, T)
    o_sel = nsa_selected(p1, p2, sel, e_mat, T)
    o_win = nsa_window(p1, p2, T)

    o_ret = retention(p1, p2, ret_consts, T)

    ff = p2[:, :, P2_SMALL + 3 * NSA_HEADS:P2_SMALL + 3 * NSA_HEADS + FOX_HEADS].astype(F32)
    ff = ff.transpose(0, 2, 1).reshape(B, FOX_HEADS, T // LANES, LANES)
    cum = fox_cum(ff, lw["fox_bias"]).reshape(B, FOX_HEADS // 2, 2, 1, T)
    o_fox = fox_attention(p2, cum, T)

    return readout(o_cmp.reshape(M, -1), o_sel.reshape(M, -1), o_win.reshape(M, -1), p2.reshape(M, P2_COLS),
                   o_ret.reshape(M, -1), o_fox.reshape(M, -1), x, mod_l, ex,
                   lw["wn"], lw["wr"], lw["wf"], lw["wo"], T)


def layer_weights(l, norm_mix, w_in, cmp_k_pe, cmp_k_w1, cmp_k_w2, cmp_v_pe, cmp_v_w1, cmp_v_w2, fox_f_bias,
                  w_read_nsa, w_read_ret, w_read_fox, w_out):
    w1, w2 = split_w_in(w_in[l])
    pe = jnp.stack([cmp_k_pe[l].reshape(1, -1), cmp_v_pe[l].reshape(1, -1)])
    pe = jnp.broadcast_to(pe, (2, 8, pe.shape[-1])).astype(BF16)
    return {
        "norm_mix": norm_mix[l].reshape(1, -1),
        "w1": w1, "w2": w2,
        "cmp_pe": pe,
        "cmp_w1": jnp.stack([cmp_k_w1[l], cmp_v_w1[l]]).astype(BF16),
        "cmp_w2": jnp.stack([cmp_k_w2[l], cmp_v_w2[l]]).astype(BF16),
        "fox_bias": jnp.broadcast_to(fox_f_bias[l][:, None, None], (FOX_HEADS, 1, LANES)),
        "wn": pad_read_nsa(w_read_nsa[l]),
        "wr": w_read_ret[l].astype(BF16),
        "wf": w_read_fox[l].astype(BF16),
        "wo": w_out[l].astype(BF16),
    }


def kernel(x, c, ada_w, ada_b, norm_mix, norm_ffn, w_in, cmp_k_pe, cmp_k_w1, cmp_k_w2, cmp_v_pe, cmp_v_w1,
           cmp_v_w2, fox_f_bias, w_read_nsa, w_read_ret, w_read_fox, w_out, ffn_w1, ffn_w3, ffn_w2, router_w,
           moe_w1, moe_w3, moe_w2, final_norm_w):
    B, T, D = x.shape
    M = B * T
    depth = ada_w.shape[0]
    mod = modulation(c, ada_w, ada_b)
    cos_t, sin_t = rope_tables(T)
    ov_t, e_mat = nsa_constants(T)
    consts = (cos_t, sin_t, ov_t, e_mat, retention_consts(), nsa_gate_expand())
    xs = x.reshape(M, D)
    for l in range(depth):
        lw = layer_weights(l, norm_mix, w_in, cmp_k_pe, cmp_k_w1, cmp_k_w2, cmp_v_pe, cmp_v_w1, cmp_v_w2,
                           fox_f_bias, w_read_nsa, w_read_ret, w_read_fox, w_out)
        xs = token_mixing(xs, mod[l], lw, consts, B, T)
        nf = norm_ffn[l].reshape(1, D)
        if l % 2 == 0:
            k = l // 2
            xs = ffn(xs, mod[l], nf, ffn_w1[k][None].astype(BF16), ffn_w3[k][None].astype(BF16),
                     ffn_w2[k][None].astype(BF16), None, T, tm=512, tf=D_FF // 2)
        else:
            k = l // 2
            xs = moe_ffn(xs, mod[l], nf, router_w[k], moe_w1[k].astype(BF16), moe_w3[k].astype(BF16),
                         moe_w2[k].astype(BF16), T)
    return final_norm(xs, final_norm_w.reshape(1, D)).reshape(B, T, D)
```

```python
import functools
import math

import jax
import jax.numpy as jnp
import numpy as np
from jax import lax
from jax.experimental import pallas as pl
from jax.experimental.pallas import tpu as pltpu

F32 = jnp.float32
BF16 = jnp.bfloat16

D_MODEL = 1024
DEPTH = 2
HEAD_DIM = 64
ROPE_THETA = 10000.0
NORM_EPS = 1e-6
NEG_INF = -1e30
REMOVED = -3e38

NSA_HEADS = 8
NSA_GROUPS = 2
NSA_HPG = NSA_HEADS // NSA_GROUPS
CMP_LEN = 32
CMP_STRIDE = 16
CMP_HIDDEN = 256
SEL_LEN = 64
SEL_TOPN = 16
WINDOW = 512
FORCE_SCORE = 1e4
NSA_QBLOCK = 128

RET_HEADS = 4
RET_QK_DIM = 64
RET_V_DIM = 128
RET_CHUNK = 128

FOX_HEADS = 8
FOX_TQ = 1024
LOG2E = 1.4426950408889634

D_FF = 2816
N_EXPERTS = 8
D_FF_EXPERT = 3584

LANES = 128
VMEM_LIMIT = 56 * 1024 * 1024

P1_NQ = 0
P1_RQ = 1024
P1_RK = 1536
P1_NKC = 1792
P1_NKS = 1920
P1_NKW = 2048
P1_COLS = 2176
P2_MG = 0
P2_RV = 3072
P2_RG = 3584
P2_FQ = 4096
P2_FK = 5120
P2_FV = 5632
P2_NVC = 6144
P2_NVS = 6272
P2_NVW = 6400
P2_SMALL = 6528
P2_COLS = 6656


def _cparams(*sem):
    return pltpu.CompilerParams(dimension_semantics=tuple(sem), vmem_limit_bytes=VMEM_LIMIT)


def _sigmoid(x):
    return 1.0 / (1.0 + jnp.exp(-x))


def _dot(a, b):
    return jnp.dot(a, b, preferred_element_type=F32)


def _dot_nt(a, b):
    return lax.dot_general(a, b, (((1,), (1,)), ((), ())), preferred_element_type=F32)


def _dot_tn(a, b):
    return lax.dot_general(a, b, (((0,), (0,)), ((), ())), preferred_element_type=F32)


def _split3(x):
    hi = x.astype(BF16)
    r1 = x - hi.astype(F32)
    mid = r1.astype(BF16)
    lo = (r1 - mid.astype(F32)).astype(BF16)
    return hi, mid, lo


def _norm_mod(x, nw, sc, sh):
    ms = jnp.mean(x * x, axis=-1, keepdims=True)
    y = x * lax.rsqrt(ms + NORM_EPS) * nw
    return y * (1.0 + sc) + sh


def _mod_kernel(c_ref, w_ref, b_ref, o_ref):
    c = c_ref[...]
    s = c * _sigmoid(c)
    o_ref[0] = _dot(s.astype(BF16), w_ref[0].astype(BF16)) + b_ref[0]


def modulation(c, ada_w, ada_b):
    B, D = c.shape
    depth = ada_w.shape[0]
    rows = 8
    c_pad = jnp.zeros((rows, D), F32).at[:B].set(c)
    out = pl.pallas_call(
        _mod_kernel,
        grid=(depth, 6),
        in_specs=[pl.BlockSpec((rows, D), lambda l, j: (0, 0)),
                  pl.BlockSpec((1, D, D), lambda l, j: (l, 0, j)),
                  pl.BlockSpec((1, 1, D), lambda l, j: (l, 0, j))],
        out_specs=pl.BlockSpec((1, rows, D), lambda l, j: (l, 0, j)),
        out_shape=jax.ShapeDtypeStruct((depth, rows, 6 * D), F32),
        compiler_params=_cparams("parallel", "parallel"),
        name="modulation",
    )(c_pad, ada_w, ada_b.reshape(depth, 1, 6 * D))
    return out[:, :B].reshape(depth, B, 6, 1, D)


def _proj_plain_kernel(x_ref, nw_ref, sc_ref, sh_ref, w_ref, o_ref, h_ref):
    @pl.when(pl.program_id(1) == 0)
    def _():
        h_ref[...] = _norm_mod(x_ref[...], nw_ref[...], sc_ref[...], sh_ref[...]).astype(BF16)

    o_ref[...] = _dot(h_ref[...], w_ref[...]).astype(o_ref.dtype)


def _proj_rope_kernel(x_ref, nw_ref, sc_ref, sh_ref, w_ref, cos_ref, sin_ref, o_ref, *, scales):
    h = _norm_mod(x_ref[...], nw_ref[...], sc_ref[...], sh_ref[...]).astype(BF16)
    y = _dot(h, w_ref[...])
    cos = cos_ref[...]
    sin = sin_ref[...]
    lane = lax.broadcasted_iota(jnp.int32, cos.shape, 1)
    first_half = (lane % HEAD_DIM) < (HEAD_DIM // 2)
    for g, scale in enumerate(scales):
        yg = y[:, g * LANES:(g + 1) * LANES]
        rot = jnp.where(first_half, pltpu.roll(yg, LANES - HEAD_DIM // 2, 1),
                        pltpu.roll(yg, HEAD_DIM // 2, 1))
        r = yg * cos + rot * sin
        if scale != 1.0:
            r = r * scale
        o_ref[:, g * LANES:(g + 1) * LANES] = r.astype(o_ref.dtype)


def _mod_specs(T, tm, sc_idx, sh_idx, nargs):
    per_b = T // tm
    if nargs == 1:
        return [pl.BlockSpec((None, None, 1, D_MODEL), lambda i: (i // per_b, sc_idx, 0, 0)),
                pl.BlockSpec((None, None, 1, D_MODEL), lambda i: (i // per_b, sh_idx, 0, 0))]
    return [pl.BlockSpec((None, None, 1, D_MODEL), lambda i, j: (i // per_b, sc_idx, 0, 0)),
            pl.BlockSpec((None, None, 1, D_MODEL), lambda i, j: (i // per_b, sh_idx, 0, 0))]


def proj_plain(x, mod_l, nw, w, T, *, tm=1024, tn=512):
    M, D = x.shape
    N = w.shape[1]
    return pl.pallas_call(
        _proj_plain_kernel,
        grid=(M // tm, N // tn),
        in_specs=[pl.BlockSpec((tm, D), lambda i, j: (i, 0)),
                  pl.BlockSpec((1, D), lambda i, j: (0, 0))]
        + _mod_specs(T, tm, 1, 0, 2)
        + [pl.BlockSpec((D, tn), lambda i, j: (0, j))],
        out_specs=pl.BlockSpec((tm, tn), lambda i, j: (i, j)),
        out_shape=jax.ShapeDtypeStruct((M, N), BF16),
        scratch_shapes=[pltpu.VMEM((tm, D), BF16)],
        compiler_params=_cparams("parallel", "arbitrary"),
        name="proj_plain",
    )(x, nw, mod_l, mod_l, w)


def proj_rope(x, mod_l, nw, w, cos, sin, scales, T, *, tm=512):
    M, D = x.shape
    N = w.shape[1]
    per_b = T // tm
    return pl.pallas_call(
        functools.partial(_proj_rope_kernel, scales=scales),
        grid=(M // tm,),
        in_specs=[pl.BlockSpec((tm, D), lambda i: (i, 0)),
                  pl.BlockSpec((1, D), lambda i: (0, 0))]
        + _mod_specs(T, tm, 1, 0, 1)
        + [pl.BlockSpec((D, N), lambda i: (0, 0)),
           pl.BlockSpec((tm, LANES), lambda i: (i % per_b, 0)),
           pl.BlockSpec((tm, LANES), lambda i: (i % per_b, 0))],
        out_specs=pl.BlockSpec((tm, N), lambda i: (i, 0)),
        out_shape=jax.ShapeDtypeStruct((M, N), BF16),
        compiler_params=_cparams("parallel"),
        name="proj_rope",
    )(x, nw, mod_l, mod_l, w, cos, sin)


def rope_tables(T):
    d = HEAD_DIM
    pos = jnp.arange(T, dtype=F32)
    inv = ROPE_THETA ** (-jnp.arange(0, d, 2, dtype=F32) / d)
    ang = pos[:, None] * inv[None, :]
    cos = jnp.cos(ang)
    sin = jnp.sin(ang)
    cos_t = jnp.concatenate([cos, cos, cos, cos], axis=-1)
    sin_t = jnp.concatenate([-sin, sin, -sin, sin], axis=-1)
    return cos_t, sin_t


def _pad_heads(w, n_heads, half_of_head):
    D = w.shape[0]
    w = w.reshape(D, n_heads, HEAD_DIM)
    z = jnp.zeros_like(w)
    halves = np.array([half_of_head(h) for h in range(n_heads)])
    lo = jnp.where(halves[None, :, None] == 0, w, z)
    hi = jnp.where(halves[None, :, None] == 1, w, z)
    return jnp.concatenate([lo, hi], axis=-1).reshape(D, n_heads * LANES)


def split_w_in(w_in):
    sizes = [512, 128, 128, 128, 128, 128, 128, 24, 256, 256, 512, 512, 512, 512, 512, 8, 3072]
    offs = np.cumsum([0] + sizes)
    (nq, nkc, nvc, nks, nvs, nkw, nvw, ngate, rq, rk, rv, rg, fq, fk, fv, ff, mg) = [
        w_in[:, offs[i]:offs[i + 1]] for i in range(len(sizes))]
    D = w_in.shape[0]
    nq_p = _pad_heads(nq, NSA_HEADS, lambda h: h // NSA_HPG)
    rq_p = _pad_heads(rq, RET_HEADS, lambda h: h % 2)
    fq_p = _pad_heads(fq, FOX_HEADS, lambda h: 0) * (HEAD_DIM ** -0.5 * LOG2E)
    small = jnp.concatenate([ngate, ff, jnp.zeros((D, LANES - 32), w_in.dtype)], axis=-1)
    w1 = jnp.concatenate([nq_p, rq_p, rk, nkc, nks, nkw], axis=-1).astype(BF16)
    w2 = jnp.concatenate([mg, rv, rg, fq_p, fk, fv, nvc, nvs, nvw, small], axis=-1).astype(BF16)
    assert w1.shape[1] == P1_COLS and w2.shape[1] == P2_COLS
    return w1, w2


def p1_scales():
    s = [1.0] * (P1_COLS // LANES)
    for g in range(P1_NQ // LANES, P1_RQ // LANES):
        s[g] = HEAD_DIM ** -0.5 * LOG2E
    for g in range(P1_RK // LANES, P1_NKC // LANES):
        s[g] = RET_QK_DIM ** -0.5
    return tuple(s)


def _compress_kernel(x_ref, pe_ref, w1_ref, w2_ref, o_ref):
    r = x_ref[...]
    half = r.shape[1]
    w1 = w1_ref[...]
    a = _dot(r, w1[:half])
    b = _dot(r, w1[half:])
    pe = _dot(pe_ref[...], w1)[0:1]
    n = a.shape[0]
    hid = a + pltpu.roll(b, n - 1, 0) + pe
    hid = hid * _sigmoid(hid)
    o_ref[...] = _dot(hid.astype(BF16), w2_ref[...]).astype(o_ref.dtype)


def compress(xr, pe, w1, w2):
    _, B, G, R, W = xr.shape
    H = w1.shape[-1]
    return pl.pallas_call(
        _compress_kernel,
        grid=(2, B, G),
        in_specs=[pl.BlockSpec((None, None, None, R, W), lambda s, b, g: (s, b, g, 0, 0)),
                  pl.BlockSpec((None, 8, 2 * W), lambda s, b, g: (s, 0, 0)),
                  pl.BlockSpec((None, 2 * W, H), lambda s, b, g: (s, 0, 0)),
                  pl.BlockSpec((None, H, HEAD_DIM), lambda s, b, g: (s, 0, 0))],
        out_specs=pl.BlockSpec((None, None, None, R, HEAD_DIM), lambda s, b, g: (s, b, g, 0, 0)),
        out_shape=jax.ShapeDtypeStruct((2, B, G, R, HEAD_DIM), BF16),
        compiler_params=_cparams("parallel", "parallel", "parallel"),
        name="nsa_compress",
    )(xr, pe, w1, w2)


def _stack_heads(q_ref, g):
    return jnp.concatenate(
        [q_ref[:, (NSA_HPG * g + hh) * LANES:(NSA_HPG * g + hh + 1) * LANES] for hh in range(NSA_HPG)],
        axis=0)


def _store_heads(o_ref, g, o, tq):
    for hh in range(NSA_HPG):
        h = NSA_HPG * g + hh
        o_ref[:, h * LANES:(h + 1) * LANES] = o[hh * tq:(hh + 1) * tq].astype(o_ref.dtype)


def _nsa_cmp_kernel(q_ref, kc_ref, vc_ref, ov_ref, o_ref, m_ref, *, tq, n_sel, top_n):
    t0 = pl.program_id(1) * tq
    kc = kc_ref[...]
    vc = vc_ref[...]
    ncp = kc.shape[0]
    nsp = ov_ref.shape[0]
    rows = NSA_HPG * tq
    n_idx = lax.broadcasted_iota(jnp.int32, (rows, ncp), 1)
    t_idx = t0 + lax.broadcasted_iota(jnp.int32, (rows, ncp), 0) % tq
    valid = (n_idx * CMP_STRIDE + (CMP_LEN - 1)) <= t_idx
    j_idx = lax.broadcasted_iota(jnp.int32, (nsp, tq), 0)
    cur = (t0 + lax.broadcasted_iota(jnp.int32, (nsp, tq), 1)) // SEL_LEN
    forced = (j_idx == 0) | (j_idx == cur) | (j_idx == cur - 1)
    j_f = j_idx.astype(F32)
    for g in range(NSA_GROUPS):
        q = _stack_heads(q_ref, g)
        s = jnp.where(valid, _dot_nt(q, kc), NEG_INF)
        m = jnp.max(s, axis=-1, keepdims=True)
        e = jnp.where(valid, jnp.exp2(s - m), 0.0)
        l = jnp.sum(e, axis=-1, keepdims=True)
        p = e / jnp.where(l > 0.0, l, 1.0)
        _store_heads(o_ref, g, _dot(p.astype(BF16), vc), tq)
        psum = p[0:tq]
        for hh in range(1, NSA_HPG):
            psum = psum + p[hh * tq:(hh + 1) * tq]
        imp_t = _dot_nt(ov_ref[...], psum.astype(BF16))
        score = jnp.where(forced, FORCE_SCORE, imp_t)
        score = jnp.where(j_idx <= cur, score, NEG_INF)
        score = jnp.where(j_idx < n_sel, score, REMOVED)
        sel = jnp.zeros((nsp, tq), F32)
        for _ in range(top_n):
            mx = jnp.max(score, axis=0, keepdims=True)
            idx = jnp.min(jnp.where(score == mx, j_f, float(nsp)), axis=0, keepdims=True)
            hit = j_f == idx
            sel = jnp.where(hit, 1.0, sel)
            score = jnp.where(hit, REMOVED, score)
        sel = jnp.where(j_idx <= cur, sel, 0.0)
        m_ref[g] = sel.T.astype(m_ref.dtype)


def nsa_cmp_select(p1, kc, vc, ov_t, T):
    B = p1.shape[0]
    tq = NSA_QBLOCK
    ncp = kc.shape[1]
    nsp = ov_t.shape[0]
    n_sel = T // SEL_LEN
    return pl.pallas_call(
        functools.partial(_nsa_cmp_kernel, tq=tq, n_sel=n_sel, top_n=min(SEL_TOPN, n_sel)),
        grid=(B, T // tq),
        in_specs=[pl.BlockSpec((None, tq, NSA_HEADS * LANES), lambda b, i: (b, i, 0)),
                  pl.BlockSpec((None, ncp, LANES), lambda b, i: (b, 0, 0)),
                  pl.BlockSpec((None, ncp, LANES), lambda b, i: (b, 0, 0)),
                  pl.BlockSpec((nsp, ncp), lambda b, i: (0, 0))],
        out_specs=[pl.BlockSpec((None, tq, NSA_HEADS * LANES), lambda b, i: (b, i, 0)),
                   pl.BlockSpec((None, NSA_GROUPS, tq, nsp), lambda b, i: (b, 0, i, 0))],
        out_shape=[jax.ShapeDtypeStruct((B, T, NSA_HEADS * LANES), BF16),
                   jax.ShapeDtypeStruct((B, NSA_GROUPS, T, nsp), BF16)],
        compiler_params=_cparams("parallel", "parallel"),
        name="nsa_cmp_select",
    )(p1, kc, vc, ov_t)


SEL_BONUS = 8192.0
NSA_SEL_TQ = 256
NSA_SEL_TK = 1024


def _nsa_sel_kernel(q_ref, k_ref, v_ref, m_ref, et_ref, o_ref, *, tq, tk):
    t0 = pl.program_id(1) * tq
    n_tiles = (t0 + tq + tk - 1) // tk
    rows = NSA_HPG * tq
    for g in range(NSA_GROUPS):
        q = jnp.concatenate([_stack_heads(q_ref, g), jnp.concatenate([m_ref[g]] * NSA_HPG, axis=0)], axis=1)
        den = HEAD_DIM * (1 - g)

        def step(j, carry, masked, q=q, g=g):
            m, acc = carry
            start = pl.multiple_of(j * tk, tk)
            ks = jnp.concatenate([k_ref[pl.ds(start, tk), :], et_ref[pl.ds(start, tk), :]], axis=1)
            s = _dot_nt(q, ks)
            if masked:
                trow = t0 + lax.broadcasted_iota(jnp.int32, (rows, tk), 0) % tq
                kpos = start + lax.broadcasted_iota(jnp.int32, (rows, tk), 1)
                s = jnp.where(kpos <= trow, s, NEG_INF)
            m_new = jnp.maximum(m, jnp.max(s, axis=-1, keepdims=True))
            p = jnp.exp2(s - m_new)
            acc = jnp.exp2(m - m_new) * acc + _dot(p.astype(BF16), v_ref[g, pl.ds(start, tk), :])
            return m_new, acc

        init = (jnp.full((rows, 1), NEG_INF, F32), jnp.zeros((rows, LANES), F32))
        carry = lax.fori_loop(0, n_tiles - 1, functools.partial(step, masked=False), init)
        _, acc = step(n_tiles - 1, carry, True)
        _store_heads(o_ref, g, acc / acc[:, den:den + 1], tq)


def nsa_value_augment(v):
    ones = jnp.ones_like(v[..., :HEAD_DIM])
    return jnp.stack([jnp.concatenate([v[..., :HEAD_DIM], ones], axis=-1),
                      jnp.concatenate([ones, v[..., HEAD_DIM:]], axis=-1)], axis=1)


def nsa_selected(p1, v_aug, sel, et_mat, T, *, tq=NSA_SEL_TQ, tk=NSA_SEL_TK):
    B = p1.shape[0]
    nsp = sel.shape[-1]
    return pl.pallas_call(
        functools.partial(_nsa_sel_kernel, tq=tq, tk=tk),
        grid=(B, T // tq),
        in_specs=[pl.BlockSpec((None, tq, NSA_HEADS * LANES), lambda b, i: (b, i, 0)),
                  pl.BlockSpec((None, T, LANES), lambda b, i: (b, 0, P1_NKS // LANES)),
                  pl.BlockSpec((None, NSA_GROUPS, T, LANES), lambda b, i: (b, 0, 0, 0)),
                  pl.BlockSpec((None, NSA_GROUPS, tq, nsp), lambda b, i: (b, 0, i, 0)),
                  pl.BlockSpec((T, nsp), lambda b, i: (0, 0))],
        out_specs=pl.BlockSpec((None, tq, NSA_HEADS * LANES), lambda b, i: (b, i, 0)),
        out_shape=jax.ShapeDtypeStruct((B, T, NSA_HEADS * LANES), BF16),
        compiler_params=_cparams("parallel", "parallel"),
        name="nsa_selected",
    )(p1, p1, v_aug, sel, et_mat)


def _nsa_win_kernel(q_ref, k_ref, v_ref, o_ref, *, tq):
    t0 = pl.program_id(1) * tq
    span = WINDOW + tq
    start = pl.multiple_of(jnp.maximum(t0 - WINDOW, 0), tq)
    ks = k_ref[pl.ds(start, span), :]
    vs = v_ref[pl.ds(start, span), :]
    rows = NSA_HPG * tq
    t_idx = t0 + lax.broadcasted_iota(jnp.int32, (rows, span), 0) % tq
    kpos = start + lax.broadcasted_iota(jnp.int32, (rows, span), 1)
    ok = (kpos <= t_idx) & ((t_idx - kpos) < WINDOW)
    for g in range(NSA_GROUPS):
        q = _stack_heads(q_ref, g)
        s = jnp.where(ok, _dot_nt(q, ks), NEG_INF)
        m = jnp.max(s, axis=-1, keepdims=True)
        p = jnp.exp2(s - m)
        l = jnp.sum(p, axis=-1, keepdims=True)
        _store_heads(o_ref, g, _dot(p.astype(BF16), vs) / l, tq)


def nsa_window(p1, p2, T):
    B = p1.shape[0]
    tq = NSA_QBLOCK
    return pl.pallas_call(
        functools.partial(_nsa_win_kernel, tq=tq),
        grid=(B, T // tq),
        in_specs=[pl.BlockSpec((None, tq, NSA_HEADS * LANES), lambda b, i: (b, i, 0)),
                  pl.BlockSpec((None, T, LANES), lambda b, i: (b, 0, P1_NKW // LANES)),
                  pl.BlockSpec((None, T, LANES), lambda b, i: (b, 0, P2_NVW // LANES))],
        out_specs=pl.BlockSpec((None, tq, NSA_HEADS * LANES), lambda b, i: (b, i, 0)),
        out_shape=jax.ShapeDtypeStruct((B, T, NSA_HEADS * LANES), BF16),
        compiler_params=_cparams("parallel", "parallel"),
        name="nsa_window",
    )(p1, p1, p2)


def _retention_kernel(q_ref, k_ref, v_ref, g_ref, din_ref, qd_ref, kd_ref, cd_ref, o_ref, st_ref):
    @pl.when(pl.program_id(0) == 0)
    def _():
        st_ref[...] = jnp.zeros_like(st_ref)

    B = q_ref.shape[0]
    for b in range(B):
        for h in range(RET_HEADS):
            lanes = slice(h * LANES, (h + 1) * LANES)
            qh = q_ref[b, :, lanes]
            kp = k_ref[b, :, (h // 2) * LANES:(h // 2 + 1) * LANES]
            vh = v_ref[b, :, lanes]
            st = st_ref[b, h]
            inner = _dot_nt(qh, kp) * din_ref[h]
            o = _dot(inner.astype(BF16), vh) + _dot(qh, st.astype(BF16)) * qd_ref[h]
            kd = (kp.astype(F32) * kd_ref[h]).astype(BF16)
            st_ref[b, h] = st * cd_ref[h, 0:1, :] + _dot_tn(kd, vh)
            mu = jnp.mean(o, axis=-1, keepdims=True)
            d = o - mu
            var = jnp.mean(d * d, axis=-1, keepdims=True)
            on = d * lax.rsqrt(var + NORM_EPS)
            gh = g_ref[b, :, lanes].astype(F32)
            o_ref[b, :, lanes] = (gh * _sigmoid(gh) * on).astype(o_ref.dtype)


def retention_consts():
    C = RET_CHUNK
    H = RET_HEADS
    log_g = jnp.log(1.0 - 2.0 ** (-5.0 - jnp.arange(H, dtype=F32)))
    n = jnp.arange(C, dtype=F32)
    diff = n[:, None] - n[None, :]
    causal = diff >= 0
    decay_in = jnp.where(causal[None], jnp.exp(jnp.where(causal, diff, 0.0)[None] * log_g[:, None, None]), 0.0)
    q_decay = jnp.exp((n[None, :] + 1.0) * log_g[:, None])
    k_decay = jnp.exp((C - 1.0 - n)[None, :] * log_g[:, None])
    chunk_decay = jnp.exp(C * log_g)
    qd = jnp.broadcast_to(q_decay[:, :, None], (H, C, LANES))
    kd = jnp.broadcast_to(k_decay[:, :, None], (H, C, LANES))
    cd = jnp.broadcast_to(chunk_decay[:, None, None], (H, 8, LANES))
    return decay_in, qd, kd, cd


def retention(p1, p2, consts, T):
    B = p1.shape[0]
    C = RET_CHUNK
    din, qd, kd, cd = consts
    W = RET_HEADS * LANES
    full = lambda shape: pl.BlockSpec(shape, lambda c: (0,) * len(shape))
    return pl.pallas_call(
        _retention_kernel,
        grid=(T // C,),
        in_specs=[pl.BlockSpec((B, C, W), lambda c: (0, c, P1_RQ // W)),
                  pl.BlockSpec((B, C, W // 2), lambda c: (0, c, P1_RK // (W // 2))),
                  pl.BlockSpec((B, C, W), lambda c: (0, c, P2_RV // W)),
                  pl.BlockSpec((B, C, W), lambda c: (0, c, P2_RG // W)),
                  full(din.shape), full(qd.shape), full(kd.shape), full(cd.shape)],
        out_specs=pl.BlockSpec((B, C, W), lambda c: (0, c, 0)),
        out_shape=jax.ShapeDtypeStruct((B, T, W), BF16),
        scratch_shapes=[pltpu.VMEM((B, RET_HEADS, LANES, LANES), F32)],
        compiler_params=_cparams("arbitrary"),
        name="retention",
    )(p1, p1, p2, p2, din, qd, kd, cd)


def _fox_cum_kernel(f_ref, b_ref, o_ref):
    x = f_ref[...] + b_ref[...]
    ls = jnp.minimum(x, 0.0) - jnp.log1p(jnp.exp(-jnp.abs(x)))
    R = x.shape[0]
    ki = lax.broadcasted_iota(jnp.int32, (LANES, LANES), 0)
    ji = lax.broadcasted_iota(jnp.int32, (LANES, LANES), 1)
    upper = jnp.where(ki <= ji, 1.0, 0.0).astype(BF16)
    hi, mid, lo = _split3(ls)
    rowcum = _dot(hi, upper) + _dot(mid, upper) + _dot(lo, upper)
    tot = jnp.broadcast_to(rowcum[:, LANES - 1:LANES], (R, LANES))
    ri = lax.broadcasted_iota(jnp.int32, (R, R), 0)
    ci = lax.broadcasted_iota(jnp.int32, (R, R), 1)
    lower = jnp.where(ci < ri, 1.0, 0.0).astype(BF16)
    hi, mid, lo = _split3(tot)
    offs = _dot(lower, hi) + _dot(lower, mid) + _dot(lower, lo)
    o_ref[...] = (rowcum + offs) * LOG2E


def fox_cum(f_logit, bias):
    B, H, R, _ = f_logit.shape
    return pl.pallas_call(
        _fox_cum_kernel,
        grid=(B, H),
        in_specs=[pl.BlockSpec((None, None, R, LANES), lambda b, h: (b, h, 0, 0)),
                  pl.BlockSpec((None, 1, LANES), lambda b, h: (h, 0, 0))],
        out_specs=pl.BlockSpec((None, None, R, LANES), lambda b, h: (b, h, 0, 0)),
        out_shape=jax.ShapeDtypeStruct((B, H, R, LANES), F32),
        compiler_params=_cparams("parallel", "parallel"),
        name="fox_cum",
    )(f_logit, bias)


FOX_BIAS_LANES = 3


def _fox_kernel(q_ref, k_ref, v_ref, o_ref, *, tq):
    i = pl.program_id(2)
    tk = tq
    row = lax.broadcasted_iota(jnp.int32, (tq, tk), 0)
    col = lax.broadcasted_iota(jnp.int32, (tq, tk), 1)
    lane = lax.broadcasted_iota(jnp.int32, (tq, LANES), 1)
    ones_lanes = (lane >= HEAD_DIM) & (lane < HEAD_DIM + FOX_BIAS_LANES)
    qs = [jnp.where(ones_lanes, 1.0, q_ref[:, hh * LANES:(hh + 1) * LANES].astype(F32)).astype(BF16)
          for hh in range(2)]

    def step(j, carry, masked):
        start = pl.multiple_of(j * tk, tk)
        out = []
        for hh in range(2):
            m, acc = carry[hh]
            s = _dot_nt(qs[hh], k_ref[hh, pl.ds(start, tk), :])
            if masked:
                s = jnp.where(col <= row, s, NEG_INF)
            m_new = jnp.maximum(m, jnp.max(s, axis=-1, keepdims=True))
            p = jnp.exp2(s - m_new)
            acc = jnp.exp2(m - m_new) * acc + _dot(p.astype(BF16), v_ref[hh, pl.ds(start, tk), :])
            out.append((m_new, acc))
        return tuple(out)

    one = (jnp.full((tq, 1), NEG_INF, F32), jnp.zeros((tq, LANES), F32))
    carry = lax.fori_loop(0, i, functools.partial(step, masked=False), (one, one))
    (_, acc0), (_, acc1) = step(i, carry, True)
    o0 = acc0 / acc0[:, HEAD_DIM:HEAD_DIM + 1]
    o1 = acc1 / acc1[:, HEAD_DIM:HEAD_DIM + 1]
    o_ref[...] = jnp.where(lane < HEAD_DIM, o0, pltpu.roll(o1, HEAD_DIM, 1)).astype(o_ref.dtype)


def fox_augment(p2, cum, T):
    B = p2.shape[0]
    H = FOX_HEADS
    k = p2[:, :, P2_FK:P2_FK + H * HEAD_DIM].reshape(B, T, H, HEAD_DIM).transpose(0, 2, 1, 3)
    v = p2[:, :, P2_FV:P2_FV + H * HEAD_DIM].reshape(B, T, H, HEAD_DIM).transpose(0, 2, 1, 3)
    hi, mid, lo = _split3(-cum.reshape(B, H, T))
    bias = jnp.stack([hi, mid, lo], axis=-1)
    pad = jnp.zeros((B, H, T, LANES - HEAD_DIM - FOX_BIAS_LANES), BF16)
    k_aug = jnp.concatenate([k, bias, pad], axis=-1)
    v_aug = jnp.concatenate([v, jnp.ones((B, H, T, LANES - HEAD_DIM), BF16)], axis=-1)
    return k_aug, v_aug


def fox_attention(p2, k_aug, v_aug, T, *, tq=FOX_TQ):
    B = p2.shape[0]
    HP = FOX_HEADS // 2
    return pl.pallas_call(
        functools.partial(_fox_kernel, tq=tq),
        grid=(B, HP, T // tq),
        in_specs=[pl.BlockSpec((None, tq, 2 * LANES), lambda b, h, i: (b, i, P2_FQ // (2 * LANES) + h)),
                  pl.BlockSpec((None, 2, T, LANES), lambda b, h, i: (b, h, 0, 0)),
                  pl.BlockSpec((None, 2, T, LANES), lambda b, h, i: (b, h, 0, 0))],
        out_specs=pl.BlockSpec((None, tq, LANES), lambda b, h, i: (b, i, h)),
        out_shape=jax.ShapeDtypeStruct((B, T, FOX_HEADS * HEAD_DIM), BF16),
        compiler_params=_cparams("parallel", "parallel", "parallel"),
        name="fox_attention",
    )(p2, k_aug, v_aug)


def _readout_kernel(ocmp_ref, osel_ref, owin_ref, small_ref, oret_ref, ofox_ref, mg_ref, x_ref, g1_ref,
                    ex_ref, wn_ref, wr_ref, wf_ref, wo_ref, o_ref):
    W = NSA_HEADS * LANES
    gs = _sigmoid(small_ref[...].astype(F32)).astype(BF16)
    ge = _dot(gs, ex_ref[...])
    onsa = (ge[:, :W] * ocmp_ref[...].astype(F32) + ge[:, W:2 * W] * osel_ref[...].astype(F32)
            + ge[:, 2 * W:] * owin_ref[...].astype(F32))
    D = D_MODEL
    merged = (_sigmoid(mg_ref[:, :D].astype(F32)) * _dot(onsa.astype(BF16), wn_ref[...])
              + _sigmoid(mg_ref[:, D:2 * D].astype(F32)) * _dot(oret_ref[...], wr_ref[...])
              + _sigmoid(mg_ref[:, 2 * D:].astype(F32)) * _dot(ofox_ref[...], wf_ref[...]))
    y = _dot(merged.astype(BF16), wo_ref[...])
    o_ref[...] = x_ref[...] + g1_ref[...] * y


def readout(o_cmp, o_sel, o_win, p2, o_ret, o_fox, x, mod_l, ex, wn, wr, wf, wo, T, *, tm=512):
    M, D = x.shape
    per_b = T // tm
    W = NSA_HEADS * LANES
    row = lambda width, col=0: pl.BlockSpec((tm, width), lambda i: (i, col))
    full = lambda a: pl.BlockSpec(a.shape, lambda i: (0,) * a.ndim)
    return pl.pallas_call(
        _readout_kernel,
        grid=(M // tm,),
        in_specs=[row(W), row(W), row(W), row(LANES, P2_SMALL // LANES), row(512), row(512),
                  row(3 * D, 0), row(D),
                  pl.BlockSpec((None, None, 1, D), lambda i: (i // per_b, 2, 0, 0)),
                  full(ex), full(wn), full(wr), full(wf), full(wo)],
        out_specs=row(D),
        out_shape=jax.ShapeDtypeStruct((M, D), F32),
        compiler_params=_cparams("parallel"),
        name="mixer_readout",
    )(o_cmp, o_sel, o_win, p2, o_ret, o_fox, p2, x, mod_l, ex, wn, wr, wf, wo)


def nsa_gate_expand():
    ex = np.zeros((LANES, 3 * NSA_HEADS * LANES), np.float32)
    for br in range(3):
        for h in range(NSA_HEADS):
            c0 = br * NSA_HEADS * LANES + h * LANES
            ex[br * NSA_HEADS + h, c0:c0 + LANES] = 1.0
    return jnp.asarray(ex, BF16)


def pad_read_nsa(w):
    D = w.shape[1]
    w = w.reshape(NSA_HEADS, HEAD_DIM, D)
    z = jnp.zeros_like(w)
    g = (np.arange(NSA_HEADS) // NSA_HPG)[:, None, None]
    lo = jnp.where(g == 0, w, z)
    hi = jnp.where(g == 1, w, z)
    return jnp.concatenate([lo, hi], axis=1).reshape(NSA_HEADS * LANES, D).astype(BF16)


def _ffn_kernel(*refs, gated):
    if gated:
        x_ref, nw_ref, sc_ref, sh_ref, g2_ref, gate_ref, w1_ref, w3_ref, w2_ref, o_ref, h_ref, acc_ref = refs
    else:
        x_ref, nw_ref, sc_ref, sh_ref, g2_ref, w1_ref, w3_ref, w2_ref, o_ref, h_ref, acc_ref = refs
    e = pl.program_id(1)
    f = pl.program_id(2)

    @pl.when((e == 0) & (f == 0))
    def _():
        h_ref[...] = _norm_mod(x_ref[...], nw_ref[...], sc_ref[...], sh_ref[...]).astype(BF16)
        acc_ref[...] = jnp.zeros_like(acc_ref)

    h = h_ref[...]
    u = _dot(h, w1_ref[...])
    v = _dot(h, w3_ref[...])
    a = (u * _sigmoid(u) * v).astype(BF16)
    y = _dot(a, w2_ref[...])
    if gated:
        gate = gate_ref[...]
        lane = lax.broadcasted_iota(jnp.int32, gate.shape, 1)
        y = y * jnp.sum(jnp.where(lane == e, gate, 0.0), axis=-1, keepdims=True)
    acc_ref[...] += y

    @pl.when((e == pl.num_programs(1) - 1) & (f == pl.num_programs(2) - 1))
    def _():
        o_ref[...] = x_ref[...] + g2_ref[...] * acc_ref[...]


def ffn(x, mod_l, nw, w1, w3, w2, gate, T, *, tm, tf):
    M, D = x.shape
    E, _, F = w1.shape
    per_b = T // tm
    gated = gate is not None
    modspec = lambda k: pl.BlockSpec((None, None, 1, D), lambda i, e, f: (i // per_b, k, 0, 0))
    in_specs = [pl.BlockSpec((tm, D), lambda i, e, f: (i, 0)),
                pl.BlockSpec((1, D), lambda i, e, f: (0, 0)),
                modspec(4), modspec(3), modspec(5)]
    args = [x, nw, mod_l, mod_l, mod_l]
    if gated:
        in_specs.append(pl.BlockSpec((tm, LANES), lambda i, e, f: (i, 0)))
        args.append(gate)
    in_specs += [pl.BlockSpec((None, D, tf), lambda i, e, f: (e, 0, f)),
                 pl.BlockSpec((None, D, tf), lambda i, e, f: (e, 0, f)),
                 pl.BlockSpec((None, tf, D), lambda i, e, f: (e, f, 0))]
    args += [w1, w3, w2]
    return pl.pallas_call(
        functools.partial(_ffn_kernel, gated=gated),
        grid=(M // tm, E, F // tf),
        in_specs=in_specs,
        out_specs=pl.BlockSpec((tm, D), lambda i, e, f: (i, 0)),
        out_shape=jax.ShapeDtypeStruct((M, D), F32),
        scratch_shapes=[pltpu.VMEM((tm, D), BF16), pltpu.VMEM((tm, D), F32)],
        compiler_params=_cparams("parallel", "arbitrary", "arbitrary"),
        name="ffn_gated" if gated else "ffn_dense",
    )(*args)


MOE_TC = 512
MOE_TS = 512


def _router_kernel(x_ref, nw_ref, sc_ref, sh_ref, wh_ref, wl_ref,
                   h_ref, gate_ref, rank_ref, rank_t_ref, cstart_ref, cnt_ref, carry_ref, carry_t_ref):
    @pl.when(pl.program_id(0) == 0)
    def _():
        carry_ref[...] = jnp.zeros_like(carry_ref)
        carry_t_ref[...] = jnp.zeros_like(carry_t_ref)

    h = _norm_mod(x_ref[...], nw_ref[...], sc_ref[...], sh_ref[...])
    hh = h.astype(BF16)
    h_ref[...] = hh
    hl = (h - hh.astype(F32)).astype(BF16)
    logits = _dot(hh, wh_ref[...]) + (_dot(hl, wh_ref[...]) + _dot(hh, wl_ref[...]))
    tm = logits.shape[0]
    lane = lax.broadcasted_iota(jnp.int32, logits.shape, 1)
    logits = jnp.where(lane < N_EXPERTS, logits, REMOVED)
    lane_f = lane.astype(F32)
    v1 = jnp.max(logits, axis=-1, keepdims=True)
    i1 = jnp.min(jnp.where(logits == v1, lane_f, float(LANES)), axis=-1, keepdims=True)
    rest = jnp.where(lane_f == i1, REMOVED, logits)
    v2 = jnp.max(rest, axis=-1, keepdims=True)
    i2 = jnp.min(jnp.where(rest == v2, lane_f, float(LANES)), axis=-1, keepdims=True)
    e2 = jnp.exp(v2 - v1)
    w1 = 1.0 / (1.0 + e2)
    w2 = e2 / (1.0 + e2)
    gate_ref[...] = jnp.where(lane_f == i1, w1, jnp.where(lane_f == i2, w2, 0.0))

    sel = jnp.where((lane_f == i1) | (lane_f == i2), 1.0, 0.0)
    sel_t = sel.T[0:N_EXPERTS]
    ri = lax.broadcasted_iota(jnp.int32, (tm, tm), 0)
    ci = lax.broadcasted_iota(jnp.int32, (tm, tm), 1)
    before = jnp.where(ci < ri, 1.0, 0.0).astype(BF16)
    after = jnp.where(ri < ci, 1.0, 0.0).astype(BF16)
    carry = carry_ref[0:1, :]
    carry_t = carry_t_ref[:, 0:1]
    rank = _dot(before, sel.astype(BF16)) + carry
    rank_t = _dot(sel_t.astype(BF16), after) + carry_t
    rank_ref[...] = jnp.where(sel > 0.0, rank, -1.0)
    rank_t_ref[...] = jnp.where(sel_t > 0.0, rank_t, -1.0)
    cstart_ref[0] = carry_ref[...]
    carry_ref[...] = carry_ref[...] + jnp.sum(sel, axis=0, keepdims=True)
    carry_t_ref[...] = carry_t_ref[...] + jnp.sum(sel_t, axis=1, keepdims=True)
    cnt_ref[...] = carry_ref[...]


def router(x, mod_l, nw, w_router, T):
    M, D = x.shape
    tm = MOE_TC
    per_b = T // tm
    wp = jnp.zeros((D, LANES), F32).at[:, :N_EXPERTS].set(w_router)
    wh = wp.astype(BF16)
    wl = (wp - wh.astype(F32)).astype(BF16)
    nc = M // tm
    return pl.pallas_call(
        _router_kernel,
        grid=(nc,),
        in_specs=[pl.BlockSpec((tm, D), lambda i: (i, 0)),
                  pl.BlockSpec((1, D), lambda i: (0, 0))]
        + _mod_specs(T, tm, 4, 3, 1)
        + [pl.BlockSpec((D, LANES), lambda i: (0, 0)),
           pl.BlockSpec((D, LANES), lambda i: (0, 0))],
        out_specs=[pl.BlockSpec((tm, D), lambda i: (i, 0)),
                   pl.BlockSpec((tm, LANES), lambda i: (i, 0)),
                   pl.BlockSpec((tm, LANES), lambda i: (i, 0)),
                   pl.BlockSpec((N_EXPERTS, tm), lambda i: (0, i)),
                   pl.BlockSpec((1, 8, LANES), lambda i: (i, 0, 0)),
                   pl.BlockSpec((8, LANES), lambda i: (0, 0))],
        out_shape=[jax.ShapeDtypeStruct((M, D), BF16),
                   jax.ShapeDtypeStruct((M, LANES), F32),
                   jax.ShapeDtypeStruct((M, LANES), F32),
                   jax.ShapeDtypeStruct((N_EXPERTS, M), F32),
                   jax.ShapeDtypeStruct((nc, 8, LANES), F32),
                   jax.ShapeDtypeStruct((8, LANES), F32)],
        scratch_shapes=[pltpu.VMEM((8, LANES), F32), pltpu.VMEM((8, LANES), F32)],
        compiler_params=_cparams("arbitrary"),
        name="moe_router",
    )(x, nw, mod_l, mod_l, wh, wl)


def moe_schedule(counts, cstart, M):
    ts, tc = MOE_TS, MOE_TC
    nc = M // tc
    rt = (2 * M) // ts + N_EXPERTS
    smax = rt + N_EXPERTS * nc
    i32 = jnp.int32
    cnt = counts.astype(i32)
    cs = cstart.astype(i32)
    ce = jnp.concatenate([cs[1:], cnt[None]], axis=0)
    ntile = (cnt + ts - 1) // ts
    tile_end = jnp.cumsum(ntile)
    tile_off = tile_end - ntile
    total_tiles = tile_end[-1]
    r = jnp.arange(rt, dtype=i32)
    e_r = jnp.minimum(jnp.searchsorted(tile_end, r, side="right"), N_EXPERTS - 1).astype(i32)
    valid_r = r < total_tiles
    k_r = r - tile_off[e_r]
    lo = k_r * ts
    hi = jnp.minimum(lo + ts, cnt[e_r])
    ov = (ce[:, e_r] > lo[None]) & (cs[:, e_r] < hi[None]) & valid_r[None]
    c_lo = jnp.argmax(ov, axis=0).astype(i32)
    n_c = jnp.sum(ov, axis=0).astype(i32)
    pend = jnp.cumsum(n_c)
    pstart = pend - n_c
    n_pairs = pend[-1]
    s = jnp.minimum(jnp.arange(smax, dtype=i32), n_pairs - 1)
    g_r = jnp.minimum(jnp.searchsorted(pend, s, side="right"), rt - 1).astype(i32)
    g_c = c_lo[g_r] + (s - pstart[g_r])
    g_valid = jnp.arange(smax, dtype=i32) < n_pairs
    g_first = (s == pstart[g_r]) & g_valid
    gather = dict(r=g_r, c=g_c, e=e_r[g_r], base=k_r[g_r] * ts, first=g_first.astype(i32), valid=g_valid.astype(i32))

    row_off = tile_off * ts
    has = ce > cs
    r_lo = (row_off[None] + cs) // ts
    r_hi = (row_off[None] + ce - 1) // ts
    n_q = jnp.where(has, r_hi - r_lo + 1, 0).reshape(-1).astype(i32)
    qend = jnp.cumsum(n_q)
    qstart = qend - n_q
    n_pairs2 = qend[-1]
    s2 = jnp.minimum(jnp.arange(smax, dtype=i32), n_pairs2 - 1)
    q = jnp.minimum(jnp.searchsorted(qend, s2, side="right"), nc * N_EXPERTS - 1).astype(i32)
    c_c = q // N_EXPERTS
    c_e = q % N_EXPERTS
    c_r = r_lo.reshape(-1)[q] + (s2 - qstart[q])
    c_valid = jnp.arange(smax, dtype=i32) < n_pairs2
    tok_first = qstart.reshape(nc, N_EXPERTS)[:, 0]
    tok_last = qend.reshape(nc, N_EXPERTS)[:, -1] - 1
    c_first = (s2 == tok_first[c_c]) & c_valid
    c_last = (s2 == tok_last[c_c]) & c_valid
    combine = dict(r=c_r.astype(i32), c=c_c, e=c_e, base=(c_r * ts - row_off[c_e]).astype(i32),
                   first=c_first.astype(i32), last=c_last.astype(i32), valid=c_valid.astype(i32))
    tiles = dict(e=e_r, total=total_tiles.reshape(1).astype(i32))
    return tiles, gather, combine, rt, smax


def _moe_gather_kernel(r_s, c_s, e_s, base_s, first_s, valid_s, rank_t_ref, h_ref, o_ref):
    s = pl.program_id(0)

    @pl.when(first_s[s] == 1)
    def _():
        o_ref[...] = jnp.zeros_like(o_ref)

    @pl.when(valid_s[s] == 1)
    def _():
        ts = o_ref.shape[0]
        tc = h_ref.shape[0]
        rk = rank_t_ref[pl.ds(e_s[s], 1), :] - base_s[s].astype(F32)
        row = lax.broadcasted_iota(jnp.int32, (ts, tc), 0).astype(F32)
        onehot = jnp.where(rk == row, 1.0, 0.0).astype(BF16)
        o_ref[...] += _dot(onehot, h_ref[...]).astype(o_ref.dtype)


def moe_gather(h, rank_t, g, rt, smax):
    M, D = h.shape
    ts, tc = MOE_TS, MOE_TC
    return pl.pallas_call(
        _moe_gather_kernel,
        grid_spec=pltpu.PrefetchScalarGridSpec(
            num_scalar_prefetch=6,
            grid=(smax,),
            in_specs=[pl.BlockSpec((N_EXPERTS, tc), lambda s, r, c, *_: (0, c[s])),
                      pl.BlockSpec((tc, D), lambda s, r, c, *_: (c[s], 0))],
            out_specs=pl.BlockSpec((ts, D), lambda s, r, c, *_: (r[s], 0)),
        ),
        out_shape=jax.ShapeDtypeStruct((rt * ts, D), BF16),
        compiler_params=_cparams("arbitrary"),
        name="moe_gather",
    )(g["r"], g["c"], g["e"], g["base"], g["first"], g["valid"], rank_t, h)


def _moe_up_kernel(e_r, total, x_ref, w1_ref, w3_ref, o_ref):
    @pl.when(pl.program_id(1) < total[0])
    def _():
        x = x_ref[...]
        u = _dot(x, w1_ref[...])
        v = _dot(x, w3_ref[...])
        o_ref[...] = (u * _sigmoid(u) * v).astype(o_ref.dtype)


def moe_up(xs, w1, w3, tiles, rt, *, tf=896):
    R, D = xs.shape
    ts = MOE_TS
    F = w1.shape[-1]
    live = lambda r, total: jnp.minimum(r, total[0] - 1)
    return pl.pallas_call(
        _moe_up_kernel,
        grid_spec=pltpu.PrefetchScalarGridSpec(
            num_scalar_prefetch=2,
            grid=(F // tf, rt),
            in_specs=[pl.BlockSpec((ts, D), lambda n, r, e, total: (live(r, total), 0)),
                      pl.BlockSpec((None, D, tf), lambda n, r, e, total: (e[live(r, total)], 0, n)),
                      pl.BlockSpec((None, D, tf), lambda n, r, e, total: (e[live(r, total)], 0, n))],
            out_specs=pl.BlockSpec((ts, tf), lambda n, r, e, total: (r, n)),
        ),
        out_shape=jax.ShapeDtypeStruct((R, F), BF16),
        compiler_params=_cparams("arbitrary", "arbitrary"),
        name="moe_up",
    )(tiles["e"], tiles["total"], xs, w1, w3)


def _moe_down_kernel(e_r, total, a_ref, w2_ref, o_ref):
    @pl.when(pl.program_id(0) < total[0])
    def _():
        o_ref[...] = _dot(a_ref[...], w2_ref[...]).astype(o_ref.dtype)


def moe_down(a, w2, tiles, rt):
    R, F = a.shape
    ts = MOE_TS
    D = w2.shape[-1]
    live = lambda r, total: jnp.minimum(r, total[0] - 1)
    return pl.pallas_call(
        _moe_down_kernel,
        grid_spec=pltpu.PrefetchScalarGridSpec(
            num_scalar_prefetch=2,
            grid=(rt,),
            in_specs=[pl.BlockSpec((ts, F), lambda r, e, total: (live(r, total), 0)),
                      pl.BlockSpec((None, F, D), lambda r, e, total: (e[live(r, total)], 0, 0))],
            out_specs=pl.BlockSpec((ts, D), lambda r, e, total: (r, 0)),
        ),
        out_shape=jax.ShapeDtypeStruct((R, D), BF16),
        compiler_params=_cparams("arbitrary"),
        name="moe_down",
    )(tiles["e"], tiles["total"], a, w2)


def _moe_combine_kernel(r_s, c_s, e_s, base_s, first_s, last_s, valid_s,
                        rank_ref, gate_ref, y_ref, x_ref, g2_ref, o_ref, acc_ref):
    s = pl.program_id(0)

    @pl.when(first_s[s] == 1)
    def _():
        acc_ref[...] = jnp.zeros_like(acc_ref)

    @pl.when(valid_s[s] == 1)
    def _():
        tc = rank_ref.shape[0]
        ts = y_ref.shape[0]
        lane = lax.broadcasted_iota(jnp.int32, (tc, LANES), 1)
        mine = lane == e_s[s]
        rank_col = jnp.sum(jnp.where(mine, rank_ref[...], 0.0), axis=-1, keepdims=True) - base_s[s].astype(F32)
        gate_col = jnp.sum(jnp.where(mine, gate_ref[...], 0.0), axis=-1, keepdims=True)
        col = lax.broadcasted_iota(jnp.int32, (tc, ts), 1).astype(F32)
        onehot = jnp.where(rank_col == col, 1.0, 0.0).astype(BF16)
        acc_ref[...] += gate_col * _dot(onehot, y_ref[...])

    @pl.when(last_s[s] == 1)
    def _():
        o_ref[...] = x_ref[...] + g2_ref[...] * acc_ref[...]


def moe_combine(y, rank, gate, x, mod_l, cb, T, smax):
    M, D = x.shape
    ts, tc = MOE_TS, MOE_TC
    per_b = T // tc
    tok = lambda width: pl.BlockSpec((tc, width), lambda s, r, c, *_: (c[s], 0))
    return pl.pallas_call(
        _moe_combine_kernel,
        grid_spec=pltpu.PrefetchScalarGridSpec(
            num_scalar_prefetch=7,
            grid=(smax,),
            in_specs=[tok(LANES), tok(LANES),
                      pl.BlockSpec((ts, D), lambda s, r, c, *_: (r[s], 0)),
                      tok(D),
                      pl.BlockSpec((None, None, 1, D), lambda s, r, c, *_: (c[s] // per_b, 5, 0, 0))],
            out_specs=tok(D),
            scratch_shapes=[pltpu.VMEM((tc, D), F32)],
        ),
        out_shape=jax.ShapeDtypeStruct((M, D), F32),
        compiler_params=_cparams("arbitrary"),
        name="moe_combine",
    )(cb["r"], cb["c"], cb["e"], cb["base"], cb["first"], cb["last"], cb["valid"], rank, gate, y, x, mod_l)


def moe_ffn(x, mod_l, nw, w_router, w1, w3, w2, T):
    M = x.shape[0]
    h, gate, rank, rank_t, cstart, cnt = router(x, mod_l, nw, w_router, T)
    tiles, g, cb, rt, smax = moe_schedule(cnt[0, :N_EXPERTS], cstart[:, 0, :N_EXPERTS], M)
    xs = moe_gather(h, rank_t, g, rt, smax)
    a = moe_up(xs, w1, w3, tiles, rt)
    y = moe_down(a, w2, tiles, rt)
    return moe_combine(y, rank, gate, x, mod_l, cb, T, smax)


def _final_norm_kernel(x_ref, w_ref, o_ref):
    x = x_ref[...]
    ms = jnp.mean(x * x, axis=-1, keepdims=True)
    o_ref[...] = x * lax.rsqrt(ms + NORM_EPS) * w_ref[...]


def final_norm(x, w, *, tm=1024):
    M, D = x.shape
    return pl.pallas_call(
        _final_norm_kernel,
        grid=(M // tm,),
        in_specs=[pl.BlockSpec((tm, D), lambda i: (i, 0)), pl.BlockSpec((1, D), lambda i: (0, 0))],
        out_specs=pl.BlockSpec((tm, D), lambda i: (i, 0)),
        out_shape=jax.ShapeDtypeStruct((M, D), F32),
        compiler_params=_cparams("parallel"),
        name="final_norm",
    )(x, w)


def nsa_constants(T):
    n_sel = T // SEL_LEN
    nsp = max(LANES, n_sel)
    ncp = T // CMP_STRIDE
    cmp_start = np.arange(ncp) * CMP_STRIDE
    sel_start = np.arange(nsp) * SEL_LEN
    ov = ((cmp_start[:, None] < sel_start[None, :] + SEL_LEN)
          & (cmp_start[:, None] + CMP_LEN > sel_start[None, :]))
    ov[(T - CMP_LEN) // CMP_STRIDE + 1:] = False
    ov[:, n_sel:] = False
    et_mat = ((np.arange(T)[:, None] // SEL_LEN) == np.arange(nsp)[None, :]) * SEL_BONUS
    return jnp.asarray(ov.T, BF16), jnp.asarray(et_mat, BF16)


def token_mixing(x, mod_l, lw, consts, B, T):
    M = B * T
    cos_t, sin_t, ov_t, e_mat, ret_consts, ex = consts
    p1 = proj_rope(x, mod_l, lw["norm_mix"], lw["w1"], cos_t, sin_t, p1_scales(), T).reshape(B, T, P1_COLS)
    p2 = proj_plain(x, mod_l, lw["norm_mix"], lw["w2"], T).reshape(B, T, P2_COLS)

    def group_rows(a):
        return a.reshape(B, T, NSA_GROUPS, HEAD_DIM).transpose(0, 2, 1, 3).reshape(
            B, NSA_GROUPS, T // CMP_STRIDE, CMP_STRIDE * HEAD_DIM)

    xr = jnp.stack([group_rows(p1[:, :, P1_NKC:P1_NKC + LANES]), group_rows(p2[:, :, P2_NVC:P2_NVC + LANES])])
    cmp_out = compress(xr, lw["cmp_pe"], lw["cmp_w1"], lw["cmp_w2"])
    cmp_out = cmp_out.transpose(0, 1, 3, 2, 4).reshape(2, B, T // CMP_STRIDE, LANES)
    o_cmp, sel = nsa_cmp_select(p1, cmp_out[0], cmp_out[1], ov_t, T)
    o_sel = nsa_selected(p1, nsa_value_augment(p2[:, :, P2_NVS:P2_NVS + LANES]), sel, e_mat, T)
    o_win = nsa_window(p1, p2, T)

    o_ret = retention(p1, p2, ret_consts, T)

    ff = p2[:, :, P2_SMALL + 3 * NSA_HEADS:P2_SMALL + 3 * NSA_HEADS + FOX_HEADS].astype(F32)
    ff = ff.transpose(0, 2, 1).reshape(B, FOX_HEADS, T // LANES, LANES)
    k_aug, v_aug = fox_augment(p2, fox_cum(ff, lw["fox_bias"]), T)
    o_fox = fox_attention(p2, k_aug, v_aug, T)

    return readout(o_cmp.reshape(M, -1), o_sel.reshape(M, -1), o_win.reshape(M, -1), p2.reshape(M, P2_COLS),
                   o_ret.reshape(M, -1), o_fox.reshape(M, -1), x, mod_l, ex,
                   lw["wn"], lw["wr"], lw["wf"], lw["wo"], T)


def layer_weights(l, norm_mix, w_in, cmp_k_pe, cmp_k_w1, cmp_k_w2, cmp_v_pe, cmp_v_w1, cmp_v_w2, fox_f_bias,
                  w_read_nsa, w_read_ret, w_read_fox, w_out):
    w1, w2 = split_w_in(w_in[l])
    pe = jnp.stack([cmp_k_pe[l].reshape(1, -1), cmp_v_pe[l].reshape(1, -1)])
    pe = jnp.broadcast_to(pe, (2, 8, pe.shape[-1])).astype(BF16)
    return {
        "norm_mix": norm_mix[l].reshape(1, -1),
        "w1": w1, "w2": w2,
        "cmp_pe": pe,
        "cmp_w1": jnp.stack([cmp_k_w1[l], cmp_v_w1[l]]).astype(BF16),
        "cmp_w2": jnp.stack([cmp_k_w2[l], cmp_v_w2[l]]).astype(BF16),
        "fox_bias": jnp.broadcast_to(fox_f_bias[l][:, None, None], (FOX_HEADS, 1, LANES)),
        "wn": pad_read_nsa(w_read_nsa[l]),
        "wr": w_read_ret[l].astype(BF16),
        "wf": w_read_fox[l].astype(BF16),
        "wo": w_out[l].astype(BF16),
    }


def kernel(x, c, ada_w, ada_b, norm_mix, norm_ffn, w_in, cmp_k_pe, cmp_k_w1, cmp_k_w2, cmp_v_pe, cmp_v_w1,
           cmp_v_w2, fox_f_bias, w_read_nsa, w_read_ret, w_read_fox, w_out, ffn_w1, ffn_w3, ffn_w2, router_w,
           moe_w1, moe_w3, moe_w2, final_norm_w):
    B, T, D = x.shape
    M = B * T
    depth = ada_w.shape[0]
    mod = modulation(c, ada_w, ada_b)
    cos_t, sin_t = rope_tables(T)
    ov_t, e_mat = nsa_constants(T)
    consts = (cos_t, sin_t, ov_t, e_mat, retention_consts(), nsa_gate_expand())
    xs = x.reshape(M, D)
    for l in range(depth):
        lw = layer_weights(l, norm_mix, w_in, cmp_k_pe, cmp_k_w1, cmp_k_w2, cmp_v_pe, cmp_v_w1, cmp_v_w2,
                           fox_f_bias, w_read_nsa, w_read_ret, w_read_fox, w_out)
        xs = token_mixing(xs, mod[l], lw, consts, B, T)
        nf = norm_ffn[l].reshape(1, D)
        if l % 2 == 0:
            k = l // 2
            xs = ffn(xs, mod[l], nf, ffn_w1[k][None].astype(BF16), ffn_w3[k][None].astype(BF16),
                     ffn_w2[k][None].astype(BF16), None, T, tm=512, tf=D_FF // 2)
        else:
            k = l // 2
            xs = moe_ffn(xs, mod[l], nf, router_w[k], moe_w1[k].astype(BF16), moe_w3[k].astype(BF16),
                         moe_w2[k].astype(BF16), T)
    return final_norm(xs, final_norm_w.reshape(1, D)).reshape(B, T, D)
```

```python
import functools
import math

import jax
import jax.numpy as jnp
import numpy as np
from jax import lax
from jax.experimental import pallas as pl
from jax.experimental.pallas import tpu as pltpu

F32 = jnp.float32
BF16 = jnp.bfloat16

D_MODEL = 1024
DEPTH = 2
HEAD_DIM = 64
ROPE_THETA = 10000.0
NORM_EPS = 1e-6
NEG_INF = -1e30
REMOVED = -3e38

NSA_HEADS = 8
NSA_GROUPS = 2
NSA_HPG = NSA_HEADS // NSA_GROUPS
CMP_LEN = 32
CMP_STRIDE = 16
CMP_HIDDEN = 256
SEL_LEN = 64
SEL_TOPN = 16
WINDOW = 512
FORCE_SCORE = 1e4
NSA_QBLOCK = 128

RET_HEADS = 4
RET_QK_DIM = 64
RET_V_DIM = 128
RET_CHUNK = 128

FOX_HEADS = 8
FOX_TQ = 1024
LOG2E = 1.4426950408889634

D_FF = 2816
N_EXPERTS = 8
D_FF_EXPERT = 3584

LANES = 128
VMEM_LIMIT = 56 * 1024 * 1024

P1_NQ = 0
P1_RQ = 1024
P1_RK = 1536
P1_NKC = 1792
P1_NKS = 1920
P1_NKW = 2048
P1_COLS = 2176
P2_MG = 0
P2_RV = 3072
P2_RG = 3584
P2_FQ = 4096
P2_FK = 5120
P2_FV = 5632
P2_NVC = 6144
P2_NVS = 6272
P2_NVW = 6400
P2_SMALL = 6528
P2_COLS = 6656


def _cparams(*sem):
    return pltpu.CompilerParams(dimension_semantics=tuple(sem), vmem_limit_bytes=VMEM_LIMIT)


def _sigmoid(x):
    return 1.0 / (1.0 + jnp.exp(-x))


def _dot(a, b):
    return jnp.dot(a, b, preferred_element_type=F32)


def _dot_nt(a, b):
    return lax.dot_general(a, b, (((1,), (1,)), ((), ())), preferred_element_type=F32)


def _dot_tn(a, b):
    return lax.dot_general(a, b, (((0,), (0,)), ((), ())), preferred_element_type=F32)


def _split3(x):
    hi = x.astype(BF16)
    r1 = x - hi.astype(F32)
    mid = r1.astype(BF16)
    lo = (r1 - mid.astype(F32)).astype(BF16)
    return hi, mid, lo


def _norm_mod(x, nw, sc, sh):
    ms = jnp.mean(x * x, axis=-1, keepdims=True)
    y = x * lax.rsqrt(ms + NORM_EPS) * nw
    return y * (1.0 + sc) + sh


def _mod_kernel(c_ref, w_ref, b_ref, o_ref):
    c = c_ref[...]
    s = c * _sigmoid(c)
    o_ref[0] = _dot(s.astype(BF16), w_ref[0].astype(BF16)) + b_ref[0]


def modulation(c, ada_w, ada_b):
    B, D = c.shape
    depth = ada_w.shape[0]
    rows = 8
    c_pad = jnp.zeros((rows, D), F32).at[:B].set(c)
    out = pl.pallas_call(
        _mod_kernel,
        grid=(depth, 6),
        in_specs=[pl.BlockSpec((rows, D), lambda l, j: (0, 0)),
                  pl.BlockSpec((1, D, D), lambda l, j: (l, 0, j)),
                  pl.BlockSpec((1, 1, D), lambda l, j: (l, 0, j))],
        out_specs=pl.BlockSpec((1, rows, D), lambda l, j: (l, 0, j)),
        out_shape=jax.ShapeDtypeStruct((depth, rows, 6 * D), F32),
        compiler_params=_cparams("parallel", "parallel"),
        name="modulation",
    )(c_pad, ada_w, ada_b.reshape(depth, 1, 6 * D))
    return out[:, :B].reshape(depth, B, 6, 1, D)


def _proj_plain_kernel(x_ref, nw_ref, sc_ref, sh_ref, w_ref, o_ref, h_ref):
    @pl.when(pl.program_id(1) == 0)
    def _():
        h_ref[...] = _norm_mod(x_ref[...], nw_ref[...], sc_ref[...], sh_ref[...]).astype(BF16)

    o_ref[...] = _dot(h_ref[...], w_ref[...]).astype(o_ref.dtype)


def _proj_rope_kernel(x_ref, nw_ref, sc_ref, sh_ref, w_ref, cos_ref, sin_ref, o_ref, *, scales):
    h = _norm_mod(x_ref[...], nw_ref[...], sc_ref[...], sh_ref[...]).astype(BF16)
    y = _dot(h, w_ref[...])
    cos = cos_ref[...]
    sin = sin_ref[...]
    lane = lax.broadcasted_iota(jnp.int32, cos.shape, 1)
    first_half = (lane % HEAD_DIM) < (HEAD_DIM // 2)
    for g, scale in enumerate(scales):
        yg = y[:, g * LANES:(g + 1) * LANES]
        rot = jnp.where(first_half, pltpu.roll(yg, LANES - HEAD_DIM // 2, 1),
                        pltpu.roll(yg, HEAD_DIM // 2, 1))
        r = yg * cos + rot * sin
        if scale != 1.0:
            r = r * scale
        o_ref[:, g * LANES:(g + 1) * LANES] = r.astype(o_ref.dtype)


def _mod_specs(T, tm, sc_idx, sh_idx, nargs):
    per_b = T // tm
    if nargs == 1:
        return [pl.BlockSpec((None, None, 1, D_MODEL), lambda i: (i // per_b, sc_idx, 0, 0)),
                pl.BlockSpec((None, None, 1, D_MODEL), lambda i: (i // per_b, sh_idx, 0, 0))]
    return [pl.BlockSpec((None, None, 1, D_MODEL), lambda i, j: (i // per_b, sc_idx, 0, 0)),
            pl.BlockSpec((None, None, 1, D_MODEL), lambda i, j: (i // per_b, sh_idx, 0, 0))]


def proj_plain(x, mod_l, nw, w, T, *, tm=1024, tn=512):
    M, D = x.shape
    N = w.shape[1]
    return pl.pallas_call(
        _proj_plain_kernel,
        grid=(M // tm, N // tn),
        in_specs=[pl.BlockSpec((tm, D), lambda i, j: (i, 0)),
                  pl.BlockSpec((1, D), lambda i, j: (0, 0))]
        + _mod_specs(T, tm, 1, 0, 2)
        + [pl.BlockSpec((D, tn), lambda i, j: (0, j))],
        out_specs=pl.BlockSpec((tm, tn), lambda i, j: (i, j)),
        out_shape=jax.ShapeDtypeStruct((M, N), BF16),
        scratch_shapes=[pltpu.VMEM((tm, D), BF16)],
        compiler_params=_cparams("parallel", "arbitrary"),
        name="proj_plain",
    )(x, nw, mod_l, mod_l, w)


def proj_rope(x, mod_l, nw, w, cos, sin, scales, T, *, tm=512):
    M, D = x.shape
    N = w.shape[1]
    per_b = T // tm
    return pl.pallas_call(
        functools.partial(_proj_rope_kernel, scales=scales),
        grid=(M // tm,),
        in_specs=[pl.BlockSpec((tm, D), lambda i: (i, 0)),
                  pl.BlockSpec((1, D), lambda i: (0, 0))]
        + _mod_specs(T, tm, 1, 0, 1)
        + [pl.BlockSpec((D, N), lambda i: (0, 0)),
           pl.BlockSpec((tm, LANES), lambda i: (i % per_b, 0)),
           pl.BlockSpec((tm, LANES), lambda i: (i % per_b, 0))],
        out_specs=pl.BlockSpec((tm, N), lambda i: (i, 0)),
        out_shape=jax.ShapeDtypeStruct((M, N), BF16),
        compiler_params=_cparams("parallel"),
        name="proj_rope",
    )(x, nw, mod_l, mod_l, w, cos, sin)


def rope_tables(T):
    d = HEAD_DIM
    pos = jnp.arange(T, dtype=F32)
    inv = ROPE_THETA ** (-jnp.arange(0, d, 2, dtype=F32) / d)
    ang = pos[:, None] * inv[None, :]
    cos = jnp.cos(ang)
    sin = jnp.sin(ang)
    cos_t = jnp.concatenate([cos, cos, cos, cos], axis=-1)
    sin_t = jnp.concatenate([-sin, sin, -sin, sin], axis=-1)
    return cos_t, sin_t


def _pad_heads(w, n_heads, half_of_head):
    D = w.shape[0]
    w = w.reshape(D, n_heads, HEAD_DIM)
    z = jnp.zeros_like(w)
    halves = np.array([half_of_head(h) for h in range(n_heads)])
    lo = jnp.where(halves[None, :, None] == 0, w, z)
    hi = jnp.where(halves[None, :, None] == 1, w, z)
    return jnp.concatenate([lo, hi], axis=-1).reshape(D, n_heads * LANES)


def split_w_in(w_in):
    sizes = [512, 128, 128, 128, 128, 128, 128, 24, 256, 256, 512, 512, 512, 512, 512, 8, 3072]
    offs = np.cumsum([0] + sizes)
    (nq, nkc, nvc, nks, nvs, nkw, nvw, ngate, rq, rk, rv, rg, fq, fk, fv, ff, mg) = [
        w_in[:, offs[i]:offs[i + 1]] for i in range(len(sizes))]
    D = w_in.shape[0]
    nq_p = _pad_heads(nq, NSA_HEADS, lambda h: h // NSA_HPG)
    rq_p = _pad_heads(rq, RET_HEADS, lambda h: h % 2)
    fq_p = _pad_heads(fq, FOX_HEADS, lambda h: 0) * (HEAD_DIM ** -0.5 * LOG2E)
    small = jnp.concatenate([ngate, ff, jnp.zeros((D, LANES - 32), w_in.dtype)], axis=-1)
    w1 = jnp.concatenate([nq_p, rq_p, rk, nkc, nks, nkw], axis=-1).astype(BF16)
    w2 = jnp.concatenate([mg, rv, rg, fq_p, fk, fv, nvc, nvs, nvw, small], axis=-1).astype(BF16)
    assert w1.shape[1] == P1_COLS and w2.shape[1] == P2_COLS
    return w1, w2


def p1_scales():
    s = [1.0] * (P1_COLS // LANES)
    for g in range(P1_NQ // LANES, P1_RQ // LANES):
        s[g] = HEAD_DIM ** -0.5 * LOG2E
    for g in range(P1_RK // LANES, P1_NKC // LANES):
        s[g] = RET_QK_DIM ** -0.5
    return tuple(s)


def _compress_kernel(x_ref, pe_ref, w1_ref, w2_ref, o_ref):
    r = x_ref[...]
    half = r.shape[1]
    w1 = w1_ref[...]
    a = _dot(r, w1[:half])
    b = _dot(r, w1[half:])
    pe = _dot(pe_ref[...], w1)[0:1]
    n = a.shape[0]
    hid = a + pltpu.roll(b, n - 1, 0) + pe
    hid = hid * _sigmoid(hid)
    o_ref[...] = _dot(hid.astype(BF16), w2_ref[...]).astype(o_ref.dtype)


def compress(xr, pe, w1, w2):
    _, B, G, R, W = xr.shape
    H = w1.shape[-1]
    return pl.pallas_call(
        _compress_kernel,
        grid=(2, B, G),
        in_specs=[pl.BlockSpec((None, None, None, R, W), lambda s, b, g: (s, b, g, 0, 0)),
                  pl.BlockSpec((None, 8, 2 * W), lambda s, b, g: (s, 0, 0)),
                  pl.BlockSpec((None, 2 * W, H), lambda s, b, g: (s, 0, 0)),
                  pl.BlockSpec((None, H, HEAD_DIM), lambda s, b, g: (s, 0, 0))],
        out_specs=pl.BlockSpec((None, None, None, R, HEAD_DIM), lambda s, b, g: (s, b, g, 0, 0)),
        out_shape=jax.ShapeDtypeStruct((2, B, G, R, HEAD_DIM), BF16),
        compiler_params=_cparams("parallel", "parallel", "parallel"),
        name="nsa_compress",
    )(xr, pe, w1, w2)


def _stack_heads(q_ref, g):
    return jnp.concatenate(
        [q_ref[:, (NSA_HPG * g + hh) * LANES:(NSA_HPG * g + hh + 1) * LANES] for hh in range(NSA_HPG)],
        axis=0)


def _store_heads(o_ref, g, o, tq):
    for hh in range(NSA_HPG):
        h = NSA_HPG * g + hh
        o_ref[:, h * LANES:(h + 1) * LANES] = o[hh * tq:(hh + 1) * tq].astype(o_ref.dtype)


def _nsa_cmp_kernel(q_ref, kc_ref, vc_ref, ov_ref, o_ref, m_ref, *, tq, n_sel, top_n):
    t0 = pl.program_id(1) * tq
    kc = kc_ref[...]
    vc = vc_ref[...]
    ncp = kc.shape[0]
    nsp = ov_ref.shape[0]
    rows = NSA_HPG * tq
    n_idx = lax.broadcasted_iota(jnp.int32, (rows, ncp), 1)
    t_idx = t0 + lax.broadcasted_iota(jnp.int32, (rows, ncp), 0) % tq
    valid = (n_idx * CMP_STRIDE + (CMP_LEN - 1)) <= t_idx
    j_idx = lax.broadcasted_iota(jnp.int32, (nsp, tq), 0)
    cur = (t0 + lax.broadcasted_iota(jnp.int32, (nsp, tq), 1)) // SEL_LEN
    forced = (j_idx == 0) | (j_idx == cur) | (j_idx == cur - 1)
    j_f = j_idx.astype(F32)
    for g in range(NSA_GROUPS):
        q = _stack_heads(q_ref, g)
        s = jnp.where(valid, _dot_nt(q, kc), NEG_INF)
        m = jnp.max(s, axis=-1, keepdims=True)
        e = jnp.where(valid, jnp.exp2(s - m), 0.0)
        l = jnp.sum(e, axis=-1, keepdims=True)
        p = e / jnp.where(l > 0.0, l, 1.0)
        _store_heads(o_ref, g, _dot(p.astype(BF16), vc), tq)
        psum = p[0:tq]
        for hh in range(1, NSA_HPG):
            psum = psum + p[hh * tq:(hh + 1) * tq]
        imp_t = _dot_nt(ov_ref[...], psum.astype(BF16))
        score = jnp.where(forced, FORCE_SCORE, imp_t)
        score = jnp.where(j_idx <= cur, score, NEG_INF)
        score = jnp.where(j_idx < n_sel, score, REMOVED)
        sel = jnp.zeros((nsp, tq), F32)
        for _ in range(top_n):
            mx = jnp.max(score, axis=0, keepdims=True)
            idx = jnp.min(jnp.where(score == mx, j_f, float(nsp)), axis=0, keepdims=True)
            hit = j_f == idx
            sel = jnp.where(hit, 1.0, sel)
            score = jnp.where(hit, REMOVED, score)
        sel = jnp.where(j_idx <= cur, sel, 0.0)
        m_ref[g] = sel.T.astype(m_ref.dtype)


def nsa_cmp_select(p1, kc, vc, ov_t, T):
    B = p1.shape[0]
    tq = NSA_QBLOCK
    ncp = kc.shape[1]
    nsp = ov_t.shape[0]
    n_sel = T // SEL_LEN
    return pl.pallas_call(
        functools.partial(_nsa_cmp_kernel, tq=tq, n_sel=n_sel, top_n=min(SEL_TOPN, n_sel)),
        grid=(B, T // tq),
        in_specs=[pl.BlockSpec((None, tq, NSA_HEADS * LANES), lambda b, i: (b, i, 0)),
                  pl.BlockSpec((None, ncp, LANES), lambda b, i: (b, 0, 0)),
                  pl.BlockSpec((None, ncp, LANES), lambda b, i: (b, 0, 0)),
                  pl.BlockSpec((nsp, ncp), lambda b, i: (0, 0))],
        out_specs=[pl.BlockSpec((None, tq, NSA_HEADS * LANES), lambda b, i: (b, i, 0)),
                   pl.BlockSpec((None, NSA_GROUPS, tq, nsp), lambda b, i: (b, 0, i, 0))],
        out_shape=[jax.ShapeDtypeStruct((B, T, NSA_HEADS * LANES), BF16),
                   jax.ShapeDtypeStruct((B, NSA_GROUPS, T, nsp), BF16)],
        compiler_params=_cparams("parallel", "parallel"),
        name="nsa_cmp_select",
    )(p1, kc, vc, ov_t)


SEL_BONUS = 8192.0
NSA_SEL_TQ = 256
NSA_SEL_TK = 1024


def _nsa_sel_kernel(q_ref, k_ref, v_ref, m_ref, et_ref, o_ref, *, tq, tk):
    t0 = pl.program_id(1) * tq
    n_tiles = (t0 + tq + tk - 1) // tk
    rows = NSA_HPG * tq
    for g in range(NSA_GROUPS):
        q = jnp.concatenate([_stack_heads(q_ref, g), jnp.concatenate([m_ref[g]] * NSA_HPG, axis=0)], axis=1)
        den = HEAD_DIM * (1 - g)

        def step(j, carry, masked, q=q, g=g):
            m, acc = carry
            start = pl.multiple_of(j * tk, tk)
            ks = jnp.concatenate([k_ref[pl.ds(start, tk), :], et_ref[pl.ds(start, tk), :]], axis=1)
            s = _dot_nt(q, ks)
            if masked:
                trow = t0 + lax.broadcasted_iota(jnp.int32, (rows, tk), 0) % tq
                kpos = start + lax.broadcasted_iota(jnp.int32, (rows, tk), 1)
                s = jnp.where(kpos <= trow, s, NEG_INF)
            m_new = jnp.maximum(m, jnp.max(s, axis=-1, keepdims=True))
            p = jnp.exp2(s - m_new)
            acc = jnp.exp2(m - m_new) * acc + _dot(p.astype(BF16), v_ref[g, pl.ds(start, tk), :])
            return m_new, acc

        init = (jnp.full((rows, 1), NEG_INF, F32), jnp.zeros((rows, LANES), F32))
        carry = lax.fori_loop(0, n_tiles - 1, functools.partial(step, masked=False), init)
        _, acc = step(n_tiles - 1, carry, True)
        _store_heads(o_ref, g, acc / acc[:, den:den + 1], tq)


def nsa_value_augment(v):
    ones = jnp.ones_like(v[..., :HEAD_DIM])
    return jnp.stack([jnp.concatenate([v[..., :HEAD_DIM], ones], axis=-1),
                      jnp.concatenate([ones, v[..., HEAD_DIM:]], axis=-1)], axis=1)


def nsa_selected(p1, v_aug, sel, et_mat, T, *, tq=NSA_SEL_TQ, tk=NSA_SEL_TK):
    B = p1.shape[0]
    nsp = sel.shape[-1]
    return pl.pallas_call(
        functools.partial(_nsa_sel_kernel, tq=tq, tk=tk),
        grid=(B, T // tq),
        in_specs=[pl.BlockSpec((None, tq, NSA_HEADS * LANES), lambda b, i: (b, i, 0)),
                  pl.BlockSpec((None, T, LANES), lambda b, i: (b, 0, P1_NKS // LANES)),
                  pl.BlockSpec((None, NSA_GROUPS, T, LANES), lambda b, i: (b, 0, 0, 0)),
                  pl.BlockSpec((None, NSA_GROUPS, tq, nsp), lambda b, i: (b, 0, i, 0)),
                  pl.BlockSpec((T, nsp), lambda b, i: (0, 0))],
        out_specs=pl.BlockSpec((None, tq, NSA_HEADS * LANES), lambda b, i: (b, i, 0)),
        out_shape=jax.ShapeDtypeStruct((B, T, NSA_HEADS * LANES), BF16),
        compiler_params=_cparams("parallel", "parallel"),
        name="nsa_selected",
    )(p1, p1, v_aug, sel, et_mat)


def _nsa_win_kernel(q_ref, k_ref, v_ref, o_ref, *, tq):
    t0 = pl.program_id(1) * tq
    span = WINDOW + tq
    start = pl.multiple_of(jnp.maximum(t0 - WINDOW, 0), tq)
    ks = k_ref[pl.ds(start, span), :]
    vs = v_ref[pl.ds(start, span), :]
    rows = NSA_HPG * tq
    t_idx = t0 + lax.broadcasted_iota(jnp.int32, (rows, span), 0) % tq
    kpos = start + lax.broadcasted_iota(jnp.int32, (rows, span), 1)
    ok = (kpos <= t_idx) & ((t_idx - kpos) < WINDOW)
    for g in range(NSA_GROUPS):
        q = _stack_heads(q_ref, g)
        s = jnp.where(ok, _dot_nt(q, ks), NEG_INF)
        m = jnp.max(s, axis=-1, keepdims=True)
        p = jnp.exp2(s - m)
        l = jnp.sum(p, axis=-1, keepdims=True)
        _store_heads(o_ref, g, _dot(p.astype(BF16), vs) / l, tq)


def nsa_window(p1, p2, T):
    B = p1.shape[0]
    tq = NSA_QBLOCK
    return pl.pallas_call(
        functools.partial(_nsa_win_kernel, tq=tq),
        grid=(B, T // tq),
        in_specs=[pl.BlockSpec((None, tq, NSA_HEADS * LANES), lambda b, i: (b, i, 0)),
                  pl.BlockSpec((None, T, LANES), lambda b, i: (b, 0, P1_NKW // LANES)),
                  pl.BlockSpec((None, T, LANES), lambda b, i: (b, 0, P2_NVW // LANES))],
        out_specs=pl.BlockSpec((None, tq, NSA_HEADS * LANES), lambda b, i: (b, i, 0)),
        out_shape=jax.ShapeDtypeStruct((B, T, NSA_HEADS * LANES), BF16),
        compiler_params=_cparams("parallel", "parallel"),
        name="nsa_window",
    )(p1, p1, p2)


def _retention_kernel(q_ref, k_ref, v_ref, g_ref, din_ref, qd_ref, kd_ref, cd_ref, o_ref, st_ref):
    @pl.when(pl.program_id(0) == 0)
    def _():
        st_ref[...] = jnp.zeros_like(st_ref)

    B = q_ref.shape[0]
    for b in range(B):
        for h in range(RET_HEADS):
            lanes = slice(h * LANES, (h + 1) * LANES)
            qh = q_ref[b, :, lanes]
            kp = k_ref[b, :, (h // 2) * LANES:(h // 2 + 1) * LANES]
            vh = v_ref[b, :, lanes]
            st = st_ref[b, h]
            inner = _dot_nt(qh, kp) * din_ref[h]
            o = _dot(inner.astype(BF16), vh) + _dot(qh, st.astype(BF16)) * qd_ref[h]
            kd = (kp.astype(F32) * kd_ref[h]).astype(BF16)
            st_ref[b, h] = st * cd_ref[h, 0:1, :] + _dot_tn(kd, vh)
            mu = jnp.mean(o, axis=-1, keepdims=True)
            d = o - mu
            var = jnp.mean(d * d, axis=-1, keepdims=True)
            on = d * lax.rsqrt(var + NORM_EPS)
            gh = g_ref[b, :, lanes].astype(F32)
            o_ref[b, :, lanes] = (gh * _sigmoid(gh) * on).astype(o_ref.dtype)


def retention_consts():
    C = RET_CHUNK
    H = RET_HEADS
    log_g = jnp.log(1.0 - 2.0 ** (-5.0 - jnp.arange(H, dtype=F32)))
    n = jnp.arange(C, dtype=F32)
    diff = n[:, None] - n[None, :]
    causal = diff >= 0
    decay_in = jnp.where(causal[None], jnp.exp(jnp.where(causal, diff, 0.0)[None] * log_g[:, None, None]), 0.0)
    q_decay = jnp.exp((n[None, :] + 1.0) * log_g[:, None])
    k_decay = jnp.exp((C - 1.0 - n)[None, :] * log_g[:, None])
    chunk_decay = jnp.exp(C * log_g)
    qd = jnp.broadcast_to(q_decay[:, :, None], (H, C, LANES))
    kd = jnp.broadcast_to(k_decay[:, :, None], (H, C, LANES))
    cd = jnp.broadcast_to(chunk_decay[:, None, None], (H, 8, LANES))
    return decay_in, qd, kd, cd


def retention(p1, p2, consts, T):
    B = p1.shape[0]
    C = RET_CHUNK
    din, qd, kd, cd = consts
    W = RET_HEADS * LANES
    full = lambda shape: pl.BlockSpec(shape, lambda c: (0,) * len(shape))
    return pl.pallas_call(
        _retention_kernel,
        grid=(T // C,),
        in_specs=[pl.BlockSpec((B, C, W), lambda c: (0, c, P1_RQ // W)),
                  pl.BlockSpec((B, C, W // 2), lambda c: (0, c, P1_RK // (W // 2))),
                  pl.BlockSpec((B, C, W), lambda c: (0, c, P2_RV // W)),
                  pl.BlockSpec((B, C, W), lambda c: (0, c, P2_RG // W)),
                  full(din.shape), full(qd.shape), full(kd.shape), full(cd.shape)],
        out_specs=pl.BlockSpec((B, C, W), lambda c: (0, c, 0)),
        out_shape=jax.ShapeDtypeStruct((B, T, W), BF16),
        scratch_shapes=[pltpu.VMEM((B, RET_HEADS, LANES, LANES), F32)],
        compiler_params=_cparams("arbitrary"),
        name="retention",
    )(p1, p1, p2, p2, din, qd, kd, cd)


def _fox_cum_kernel(f_ref, b_ref, o_ref):
    x = f_ref[...] + b_ref[...]
    ls = jnp.minimum(x, 0.0) - jnp.log1p(jnp.exp(-jnp.abs(x)))
    R = x.shape[0]
    ki = lax.broadcasted_iota(jnp.int32, (LANES, LANES), 0)
    ji = lax.broadcasted_iota(jnp.int32, (LANES, LANES), 1)
    upper = jnp.where(ki <= ji, 1.0, 0.0).astype(BF16)
    hi, mid, lo = _split3(ls)
    rowcum = _dot(hi, upper) + _dot(mid, upper) + _dot(lo, upper)
    tot = jnp.broadcast_to(rowcum[:, LANES - 1:LANES], (R, LANES))
    ri = lax.broadcasted_iota(jnp.int32, (R, R), 0)
    ci = lax.broadcasted_iota(jnp.int32, (R, R), 1)
    lower = jnp.where(ci < ri, 1.0, 0.0).astype(BF16)
    hi, mid, lo = _split3(tot)
    offs = _dot(lower, hi) + _dot(lower, mid) + _dot(lower, lo)
    o_ref[...] = (rowcum + offs) * LOG2E


def fox_cum(f_logit, bias):
    B, H, R, _ = f_logit.shape
    return pl.pallas_call(
        _fox_cum_kernel,
        grid=(B, H),
        in_specs=[pl.BlockSpec((None, None, R, LANES), lambda b, h: (b, h, 0, 0)),
                  pl.BlockSpec((None, 1, LANES), lambda b, h: (h, 0, 0))],
        out_specs=pl.BlockSpec((None, None, R, LANES), lambda b, h: (b, h, 0, 0)),
        out_shape=jax.ShapeDtypeStruct((B, H, R, LANES), F32),
        compiler_params=_cparams("parallel", "parallel"),
        name="fox_cum",
    )(f_logit, bias)


FOX_BIAS_LANES = 3


def _fox_kernel(q_ref, k_ref, v_ref, c_ref, o_ref, ka_ref, va_ref, *, tq):
    i = pl.program_id(2)
    tk = tq
    T = k_ref.shape[0]
    chunk = 512

    @pl.when(i == 0)
    def _():
        lane = lax.broadcasted_iota(jnp.int32, (chunk, LANES), 1)
        ri = lax.broadcasted_iota(jnp.int32, (16, LANES), 0)
        ci = lax.broadcasted_iota(jnp.int32, (16, LANES), 1)
        place = jnp.where((ci == ri + HEAD_DIM) & (ri < FOX_BIAS_LANES), 1.0, 0.0).astype(BF16)

        def build(c, _):
            c0 = pl.multiple_of(c * chunk, chunk)
            kp = k_ref[pl.ds(c0, chunk), :].astype(F32)
            vp = v_ref[pl.ds(c0, chunk), :].astype(F32)
            for hh in range(2):
                hi, mid, lo = _split3(-c_ref[hh, :, pl.ds(c0, chunk)])
                terms = jnp.concatenate([hi, mid, lo, jnp.zeros((13, chunk), BF16)], axis=0)
                bias = _dot_tn(terms, place)
                kh = kp if hh == 0 else pltpu.roll(kp, HEAD_DIM, 1)
                vh = vp if hh == 0 else pltpu.roll(vp, HEAD_DIM, 1)
                ka_ref[hh, pl.ds(c0, chunk), :] = jnp.where(lane < HEAD_DIM, kh, bias).astype(BF16)
                va_ref[hh, pl.ds(c0, chunk), :] = jnp.where(lane < HEAD_DIM, vh, 1.0).astype(BF16)
            return 0

        lax.fori_loop(0, T // chunk, build, 0)

    row = lax.broadcasted_iota(jnp.int32, (tq, tk), 0)
    col = lax.broadcasted_iota(jnp.int32, (tq, tk), 1)
    lane = lax.broadcasted_iota(jnp.int32, (tq, LANES), 1)
    ones_lanes = (lane >= HEAD_DIM) & (lane < HEAD_DIM + FOX_BIAS_LANES)
    qs = [jnp.where(ones_lanes, 1.0, q_ref[:, hh * LANES:(hh + 1) * LANES].astype(F32)).astype(BF16)
          for hh in range(2)]

    def step(j, carry, masked):
        start = pl.multiple_of(j * tk, tk)
        out = []
        for hh in range(2):
            m, acc = carry[hh]
            s = _dot_nt(qs[hh], ka_ref[hh, pl.ds(start, tk), :])
            if masked:
                s = jnp.where(col <= row, s, NEG_INF)
            m_new = jnp.maximum(m, jnp.max(s, axis=-1, keepdims=True))
            p = jnp.exp2(s - m_new)
            acc = jnp.exp2(m - m_new) * acc + _dot(p.astype(BF16), va_ref[hh, pl.ds(start, tk), :])
            out.append((m_new, acc))
        return tuple(out)

    one = (jnp.full((tq, 1), NEG_INF, F32), jnp.zeros((tq, LANES), F32))
    carry = lax.fori_loop(0, i, functools.partial(step, masked=False), (one, one))
    (_, acc0), (_, acc1) = step(i, carry, True)
    o0 = acc0 / acc0[:, HEAD_DIM:HEAD_DIM + 1]
    o1 = acc1 / acc1[:, HEAD_DIM:HEAD_DIM + 1]
    o_ref[...] = jnp.where(lane < HEAD_DIM, o0, pltpu.roll(o1, HEAD_DIM, 1)).astype(o_ref.dtype)


def fox_attention(p2, cum, T, *, tq=FOX_TQ):
    B = p2.shape[0]
    HP = FOX_HEADS // 2
    return pl.pallas_call(
        functools.partial(_fox_kernel, tq=tq),
        grid=(B, HP, T // tq),
        in_specs=[pl.BlockSpec((None, tq, 2 * LANES), lambda b, h, i: (b, i, P2_FQ // (2 * LANES) + h)),
                  pl.BlockSpec((None, T, LANES), lambda b, h, i: (b, 0, P2_FK // LANES + h)),
                  pl.BlockSpec((None, T, LANES), lambda b, h, i: (b, 0, P2_FV // LANES + h)),
                  pl.BlockSpec((None, None, 2, 1, T), lambda b, h, i: (b, h, 0, 0, 0))],
        out_specs=pl.BlockSpec((None, tq, LANES), lambda b, h, i: (b, i, h)),
        out_shape=jax.ShapeDtypeStruct((B, T, FOX_HEADS * HEAD_DIM), BF16),
        scratch_shapes=[pltpu.VMEM((2, T, LANES), BF16), pltpu.VMEM((2, T, LANES), BF16)],
        compiler_params=_cparams("parallel", "parallel", "arbitrary"),
        name="fox_attention",
    )(p2, p2, p2, cum)


def _readout_kernel(ocmp_ref, osel_ref, owin_ref, small_ref, oret_ref, ofox_ref, mg_ref, x_ref, g1_ref,
                    ex_ref, wn_ref, wr_ref, wf_ref, wo_ref, o_ref):
    W = NSA_HEADS * LANES
    gs = _sigmoid(small_ref[...].astype(F32)).astype(BF16)
    ge = _dot(gs, ex_ref[...])
    onsa = (ge[:, :W] * ocmp_ref[...].astype(F32) + ge[:, W:2 * W] * osel_ref[...].astype(F32)
            + ge[:, 2 * W:] * owin_ref[...].astype(F32))
    D = D_MODEL
    merged = (_sigmoid(mg_ref[:, :D].astype(F32)) * _dot(onsa.astype(BF16), wn_ref[...])
              + _sigmoid(mg_ref[:, D:2 * D].astype(F32)) * _dot(oret_ref[...], wr_ref[...])
              + _sigmoid(mg_ref[:, 2 * D:].astype(F32)) * _dot(ofox_ref[...], wf_ref[...]))
    y = _dot(merged.astype(BF16), wo_ref[...])
    o_ref[...] = x_ref[...] + g1_ref[...] * y


def readout(o_cmp, o_sel, o_win, p2, o_ret, o_fox, x, mod_l, ex, wn, wr, wf, wo, T, *, tm=512):
    M, D = x.shape
    per_b = T // tm
    W = NSA_HEADS * LANES
    row = lambda width, col=0: pl.BlockSpec((tm, width), lambda i: (i, col))
    full = lambda a: pl.BlockSpec(a.shape, lambda i: (0,) * a.ndim)
    return pl.pallas_call(
        _readout_kernel,
        grid=(M // tm,),
        in_specs=[row(W), row(W), row(W), row(LANES, P2_SMALL // LANES), row(512), row(512),
                  row(3 * D, 0), row(D),
                  pl.BlockSpec((None, None, 1, D), lambda i: (i // per_b, 2, 0, 0)),
                  full(ex), full(wn), full(wr), full(wf), full(wo)],
        out_specs=row(D),
        out_shape=jax.ShapeDtypeStruct((M, D), F32),
        compiler_params=_cparams("parallel"),
        name="mixer_readout",
    )(o_cmp, o_sel, o_win, p2, o_ret, o_fox, p2, x, mod_l, ex, wn, wr, wf, wo)


def nsa_gate_expand():
    ex = np.zeros((LANES, 3 * NSA_HEADS * LANES), np.float32)
    for br in range(3):
        for h in range(NSA_HEADS):
            c0 = br * NSA_HEADS * LANES + h * LANES
            ex[br * NSA_HEADS + h, c0:c0 + LANES] = 1.0
    return jnp.asarray(ex, BF16)


def pad_read_nsa(w):
    D = w.shape[1]
    w = w.reshape(NSA_HEADS, HEAD_DIM, D)
    z = jnp.zeros_like(w)
    g = (np.arange(NSA_HEADS) // NSA_HPG)[:, None, None]
    lo = jnp.where(g == 0, w, z)
    hi = jnp.where(g == 1, w, z)
    return jnp.concatenate([lo, hi], axis=1).reshape(NSA_HEADS * LANES, D).astype(BF16)


def _ffn_kernel(*refs, gated):
    if gated:
        x_ref, nw_ref, sc_ref, sh_ref, g2_ref, gate_ref, w1_ref, w3_ref, w2_ref, o_ref, h_ref, acc_ref = refs
    else:
        x_ref, nw_ref, sc_ref, sh_ref, g2_ref, w1_ref, w3_ref, w2_ref, o_ref, h_ref, acc_ref = refs
    e = pl.program_id(1)
    f = pl.program_id(2)

    @pl.when((e == 0) & (f == 0))
    def _():
        h_ref[...] = _norm_mod(x_ref[...], nw_ref[...], sc_ref[...], sh_ref[...]).astype(BF16)
        acc_ref[...] = jnp.zeros_like(acc_ref)

    h = h_ref[...]
    u = _dot(h, w1_ref[...])
    v = _dot(h, w3_ref[...])
    a = (u * _sigmoid(u) * v).astype(BF16)
    y = _dot(a, w2_ref[...])
    if gated:
        gate = gate_ref[...]
        lane = lax.broadcasted_iota(jnp.int32, gate.shape, 1)
        y = y * jnp.sum(jnp.where(lane == e, gate, 0.0), axis=-1, keepdims=True)
    acc_ref[...] += y

    @pl.when((e == pl.num_programs(1) - 1) & (f == pl.num_programs(2) - 1))
    def _():
        o_ref[...] = x_ref[...] + g2_ref[...] * acc_ref[...]


def ffn(x, mod_l, nw, w1, w3, w2, gate, T, *, tm, tf):
    M, D = x.shape
    E, _, F = w1.shape
    per_b = T // tm
    gated = gate is not None
    modspec = lambda k: pl.BlockSpec((None, None, 1, D), lambda i, e, f: (i // per_b, k, 0, 0))
    in_specs = [pl.BlockSpec((tm, D), lambda i, e, f: (i, 0)),
                pl.BlockSpec((1, D), lambda i, e, f: (0, 0)),
                modspec(4), modspec(3), modspec(5)]
    args = [x, nw, mod_l, mod_l, mod_l]
    if gated:
        in_specs.append(pl.BlockSpec((tm, LANES), lambda i, e, f: (i, 0)))
        args.append(gate)
    in_specs += [pl.BlockSpec((None, D, tf), lambda i, e, f: (e, 0, f)),
                 pl.BlockSpec((None, D, tf), lambda i, e, f: (e, 0, f)),
                 pl.BlockSpec((None, tf, D), lambda i, e, f: (e, f, 0))]
    args += [w1, w3, w2]
    return pl.pallas_call(
        functools.partial(_ffn_kernel, gated=gated),
        grid=(M // tm, E, F // tf),
        in_specs=in_specs,
        out_specs=pl.BlockSpec((tm, D), lambda i, e, f: (i, 0)),
        out_shape=jax.ShapeDtypeStruct((M, D), F32),
        scratch_shapes=[pltpu.VMEM((tm, D), BF16), pltpu.VMEM((tm, D), F32)],
        compiler_params=_cparams("parallel", "arbitrary", "arbitrary"),
        name="ffn_gated" if gated else "ffn_dense",
    )(*args)


MOE_TC = 512
MOE_TS = 512


def _router_kernel(x_ref, nw_ref, sc_ref, sh_ref, wh_ref, wl_ref,
                   h_ref, gate_ref, rank_ref, rank_t_ref, cstart_ref, cnt_ref, carry_ref, carry_t_ref):
    @pl.when(pl.program_id(0) == 0)
    def _():
        carry_ref[...] = jnp.zeros_like(carry_ref)
        carry_t_ref[...] = jnp.zeros_like(carry_t_ref)

    h = _norm_mod(x_ref[...], nw_ref[...], sc_ref[...], sh_ref[...])
    hh = h.astype(BF16)
    h_ref[...] = hh
    hl = (h - hh.astype(F32)).astype(BF16)
    logits = _dot(hh, wh_ref[...]) + (_dot(hl, wh_ref[...]) + _dot(hh, wl_ref[...]))
    tm = logits.shape[0]
    lane = lax.broadcasted_iota(jnp.int32, logits.shape, 1)
    logits = jnp.where(lane < N_EXPERTS, logits, REMOVED)
    lane_f = lane.astype(F32)
    v1 = jnp.max(logits, axis=-1, keepdims=True)
    i1 = jnp.min(jnp.where(logits == v1, lane_f, float(LANES)), axis=-1, keepdims=True)
    rest = jnp.where(lane_f == i1, REMOVED, logits)
    v2 = jnp.max(rest, axis=-1, keepdims=True)
    i2 = jnp.min(jnp.where(rest == v2, lane_f, float(LANES)), axis=-1, keepdims=True)
    e2 = jnp.exp(v2 - v1)
    w1 = 1.0 / (1.0 + e2)
    w2 = e2 / (1.0 + e2)
    gate_ref[...] = jnp.where(lane_f == i1, w1, jnp.where(lane_f == i2, w2, 0.0))

    sel = jnp.where((lane_f == i1) | (lane_f == i2), 1.0, 0.0)
    sel_t = sel.T[0:N_EXPERTS]
    ri = lax.broadcasted_iota(jnp.int32, (tm, tm), 0)
    ci = lax.broadcasted_iota(jnp.int32, (tm, tm), 1)
    before = jnp.where(ci < ri, 1.0, 0.0).astype(BF16)
    after = jnp.where(ri < ci, 1.0, 0.0).astype(BF16)
    carry = carry_ref[0:1, :]
    carry_t = carry_t_ref[:, 0:1]
    rank = _dot(before, sel.astype(BF16)) + carry
    rank_t = _dot(sel_t.astype(BF16), after) + carry_t
    rank_ref[...] = jnp.where(sel > 0.0, rank, -1.0)
    rank_t_ref[...] = jnp.where(sel_t > 0.0, rank_t, -1.0)
    cstart_ref[0] = carry_ref[...]
    carry_ref[...] = carry_ref[...] + jnp.sum(sel, axis=0, keepdims=True)
    carry_t_ref[...] = carry_t_ref[...] + jnp.sum(sel_t, axis=1, keepdims=True)
    cnt_ref[...] = carry_ref[...]


def router(x, mod_l, nw, w_router, T):
    M, D = x.shape
    tm = MOE_TC
    per_b = T // tm
    wp = jnp.zeros((D, LANES), F32).at[:, :N_EXPERTS].set(w_router)
    wh = wp.astype(BF16)
    wl = (wp - wh.astype(F32)).astype(BF16)
    nc = M // tm
    return pl.pallas_call(
        _router_kernel,
        grid=(nc,),
        in_specs=[pl.BlockSpec((tm, D), lambda i: (i, 0)),
                  pl.BlockSpec((1, D), lambda i: (0, 0))]
        + _mod_specs(T, tm, 4, 3, 1)
        + [pl.BlockSpec((D, LANES), lambda i: (0, 0)),
           pl.BlockSpec((D, LANES), lambda i: (0, 0))],
        out_specs=[pl.BlockSpec((tm, D), lambda i: (i, 0)),
                   pl.BlockSpec((tm, LANES), lambda i: (i, 0)),
                   pl.BlockSpec((tm, LANES), lambda i: (i, 0)),
                   pl.BlockSpec((N_EXPERTS, tm), lambda i: (0, i)),
                   pl.BlockSpec((1, 8, LANES), lambda i: (i, 0, 0)),
                   pl.BlockSpec((8, LANES), lambda i: (0, 0))],
        out_shape=[jax.ShapeDtypeStruct((M, D), BF16),
                   jax.ShapeDtypeStruct((M, LANES), F32),
                   jax.ShapeDtypeStruct((M, LANES), F32),
                   jax.ShapeDtypeStruct((N_EXPERTS, M), F32),
                   jax.ShapeDtypeStruct((nc, 8, LANES), F32),
                   jax.ShapeDtypeStruct((8, LANES), F32)],
        scratch_shapes=[pltpu.VMEM((8, LANES), F32), pltpu.VMEM((8, LANES), F32)],
        compiler_params=_cparams("arbitrary"),
        name="moe_router",
    )(x, nw, mod_l, mod_l, wh, wl)


def _count_le(sorted_vals, x):
    return jnp.sum(sorted_vals[None, :] <= x[:, None], axis=1, dtype=jnp.int32)


def _take(vals, idx):
    n = vals.shape[0]
    hit = idx[:, None] == jnp.arange(n, dtype=jnp.int32)[None, :]
    return jnp.sum(jnp.where(hit, vals[None, :], 0), axis=1, dtype=jnp.int32)


def moe_schedule(counts, cstart, M):
    ts, tc = MOE_TS, MOE_TC
    nc = M // tc
    rt = (2 * M) // ts + N_EXPERTS
    smax = rt + N_EXPERTS * nc
    i32 = jnp.int32
    cnt = counts.astype(i32)
    cs = cstart.astype(i32)
    ce = jnp.concatenate([cs[1:], cnt[None]], axis=0)
    ntile = (cnt + ts - 1) // ts
    tile_end = jnp.cumsum(ntile)
    tile_off = tile_end - ntile
    total_tiles = tile_end[-1]
    r = jnp.arange(rt, dtype=i32)
    e_r = jnp.minimum(_count_le(tile_end, r), N_EXPERTS - 1)
    valid_r = r < total_tiles
    onehot_e = e_r[:, None] == jnp.arange(N_EXPERTS, dtype=i32)[None, :]
    pick = lambda per_expert: jnp.sum(jnp.where(onehot_e, per_expert[None, :], 0), axis=1, dtype=i32)
    k_r = r - pick(tile_off)
    lo = k_r * ts
    hi = jnp.minimum(lo + ts, pick(cnt))
    cs_r = jnp.sum(jnp.where(onehot_e[None], cs[:, None, :], 0), axis=2, dtype=i32)
    ce_r = jnp.sum(jnp.where(onehot_e[None], ce[:, None, :], 0), axis=2, dtype=i32)
    ov = (ce_r > lo[None]) & (cs_r < hi[None]) & valid_r[None]
    n_c = jnp.sum(ov, axis=0, dtype=i32)
    c_lo = jnp.sum(jnp.cumsum(ov, axis=0) == 0, axis=0, dtype=i32)
    pend = jnp.cumsum(n_c)
    pstart = pend - n_c
    n_pairs = pend[-1]
    steps = jnp.arange(smax, dtype=i32)
    s = jnp.minimum(steps, n_pairs - 1)
    g_r = jnp.minimum(_count_le(pend, s), rt - 1)
    g_c = _take(c_lo, g_r) + (s - _take(pstart, g_r))
    g_valid = steps < n_pairs
    g_first = (s == _take(pstart, g_r)) & g_valid
    gather = dict(r=g_r, c=g_c, e=_take(e_r, g_r), base=_take(k_r, g_r) * ts,
                  first=g_first.astype(i32), valid=g_valid.astype(i32))

    row_off = tile_off * ts
    has = ce > cs
    r_lo = (row_off[None] + cs) // ts
    r_hi = (row_off[None] + ce - 1) // ts
    n_q = jnp.where(has, r_hi - r_lo + 1, 0).reshape(-1).astype(i32)
    qend = jnp.cumsum(n_q)
    qstart = qend - n_q
    n_pairs2 = qend[-1]
    s2 = jnp.minimum(steps, n_pairs2 - 1)
    q = jnp.minimum(_count_le(qend, s2), nc * N_EXPERTS - 1)
    c_c = q // N_EXPERTS
    c_e = q % N_EXPERTS
    c_r = _take(r_lo.reshape(-1).astype(i32), q) + (s2 - _take(qstart, q))
    c_valid = steps < n_pairs2
    tok_first = qstart.reshape(nc, N_EXPERTS)[:, 0]
    tok_last = qend.reshape(nc, N_EXPERTS)[:, -1] - 1
    c_first = (s2 == _take(tok_first, c_c)) & c_valid
    c_last = (s2 == _take(tok_last, c_c)) & c_valid
    combine = dict(r=c_r, c=c_c, e=c_e, base=c_r * ts - _take(row_off, c_e),
                   first=c_first.astype(i32), last=c_last.astype(i32), valid=c_valid.astype(i32))
    tiles = dict(e=e_r, total=total_tiles.reshape(1).astype(i32))
    return tiles, gather, combine, rt, smax


def _moe_gather_kernel(r_s, c_s, e_s, base_s, first_s, valid_s, rank_t_ref, h_ref, o_ref):
    s = pl.program_id(0)

    @pl.when(first_s[s] == 1)
    def _():
        o_ref[...] = jnp.zeros_like(o_ref)

    @pl.when(valid_s[s] == 1)
    def _():
        ts = o_ref.shape[0]
        tc = h_ref.shape[0]
        rk = rank_t_ref[pl.ds(e_s[s], 1), :] - base_s[s].astype(F32)
        row = lax.broadcasted_iota(jnp.int32, (ts, tc), 0).astype(F32)
        onehot = jnp.where(rk == row, 1.0, 0.0).astype(BF16)
        o_ref[...] += _dot(onehot, h_ref[...]).astype(o_ref.dtype)


def moe_gather(h, rank_t, g, rt, smax):
    M, D = h.shape
    ts, tc = MOE_TS, MOE_TC
    return pl.pallas_call(
        _moe_gather_kernel,
        grid_spec=pltpu.PrefetchScalarGridSpec(
            num_scalar_prefetch=6,
            grid=(smax,),
            in_specs=[pl.BlockSpec((N_EXPERTS, tc), lambda s, r, c, *_: (0, c[s])),
                      pl.BlockSpec((tc, D), lambda s, r, c, *_: (c[s], 0))],
            out_specs=pl.BlockSpec((ts, D), lambda s, r, c, *_: (r[s], 0)),
        ),
        out_shape=jax.ShapeDtypeStruct((rt * ts, D), BF16),
        compiler_params=_cparams("arbitrary"),
        name="moe_gather",
    )(g["r"], g["c"], g["e"], g["base"], g["first"], g["valid"], rank_t, h)


def _moe_up_kernel(e_r, total, x_ref, w1_ref, w3_ref, o_ref, w1b_ref, w3b_ref):
    r = pl.program_id(1)
    live = r < total[0]

    @pl.when(live & ((r == 0) | (e_r[r] != e_r[jnp.maximum(r - 1, 0)])))
    def _():
        w1b_ref[...] = w1_ref[...].astype(BF16)
        w3b_ref[...] = w3_ref[...].astype(BF16)

    @pl.when(live)
    def _():
        x = x_ref[...]
        u = _dot(x, w1b_ref[...])
        v = _dot(x, w3b_ref[...])
        o_ref[...] = (u * _sigmoid(u) * v).astype(o_ref.dtype)


def moe_up(xs, w1, w3, tiles, rt, *, tf=896):
    R, D = xs.shape
    ts = MOE_TS
    F = w1.shape[-1]
    live = lambda r, total: jnp.minimum(r, total[0] - 1)
    return pl.pallas_call(
        _moe_up_kernel,
        grid_spec=pltpu.PrefetchScalarGridSpec(
            num_scalar_prefetch=2,
            grid=(F // tf, rt),
            in_specs=[pl.BlockSpec((ts, D), lambda n, r, e, total: (live(r, total), 0)),
                      pl.BlockSpec((None, D, tf), lambda n, r, e, total: (e[live(r, total)], 0, n)),
                      pl.BlockSpec((None, D, tf), lambda n, r, e, total: (e[live(r, total)], 0, n))],
            out_specs=pl.BlockSpec((ts, tf), lambda n, r, e, total: (r, n)),
            scratch_shapes=[pltpu.VMEM((D, tf), BF16), pltpu.VMEM((D, tf), BF16)],
        ),
        out_shape=jax.ShapeDtypeStruct((R, F), BF16),
        compiler_params=_cparams("arbitrary", "arbitrary"),
        name="moe_up",
    )(tiles["e"], tiles["total"], xs, w1, w3)


def _moe_down_kernel(e_r, total, a_ref, w2_ref, o_ref):
    @pl.when(pl.program_id(0) < total[0])
    def _():
        o_ref[...] = _dot(a_ref[...], w2_ref[...]).astype(o_ref.dtype)


def moe_down(a, w2, tiles, rt):
    R, F = a.shape
    ts = MOE_TS
    D = w2.shape[-1]
    live = lambda r, total: jnp.minimum(r, total[0] - 1)
    return pl.pallas_call(
        _moe_down_kernel,
        grid_spec=pltpu.PrefetchScalarGridSpec(
            num_scalar_prefetch=2,
            grid=(rt,),
            in_specs=[pl.BlockSpec((ts, F), lambda r, e, total: (live(r, total), 0)),
                      pl.BlockSpec((None, F, D), lambda r, e, total: (e[live(r, total)], 0, 0))],
            out_specs=pl.BlockSpec((ts, D), lambda r, e, total: (r, 0)),
        ),
        out_shape=jax.ShapeDtypeStruct((R, D), BF16),
        compiler_params=_cparams("arbitrary"),
        name="moe_down",
    )(tiles["e"], tiles["total"], a, w2)


def _moe_combine_kernel(r_s, c_s, e_s, base_s, first_s, last_s, valid_s,
                        rank_ref, gate_ref, y_ref, x_ref, g2_ref, o_ref, acc_ref):
    s = pl.program_id(0)

    @pl.when(first_s[s] == 1)
    def _():
        acc_ref[...] = jnp.zeros_like(acc_ref)

    @pl.when(valid_s[s] == 1)
    def _():
        tc = rank_ref.shape[0]
        ts = y_ref.shape[0]
        lane = lax.broadcasted_iota(jnp.int32, (tc, LANES), 1)
        mine = lane == e_s[s]
        rank_col = jnp.sum(jnp.where(mine, rank_ref[...], 0.0), axis=-1, keepdims=True) - base_s[s].astype(F32)
        gate_col = jnp.sum(jnp.where(mine, gate_ref[...], 0.0), axis=-1, keepdims=True)
        col = lax.broadcasted_iota(jnp.int32, (tc, ts), 1).astype(F32)
        onehot = jnp.where(rank_col == col, 1.0, 0.0).astype(BF16)
        acc_ref[...] += gate_col * _dot(onehot, y_ref[...])

    @pl.when(last_s[s] == 1)
    def _():
        o_ref[...] = x_ref[...] + g2_ref[...] * acc_ref[...]


def moe_combine(y, rank, gate, x, mod_l, cb, T, smax):
    M, D = x.shape
    ts, tc = MOE_TS, MOE_TC
    per_b = T // tc
    tok = lambda width: pl.BlockSpec((tc, width), lambda s, r, c, *_: (c[s], 0))
    return pl.pallas_call(
        _moe_combine_kernel,
        grid_spec=pltpu.PrefetchScalarGridSpec(
            num_scalar_prefetch=7,
            grid=(smax,),
            in_specs=[tok(LANES), tok(LANES),
                      pl.BlockSpec((ts, D), lambda s, r, c, *_: (r[s], 0)),
                      tok(D),
                      pl.BlockSpec((None, None, 1, D), lambda s, r, c, *_: (c[s] // per_b, 5, 0, 0))],
            out_specs=tok(D),
            scratch_shapes=[pltpu.VMEM((tc, D), F32)],
        ),
        out_shape=jax.ShapeDtypeStruct((M, D), F32),
        compiler_params=_cparams("arbitrary"),
        name="moe_combine",
    )(cb["r"], cb["c"], cb["e"], cb["base"], cb["first"], cb["last"], cb["valid"], rank, gate, y, x, mod_l)


def moe_ffn(x, mod_l, nw, w_router, w1, w3, w2, T):
    M = x.shape[0]
    h, gate, rank, rank_t, cstart, cnt = router(x, mod_l, nw, w_router, T)
    tiles, g, cb, rt, smax = moe_schedule(cnt[0, :N_EXPERTS], cstart[:, 0, :N_EXPERTS], M)
    xs = moe_gather(h, rank_t, g, rt, smax)
    a = moe_up(xs, w1, w3, tiles, rt)
    y = moe_down(a, w2, tiles, rt)
    return moe_combine(y, rank, gate, x, mod_l, cb, T, smax)


def _final_norm_kernel(x_ref, w_ref, o_ref):
    x = x_ref[...]
    ms = jnp.mean(x * x, axis=-1, keepdims=True)
    o_ref[...] = x * lax.rsqrt(ms + NORM_EPS) * w_ref[...]


def final_norm(x, w, *, tm=1024):
    M, D = x.shape
    return pl.pallas_call(
        _final_norm_kernel,
        grid=(M // tm,),
        in_specs=[pl.BlockSpec((tm, D), lambda i: (i, 0)), pl.BlockSpec((1, D), lambda i: (0, 0))],
        out_specs=pl.BlockSpec((tm, D), lambda i: (i, 0)),
        out_shape=jax.ShapeDtypeStruct((M, D), F32),
        compiler_params=_cparams("parallel"),
        name="final_norm",
    )(x, w)


def nsa_constants(T):
    n_sel = T // SEL_LEN
    nsp = max(LANES, n_sel)
    ncp = T // CMP_STRIDE
    cmp_start = np.arange(ncp) * CMP_STRIDE
    sel_start = np.arange(nsp) * SEL_LEN
    ov = ((cmp_start[:, None] < sel_start[None, :] + SEL_LEN)
          & (cmp_start[:, None] + CMP_LEN > sel_start[None, :]))
    ov[(T - CMP_LEN) // CMP_STRIDE + 1:] = False
    ov[:, n_sel:] = False
    et_mat = ((np.arange(T)[:, None] // SEL_LEN) == np.arange(nsp)[None, :]) * SEL_BONUS
    return jnp.asarray(ov.T, BF16), jnp.asarray(et_mat, BF16)


def token_mixing(x, mod_l, lw, consts, B, T):
    M = B * T
    cos_t, sin_t, ov_t, e_mat, ret_consts, ex = consts
    p1 = proj_rope(x, mod_l, lw["norm_mix"], lw["w1"], cos_t, sin_t, p1_scales(), T).reshape(B, T, P1_COLS)
    p2 = proj_plain(x, mod_l, lw["norm_mix"], lw["w2"], T).reshape(B, T, P2_COLS)

    def group_rows(a):
        return a.reshape(B, T, NSA_GROUPS, HEAD_DIM).transpose(0, 2, 1, 3).reshape(
            B, NSA_GROUPS, T // CMP_STRIDE, CMP_STRIDE * HEAD_DIM)

    xr = jnp.stack([group_rows(p1[:, :, P1_NKC:P1_NKC + LANES]), group_rows(p2[:, :, P2_NVC:P2_NVC + LANES])])
    cmp_out = compress(xr, lw["cmp_pe"], lw["cmp_w1"], lw["cmp_w2"])
    cmp_out = cmp_out.transpose(0, 1, 3, 2, 4).reshape(2, B, T // CMP_STRIDE, LANES)
    o_cmp, sel = nsa_cmp_select(p1, cmp_out[0], cmp_out[1], ov_t, T)
    o_sel = nsa_selected(p1, nsa_value_augment(p2[:, :, P2_NVS:P2_NVS + LANES]), sel, e_mat, T)
    o_win = nsa_window(p1, p2, T)

    o_ret = retention(p1, p2, ret_consts, T)

    ff = p2[:, :, P2_SMALL + 3 * NSA_HEADS:P2_SMALL + 3 * NSA_HEADS + FOX_HEADS].astype(F32)
    ff = ff.transpose(0, 2, 1).reshape(B, FOX_HEADS, T // LANES, LANES)
    cum = fox_cum(ff, lw["fox_bias"]).reshape(B, FOX_HEADS // 2, 2, 1, T)
    o_fox = fox_attention(p2, cum, T)

    return readout(o_cmp.reshape(M, -1), o_sel.reshape(M, -1), o_win.reshape(M, -1), p2.reshape(M, P2_COLS),
                   o_ret.reshape(M, -1), o_fox.reshape(M, -1), x, mod_l, ex,
                   lw["wn"], lw["wr"], lw["wf"], lw["wo"], T)


def layer_weights(l, norm_mix, w_in, cmp_k_pe, cmp_k_w1, cmp_k_w2, cmp_v_pe, cmp_v_w1, cmp_v_w2, fox_f_bias,
                  w_read_nsa, w_read_ret, w_read_fox, w_out):
    w1, w2 = split_w_in(w_in[l])
    pe = jnp.stack([cmp_k_pe[l].reshape(1, -1), cmp_v_pe[l].reshape(1, -1)])
    pe = jnp.broadcast_to(pe, (2, 8, pe.shape[-1])).astype(BF16)
    return {
        "norm_mix": norm_mix[l].reshape(1, -1),
        "w1": w1, "w2": w2,
        "cmp_pe": pe,
        "cmp_w1": jnp.stack([cmp_k_w1[l], cmp_v_w1[l]]).astype(BF16),
        "cmp_w2": jnp.stack([cmp_k_w2[l], cmp_v_w2[l]]).astype(BF16),
        "fox_bias": jnp.broadcast_to(fox_f_bias[l][:, None, None], (FOX_HEADS, 1, LANES)),
        "wn": pad_read_nsa(w_read_nsa[l]),
        "wr": w_read_ret[l].astype(BF16),
        "wf": w_read_fox[l].astype(BF16),
        "wo": w_out[l].astype(BF16),
    }


def kernel(x, c, ada_w, ada_b, norm_mix, norm_ffn, w_in, cmp_k_pe, cmp_k_w1, cmp_k_w2, cmp_v_pe, cmp_v_w1,
           cmp_v_w2, fox_f_bias, w_read_nsa, w_read_ret, w_read_fox, w_out, ffn_w1, ffn_w3, ffn_w2, router_w,
           moe_w1, moe_w3, moe_w2, final_norm_w):
    B, T, D = x.shape
    M = B * T
    depth = ada_w.shape[0]
    mod = modulation(c, ada_w, ada_b)
    cos_t, sin_t = rope_tables(T)
    ov_t, e_mat = nsa_constants(T)
    consts = (cos_t, sin_t, ov_t, e_mat, retention_consts(), nsa_gate_expand())
    xs = x.reshape(M, D)
    for l in range(depth):
        lw = layer_weights(l, norm_mix, w_in, cmp_k_pe, cmp_k_w1, cmp_k_w2, cmp_v_pe, cmp_v_w1, cmp_v_w2,
                           fox_f_bias, w_read_nsa, w_read_ret, w_read_fox, w_out)
        xs = token_mixing(xs, mod[l], lw, consts, B, T)
        nf = norm_ffn[l].reshape(1, D)
        if l % 2 == 0:
            k = l // 2
            xs = ffn(xs, mod[l], nf, ffn_w1[k][None].astype(BF16), ffn_w3[k][None].astype(BF16),
                     ffn_w2[k][None].astype(BF16), None, T, tm=512, tf=D_FF // 2)
        else:
            k = l // 2
            xs = moe_ffn(xs, mod[l], nf, router_w[k], moe_w1[k], moe_w3[k], moe_w2[k].astype(BF16), T)
    return final_norm(xs, final_norm_w.reshape(1, D)).reshape(B, T, D)
```

```python
import functools
import math

import jax
import jax.numpy as jnp
import numpy as np
from jax import lax
from jax.experimental import pallas as pl
from jax.experimental.pallas import tpu as pltpu
from jax.experimental.pallas import tpu_sc as plsc

F32 = jnp.float32
BF16 = jnp.bfloat16

D_MODEL = 1024
DEPTH = 2
HEAD_DIM = 64
ROPE_THETA = 10000.0
NORM_EPS = 1e-6
NEG_INF = -1e30
REMOVED = -3e38

NSA_HEADS = 8
NSA_GROUPS = 2
NSA_HPG = NSA_HEADS // NSA_GROUPS
CMP_LEN = 32
CMP_STRIDE = 16
CMP_HIDDEN = 256
SEL_LEN = 64
SEL_TOPN = 16
WINDOW = 512
FORCE_SCORE = 1e4
NSA_QBLOCK = 128

RET_HEADS = 4
RET_QK_DIM = 64
RET_V_DIM = 128
RET_CHUNK = 128

FOX_HEADS = 8
FOX_TQ = 1024
LOG2E = 1.4426950408889634

D_FF = 2816
N_EXPERTS = 8
D_FF_EXPERT = 3584

LANES = 128
VMEM_LIMIT = 56 * 1024 * 1024

P1_NQ = 0
P1_RQ = 1024
P1_RK = 1536
P1_NKC = 1792
P1_NKS = 1920
P1_NKW = 2048
P1_COLS = 2176
P2_MG = 0
P2_RV = 3072
P2_RG = 3584
P2_FQ = 4096
P2_FK = 5120
P2_FV = 5632
P2_NVC = 6144
P2_NVS = 6272
P2_NVW = 6400
P2_SMALL = 6528
P2_COLS = 6656


def _cparams(*sem):
    return pltpu.CompilerParams(dimension_semantics=tuple(sem), vmem_limit_bytes=VMEM_LIMIT)


def _sigmoid(x):
    return 1.0 / (1.0 + jnp.exp(-x))


def _dot(a, b):
    return jnp.dot(a, b, preferred_element_type=F32)


def _dot_nt(a, b):
    return lax.dot_general(a, b, (((1,), (1,)), ((), ())), preferred_element_type=F32)


def _dot_tn(a, b):
    return lax.dot_general(a, b, (((0,), (0,)), ((), ())), preferred_element_type=F32)


def _split3(x):
    hi = x.astype(BF16)
    r1 = x - hi.astype(F32)
    mid = r1.astype(BF16)
    lo = (r1 - mid.astype(F32)).astype(BF16)
    return hi, mid, lo


def _norm_mod(x, nw, sc, sh):
    ms = jnp.mean(x * x, axis=-1, keepdims=True)
    y = x * lax.rsqrt(ms + NORM_EPS) * nw
    return y * (1.0 + sc) + sh


def _mod_kernel(c_ref, w_ref, b_ref, o_ref):
    c = c_ref[...]
    s = c * _sigmoid(c)
    o_ref[0] = _dot(s.astype(BF16), w_ref[0].astype(BF16)) + b_ref[0]


def modulation(c, ada_w, ada_b):
    B, D = c.shape
    depth = ada_w.shape[0]
    rows = 8
    c_pad = jnp.zeros((rows, D), F32).at[:B].set(c)
    out = pl.pallas_call(
        _mod_kernel,
        grid=(depth, 6),
        in_specs=[pl.BlockSpec((rows, D), lambda l, j: (0, 0)),
                  pl.BlockSpec((1, D, D), lambda l, j: (l, 0, j)),
                  pl.BlockSpec((1, 1, D), lambda l, j: (l, 0, j))],
        out_specs=pl.BlockSpec((1, rows, D), lambda l, j: (l, 0, j)),
        out_shape=jax.ShapeDtypeStruct((depth, rows, 6 * D), F32),
        compiler_params=_cparams("parallel", "parallel"),
        name="modulation",
    )(c_pad, ada_w, ada_b.reshape(depth, 1, 6 * D))
    return out[:, :B].reshape(depth, B, 6, 1, D)


def _proj_plain_kernel(x_ref, nw_ref, sc_ref, sh_ref, w_ref, o_ref, h_ref):
    @pl.when(pl.program_id(1) == 0)
    def _():
        h_ref[...] = _norm_mod(x_ref[...], nw_ref[...], sc_ref[...], sh_ref[...]).astype(BF16)

    o_ref[...] = _dot(h_ref[...], w_ref[...]).astype(o_ref.dtype)


def _proj_rope_kernel(x_ref, nw_ref, sc_ref, sh_ref, w_ref, cos_ref, sin_ref, o_ref, *, scales):
    h = _norm_mod(x_ref[...], nw_ref[...], sc_ref[...], sh_ref[...]).astype(BF16)
    y = _dot(h, w_ref[...])
    cos = cos_ref[...]
    sin = sin_ref[...]
    lane = lax.broadcasted_iota(jnp.int32, cos.shape, 1)
    first_half = (lane % HEAD_DIM) < (HEAD_DIM // 2)
    for g, scale in enumerate(scales):
        yg = y[:, g * LANES:(g + 1) * LANES]
        rot = jnp.where(first_half, pltpu.roll(yg, LANES - HEAD_DIM // 2, 1),
                        pltpu.roll(yg, HEAD_DIM // 2, 1))
        r = yg * cos + rot * sin
        if scale != 1.0:
            r = r * scale
        o_ref[:, g * LANES:(g + 1) * LANES] = r.astype(o_ref.dtype)


def _mod_specs(T, tm, sc_idx, sh_idx, nargs):
    per_b = T // tm
    if nargs == 1:
        return [pl.BlockSpec((None, None, 1, D_MODEL), lambda i: (i // per_b, sc_idx, 0, 0)),
                pl.BlockSpec((None, None, 1, D_MODEL), lambda i: (i // per_b, sh_idx, 0, 0))]
    return [pl.BlockSpec((None, None, 1, D_MODEL), lambda i, j: (i // per_b, sc_idx, 0, 0)),
            pl.BlockSpec((None, None, 1, D_MODEL), lambda i, j: (i // per_b, sh_idx, 0, 0))]


def proj_plain(x, mod_l, nw, w, T, *, tm=1024, tn=512):
    M, D = x.shape
    N = w.shape[1]
    return pl.pallas_call(
        _proj_plain_kernel,
        grid=(M // tm, N // tn),
        in_specs=[pl.BlockSpec((tm, D), lambda i, j: (i, 0)),
                  pl.BlockSpec((1, D), lambda i, j: (0, 0))]
        + _mod_specs(T, tm, 1, 0, 2)
        + [pl.BlockSpec((D, tn), lambda i, j: (0, j))],
        out_specs=pl.BlockSpec((tm, tn), lambda i, j: (i, j)),
        out_shape=jax.ShapeDtypeStruct((M, N), BF16),
        scratch_shapes=[pltpu.VMEM((tm, D), BF16)],
        compiler_params=_cparams("parallel", "arbitrary"),
        name="proj_plain",
    )(x, nw, mod_l, mod_l, w)


def proj_rope(x, mod_l, nw, w, cos, sin, scales, T, *, tm=512):
    M, D = x.shape
    N = w.shape[1]
    per_b = T // tm
    return pl.pallas_call(
        functools.partial(_proj_rope_kernel, scales=scales),
        grid=(M // tm,),
        in_specs=[pl.BlockSpec((tm, D), lambda i: (i, 0)),
                  pl.BlockSpec((1, D), lambda i: (0, 0))]
        + _mod_specs(T, tm, 1, 0, 1)
        + [pl.BlockSpec((D, N), lambda i: (0, 0)),
           pl.BlockSpec((tm, LANES), lambda i: (i % per_b, 0)),
           pl.BlockSpec((tm, LANES), lambda i: (i % per_b, 0))],
        out_specs=pl.BlockSpec((tm, N), lambda i: (i, 0)),
        out_shape=jax.ShapeDtypeStruct((M, N), BF16),
        compiler_params=_cparams("parallel"),
        name="proj_rope",
    )(x, nw, mod_l, mod_l, w, cos, sin)


def rope_tables(T):
    d = HEAD_DIM
    pos = jnp.arange(T, dtype=F32)
    inv = ROPE_THETA ** (-jnp.arange(0, d, 2, dtype=F32) / d)
    ang = pos[:, None] * inv[None, :]
    cos = jnp.cos(ang)
    sin = jnp.sin(ang)
    cos_t = jnp.concatenate([cos, cos, cos, cos], axis=-1)
    sin_t = jnp.concatenate([-sin, sin, -sin, sin], axis=-1)
    return cos_t, sin_t


def _pad_heads(w, n_heads, half_of_head):
    D = w.shape[0]
    w = w.reshape(D, n_heads, HEAD_DIM)
    z = jnp.zeros_like(w)
    halves = np.array([half_of_head(h) for h in range(n_heads)])
    lo = jnp.where(halves[None, :, None] == 0, w, z)
    hi = jnp.where(halves[None, :, None] == 1, w, z)
    return jnp.concatenate([lo, hi], axis=-1).reshape(D, n_heads * LANES)


def split_w_in(w_in):
    sizes = [512, 128, 128, 128, 128, 128, 128, 24, 256, 256, 512, 512, 512, 512, 512, 8, 3072]
    offs = np.cumsum([0] + sizes)
    (nq, nkc, nvc, nks, nvs, nkw, nvw, ngate, rq, rk, rv, rg, fq, fk, fv, ff, mg) = [
        w_in[:, offs[i]:offs[i + 1]] for i in range(len(sizes))]
    D = w_in.shape[0]
    nq_p = _pad_heads(nq, NSA_HEADS, lambda h: h // NSA_HPG)
    rq_p = _pad_heads(rq, RET_HEADS, lambda h: h % 2)
    fq_p = _pad_heads(fq, FOX_HEADS, lambda h: 0) * (HEAD_DIM ** -0.5 * LOG2E)
    small = jnp.concatenate([ngate, ff, jnp.zeros((D, LANES - 32), w_in.dtype)], axis=-1)
    w1 = jnp.concatenate([nq_p, rq_p, rk, nkc, nks, nkw], axis=-1).astype(BF16)
    w2 = jnp.concatenate([mg, rv, rg, fq_p, fk, fv, nvc, nvs, nvw, small], axis=-1).astype(BF16)
    assert w1.shape[1] == P1_COLS and w2.shape[1] == P2_COLS
    return w1, w2


def p1_scales():
    s = [1.0] * (P1_COLS // LANES)
    for g in range(P1_NQ // LANES, P1_RQ // LANES):
        s[g] = HEAD_DIM ** -0.5 * LOG2E
    for g in range(P1_RK // LANES, P1_NKC // LANES):
        s[g] = RET_QK_DIM ** -0.5
    return tuple(s)


def _compress_kernel(x_ref, pe_ref, w1_ref, w2_ref, o_ref):
    r = x_ref[...]
    half = r.shape[1]
    w1 = w1_ref[...]
    a = _dot(r, w1[:half])
    b = _dot(r, w1[half:])
    pe = _dot(pe_ref[...], w1)[0:1]
    n = a.shape[0]
    hid = a + pltpu.roll(b, n - 1, 0) + pe
    hid = hid * _sigmoid(hid)
    o_ref[...] = _dot(hid.astype(BF16), w2_ref[...]).astype(o_ref.dtype)


def compress(xr, pe, w1, w2):
    _, B, G, R, W = xr.shape
    H = w1.shape[-1]
    return pl.pallas_call(
        _compress_kernel,
        grid=(2, B, G),
        in_specs=[pl.BlockSpec((None, None, None, R, W), lambda s, b, g: (s, b, g, 0, 0)),
                  pl.BlockSpec((None, 8, 2 * W), lambda s, b, g: (s, 0, 0)),
                  pl.BlockSpec((None, 2 * W, H), lambda s, b, g: (s, 0, 0)),
                  pl.BlockSpec((None, H, HEAD_DIM), lambda s, b, g: (s, 0, 0))],
        out_specs=pl.BlockSpec((None, None, None, R, HEAD_DIM), lambda s, b, g: (s, b, g, 0, 0)),
        out_shape=jax.ShapeDtypeStruct((2, B, G, R, HEAD_DIM), BF16),
        compiler_params=_cparams("parallel", "parallel", "parallel"),
        name="nsa_compress",
    )(xr, pe, w1, w2)


def _stack_heads(q_ref, g):
    return jnp.concatenate(
        [q_ref[:, (NSA_HPG * g + hh) * LANES:(NSA_HPG * g + hh + 1) * LANES] for hh in range(NSA_HPG)],
        axis=0)


def _store_heads(o_ref, g, o, tq):
    for hh in range(NSA_HPG):
        h = NSA_HPG * g + hh
        o_ref[:, h * LANES:(h + 1) * LANES] = o[hh * tq:(hh + 1) * tq].astype(o_ref.dtype)


def _nsa_cmp_kernel(q_ref, kc_ref, vc_ref, ov_ref, o_ref, m_ref, *, tq, n_sel, top_n):
    t0 = pl.program_id(1) * tq
    kc = kc_ref[...]
    vc = vc_ref[...]
    ncp = kc.shape[0]
    nsp = ov_ref.shape[0]
    rows = NSA_HPG * tq
    n_idx = lax.broadcasted_iota(jnp.int32, (rows, ncp), 1)
    t_idx = t0 + lax.broadcasted_iota(jnp.int32, (rows, ncp), 0) % tq
    valid = (n_idx * CMP_STRIDE + (CMP_LEN - 1)) <= t_idx
    j_idx = lax.broadcasted_iota(jnp.int32, (nsp, tq), 0)
    cur = (t0 + lax.broadcasted_iota(jnp.int32, (nsp, tq), 1)) // SEL_LEN
    forced = (j_idx == 0) | (j_idx == cur) | (j_idx == cur - 1)
    j_f = j_idx.astype(F32)
    for g in range(NSA_GROUPS):
        q = _stack_heads(q_ref, g)
        s = jnp.where(valid, _dot_nt(q, kc), NEG_INF)
        m = jnp.max(s, axis=-1, keepdims=True)
        e = jnp.where(valid, jnp.exp2(s - m), 0.0)
        l = jnp.sum(e, axis=-1, keepdims=True)
        p = e / jnp.where(l > 0.0, l, 1.0)
        _store_heads(o_ref, g, _dot(p.astype(BF16), vc), tq)
        psum = p[0:tq]
        for hh in range(1, NSA_HPG):
            psum = psum + p[hh * tq:(hh + 1) * tq]
        imp_t = _dot_nt(ov_ref[...], psum.astype(BF16))
        score = jnp.where(forced, FORCE_SCORE, imp_t)
        score = jnp.where(j_idx <= cur, score, NEG_INF)
        score = jnp.where(j_idx < n_sel, score, REMOVED)
        sel = jnp.zeros((nsp, tq), F32)
        for _ in range(top_n):
            mx = jnp.max(score, axis=0, keepdims=True)
            idx = jnp.min(jnp.where(score == mx, j_f, float(nsp)), axis=0, keepdims=True)
            hit = j_f == idx
            sel = jnp.where(hit, 1.0, sel)
            score = jnp.where(hit, REMOVED, score)
        sel = jnp.where(j_idx <= cur, sel, 0.0)
        m_ref[g] = sel.T.astype(m_ref.dtype)


def nsa_cmp_select(p1, kc, vc, ov_t, T):
    B = p1.shape[0]
    tq = NSA_QBLOCK
    ncp = kc.shape[1]
    nsp = ov_t.shape[0]
    n_sel = T // SEL_LEN
    return pl.pallas_call(
        functools.partial(_nsa_cmp_kernel, tq=tq, n_sel=n_sel, top_n=min(SEL_TOPN, n_sel)),
        grid=(B, T // tq),
        in_specs=[pl.BlockSpec((None, tq, NSA_HEADS * LANES), lambda b, i: (b, i, 0)),
                  pl.BlockSpec((None, ncp, LANES), lambda b, i: (b, 0, 0)),
                  pl.BlockSpec((None, ncp, LANES), lambda b, i: (b, 0, 0)),
                  pl.BlockSpec((nsp, ncp), lambda b, i: (0, 0))],
        out_specs=[pl.BlockSpec((None, tq, NSA_HEADS * LANES), lambda b, i: (b, i, 0)),
                   pl.BlockSpec((None, NSA_GROUPS, tq, nsp), lambda b, i: (b, 0, i, 0))],
        out_shape=[jax.ShapeDtypeStruct((B, T, NSA_HEADS * LANES), BF16),
                   jax.ShapeDtypeStruct((B, NSA_GROUPS, T, nsp), BF16)],
        compiler_params=_cparams("parallel", "parallel"),
        name="nsa_cmp_select",
    )(p1, kc, vc, ov_t)


SEL_BONUS = 8192.0
NSA_SEL_TQ = 256
NSA_SEL_TK = 1024


def _nsa_sel_kernel(q_ref, k_ref, v_ref, m_ref, et_ref, o_ref, *, tq, tk):
    t0 = pl.program_id(1) * tq
    n_tiles = (t0 + tq + tk - 1) // tk
    rows = NSA_HPG * tq
    for g in range(NSA_GROUPS):
        q = jnp.concatenate([_stack_heads(q_ref, g), jnp.concatenate([m_ref[g]] * NSA_HPG, axis=0)], axis=1)
        den = HEAD_DIM * (1 - g)

        def step(j, carry, masked, q=q, g=g):
            m, acc = carry
            start = pl.multiple_of(j * tk, tk)
            ks = jnp.concatenate([k_ref[pl.ds(start, tk), :], et_ref[pl.ds(start, tk), :]], axis=1)
            s = _dot_nt(q, ks)
            if masked:
                trow = t0 + lax.broadcasted_iota(jnp.int32, (rows, tk), 0) % tq
                kpos = start + lax.broadcasted_iota(jnp.int32, (rows, tk), 1)
                s = jnp.where(kpos <= trow, s, NEG_INF)
            m_new = jnp.maximum(m, jnp.max(s, axis=-1, keepdims=True))
            p = jnp.exp2(s - m_new)
            acc = jnp.exp2(m - m_new) * acc + _dot(p.astype(BF16), v_ref[g, pl.ds(start, tk), :])
            return m_new, acc

        init = (jnp.full((rows, 1), NEG_INF, F32), jnp.zeros((rows, LANES), F32))
        carry = lax.fori_loop(0, n_tiles - 1, functools.partial(step, masked=False), init)
        _, acc = step(n_tiles - 1, carry, True)
        _store_heads(o_ref, g, acc / acc[:, den:den + 1], tq)


def nsa_value_augment(v):
    ones = jnp.ones_like(v[..., :HEAD_DIM])
    return jnp.stack([jnp.concatenate([v[..., :HEAD_DIM], ones], axis=-1),
                      jnp.concatenate([ones, v[..., HEAD_DIM:]], axis=-1)], axis=1)


def nsa_selected(p1, v_aug, sel, et_mat, T, *, tq=NSA_SEL_TQ, tk=NSA_SEL_TK):
    B = p1.shape[0]
    nsp = sel.shape[-1]
    return pl.pallas_call(
        functools.partial(_nsa_sel_kernel, tq=tq, tk=tk),
        grid=(B, T // tq),
        in_specs=[pl.BlockSpec((None, tq, NSA_HEADS * LANES), lambda b, i: (b, i, 0)),
                  pl.BlockSpec((None, T, LANES), lambda b, i: (b, 0, P1_NKS // LANES)),
                  pl.BlockSpec((None, NSA_GROUPS, T, LANES), lambda b, i: (b, 0, 0, 0)),
                  pl.BlockSpec((None, NSA_GROUPS, tq, nsp), lambda b, i: (b, 0, i, 0)),
                  pl.BlockSpec((T, nsp), lambda b, i: (0, 0))],
        out_specs=pl.BlockSpec((None, tq, NSA_HEADS * LANES), lambda b, i: (b, i, 0)),
        out_shape=jax.ShapeDtypeStruct((B, T, NSA_HEADS * LANES), BF16),
        compiler_params=_cparams("parallel", "parallel"),
        name="nsa_selected",
    )(p1, p1, v_aug, sel, et_mat)


def _nsa_win_kernel(q_ref, k_ref, v_ref, b_ref, o_ref, *, tq):
    t0 = pl.program_id(1) * tq
    span = WINDOW + tq
    start = pl.multiple_of(jnp.maximum(t0 - WINDOW, 0), tq)
    ks = k_ref[pl.ds(start, span), :]
    vs = v_ref[pl.ds(start, span), :]

    def run(bias):
        bias = jnp.concatenate([bias] * NSA_HPG, axis=0)
        for g in range(NSA_GROUPS):
            s = _dot_nt(_stack_heads(q_ref, g), ks) + bias
            m = jnp.max(s, axis=-1, keepdims=True)
            p = jnp.exp2(s - m)
            l = jnp.sum(p, axis=-1, keepdims=True)
            _store_heads(o_ref, g, _dot(p.astype(BF16), vs) / l, tq)

    @pl.when(t0 >= WINDOW)
    def _():
        run(b_ref[...])

    @pl.when(t0 < WINDOW)
    def _():
        row = lax.broadcasted_iota(jnp.int32, (tq, span), 0)
        col = lax.broadcasted_iota(jnp.int32, (tq, span), 1)
        run(jnp.where(col <= t0 + row, 0.0, NEG_INF))


def nsa_window(p1, p2, T):
    B = p1.shape[0]
    tq = NSA_QBLOCK
    span = WINDOW + tq
    r = np.arange(tq)[:, None]
    c = np.arange(span)[None, :]
    band = jnp.asarray(np.where((c > r) & (c <= r + WINDOW), 0.0, NEG_INF), F32)
    return pl.pallas_call(
        functools.partial(_nsa_win_kernel, tq=tq),
        grid=(B, T // tq),
        in_specs=[pl.BlockSpec((None, tq, NSA_HEADS * LANES), lambda b, i: (b, i, 0)),
                  pl.BlockSpec((None, T, LANES), lambda b, i: (b, 0, P1_NKW // LANES)),
                  pl.BlockSpec((None, T, LANES), lambda b, i: (b, 0, P2_NVW // LANES)),
                  pl.BlockSpec((tq, span), lambda b, i: (0, 0))],
        out_specs=pl.BlockSpec((None, tq, NSA_HEADS * LANES), lambda b, i: (b, i, 0)),
        out_shape=jax.ShapeDtypeStruct((B, T, NSA_HEADS * LANES), BF16),
        compiler_params=_cparams("parallel", "parallel"),
        name="nsa_window",
    )(p1, p1, p2, band)


def _retention_kernel(q_ref, k_ref, v_ref, g_ref, din_ref, qd_ref, kd_ref, cd_ref, o_ref, st_ref):
    @pl.when(pl.program_id(0) == 0)
    def _():
        st_ref[...] = jnp.zeros_like(st_ref)

    B = q_ref.shape[0]
    for b in range(B):
        for h in range(RET_HEADS):
            lanes = slice(h * LANES, (h + 1) * LANES)
            qh = q_ref[b, :, lanes]
            kp = k_ref[b, :, (h // 2) * LANES:(h // 2 + 1) * LANES]
            vh = v_ref[b, :, lanes]
            st = st_ref[b, h]
            inner = _dot_nt(qh, kp) * din_ref[h]
            o = _dot(inner.astype(BF16), vh) + _dot(qh, st.astype(BF16)) * qd_ref[h]
            kd = (kp.astype(F32) * kd_ref[h]).astype(BF16)
            st_ref[b, h] = st * cd_ref[h, 0:1, :] + _dot_tn(kd, vh)
            mu = jnp.mean(o, axis=-1, keepdims=True)
            d = o - mu
            var = jnp.mean(d * d, axis=-1, keepdims=True)
            on = d * lax.rsqrt(var + NORM_EPS)
            gh = g_ref[b, :, lanes].astype(F32)
            o_ref[b, :, lanes] = (gh * _sigmoid(gh) * on).astype(o_ref.dtype)


def retention_consts():
    C = RET_CHUNK
    H = RET_HEADS
    log_g = jnp.log(1.0 - 2.0 ** (-5.0 - jnp.arange(H, dtype=F32)))
    n = jnp.arange(C, dtype=F32)
    diff = n[:, None] - n[None, :]
    causal = diff >= 0
    decay_in = jnp.where(causal[None], jnp.exp(jnp.where(causal, diff, 0.0)[None] * log_g[:, None, None]), 0.0)
    q_decay = jnp.exp((n[None, :] + 1.0) * log_g[:, None])
    k_decay = jnp.exp((C - 1.0 - n)[None, :] * log_g[:, None])
    chunk_decay = jnp.exp(C * log_g)
    qd = jnp.broadcast_to(q_decay[:, :, None], (H, C, LANES))
    kd = jnp.broadcast_to(k_decay[:, :, None], (H, C, LANES))
    cd = jnp.broadcast_to(chunk_decay[:, None, None], (H, 8, LANES))
    return decay_in, qd, kd, cd


def retention(p1, p2, consts, T):
    B = p1.shape[0]
    C = RET_CHUNK
    din, qd, kd, cd = consts
    W = RET_HEADS * LANES
    full = lambda shape: pl.BlockSpec(shape, lambda c: (0,) * len(shape))
    return pl.pallas_call(
        _retention_kernel,
        grid=(T // C,),
        in_specs=[pl.BlockSpec((B, C, W), lambda c: (0, c, P1_RQ // W)),
                  pl.BlockSpec((B, C, W // 2), lambda c: (0, c, P1_RK // (W // 2))),
                  pl.BlockSpec((B, C, W), lambda c: (0, c, P2_RV // W)),
                  pl.BlockSpec((B, C, W), lambda c: (0, c, P2_RG // W)),
                  full(din.shape), full(qd.shape), full(kd.shape), full(cd.shape)],
        out_specs=pl.BlockSpec((B, C, W), lambda c: (0, c, 0)),
        out_shape=jax.ShapeDtypeStruct((B, T, W), BF16),
        scratch_shapes=[pltpu.VMEM((B, RET_HEADS, LANES, LANES), F32)],
        compiler_params=_cparams("arbitrary"),
        name="retention",
    )(p1, p1, p2, p2, din, qd, kd, cd)


def _fox_cum_kernel(f_ref, b_ref, o_ref):
    x = f_ref[...] + b_ref[...]
    ls = jnp.minimum(x, 0.0) - jnp.log1p(jnp.exp(-jnp.abs(x)))
    R = x.shape[0]
    ki = lax.broadcasted_iota(jnp.int32, (LANES, LANES), 0)
    ji = lax.broadcasted_iota(jnp.int32, (LANES, LANES), 1)
    upper = jnp.where(ki <= ji, 1.0, 0.0).astype(BF16)
    hi, mid, lo = _split3(ls)
    rowcum = _dot(hi, upper) + _dot(mid, upper) + _dot(lo, upper)
    tot = jnp.broadcast_to(rowcum[:, LANES - 1:LANES], (R, LANES))
    ri = lax.broadcasted_iota(jnp.int32, (R, R), 0)
    ci = lax.broadcasted_iota(jnp.int32, (R, R), 1)
    lower = jnp.where(ci < ri, 1.0, 0.0).astype(BF16)
    hi, mid, lo = _split3(tot)
    offs = _dot(lower, hi) + _dot(lower, mid) + _dot(lower, lo)
    o_ref[...] = (rowcum + offs) * LOG2E


def fox_cum(f_logit, bias):
    B, H, R, _ = f_logit.shape
    return pl.pallas_call(
        _fox_cum_kernel,
        grid=(B, H),
        in_specs=[pl.BlockSpec((None, None, R, LANES), lambda b, h: (b, h, 0, 0)),
                  pl.BlockSpec((None, 1, LANES), lambda b, h: (h, 0, 0))],
        out_specs=pl.BlockSpec((None, None, R, LANES), lambda b, h: (b, h, 0, 0)),
        out_shape=jax.ShapeDtypeStruct((B, H, R, LANES), F32),
        compiler_params=_cparams("parallel", "parallel"),
        name="fox_cum",
    )(f_logit, bias)


FOX_BIAS_LANES = 3


def _fox_kernel(q_ref, k_ref, v_ref, c_ref, o_ref, ka_ref, va_ref, *, tq):
    i = pl.program_id(2)
    tk = tq
    T = k_ref.shape[0]
    chunk = 512

    @pl.when(i == 0)
    def _():
        lane = lax.broadcasted_iota(jnp.int32, (chunk, LANES), 1)
        ri = lax.broadcasted_iota(jnp.int32, (16, LANES), 0)
        ci = lax.broadcasted_iota(jnp.int32, (16, LANES), 1)
        place = jnp.where((ci == ri + HEAD_DIM) & (ri < FOX_BIAS_LANES), 1.0, 0.0).astype(BF16)

        def build(c, _):
            c0 = pl.multiple_of(c * chunk, chunk)
            kp = k_ref[pl.ds(c0, chunk), :].astype(F32)
            vp = v_ref[pl.ds(c0, chunk), :].astype(F32)
            for hh in range(2):
                hi, mid, lo = _split3(-c_ref[hh, :, pl.ds(c0, chunk)])
                terms = jnp.concatenate([hi, mid, lo, jnp.zeros((13, chunk), BF16)], axis=0)
                bias = _dot_tn(terms, place)
                kh = kp if hh == 0 else pltpu.roll(kp, HEAD_DIM, 1)
                vh = vp if hh == 0 else pltpu.roll(vp, HEAD_DIM, 1)
                ka_ref[hh, pl.ds(c0, chunk), :] = jnp.where(lane < HEAD_DIM, kh, bias).astype(BF16)
                va_ref[hh, pl.ds(c0, chunk), :] = jnp.where(lane < HEAD_DIM, vh, 1.0).astype(BF16)
            return 0

        lax.fori_loop(0, T // chunk, build, 0)

    row = lax.broadcasted_iota(jnp.int32, (tq, tk), 0)
    col = lax.broadcasted_iota(jnp.int32, (tq, tk), 1)
    lane = lax.broadcasted_iota(jnp.int32, (tq, LANES), 1)
    ones_lanes = (lane >= HEAD_DIM) & (lane < HEAD_DIM + FOX_BIAS_LANES)
    qs = [jnp.where(ones_lanes, 1.0, q_ref[:, hh * LANES:(hh + 1) * LANES].astype(F32)).astype(BF16)
          for hh in range(2)]

    def step(j, carry, masked):
        start = pl.multiple_of(j * tk, tk)
        out = []
        for hh in range(2):
            m, acc = carry[hh]
            s = _dot_nt(qs[hh], ka_ref[hh, pl.ds(start, tk), :])
            if masked:
                s = jnp.where(col <= row, s, NEG_INF)
            m_new = jnp.maximum(m, jnp.max(s, axis=-1, keepdims=True))
            p = jnp.exp2(s - m_new)
            acc = jnp.exp2(m - m_new) * acc + _dot(p.astype(BF16), va_ref[hh, pl.ds(start, tk), :])
            out.append((m_new, acc))
        return tuple(out)

    one = (jnp.full((tq, 1), NEG_INF, F32), jnp.zeros((tq, LANES), F32))
    carry = lax.fori_loop(0, i, functools.partial(step, masked=False), (one, one))
    (_, acc0), (_, acc1) = step(i, carry, True)
    o0 = acc0 / acc0[:, HEAD_DIM:HEAD_DIM + 1]
    o1 = acc1 / acc1[:, HEAD_DIM:HEAD_DIM + 1]
    o_ref[...] = jnp.where(lane < HEAD_DIM, o0, pltpu.roll(o1, HEAD_DIM, 1)).astype(o_ref.dtype)


def fox_attention(p2, cum, T, *, tq=FOX_TQ):
    B = p2.shape[0]
    HP = FOX_HEADS // 2
    return pl.pallas_call(
        functools.partial(_fox_kernel, tq=tq),
        grid=(B, HP, T // tq),
        in_specs=[pl.BlockSpec((None, tq, 2 * LANES), lambda b, h, i: (b, i, P2_FQ // (2 * LANES) + h)),
                  pl.BlockSpec((None, T, LANES), lambda b, h, i: (b, 0, P2_FK // LANES + h)),
                  pl.BlockSpec((None, T, LANES), lambda b, h, i: (b, 0, P2_FV // LANES + h)),
                  pl.BlockSpec((None, None, 2, 1, T), lambda b, h, i: (b, h, 0, 0, 0))],
        out_specs=pl.BlockSpec((None, tq, LANES), lambda b, h, i: (b, i, h)),
        out_shape=jax.ShapeDtypeStruct((B, T, FOX_HEADS * HEAD_DIM), BF16),
        scratch_shapes=[pltpu.VMEM((2, T, LANES), BF16), pltpu.VMEM((2, T, LANES), BF16)],
        compiler_params=_cparams("parallel", "parallel", "arbitrary"),
        name="fox_attention",
    )(p2, p2, p2, cum)


def _readout_kernel(ocmp_ref, osel_ref, owin_ref, small_ref, oret_ref, ofox_ref, mg_ref, x_ref, g1_ref,
                    ex_ref, wn_ref, wr_ref, wf_ref, wo_ref, o_ref):
    W = NSA_HEADS * LANES
    gs = _sigmoid(small_ref[...].astype(F32)).astype(BF16)
    ge = _dot(gs, ex_ref[...])
    onsa = (ge[:, :W] * ocmp_ref[...].astype(F32) + ge[:, W:2 * W] * osel_ref[...].astype(F32)
            + ge[:, 2 * W:] * owin_ref[...].astype(F32))
    D = D_MODEL
    merged = (_sigmoid(mg_ref[:, :D].astype(F32)) * _dot(onsa.astype(BF16), wn_ref[...])
              + _sigmoid(mg_ref[:, D:2 * D].astype(F32)) * _dot(oret_ref[...], wr_ref[...])
              + _sigmoid(mg_ref[:, 2 * D:].astype(F32)) * _dot(ofox_ref[...], wf_ref[...]))
    y = _dot(merged.astype(BF16), wo_ref[...])
    o_ref[...] = x_ref[...] + g1_ref[...] * y


def readout(o_cmp, o_sel, o_win, p2, o_ret, o_fox, x, mod_l, ex, wn, wr, wf, wo, T, *, tm=512):
    M, D = x.shape
    per_b = T // tm
    W = NSA_HEADS * LANES
    row = lambda width, col=0: pl.BlockSpec((tm, width), lambda i: (i, col))
    full = lambda a: pl.BlockSpec(a.shape, lambda i: (0,) * a.ndim)
    return pl.pallas_call(
        _readout_kernel,
        grid=(M // tm,),
        in_specs=[row(W), row(W), row(W), row(LANES, P2_SMALL // LANES), row(512), row(512),
                  row(3 * D, 0), row(D),
                  pl.BlockSpec((None, None, 1, D), lambda i: (i // per_b, 2, 0, 0)),
                  full(ex), full(wn), full(wr), full(wf), full(wo)],
        out_specs=row(D),
        out_shape=jax.ShapeDtypeStruct((M, D), F32),
        compiler_params=_cparams("parallel"),
        name="mixer_readout",
    )(o_cmp, o_sel, o_win, p2, o_ret, o_fox, p2, x, mod_l, ex, wn, wr, wf, wo)


def nsa_gate_expand():
    ex = np.zeros((LANES, 3 * NSA_HEADS * LANES), np.float32)
    for br in range(3):
        for h in range(NSA_HEADS):
            c0 = br * NSA_HEADS * LANES + h * LANES
            ex[br * NSA_HEADS + h, c0:c0 + LANES] = 1.0
    return jnp.asarray(ex, BF16)


def pad_read_nsa(w):
    D = w.shape[1]
    w = w.reshape(NSA_HEADS, HEAD_DIM, D)
    z = jnp.zeros_like(w)
    g = (np.arange(NSA_HEADS) // NSA_HPG)[:, None, None]
    lo = jnp.where(g == 0, w, z)
    hi = jnp.where(g == 1, w, z)
    return jnp.concatenate([lo, hi], axis=1).reshape(NSA_HEADS * LANES, D).astype(BF16)


def _ffn_kernel(*refs, gated):
    if gated:
        x_ref, nw_ref, sc_ref, sh_ref, g2_ref, gate_ref, w1_ref, w3_ref, w2_ref, o_ref, h_ref, acc_ref = refs
    else:
        x_ref, nw_ref, sc_ref, sh_ref, g2_ref, w1_ref, w3_ref, w2_ref, o_ref, h_ref, acc_ref = refs
    e = pl.program_id(1)
    f = pl.program_id(2)

    @pl.when((e == 0) & (f == 0))
    def _():
        h_ref[...] = _norm_mod(x_ref[...], nw_ref[...], sc_ref[...], sh_ref[...]).astype(BF16)
        acc_ref[...] = jnp.zeros_like(acc_ref)

    h = h_ref[...]
    u = _dot(h, w1_ref[...])
    v = _dot(h, w3_ref[...])
    a = (u * _sigmoid(u) * v).astype(BF16)
    y = _dot(a, w2_ref[...])
    if gated:
        gate = gate_ref[...]
        lane = lax.broadcasted_iota(jnp.int32, gate.shape, 1)
        y = y * jnp.sum(jnp.where(lane == e, gate, 0.0), axis=-1, keepdims=True)
    acc_ref[...] += y

    @pl.when((e == pl.num_programs(1) - 1) & (f == pl.num_programs(2) - 1))
    def _():
        o_ref[...] = x_ref[...] + g2_ref[...] * acc_ref[...]


def ffn(x, mod_l, nw, w1, w3, w2, gate, T, *, tm, tf):
    M, D = x.shape
    E, _, F = w1.shape
    per_b = T // tm
    gated = gate is not None
    modspec = lambda k: pl.BlockSpec((None, None, 1, D), lambda i, e, f: (i // per_b, k, 0, 0))
    in_specs = [pl.BlockSpec((tm, D), lambda i, e, f: (i, 0)),
                pl.BlockSpec((1, D), lambda i, e, f: (0, 0)),
                modspec(4), modspec(3), modspec(5)]
    args = [x, nw, mod_l, mod_l, mod_l]
    if gated:
        in_specs.append(pl.BlockSpec((tm, LANES), lambda i, e, f: (i, 0)))
        args.append(gate)
    in_specs += [pl.BlockSpec((None, D, tf), lambda i, e, f: (e, 0, f)),
                 pl.BlockSpec((None, D, tf), lambda i, e, f: (e, 0, f)),
                 pl.BlockSpec((None, tf, D), lambda i, e, f: (e, f, 0))]
    args += [w1, w3, w2]
    return pl.pallas_call(
        functools.partial(_ffn_kernel, gated=gated),
        grid=(M // tm, E, F // tf),
        in_specs=in_specs,
        out_specs=pl.BlockSpec((tm, D), lambda i, e, f: (i, 0)),
        out_shape=jax.ShapeDtypeStruct((M, D), F32),
        scratch_shapes=[pltpu.VMEM((tm, D), BF16), pltpu.VMEM((tm, D), F32)],
        compiler_params=_cparams("parallel", "arbitrary", "arbitrary"),
        name="ffn_gated" if gated else "ffn_dense",
    )(*args)


MOE_TC = 512
MOE_TS = 512


def _router_kernel(x_ref, nw_ref, sc_ref, sh_ref, wh_ref, wl_ref, h_ref, gate_ref, rank_ref, cnt_ref, carry_ref):
    @pl.when(pl.program_id(0) == 0)
    def _():
        carry_ref[...] = jnp.zeros_like(carry_ref)

    h = _norm_mod(x_ref[...], nw_ref[...], sc_ref[...], sh_ref[...])
    hh = h.astype(BF16)
    h_ref[...] = hh.astype(h_ref.dtype)
    hl = (h - hh.astype(F32)).astype(BF16)
    logits = _dot(hh, wh_ref[...]) + (_dot(hl, wh_ref[...]) + _dot(hh, wl_ref[...]))
    tm = logits.shape[0]
    lane = lax.broadcasted_iota(jnp.int32, logits.shape, 1)
    logits = jnp.where(lane < N_EXPERTS, logits, REMOVED)
    lane_f = lane.astype(F32)
    v1 = jnp.max(logits, axis=-1, keepdims=True)
    i1 = jnp.min(jnp.where(logits == v1, lane_f, float(LANES)), axis=-1, keepdims=True)
    rest = jnp.where(lane_f == i1, REMOVED, logits)
    v2 = jnp.max(rest, axis=-1, keepdims=True)
    i2 = jnp.min(jnp.where(rest == v2, lane_f, float(LANES)), axis=-1, keepdims=True)
    e2 = jnp.exp(v2 - v1)
    w1 = 1.0 / (1.0 + e2)
    w2 = e2 / (1.0 + e2)
    gate_ref[...] = jnp.where(lane_f == i1, w1, jnp.where(lane_f == i2, w2, 0.0))

    sel = jnp.where((lane_f == i1) | (lane_f == i2), 1.0, 0.0)
    ri = lax.broadcasted_iota(jnp.int32, (tm, tm), 0)
    ci = lax.broadcasted_iota(jnp.int32, (tm, tm), 1)
    before = jnp.where(ci < ri, 1.0, 0.0).astype(BF16)
    rank = _dot(before, sel.astype(BF16)) + carry_ref[0:1, :]
    rank_ref[...] = jnp.where(sel > 0.0, rank, -1.0)
    carry_ref[...] = carry_ref[...] + jnp.sum(sel, axis=0, keepdims=True)
    cnt_ref[...] = carry_ref[...]


def router(x, mod_l, nw, w_router, T):
    M, D = x.shape
    tm = MOE_TC
    per_b = T // tm
    wp = jnp.zeros((D, LANES), F32).at[:, :N_EXPERTS].set(w_router)
    wh = wp.astype(BF16)
    wl = (wp - wh.astype(F32)).astype(BF16)
    return pl.pallas_call(
        _router_kernel,
        grid=(M // tm,),
        in_specs=[pl.BlockSpec((tm, D), lambda i: (i, 0)),
                  pl.BlockSpec((1, D), lambda i: (0, 0))]
        + _mod_specs(T, tm, 4, 3, 1)
        + [pl.BlockSpec((D, LANES), lambda i: (0, 0)),
           pl.BlockSpec((D, LANES), lambda i: (0, 0))],
        out_specs=[pl.BlockSpec((tm, D), lambda i: (i, 0)),
                   pl.BlockSpec((tm, LANES), lambda i: (i, 0)),
                   pl.BlockSpec((tm, LANES), lambda i: (i, 0)),
                   pl.BlockSpec((8, LANES), lambda i: (0, 0))],
        out_shape=[jax.ShapeDtypeStruct((M, D), F32),
                   jax.ShapeDtypeStruct((M, LANES), F32),
                   jax.ShapeDtypeStruct((M, LANES), F32),
                   jax.ShapeDtypeStruct((8, LANES), F32)],
        scratch_shapes=[pltpu.VMEM((8, LANES), F32)],
        compiler_params=_cparams("arbitrary"),
        name="moe_router",
    )(x, nw, mod_l, mod_l, wh, wl)


def _count_le(sorted_vals, x):
    return jnp.sum(sorted_vals[None, :] <= x[:, None], axis=1, dtype=jnp.int32)


def _moe_up_kernel(e_r, total, x_ref, w1_ref, w3_ref, o_ref, w1b_ref, w3b_ref):
    r = pl.program_id(1)
    live = r < total[0]

    @pl.when(live & ((r == 0) | (e_r[r] != e_r[jnp.maximum(r - 1, 0)])))
    def _():
        w1b_ref[...] = w1_ref[...].astype(BF16)
        w3b_ref[...] = w3_ref[...].astype(BF16)

    @pl.when(live)
    def _():
        x = x_ref[...].astype(BF16)
        u = _dot(x, w1b_ref[...])
        v = _dot(x, w3b_ref[...])
        o_ref[...] = (u * _sigmoid(u) * v).astype(o_ref.dtype)


def moe_up(xs, w1, w3, tiles, rt, *, tf=896):
    R, D = xs.shape
    ts = MOE_TS
    F = w1.shape[-1]
    live = lambda r, total: jnp.minimum(r, total[0] - 1)
    return pl.pallas_call(
        _moe_up_kernel,
        grid_spec=pltpu.PrefetchScalarGridSpec(
            num_scalar_prefetch=2,
            grid=(F // tf, rt),
            in_specs=[pl.BlockSpec((ts, D), lambda n, r, e, total: (live(r, total), 0)),
                      pl.BlockSpec((None, D, tf), lambda n, r, e, total: (e[live(r, total)], 0, n)),
                      pl.BlockSpec((None, D, tf), lambda n, r, e, total: (e[live(r, total)], 0, n))],
            out_specs=pl.BlockSpec((ts, tf), lambda n, r, e, total: (r, n)),
            scratch_shapes=[pltpu.VMEM((D, tf), BF16), pltpu.VMEM((D, tf), BF16)],
        ),
        out_shape=jax.ShapeDtypeStruct((R, F), BF16),
        compiler_params=_cparams("arbitrary", "arbitrary"),
        name="moe_up",
    )(tiles["e"], tiles["total"], xs, w1, w3)


def _moe_down_kernel(e_r, total, a_ref, w2_ref, o_ref):
    @pl.when(pl.program_id(0) < total[0])
    def _():
        o_ref[...] = _dot(a_ref[...], w2_ref[...]).astype(o_ref.dtype)


def moe_down(a, w2, tiles, rt):
    R, F = a.shape
    ts = MOE_TS
    D = w2.shape[-1]
    live = lambda r, total: jnp.minimum(r, total[0] - 1)
    return pl.pallas_call(
        _moe_down_kernel,
        grid_spec=pltpu.PrefetchScalarGridSpec(
            num_scalar_prefetch=2,
            grid=(rt,),
            in_specs=[pl.BlockSpec((ts, F), lambda r, e, total: (live(r, total), 0)),
                      pl.BlockSpec((None, F, D), lambda r, e, total: (e[live(r, total)], 0, 0))],
            out_specs=pl.BlockSpec((ts, D), lambda r, e, total: (r, 0)),
        ),
        out_shape=jax.ShapeDtypeStruct((R, D), F32),
        compiler_params=_cparams("arbitrary"),
        name="moe_down",
    )(tiles["e"], tiles["total"], a, w2)


SC_WINDOW = 128
SC_WIDTH = 256


def _sc_mesh():
    return plsc.VectorSubcoreMesh(core_axis_name="core", subcore_axis_name="subcore")


def _split_rows(idx, pieces):
    return (idx[:, None] * pieces + jnp.arange(pieces, dtype=idx.dtype)[None, :]).reshape(1, -1)


def sc_scatter_rows2(x, idx_a, idx_b, n_out):
    n, d = x.shape
    pieces = d // SC_WIDTH
    n_p = n * pieces

    @pl.kernel(out_type=jax.ShapeDtypeStruct((n_out * pieces, SC_WIDTH), x.dtype), mesh=_sc_mesh(),
               scratch_types=[])
    def kern(x_hbm, ia_hbm, ib_hbm, o_hbm):
        def body(x_vmem, ia_vmem, ib_vmem):
            pltpu.sync_copy(x_vmem, o_hbm.at[ia_vmem.at[0]])
            pltpu.sync_copy(x_vmem, o_hbm.at[ib_vmem.at[0]])

        pltpu.emit_pipeline(
            body,
            grid=(n_p // SC_WINDOW,),
            in_specs=[pl.BlockSpec((SC_WINDOW, SC_WIDTH), index_map=lambda i: (i, 0)),
                      pl.BlockSpec((1, SC_WINDOW), index_map=lambda i: (0, i)),
                      pl.BlockSpec((1, SC_WINDOW), index_map=lambda i: (0, i))],
            out_specs=[],
            core_axis_name=("core", "subcore"),
            dimension_semantics=(pltpu.PARALLEL,),
        )(x_hbm, ia_hbm, ib_hbm)

    out = kern(x.reshape(n_p, SC_WIDTH), _split_rows(idx_a, pieces), _split_rows(idx_b, pieces))
    return out.reshape(n_out, d)


def sc_gather_rows(x, idx):
    n = idx.shape[0]
    rows, d = x.shape
    pieces = d // SC_WIDTH
    n_p = n * pieces

    @pl.kernel(out_type=jax.ShapeDtypeStruct((n_p, SC_WIDTH), x.dtype), mesh=_sc_mesh(), scratch_types=[])
    def kern(x_hbm, i_hbm, o_hbm):
        def body(i_vmem, o_vmem):
            pltpu.sync_copy(x_hbm.at[i_vmem.at[0]], o_vmem)

        pltpu.emit_pipeline(
            body,
            grid=(n_p // SC_WINDOW,),
            in_specs=[pl.BlockSpec((1, SC_WINDOW), index_map=lambda i: (0, i))],
            out_specs=[pl.BlockSpec((SC_WINDOW, SC_WIDTH), index_map=lambda i: (i, 0))],
            core_axis_name=("core", "subcore"),
            dimension_semantics=(pltpu.PARALLEL,),
        )(i_hbm, o_hbm)

    return kern(x.reshape(rows * pieces, SC_WIDTH), _split_rows(idx, pieces)).reshape(n, d)


def _moe_finish_kernel(x_ref, g2_ref, ya_ref, yb_ref, w_ref, nw_ref, o_ref, *, normalize):
    w = w_ref[...]
    y = w[:, 0:1] * ya_ref[...] + w[:, 1:2] * yb_ref[...]
    x = x_ref[...] + g2_ref[...] * y
    if normalize:
        ms = jnp.mean(x * x, axis=-1, keepdims=True)
        x = x * lax.rsqrt(ms + NORM_EPS) * nw_ref[...]
    o_ref[...] = x


def moe_finish(x, mod_l, y2, w_ab, norm_w, T, *, tm=512):
    M, D = x.shape
    per_b = T // tm
    normalize = norm_w is not None
    if norm_w is None:
        norm_w = jnp.ones((1, D), F32)
    return pl.pallas_call(
        functools.partial(_moe_finish_kernel, normalize=normalize),
        grid=(M // tm,),
        in_specs=[pl.BlockSpec((tm, D), lambda i: (i, 0)),
                  pl.BlockSpec((None, None, 1, D), lambda i: (i // per_b, 5, 0, 0)),
                  pl.BlockSpec((None, tm, D), lambda i: (0, i, 0)),
                  pl.BlockSpec((None, tm, D), lambda i: (1, i, 0)),
                  pl.BlockSpec((tm, LANES), lambda i: (i, 0)),
                  pl.BlockSpec((1, D), lambda i: (0, 0))],
        out_specs=pl.BlockSpec((tm, D), lambda i: (i, 0)),
        out_shape=jax.ShapeDtypeStruct((M, D), F32),
        compiler_params=_cparams("parallel"),
        name="moe_finish",
    )(x, mod_l, y2, y2, w_ab, norm_w)


def moe_ffn(x, mod_l, nw, w_router, w1, w3, w2, T, norm_w=None):
    M = x.shape[0]
    ts = MOE_TS
    rt = (2 * M) // ts + N_EXPERTS
    h, gate, rank, cnt = router(x, mod_l, nw, w_router, T)
    i32 = jnp.int32
    counts = cnt[0, :N_EXPERTS].astype(i32)
    ntile = (counts + ts - 1) // ts
    tile_end = jnp.cumsum(ntile)
    row_off = (tile_end - ntile) * ts
    e_r = jnp.minimum(_count_le(tile_end, jnp.arange(rt, dtype=i32)), N_EXPERTS - 1)
    tiles = dict(e=e_r, total=tile_end[-1].reshape(1).astype(i32))
    rk = rank[:, :N_EXPERTS].astype(i32)
    chosen = rk >= 0
    pos = jnp.where(chosen, row_off[None, :] + rk, 0)
    first = jnp.argmax(chosen, axis=1)
    last = N_EXPERTS - 1 - jnp.argmax(chosen[:, ::-1], axis=1)
    take = lambda a, e: jnp.take_along_axis(a, e[:, None], axis=1)[:, 0]
    pos_a, pos_b = take(pos, first), take(pos, last)
    g8 = gate[:, :N_EXPERTS]
    w_ab = jnp.zeros((M, LANES), F32).at[:, 0].set(take(g8, first)).at[:, 1].set(take(g8, last))

    xs = sc_scatter_rows2(h, pos_a, pos_b, rt * ts)
    a = moe_up(xs, w1, w3, tiles, rt)
    y = moe_down(a, w2, tiles, rt)
    y2 = sc_gather_rows(y, jnp.concatenate([pos_a, pos_b])).reshape(2, M, -1)
    return moe_finish(x, mod_l, y2, w_ab, norm_w, T)


def _final_norm_kernel(x_ref, w_ref, o_ref):
    x = x_ref[...]
    ms = jnp.mean(x * x, axis=-1, keepdims=True)
    o_ref[...] = x * lax.rsqrt(ms + NORM_EPS) * w_ref[...]


def final_norm(x, w, *, tm=1024):
    M, D = x.shape
    return pl.pallas_call(
        _final_norm_kernel,
        grid=(M // tm,),
        in_specs=[pl.BlockSpec((tm, D), lambda i: (i, 0)), pl.BlockSpec((1, D), lambda i: (0, 0))],
        out_specs=pl.BlockSpec((tm, D), lambda i: (i, 0)),
        out_shape=jax.ShapeDtypeStruct((M, D), F32),
        compiler_params=_cparams("parallel"),
        name="final_norm",
    )(x, w)


def nsa_constants(T):
    n_sel = T // SEL_LEN
    nsp = max(LANES, n_sel)
    ncp = T // CMP_STRIDE
    cmp_start = np.arange(ncp) * CMP_STRIDE
    sel_start = np.arange(nsp) * SEL_LEN
    ov = ((cmp_start[:, None] < sel_start[None, :] + SEL_LEN)
          & (cmp_start[:, None] + CMP_LEN > sel_start[None, :]))
    ov[(T - CMP_LEN) // CMP_STRIDE + 1:] = False
    ov[:, n_sel:] = False
    et_mat = ((np.arange(T)[:, None] // SEL_LEN) == np.arange(nsp)[None, :]) * SEL_BONUS
    return jnp.asarray(ov.T, BF16), jnp.asarray(et_mat, BF16)


def token_mixing(x, mod_l, lw, consts, B, T):
    M = B * T
    cos_t, sin_t, ov_t, e_mat, ret_consts, ex = consts
    p1 = proj_rope(x, mod_l, lw["norm_mix"], lw["w1"], cos_t, sin_t, p1_scales(), T).reshape(B, T, P1_COLS)
    p2 = proj_plain(x, mod_l, lw["norm_mix"], lw["w2"], T).reshape(B, T, P2_COLS)

    def group_rows(a):
        return a.reshape(B, T, NSA_GROUPS, HEAD_DIM).transpose(0, 2, 1, 3).reshape(
            B, NSA_GROUPS, T // CMP_STRIDE, CMP_STRIDE * HEAD_DIM)

    xr = jnp.stack([group_rows(p1[:, :, P1_NKC:P1_NKC + LANES]), group_rows(p2[:, :, P2_NVC:P2_NVC + LANES])])
    cmp_out = compress(xr, lw["cmp_pe"], lw["cmp_w1"], lw["cmp_w2"])
    cmp_out = cmp_out.transpose(0, 1, 3, 2, 4).reshape(2, B, T // CMP_STRIDE, LANES)
    o_cmp, sel = nsa_cmp_select(p1, cmp_out[0], cmp_out[1], ov_t, T)
    o_sel = nsa_selected(p1, nsa_value_augment(p2[:, :, P2_NVS:P2_NVS + LANES]), sel, e_mat, T)
    o_win = nsa_window(p1, p2, T)

    o_ret = retention(p1, p2, ret_consts, T)

    ff = p2[:, :, P2_SMALL + 3 * NSA_HEADS:P2_SMALL + 3 * NSA_HEADS + FOX_HEADS].astype(F32)
    ff = ff.transpose(0, 2, 1).reshape(B, FOX_HEADS, T // LANES, LANES)
    cum = fox_cum(ff, lw["fox_bias"]).reshape(B, FOX_HEADS // 2, 2, 1, T)
    o_fox = fox_attention(p2, cum, T)

    return readout(o_cmp.reshape(M, -1), o_sel.reshape(M, -1), o_win.reshape(M, -1), p2.reshape(M, P2_COLS),
                   o_ret.reshape(M, -1), o_fox.reshape(M, -1), x, mod_l, ex,
                   lw["wn"], lw["wr"], lw["wf"], lw["wo"], T)


def layer_weights(l, norm_mix, w_in, cmp_k_pe, cmp_k_w1, cmp_k_w2, cmp_v_pe, cmp_v_w1, cmp_v_w2, fox_f_bias,
                  w_read_nsa, w_read_ret, w_read_fox, w_out):
    w1, w2 = split_w_in(w_in[l])
    pe = jnp.stack([cmp_k_pe[l].reshape(1, -1), cmp_v_pe[l].reshape(1, -1)])
    pe = jnp.broadcast_to(pe, (2, 8, pe.shape[-1])).astype(BF16)
    return {
        "norm_mix": norm_mix[l].reshape(1, -1),
        "w1": w1, "w2": w2,
        "cmp_pe": pe,
        "cmp_w1": jnp.stack([cmp_k_w1[l], cmp_v_w1[l]]).astype(BF16),
        "cmp_w2": jnp.stack([cmp_k_w2[l], cmp_v_w2[l]]).astype(BF16),
        "fox_bias": jnp.broadcast_to(fox_f_bias[l][:, None, None], (FOX_HEADS, 1, LANES)),
        "wn": pad_read_nsa(w_read_nsa[l]),
        "wr": w_read_ret[l].astype(BF16),
        "wf": w_read_fox[l].astype(BF16),
        "wo": w_out[l].astype(BF16),
    }


def kernel(x, c, ada_w, ada_b, norm_mix, norm_ffn, w_in, cmp_k_pe, cmp_k_w1, cmp_k_w2, cmp_v_pe, cmp_v_w1,
           cmp_v_w2, fox_f_bias, w_read_nsa, w_read_ret, w_read_fox, w_out, ffn_w1, ffn_w3, ffn_w2, router_w,
           moe_w1, moe_w3, moe_w2, final_norm_w):
    B, T, D = x.shape
    M = B * T
    depth = ada_w.shape[0]
    mod = modulation(c, ada_w, ada_b)
    cos_t, sin_t = rope_tables(T)
    ov_t, e_mat = nsa_constants(T)
    consts = (cos_t, sin_t, ov_t, e_mat, retention_consts(), nsa_gate_expand())
    xs = x.reshape(M, D)
    for l in range(depth):
        lw = layer_weights(l, norm_mix, w_in, cmp_k_pe, cmp_k_w1, cmp_k_w2, cmp_v_pe, cmp_v_w1, cmp_v_w2,
                           fox_f_bias, w_read_nsa, w_read_ret, w_read_fox, w_out)
        xs = token_mixing(xs, mod[l], lw, consts, B, T)
        nf = norm_ffn[l].reshape(1, D)
        if l % 2 == 0:
            k = l // 2
            xs = ffn(xs, mod[l], nf, ffn_w1[k][None].astype(BF16), ffn_w3[k][None].astype(BF16),
                     ffn_w2[k][None].astype(BF16), None, T, tm=512, tf=D_FF // 2)
        else:
            k = l // 2
            fuse = final_norm_w.reshape(1, D) if l == depth - 1 else None
            xs = moe_ffn(xs, mod[l], nf, router_w[k], moe_w1[k], moe_w3[k], moe_w2[k].astype(BF16), T, fuse)
    if depth % 2 == 1:
        xs = final_norm(xs, final_norm_w.reshape(1, D))
    return xs.reshape(B, T, D)
```

```python
import functools
import math

import jax
import jax.numpy as jnp
import numpy as np
from jax import lax
from jax.experimental import pallas as pl
from jax.experimental.pallas import tpu as pltpu
from jax.experimental.pallas import tpu_sc as plsc

F32 = jnp.float32
BF16 = jnp.bfloat16

D_MODEL = 1024
DEPTH = 2
HEAD_DIM = 64
ROPE_THETA = 10000.0
NORM_EPS = 1e-6
NEG_INF = -1e30
REMOVED = -3e38

NSA_HEADS = 8
NSA_GROUPS = 2
NSA_HPG = NSA_HEADS // NSA_GROUPS
CMP_LEN = 32
CMP_STRIDE = 16
CMP_HIDDEN = 256
SEL_LEN = 64
SEL_TOPN = 16
WINDOW = 512
FORCE_SCORE = 1e4
NSA_QBLOCK = 128

RET_HEADS = 4
RET_QK_DIM = 64
RET_V_DIM = 128
RET_CHUNK = 128

FOX_HEADS = 8
FOX_TQ = 1024
LOG2E = 1.4426950408889634

D_FF = 2816
N_EXPERTS = 8
D_FF_EXPERT = 3584

LANES = 128
VMEM_LIMIT = 56 * 1024 * 1024

P1_NQ = 0
P1_RQ = 1024
P1_RK = 1536
P1_NKC = 1792
P1_NKS = 1920
P1_NKW = 2048
P1_COLS = 2176
P2_MG = 0
P2_RV = 3072
P2_RG = 3584
P2_FQ = 4096
P2_FK = 5120
P2_FV = 5632
P2_NVC = 6144
P2_NVS = 6272
P2_NVW = 6400
P2_SMALL = 6528
P2_COLS = 6656


def _cparams(*sem):
    return pltpu.CompilerParams(dimension_semantics=tuple(sem), vmem_limit_bytes=VMEM_LIMIT)


def _sigmoid(x):
    return 1.0 / (1.0 + jnp.exp(-x))


def _dot(a, b):
    return jnp.dot(a, b, preferred_element_type=F32)


def _dot_nt(a, b):
    return lax.dot_general(a, b, (((1,), (1,)), ((), ())), preferred_element_type=F32)


def _dot_tn(a, b):
    return lax.dot_general(a, b, (((0,), (0,)), ((), ())), preferred_element_type=F32)


def _split3(x):
    hi = x.astype(BF16)
    r1 = x - hi.astype(F32)
    mid = r1.astype(BF16)
    lo = (r1 - mid.astype(F32)).astype(BF16)
    return hi, mid, lo


def _norm_mod(x, nw, sc, sh):
    ms = jnp.mean(x * x, axis=-1, keepdims=True)
    y = x * lax.rsqrt(ms + NORM_EPS) * nw
    return y * (1.0 + sc) + sh


def _mod_kernel(c_ref, w_ref, b_ref, o_ref):
    c = c_ref[...]
    s = c * _sigmoid(c)
    o_ref[0] = _dot(s.astype(BF16), w_ref[0].astype(BF16)) + b_ref[0]


def modulation(c, ada_w, ada_b):
    B, D = c.shape
    depth = ada_w.shape[0]
    rows = 8
    c_pad = jnp.zeros((rows, D), F32).at[:B].set(c)
    out = pl.pallas_call(
        _mod_kernel,
        grid=(depth, 6),
        in_specs=[pl.BlockSpec((rows, D), lambda l, j: (0, 0)),
                  pl.BlockSpec((1, D, D), lambda l, j: (l, 0, j)),
                  pl.BlockSpec((1, 1, D), lambda l, j: (l, 0, j))],
        out_specs=pl.BlockSpec((1, rows, D), lambda l, j: (l, 0, j)),
        out_shape=jax.ShapeDtypeStruct((depth, rows, 6 * D), F32),
        compiler_params=_cparams("parallel", "parallel"),
        name="modulation",
    )(c_pad, ada_w, ada_b.reshape(depth, 1, 6 * D))
    return out[:, :B].reshape(depth, B, 6, 1, D)


def _proj_plain_kernel(x_ref, nw_ref, sc_ref, sh_ref, w_ref, o_ref, h_ref):
    @pl.when(pl.program_id(1) == 0)
    def _():
        h_ref[...] = _norm_mod(x_ref[...], nw_ref[...], sc_ref[...], sh_ref[...]).astype(BF16)

    o_ref[...] = _dot(h_ref[...], w_ref[...]).astype(o_ref.dtype)


def _proj_rope_kernel(x_ref, nw_ref, sc_ref, sh_ref, w_ref, cos_ref, sin_ref, o_ref, *, scales):
    h = _norm_mod(x_ref[...], nw_ref[...], sc_ref[...], sh_ref[...]).astype(BF16)
    y = _dot(h, w_ref[...])
    cos = cos_ref[...]
    sin = sin_ref[...]
    lane = lax.broadcasted_iota(jnp.int32, cos.shape, 1)
    first_half = (lane % HEAD_DIM) < (HEAD_DIM // 2)
    for g, scale in enumerate(scales):
        yg = y[:, g * LANES:(g + 1) * LANES]
        rot = jnp.where(first_half, pltpu.roll(yg, LANES - HEAD_DIM // 2, 1),
                        pltpu.roll(yg, HEAD_DIM // 2, 1))
        r = yg * cos + rot * sin
        if scale != 1.0:
            r = r * scale
        o_ref[:, g * LANES:(g + 1) * LANES] = r.astype(o_ref.dtype)


def _mod_specs(T, tm, sc_idx, sh_idx, nargs):
    per_b = T // tm
    if nargs == 1:
        return [pl.BlockSpec((None, None, 1, D_MODEL), lambda i: (i // per_b, sc_idx, 0, 0)),
                pl.BlockSpec((None, None, 1, D_MODEL), lambda i: (i // per_b, sh_idx, 0, 0))]
    return [pl.BlockSpec((None, None, 1, D_MODEL), lambda i, j: (i // per_b, sc_idx, 0, 0)),
            pl.BlockSpec((None, None, 1, D_MODEL), lambda i, j: (i // per_b, sh_idx, 0, 0))]


def proj_plain(x, mod_l, nw, w, T, *, tm=1024, tn=512):
    M, D = x.shape
    N = w.shape[1]
    return pl.pallas_call(
        _proj_plain_kernel,
        grid=(M // tm, N // tn),
        in_specs=[pl.BlockSpec((tm, D), lambda i, j: (i, 0)),
                  pl.BlockSpec((1, D), lambda i, j: (0, 0))]
        + _mod_specs(T, tm, 1, 0, 2)
        + [pl.BlockSpec((D, tn), lambda i, j: (0, j))],
        out_specs=pl.BlockSpec((tm, tn), lambda i, j: (i, j)),
        out_shape=jax.ShapeDtypeStruct((M, N), BF16),
        scratch_shapes=[pltpu.VMEM((tm, D), BF16)],
        compiler_params=_cparams("parallel", "arbitrary"),
        name="proj_plain",
    )(x, nw, mod_l, mod_l, w)


def proj_rope(x, mod_l, nw, w, cos, sin, scales, T, *, tm=512):
    M, D = x.shape
    N = w.shape[1]
    per_b = T // tm
    return pl.pallas_call(
        functools.partial(_proj_rope_kernel, scales=scales),
        grid=(M // tm,),
        in_specs=[pl.BlockSpec((tm, D), lambda i: (i, 0)),
                  pl.BlockSpec((1, D), lambda i: (0, 0))]
        + _mod_specs(T, tm, 1, 0, 1)
        + [pl.BlockSpec((D, N), lambda i: (0, 0)),
           pl.BlockSpec((tm, LANES), lambda i: (i % per_b, 0)),
           pl.BlockSpec((tm, LANES), lambda i: (i % per_b, 0))],
        out_specs=pl.BlockSpec((tm, N), lambda i: (i, 0)),
        out_shape=jax.ShapeDtypeStruct((M, N), BF16),
        compiler_params=_cparams("parallel"),
        name="proj_rope",
    )(x, nw, mod_l, mod_l, w, cos, sin)


def rope_tables(T):
    d = HEAD_DIM
    pos = jnp.arange(T, dtype=F32)
    inv = ROPE_THETA ** (-jnp.arange(0, d, 2, dtype=F32) / d)
    ang = pos[:, None] * inv[None, :]
    cos = jnp.cos(ang)
    sin = jnp.sin(ang)
    cos_t = jnp.concatenate([cos, cos, cos, cos], axis=-1)
    sin_t = jnp.concatenate([-sin, sin, -sin, sin], axis=-1)
    return cos_t, sin_t


def _pad_heads(w, n_heads, half_of_head):
    D = w.shape[0]
    w = w.reshape(D, n_heads, HEAD_DIM)
    z = jnp.zeros_like(w)
    halves = np.array([half_of_head(h) for h in range(n_heads)])
    lo = jnp.where(halves[None, :, None] == 0, w, z)
    hi = jnp.where(halves[None, :, None] == 1, w, z)
    return jnp.concatenate([lo, hi], axis=-1).reshape(D, n_heads * LANES)


def split_w_in(w_in):
    sizes = [512, 128, 128, 128, 128, 128, 128, 24, 256, 256, 512, 512, 512, 512, 512, 8, 3072]
    offs = np.cumsum([0] + sizes)
    (nq, nkc, nvc, nks, nvs, nkw, nvw, ngate, rq, rk, rv, rg, fq, fk, fv, ff, mg) = [
        w_in[:, offs[i]:offs[i + 1]] for i in range(len(sizes))]
    D = w_in.shape[0]
    nq_p = _pad_heads(nq, NSA_HEADS, lambda h: h // NSA_HPG)
    rq_p = _pad_heads(rq, RET_HEADS, lambda h: h % 2)
    fq_p = _pad_heads(fq, FOX_HEADS, lambda h: 0) * (HEAD_DIM ** -0.5 * LOG2E)
    small = jnp.concatenate([ngate, ff, jnp.zeros((D, LANES - 32), w_in.dtype)], axis=-1)
    w1 = jnp.concatenate([nq_p, rq_p, rk, nkc, nks, nkw], axis=-1).astype(BF16)
    w2 = jnp.concatenate([mg, rv, rg, fq_p, fk, fv, nvc, nvs, nvw, small], axis=-1).astype(BF16)
    assert w1.shape[1] == P1_COLS and w2.shape[1] == P2_COLS
    return w1, w2


def p1_scales():
    s = [1.0] * (P1_COLS // LANES)
    for g in range(P1_NQ // LANES, P1_RQ // LANES):
        s[g] = HEAD_DIM ** -0.5 * LOG2E
    for g in range(P1_RK // LANES, P1_NKC // LANES):
        s[g] = RET_QK_DIM ** -0.5
    return tuple(s)


def _compress_kernel(x_ref, pe_ref, w1_ref, w2_ref, o_ref):
    r = x_ref[...]
    half = r.shape[1]
    w1 = w1_ref[...]
    a = _dot(r, w1[:half])
    b = _dot(r, w1[half:])
    pe = _dot(pe_ref[...], w1)[0:1]
    n = a.shape[0]
    hid = a + pltpu.roll(b, n - 1, 0) + pe
    hid = hid * _sigmoid(hid)
    o_ref[...] = _dot(hid.astype(BF16), w2_ref[...]).astype(o_ref.dtype)


def compress(xr, pe, w1, w2):
    _, B, G, R, W = xr.shape
    H = w1.shape[-1]
    return pl.pallas_call(
        _compress_kernel,
        grid=(2, B, G),
        in_specs=[pl.BlockSpec((None, None, None, R, W), lambda s, b, g: (s, b, g, 0, 0)),
                  pl.BlockSpec((None, 8, 2 * W), lambda s, b, g: (s, 0, 0)),
                  pl.BlockSpec((None, 2 * W, H), lambda s, b, g: (s, 0, 0)),
                  pl.BlockSpec((None, H, HEAD_DIM), lambda s, b, g: (s, 0, 0))],
        out_specs=pl.BlockSpec((None, None, None, R, HEAD_DIM), lambda s, b, g: (s, b, g, 0, 0)),
        out_shape=jax.ShapeDtypeStruct((2, B, G, R, HEAD_DIM), BF16),
        compiler_params=_cparams("parallel", "parallel", "parallel"),
        name="nsa_compress",
    )(xr, pe, w1, w2)


def _stack_heads(q_ref, g):
    return jnp.concatenate(
        [q_ref[:, (NSA_HPG * g + hh) * LANES:(NSA_HPG * g + hh + 1) * LANES] for hh in range(NSA_HPG)],
        axis=0)


def _store_heads(o_ref, g, o, tq):
    for hh in range(NSA_HPG):
        h = NSA_HPG * g + hh
        o_ref[:, h * LANES:(h + 1) * LANES] = o[hh * tq:(hh + 1) * tq].astype(o_ref.dtype)


def _nsa_cmp_kernel(q_ref, kc_ref, vc_ref, ov_ref, o_ref, m_ref, *, tq, n_sel, top_n):
    t0 = pl.program_id(1) * tq
    kc = kc_ref[...]
    vc = vc_ref[...]
    ncp = kc.shape[0]
    nsp = ov_ref.shape[0]
    rows = NSA_HPG * tq
    n_idx = lax.broadcasted_iota(jnp.int32, (rows, ncp), 1)
    t_idx = t0 + lax.broadcasted_iota(jnp.int32, (rows, ncp), 0) % tq
    valid = (n_idx * CMP_STRIDE + (CMP_LEN - 1)) <= t_idx
    j_idx = lax.broadcasted_iota(jnp.int32, (nsp, tq), 0)
    cur = (t0 + lax.broadcasted_iota(jnp.int32, (nsp, tq), 1)) // SEL_LEN
    forced = (j_idx == 0) | (j_idx == cur) | (j_idx == cur - 1)
    j_f = j_idx.astype(F32)
    for g in range(NSA_GROUPS):
        q = _stack_heads(q_ref, g)
        s = jnp.where(valid, _dot_nt(q, kc), NEG_INF)
        m = jnp.max(s, axis=-1, keepdims=True)
        e = jnp.where(valid, jnp.exp2(s - m), 0.0)
        l = jnp.sum(e, axis=-1, keepdims=True)
        p = e / jnp.where(l > 0.0, l, 1.0)
        _store_heads(o_ref, g, _dot(p.astype(BF16), vc), tq)
        psum = p[0:tq]
        for hh in range(1, NSA_HPG):
            psum = psum + p[hh * tq:(hh + 1) * tq]
        imp_t = _dot_nt(ov_ref[...], psum.astype(BF16))
        score = jnp.where(forced, FORCE_SCORE, imp_t)
        score = jnp.where(j_idx <= cur, score, NEG_INF)
        score = jnp.where(j_idx < n_sel, score, REMOVED)
        sel = jnp.zeros((nsp, tq), F32)
        for _ in range(top_n):
            mx = jnp.max(score, axis=0, keepdims=True)
            idx = jnp.min(jnp.where(score == mx, j_f, float(nsp)), axis=0, keepdims=True)
            hit = j_f == idx
            sel = jnp.where(hit, 1.0, sel)
            score = jnp.where(hit, REMOVED, score)
        sel = jnp.where(j_idx <= cur, sel, 0.0)
        m_ref[g] = sel.T.astype(m_ref.dtype)


def nsa_cmp_select(p1, kc, vc, ov_t, T):
    B = p1.shape[0]
    tq = NSA_QBLOCK
    ncp = kc.shape[1]
    nsp = ov_t.shape[0]
    n_sel = T // SEL_LEN
    return pl.pallas_call(
        functools.partial(_nsa_cmp_kernel, tq=tq, n_sel=n_sel, top_n=min(SEL_TOPN, n_sel)),
        grid=(B, T // tq),
        in_specs=[pl.BlockSpec((None, tq, NSA_HEADS * LANES), lambda b, i: (b, i, 0)),
                  pl.BlockSpec((None, ncp, LANES), lambda b, i: (b, 0, 0)),
                  pl.BlockSpec((None, ncp, LANES), lambda b, i: (b, 0, 0)),
                  pl.BlockSpec((nsp, ncp), lambda b, i: (0, 0))],
        out_specs=[pl.BlockSpec((None, tq, NSA_HEADS * LANES), lambda b, i: (b, i, 0)),
                   pl.BlockSpec((None, NSA_GROUPS, tq, nsp), lambda b, i: (b, 0, i, 0))],
        out_shape=[jax.ShapeDtypeStruct((B, T, NSA_HEADS * LANES), BF16),
                   jax.ShapeDtypeStruct((B, NSA_GROUPS, T, nsp), BF16)],
        compiler_params=_cparams("parallel", "parallel"),
        name="nsa_cmp_select",
    )(p1, kc, vc, ov_t)


SEL_BONUS = 8192.0
NSA_SEL_TQ = 256
NSA_SEL_TK = 1024


def _nsa_sel_kernel(q_ref, k_ref, v_ref, m_ref, et_ref, o_ref, *, tq, tk):
    t0 = pl.program_id(1) * tq
    n_tiles = (t0 + tq + tk - 1) // tk
    rows = NSA_HPG * tq
    for g in range(NSA_GROUPS):
        q = jnp.concatenate([_stack_heads(q_ref, g), jnp.concatenate([m_ref[g]] * NSA_HPG, axis=0)], axis=1)
        den = HEAD_DIM * (1 - g)

        def step(j, carry, masked, q=q, g=g):
            m, acc = carry
            start = pl.multiple_of(j * tk, tk)
            ks = jnp.concatenate([k_ref[pl.ds(start, tk), :], et_ref[pl.ds(start, tk), :]], axis=1)
            s = _dot_nt(q, ks)
            if masked:
                trow = t0 + lax.broadcasted_iota(jnp.int32, (rows, tk), 0) % tq
                kpos = start + lax.broadcasted_iota(jnp.int32, (rows, tk), 1)
                s = jnp.where(kpos <= trow, s, NEG_INF)
            m_new = jnp.maximum(m, jnp.max(s, axis=-1, keepdims=True))
            p = jnp.exp2(s - m_new)
            acc = jnp.exp2(m - m_new) * acc + _dot(p.astype(BF16), v_ref[g, pl.ds(start, tk), :])
            return m_new, acc

        init = (jnp.full((rows, 1), NEG_INF, F32), jnp.zeros((rows, LANES), F32))
        carry = lax.fori_loop(0, n_tiles - 1, functools.partial(step, masked=False), init)
        _, acc = step(n_tiles - 1, carry, True)
        _store_heads(o_ref, g, acc / acc[:, den:den + 1], tq)


def nsa_value_augment(v):
    ones = jnp.ones_like(v[..., :HEAD_DIM])
    return jnp.stack([jnp.concatenate([v[..., :HEAD_DIM], ones], axis=-1),
                      jnp.concatenate([ones, v[..., HEAD_DIM:]], axis=-1)], axis=1)


def nsa_selected(p1, v_aug, sel, et_mat, T, *, tq=NSA_SEL_TQ, tk=NSA_SEL_TK):
    B = p1.shape[0]
    nsp = sel.shape[-1]
    return pl.pallas_call(
        functools.partial(_nsa_sel_kernel, tq=tq, tk=tk),
        grid=(B, T // tq),
        in_specs=[pl.BlockSpec((None, tq, NSA_HEADS * LANES), lambda b, i: (b, i, 0)),
                  pl.BlockSpec((None, T, LANES), lambda b, i: (b, 0, P1_NKS // LANES)),
                  pl.BlockSpec((None, NSA_GROUPS, T, LANES), lambda b, i: (b, 0, 0, 0)),
                  pl.BlockSpec((None, NSA_GROUPS, tq, nsp), lambda b, i: (b, 0, i, 0)),
                  pl.BlockSpec((T, nsp), lambda b, i: (0, 0))],
        out_specs=pl.BlockSpec((None, tq, NSA_HEADS * LANES), lambda b, i: (b, i, 0)),
        out_shape=jax.ShapeDtypeStruct((B, T, NSA_HEADS * LANES), BF16),
        compiler_params=_cparams("parallel", "parallel"),
        name="nsa_selected",
    )(p1, p1, v_aug, sel, et_mat)


def _nsa_win_kernel(q_ref, k_ref, v_ref, b_ref, o_ref, *, tq):
    t0 = pl.program_id(1) * tq
    span = WINDOW + tq
    start = pl.multiple_of(jnp.maximum(t0 - WINDOW, 0), tq)
    ks = k_ref[pl.ds(start, span), :]
    vs = v_ref[pl.ds(start, span), :]

    def run(bias):
        bias = jnp.concatenate([bias] * NSA_HPG, axis=0)
        for g in range(NSA_GROUPS):
            s = _dot_nt(_stack_heads(q_ref, g), ks) + bias
            m = jnp.max(s, axis=-1, keepdims=True)
            p = jnp.exp2(s - m)
            l = jnp.sum(p, axis=-1, keepdims=True)
            _store_heads(o_ref, g, _dot(p.astype(BF16), vs) / l, tq)

    @pl.when(t0 >= WINDOW)
    def _():
        run(b_ref[...])

    @pl.when(t0 < WINDOW)
    def _():
        row = lax.broadcasted_iota(jnp.int32, (tq, span), 0)
        col = lax.broadcasted_iota(jnp.int32, (tq, span), 1)
        run(jnp.where(col <= t0 + row, 0.0, NEG_INF))


def nsa_window(p1, p2, T):
    B = p1.shape[0]
    tq = NSA_QBLOCK
    span = WINDOW + tq
    r = np.arange(tq)[:, None]
    c = np.arange(span)[None, :]
    band = jnp.asarray(np.where((c > r) & (c <= r + WINDOW), 0.0, NEG_INF), F32)
    return pl.pallas_call(
        functools.partial(_nsa_win_kernel, tq=tq),
        grid=(B, T // tq),
        in_specs=[pl.BlockSpec((None, tq, NSA_HEADS * LANES), lambda b, i: (b, i, 0)),
                  pl.BlockSpec((None, T, LANES), lambda b, i: (b, 0, P1_NKW // LANES)),
                  pl.BlockSpec((None, T, LANES), lambda b, i: (b, 0, P2_NVW // LANES)),
                  pl.BlockSpec((tq, span), lambda b, i: (0, 0))],
        out_specs=pl.BlockSpec((None, tq, NSA_HEADS * LANES), lambda b, i: (b, i, 0)),
        out_shape=jax.ShapeDtypeStruct((B, T, NSA_HEADS * LANES), BF16),
        compiler_params=_cparams("parallel", "parallel"),
        name="nsa_window",
    )(p1, p1, p2, band)


def _retention_kernel(q_ref, k_ref, v_ref, g_ref, din_ref, qd_ref, kd_ref, cd_ref, o_ref, st_ref):
    @pl.when(pl.program_id(0) == 0)
    def _():
        st_ref[...] = jnp.zeros_like(st_ref)

    B = q_ref.shape[0]
    for b in range(B):
        for h in range(RET_HEADS):
            lanes = slice(h * LANES, (h + 1) * LANES)
            qh = q_ref[b, :, lanes]
            kp = k_ref[b, :, (h // 2) * LANES:(h // 2 + 1) * LANES]
            vh = v_ref[b, :, lanes]
            st = st_ref[b, h]
            inner = _dot_nt(qh, kp) * din_ref[h]
            o = _dot(inner.astype(BF16), vh) + _dot(qh, st.astype(BF16)) * qd_ref[h]
            kd = (kp.astype(F32) * kd_ref[h]).astype(BF16)
            st_ref[b, h] = st * cd_ref[h, 0:1, :] + _dot_tn(kd, vh)
            mu = jnp.mean(o, axis=-1, keepdims=True)
            d = o - mu
            var = jnp.mean(d * d, axis=-1, keepdims=True)
            on = d * lax.rsqrt(var + NORM_EPS)
            gh = g_ref[b, :, lanes].astype(F32)
            o_ref[b, :, lanes] = (gh * _sigmoid(gh) * on).astype(o_ref.dtype)


def retention_consts():
    C = RET_CHUNK
    H = RET_HEADS
    log_g = jnp.log(1.0 - 2.0 ** (-5.0 - jnp.arange(H, dtype=F32)))
    n = jnp.arange(C, dtype=F32)
    diff = n[:, None] - n[None, :]
    causal = diff >= 0
    decay_in = jnp.where(causal[None], jnp.exp(jnp.where(causal, diff, 0.0)[None] * log_g[:, None, None]), 0.0)
    q_decay = jnp.exp((n[None, :] + 1.0) * log_g[:, None])
    k_decay = jnp.exp((C - 1.0 - n)[None, :] * log_g[:, None])
    chunk_decay = jnp.exp(C * log_g)
    qd = jnp.broadcast_to(q_decay[:, :, None], (H, C, LANES))
    kd = jnp.broadcast_to(k_decay[:, :, None], (H, C, LANES))
    cd = jnp.broadcast_to(chunk_decay[:, None, None], (H, 8, LANES))
    return decay_in, qd, kd, cd


def retention(p1, p2, consts, T):
    B = p1.shape[0]
    C = RET_CHUNK
    din, qd, kd, cd = consts
    W = RET_HEADS * LANES
    full = lambda shape: pl.BlockSpec(shape, lambda c: (0,) * len(shape))
    return pl.pallas_call(
        _retention_kernel,
        grid=(T // C,),
        in_specs=[pl.BlockSpec((B, C, W), lambda c: (0, c, P1_RQ // W)),
                  pl.BlockSpec((B, C, W // 2), lambda c: (0, c, P1_RK // (W // 2))),
                  pl.BlockSpec((B, C, W), lambda c: (0, c, P2_RV // W)),
                  pl.BlockSpec((B, C, W), lambda c: (0, c, P2_RG // W)),
                  full(din.shape), full(qd.shape), full(kd.shape), full(cd.shape)],
        out_specs=pl.BlockSpec((B, C, W), lambda c: (0, c, 0)),
        out_shape=jax.ShapeDtypeStruct((B, T, W), BF16),
        scratch_shapes=[pltpu.VMEM((B, RET_HEADS, LANES, LANES), F32)],
        compiler_params=_cparams("arbitrary"),
        name="retention",
    )(p1, p1, p2, p2, din, qd, kd, cd)


def _fox_cum_kernel(f_ref, b_ref, o_ref):
    x = f_ref[...] + b_ref[...]
    ls = jnp.minimum(x, 0.0) - jnp.log1p(jnp.exp(-jnp.abs(x)))
    R = x.shape[0]
    ki = lax.broadcasted_iota(jnp.int32, (LANES, LANES), 0)
    ji = lax.broadcasted_iota(jnp.int32, (LANES, LANES), 1)
    upper = jnp.where(ki <= ji, 1.0, 0.0).astype(BF16)
    hi, mid, lo = _split3(ls)
    rowcum = _dot(hi, upper) + _dot(mid, upper) + _dot(lo, upper)
    tot = jnp.broadcast_to(rowcum[:, LANES - 1:LANES], (R, LANES))
    ri = lax.broadcasted_iota(jnp.int32, (R, R), 0)
    ci = lax.broadcasted_iota(jnp.int32, (R, R), 1)
    lower = jnp.where(ci < ri, 1.0, 0.0).astype(BF16)
    hi, mid, lo = _split3(tot)
    offs = _dot(lower, hi) + _dot(lower, mid) + _dot(lower, lo)
    o_ref[...] = (rowcum + offs) * LOG2E


def fox_cum(f_logit, bias):
    B, H, R, _ = f_logit.shape
    return pl.pallas_call(
        _fox_cum_kernel,
        grid=(B, H),
        in_specs=[pl.BlockSpec((None, None, R, LANES), lambda b, h: (b, h, 0, 0)),
                  pl.BlockSpec((None, 1, LANES), lambda b, h: (h, 0, 0))],
        out_specs=pl.BlockSpec((None, None, R, LANES), lambda b, h: (b, h, 0, 0)),
        out_shape=jax.ShapeDtypeStruct((B, H, R, LANES), F32),
        compiler_params=_cparams("parallel", "parallel"),
        name="fox_cum",
    )(f_logit, bias)


FOX_BIAS_LANES = 3


def _fox_kernel(q_ref, k_ref, v_ref, c_ref, o_ref, ka_ref, va_ref, *, tq):
    i = pl.program_id(2)
    tk = tq
    T = k_ref.shape[0]
    chunk = 512

    @pl.when(i == 0)
    def _():
        lane = lax.broadcasted_iota(jnp.int32, (chunk, LANES), 1)
        ri = lax.broadcasted_iota(jnp.int32, (16, LANES), 0)
        ci = lax.broadcasted_iota(jnp.int32, (16, LANES), 1)
        place = jnp.where((ci == ri + HEAD_DIM) & (ri < FOX_BIAS_LANES), 1.0, 0.0).astype(BF16)

        def build(c, _):
            c0 = pl.multiple_of(c * chunk, chunk)
            kp = k_ref[pl.ds(c0, chunk), :].astype(F32)
            vp = v_ref[pl.ds(c0, chunk), :].astype(F32)
            for hh in range(2):
                hi, mid, lo = _split3(-c_ref[hh, :, pl.ds(c0, chunk)])
                terms = jnp.concatenate([hi, mid, lo, jnp.zeros((13, chunk), BF16)], axis=0)
                bias = _dot_tn(terms, place)
                kh = kp if hh == 0 else pltpu.roll(kp, HEAD_DIM, 1)
                vh = vp if hh == 0 else pltpu.roll(vp, HEAD_DIM, 1)
                ka_ref[hh, pl.ds(c0, chunk), :] = jnp.where(lane < HEAD_DIM, kh, bias).astype(BF16)
                va_ref[hh, pl.ds(c0, chunk), :] = jnp.where(lane < HEAD_DIM, vh, 1.0).astype(BF16)
            return 0

        lax.fori_loop(0, T // chunk, build, 0)

    row = lax.broadcasted_iota(jnp.int32, (tq, tk), 0)
    col = lax.broadcasted_iota(jnp.int32, (tq, tk), 1)
    lane = lax.broadcasted_iota(jnp.int32, (tq, LANES), 1)
    ones_lanes = (lane >= HEAD_DIM) & (lane < HEAD_DIM + FOX_BIAS_LANES)
    qs = [jnp.where(ones_lanes, 1.0, q_ref[:, hh * LANES:(hh + 1) * LANES].astype(F32)).astype(BF16)
          for hh in range(2)]

    def step(j, carry, masked):
        start = pl.multiple_of(j * tk, tk)
        out = []
        for hh in range(2):
            m, acc = carry[hh]
            s = _dot_nt(qs[hh], ka_ref[hh, pl.ds(start, tk), :])
            if masked:
                s = jnp.where(col <= row, s, NEG_INF)
            m_new = jnp.maximum(m, jnp.max(s, axis=-1, keepdims=True))
            p = jnp.exp2(s - m_new)
            acc = jnp.exp2(m - m_new) * acc + _dot(p.astype(BF16), va_ref[hh, pl.ds(start, tk), :])
            out.append((m_new, acc))
        return tuple(out)

    one = (jnp.full((tq, 1), NEG_INF, F32), jnp.zeros((tq, LANES), F32))
    carry = lax.fori_loop(0, i, functools.partial(step, masked=False), (one, one))
    (_, acc0), (_, acc1) = step(i, carry, True)
    o0 = acc0 / acc0[:, HEAD_DIM:HEAD_DIM + 1]
    o1 = acc1 / acc1[:, HEAD_DIM:HEAD_DIM + 1]
    o_ref[...] = jnp.where(lane < HEAD_DIM, o0, pltpu.roll(o1, HEAD_DIM, 1)).astype(o_ref.dtype)


def fox_attention(p2, cum, T, *, tq=FOX_TQ):
    B = p2.shape[0]
    HP = FOX_HEADS // 2
    return pl.pallas_call(
        functools.partial(_fox_kernel, tq=tq),
        grid=(B, HP, T // tq),
        in_specs=[pl.BlockSpec((None, tq, 2 * LANES), lambda b, h, i: (b, i, P2_FQ // (2 * LANES) + h)),
                  pl.BlockSpec((None, T, LANES), lambda b, h, i: (b, 0, P2_FK // LANES + h)),
                  pl.BlockSpec((None, T, LANES), lambda b, h, i: (b, 0, P2_FV // LANES + h)),
                  pl.BlockSpec((None, None, 2, 1, T), lambda b, h, i: (b, h, 0, 0, 0))],
        out_specs=pl.BlockSpec((None, tq, LANES), lambda b, h, i: (b, i, h)),
        out_shape=jax.ShapeDtypeStruct((B, T, FOX_HEADS * HEAD_DIM), BF16),
        scratch_shapes=[pltpu.VMEM((2, T, LANES), BF16), pltpu.VMEM((2, T, LANES), BF16)],
        compiler_params=_cparams("parallel", "parallel", "arbitrary"),
        name="fox_attention",
    )(p2, p2, p2, cum)


def _readout_kernel(ocmp_ref, osel_ref, owin_ref, small_ref, oret_ref, ofox_ref, mg_ref, x_ref, g1_ref,
                    ex_ref, wn_ref, wr_ref, wf_ref, wo_ref, o_ref):
    W = NSA_HEADS * LANES
    gs = _sigmoid(small_ref[...].astype(F32)).astype(BF16)
    ge = _dot(gs, ex_ref[...])
    onsa = (ge[:, :W] * ocmp_ref[...].astype(F32) + ge[:, W:2 * W] * osel_ref[...].astype(F32)
            + ge[:, 2 * W:] * owin_ref[...].astype(F32))
    D = D_MODEL
    merged = (_sigmoid(mg_ref[:, :D].astype(F32)) * _dot(onsa.astype(BF16), wn_ref[...])
              + _sigmoid(mg_ref[:, D:2 * D].astype(F32)) * _dot(oret_ref[...], wr_ref[...])
              + _sigmoid(mg_ref[:, 2 * D:].astype(F32)) * _dot(ofox_ref[...], wf_ref[...]))
    y = _dot(merged.astype(BF16), wo_ref[...])
    o_ref[...] = x_ref[...] + g1_ref[...] * y


def readout(o_cmp, o_sel, o_win, p2, o_ret, o_fox, x, mod_l, ex, wn, wr, wf, wo, T, *, tm=512):
    M, D = x.shape
    per_b = T // tm
    W = NSA_HEADS * LANES
    row = lambda width, col=0: pl.BlockSpec((tm, width), lambda i: (i, col))
    full = lambda a: pl.BlockSpec(a.shape, lambda i: (0,) * a.ndim)
    return pl.pallas_call(
        _readout_kernel,
        grid=(M // tm,),
        in_specs=[row(W), row(W), row(W), row(LANES, P2_SMALL // LANES), row(512), row(512),
                  row(3 * D, 0), row(D),
                  pl.BlockSpec((None, None, 1, D), lambda i: (i // per_b, 2, 0, 0)),
                  full(ex), full(wn), full(wr), full(wf), full(wo)],
        out_specs=row(D),
        out_shape=jax.ShapeDtypeStruct((M, D), F32),
        compiler_params=_cparams("parallel"),
        name="mixer_readout",
    )(o_cmp, o_sel, o_win, p2, o_ret, o_fox, p2, x, mod_l, ex, wn, wr, wf, wo)


def nsa_gate_expand():
    ex = np.zeros((LANES, 3 * NSA_HEADS * LANES), np.float32)
    for br in range(3):
        for h in range(NSA_HEADS):
            c0 = br * NSA_HEADS * LANES + h * LANES
            ex[br * NSA_HEADS + h, c0:c0 + LANES] = 1.0
    return jnp.asarray(ex, BF16)


def pad_read_nsa(w):
    D = w.shape[1]
    w = w.reshape(NSA_HEADS, HEAD_DIM, D)
    z = jnp.zeros_like(w)
    g = (np.arange(NSA_HEADS) // NSA_HPG)[:, None, None]
    lo = jnp.where(g == 0, w, z)
    hi = jnp.where(g == 1, w, z)
    return jnp.concatenate([lo, hi], axis=1).reshape(NSA_HEADS * LANES, D).astype(BF16)


def _ffn_kernel(*refs, gated):
    if gated:
        x_ref, nw_ref, sc_ref, sh_ref, g2_ref, gate_ref, w1_ref, w3_ref, w2_ref, o_ref, h_ref, acc_ref = refs
    else:
        x_ref, nw_ref, sc_ref, sh_ref, g2_ref, w1_ref, w3_ref, w2_ref, o_ref, h_ref, acc_ref = refs
    e = pl.program_id(1)
    f = pl.program_id(2)

    @pl.when((e == 0) & (f == 0))
    def _():
        h_ref[...] = _norm_mod(x_ref[...], nw_ref[...], sc_ref[...], sh_ref[...]).astype(BF16)
        acc_ref[...] = jnp.zeros_like(acc_ref)

    h = h_ref[...]
    u = _dot(h, w1_ref[...])
    v = _dot(h, w3_ref[...])
    a = (u * _sigmoid(u) * v).astype(BF16)
    y = _dot(a, w2_ref[...])
    if gated:
        gate = gate_ref[...]
        lane = lax.broadcasted_iota(jnp.int32, gate.shape, 1)
        y = y * jnp.sum(jnp.where(lane == e, gate, 0.0), axis=-1, keepdims=True)
    acc_ref[...] += y

    @pl.when((e == pl.num_programs(1) - 1) & (f == pl.num_programs(2) - 1))
    def _():
        o_ref[...] = x_ref[...] + g2_ref[...] * acc_ref[...]


def ffn(x, mod_l, nw, w1, w3, w2, gate, T, *, tm, tf):
    M, D = x.shape
    E, _, F = w1.shape
    per_b = T // tm
    gated = gate is not None
    modspec = lambda k: pl.BlockSpec((None, None, 1, D), lambda i, e, f: (i // per_b, k, 0, 0))
    in_specs = [pl.BlockSpec((tm, D), lambda i, e, f: (i, 0)),
                pl.BlockSpec((1, D), lambda i, e, f: (0, 0)),
                modspec(4), modspec(3), modspec(5)]
    args = [x, nw, mod_l, mod_l, mod_l]
    if gated:
        in_specs.append(pl.BlockSpec((tm, LANES), lambda i, e, f: (i, 0)))
        args.append(gate)
    in_specs += [pl.BlockSpec((None, D, tf), lambda i, e, f: (e, 0, f)),
                 pl.BlockSpec((None, D, tf), lambda i, e, f: (e, 0, f)),
                 pl.BlockSpec((None, tf, D), lambda i, e, f: (e, f, 0))]
    args += [w1, w3, w2]
    return pl.pallas_call(
        functools.partial(_ffn_kernel, gated=gated),
        grid=(M // tm, E, F // tf),
        in_specs=in_specs,
        out_specs=pl.BlockSpec((tm, D), lambda i, e, f: (i, 0)),
        out_shape=jax.ShapeDtypeStruct((M, D), F32),
        scratch_shapes=[pltpu.VMEM((tm, D), BF16), pltpu.VMEM((tm, D), F32)],
        compiler_params=_cparams("parallel", "arbitrary", "arbitrary"),
        name="ffn_gated" if gated else "ffn_dense",
    )(*args)


MOE_TC = 512
MOE_TS = 512


def _router_kernel(x_ref, nw_ref, sc_ref, sh_ref, wh_ref, wl_ref, h_ref, gate_ref, rank_ref, cnt_ref, carry_ref):
    @pl.when(pl.program_id(0) == 0)
    def _():
        carry_ref[...] = jnp.zeros_like(carry_ref)

    h = _norm_mod(x_ref[...], nw_ref[...], sc_ref[...], sh_ref[...])
    hh = h.astype(BF16)
    h_ref[...] = hh.astype(h_ref.dtype)
    hl = (h - hh.astype(F32)).astype(BF16)
    logits = _dot(hh, wh_ref[...]) + (_dot(hl, wh_ref[...]) + _dot(hh, wl_ref[...]))
    tm = logits.shape[0]
    lane = lax.broadcasted_iota(jnp.int32, logits.shape, 1)
    logits = jnp.where(lane < N_EXPERTS, logits, REMOVED)
    lane_f = lane.astype(F32)
    v1 = jnp.max(logits, axis=-1, keepdims=True)
    i1 = jnp.min(jnp.where(logits == v1, lane_f, float(LANES)), axis=-1, keepdims=True)
    rest = jnp.where(lane_f == i1, REMOVED, logits)
    v2 = jnp.max(rest, axis=-1, keepdims=True)
    i2 = jnp.min(jnp.where(rest == v2, lane_f, float(LANES)), axis=-1, keepdims=True)
    e2 = jnp.exp(v2 - v1)
    w1 = 1.0 / (1.0 + e2)
    w2 = e2 / (1.0 + e2)
    gate_ref[...] = jnp.where(lane_f == i1, w1, jnp.where(lane_f == i2, w2, 0.0))

    sel = jnp.where((lane_f == i1) | (lane_f == i2), 1.0, 0.0)
    ri = lax.broadcasted_iota(jnp.int32, (tm, tm), 0)
    ci = lax.broadcasted_iota(jnp.int32, (tm, tm), 1)
    before = jnp.where(ci < ri, 1.0, 0.0).astype(BF16)
    rank = _dot(before, sel.astype(BF16)) + carry_ref[0:1, :]
    rank_ref[...] = jnp.where(sel > 0.0, rank, -1.0)
    carry_ref[...] = carry_ref[...] + jnp.sum(sel, axis=0, keepdims=True)
    cnt_ref[...] = carry_ref[...]


def router(x, mod_l, nw, w_router, T):
    M, D = x.shape
    tm = MOE_TC
    per_b = T // tm
    wp = jnp.zeros((D, LANES), F32).at[:, :N_EXPERTS].set(w_router)
    wh = wp.astype(BF16)
    wl = (wp - wh.astype(F32)).astype(BF16)
    return pl.pallas_call(
        _router_kernel,
        grid=(M // tm,),
        in_specs=[pl.BlockSpec((tm, D), lambda i: (i, 0)),
                  pl.BlockSpec((1, D), lambda i: (0, 0))]
        + _mod_specs(T, tm, 4, 3, 1)
        + [pl.BlockSpec((D, LANES), lambda i: (0, 0)),
           pl.BlockSpec((D, LANES), lambda i: (0, 0))],
        out_specs=[pl.BlockSpec((tm, D), lambda i: (i, 0)),
                   pl.BlockSpec((tm, LANES), lambda i: (i, 0)),
                   pl.BlockSpec((tm, LANES), lambda i: (i, 0)),
                   pl.BlockSpec((8, LANES), lambda i: (0, 0))],
        out_shape=[jax.ShapeDtypeStruct((M, D), F32),
                   jax.ShapeDtypeStruct((M, LANES), F32),
                   jax.ShapeDtypeStruct((M, LANES), F32),
                   jax.ShapeDtypeStruct((8, LANES), F32)],
        scratch_shapes=[pltpu.VMEM((8, LANES), F32)],
        compiler_params=_cparams("arbitrary"),
        name="moe_router",
    )(x, nw, mod_l, mod_l, wh, wl)


def _count_le(sorted_vals, x):
    return jnp.sum(sorted_vals[None, :] <= x[:, None], axis=1, dtype=jnp.int32)


def _moe_up_kernel(e_r, total, x_ref, w1_ref, w3_ref, o_ref, w1b_ref, w3b_ref):
    r = pl.program_id(1)
    live = r < total[0]

    @pl.when(live & ((r == 0) | (e_r[r] != e_r[jnp.maximum(r - 1, 0)])))
    def _():
        w1b_ref[...] = w1_ref[...].astype(BF16)
        w3b_ref[...] = w3_ref[...].astype(BF16)

    @pl.when(live)
    def _():
        x = x_ref[...].astype(BF16)
        u = _dot(x, w1b_ref[...])
        v = _dot(x, w3b_ref[...])
        o_ref[...] = (u * _sigmoid(u) * v).astype(o_ref.dtype)


def moe_up(xs, w1, w3, tiles, rt, *, tf=896):
    R, D = xs.shape
    ts = MOE_TS
    F = w1.shape[-1]
    live = lambda r, total: jnp.minimum(r, total[0] - 1)
    return pl.pallas_call(
        _moe_up_kernel,
        grid_spec=pltpu.PrefetchScalarGridSpec(
            num_scalar_prefetch=2,
            grid=(F // tf, rt),
            in_specs=[pl.BlockSpec((ts, D), lambda n, r, e, total: (live(r, total), 0)),
                      pl.BlockSpec((None, D, tf), lambda n, r, e, total: (e[live(r, total)], 0, n)),
                      pl.BlockSpec((None, D, tf), lambda n, r, e, total: (e[live(r, total)], 0, n))],
            out_specs=pl.BlockSpec((ts, tf), lambda n, r, e, total: (r, n)),
            scratch_shapes=[pltpu.VMEM((D, tf), BF16), pltpu.VMEM((D, tf), BF16)],
        ),
        out_shape=jax.ShapeDtypeStruct((R, F), BF16),
        compiler_params=_cparams("arbitrary", "arbitrary"),
        name="moe_up",
    )(tiles["e"], tiles["total"], xs, w1, w3)


def _moe_down_kernel(e_r, total, a_ref, w2_ref, o_ref):
    @pl.when(pl.program_id(0) < total[0])
    def _():
        o_ref[...] = _dot(a_ref[...], w2_ref[...]).astype(o_ref.dtype)


def moe_down(a, w2, tiles, rt):
    R, F = a.shape
    ts = MOE_TS
    D = w2.shape[-1]
    live = lambda r, total: jnp.minimum(r, total[0] - 1)
    return pl.pallas_call(
        _moe_down_kernel,
        grid_spec=pltpu.PrefetchScalarGridSpec(
            num_scalar_prefetch=2,
            grid=(rt,),
            in_specs=[pl.BlockSpec((ts, F), lambda r, e, total: (live(r, total), 0)),
                      pl.BlockSpec((None, F, D), lambda r, e, total: (e[live(r, total)], 0, 0))],
            out_specs=pl.BlockSpec((ts, D), lambda r, e, total: (r, 0)),
        ),
        out_shape=jax.ShapeDtypeStruct((R, D), F32),
        compiler_params=_cparams("arbitrary"),
        name="moe_down",
    )(tiles["e"], tiles["total"], a, w2)


SC_WINDOW = 32


def _sc_mesh():
    return plsc.VectorSubcoreMesh(core_axis_name="core", subcore_axis_name="subcore")


def sc_scatter_rows2(x, idx_a, idx_b, n_out):
    n, d = x.shape
    steps = n // SC_WINDOW

    @pl.kernel(out_type=jax.ShapeDtypeStruct((n_out, d), x.dtype), mesh=_sc_mesh(), scratch_types=[])
    def kern(x_hbm, ia_hbm, ib_hbm, o_hbm):
        def body(x_vmem, ia_vmem, ib_vmem):
            pltpu.sync_copy(x_vmem, o_hbm.at[ia_vmem.at[0]])
            pltpu.sync_copy(x_vmem, o_hbm.at[ib_vmem.at[0]])

        pltpu.emit_pipeline(
            body,
            grid=(steps,),
            in_specs=[pl.BlockSpec((SC_WINDOW, d), index_map=lambda i: (i, 0)),
                      pl.BlockSpec((1, SC_WINDOW), index_map=lambda i: (i, 0)),
                      pl.BlockSpec((1, SC_WINDOW), index_map=lambda i: (i, 0))],
            out_specs=[],
            core_axis_name=("core", "subcore"),
            dimension_semantics=(pltpu.PARALLEL,),
        )(x_hbm, ia_hbm, ib_hbm)

    return kern(x, idx_a.reshape(steps, SC_WINDOW), idx_b.reshape(steps, SC_WINDOW))


def sc_gather_rows(x, idx):
    n = idx.shape[0]
    d = x.shape[1]
    steps = n // SC_WINDOW

    @pl.kernel(out_type=jax.ShapeDtypeStruct((n, d), x.dtype), mesh=_sc_mesh(), scratch_types=[])
    def kern(x_hbm, i_hbm, o_hbm):
        def body(i_vmem, o_vmem):
            pltpu.sync_copy(x_hbm.at[i_vmem.at[0]], o_vmem)

        pltpu.emit_pipeline(
            body,
            grid=(steps,),
            in_specs=[pl.BlockSpec((1, SC_WINDOW), index_map=lambda i: (i, 0))],
            out_specs=[pl.BlockSpec((SC_WINDOW, d), index_map=lambda i: (i, 0))],
            core_axis_name=("core", "subcore"),
            dimension_semantics=(pltpu.PARALLEL,),
        )(i_hbm, o_hbm)

    return kern(x, idx.reshape(steps, SC_WINDOW))


def _moe_finish_kernel(x_ref, g2_ref, ya_ref, yb_ref, gate_ref, rank_ref, nw_ref, o_ref, *, normalize):
    gate = gate_ref[...]
    chosen = rank_ref[...] >= 0.0
    lane = lax.broadcasted_iota(jnp.int32, gate.shape, 1).astype(F32)
    first = jnp.min(jnp.where(chosen, lane, float(LANES)), axis=-1, keepdims=True)
    last = jnp.max(jnp.where(chosen, lane, -1.0), axis=-1, keepdims=True)
    wa = jnp.sum(jnp.where(lane == first, gate, 0.0), axis=-1, keepdims=True)
    wb = jnp.sum(jnp.where(lane == last, gate, 0.0), axis=-1, keepdims=True)
    x = x_ref[...] + g2_ref[...] * (wa * ya_ref[...] + wb * yb_ref[...])
    if normalize:
        ms = jnp.mean(x * x, axis=-1, keepdims=True)
        x = x * lax.rsqrt(ms + NORM_EPS) * nw_ref[...]
    o_ref[...] = x


def moe_finish(x, mod_l, y2, gate, rank, norm_w, T, *, tm=512):
    M, D = x.shape
    per_b = T // tm
    normalize = norm_w is not None
    if norm_w is None:
        norm_w = jnp.ones((1, D), F32)
    return pl.pallas_call(
        functools.partial(_moe_finish_kernel, normalize=normalize),
        grid=(M // tm,),
        in_specs=[pl.BlockSpec((tm, D), lambda i: (i, 0)),
                  pl.BlockSpec((None, None, 1, D), lambda i: (i // per_b, 5, 0, 0)),
                  pl.BlockSpec((None, tm, D), lambda i: (0, i, 0)),
                  pl.BlockSpec((None, tm, D), lambda i: (1, i, 0)),
                  pl.BlockSpec((tm, LANES), lambda i: (i, 0)),
                  pl.BlockSpec((tm, LANES), lambda i: (i, 0)),
                  pl.BlockSpec((1, D), lambda i: (0, 0))],
        out_specs=pl.BlockSpec((tm, D), lambda i: (i, 0)),
        out_shape=jax.ShapeDtypeStruct((M, D), F32),
        compiler_params=_cparams("parallel"),
        name="moe_finish",
    )(x, mod_l, y2, y2, gate, rank, norm_w)


def moe_ffn(x, mod_l, nw, w_router, w1, w3, w2, T, norm_w=None):
    M = x.shape[0]
    ts = MOE_TS
    rt = (2 * M) // ts + N_EXPERTS
    h, gate, rank, cnt = router(x, mod_l, nw, w_router, T)
    i32 = jnp.int32
    counts = cnt[0, :N_EXPERTS].astype(i32)
    ntile = (counts + ts - 1) // ts
    tile_end = jnp.cumsum(ntile)
    row_off = (tile_end - ntile) * ts
    e_r = jnp.minimum(_count_le(tile_end, jnp.arange(rt, dtype=i32)), N_EXPERTS - 1)
    tiles = dict(e=e_r, total=tile_end[-1].reshape(1).astype(i32))
    rk = rank[:, :N_EXPERTS].astype(i32)
    pos = row_off[None, :] + rk
    pos_a = jnp.min(jnp.where(rk >= 0, pos, rt * ts), axis=1)
    pos_b = jnp.max(jnp.where(rk >= 0, pos, -1), axis=1)

    xs = sc_scatter_rows2(h, pos_a, pos_b, rt * ts)
    a = moe_up(xs, w1, w3, tiles, rt)
    y = moe_down(a, w2, tiles, rt)
    y2 = sc_gather_rows(y, jnp.concatenate([pos_a, pos_b])).reshape(2, M, -1)
    return moe_finish(x, mod_l, y2, gate, rank, norm_w, T)


def _final_norm_kernel(x_ref, w_ref, o_ref):
    x = x_ref[...]
    ms = jnp.mean(x * x, axis=-1, keepdims=True)
    o_ref[...] = x * lax.rsqrt(ms + NORM_EPS) * w_ref[...]


def final_norm(x, w, *, tm=1024):
    M, D = x.shape
    return pl.pallas_call(
        _final_norm_kernel,
        grid=(M // tm,),
        in_specs=[pl.BlockSpec((tm, D), lambda i: (i, 0)), pl.BlockSpec((1, D), lambda i: (0, 0))],
        out_specs=pl.BlockSpec((tm, D), lambda i: (i, 0)),
        out_shape=jax.ShapeDtypeStruct((M, D), F32),
        compiler_params=_cparams("parallel"),
        name="final_norm",
    )(x, w)


def nsa_constants(T):
    n_sel = T // SEL_LEN
    nsp = max(LANES, n_sel)
    ncp = T // CMP_STRIDE
    cmp_start = np.arange(ncp) * CMP_STRIDE
    sel_start = np.arange(nsp) * SEL_LEN
    ov = ((cmp_start[:, None] < sel_start[None, :] + SEL_LEN)
          & (cmp_start[:, None] + CMP_LEN > sel_start[None, :]))
    ov[(T - CMP_LEN) // CMP_STRIDE + 1:] = False
    ov[:, n_sel:] = False
    et_mat = ((np.arange(T)[:, None] // SEL_LEN) == np.arange(nsp)[None, :]) * SEL_BONUS
    return jnp.asarray(ov.T, BF16), jnp.asarray(et_mat, BF16)


def token_mixing(x, mod_l, lw, consts, B, T):
    M = B * T
    cos_t, sin_t, ov_t, e_mat, ret_consts, ex = consts
    p1 = proj_rope(x, mod_l, lw["norm_mix"], lw["w1"], cos_t, sin_t, p1_scales(), T).reshape(B, T, P1_COLS)
    p2 = proj_plain(x, mod_l, lw["norm_mix"], lw["w2"], T).reshape(B, T, P2_COLS)

    def group_rows(a):
        return a.reshape(B, T, NSA_GROUPS, HEAD_DIM).transpose(0, 2, 1, 3).reshape(
            B, NSA_GROUPS, T // CMP_STRIDE, CMP_STRIDE * HEAD_DIM)

    xr = jnp.stack([group_rows(p1[:, :, P1_NKC:P1_NKC + LANES]), group_rows(p2[:, :, P2_NVC:P2_NVC + LANES])])
    cmp_out = compress(xr, lw["cmp_pe"], lw["cmp_w1"], lw["cmp_w2"])
    cmp_out = cmp_out.transpose(0, 1, 3, 2, 4).reshape(2, B, T // CMP_STRIDE, LANES)
    o_cmp, sel = nsa_cmp_select(p1, cmp_out[0], cmp_out[1], ov_t, T)
    o_sel = nsa_selected(p1, nsa_value_augment(p2[:, :, P2_NVS:P2_NVS + LANES]), sel, e_mat, T)
    o_win = nsa_window(p1, p2, T)

    o_ret = retention(p1, p2, ret_consts, T)

    ff = p2[:, :, P2_SMALL + 3 * NSA_HEADS:P2_SMALL + 3 * NSA_HEADS + FOX_HEADS].astype(F32)
    ff = ff.transpose(0, 2, 1).reshape(B, FOX_HEADS, T // LANES, LANES)
    cum = fox_cum(ff, lw["fox_bias"]).reshape(B, FOX_HEADS // 2, 2, 1, T)
    o_fox = fox_attention(p2, cum, T)

    return readout(o_cmp.reshape(M, -1), o_sel.reshape(M, -1), o_win.reshape(M, -1), p2.reshape(M, P2_COLS),
                   o_ret.reshape(M, -1), o_fox.reshape(M, -1), x, mod_l, ex,
                   lw["wn"], lw["wr"], lw["wf"], lw["wo"], T)


def layer_weights(l, norm_mix, w_in, cmp_k_pe, cmp_k_w1, cmp_k_w2, cmp_v_pe, cmp_v_w1, cmp_v_w2, fox_f_bias,
                  w_read_nsa, w_read_ret, w_read_fox, w_out):
    w1, w2 = split_w_in(w_in[l])
    pe = jnp.stack([cmp_k_pe[l].reshape(1, -1), cmp_v_pe[l].reshape(1, -1)])
    pe = jnp.broadcast_to(pe, (2, 8, pe.shape[-1])).astype(BF16)
    return {
        "norm_mix": norm_mix[l].reshape(1, -1),
        "w1": w1, "w2": w2,
        "cmp_pe": pe,
        "cmp_w1": jnp.stack([cmp_k_w1[l], cmp_v_w1[l]]).astype(BF16),
        "cmp_w2": jnp.stack([cmp_k_w2[l], cmp_v_w2[l]]).astype(BF16),
        "fox_bias": jnp.broadcast_to(fox_f_bias[l][:, None, None], (FOX_HEADS, 1, LANES)),
        "wn": pad_read_nsa(w_read_nsa[l]),
        "wr": w_read_ret[l].astype(BF16),
        "wf": w_read_fox[l].astype(BF16),
        "wo": w_out[l].astype(BF16),
    }


def kernel(x, c, ada_w, ada_b, norm_mix, norm_ffn, w_in, cmp_k_pe, cmp_k_w1, cmp_k_w2, cmp_v_pe, cmp_v_w1,
           cmp_v_w2, fox_f_bias, w_read_nsa, w_read_ret, w_read_fox, w_out, ffn_w1, ffn_w3, ffn_w2, router_w,
           moe_w1, moe_w3, moe_w2, final_norm_w):
    B, T, D = x.shape
    M = B * T
    depth = ada_w.shape[0]
    mod = modulation(c, ada_w, ada_b)
    cos_t, sin_t = rope_tables(T)
    ov_t, e_mat = nsa_constants(T)
    consts = (cos_t, sin_t, ov_t, e_mat, retention_consts(), nsa_gate_expand())
    xs = x.reshape(M, D)
    for l in range(depth):
        lw = layer_weights(l, norm_mix, w_in, cmp_k_pe, cmp_k_w1, cmp_k_w2, cmp_v_pe, cmp_v_w1, cmp_v_w2,
                           fox_f_bias, w_read_nsa, w_read_ret, w_read_fox, w_out)
        xs = token_mixing(xs, mod[l], lw, consts, B, T)
        nf = norm_ffn[l].reshape(1, D)
        if l % 2 == 0:
            k = l // 2
            xs = ffn(xs, mod[l], nf, ffn_w1[k][None].astype(BF16), ffn_w3[k][None].astype(BF16),
                     ffn_w2[k][None].astype(BF16), None, T, tm=512, tf=D_FF // 2)
        else:
            k = l // 2
            fuse = final_norm_w.reshape(1, D) if l == depth - 1 else None
            xs = moe_ffn(xs, mod[l], nf, router_w[k], moe_w1[k], moe_w3[k], moe_w2[k].astype(BF16), T, fuse)
    if depth % 2 == 1:
        xs = final_norm(xs, final_norm_w.reshape(1, D))
    return xs.reshape(B, T, D)
```

```python
import functools
import math

import jax
import jax.numpy as jnp
import numpy as np
from jax import lax
from jax.experimental import pallas as pl
from jax.experimental.pallas import tpu as pltpu
from jax.experimental.pallas import tpu_sc as plsc

F32 = jnp.float32
BF16 = jnp.bfloat16

D_MODEL = 1024
DEPTH = 2
HEAD_DIM = 64
ROPE_THETA = 10000.0
NORM_EPS = 1e-6
NEG_INF = -1e30
REMOVED = -3e38

NSA_HEADS = 8
NSA_GROUPS = 2
NSA_HPG = NSA_HEADS // NSA_GROUPS
CMP_LEN = 32
CMP_STRIDE = 16
CMP_HIDDEN = 256
SEL_LEN = 64
SEL_TOPN = 16
WINDOW = 512
FORCE_SCORE = 1e4
NSA_QBLOCK = 128

RET_HEADS = 4
RET_QK_DIM = 64
RET_V_DIM = 128
RET_CHUNK = 128

FOX_HEADS = 8
FOX_TQ = 1024
LOG2E = 1.4426950408889634

D_FF = 2816
N_EXPERTS = 8
D_FF_EXPERT = 3584

LANES = 128
VMEM_LIMIT = 56 * 1024 * 1024

P1_NQ = 0
P1_RQ = 1024
P1_RK = 1536
P1_NKC = 1792
P1_NKS = 1920
P1_NKW = 2048
P1_COLS = 2176
P2_MG = 0
P2_RV = 3072
P2_RG = 3584
P2_FQ = 4096
P2_FK = 5120
P2_FV = 5632
P2_NVC = 6144
P2_NVS = 6272
P2_NVW = 6400
P2_SMALL = 6528
P2_COLS = 6656


def _cparams(*sem):
    return pltpu.CompilerParams(dimension_semantics=tuple(sem), vmem_limit_bytes=VMEM_LIMIT)


def _sigmoid(x):
    return 1.0 / (1.0 + jnp.exp(-x))


def _dot(a, b):
    return jnp.dot(a, b, preferred_element_type=F32)


def _dot_nt(a, b):
    return lax.dot_general(a, b, (((1,), (1,)), ((), ())), preferred_element_type=F32)


def _dot_tn(a, b):
    return lax.dot_general(a, b, (((0,), (0,)), ((), ())), preferred_element_type=F32)


def _split3(x):
    hi = x.astype(BF16)
    r1 = x - hi.astype(F32)
    mid = r1.astype(BF16)
    lo = (r1 - mid.astype(F32)).astype(BF16)
    return hi, mid, lo


def _norm_mod(x, nw, sc, sh):
    ms = jnp.mean(x * x, axis=-1, keepdims=True)
    y = x * lax.rsqrt(ms + NORM_EPS) * nw
    return y * (1.0 + sc) + sh


def _mod_kernel(c_ref, w_ref, b_ref, o_ref):
    c = c_ref[...]
    s = c * _sigmoid(c)
    o_ref[0] = _dot(s.astype(BF16), w_ref[0].astype(BF16)) + b_ref[0]


def modulation(c, ada_w, ada_b):
    B, D = c.shape
    depth = ada_w.shape[0]
    rows = 8
    c_pad = jnp.zeros((rows, D), F32).at[:B].set(c)
    out = pl.pallas_call(
        _mod_kernel,
        grid=(depth, 6),
        in_specs=[pl.BlockSpec((rows, D), lambda l, j: (0, 0)),
                  pl.BlockSpec((1, D, D), lambda l, j: (l, 0, j)),
                  pl.BlockSpec((1, 1, D), lambda l, j: (l, 0, j))],
        out_specs=pl.BlockSpec((1, rows, D), lambda l, j: (l, 0, j)),
        out_shape=jax.ShapeDtypeStruct((depth, rows, 6 * D), F32),
        compiler_params=_cparams("parallel", "parallel"),
        name="modulation",
    )(c_pad, ada_w, ada_b.reshape(depth, 1, 6 * D))
    return out[:, :B].reshape(depth, B, 6, 1, D)


def _proj_plain_kernel(x_ref, nw_ref, sc_ref, sh_ref, w_ref, o_ref, h_ref):
    @pl.when(pl.program_id(1) == 0)
    def _():
        h_ref[...] = _norm_mod(x_ref[...], nw_ref[...], sc_ref[...], sh_ref[...]).astype(BF16)

    o_ref[...] = _dot(h_ref[...], w_ref[...]).astype(o_ref.dtype)


def _proj_rope_kernel(x_ref, nw_ref, sc_ref, sh_ref, w_ref, cos_ref, sin_ref, o_ref, *, scales):
    h = _norm_mod(x_ref[...], nw_ref[...], sc_ref[...], sh_ref[...]).astype(BF16)
    y = _dot(h, w_ref[...])
    cos = cos_ref[...]
    sin = sin_ref[...]
    lane = lax.broadcasted_iota(jnp.int32, cos.shape, 1)
    first_half = (lane % HEAD_DIM) < (HEAD_DIM // 2)
    for g, scale in enumerate(scales):
        yg = y[:, g * LANES:(g + 1) * LANES]
        rot = jnp.where(first_half, pltpu.roll(yg, LANES - HEAD_DIM // 2, 1),
                        pltpu.roll(yg, HEAD_DIM // 2, 1))
        r = yg * cos + rot * sin
        if scale != 1.0:
            r = r * scale
        o_ref[:, g * LANES:(g + 1) * LANES] = r.astype(o_ref.dtype)


def _mod_specs(T, tm, sc_idx, sh_idx, nargs):
    per_b = T // tm
    if nargs == 1:
        return [pl.BlockSpec((None, None, 1, D_MODEL), lambda i: (i // per_b, sc_idx, 0, 0)),
                pl.BlockSpec((None, None, 1, D_MODEL), lambda i: (i // per_b, sh_idx, 0, 0))]
    return [pl.BlockSpec((None, None, 1, D_MODEL), lambda i, j: (i // per_b, sc_idx, 0, 0)),
            pl.BlockSpec((None, None, 1, D_MODEL), lambda i, j: (i // per_b, sh_idx, 0, 0))]


def proj_plain(x, mod_l, nw, w, T, *, tm=1024, tn=512):
    M, D = x.shape
    N = w.shape[1]
    return pl.pallas_call(
        _proj_plain_kernel,
        grid=(M // tm, N // tn),
        in_specs=[pl.BlockSpec((tm, D), lambda i, j: (i, 0)),
                  pl.BlockSpec((1, D), lambda i, j: (0, 0))]
        + _mod_specs(T, tm, 1, 0, 2)
        + [pl.BlockSpec((D, tn), lambda i, j: (0, j))],
        out_specs=pl.BlockSpec((tm, tn), lambda i, j: (i, j)),
        out_shape=jax.ShapeDtypeStruct((M, N), BF16),
        scratch_shapes=[pltpu.VMEM((tm, D), BF16)],
        compiler_params=_cparams("parallel", "arbitrary"),
        name="proj_plain",
    )(x, nw, mod_l, mod_l, w)


def proj_rope(x, mod_l, nw, w, cos, sin, scales, T, *, tm=512):
    M, D = x.shape
    N = w.shape[1]
    per_b = T // tm
    return pl.pallas_call(
        functools.partial(_proj_rope_kernel, scales=scales),
        grid=(M // tm,),
        in_specs=[pl.BlockSpec((tm, D), lambda i: (i, 0)),
                  pl.BlockSpec((1, D), lambda i: (0, 0))]
        + _mod_specs(T, tm, 1, 0, 1)
        + [pl.BlockSpec((D, N), lambda i: (0, 0)),
           pl.BlockSpec((tm, LANES), lambda i: (i % per_b, 0)),
           pl.BlockSpec((tm, LANES), lambda i: (i % per_b, 0))],
        out_specs=pl.BlockSpec((tm, N), lambda i: (i, 0)),
        out_shape=jax.ShapeDtypeStruct((M, N), BF16),
        compiler_params=_cparams("parallel"),
        name="proj_rope",
    )(x, nw, mod_l, mod_l, w, cos, sin)


def rope_tables(T):
    d = HEAD_DIM
    pos = jnp.arange(T, dtype=F32)
    inv = ROPE_THETA ** (-jnp.arange(0, d, 2, dtype=F32) / d)
    ang = pos[:, None] * inv[None, :]
    cos = jnp.cos(ang)
    sin = jnp.sin(ang)
    cos_t = jnp.concatenate([cos, cos, cos, cos], axis=-1)
    sin_t = jnp.concatenate([-sin, sin, -sin, sin], axis=-1)
    return cos_t, sin_t


def _pad_heads(w, n_heads, half_of_head):
    D = w.shape[0]
    w = w.reshape(D, n_heads, HEAD_DIM)
    z = jnp.zeros_like(w)
    halves = np.array([half_of_head(h) for h in range(n_heads)])
    lo = jnp.where(halves[None, :, None] == 0, w, z)
    hi = jnp.where(halves[None, :, None] == 1, w, z)
    return jnp.concatenate([lo, hi], axis=-1).reshape(D, n_heads * LANES)


def split_w_in(w_in):
    sizes = [512, 128, 128, 128, 128, 128, 128, 24, 256, 256, 512, 512, 512, 512, 512, 8, 3072]
    offs = np.cumsum([0] + sizes)
    (nq, nkc, nvc, nks, nvs, nkw, nvw, ngate, rq, rk, rv, rg, fq, fk, fv, ff, mg) = [
        w_in[:, offs[i]:offs[i + 1]] for i in range(len(sizes))]
    D = w_in.shape[0]
    nq_p = _pad_heads(nq, NSA_HEADS, lambda h: h // NSA_HPG)
    rq_p = _pad_heads(rq, RET_HEADS, lambda h: h % 2)
    fq_p = _pad_heads(fq, FOX_HEADS, lambda h: 0) * (HEAD_DIM ** -0.5 * LOG2E)
    small = jnp.concatenate([ngate, ff, jnp.zeros((D, LANES - 32), w_in.dtype)], axis=-1)
    w1 = jnp.concatenate([nq_p, rq_p, rk, nkc, nks, nkw], axis=-1).astype(BF16)
    w2 = jnp.concatenate([mg, rv, rg, fq_p, fk, fv, nvc, nvs, nvw, small], axis=-1).astype(BF16)
    assert w1.shape[1] == P1_COLS and w2.shape[1] == P2_COLS
    return w1, w2


def p1_scales():
    s = [1.0] * (P1_COLS // LANES)
    for g in range(P1_NQ // LANES, P1_RQ // LANES):
        s[g] = HEAD_DIM ** -0.5 * LOG2E
    for g in range(P1_RK // LANES, P1_NKC // LANES):
        s[g] = RET_QK_DIM ** -0.5
    return tuple(s)


def _compress_kernel(x_ref, pe_ref, w1_ref, w2_ref, o_ref):
    r = x_ref[...]
    half = r.shape[1]
    w1 = w1_ref[...]
    a = _dot(r, w1[:half])
    b = _dot(r, w1[half:])
    pe = _dot(pe_ref[...], w1)[0:1]
    n = a.shape[0]
    hid = a + pltpu.roll(b, n - 1, 0) + pe
    hid = hid * _sigmoid(hid)
    o_ref[...] = _dot(hid.astype(BF16), w2_ref[...]).astype(o_ref.dtype)


def compress(xr, pe, w1, w2):
    _, B, G, R, W = xr.shape
    H = w1.shape[-1]
    return pl.pallas_call(
        _compress_kernel,
        grid=(2, B, G),
        in_specs=[pl.BlockSpec((None, None, None, R, W), lambda s, b, g: (s, b, g, 0, 0)),
                  pl.BlockSpec((None, 8, 2 * W), lambda s, b, g: (s, 0, 0)),
                  pl.BlockSpec((None, 2 * W, H), lambda s, b, g: (s, 0, 0)),
                  pl.BlockSpec((None, H, HEAD_DIM), lambda s, b, g: (s, 0, 0))],
        out_specs=pl.BlockSpec((None, None, None, R, HEAD_DIM), lambda s, b, g: (s, b, g, 0, 0)),
        out_shape=jax.ShapeDtypeStruct((2, B, G, R, HEAD_DIM), BF16),
        compiler_params=_cparams("parallel", "parallel", "parallel"),
        name="nsa_compress",
    )(xr, pe, w1, w2)


def _stack_heads(q_ref, g):
    return jnp.concatenate(
        [q_ref[:, (NSA_HPG * g + hh) * LANES:(NSA_HPG * g + hh + 1) * LANES] for hh in range(NSA_HPG)],
        axis=0)


def _store_heads(o_ref, g, o, tq):
    for hh in range(NSA_HPG):
        h = NSA_HPG * g + hh
        o_ref[:, h * LANES:(h + 1) * LANES] = o[hh * tq:(hh + 1) * tq].astype(o_ref.dtype)


def _nsa_cmp_kernel(q_ref, kc_ref, vc_ref, ov_ref, o_ref, m_ref, *, tq, n_sel, top_n):
    t0 = pl.program_id(1) * tq
    kc = kc_ref[...]
    vc = vc_ref[...]
    ncp = kc.shape[0]
    nsp = ov_ref.shape[0]
    rows = NSA_HPG * tq
    n_idx = lax.broadcasted_iota(jnp.int32, (rows, ncp), 1)
    t_idx = t0 + lax.broadcasted_iota(jnp.int32, (rows, ncp), 0) % tq
    valid = (n_idx * CMP_STRIDE + (CMP_LEN - 1)) <= t_idx
    j_idx = lax.broadcasted_iota(jnp.int32, (nsp, tq), 0)
    cur = (t0 + lax.broadcasted_iota(jnp.int32, (nsp, tq), 1)) // SEL_LEN
    forced = (j_idx == 0) | (j_idx == cur) | (j_idx == cur - 1)
    j_f = j_idx.astype(F32)
    for g in range(NSA_GROUPS):
        q = _stack_heads(q_ref, g)
        s = jnp.where(valid, _dot_nt(q, kc), NEG_INF)
        m = jnp.max(s, axis=-1, keepdims=True)
        e = jnp.exp2(s - m)
        l = jnp.sum(e, axis=-1, keepdims=True)
        p = e * jnp.where(m > 0.5 * NEG_INF, 1.0 / l, 0.0)
        _store_heads(o_ref, g, _dot(p.astype(BF16), vc), tq)
        psum = p[0:tq]
        for hh in range(1, NSA_HPG):
            psum = psum + p[hh * tq:(hh + 1) * tq]
        imp_t = _dot_nt(ov_ref[...], psum.astype(BF16))
        score = jnp.where(j_idx <= cur, imp_t, NEG_INF)
        score = jnp.where(forced | (j_idx >= n_sel), REMOVED, score)
        sel = jnp.where(forced, 1.0, 0.0)
        for _ in range(top_n - 3):
            mx = jnp.max(score, axis=0, keepdims=True)
            idx = jnp.min(jnp.where(score == mx, j_f, float(nsp)), axis=0, keepdims=True)
            hit = j_f == idx
            sel = jnp.where(hit, 1.0, sel)
            score = jnp.where(hit, REMOVED, score)
        sel = jnp.where(j_idx <= cur, sel, 0.0)
        m_ref[g] = sel.T.astype(m_ref.dtype)


def nsa_cmp_select(p1, kc, vc, ov_t, T):
    B = p1.shape[0]
    tq = NSA_QBLOCK
    ncp = kc.shape[1]
    nsp = ov_t.shape[0]
    n_sel = T // SEL_LEN
    return pl.pallas_call(
        functools.partial(_nsa_cmp_kernel, tq=tq, n_sel=n_sel, top_n=min(SEL_TOPN, n_sel)),
        grid=(B, T // tq),
        in_specs=[pl.BlockSpec((None, tq, NSA_HEADS * LANES), lambda b, i: (b, i, 0)),
                  pl.BlockSpec((None, ncp, LANES), lambda b, i: (b, 0, 0)),
                  pl.BlockSpec((None, ncp, LANES), lambda b, i: (b, 0, 0)),
                  pl.BlockSpec((nsp, ncp), lambda b, i: (0, 0))],
        out_specs=[pl.BlockSpec((None, tq, NSA_HEADS * LANES), lambda b, i: (b, i, 0)),
                   pl.BlockSpec((None, NSA_GROUPS, tq, nsp), lambda b, i: (b, 0, i, 0))],
        out_shape=[jax.ShapeDtypeStruct((B, T, NSA_HEADS * LANES), BF16),
                   jax.ShapeDtypeStruct((B, NSA_GROUPS, T, nsp), BF16)],
        compiler_params=_cparams("parallel", "parallel"),
        name="nsa_cmp_select",
    )(p1, kc, vc, ov_t)


SEL_BONUS = 8192.0
NSA_SEL_TQ = 256
NSA_SEL_TK = 1024


def _nsa_sel_kernel(q_ref, k_ref, v_ref, m_ref, et_ref, o_ref, *, tq, tk):
    t0 = pl.program_id(1) * tq
    n_tiles = (t0 + tq + tk - 1) // tk
    rows = NSA_HPG * tq
    for g in range(NSA_GROUPS):
        q = jnp.concatenate([_stack_heads(q_ref, g), jnp.concatenate([m_ref[g]] * NSA_HPG, axis=0)], axis=1)
        den = HEAD_DIM * (1 - g)

        def step(j, carry, masked, q=q, g=g):
            m, acc = carry
            start = pl.multiple_of(j * tk, tk)
            ks = jnp.concatenate([k_ref[pl.ds(start, tk), :], et_ref[pl.ds(start, tk), :]], axis=1)
            s = _dot_nt(q, ks)
            if masked:
                trow = t0 + lax.broadcasted_iota(jnp.int32, (rows, tk), 0) % tq
                kpos = start + lax.broadcasted_iota(jnp.int32, (rows, tk), 1)
                s = jnp.where(kpos <= trow, s, NEG_INF)
            m_new = jnp.maximum(m, jnp.max(s, axis=-1, keepdims=True))
            p = jnp.exp2(s - m_new)
            acc = jnp.exp2(m - m_new) * acc + _dot(p.astype(BF16), v_ref[g, pl.ds(start, tk), :])
            return m_new, acc

        init = (jnp.full((rows, 1), NEG_INF, F32), jnp.zeros((rows, LANES), F32))
        carry = lax.fori_loop(0, n_tiles - 1, functools.partial(step, masked=False), init)
        _, acc = step(n_tiles - 1, carry, True)
        _store_heads(o_ref, g, acc / acc[:, den:den + 1], tq)


def nsa_value_augment(v):
    ones = jnp.ones_like(v[..., :HEAD_DIM])
    return jnp.stack([jnp.concatenate([v[..., :HEAD_DIM], ones], axis=-1),
                      jnp.concatenate([ones, v[..., HEAD_DIM:]], axis=-1)], axis=1)


def nsa_selected(p1, v_aug, sel, et_mat, T, *, tq=NSA_SEL_TQ, tk=NSA_SEL_TK):
    B = p1.shape[0]
    nsp = sel.shape[-1]
    return pl.pallas_call(
        functools.partial(_nsa_sel_kernel, tq=tq, tk=tk),
        grid=(B, T // tq),
        in_specs=[pl.BlockSpec((None, tq, NSA_HEADS * LANES), lambda b, i: (b, i, 0)),
                  pl.BlockSpec((None, T, LANES), lambda b, i: (b, 0, P1_NKS // LANES)),
                  pl.BlockSpec((None, NSA_GROUPS, T, LANES), lambda b, i: (b, 0, 0, 0)),
                  pl.BlockSpec((None, NSA_GROUPS, tq, nsp), lambda b, i: (b, 0, i, 0)),
                  pl.BlockSpec((T, nsp), lambda b, i: (0, 0))],
        out_specs=pl.BlockSpec((None, tq, NSA_HEADS * LANES), lambda b, i: (b, i, 0)),
        out_shape=jax.ShapeDtypeStruct((B, T, NSA_HEADS * LANES), BF16),
        compiler_params=_cparams("parallel", "parallel"),
        name="nsa_selected",
    )(p1, p1, v_aug, sel, et_mat)


def _nsa_win_kernel(q_ref, k_ref, v_ref, b_ref, o_ref, *, tq):
    t0 = pl.program_id(1) * tq
    span = WINDOW + tq
    start = pl.multiple_of(jnp.maximum(t0 - WINDOW, 0), tq)
    ks = k_ref[pl.ds(start, span), :]
    vs = v_ref[pl.ds(start, span), :]

    def run(bias):
        bias = jnp.concatenate([bias] * NSA_HPG, axis=0)
        for g in range(NSA_GROUPS):
            s = _dot_nt(_stack_heads(q_ref, g), ks) + bias
            m = jnp.max(s, axis=-1, keepdims=True)
            p = jnp.exp2(s - m)
            l = jnp.sum(p, axis=-1, keepdims=True)
            _store_heads(o_ref, g, _dot(p.astype(BF16), vs) / l, tq)

    @pl.when(t0 >= WINDOW)
    def _():
        run(b_ref[...])

    @pl.when(t0 < WINDOW)
    def _():
        row = lax.broadcasted_iota(jnp.int32, (tq, span), 0)
        col = lax.broadcasted_iota(jnp.int32, (tq, span), 1)
        run(jnp.where(col <= t0 + row, 0.0, NEG_INF))


def nsa_window(p1, p2, T):
    B = p1.shape[0]
    tq = NSA_QBLOCK
    span = WINDOW + tq
    r = np.arange(tq)[:, None]
    c = np.arange(span)[None, :]
    band = jnp.asarray(np.where((c > r) & (c <= r + WINDOW), 0.0, NEG_INF), F32)
    return pl.pallas_call(
        functools.partial(_nsa_win_kernel, tq=tq),
        grid=(B, T // tq),
        in_specs=[pl.BlockSpec((None, tq, NSA_HEADS * LANES), lambda b, i: (b, i, 0)),
                  pl.BlockSpec((None, T, LANES), lambda b, i: (b, 0, P1_NKW // LANES)),
                  pl.BlockSpec((None, T, LANES), lambda b, i: (b, 0, P2_NVW // LANES)),
                  pl.BlockSpec((tq, span), lambda b, i: (0, 0))],
        out_specs=pl.BlockSpec((None, tq, NSA_HEADS * LANES), lambda b, i: (b, i, 0)),
        out_shape=jax.ShapeDtypeStruct((B, T, NSA_HEADS * LANES), BF16),
        compiler_params=_cparams("parallel", "parallel"),
        name="nsa_window",
    )(p1, p1, p2, band)


def _retention_kernel(q_ref, k_ref, v_ref, g_ref, din_ref, qd_ref, kd_ref, cd_ref, o_ref, st_ref):
    @pl.when(pl.program_id(0) == 0)
    def _():
        st_ref[...] = jnp.zeros_like(st_ref)

    B = q_ref.shape[0]
    for b in range(B):
        for h in range(RET_HEADS):
            lanes = slice(h * LANES, (h + 1) * LANES)
            qh = q_ref[b, :, lanes]
            kp = k_ref[b, :, (h // 2) * LANES:(h // 2 + 1) * LANES]
            vh = v_ref[b, :, lanes]
            st = st_ref[b, h]
            inner = _dot_nt(qh, kp) * din_ref[h]
            o = _dot(inner.astype(BF16), vh) + _dot(qh, st.astype(BF16)) * qd_ref[h]
            kd = (kp.astype(F32) * kd_ref[h]).astype(BF16)
            st_ref[b, h] = st * cd_ref[h, 0:1, :] + _dot_tn(kd, vh)
            mu = jnp.mean(o, axis=-1, keepdims=True)
            d = o - mu
            var = jnp.mean(d * d, axis=-1, keepdims=True)
            on = d * lax.rsqrt(var + NORM_EPS)
            gh = g_ref[b, :, lanes].astype(F32)
            o_ref[b, :, lanes] = (gh * _sigmoid(gh) * on).astype(o_ref.dtype)


def retention_consts():
    C = RET_CHUNK
    H = RET_HEADS
    log_g = jnp.log(1.0 - 2.0 ** (-5.0 - jnp.arange(H, dtype=F32)))
    n = jnp.arange(C, dtype=F32)
    diff = n[:, None] - n[None, :]
    causal = diff >= 0
    decay_in = jnp.where(causal[None], jnp.exp(jnp.where(causal, diff, 0.0)[None] * log_g[:, None, None]), 0.0)
    q_decay = jnp.exp((n[None, :] + 1.0) * log_g[:, None])
    k_decay = jnp.exp((C - 1.0 - n)[None, :] * log_g[:, None])
    chunk_decay = jnp.exp(C * log_g)
    qd = jnp.broadcast_to(q_decay[:, :, None], (H, C, LANES))
    kd = jnp.broadcast_to(k_decay[:, :, None], (H, C, LANES))
    cd = jnp.broadcast_to(chunk_decay[:, None, None], (H, 8, LANES))
    return decay_in, qd, kd, cd


def retention(p1, p2, consts, T):
    B = p1.shape[0]
    C = RET_CHUNK
    din, qd, kd, cd = consts
    W = RET_HEADS * LANES
    full = lambda shape: pl.BlockSpec(shape, lambda c: (0,) * len(shape))
    return pl.pallas_call(
        _retention_kernel,
        grid=(T // C,),
        in_specs=[pl.BlockSpec((B, C, W), lambda c: (0, c, P1_RQ // W)),
                  pl.BlockSpec((B, C, W // 2), lambda c: (0, c, P1_RK // (W // 2))),
                  pl.BlockSpec((B, C, W), lambda c: (0, c, P2_RV // W)),
                  pl.BlockSpec((B, C, W), lambda c: (0, c, P2_RG // W)),
                  full(din.shape), full(qd.shape), full(kd.shape), full(cd.shape)],
        out_specs=pl.BlockSpec((B, C, W), lambda c: (0, c, 0)),
        out_shape=jax.ShapeDtypeStruct((B, T, W), BF16),
        scratch_shapes=[pltpu.VMEM((B, RET_HEADS, LANES, LANES), F32)],
        compiler_params=_cparams("arbitrary"),
        name="retention",
    )(p1, p1, p2, p2, din, qd, kd, cd)


def _fox_cum_kernel(f_ref, b_ref, o_ref):
    x = f_ref[...] + b_ref[...]
    ls = jnp.minimum(x, 0.0) - jnp.log1p(jnp.exp(-jnp.abs(x)))
    R = x.shape[0]
    ki = lax.broadcasted_iota(jnp.int32, (LANES, LANES), 0)
    ji = lax.broadcasted_iota(jnp.int32, (LANES, LANES), 1)
    upper = jnp.where(ki <= ji, 1.0, 0.0).astype(BF16)
    hi, mid, lo = _split3(ls)
    rowcum = _dot(hi, upper) + _dot(mid, upper) + _dot(lo, upper)
    tot = jnp.broadcast_to(rowcum[:, LANES - 1:LANES], (R, LANES))
    ri = lax.broadcasted_iota(jnp.int32, (R, R), 0)
    ci = lax.broadcasted_iota(jnp.int32, (R, R), 1)
    lower = jnp.where(ci < ri, 1.0, 0.0).astype(BF16)
    hi, mid, lo = _split3(tot)
    offs = _dot(lower, hi) + _dot(lower, mid) + _dot(lower, lo)
    o_ref[...] = (rowcum + offs) * LOG2E


def fox_cum(f_logit, bias):
    B, H, R, _ = f_logit.shape
    return pl.pallas_call(
        _fox_cum_kernel,
        grid=(B, H),
        in_specs=[pl.BlockSpec((None, None, R, LANES), lambda b, h: (b, h, 0, 0)),
                  pl.BlockSpec((None, 1, LANES), lambda b, h: (h, 0, 0))],
        out_specs=pl.BlockSpec((None, None, R, LANES), lambda b, h: (b, h, 0, 0)),
        out_shape=jax.ShapeDtypeStruct((B, H, R, LANES), F32),
        compiler_params=_cparams("parallel", "parallel"),
        name="fox_cum",
    )(f_logit, bias)


FOX_BIAS_LANES = 3


def _fox_kernel(q_ref, k_ref, v_ref, c_ref, o_ref, ka_ref, va_ref, *, tq):
    i = pl.program_id(2)
    tk = tq
    T = k_ref.shape[0]
    chunk = 512

    @pl.when(i == 0)
    def _():
        lane = lax.broadcasted_iota(jnp.int32, (chunk, LANES), 1)
        ri = lax.broadcasted_iota(jnp.int32, (16, LANES), 0)
        ci = lax.broadcasted_iota(jnp.int32, (16, LANES), 1)
        place = jnp.where((ci == ri + HEAD_DIM) & (ri < FOX_BIAS_LANES), 1.0, 0.0).astype(BF16)

        def build(c, _):
            c0 = pl.multiple_of(c * chunk, chunk)
            kp = k_ref[pl.ds(c0, chunk), :].astype(F32)
            vp = v_ref[pl.ds(c0, chunk), :].astype(F32)
            for hh in range(2):
                hi, mid, lo = _split3(-c_ref[hh, :, pl.ds(c0, chunk)])
                terms = jnp.concatenate([hi, mid, lo, jnp.zeros((13, chunk), BF16)], axis=0)
                bias = _dot_tn(terms, place)
                kh = kp if hh == 0 else pltpu.roll(kp, HEAD_DIM, 1)
                vh = vp if hh == 0 else pltpu.roll(vp, HEAD_DIM, 1)
                ka_ref[hh, pl.ds(c0, chunk), :] = jnp.where(lane < HEAD_DIM, kh, bias).astype(BF16)
                va_ref[hh, pl.ds(c0, chunk), :] = jnp.where(lane < HEAD_DIM, vh, 1.0).astype(BF16)
            return 0

        lax.fori_loop(0, T // chunk, build, 0)

    row = lax.broadcasted_iota(jnp.int32, (tq, tk), 0)
    col = lax.broadcasted_iota(jnp.int32, (tq, tk), 1)
    lane = lax.broadcasted_iota(jnp.int32, (tq, LANES), 1)
    ones_lanes = (lane >= HEAD_DIM) & (lane < HEAD_DIM + FOX_BIAS_LANES)
    qs = [jnp.where(ones_lanes, 1.0, q_ref[:, hh * LANES:(hh + 1) * LANES].astype(F32)).astype(BF16)
          for hh in range(2)]

    def step(j, carry, masked):
        start = pl.multiple_of(j * tk, tk)
        out = []
        for hh in range(2):
            m, acc = carry[hh]
            s = _dot_nt(qs[hh], ka_ref[hh, pl.ds(start, tk), :])
            if masked:
                s = jnp.where(col <= row, s, NEG_INF)
            m_new = jnp.maximum(m, jnp.max(s, axis=-1, keepdims=True))
            p = jnp.exp2(s - m_new)
            acc = jnp.exp2(m - m_new) * acc + _dot(p.astype(BF16), va_ref[hh, pl.ds(start, tk), :])
            out.append((m_new, acc))
        return tuple(out)

    one = (jnp.full((tq, 1), NEG_INF, F32), jnp.zeros((tq, LANES), F32))
    carry = lax.fori_loop(0, i, functools.partial(step, masked=False), (one, one))
    (_, acc0), (_, acc1) = step(i, carry, True)
    o0 = acc0 / acc0[:, HEAD_DIM:HEAD_DIM + 1]
    o1 = acc1 / acc1[:, HEAD_DIM:HEAD_DIM + 1]
    o_ref[...] = jnp.where(lane < HEAD_DIM, o0, pltpu.roll(o1, HEAD_DIM, 1)).astype(o_ref.dtype)


def fox_attention(p2, cum, T, *, tq=FOX_TQ):
    B = p2.shape[0]
    HP = FOX_HEADS // 2
    return pl.pallas_call(
        functools.partial(_fox_kernel, tq=tq),
        grid=(B, HP, T // tq),
        in_specs=[pl.BlockSpec((None, tq, 2 * LANES), lambda b, h, i: (b, i, P2_FQ // (2 * LANES) + h)),
                  pl.BlockSpec((None, T, LANES), lambda b, h, i: (b, 0, P2_FK // LANES + h)),
                  pl.BlockSpec((None, T, LANES), lambda b, h, i: (b, 0, P2_FV // LANES + h)),
                  pl.BlockSpec((None, None, 2, 1, T), lambda b, h, i: (b, h, 0, 0, 0))],
        out_specs=pl.BlockSpec((None, tq, LANES), lambda b, h, i: (b, i, h)),
        out_shape=jax.ShapeDtypeStruct((B, T, FOX_HEADS * HEAD_DIM), BF16),
        scratch_shapes=[pltpu.VMEM((2, T, LANES), BF16), pltpu.VMEM((2, T, LANES), BF16)],
        compiler_params=_cparams("parallel", "parallel", "arbitrary"),
        name="fox_attention",
    )(p2, p2, p2, cum)


def _readout_kernel(ocmp_ref, osel_ref, owin_ref, small_ref, oret_ref, ofox_ref, mg_ref, x_ref, g1_ref,
                    ex_ref, wn_ref, wr_ref, wf_ref, wo_ref, o_ref):
    W = NSA_HEADS * LANES
    gs = _sigmoid(small_ref[...].astype(F32)).astype(BF16)
    ge = _dot(gs, ex_ref[...])
    onsa = (ge[:, :W] * ocmp_ref[...].astype(F32) + ge[:, W:2 * W] * osel_ref[...].astype(F32)
            + ge[:, 2 * W:] * owin_ref[...].astype(F32))
    D = D_MODEL
    merged = (_sigmoid(mg_ref[:, :D].astype(F32)) * _dot(onsa.astype(BF16), wn_ref[...])
              + _sigmoid(mg_ref[:, D:2 * D].astype(F32)) * _dot(oret_ref[...], wr_ref[...])
              + _sigmoid(mg_ref[:, 2 * D:].astype(F32)) * _dot(ofox_ref[...], wf_ref[...]))
    y = _dot(merged.astype(BF16), wo_ref[...])
    o_ref[...] = x_ref[...] + g1_ref[...] * y


def readout(o_cmp, o_sel, o_win, p2, o_ret, o_fox, x, mod_l, ex, wn, wr, wf, wo, T, *, tm=512):
    M, D = x.shape
    per_b = T // tm
    W = NSA_HEADS * LANES
    row = lambda width, col=0: pl.BlockSpec((tm, width), lambda i: (i, col))
    full = lambda a: pl.BlockSpec(a.shape, lambda i: (0,) * a.ndim)
    return pl.pallas_call(
        _readout_kernel,
        grid=(M // tm,),
        in_specs=[row(W), row(W), row(W), row(LANES, P2_SMALL // LANES), row(512), row(512),
                  row(3 * D, 0), row(D),
                  pl.BlockSpec((None, None, 1, D), lambda i: (i // per_b, 2, 0, 0)),
                  full(ex), full(wn), full(wr), full(wf), full(wo)],
        out_specs=row(D),
        out_shape=jax.ShapeDtypeStruct((M, D), F32),
        compiler_params=_cparams("parallel"),
        name="mixer_readout",
    )(o_cmp, o_sel, o_win, p2, o_ret, o_fox, p2, x, mod_l, ex, wn, wr, wf, wo)


def nsa_gate_expand():
    ex = np.zeros((LANES, 3 * NSA_HEADS * LANES), np.float32)
    for br in range(3):
        for h in range(NSA_HEADS):
            c0 = br * NSA_HEADS * LANES + h * LANES
            ex[br * NSA_HEADS + h, c0:c0 + LANES] = 1.0
    return jnp.asarray(ex, BF16)


def pad_read_nsa(w):
    D = w.shape[1]
    w = w.reshape(NSA_HEADS, HEAD_DIM, D)
    z = jnp.zeros_like(w)
    g = (np.arange(NSA_HEADS) // NSA_HPG)[:, None, None]
    lo = jnp.where(g == 0, w, z)
    hi = jnp.where(g == 1, w, z)
    return jnp.concatenate([lo, hi], axis=1).reshape(NSA_HEADS * LANES, D).astype(BF16)


def _ffn_kernel(*refs, gated):
    if gated:
        x_ref, nw_ref, sc_ref, sh_ref, g2_ref, gate_ref, w1_ref, w3_ref, w2_ref, o_ref, h_ref, acc_ref = refs
    else:
        x_ref, nw_ref, sc_ref, sh_ref, g2_ref, w1_ref, w3_ref, w2_ref, o_ref, h_ref, acc_ref = refs
    e = pl.program_id(1)
    f = pl.program_id(2)

    @pl.when((e == 0) & (f == 0))
    def _():
        h_ref[...] = _norm_mod(x_ref[...], nw_ref[...], sc_ref[...], sh_ref[...]).astype(BF16)
        acc_ref[...] = jnp.zeros_like(acc_ref)

    h = h_ref[...]
    u = _dot(h, w1_ref[...])
    v = _dot(h, w3_ref[...])
    a = (u * _sigmoid(u) * v).astype(BF16)
    y = _dot(a, w2_ref[...])
    if gated:
        gate = gate_ref[...]
        lane = lax.broadcasted_iota(jnp.int32, gate.shape, 1)
        y = y * jnp.sum(jnp.where(lane == e, gate, 0.0), axis=-1, keepdims=True)
    acc_ref[...] += y

    @pl.when((e == pl.num_programs(1) - 1) & (f == pl.num_programs(2) - 1))
    def _():
        o_ref[...] = x_ref[...] + g2_ref[...] * acc_ref[...]


def ffn(x, mod_l, nw, w1, w3, w2, gate, T, *, tm, tf):
    M, D = x.shape
    E, _, F = w1.shape
    per_b = T // tm
    gated = gate is not None
    modspec = lambda k: pl.BlockSpec((None, None, 1, D), lambda i, e, f: (i // per_b, k, 0, 0))
    in_specs = [pl.BlockSpec((tm, D), lambda i, e, f: (i, 0)),
                pl.BlockSpec((1, D), lambda i, e, f: (0, 0)),
                modspec(4), modspec(3), modspec(5)]
    args = [x, nw, mod_l, mod_l, mod_l]
    if gated:
        in_specs.append(pl.BlockSpec((tm, LANES), lambda i, e, f: (i, 0)))
        args.append(gate)
    in_specs += [pl.BlockSpec((None, D, tf), lambda i, e, f: (e, 0, f)),
                 pl.BlockSpec((None, D, tf), lambda i, e, f: (e, 0, f)),
                 pl.BlockSpec((None, tf, D), lambda i, e, f: (e, f, 0))]
    args += [w1, w3, w2]
    return pl.pallas_call(
        functools.partial(_ffn_kernel, gated=gated),
        grid=(M // tm, E, F // tf),
        in_specs=in_specs,
        out_specs=pl.BlockSpec((tm, D), lambda i, e, f: (i, 0)),
        out_shape=jax.ShapeDtypeStruct((M, D), F32),
        scratch_shapes=[pltpu.VMEM((tm, D), BF16), pltpu.VMEM((tm, D), F32)],
        compiler_params=_cparams("parallel", "arbitrary", "arbitrary"),
        name="ffn_gated" if gated else "ffn_dense",
    )(*args)


MOE_TC = 512
MOE_TS = 512


def _router_kernel(x_ref, nw_ref, sc_ref, sh_ref, wh_ref, wl_ref, h_ref, gate_ref, rank_ref, cnt_ref, carry_ref):
    @pl.when(pl.program_id(0) == 0)
    def _():
        carry_ref[...] = jnp.zeros_like(carry_ref)

    h = _norm_mod(x_ref[...], nw_ref[...], sc_ref[...], sh_ref[...])
    hh = h.astype(BF16)
    h_ref[...] = hh.astype(h_ref.dtype)
    hl = (h - hh.astype(F32)).astype(BF16)
    logits = _dot(hh, wh_ref[...]) + (_dot(hl, wh_ref[...]) + _dot(hh, wl_ref[...]))
    tm = logits.shape[0]
    lane = lax.broadcasted_iota(jnp.int32, logits.shape, 1)
    logits = jnp.where(lane < N_EXPERTS, logits, REMOVED)
    lane_f = lane.astype(F32)
    v1 = jnp.max(logits, axis=-1, keepdims=True)
    i1 = jnp.min(jnp.where(logits == v1, lane_f, float(LANES)), axis=-1, keepdims=True)
    rest = jnp.where(lane_f == i1, REMOVED, logits)
    v2 = jnp.max(rest, axis=-1, keepdims=True)
    i2 = jnp.min(jnp.where(rest == v2, lane_f, float(LANES)), axis=-1, keepdims=True)
    e2 = jnp.exp(v2 - v1)
    w1 = 1.0 / (1.0 + e2)
    w2 = e2 / (1.0 + e2)
    gate_ref[...] = jnp.where(lane_f == i1, w1, jnp.where(lane_f == i2, w2, 0.0))

    sel = jnp.where((lane_f == i1) | (lane_f == i2), 1.0, 0.0)
    ri = lax.broadcasted_iota(jnp.int32, (tm, tm), 0)
    ci = lax.broadcasted_iota(jnp.int32, (tm, tm), 1)
    before = jnp.where(ci < ri, 1.0, 0.0).astype(BF16)
    rank = _dot(before, sel.astype(BF16)) + carry_ref[0:1, :]
    rank_ref[...] = jnp.where(sel > 0.0, rank, -1.0)
    carry_ref[...] = carry_ref[...] + jnp.sum(sel, axis=0, keepdims=True)
    cnt_ref[...] = carry_ref[...]


def router(x, mod_l, nw, w_router, T):
    M, D = x.shape
    tm = MOE_TC
    per_b = T // tm
    wp = jnp.zeros((D, LANES), F32).at[:, :N_EXPERTS].set(w_router)
    wh = wp.astype(BF16)
    wl = (wp - wh.astype(F32)).astype(BF16)
    return pl.pallas_call(
        _router_kernel,
        grid=(M // tm,),
        in_specs=[pl.BlockSpec((tm, D), lambda i: (i, 0)),
                  pl.BlockSpec((1, D), lambda i: (0, 0))]
        + _mod_specs(T, tm, 4, 3, 1)
        + [pl.BlockSpec((D, LANES), lambda i: (0, 0)),
           pl.BlockSpec((D, LANES), lambda i: (0, 0))],
        out_specs=[pl.BlockSpec((tm, D), lambda i: (i, 0)),
                   pl.BlockSpec((tm, LANES), lambda i: (i, 0)),
                   pl.BlockSpec((tm, LANES), lambda i: (i, 0)),
                   pl.BlockSpec((8, LANES), lambda i: (0, 0))],
        out_shape=[jax.ShapeDtypeStruct((M, D), F32),
                   jax.ShapeDtypeStruct((M, LANES), F32),
                   jax.ShapeDtypeStruct((M, LANES), F32),
                   jax.ShapeDtypeStruct((8, LANES), F32)],
        scratch_shapes=[pltpu.VMEM((8, LANES), F32)],
        compiler_params=_cparams("arbitrary"),
        name="moe_router",
    )(x, nw, mod_l, mod_l, wh, wl)


def _count_le(sorted_vals, x):
    return jnp.sum(sorted_vals[None, :] <= x[:, None], axis=1, dtype=jnp.int32)


def _moe_up_kernel(e_r, total, x_ref, w1_ref, w3_ref, o_ref, w1b_ref, w3b_ref):
    r = pl.program_id(1)
    live = r < total[0]

    @pl.when(live & ((r == 0) | (e_r[r] != e_r[jnp.maximum(r - 1, 0)])))
    def _():
        w1b_ref[...] = w1_ref[...].astype(BF16)
        w3b_ref[...] = w3_ref[...].astype(BF16)

    @pl.when(live)
    def _():
        x = x_ref[...].astype(BF16)
        u = _dot(x, w1b_ref[...])
        v = _dot(x, w3b_ref[...])
        o_ref[...] = (u * _sigmoid(u) * v).astype(o_ref.dtype)


def moe_up(xs, w1, w3, tiles, rt, *, tf=1792):
    R, D = xs.shape
    ts = MOE_TS
    F = w1.shape[-1]
    live = lambda r, total: jnp.minimum(r, total[0] - 1)
    return pl.pallas_call(
        _moe_up_kernel,
        grid_spec=pltpu.PrefetchScalarGridSpec(
            num_scalar_prefetch=2,
            grid=(F // tf, rt),
            in_specs=[pl.BlockSpec((ts, D), lambda n, r, e, total: (live(r, total), 0)),
                      pl.BlockSpec((None, D, tf), lambda n, r, e, total: (e[live(r, total)], 0, n)),
                      pl.BlockSpec((None, D, tf), lambda n, r, e, total: (e[live(r, total)], 0, n))],
            out_specs=pl.BlockSpec((ts, tf), lambda n, r, e, total: (r, n)),
            scratch_shapes=[pltpu.VMEM((D, tf), BF16), pltpu.VMEM((D, tf), BF16)],
        ),
        out_shape=jax.ShapeDtypeStruct((R, F), BF16),
        compiler_params=_cparams("arbitrary", "arbitrary"),
        name="moe_up",
    )(tiles["e"], tiles["total"], xs, w1, w3)


def _moe_down_kernel(e_r, total, a_ref, w2_ref, o_ref, w2b_ref):
    r = pl.program_id(0)
    live = r < total[0]

    @pl.when(live & ((r == 0) | (e_r[r] != e_r[jnp.maximum(r - 1, 0)])))
    def _():
        w2b_ref[...] = w2_ref[...].astype(BF16)

    @pl.when(live)
    def _():
        o_ref[...] = _dot(a_ref[...], w2b_ref[...]).astype(o_ref.dtype)


def moe_down(a, w2, tiles, rt):
    R, F = a.shape
    ts = MOE_TS
    D = w2.shape[-1]
    live = lambda r, total: jnp.minimum(r, total[0] - 1)
    return pl.pallas_call(
        _moe_down_kernel,
        grid_spec=pltpu.PrefetchScalarGridSpec(
            num_scalar_prefetch=2,
            grid=(rt,),
            in_specs=[pl.BlockSpec((ts, F), lambda r, e, total: (live(r, total), 0)),
                      pl.BlockSpec((None, F, D), lambda r, e, total: (e[live(r, total)], 0, 0))],
            out_specs=pl.BlockSpec((ts, D), lambda r, e, total: (r, 0)),
            scratch_shapes=[pltpu.VMEM((F, D), BF16)],
        ),
        out_shape=jax.ShapeDtypeStruct((R, D), F32),
        compiler_params=_cparams("arbitrary"),
        name="moe_down",
    )(tiles["e"], tiles["total"], a, w2)


SC_WINDOW = 32


def _sc_mesh():
    return plsc.VectorSubcoreMesh(core_axis_name="core", subcore_axis_name="subcore")


def sc_scatter_rows2(x, idx_a, idx_b, n_out):
    n, d = x.shape
    steps = n // SC_WINDOW

    @pl.kernel(out_type=jax.ShapeDtypeStruct((n_out, d), x.dtype), mesh=_sc_mesh(), scratch_types=[])
    def kern(x_hbm, ia_hbm, ib_hbm, o_hbm):
        def body(x_vmem, ia_vmem, ib_vmem):
            pltpu.sync_copy(x_vmem, o_hbm.at[ia_vmem.at[0]])
            pltpu.sync_copy(x_vmem, o_hbm.at[ib_vmem.at[0]])

        pltpu.emit_pipeline(
            body,
            grid=(steps,),
            in_specs=[pl.BlockSpec((SC_WINDOW, d), index_map=lambda i: (i, 0)),
                      pl.BlockSpec((1, SC_WINDOW), index_map=lambda i: (i, 0)),
                      pl.BlockSpec((1, SC_WINDOW), index_map=lambda i: (i, 0))],
            out_specs=[],
            core_axis_name=("core", "subcore"),
            dimension_semantics=(pltpu.PARALLEL,),
        )(x_hbm, ia_hbm, ib_hbm)

    return kern(x, idx_a.reshape(steps, SC_WINDOW), idx_b.reshape(steps, SC_WINDOW))


def sc_gather_rows(x, idx):
    n = idx.shape[0]
    d = x.shape[1]
    steps = n // SC_WINDOW

    @pl.kernel(out_type=jax.ShapeDtypeStruct((n, d), x.dtype), mesh=_sc_mesh(), scratch_types=[])
    def kern(x_hbm, i_hbm, o_hbm):
        def body(i_vmem, o_vmem):
            pltpu.sync_copy(x_hbm.at[i_vmem.at[0]], o_vmem)

        pltpu.emit_pipeline(
            body,
            grid=(steps,),
            in_specs=[pl.BlockSpec((1, SC_WINDOW), index_map=lambda i: (i, 0))],
            out_specs=[pl.BlockSpec((SC_WINDOW, d), index_map=lambda i: (i, 0))],
            core_axis_name=("core", "subcore"),
            dimension_semantics=(pltpu.PARALLEL,),
        )(i_hbm, o_hbm)

    return kern(x, idx.reshape(steps, SC_WINDOW))


def _moe_finish_kernel(x_ref, g2_ref, ya_ref, yb_ref, gate_ref, rank_ref, nw_ref, o_ref, *, normalize):
    gate = gate_ref[...]
    chosen = rank_ref[...] >= 0.0
    lane = lax.broadcasted_iota(jnp.int32, gate.shape, 1).astype(F32)
    first = jnp.min(jnp.where(chosen, lane, float(LANES)), axis=-1, keepdims=True)
    last = jnp.max(jnp.where(chosen, lane, -1.0), axis=-1, keepdims=True)
    wa = jnp.sum(jnp.where(lane == first, gate, 0.0), axis=-1, keepdims=True)
    wb = jnp.sum(jnp.where(lane == last, gate, 0.0), axis=-1, keepdims=True)
    x = x_ref[...] + g2_ref[...] * (wa * ya_ref[...] + wb * yb_ref[...])
    if normalize:
        ms = jnp.mean(x * x, axis=-1, keepdims=True)
        x = x * lax.rsqrt(ms + NORM_EPS) * nw_ref[...]
    o_ref[...] = x


def moe_finish(x, mod_l, y2, gate, rank, norm_w, T, *, tm=512):
    M, D = x.shape
    per_b = T // tm
    normalize = norm_w is not None
    if norm_w is None:
        norm_w = jnp.ones((1, D), F32)
    return pl.pallas_call(
        functools.partial(_moe_finish_kernel, normalize=normalize),
        grid=(M // tm,),
        in_specs=[pl.BlockSpec((tm, D), lambda i: (i, 0)),
                  pl.BlockSpec((None, None, 1, D), lambda i: (i // per_b, 5, 0, 0)),
                  pl.BlockSpec((None, tm, D), lambda i: (0, i, 0)),
                  pl.BlockSpec((None, tm, D), lambda i: (1, i, 0)),
                  pl.BlockSpec((tm, LANES), lambda i: (i, 0)),
                  pl.BlockSpec((tm, LANES), lambda i: (i, 0)),
                  pl.BlockSpec((1, D), lambda i: (0, 0))],
        out_specs=pl.BlockSpec((tm, D), lambda i: (i, 0)),
        out_shape=jax.ShapeDtypeStruct((M, D), F32),
        compiler_params=_cparams("parallel"),
        name="moe_finish",
    )(x, mod_l, y2, y2, gate, rank, norm_w)


def moe_ffn(x, mod_l, nw, w_router, w1, w3, w2, T, norm_w=None):
    M = x.shape[0]
    ts = MOE_TS
    rt = (2 * M) // ts + N_EXPERTS
    h, gate, rank, cnt = router(x, mod_l, nw, w_router, T)
    i32 = jnp.int32
    counts = cnt[0, :N_EXPERTS].astype(i32)
    ntile = (counts + ts - 1) // ts
    tile_end = jnp.cumsum(ntile)
    row_off = (tile_end - ntile) * ts
    e_r = jnp.minimum(_count_le(tile_end, jnp.arange(rt, dtype=i32)), N_EXPERTS - 1)
    tiles = dict(e=e_r, total=tile_end[-1].reshape(1).astype(i32))
    rk = rank[:, :N_EXPERTS].astype(i32)
    pos = row_off[None, :] + rk
    pos_a = jnp.min(jnp.where(rk >= 0, pos, rt * ts), axis=1)
    pos_b = jnp.max(jnp.where(rk >= 0, pos, -1), axis=1)

    xs = sc_scatter_rows2(h, pos_a, pos_b, rt * ts)
    a = moe_up(xs, w1, w3, tiles, rt)
    y = moe_down(a, w2, tiles, rt)
    y2 = sc_gather_rows(y, jnp.concatenate([pos_a, pos_b])).reshape(2, M, -1)
    return moe_finish(x, mod_l, y2, gate, rank, norm_w, T)


def _final_norm_kernel(x_ref, w_ref, o_ref):
    x = x_ref[...]
    ms = jnp.mean(x * x, axis=-1, keepdims=True)
    o_ref[...] = x * lax.rsqrt(ms + NORM_EPS) * w_ref[...]


def final_norm(x, w, *, tm=1024):
    M, D = x.shape
    return pl.pallas_call(
        _final_norm_kernel,
        grid=(M // tm,),
        in_specs=[pl.BlockSpec((tm, D), lambda i: (i, 0)), pl.BlockSpec((1, D), lambda i: (0, 0))],
        out_specs=pl.BlockSpec((tm, D), lambda i: (i, 0)),
        out_shape=jax.ShapeDtypeStruct((M, D), F32),
        compiler_params=_cparams("parallel"),
        name="final_norm",
    )(x, w)


def nsa_constants(T):
    n_sel = T // SEL_LEN
    nsp = max(LANES, n_sel)
    ncp = T // CMP_STRIDE
    cmp_start = np.arange(ncp) * CMP_STRIDE
    sel_start = np.arange(nsp) * SEL_LEN
    ov = ((cmp_start[:, None] < sel_start[None, :] + SEL_LEN)
          & (cmp_start[:, None] + CMP_LEN > sel_start[None, :]))
    ov[(T - CMP_LEN) // CMP_STRIDE + 1:] = False
    ov[:, n_sel:] = False
    et_mat = ((np.arange(T)[:, None] // SEL_LEN) == np.arange(nsp)[None, :]) * SEL_BONUS
    return jnp.asarray(ov.T, BF16), jnp.asarray(et_mat, BF16)


def token_mixing(x, mod_l, lw, consts, B, T):
    M = B * T
    cos_t, sin_t, ov_t, e_mat, ret_consts, ex = consts
    p1 = proj_rope(x, mod_l, lw["norm_mix"], lw["w1"], cos_t, sin_t, p1_scales(), T).reshape(B, T, P1_COLS)
    p2 = proj_plain(x, mod_l, lw["norm_mix"], lw["w2"], T).reshape(B, T, P2_COLS)

    def group_rows(a):
        return a.reshape(B, T, NSA_GROUPS, HEAD_DIM).transpose(0, 2, 1, 3).reshape(
            B, NSA_GROUPS, T // CMP_STRIDE, CMP_STRIDE * HEAD_DIM)

    xr = jnp.stack([group_rows(p1[:, :, P1_NKC:P1_NKC + LANES]), group_rows(p2[:, :, P2_NVC:P2_NVC + LANES])])
    cmp_out = compress(xr, lw["cmp_pe"], lw["cmp_w1"], lw["cmp_w2"])
    cmp_out = cmp_out.transpose(0, 1, 3, 2, 4).reshape(2, B, T // CMP_STRIDE, LANES)
    o_cmp, sel = nsa_cmp_select(p1, cmp_out[0], cmp_out[1], ov_t, T)
    o_sel = nsa_selected(p1, nsa_value_augment(p2[:, :, P2_NVS:P2_NVS + LANES]), sel, e_mat, T)
    o_win = nsa_window(p1, p2, T)

    o_ret = retention(p1, p2, ret_consts, T)

    ff = p2[:, :, P2_SMALL + 3 * NSA_HEADS:P2_SMALL + 3 * NSA_HEADS + FOX_HEADS].astype(F32)
    ff = ff.transpose(0, 2, 1).reshape(B, FOX_HEADS, T // LANES, LANES)
    cum = fox_cum(ff, lw["fox_bias"]).reshape(B, FOX_HEADS // 2, 2, 1, T)
    o_fox = fox_attention(p2, cum, T)

    return readout(o_cmp.reshape(M, -1), o_sel.reshape(M, -1), o_win.reshape(M, -1), p2.reshape(M, P2_COLS),
                   o_ret.reshape(M, -1), o_fox.reshape(M, -1), x, mod_l, ex,
                   lw["wn"], lw["wr"], lw["wf"], lw["wo"], T)


def layer_weights(l, norm_mix, w_in, cmp_k_pe, cmp_k_w1, cmp_k_w2, cmp_v_pe, cmp_v_w1, cmp_v_w2, fox_f_bias,
                  w_read_nsa, w_read_ret, w_read_fox, w_out):
    w1, w2 = split_w_in(w_in[l])
    pe = jnp.stack([cmp_k_pe[l].reshape(1, -1), cmp_v_pe[l].reshape(1, -1)])
    pe = jnp.broadcast_to(pe, (2, 8, pe.shape[-1])).astype(BF16)
    return {
        "norm_mix": norm_mix[l].reshape(1, -1),
        "w1": w1, "w2": w2,
        "cmp_pe": pe,
        "cmp_w1": jnp.stack([cmp_k_w1[l], cmp_v_w1[l]]).astype(BF16),
        "cmp_w2": jnp.stack([cmp_k_w2[l], cmp_v_w2[l]]).astype(BF16),
        "fox_bias": jnp.broadcast_to(fox_f_bias[l][:, None, None], (FOX_HEADS, 1, LANES)),
        "wn": pad_read_nsa(w_read_nsa[l]),
        "wr": w_read_ret[l].astype(BF16),
        "wf": w_read_fox[l].astype(BF16),
        "wo": w_out[l].astype(BF16),
    }


def kernel(x, c, ada_w, ada_b, norm_mix, norm_ffn, w_in, cmp_k_pe, cmp_k_w1, cmp_k_w2, cmp_v_pe, cmp_v_w1,
           cmp_v_w2, fox_f_bias, w_read_nsa, w_read_ret, w_read_fox, w_out, ffn_w1, ffn_w3, ffn_w2, router_w,
           moe_w1, moe_w3, moe_w2, final_norm_w):
    B, T, D = x.shape
    M = B * T
    depth = ada_w.shape[0]
    mod = modulation(c, ada_w, ada_b)
    cos_t, sin_t = rope_tables(T)
    ov_t, e_mat = nsa_constants(T)
    consts = (cos_t, sin_t, ov_t, e_mat, retention_consts(), nsa_gate_expand())
    xs = x.reshape(M, D)
    for l in range(depth):
        lw = layer_weights(l, norm_mix, w_in, cmp_k_pe, cmp_k_w1, cmp_k_w2, cmp_v_pe, cmp_v_w1, cmp_v_w2,
                           fox_f_bias, w_read_nsa, w_read_ret, w_read_fox, w_out)
        xs = token_mixing(xs, mod[l], lw, consts, B, T)
        nf = norm_ffn[l].reshape(1, D)
        if l % 2 == 0:
            k = l // 2
            xs = ffn(xs, mod[l], nf, ffn_w1[k][None].astype(BF16), ffn_w3[k][None].astype(BF16),
                     ffn_w2[k][None].astype(BF16), None, T, tm=512, tf=D_FF // 2)
        else:
            k = l // 2
            fuse = final_norm_w.reshape(1, D) if l == depth - 1 else None
            xs = moe_ffn(xs, mod[l], nf, router_w[k], moe_w1[k], moe_w3[k], moe_w2[k], T, fuse)
    if depth % 2 == 1:
        xs = final_norm(xs, final_norm_w.reshape(1, D))
    return xs.reshape(B, T, D)
```

```python
import functools
import math

import jax
import jax.numpy as jnp
import numpy as np
from jax import lax
from jax.experimental import pallas as pl
from jax.experimental.pallas import tpu as pltpu
from jax.experimental.pallas import tpu_sc as plsc

F32 = jnp.float32
BF16 = jnp.bfloat16

D_MODEL = 1024
DEPTH = 2
HEAD_DIM = 64
ROPE_THETA = 10000.0
NORM_EPS = 1e-6
NEG_INF = -1e30
REMOVED = -3e38

NSA_HEADS = 8
NSA_GROUPS = 2
NSA_HPG = NSA_HEADS // NSA_GROUPS
CMP_LEN = 32
CMP_STRIDE = 16
CMP_HIDDEN = 256
SEL_LEN = 64
SEL_TOPN = 16
WINDOW = 512
FORCE_SCORE = 1e4
NSA_QBLOCK = 128

RET_HEADS = 4
RET_QK_DIM = 64
RET_V_DIM = 128
RET_CHUNK = 128

FOX_HEADS = 8
FOX_TQ = 1024
LOG2E = 1.4426950408889634

D_FF = 2816
N_EXPERTS = 8
D_FF_EXPERT = 3584

LANES = 128
VMEM_LIMIT = 56 * 1024 * 1024

P1_NQ = 0
P1_RQ = 512
P1_RK = 768
P1_NKC = 1024
P1_NKS = 1152
P1_NKW = 1280
P1_COLS = 1408
P2_MG = 0
P2_RV = 3072
P2_RG = 3584
P2_FQ = 4096
P2_FK = 4608
P2_FV = 5120
P2_NVC = 5632
P2_NVS = 5760
P2_NVW = 5888
P2_SMALL = 6016
P2_COLS = 6144
NSA_OUT = NSA_HEADS * LANES


def _cparams(*sem):
    return pltpu.CompilerParams(dimension_semantics=tuple(sem), vmem_limit_bytes=VMEM_LIMIT)


def _sigmoid(x):
    return 1.0 / (1.0 + jnp.exp(-x))


def _dot(a, b):
    return jnp.dot(a, b, preferred_element_type=F32)


def _dot_nt(a, b):
    return lax.dot_general(a, b, (((1,), (1,)), ((), ())), preferred_element_type=F32)


def _dot_tn(a, b):
    return lax.dot_general(a, b, (((0,), (0,)), ((), ())), preferred_element_type=F32)


def _split3(x):
    hi = x.astype(BF16)
    r1 = x - hi.astype(F32)
    mid = r1.astype(BF16)
    lo = (r1 - mid.astype(F32)).astype(BF16)
    return hi, mid, lo


def _norm_mod(x, nw, sc, sh):
    ms = jnp.mean(x * x, axis=-1, keepdims=True)
    y = x * lax.rsqrt(ms + NORM_EPS) * nw
    return y * (1.0 + sc) + sh


def _mod_kernel(c_ref, w_ref, b_ref, o_ref):
    c = c_ref[...]
    s = c * _sigmoid(c)
    o_ref[0] = _dot(s.astype(BF16), w_ref[0].astype(BF16)) + b_ref[0]


def modulation(c, ada_w, ada_b):
    B, D = c.shape
    depth = ada_w.shape[0]
    rows = 8
    c_pad = jnp.zeros((rows, D), F32).at[:B].set(c)
    out = pl.pallas_call(
        _mod_kernel,
        grid=(depth, 6),
        in_specs=[pl.BlockSpec((rows, D), lambda l, j: (0, 0)),
                  pl.BlockSpec((1, D, D), lambda l, j: (l, 0, j)),
                  pl.BlockSpec((1, 1, D), lambda l, j: (l, 0, j))],
        out_specs=pl.BlockSpec((1, rows, D), lambda l, j: (l, 0, j)),
        out_shape=jax.ShapeDtypeStruct((depth, rows, 6 * D), F32),
        compiler_params=_cparams("parallel", "parallel"),
        name="modulation",
    )(c_pad, ada_w, ada_b.reshape(depth, 1, 6 * D))
    return out[:, :B].reshape(depth, B, 6, 1, D)


def _proj_plain_kernel(x_ref, nw_ref, sc_ref, sh_ref, w_ref, o_ref, h_ref):
    @pl.when(pl.program_id(1) == 0)
    def _():
        h_ref[...] = _norm_mod(x_ref[...], nw_ref[...], sc_ref[...], sh_ref[...]).astype(BF16)

    o_ref[...] = _dot(h_ref[...], w_ref[...]).astype(o_ref.dtype)


def _proj_rope_kernel(x_ref, nw_ref, sc_ref, sh_ref, w_ref, cos_ref, sin_ref, o_ref, *, scales):
    h = _norm_mod(x_ref[...], nw_ref[...], sc_ref[...], sh_ref[...]).astype(BF16)
    y = _dot(h, w_ref[...])
    cos = cos_ref[...]
    sin = sin_ref[...]
    lane = lax.broadcasted_iota(jnp.int32, cos.shape, 1)
    first_half = (lane % HEAD_DIM) < (HEAD_DIM // 2)
    for g, scale in enumerate(scales):
        yg = y[:, g * LANES:(g + 1) * LANES]
        rot = jnp.where(first_half, pltpu.roll(yg, LANES - HEAD_DIM // 2, 1),
                        pltpu.roll(yg, HEAD_DIM // 2, 1))
        r = yg * cos + rot * sin
        if scale != 1.0:
            r = r * scale
        o_ref[:, g * LANES:(g + 1) * LANES] = r.astype(o_ref.dtype)


def _mod_specs(T, tm, sc_idx, sh_idx, nargs):
    per_b = T // tm
    if nargs == 1:
        return [pl.BlockSpec((None, None, 1, D_MODEL), lambda i: (i // per_b, sc_idx, 0, 0)),
                pl.BlockSpec((None, None, 1, D_MODEL), lambda i: (i // per_b, sh_idx, 0, 0))]
    return [pl.BlockSpec((None, None, 1, D_MODEL), lambda i, j: (i // per_b, sc_idx, 0, 0)),
            pl.BlockSpec((None, None, 1, D_MODEL), lambda i, j: (i // per_b, sh_idx, 0, 0))]


def proj_plain(x, mod_l, nw, w, T, *, tm=1024, tn=512):
    M, D = x.shape
    N = w.shape[1]
    return pl.pallas_call(
        _proj_plain_kernel,
        grid=(M // tm, N // tn),
        in_specs=[pl.BlockSpec((tm, D), lambda i, j: (i, 0)),
                  pl.BlockSpec((1, D), lambda i, j: (0, 0))]
        + _mod_specs(T, tm, 1, 0, 2)
        + [pl.BlockSpec((D, tn), lambda i, j: (0, j))],
        out_specs=pl.BlockSpec((tm, tn), lambda i, j: (i, j)),
        out_shape=jax.ShapeDtypeStruct((M, N), BF16),
        scratch_shapes=[pltpu.VMEM((tm, D), BF16)],
        compiler_params=_cparams("parallel", "arbitrary"),
        name="proj_plain",
    )(x, nw, mod_l, mod_l, w)


def proj_rope(x, mod_l, nw, w, cos, sin, scales, T, *, tm=512):
    M, D = x.shape
    N = w.shape[1]
    per_b = T // tm
    return pl.pallas_call(
        functools.partial(_proj_rope_kernel, scales=scales),
        grid=(M // tm,),
        in_specs=[pl.BlockSpec((tm, D), lambda i: (i, 0)),
                  pl.BlockSpec((1, D), lambda i: (0, 0))]
        + _mod_specs(T, tm, 1, 0, 1)
        + [pl.BlockSpec((D, N), lambda i: (0, 0)),
           pl.BlockSpec((tm, LANES), lambda i: (i % per_b, 0)),
           pl.BlockSpec((tm, LANES), lambda i: (i % per_b, 0))],
        out_specs=pl.BlockSpec((tm, N), lambda i: (i, 0)),
        out_shape=jax.ShapeDtypeStruct((M, N), BF16),
        compiler_params=_cparams("parallel"),
        name="proj_rope",
    )(x, nw, mod_l, mod_l, w, cos, sin)


def rope_tables(T):
    d = HEAD_DIM
    pos = jnp.arange(T, dtype=F32)
    inv = ROPE_THETA ** (-jnp.arange(0, d, 2, dtype=F32) / d)
    ang = pos[:, None] * inv[None, :]
    cos = jnp.cos(ang)
    sin = jnp.sin(ang)
    cos_t = jnp.concatenate([cos, cos, cos, cos], axis=-1)
    sin_t = jnp.concatenate([-sin, sin, -sin, sin], axis=-1)
    return cos_t, sin_t


def split_w_in(w_in):
    sizes = [512, 128, 128, 128, 128, 128, 128, 24, 256, 256, 512, 512, 512, 512, 512, 8, 3072]
    offs = np.cumsum([0] + sizes)
    (nq, nkc, nvc, nks, nvs, nkw, nvw, ngate, rq, rk, rv, rg, fq, fk, fv, ff, mg) = [
        w_in[:, offs[i]:offs[i + 1]] for i in range(len(sizes))]
    D = w_in.shape[0]
    fq_s = fq * (HEAD_DIM ** -0.5 * LOG2E)
    small = jnp.concatenate([ngate, ff, jnp.zeros((D, LANES - 32), w_in.dtype)], axis=-1)
    w1 = jnp.concatenate([nq, rq, rk, nkc, nks, nkw], axis=-1).astype(BF16)
    w2 = jnp.concatenate([mg, rv, rg, fq_s, fk, fv, nvc, nvs, nvw, small], axis=-1).astype(BF16)
    assert w1.shape[1] == P1_COLS and w2.shape[1] == P2_COLS
    return w1, w2


def p1_scales():
    s = [1.0] * (P1_COLS // LANES)
    for g in range(P1_NQ // LANES, P1_RQ // LANES):
        s[g] = HEAD_DIM ** -0.5 * LOG2E
    for g in range(P1_RK // LANES, P1_NKC // LANES):
        s[g] = RET_QK_DIM ** -0.5
    return tuple(s)


def _compress_kernel(x_ref, pe_ref, w1_ref, w2_ref, o_ref):
    r = x_ref[...]
    half = r.shape[1]
    w1 = w1_ref[...]
    a = _dot(r, w1[:half])
    b = _dot(r, w1[half:])
    pe = _dot(pe_ref[...], w1)[0:1]
    n = a.shape[0]
    hid = a + pltpu.roll(b, n - 1, 0) + pe
    hid = hid * _sigmoid(hid)
    o_ref[...] = _dot(hid.astype(BF16), w2_ref[...]).astype(o_ref.dtype)


def compress(xr, pe, w1, w2):
    _, B, G, R, W = xr.shape
    H = w1.shape[-1]
    return pl.pallas_call(
        _compress_kernel,
        grid=(2, B, G),
        in_specs=[pl.BlockSpec((None, None, None, R, W), lambda s, b, g: (s, b, g, 0, 0)),
                  pl.BlockSpec((None, 8, 2 * W), lambda s, b, g: (s, 0, 0)),
                  pl.BlockSpec((None, 2 * W, H), lambda s, b, g: (s, 0, 0)),
                  pl.BlockSpec((None, H, HEAD_DIM), lambda s, b, g: (s, 0, 0))],
        out_specs=pl.BlockSpec((None, None, None, R, HEAD_DIM), lambda s, b, g: (s, b, g, 0, 0)),
        out_shape=jax.ShapeDtypeStruct((2, B, G, R, HEAD_DIM), BF16),
        compiler_params=_cparams("parallel", "parallel", "parallel"),
        name="nsa_compress",
    )(xr, pe, w1, w2)


def _stack_heads(q_ref, g):
    tq = q_ref.shape[0]
    half = lax.broadcasted_iota(jnp.int32, (tq, LANES), 1) // HEAD_DIM
    rows = []
    for hh in range(NSA_HPG):
        h = NSA_HPG * g + hh
        x = q_ref[:, (h // 2) * LANES:(h // 2 + 1) * LANES].astype(F32)
        if h % 2 != g:
            x = pltpu.roll(x, HEAD_DIM, 1)
        rows.append(jnp.where(half == g, x, 0.0).astype(BF16))
    return jnp.concatenate(rows, axis=0)


def _store_heads(o_ref, g, o, tq):
    for hh in range(NSA_HPG):
        h = NSA_HPG * g + hh
        o_ref[:, h * LANES:(h + 1) * LANES] = o[hh * tq:(hh + 1) * tq].astype(o_ref.dtype)


def _nsa_cmp_kernel(q_ref, kc_ref, vc_ref, ov_ref, o_ref, m_ref, *, tq, n_sel, top_n):
    t0 = pl.program_id(1) * tq
    kc = kc_ref[...]
    vc = vc_ref[...]
    ncp = kc.shape[0]
    nsp = ov_ref.shape[0]
    rows = NSA_HPG * tq
    n_idx = lax.broadcasted_iota(jnp.int32, (rows, ncp), 1)
    t_idx = t0 + lax.broadcasted_iota(jnp.int32, (rows, ncp), 0) % tq
    valid = (n_idx * CMP_STRIDE + (CMP_LEN - 1)) <= t_idx
    j_idx = lax.broadcasted_iota(jnp.int32, (nsp, tq), 0)
    cur = (t0 + lax.broadcasted_iota(jnp.int32, (nsp, tq), 1)) // SEL_LEN
    forced = (j_idx == 0) | (j_idx == cur) | (j_idx == cur - 1)
    j_f = j_idx.astype(F32)
    for g in range(NSA_GROUPS):
        q = _stack_heads(q_ref, g)
        s = jnp.where(valid, _dot_nt(q, kc), NEG_INF)
        m = jnp.max(s, axis=-1, keepdims=True)
        e = jnp.exp2(s - m)
        l = jnp.sum(e, axis=-1, keepdims=True)
        p = e * jnp.where(m > 0.5 * NEG_INF, 1.0 / l, 0.0)
        _store_heads(o_ref, g, _dot(p.astype(BF16), vc), tq)
        psum = p[0:tq]
        for hh in range(1, NSA_HPG):
            psum = psum + p[hh * tq:(hh + 1) * tq]
        imp_t = _dot_nt(ov_ref[...], psum.astype(BF16))
        score = jnp.where(j_idx <= cur, imp_t, NEG_INF)
        score = jnp.where(forced | (j_idx >= n_sel), REMOVED, score)
        sel = jnp.where(forced, 1.0, 0.0)
        for _ in range(top_n - 3):
            mx = jnp.max(score, axis=0, keepdims=True)
            idx = jnp.min(jnp.where(score == mx, j_f, float(nsp)), axis=0, keepdims=True)
            hit = j_f == idx
            sel = jnp.where(hit, 1.0, sel)
            score = jnp.where(hit, REMOVED, score)
        sel = jnp.where(j_idx <= cur, sel, 0.0)
        m_ref[g] = sel.T.astype(m_ref.dtype)


def nsa_cmp_select(p1, kc, vc, ov_t, T):
    B = p1.shape[0]
    tq = NSA_QBLOCK
    ncp = kc.shape[1]
    nsp = ov_t.shape[0]
    n_sel = T // SEL_LEN
    return pl.pallas_call(
        functools.partial(_nsa_cmp_kernel, tq=tq, n_sel=n_sel, top_n=min(SEL_TOPN, n_sel)),
        grid=(B, T // tq),
        in_specs=[pl.BlockSpec((None, tq, NSA_HEADS * HEAD_DIM), lambda b, i: (b, i, 0)),
                  pl.BlockSpec((None, ncp, LANES), lambda b, i: (b, 0, 0)),
                  pl.BlockSpec((None, ncp, LANES), lambda b, i: (b, 0, 0)),
                  pl.BlockSpec((nsp, ncp), lambda b, i: (0, 0))],
        out_specs=[pl.BlockSpec((None, tq, NSA_OUT), lambda b, i: (b, i, 0)),
                   pl.BlockSpec((None, NSA_GROUPS, tq, nsp), lambda b, i: (b, 0, i, 0))],
        out_shape=[jax.ShapeDtypeStruct((B, T, NSA_OUT), BF16),
                   jax.ShapeDtypeStruct((B, NSA_GROUPS, T, nsp), BF16)],
        compiler_params=_cparams("parallel", "parallel"),
        name="nsa_cmp_select",
    )(p1, kc, vc, ov_t)


SEL_BONUS = 8192.0
NSA_SEL_TQ = 256
NSA_SEL_TK = 1024


def _nsa_sel_kernel(q_ref, k_ref, v_ref, m_ref, et_ref, o_ref, *, tq, tk):
    t0 = pl.program_id(1) * tq
    n_tiles = (t0 + tq + tk - 1) // tk
    rows = NSA_HPG * tq
    for g in range(NSA_GROUPS):
        q = jnp.concatenate([_stack_heads(q_ref, g), jnp.concatenate([m_ref[g]] * NSA_HPG, axis=0)], axis=1)
        den = HEAD_DIM * (1 - g)

        def step(j, carry, masked, q=q, g=g):
            m, acc = carry
            start = pl.multiple_of(j * tk, tk)
            ks = jnp.concatenate([k_ref[pl.ds(start, tk), :], et_ref[pl.ds(start, tk), :]], axis=1)
            s = _dot_nt(q, ks)
            if masked:
                trow = t0 + lax.broadcasted_iota(jnp.int32, (rows, tk), 0) % tq
                kpos = start + lax.broadcasted_iota(jnp.int32, (rows, tk), 1)
                s = jnp.where(kpos <= trow, s, NEG_INF)
            m_new = jnp.maximum(m, jnp.max(s, axis=-1, keepdims=True))
            p = jnp.exp2(s - m_new)
            acc = jnp.exp2(m - m_new) * acc + _dot(p.astype(BF16), v_ref[g, pl.ds(start, tk), :])
            return m_new, acc

        init = (jnp.full((rows, 1), NEG_INF, F32), jnp.zeros((rows, LANES), F32))
        carry = lax.fori_loop(0, n_tiles - 1, functools.partial(step, masked=False), init)
        _, acc = step(n_tiles - 1, carry, True)
        _store_heads(o_ref, g, acc / acc[:, den:den + 1], tq)


def nsa_value_augment(v):
    ones = jnp.ones_like(v[..., :HEAD_DIM])
    return jnp.stack([jnp.concatenate([v[..., :HEAD_DIM], ones], axis=-1),
                      jnp.concatenate([ones, v[..., HEAD_DIM:]], axis=-1)], axis=1)


def nsa_selected(p1, v_aug, sel, et_mat, T, *, tq=NSA_SEL_TQ, tk=NSA_SEL_TK):
    B = p1.shape[0]
    nsp = sel.shape[-1]
    return pl.pallas_call(
        functools.partial(_nsa_sel_kernel, tq=tq, tk=tk),
        grid=(B, T // tq),
        in_specs=[pl.BlockSpec((None, tq, NSA_HEADS * HEAD_DIM), lambda b, i: (b, i, 0)),
                  pl.BlockSpec((None, T, LANES), lambda b, i: (b, 0, P1_NKS // LANES)),
                  pl.BlockSpec((None, NSA_GROUPS, T, LANES), lambda b, i: (b, 0, 0, 0)),
                  pl.BlockSpec((None, NSA_GROUPS, tq, nsp), lambda b, i: (b, 0, i, 0)),
                  pl.BlockSpec((T, nsp), lambda b, i: (0, 0))],
        out_specs=pl.BlockSpec((None, tq, NSA_OUT), lambda b, i: (b, i, 0)),
        out_shape=jax.ShapeDtypeStruct((B, T, NSA_OUT), BF16),
        compiler_params=_cparams("parallel", "parallel"),
        name="nsa_selected",
    )(p1, p1, v_aug, sel, et_mat)


def _nsa_win_kernel(q_ref, k_ref, v_ref, b_ref, o_ref, *, tq):
    t0 = pl.program_id(1) * tq
    span = WINDOW + tq
    start = pl.multiple_of(jnp.maximum(t0 - WINDOW, 0), tq)
    ks = k_ref[pl.ds(start, span), :]
    vs = v_ref[pl.ds(start, span), :]

    def run(bias):
        bias = jnp.concatenate([bias] * NSA_HPG, axis=0)
        for g in range(NSA_GROUPS):
            s = _dot_nt(_stack_heads(q_ref, g), ks) + bias
            m = jnp.max(s, axis=-1, keepdims=True)
            p = jnp.exp2(s - m)
            l = jnp.sum(p, axis=-1, keepdims=True)
            _store_heads(o_ref, g, _dot(p.astype(BF16), vs) / l, tq)

    @pl.when(t0 >= WINDOW)
    def _():
        run(b_ref[...])

    @pl.when(t0 < WINDOW)
    def _():
        row = lax.broadcasted_iota(jnp.int32, (tq, span), 0)
        col = lax.broadcasted_iota(jnp.int32, (tq, span), 1)
        run(jnp.where(col <= t0 + row, 0.0, NEG_INF))


def nsa_window(p1, p2, T):
    B = p1.shape[0]
    tq = NSA_QBLOCK
    span = WINDOW + tq
    r = np.arange(tq)[:, None]
    c = np.arange(span)[None, :]
    band = jnp.asarray(np.where((c > r) & (c <= r + WINDOW), 0.0, NEG_INF), F32)
    return pl.pallas_call(
        functools.partial(_nsa_win_kernel, tq=tq),
        grid=(B, T // tq),
        in_specs=[pl.BlockSpec((None, tq, NSA_HEADS * HEAD_DIM), lambda b, i: (b, i, 0)),
                  pl.BlockSpec((None, T, LANES), lambda b, i: (b, 0, P1_NKW // LANES)),
                  pl.BlockSpec((None, T, LANES), lambda b, i: (b, 0, P2_NVW // LANES)),
                  pl.BlockSpec((tq, span), lambda b, i: (0, 0))],
        out_specs=pl.BlockSpec((None, tq, NSA_OUT), lambda b, i: (b, i, 0)),
        out_shape=jax.ShapeDtypeStruct((B, T, NSA_OUT), BF16),
        compiler_params=_cparams("parallel", "parallel"),
        name="nsa_window",
    )(p1, p1, p2, band)


def _retention_kernel(q_ref, k_ref, v_ref, g_ref, din_ref, qd_ref, kd_ref, cd_ref, o_ref, st_ref):
    @pl.when(pl.program_id(0) == 0)
    def _():
        st_ref[...] = jnp.zeros_like(st_ref)

    B = q_ref.shape[0]
    half = lax.broadcasted_iota(jnp.int32, (q_ref.shape[1], LANES), 1) // HEAD_DIM
    for b in range(B):
        for h in range(RET_HEADS):
            lanes = slice(h * LANES, (h + 1) * LANES)
            pair = slice((h // 2) * LANES, (h // 2 + 1) * LANES)
            qh = jnp.where(half == h % 2, q_ref[b, :, pair], 0.0).astype(BF16)
            kp = k_ref[b, :, pair]
            vh = v_ref[b, :, lanes]
            st = st_ref[b, h]
            inner = _dot_nt(qh, kp) * din_ref[h]
            o = _dot(inner.astype(BF16), vh) + _dot(qh, st.astype(BF16)) * qd_ref[h]
            kd = (kp.astype(F32) * kd_ref[h]).astype(BF16)
            st_ref[b, h] = st * cd_ref[h, 0:1, :] + _dot_tn(kd, vh)
            mu = jnp.mean(o, axis=-1, keepdims=True)
            d = o - mu
            var = jnp.mean(d * d, axis=-1, keepdims=True)
            on = d * lax.rsqrt(var + NORM_EPS)
            gh = g_ref[b, :, lanes].astype(F32)
            o_ref[b, :, lanes] = (gh * _sigmoid(gh) * on).astype(o_ref.dtype)


def retention_consts():
    C = RET_CHUNK
    H = RET_HEADS
    log_g = jnp.log(1.0 - 2.0 ** (-5.0 - jnp.arange(H, dtype=F32)))
    n = jnp.arange(C, dtype=F32)
    diff = n[:, None] - n[None, :]
    causal = diff >= 0
    decay_in = jnp.where(causal[None], jnp.exp(jnp.where(causal, diff, 0.0)[None] * log_g[:, None, None]), 0.0)
    q_decay = jnp.exp((n[None, :] + 1.0) * log_g[:, None])
    k_decay = jnp.exp((C - 1.0 - n)[None, :] * log_g[:, None])
    chunk_decay = jnp.exp(C * log_g)
    qd = jnp.broadcast_to(q_decay[:, :, None], (H, C, LANES))
    kd = jnp.broadcast_to(k_decay[:, :, None], (H, C, LANES))
    cd = jnp.broadcast_to(chunk_decay[:, None, None], (H, 8, LANES))
    return decay_in, qd, kd, cd


def retention(p1, p2, consts, T):
    B = p1.shape[0]
    C = RET_CHUNK
    din, qd, kd, cd = consts
    W = RET_HEADS * LANES
    full = lambda shape: pl.BlockSpec(shape, lambda c: (0,) * len(shape))
    return pl.pallas_call(
        _retention_kernel,
        grid=(T // C,),
        in_specs=[pl.BlockSpec((B, C, W // 2), lambda c: (0, c, P1_RQ // (W // 2))),
                  pl.BlockSpec((B, C, W // 2), lambda c: (0, c, P1_RK // (W // 2))),
                  pl.BlockSpec((B, C, W), lambda c: (0, c, P2_RV // W)),
                  pl.BlockSpec((B, C, W), lambda c: (0, c, P2_RG // W)),
                  full(din.shape), full(qd.shape), full(kd.shape), full(cd.shape)],
        out_specs=pl.BlockSpec((B, C, W), lambda c: (0, c, 0)),
        out_shape=jax.ShapeDtypeStruct((B, T, W), BF16),
        scratch_shapes=[pltpu.VMEM((B, RET_HEADS, LANES, LANES), F32)],
        compiler_params=_cparams("arbitrary"),
        name="retention",
    )(p1, p1, p2, p2, din, qd, kd, cd)


def _fox_cum_kernel(f_ref, b_ref, o_ref):
    x = f_ref[...] + b_ref[...]
    ls = jnp.minimum(x, 0.0) - jnp.log1p(jnp.exp(-jnp.abs(x)))
    R = x.shape[0]
    ki = lax.broadcasted_iota(jnp.int32, (LANES, LANES), 0)
    ji = lax.broadcasted_iota(jnp.int32, (LANES, LANES), 1)
    upper = jnp.where(ki <= ji, 1.0, 0.0).astype(BF16)
    hi, mid, lo = _split3(ls)
    rowcum = _dot(hi, upper) + _dot(mid, upper) + _dot(lo, upper)
    tot = jnp.broadcast_to(rowcum[:, LANES - 1:LANES], (R, LANES))
    ri = lax.broadcasted_iota(jnp.int32, (R, R), 0)
    ci = lax.broadcasted_iota(jnp.int32, (R, R), 1)
    lower = jnp.where(ci < ri, 1.0, 0.0).astype(BF16)
    hi, mid, lo = _split3(tot)
    offs = _dot(lower, hi) + _dot(lower, mid) + _dot(lower, lo)
    o_ref[...] = (rowcum + offs) * LOG2E


def fox_cum(f_logit, bias):
    B, H, R, _ = f_logit.shape
    return pl.pallas_call(
        _fox_cum_kernel,
        grid=(B, H),
        in_specs=[pl.BlockSpec((None, None, R, LANES), lambda b, h: (b, h, 0, 0)),
                  pl.BlockSpec((None, 1, LANES), lambda b, h: (h, 0, 0))],
        out_specs=pl.BlockSpec((None, None, R, LANES), lambda b, h: (b, h, 0, 0)),
        out_shape=jax.ShapeDtypeStruct((B, H, R, LANES), F32),
        compiler_params=_cparams("parallel", "parallel"),
        name="fox_cum",
    )(f_logit, bias)


FOX_BIAS_LANES = 3


def _fox_kernel(q_ref, k_ref, v_ref, c_ref, o_ref, ka_ref, va_ref, *, tq):
    i = pl.program_id(2)
    tk = tq
    T = k_ref.shape[0]
    chunk = 512

    @pl.when(i == 0)
    def _():
        lane = lax.broadcasted_iota(jnp.int32, (chunk, LANES), 1)
        ri = lax.broadcasted_iota(jnp.int32, (16, LANES), 0)
        ci = lax.broadcasted_iota(jnp.int32, (16, LANES), 1)
        place = jnp.where((ci == ri + HEAD_DIM) & (ri < FOX_BIAS_LANES), 1.0, 0.0).astype(BF16)

        def build(c, _):
            c0 = pl.multiple_of(c * chunk, chunk)
            kp = k_ref[pl.ds(c0, chunk), :].astype(F32)
            vp = v_ref[pl.ds(c0, chunk), :].astype(F32)
            for hh in range(2):
                hi, mid, lo = _split3(-c_ref[hh, :, pl.ds(c0, chunk)])
                terms = jnp.concatenate([hi, mid, lo, jnp.zeros((13, chunk), BF16)], axis=0)
                bias = _dot_tn(terms, place)
                kh = kp if hh == 0 else pltpu.roll(kp, HEAD_DIM, 1)
                vh = vp if hh == 0 else pltpu.roll(vp, HEAD_DIM, 1)
                ka_ref[hh, pl.ds(c0, chunk), :] = jnp.where(lane < HEAD_DIM, kh, bias).astype(BF16)
                va_ref[hh, pl.ds(c0, chunk), :] = jnp.where(lane < HEAD_DIM, vh, 1.0).astype(BF16)
            return 0

        lax.fori_loop(0, T // chunk, build, 0)

    row = lax.broadcasted_iota(jnp.int32, (tq, tk), 0)
    col = lax.broadcasted_iota(jnp.int32, (tq, tk), 1)
    lane = lax.broadcasted_iota(jnp.int32, (tq, LANES), 1)
    ones_lanes = (lane >= HEAD_DIM) & (lane < HEAD_DIM + FOX_BIAS_LANES)
    qp = q_ref[...].astype(F32)
    qs = [jnp.where(lane < HEAD_DIM, qh, jnp.where(ones_lanes, 1.0, 0.0)).astype(BF16)
          for qh in (qp, pltpu.roll(qp, HEAD_DIM, 1))]

    def step(j, carry, masked):
        start = pl.multiple_of(j * tk, tk)
        out = []
        for hh in range(2):
            m, acc = carry[hh]
            s = _dot_nt(qs[hh], ka_ref[hh, pl.ds(start, tk), :])
            if masked:
                s = jnp.where(col <= row, s, NEG_INF)
            m_new = jnp.maximum(m, jnp.max(s, axis=-1, keepdims=True))
            p = jnp.exp2(s - m_new)
            acc = jnp.exp2(m - m_new) * acc + _dot(p.astype(BF16), va_ref[hh, pl.ds(start, tk), :])
            out.append((m_new, acc))
        return tuple(out)

    one = (jnp.full((tq, 1), NEG_INF, F32), jnp.zeros((tq, LANES), F32))
    carry = lax.fori_loop(0, i, functools.partial(step, masked=False), (one, one))
    (_, acc0), (_, acc1) = step(i, carry, True)
    o0 = acc0 / acc0[:, HEAD_DIM:HEAD_DIM + 1]
    o1 = acc1 / acc1[:, HEAD_DIM:HEAD_DIM + 1]
    o_ref[...] = jnp.where(lane < HEAD_DIM, o0, pltpu.roll(o1, HEAD_DIM, 1)).astype(o_ref.dtype)


def fox_attention(p2, cum, T, *, tq=FOX_TQ):
    B = p2.shape[0]
    HP = FOX_HEADS // 2
    return pl.pallas_call(
        functools.partial(_fox_kernel, tq=tq),
        grid=(B, HP, T // tq),
        in_specs=[pl.BlockSpec((None, tq, LANES), lambda b, h, i: (b, i, P2_FQ // LANES + h)),
                  pl.BlockSpec((None, T, LANES), lambda b, h, i: (b, 0, P2_FK // LANES + h)),
                  pl.BlockSpec((None, T, LANES), lambda b, h, i: (b, 0, P2_FV // LANES + h)),
                  pl.BlockSpec((None, None, 2, 1, T), lambda b, h, i: (b, h, 0, 0, 0))],
        out_specs=pl.BlockSpec((None, tq, LANES), lambda b, h, i: (b, i, h)),
        out_shape=jax.ShapeDtypeStruct((B, T, FOX_HEADS * HEAD_DIM), BF16),
        scratch_shapes=[pltpu.VMEM((2, T, LANES), BF16), pltpu.VMEM((2, T, LANES), BF16)],
        compiler_params=_cparams("parallel", "parallel", "arbitrary"),
        name="fox_attention",
    )(p2, p2, p2, cum)


def _readout_kernel(ocmp_ref, osel_ref, owin_ref, small_ref, oret_ref, ofox_ref, mg_ref, x_ref, g1_ref,
                    ex_ref, wn_ref, wr_ref, wf_ref, wo_ref, o_ref):
    W = NSA_OUT
    gs = _sigmoid(small_ref[...].astype(F32)).astype(BF16)
    ge = _dot(gs, ex_ref[...])
    onsa = (ge[:, :W] * ocmp_ref[...].astype(F32) + ge[:, W:2 * W] * osel_ref[...].astype(F32)
            + ge[:, 2 * W:] * owin_ref[...].astype(F32))
    D = D_MODEL
    merged = (_sigmoid(mg_ref[:, :D].astype(F32)) * _dot(onsa.astype(BF16), wn_ref[...])
              + _sigmoid(mg_ref[:, D:2 * D].astype(F32)) * _dot(oret_ref[...], wr_ref[...])
              + _sigmoid(mg_ref[:, 2 * D:].astype(F32)) * _dot(ofox_ref[...], wf_ref[...]))
    y = _dot(merged.astype(BF16), wo_ref[...])
    o_ref[...] = x_ref[...] + g1_ref[...] * y


def readout(o_cmp, o_sel, o_win, p2, o_ret, o_fox, x, mod_l, ex, wn, wr, wf, wo, T, *, tm=512):
    M, D = x.shape
    per_b = T // tm
    W = NSA_OUT
    row = lambda width, col=0: pl.BlockSpec((tm, width), lambda i: (i, col))
    full = lambda a: pl.BlockSpec(a.shape, lambda i: (0,) * a.ndim)
    return pl.pallas_call(
        _readout_kernel,
        grid=(M // tm,),
        in_specs=[row(W), row(W), row(W), row(LANES, P2_SMALL // LANES), row(512), row(512),
                  row(3 * D, 0), row(D),
                  pl.BlockSpec((None, None, 1, D), lambda i: (i // per_b, 2, 0, 0)),
                  full(ex), full(wn), full(wr), full(wf), full(wo)],
        out_specs=row(D),
        out_shape=jax.ShapeDtypeStruct((M, D), F32),
        compiler_params=_cparams("parallel"),
        name="mixer_readout",
    )(o_cmp, o_sel, o_win, p2, o_ret, o_fox, p2, x, mod_l, ex, wn, wr, wf, wo)


def nsa_gate_expand():
    ex = np.zeros((LANES, 3 * NSA_OUT), np.float32)
    for br in range(3):
        for h in range(NSA_HEADS):
            c0 = br * NSA_OUT + h * LANES
            ex[br * NSA_HEADS + h, c0:c0 + LANES] = 1.0
    return jnp.asarray(ex, BF16)


def pad_read_nsa(w):
    D = w.shape[1]
    w = w.reshape(NSA_HEADS, HEAD_DIM, D)
    z = jnp.zeros_like(w)
    g = (np.arange(NSA_HEADS) // NSA_HPG)[:, None, None]
    lo = jnp.where(g == 0, w, z)
    hi = jnp.where(g == 1, w, z)
    return jnp.concatenate([lo, hi], axis=1).reshape(NSA_OUT, D).astype(BF16)


def _ffn_kernel(*refs, gated):
    if gated:
        x_ref, nw_ref, sc_ref, sh_ref, g2_ref, gate_ref, w1_ref, w3_ref, w2_ref, o_ref, h_ref, acc_ref = refs
    else:
        x_ref, nw_ref, sc_ref, sh_ref, g2_ref, w1_ref, w3_ref, w2_ref, o_ref, h_ref, acc_ref = refs
    e = pl.program_id(1)
    f = pl.program_id(2)

    @pl.when((e == 0) & (f == 0))
    def _():
        h_ref[...] = _norm_mod(x_ref[...], nw_ref[...], sc_ref[...], sh_ref[...]).astype(BF16)
        acc_ref[...] = jnp.zeros_like(acc_ref)

    h = h_ref[...]
    u = _dot(h, w1_ref[...])
    v = _dot(h, w3_ref[...])
    a = (u * _sigmoid(u) * v).astype(BF16)
    y = _dot(a, w2_ref[...])
    if gated:
        gate = gate_ref[...]
        lane = lax.broadcasted_iota(jnp.int32, gate.shape, 1)
        y = y * jnp.sum(jnp.where(lane == e, gate, 0.0), axis=-1, keepdims=True)
    acc_ref[...] += y

    @pl.when((e == pl.num_programs(1) - 1) & (f == pl.num_programs(2) - 1))
    def _():
        o_ref[...] = x_ref[...] + g2_ref[...] * acc_ref[...]


def ffn(x, mod_l, nw, w1, w3, w2, gate, T, *, tm, tf):
    M, D = x.shape
    E, _, F = w1.shape
    per_b = T // tm
    gated = gate is not None
    modspec = lambda k: pl.BlockSpec((None, None, 1, D), lambda i, e, f: (i // per_b, k, 0, 0))
    in_specs = [pl.BlockSpec((tm, D), lambda i, e, f: (i, 0)),
                pl.BlockSpec((1, D), lambda i, e, f: (0, 0)),
                modspec(4), modspec(3), modspec(5)]
    args = [x, nw, mod_l, mod_l, mod_l]
    if gated:
        in_specs.append(pl.BlockSpec((tm, LANES), lambda i, e, f: (i, 0)))
        args.append(gate)
    in_specs += [pl.BlockSpec((None, D, tf), lambda i, e, f: (e, 0, f)),
                 pl.BlockSpec((None, D, tf), lambda i, e, f: (e, 0, f)),
                 pl.BlockSpec((None, tf, D), lambda i, e, f: (e, f, 0))]
    args += [w1, w3, w2]
    return pl.pallas_call(
        functools.partial(_ffn_kernel, gated=gated),
        grid=(M // tm, E, F // tf),
        in_specs=in_specs,
        out_specs=pl.BlockSpec((tm, D), lambda i, e, f: (i, 0)),
        out_shape=jax.ShapeDtypeStruct((M, D), F32),
        scratch_shapes=[pltpu.VMEM((tm, D), BF16), pltpu.VMEM((tm, D), F32)],
        compiler_params=_cparams("parallel", "arbitrary", "arbitrary"),
        name="ffn_gated" if gated else "ffn_dense",
    )(*args)


MOE_TC = 512
MOE_TS = 512


def _router_kernel(x_ref, nw_ref, sc_ref, sh_ref, wh_ref, wl_ref, h_ref, gate_ref, rank_ref, cnt_ref, carry_ref):
    @pl.when(pl.program_id(0) == 0)
    def _():
        carry_ref[...] = jnp.zeros_like(carry_ref)

    h = _norm_mod(x_ref[...], nw_ref[...], sc_ref[...], sh_ref[...])
    hh = h.astype(BF16)
    h_ref[...] = hh.astype(h_ref.dtype)
    hl = (h - hh.astype(F32)).astype(BF16)
    logits = _dot(hh, wh_ref[...]) + (_dot(hl, wh_ref[...]) + _dot(hh, wl_ref[...]))
    tm = logits.shape[0]
    lane = lax.broadcasted_iota(jnp.int32, logits.shape, 1)
    logits = jnp.where(lane < N_EXPERTS, logits, REMOVED)
    lane_f = lane.astype(F32)
    v1 = jnp.max(logits, axis=-1, keepdims=True)
    i1 = jnp.min(jnp.where(logits == v1, lane_f, float(LANES)), axis=-1, keepdims=True)
    rest = jnp.where(lane_f == i1, REMOVED, logits)
    v2 = jnp.max(rest, axis=-1, keepdims=True)
    i2 = jnp.min(jnp.where(rest == v2, lane_f, float(LANES)), axis=-1, keepdims=True)
    e2 = jnp.exp(v2 - v1)
    w1 = 1.0 / (1.0 + e2)
    w2 = e2 / (1.0 + e2)
    gate_ref[...] = jnp.where(lane_f == i1, w1, jnp.where(lane_f == i2, w2, 0.0))

    sel = jnp.where((lane_f == i1) | (lane_f == i2), 1.0, 0.0)
    ri = lax.broadcasted_iota(jnp.int32, (tm, tm), 0)
    ci = lax.broadcasted_iota(jnp.int32, (tm, tm), 1)
    before = jnp.where(ci < ri, 1.0, 0.0).astype(BF16)
    rank = _dot(before, sel.astype(BF16)) + carry_ref[0:1, :]
    rank_ref[...] = jnp.where(sel > 0.0, rank, -1.0)
    carry_ref[...] = carry_ref[...] + jnp.sum(sel, axis=0, keepdims=True)
    cnt_ref[...] = carry_ref[...]


def router(x, mod_l, nw, w_router, T):
    M, D = x.shape
    tm = MOE_TC
    per_b = T // tm
    wp = jnp.zeros((D, LANES), F32).at[:, :N_EXPERTS].set(w_router)
    wh = wp.astype(BF16)
    wl = (wp - wh.astype(F32)).astype(BF16)
    return pl.pallas_call(
        _router_kernel,
        grid=(M // tm,),
        in_specs=[pl.BlockSpec((tm, D), lambda i: (i, 0)),
                  pl.BlockSpec((1, D), lambda i: (0, 0))]
        + _mod_specs(T, tm, 4, 3, 1)
        + [pl.BlockSpec((D, LANES), lambda i: (0, 0)),
           pl.BlockSpec((D, LANES), lambda i: (0, 0))],
        out_specs=[pl.BlockSpec((tm, D), lambda i: (i, 0)),
                   pl.BlockSpec((tm, LANES), lambda i: (i, 0)),
                   pl.BlockSpec((tm, LANES), lambda i: (i, 0)),
                   pl.BlockSpec((8, LANES), lambda i: (0, 0))],
        out_shape=[jax.ShapeDtypeStruct((M, D), F32),
                   jax.ShapeDtypeStruct((M, LANES), F32),
                   jax.ShapeDtypeStruct((M, LANES), F32),
                   jax.ShapeDtypeStruct((8, LANES), F32)],
        scratch_shapes=[pltpu.VMEM((8, LANES), F32)],
        compiler_params=_cparams("arbitrary"),
        name="moe_router",
    )(x, nw, mod_l, mod_l, wh, wl)


def _count_le(sorted_vals, x):
    return jnp.sum(sorted_vals[None, :] <= x[:, None], axis=1, dtype=jnp.int32)


def _moe_up_kernel(e_r, total, x_ref, w1_ref, w3_ref, o_ref, w1b_ref, w3b_ref):
    r = pl.program_id(1)
    live = r < total[0]

    @pl.when(live & ((r == 0) | (e_r[r] != e_r[jnp.maximum(r - 1, 0)])))
    def _():
        w1b_ref[...] = w1_ref[...].astype(BF16)
        w3b_ref[...] = w3_ref[...].astype(BF16)

    @pl.when(live)
    def _():
        x = x_ref[...].astype(BF16)
        u = _dot(x, w1b_ref[...])
        v = _dot(x, w3b_ref[...])
        o_ref[...] = (u * _sigmoid(u) * v).astype(o_ref.dtype)


def moe_up(xs, w1, w3, tiles, rt, *, tf=1792):
    R, D = xs.shape
    ts = MOE_TS
    F = w1.shape[-1]
    live = lambda r, total: jnp.minimum(r, total[0] - 1)
    return pl.pallas_call(
        _moe_up_kernel,
        grid_spec=pltpu.PrefetchScalarGridSpec(
            num_scalar_prefetch=2,
            grid=(F // tf, rt),
            in_specs=[pl.BlockSpec((ts, D), lambda n, r, e, total: (live(r, total), 0)),
                      pl.BlockSpec((None, D, tf), lambda n, r, e, total: (e[live(r, total)], 0, n)),
                      pl.BlockSpec((None, D, tf), lambda n, r, e, total: (e[live(r, total)], 0, n))],
            out_specs=pl.BlockSpec((ts, tf), lambda n, r, e, total: (r, n)),
            scratch_shapes=[pltpu.VMEM((D, tf), BF16), pltpu.VMEM((D, tf), BF16)],
        ),
        out_shape=jax.ShapeDtypeStruct((R, F), BF16),
        compiler_params=_cparams("arbitrary", "arbitrary"),
        name="moe_up",
    )(tiles["e"], tiles["total"], xs, w1, w3)


def _moe_down_kernel(e_r, total, a_ref, w2_ref, o_ref, w2b_ref):
    r = pl.program_id(0)
    live = r < total[0]

    @pl.when(live & ((r == 0) | (e_r[r] != e_r[jnp.maximum(r - 1, 0)])))
    def _():
        w2b_ref[...] = w2_ref[...].astype(BF16)

    @pl.when(live)
    def _():
        o_ref[...] = _dot(a_ref[...], w2b_ref[...]).astype(o_ref.dtype)


def moe_down(a, w2, tiles, rt):
    R, F = a.shape
    ts = MOE_TS
    D = w2.shape[-1]
    live = lambda r, total: jnp.minimum(r, total[0] - 1)
    return pl.pallas_call(
        _moe_down_kernel,
        grid_spec=pltpu.PrefetchScalarGridSpec(
            num_scalar_prefetch=2,
            grid=(rt,),
            in_specs=[pl.BlockSpec((ts, F), lambda r, e, total: (live(r, total), 0)),
                      pl.BlockSpec((None, F, D), lambda r, e, total: (e[live(r, total)], 0, 0))],
            out_specs=pl.BlockSpec((ts, D), lambda r, e, total: (r, 0)),
            scratch_shapes=[pltpu.VMEM((F, D), BF16)],
        ),
        out_shape=jax.ShapeDtypeStruct((R, D), F32),
        compiler_params=_cparams("arbitrary"),
        name="moe_down",
    )(tiles["e"], tiles["total"], a, w2)


SC_WINDOW = 32


def _sc_mesh():
    return plsc.VectorSubcoreMesh(core_axis_name="core", subcore_axis_name="subcore")


def sc_scatter_rows2(x, idx_a, idx_b, n_out):
    n, d = x.shape
    steps = n // SC_WINDOW

    @pl.kernel(out_type=jax.ShapeDtypeStruct((n_out, d), x.dtype), mesh=_sc_mesh(), scratch_types=[])
    def kern(x_hbm, ia_hbm, ib_hbm, o_hbm):
        def body(x_vmem, ia_vmem, ib_vmem):
            pltpu.sync_copy(x_vmem, o_hbm.at[ia_vmem.at[0]])
            pltpu.sync_copy(x_vmem, o_hbm.at[ib_vmem.at[0]])

        pltpu.emit_pipeline(
            body,
            grid=(steps,),
            in_specs=[pl.BlockSpec((SC_WINDOW, d), index_map=lambda i: (i, 0)),
                      pl.BlockSpec((1, SC_WINDOW), index_map=lambda i: (i, 0)),
                      pl.BlockSpec((1, SC_WINDOW), index_map=lambda i: (i, 0))],
            out_specs=[],
            core_axis_name=("core", "subcore"),
            dimension_semantics=(pltpu.PARALLEL,),
        )(x_hbm, ia_hbm, ib_hbm)

    return kern(x, idx_a.reshape(steps, SC_WINDOW), idx_b.reshape(steps, SC_WINDOW))


def sc_gather_rows(x, idx):
    n = idx.shape[0]
    d = x.shape[1]
    steps = n // SC_WINDOW

    @pl.kernel(out_type=jax.ShapeDtypeStruct((n, d), x.dtype), mesh=_sc_mesh(), scratch_types=[])
    def kern(x_hbm, i_hbm, o_hbm):
        def body(i_vmem, o_vmem):
            pltpu.sync_copy(x_hbm.at[i_vmem.at[0]], o_vmem)

        pltpu.emit_pipeline(
            body,
            grid=(steps,),
            in_specs=[pl.BlockSpec((1, SC_WINDOW), index_map=lambda i: (i, 0))],
            out_specs=[pl.BlockSpec((SC_WINDOW, d), index_map=lambda i: (i, 0))],
            core_axis_name=("core", "subcore"),
            dimension_semantics=(pltpu.PARALLEL,),
        )(i_hbm, o_hbm)

    return kern(x, idx.reshape(steps, SC_WINDOW))


def _moe_finish_kernel(x_ref, g2_ref, ya_ref, yb_ref, gate_ref, rank_ref, nw_ref, o_ref, *, normalize):
    gate = gate_ref[...]
    chosen = rank_ref[...] >= 0.0
    lane = lax.broadcasted_iota(jnp.int32, gate.shape, 1).astype(F32)
    first = jnp.min(jnp.where(chosen, lane, float(LANES)), axis=-1, keepdims=True)
    last = jnp.max(jnp.where(chosen, lane, -1.0), axis=-1, keepdims=True)
    wa = jnp.sum(jnp.where(lane == first, gate, 0.0), axis=-1, keepdims=True)
    wb = jnp.sum(jnp.where(lane == last, gate, 0.0), axis=-1, keepdims=True)
    x = x_ref[...] + g2_ref[...] * (wa * ya_ref[...] + wb * yb_ref[...])
    if normalize:
        ms = jnp.mean(x * x, axis=-1, keepdims=True)
        x = x * lax.rsqrt(ms + NORM_EPS) * nw_ref[...]
    o_ref[...] = x


def moe_finish(x, mod_l, y2, gate, rank, norm_w, T, *, tm=512):
    M, D = x.shape
    per_b = T // tm
    normalize = norm_w is not None
    if norm_w is None:
        norm_w = jnp.ones((1, D), F32)
    return pl.pallas_call(
        functools.partial(_moe_finish_kernel, normalize=normalize),
        grid=(M // tm,),
        in_specs=[pl.BlockSpec((tm, D), lambda i: (i, 0)),
                  pl.BlockSpec((None, None, 1, D), lambda i: (i // per_b, 5, 0, 0)),
                  pl.BlockSpec((None, tm, D), lambda i: (0, i, 0)),
                  pl.BlockSpec((None, tm, D), lambda i: (1, i, 0)),
                  pl.BlockSpec((tm, LANES), lambda i: (i, 0)),
                  pl.BlockSpec((tm, LANES), lambda i: (i, 0)),
                  pl.BlockSpec((1, D), lambda i: (0, 0))],
        out_specs=pl.BlockSpec((tm, D), lambda i: (i, 0)),
        out_shape=jax.ShapeDtypeStruct((M, D), F32),
        compiler_params=_cparams("parallel"),
        name="moe_finish",
    )(x, mod_l, y2, y2, gate, rank, norm_w)


def moe_ffn(x, mod_l, nw, w_router, w1, w3, w2, T, norm_w=None):
    M = x.shape[0]
    ts = MOE_TS
    rt = (2 * M) // ts + N_EXPERTS
    h, gate, rank, cnt = router(x, mod_l, nw, w_router, T)
    i32 = jnp.int32
    counts = cnt[0, :N_EXPERTS].astype(i32)
    ntile = (counts + ts - 1) // ts
    tile_end = jnp.cumsum(ntile)
    row_off = (tile_end - ntile) * ts
    e_r = jnp.minimum(_count_le(tile_end, jnp.arange(rt, dtype=i32)), N_EXPERTS - 1)
    tiles = dict(e=e_r, total=tile_end[-1].reshape(1).astype(i32))
    rk = rank[:, :N_EXPERTS].astype(i32)
    pos = row_off[None, :] + rk
    pos_a = jnp.min(jnp.where(rk >= 0, pos, rt * ts), axis=1)
    pos_b = jnp.max(jnp.where(rk >= 0, pos, -1), axis=1)

    xs = sc_scatter_rows2(h, pos_a, pos_b, rt * ts)
    a = moe_up(xs, w1, w3, tiles, rt)
    y = moe_down(a, w2, tiles, rt)
    y2 = sc_gather_rows(y, jnp.concatenate([pos_a, pos_b])).reshape(2, M, -1)
    return moe_finish(x, mod_l, y2, gate, rank, norm_w, T)


def _final_norm_kernel(x_ref, w_ref, o_ref):
    x = x_ref[...]
    ms = jnp.mean(x * x, axis=-1, keepdims=True)
    o_ref[...] = x * lax.rsqrt(ms + NORM_EPS) * w_ref[...]


def final_norm(x, w, *, tm=1024):
    M, D = x.shape
    return pl.pallas_call(
        _final_norm_kernel,
        grid=(M // tm,),
        in_specs=[pl.BlockSpec((tm, D), lambda i: (i, 0)), pl.BlockSpec((1, D), lambda i: (0, 0))],
        out_specs=pl.BlockSpec((tm, D), lambda i: (i, 0)),
        out_shape=jax.ShapeDtypeStruct((M, D), F32),
        compiler_params=_cparams("parallel"),
        name="final_norm",
    )(x, w)


def nsa_constants(T):
    n_sel = T // SEL_LEN
    nsp = max(LANES, n_sel)
    ncp = T // CMP_STRIDE
    cmp_start = np.arange(ncp) * CMP_STRIDE
    sel_start = np.arange(nsp) * SEL_LEN
    ov = ((cmp_start[:, None] < sel_start[None, :] + SEL_LEN)
          & (cmp_start[:, None] + CMP_LEN > sel_start[None, :]))
    ov[(T - CMP_LEN) // CMP_STRIDE + 1:] = False
    ov[:, n_sel:] = False
    et_mat = ((np.arange(T)[:, None] // SEL_LEN) == np.arange(nsp)[None, :]) * SEL_BONUS
    return jnp.asarray(ov.T, BF16), jnp.asarray(et_mat, BF16)


def token_mixing(x, mod_l, lw, consts, B, T):
    M = B * T
    cos_t, sin_t, ov_t, e_mat, ret_consts, ex = consts
    p1 = proj_rope(x, mod_l, lw["norm_mix"], lw["w1"], cos_t, sin_t, p1_scales(), T).reshape(B, T, P1_COLS)
    p2 = proj_plain(x, mod_l, lw["norm_mix"], lw["w2"], T).reshape(B, T, P2_COLS)

    def group_rows(a):
        return a.reshape(B, T, NSA_GROUPS, HEAD_DIM).transpose(0, 2, 1, 3).reshape(
            B, NSA_GROUPS, T // CMP_STRIDE, CMP_STRIDE * HEAD_DIM)

    xr = jnp.stack([group_rows(p1[:, :, P1_NKC:P1_NKC + LANES]), group_rows(p2[:, :, P2_NVC:P2_NVC + LANES])])
    cmp_out = compress(xr, lw["cmp_pe"], lw["cmp_w1"], lw["cmp_w2"])
    cmp_out = cmp_out.transpose(0, 1, 3, 2, 4).reshape(2, B, T // CMP_STRIDE, LANES)
    o_cmp, sel = nsa_cmp_select(p1, cmp_out[0], cmp_out[1], ov_t, T)
    o_sel = nsa_selected(p1, nsa_value_augment(p2[:, :, P2_NVS:P2_NVS + LANES]), sel, e_mat, T)
    o_win = nsa_window(p1, p2, T)

    o_ret = retention(p1, p2, ret_consts, T)

    ff = p2[:, :, P2_SMALL + 3 * NSA_HEADS:P2_SMALL + 3 * NSA_HEADS + FOX_HEADS].astype(F32)
    ff = ff.transpose(0, 2, 1).reshape(B, FOX_HEADS, T // LANES, LANES)
    cum = fox_cum(ff, lw["fox_bias"]).reshape(B, FOX_HEADS // 2, 2, 1, T)
    o_fox = fox_attention(p2, cum, T)

    return readout(o_cmp.reshape(M, -1), o_sel.reshape(M, -1), o_win.reshape(M, -1), p2.reshape(M, P2_COLS),
                   o_ret.reshape(M, -1), o_fox.reshape(M, -1), x, mod_l, ex,
                   lw["wn"], lw["wr"], lw["wf"], lw["wo"], T)


def layer_weights(l, norm_mix, w_in, cmp_k_pe, cmp_k_w1, cmp_k_w2, cmp_v_pe, cmp_v_w1, cmp_v_w2, fox_f_bias,
                  w_read_nsa, w_read_ret, w_read_fox, w_out):
    w1, w2 = split_w_in(w_in[l])
    pe = jnp.stack([cmp_k_pe[l].reshape(1, -1), cmp_v_pe[l].reshape(1, -1)])
    pe = jnp.broadcast_to(pe, (2, 8, pe.shape[-1])).astype(BF16)
    return {
        "norm_mix": norm_mix[l].reshape(1, -1),
        "w1": w1, "w2": w2,
        "cmp_pe": pe,
        "cmp_w1": jnp.stack([cmp_k_w1[l], cmp_v_w1[l]]).astype(BF16),
        "cmp_w2": jnp.stack([cmp_k_w2[l], cmp_v_w2[l]]).astype(BF16),
        "fox_bias": jnp.broadcast_to(fox_f_bias[l][:, None, None], (FOX_HEADS, 1, LANES)),
        "wn": pad_read_nsa(w_read_nsa[l]),
        "wr": w_read_ret[l].astype(BF16),
        "wf": w_read_fox[l].astype(BF16),
        "wo": w_out[l].astype(BF16),
    }


def kernel(x, c, ada_w, ada_b, norm_mix, norm_ffn, w_in, cmp_k_pe, cmp_k_w1, cmp_k_w2, cmp_v_pe, cmp_v_w1,
           cmp_v_w2, fox_f_bias, w_read_nsa, w_read_ret, w_read_fox, w_out, ffn_w1, ffn_w3, ffn_w2, router_w,
           moe_w1, moe_w3, moe_w2, final_norm_w):
    B, T, D = x.shape
    M = B * T
    depth = ada_w.shape[0]
    mod = modulation(c, ada_w, ada_b)
    cos_t, sin_t = rope_tables(T)
    ov_t, e_mat = nsa_constants(T)
    consts = (cos_t, sin_t, ov_t, e_mat, retention_consts(), nsa_gate_expand())
    xs = x.reshape(M, D)
    for l in range(depth):
        lw = layer_weights(l, norm_mix, w_in, cmp_k_pe, cmp_k_w1, cmp_k_w2, cmp_v_pe, cmp_v_w1, cmp_v_w2,
                           fox_f_bias, w_read_nsa, w_read_ret, w_read_fox, w_out)
        xs = token_mixing(xs, mod[l], lw, consts, B, T)
        nf = norm_ffn[l].reshape(1, D)
        if l % 2 == 0:
            k = l // 2
            xs = ffn(xs, mod[l], nf, ffn_w1[k][None].astype(BF16), ffn_w3[k][None].astype(BF16),
                     ffn_w2[k][None].astype(BF16), None, T, tm=512, tf=D_FF // 2)
        else:
            k = l // 2
            fuse = final_norm_w.reshape(1, D) if l == depth - 1 else None
            xs = moe_ffn(xs, mod[l], nf, router_w[k], moe_w1[k], moe_w3[k], moe_w2[k], T, fuse)
    if depth % 2 == 1:
        xs = final_norm(xs, final_norm_w.reshape(1, D))
    return xs.reshape(B, T, D)
```

```python
import functools
import math

import jax
import jax.numpy as jnp
import numpy as np
from jax import lax
from jax.experimental import pallas as pl
from jax.experimental.pallas import tpu as pltpu
from jax.experimental.pallas import tpu_sc as plsc

F32 = jnp.float32
BF16 = jnp.bfloat16

D_MODEL = 1024
DEPTH = 2
HEAD_DIM = 64
ROPE_THETA = 10000.0
NORM_EPS = 1e-6
NEG_INF = -1e30
REMOVED = -3e38

NSA_HEADS = 8
NSA_GROUPS = 2
NSA_HPG = NSA_HEADS // NSA_GROUPS
CMP_LEN = 32
CMP_STRIDE = 16
CMP_HIDDEN = 256
SEL_LEN = 64
SEL_TOPN = 16
WINDOW = 512
FORCE_SCORE = 1e4
NSA_QBLOCK = 128

RET_HEADS = 4
RET_QK_DIM = 64
RET_V_DIM = 128
RET_CHUNK = 128

FOX_HEADS = 8
FOX_TQ = 1024
LOG2E = 1.4426950408889634

D_FF = 2816
N_EXPERTS = 8
D_FF_EXPERT = 3584

LANES = 128
VMEM_LIMIT = 56 * 1024 * 1024

P1_NQ = 0
P1_RQ = 512
P1_RK = 768
P1_NKC = 1024
P1_NKS = 1152
P1_NKW = 1280
P1_COLS = 1408
P2_MG = 0
P2_RV = 3072
P2_RG = 3584
P2_FQ = 4096
P2_FK = 4608
P2_FV = 5120
P2_NVC = 5632
P2_NVS = 5760
P2_NVW = 5888
P2_SMALL = 6016
P2_COLS = 6144
NSA_OUT = NSA_HEADS * LANES


def _cparams(*sem):
    return pltpu.CompilerParams(dimension_semantics=tuple(sem), vmem_limit_bytes=VMEM_LIMIT)


def _sigmoid(x):
    return 1.0 / (1.0 + jnp.exp(-x))


def _dot(a, b):
    return jnp.dot(a, b, preferred_element_type=F32)


def _dot_nt(a, b):
    return lax.dot_general(a, b, (((1,), (1,)), ((), ())), preferred_element_type=F32)


def _dot_tn(a, b):
    return lax.dot_general(a, b, (((0,), (0,)), ((), ())), preferred_element_type=F32)


def _split3(x):
    hi = x.astype(BF16)
    r1 = x - hi.astype(F32)
    mid = r1.astype(BF16)
    lo = (r1 - mid.astype(F32)).astype(BF16)
    return hi, mid, lo


def _norm_mod(x, nw, sc, sh):
    ms = jnp.mean(x * x, axis=-1, keepdims=True)
    y = x * lax.rsqrt(ms + NORM_EPS) * nw
    return y * (1.0 + sc) + sh


def _mod_kernel(c_ref, w_ref, b_ref, o_ref):
    c = c_ref[...]
    s = c * _sigmoid(c)
    o_ref[0] = _dot(s.astype(BF16), w_ref[0].astype(BF16)) + b_ref[0]


def modulation(c, ada_w, ada_b):
    B, D = c.shape
    depth = ada_w.shape[0]
    rows = 8
    c_pad = jnp.zeros((rows, D), F32).at[:B].set(c)
    out = pl.pallas_call(
        _mod_kernel,
        grid=(depth, 6),
        in_specs=[pl.BlockSpec((rows, D), lambda l, j: (0, 0)),
                  pl.BlockSpec((1, D, D), lambda l, j: (l, 0, j)),
                  pl.BlockSpec((1, 1, D), lambda l, j: (l, 0, j))],
        out_specs=pl.BlockSpec((1, rows, D), lambda l, j: (l, 0, j)),
        out_shape=jax.ShapeDtypeStruct((depth, rows, 6 * D), F32),
        compiler_params=_cparams("parallel", "parallel"),
        name="modulation",
    )(c_pad, ada_w, ada_b.reshape(depth, 1, 6 * D))
    return out[:, :B].reshape(depth, B, 6, 1, D)


def _proj_plain_kernel(x_ref, nw_ref, sc_ref, sh_ref, w_ref, o_ref, *, tn):
    h = _norm_mod(x_ref[...], nw_ref[...], sc_ref[...], sh_ref[...]).astype(BF16)
    for n in range(w_ref.shape[1] // tn):
        cols = slice(n * tn, (n + 1) * tn)
        o_ref[:, cols] = _dot(h, w_ref[:, cols]).astype(o_ref.dtype)


def _proj_rope_kernel(x_ref, nw_ref, sc_ref, sh_ref, w_ref, cos_ref, sin_ref, o_ref, *, scales):
    h = _norm_mod(x_ref[...], nw_ref[...], sc_ref[...], sh_ref[...]).astype(BF16)
    y = _dot(h, w_ref[...])
    cos = cos_ref[...]
    sin = sin_ref[...]
    lane = lax.broadcasted_iota(jnp.int32, cos.shape, 1)
    first_half = (lane % HEAD_DIM) < (HEAD_DIM // 2)
    for g, scale in enumerate(scales):
        yg = y[:, g * LANES:(g + 1) * LANES]
        rot = jnp.where(first_half, pltpu.roll(yg, LANES - HEAD_DIM // 2, 1),
                        pltpu.roll(yg, HEAD_DIM // 2, 1))
        r = yg * cos + rot * sin
        if scale != 1.0:
            r = r * scale
        o_ref[:, g * LANES:(g + 1) * LANES] = r.astype(o_ref.dtype)


def _mod_specs(T, tm, sc_idx, sh_idx, nargs):
    per_b = T // tm
    if nargs == 1:
        return [pl.BlockSpec((None, None, 1, D_MODEL), lambda i: (i // per_b, sc_idx, 0, 0)),
                pl.BlockSpec((None, None, 1, D_MODEL), lambda i: (i // per_b, sh_idx, 0, 0))]
    return [pl.BlockSpec((None, None, 1, D_MODEL), lambda i, j: (i // per_b, sc_idx, 0, 0)),
            pl.BlockSpec((None, None, 1, D_MODEL), lambda i, j: (i // per_b, sh_idx, 0, 0))]


def proj_plain(x, mod_l, nw, w, T, *, tm=512, tn=512):
    M, D = x.shape
    N = w.shape[1]
    return pl.pallas_call(
        functools.partial(_proj_plain_kernel, tn=tn),
        grid=(M // tm,),
        in_specs=[pl.BlockSpec((tm, D), lambda i: (i, 0)),
                  pl.BlockSpec((1, D), lambda i: (0, 0))]
        + _mod_specs(T, tm, 1, 0, 1)
        + [pl.BlockSpec((D, N), lambda i: (0, 0))],
        out_specs=pl.BlockSpec((tm, N), lambda i: (i, 0)),
        out_shape=jax.ShapeDtypeStruct((M, N), BF16),
        compiler_params=_cparams("parallel"),
        name="proj_plain",
    )(x, nw, mod_l, mod_l, w)


def proj_rope(x, mod_l, nw, w, cos, sin, scales, T, *, tm=512):
    M, D = x.shape
    N = w.shape[1]
    per_b = T // tm
    return pl.pallas_call(
        functools.partial(_proj_rope_kernel, scales=scales),
        grid=(M // tm,),
        in_specs=[pl.BlockSpec((tm, D), lambda i: (i, 0)),
                  pl.BlockSpec((1, D), lambda i: (0, 0))]
        + _mod_specs(T, tm, 1, 0, 1)
        + [pl.BlockSpec((D, N), lambda i: (0, 0)),
           pl.BlockSpec((tm, LANES), lambda i: (i % per_b, 0)),
           pl.BlockSpec((tm, LANES), lambda i: (i % per_b, 0))],
        out_specs=pl.BlockSpec((tm, N), lambda i: (i, 0)),
        out_shape=jax.ShapeDtypeStruct((M, N), BF16),
        compiler_params=_cparams("parallel"),
        name="proj_rope",
    )(x, nw, mod_l, mod_l, w, cos, sin)


def rope_tables(T):
    d = HEAD_DIM
    pos = jnp.arange(T, dtype=F32)
    inv = ROPE_THETA ** (-jnp.arange(0, d, 2, dtype=F32) / d)
    ang = pos[:, None] * inv[None, :]
    cos = jnp.cos(ang)
    sin = jnp.sin(ang)
    cos_t = jnp.concatenate([cos, cos, cos, cos], axis=-1)
    sin_t = jnp.concatenate([-sin, sin, -sin, sin], axis=-1)
    return cos_t, sin_t


def split_w_in(w_in):
    sizes = [512, 128, 128, 128, 128, 128, 128, 24, 256, 256, 512, 512, 512, 512, 512, 8, 3072]
    offs = np.cumsum([0] + sizes)
    (nq, nkc, nvc, nks, nvs, nkw, nvw, ngate, rq, rk, rv, rg, fq, fk, fv, ff, mg) = [
        w_in[:, offs[i]:offs[i + 1]] for i in range(len(sizes))]
    D = w_in.shape[0]
    fq_s = fq * (HEAD_DIM ** -0.5 * LOG2E)
    small = jnp.concatenate([ngate, ff, jnp.zeros((D, LANES - 32), w_in.dtype)], axis=-1)
    w1 = jnp.concatenate([nq, rq, rk, nkc, nks, nkw], axis=-1).astype(BF16)
    w2 = jnp.concatenate([mg, rv, rg, fq_s, fk, fv, nvc, nvs, nvw, small], axis=-1).astype(BF16)
    assert w1.shape[1] == P1_COLS and w2.shape[1] == P2_COLS
    return w1, w2


def p1_scales():
    s = [1.0] * (P1_COLS // LANES)
    for g in range(P1_NQ // LANES, P1_RQ // LANES):
        s[g] = HEAD_DIM ** -0.5 * LOG2E
    for g in range(P1_RK // LANES, P1_NKC // LANES):
        s[g] = RET_QK_DIM ** -0.5
    return tuple(s)


def _compress_kernel(x_ref, pe_ref, w1_ref, w2_ref, o_ref):
    r = x_ref[...]
    half = r.shape[1]
    w1 = w1_ref[...]
    a = _dot(r, w1[:half])
    b = _dot(r, w1[half:])
    pe = _dot(pe_ref[...], w1)[0:1]
    n = a.shape[0]
    hid = a + pltpu.roll(b, n - 1, 0) + pe
    hid = hid * _sigmoid(hid)
    o_ref[...] = _dot(hid.astype(BF16), w2_ref[...]).astype(o_ref.dtype)


def compress(xr, pe, w1, w2):
    _, B, G, R, W = xr.shape
    H = w1.shape[-1]
    return pl.pallas_call(
        _compress_kernel,
        grid=(2, B, G),
        in_specs=[pl.BlockSpec((None, None, None, R, W), lambda s, b, g: (s, b, g, 0, 0)),
                  pl.BlockSpec((None, 8, 2 * W), lambda s, b, g: (s, 0, 0)),
                  pl.BlockSpec((None, 2 * W, H), lambda s, b, g: (s, 0, 0)),
                  pl.BlockSpec((None, H, HEAD_DIM), lambda s, b, g: (s, 0, 0))],
        out_specs=pl.BlockSpec((None, None, None, R, HEAD_DIM), lambda s, b, g: (s, b, g, 0, 0)),
        out_shape=jax.ShapeDtypeStruct((2, B, G, R, HEAD_DIM), BF16),
        compiler_params=_cparams("parallel", "parallel", "parallel"),
        name="nsa_compress",
    )(xr, pe, w1, w2)


def _stack_heads(q_ref, g):
    tq = q_ref.shape[0]
    half = lax.broadcasted_iota(jnp.int32, (tq, LANES), 1) // HEAD_DIM
    rows = []
    for hh in range(NSA_HPG):
        h = NSA_HPG * g + hh
        x = q_ref[:, (h // 2) * LANES:(h // 2 + 1) * LANES].astype(F32)
        if h % 2 != g:
            x = pltpu.roll(x, HEAD_DIM, 1)
        rows.append(jnp.where(half == g, x, 0.0).astype(BF16))
    return jnp.concatenate(rows, axis=0)


def _store_heads(o_ref, g, o, tq):
    for hh in range(NSA_HPG):
        h = NSA_HPG * g + hh
        o_ref[:, h * LANES:(h + 1) * LANES] = o[hh * tq:(hh + 1) * tq].astype(o_ref.dtype)


CMP_CHUNK = 128


def _nsa_cmp_kernel(q_ref, kc_ref, vc_ref, ov_ref, o_ref, m_ref, imp_ref, *, tq, n_sel, top_n):
    t0 = pl.program_id(1) * tq
    ncp = kc_ref.shape[0]
    nsp = ov_ref.shape[0]
    rows = NSA_HPG * tq

    def attend(ncols):
        kc = kc_ref[0:ncols, :]
        vc = vc_ref[0:ncols, :]
        n_idx = lax.broadcasted_iota(jnp.int32, (rows, ncols), 1)
        t_idx = t0 + lax.broadcasted_iota(jnp.int32, (rows, ncols), 0) % tq
        valid = (n_idx * CMP_STRIDE + (CMP_LEN - 1)) <= t_idx
        for g in range(NSA_GROUPS):
            q = _stack_heads(q_ref, g)
            s = jnp.where(valid, _dot_nt(q, kc), NEG_INF)
            m = jnp.max(s, axis=-1, keepdims=True)
            e = jnp.exp2(s - m)
            l = jnp.sum(e, axis=-1, keepdims=True)
            p = e * jnp.where(m > 0.5 * NEG_INF, 1.0 / l, 0.0)
            _store_heads(o_ref, g, _dot(p.astype(BF16), vc), tq)
            psum = p[0:tq]
            for hh in range(1, NSA_HPG):
                psum = psum + p[hh * tq:(hh + 1) * tq]
            imp_ref[g] = _dot_nt(ov_ref[:, 0:ncols], psum.astype(BF16))

    n_live = jnp.maximum((t0 + tq - CMP_LEN) // CMP_STRIDE + 1, 1)
    n_chunks = jnp.minimum((n_live + CMP_CHUNK - 1) // CMP_CHUNK, ncp // CMP_CHUNK)
    for nc in range(1, ncp // CMP_CHUNK + 1):
        pl.when(n_chunks == nc)(functools.partial(attend, nc * CMP_CHUNK))

    j_idx = lax.broadcasted_iota(jnp.int32, (nsp, tq), 0)
    cur = (t0 + lax.broadcasted_iota(jnp.int32, (nsp, tq), 1)) // SEL_LEN
    forced = (j_idx == 0) | (j_idx == cur) | (j_idx == cur - 1)
    j_f = j_idx.astype(F32)
    for g in range(NSA_GROUPS):
        score = jnp.where(j_idx <= cur, imp_ref[g], NEG_INF)
        score = jnp.where(forced | (j_idx >= n_sel), REMOVED, score)
        sel = jnp.where(forced, 1.0, 0.0)
        for _ in range(max(top_n - 3, 0)):
            mx = jnp.max(score, axis=0, keepdims=True)
            idx = jnp.min(jnp.where(score == mx, j_f, float(nsp)), axis=0, keepdims=True)
            hit = j_f == idx
            sel = jnp.where(hit, 1.0, sel)
            score = jnp.where(hit, REMOVED, score)
        sel = jnp.where(j_idx <= cur, sel, 0.0)
        m_ref[g] = sel.T.astype(m_ref.dtype)


def nsa_cmp_select(p1, kc, vc, ov_t, T):
    B = p1.shape[0]
    tq = NSA_QBLOCK
    ncp = kc.shape[1]
    nsp = ov_t.shape[0]
    n_sel = T // SEL_LEN
    return pl.pallas_call(
        functools.partial(_nsa_cmp_kernel, tq=tq, n_sel=n_sel, top_n=min(SEL_TOPN, n_sel)),
        grid=(B, T // tq),
        in_specs=[pl.BlockSpec((None, tq, NSA_HEADS * HEAD_DIM), lambda b, i: (b, i, 0)),
                  pl.BlockSpec((None, ncp, LANES), lambda b, i: (b, 0, 0)),
                  pl.BlockSpec((None, ncp, LANES), lambda b, i: (b, 0, 0)),
                  pl.BlockSpec((nsp, ncp), lambda b, i: (0, 0))],
        out_specs=[pl.BlockSpec((None, tq, NSA_OUT), lambda b, i: (b, i, 0)),
                   pl.BlockSpec((None, NSA_GROUPS, tq, nsp), lambda b, i: (b, 0, i, 0))],
        out_shape=[jax.ShapeDtypeStruct((B, T, NSA_OUT), BF16),
                   jax.ShapeDtypeStruct((B, NSA_GROUPS, T, nsp), BF16)],
        scratch_shapes=[pltpu.VMEM((NSA_GROUPS, nsp, tq), F32)],
        compiler_params=_cparams("parallel", "parallel"),
        name="nsa_cmp_select",
    )(p1, kc, vc, ov_t)


SEL_BONUS = 8192.0
NSA_SEL_TQ = 256
NSA_SEL_TK = 1024


def _nsa_sel_kernel(q_ref, k_ref, v_ref, m_ref, et_ref, o_ref, *, tq, tk):
    t0 = pl.program_id(1) * tq
    n_tiles = (t0 + tq + tk - 1) // tk
    rows = NSA_HPG * tq
    for g in range(NSA_GROUPS):
        q = jnp.concatenate([_stack_heads(q_ref, g), jnp.concatenate([m_ref[g]] * NSA_HPG, axis=0)], axis=1)
        den = HEAD_DIM * (1 - g)

        def step(j, carry, masked, q=q, g=g):
            m, acc = carry
            start = pl.multiple_of(j * tk, tk)
            ks = jnp.concatenate([k_ref[pl.ds(start, tk), :], et_ref[pl.ds(start, tk), :]], axis=1)
            s = _dot_nt(q, ks)
            if masked:
                trow = t0 + lax.broadcasted_iota(jnp.int32, (rows, tk), 0) % tq
                kpos = start + lax.broadcasted_iota(jnp.int32, (rows, tk), 1)
                s = jnp.where(kpos <= trow, s, NEG_INF)
            m_new = jnp.maximum(m, jnp.max(s, axis=-1, keepdims=True))
            p = jnp.exp2(s - m_new)
            acc = jnp.exp2(m - m_new) * acc + _dot(p.astype(BF16), v_ref[g, pl.ds(start, tk), :])
            return m_new, acc

        init = (jnp.full((rows, 1), NEG_INF, F32), jnp.zeros((rows, LANES), F32))
        carry = lax.fori_loop(0, n_tiles - 1, functools.partial(step, masked=False), init)
        _, acc = step(n_tiles - 1, carry, True)
        _store_heads(o_ref, g, acc / acc[:, den:den + 1], tq)


def nsa_value_augment(v):
    ones = jnp.ones_like(v[..., :HEAD_DIM])
    return jnp.stack([jnp.concatenate([v[..., :HEAD_DIM], ones], axis=-1),
                      jnp.concatenate([ones, v[..., HEAD_DIM:]], axis=-1)], axis=1)


def nsa_selected(p1, v_aug, sel, et_mat, T, *, tq=NSA_SEL_TQ, tk=NSA_SEL_TK):
    B = p1.shape[0]
    nsp = sel.shape[-1]
    return pl.pallas_call(
        functools.partial(_nsa_sel_kernel, tq=tq, tk=tk),
        grid=(B, T // tq),
        in_specs=[pl.BlockSpec((None, tq, NSA_HEADS * HEAD_DIM), lambda b, i: (b, i, 0)),
                  pl.BlockSpec((None, T, LANES), lambda b, i: (b, 0, P1_NKS // LANES)),
                  pl.BlockSpec((None, NSA_GROUPS, T, LANES), lambda b, i: (b, 0, 0, 0)),
                  pl.BlockSpec((None, NSA_GROUPS, tq, nsp), lambda b, i: (b, 0, i, 0)),
                  pl.BlockSpec((T, nsp), lambda b, i: (0, 0))],
        out_specs=pl.BlockSpec((None, tq, NSA_OUT), lambda b, i: (b, i, 0)),
        out_shape=jax.ShapeDtypeStruct((B, T, NSA_OUT), BF16),
        compiler_params=_cparams("parallel", "parallel"),
        name="nsa_selected",
    )(p1, p1, v_aug, sel, et_mat)


def _nsa_win_kernel(q_ref, k_ref, v_ref, b_ref, o_ref, *, tq):
    t0 = pl.program_id(1) * tq
    span = WINDOW + tq
    start = pl.multiple_of(jnp.maximum(t0 - WINDOW, 0), tq)
    ks = k_ref[pl.ds(start, span), :]
    vs = v_ref[pl.ds(start, span), :]

    def run(bias):
        bias = jnp.concatenate([bias] * NSA_HPG, axis=0)
        for g in range(NSA_GROUPS):
            s = _dot_nt(_stack_heads(q_ref, g), ks) + bias
            m = jnp.max(s, axis=-1, keepdims=True)
            p = jnp.exp2(s - m)
            l = jnp.sum(p, axis=-1, keepdims=True)
            _store_heads(o_ref, g, _dot(p.astype(BF16), vs) / l, tq)

    @pl.when(t0 >= WINDOW)
    def _():
        run(b_ref[...])

    @pl.when(t0 < WINDOW)
    def _():
        row = lax.broadcasted_iota(jnp.int32, (tq, span), 0)
        col = lax.broadcasted_iota(jnp.int32, (tq, span), 1)
        run(jnp.where(col <= t0 + row, 0.0, NEG_INF))


def nsa_window(p1, p2, T):
    B = p1.shape[0]
    tq = NSA_QBLOCK
    span = WINDOW + tq
    r = np.arange(tq)[:, None]
    c = np.arange(span)[None, :]
    band = jnp.asarray(np.where((c > r) & (c <= r + WINDOW), 0.0, NEG_INF), F32)
    return pl.pallas_call(
        functools.partial(_nsa_win_kernel, tq=tq),
        grid=(B, T // tq),
        in_specs=[pl.BlockSpec((None, tq, NSA_HEADS * HEAD_DIM), lambda b, i: (b, i, 0)),
                  pl.BlockSpec((None, T, LANES), lambda b, i: (b, 0, P1_NKW // LANES)),
                  pl.BlockSpec((None, T, LANES), lambda b, i: (b, 0, P2_NVW // LANES)),
                  pl.BlockSpec((tq, span), lambda b, i: (0, 0))],
        out_specs=pl.BlockSpec((None, tq, NSA_OUT), lambda b, i: (b, i, 0)),
        out_shape=jax.ShapeDtypeStruct((B, T, NSA_OUT), BF16),
        compiler_params=_cparams("parallel", "parallel"),
        name="nsa_window",
    )(p1, p1, p2, band)


def _retention_kernel(q_ref, k_ref, v_ref, g_ref, din_ref, qd_ref, kd_ref, cd_ref, o_ref, st_ref):
    @pl.when(pl.program_id(0) == 0)
    def _():
        st_ref[...] = jnp.zeros_like(st_ref)

    B = q_ref.shape[0]
    half = lax.broadcasted_iota(jnp.int32, (q_ref.shape[1], LANES), 1) // HEAD_DIM
    for b in range(B):
        for h in range(RET_HEADS):
            lanes = slice(h * LANES, (h + 1) * LANES)
            pair = slice((h // 2) * LANES, (h // 2 + 1) * LANES)
            qh = jnp.where(half == h % 2, q_ref[b, :, pair], 0.0).astype(BF16)
            kp = k_ref[b, :, pair]
            vh = v_ref[b, :, lanes]
            st = st_ref[b, h]
            inner = _dot_nt(qh, kp) * din_ref[h]
            o = _dot(inner.astype(BF16), vh) + _dot(qh, st.astype(BF16)) * qd_ref[h]
            kd = (kp.astype(F32) * kd_ref[h]).astype(BF16)
            st_ref[b, h] = st * cd_ref[h, 0:1, :] + _dot_tn(kd, vh)
            mu = jnp.mean(o, axis=-1, keepdims=True)
            d = o - mu
            var = jnp.mean(d * d, axis=-1, keepdims=True)
            on = d * lax.rsqrt(var + NORM_EPS)
            gh = g_ref[b, :, lanes].astype(F32)
            o_ref[b, :, lanes] = (gh * _sigmoid(gh) * on).astype(o_ref.dtype)


def retention_consts():
    C = RET_CHUNK
    H = RET_HEADS
    log_g = jnp.log(1.0 - 2.0 ** (-5.0 - jnp.arange(H, dtype=F32)))
    n = jnp.arange(C, dtype=F32)
    diff = n[:, None] - n[None, :]
    causal = diff >= 0
    decay_in = jnp.where(causal[None], jnp.exp(jnp.where(causal, diff, 0.0)[None] * log_g[:, None, None]), 0.0)
    q_decay = jnp.exp((n[None, :] + 1.0) * log_g[:, None])
    k_decay = jnp.exp((C - 1.0 - n)[None, :] * log_g[:, None])
    chunk_decay = jnp.exp(C * log_g)
    qd = jnp.broadcast_to(q_decay[:, :, None], (H, C, LANES))
    kd = jnp.broadcast_to(k_decay[:, :, None], (H, C, LANES))
    cd = jnp.broadcast_to(chunk_decay[:, None, None], (H, 8, LANES))
    return decay_in, qd, kd, cd


def retention(p1, p2, consts, T):
    B = p1.shape[0]
    C = RET_CHUNK
    din, qd, kd, cd = consts
    W = RET_HEADS * LANES
    full = lambda shape: pl.BlockSpec(shape, lambda c: (0,) * len(shape))
    return pl.pallas_call(
        _retention_kernel,
        grid=(T // C,),
        in_specs=[pl.BlockSpec((B, C, W // 2), lambda c: (0, c, P1_RQ // (W // 2))),
                  pl.BlockSpec((B, C, W // 2), lambda c: (0, c, P1_RK // (W // 2))),
                  pl.BlockSpec((B, C, W), lambda c: (0, c, P2_RV // W)),
                  pl.BlockSpec((B, C, W), lambda c: (0, c, P2_RG // W)),
                  full(din.shape), full(qd.shape), full(kd.shape), full(cd.shape)],
        out_specs=pl.BlockSpec((B, C, W), lambda c: (0, c, 0)),
        out_shape=jax.ShapeDtypeStruct((B, T, W), BF16),
        scratch_shapes=[pltpu.VMEM((B, RET_HEADS, LANES, LANES), F32)],
        compiler_params=_cparams("arbitrary"),
        name="retention",
    )(p1, p1, p2, p2, din, qd, kd, cd)


def _fox_cum_kernel(f_ref, b_ref, o_ref):
    x = f_ref[...] + b_ref[...]
    ls = jnp.minimum(x, 0.0) - jnp.log1p(jnp.exp(-jnp.abs(x)))
    R = x.shape[0]
    ki = lax.broadcasted_iota(jnp.int32, (LANES, LANES), 0)
    ji = lax.broadcasted_iota(jnp.int32, (LANES, LANES), 1)
    upper = jnp.where(ki <= ji, 1.0, 0.0).astype(BF16)
    hi, mid, lo = _split3(ls)
    rowcum = _dot(hi, upper) + _dot(mid, upper) + _dot(lo, upper)
    tot = jnp.broadcast_to(rowcum[:, LANES - 1:LANES], (R, LANES))
    ri = lax.broadcasted_iota(jnp.int32, (R, R), 0)
    ci = lax.broadcasted_iota(jnp.int32, (R, R), 1)
    lower = jnp.where(ci < ri, 1.0, 0.0).astype(BF16)
    hi, mid, lo = _split3(tot)
    offs = _dot(lower, hi) + _dot(lower, mid) + _dot(lower, lo)
    o_ref[...] = (rowcum + offs) * LOG2E


def fox_cum(f_logit, bias):
    B, H, R, _ = f_logit.shape
    return pl.pallas_call(
        _fox_cum_kernel,
        grid=(B, H),
        in_specs=[pl.BlockSpec((None, None, R, LANES), lambda b, h: (b, h, 0, 0)),
                  pl.BlockSpec((None, 1, LANES), lambda b, h: (h, 0, 0))],
        out_specs=pl.BlockSpec((None, None, R, LANES), lambda b, h: (b, h, 0, 0)),
        out_shape=jax.ShapeDtypeStruct((B, H, R, LANES), F32),
        compiler_params=_cparams("parallel", "parallel"),
        name="fox_cum",
    )(f_logit, bias)


FOX_BIAS_LANES = 3


def _fox_kernel(q_ref, k_ref, v_ref, c_ref, o_ref, ka_ref, va_ref, *, tq):
    i = pl.program_id(2)
    tk = tq
    T = k_ref.shape[0]
    chunk = 512

    @pl.when(i == 0)
    def _():
        lane = lax.broadcasted_iota(jnp.int32, (chunk, LANES), 1)
        ri = lax.broadcasted_iota(jnp.int32, (16, LANES), 0)
        ci = lax.broadcasted_iota(jnp.int32, (16, LANES), 1)
        place = jnp.where((ci == ri + HEAD_DIM) & (ri < FOX_BIAS_LANES), 1.0, 0.0).astype(BF16)

        def build(c, _):
            c0 = pl.multiple_of(c * chunk, chunk)
            kp = k_ref[pl.ds(c0, chunk), :].astype(F32)
            vp = v_ref[pl.ds(c0, chunk), :].astype(F32)
            for hh in range(2):
                hi, mid, lo = _split3(-c_ref[hh, :, pl.ds(c0, chunk)])
                terms = jnp.concatenate([hi, mid, lo, jnp.zeros((13, chunk), BF16)], axis=0)
                bias = _dot_tn(terms, place)
                kh = kp if hh == 0 else pltpu.roll(kp, HEAD_DIM, 1)
                vh = vp if hh == 0 else pltpu.roll(vp, HEAD_DIM, 1)
                ka_ref[hh, pl.ds(c0, chunk), :] = jnp.where(lane < HEAD_DIM, kh, bias).astype(BF16)
                va_ref[hh, pl.ds(c0, chunk), :] = jnp.where(lane < HEAD_DIM, vh, 1.0).astype(BF16)
            return 0

        lax.fori_loop(0, T // chunk, build, 0)

    row = lax.broadcasted_iota(jnp.int32, (tq, tk), 0)
    col = lax.broadcasted_iota(jnp.int32, (tq, tk), 1)
    lane = lax.broadcasted_iota(jnp.int32, (tq, LANES), 1)
    ones_lanes = (lane >= HEAD_DIM) & (lane < HEAD_DIM + FOX_BIAS_LANES)
    qp = q_ref[...].astype(F32)
    qs = [jnp.where(lane < HEAD_DIM, qh, jnp.where(ones_lanes, 1.0, 0.0)).astype(BF16)
          for qh in (qp, pltpu.roll(qp, HEAD_DIM, 1))]

    def step(j, carry, masked):
        start = pl.multiple_of(j * tk, tk)
        out = []
        for hh in range(2):
            m, acc = carry[hh]
            s = _dot_nt(qs[hh], ka_ref[hh, pl.ds(start, tk), :])
            if masked:
                s = jnp.where(col <= row, s, NEG_INF)
            m_new = jnp.maximum(m, jnp.max(s, axis=-1, keepdims=True))
            p = jnp.exp2(s - m_new)
            acc = jnp.exp2(m - m_new) * acc + _dot(p.astype(BF16), va_ref[hh, pl.ds(start, tk), :])
            out.append((m_new, acc))
        return tuple(out)

    one = (jnp.full((tq, 1), NEG_INF, F32), jnp.zeros((tq, LANES), F32))
    carry = lax.fori_loop(0, i, functools.partial(step, masked=False), (one, one))
    (_, acc0), (_, acc1) = step(i, carry, True)
    o0 = acc0 / acc0[:, HEAD_DIM:HEAD_DIM + 1]
    o1 = acc1 / acc1[:, HEAD_DIM:HEAD_DIM + 1]
    o_ref[...] = jnp.where(lane < HEAD_DIM, o0, pltpu.roll(o1, HEAD_DIM, 1)).astype(o_ref.dtype)


def fox_attention(p2, cum, T, *, tq=FOX_TQ):
    B = p2.shape[0]
    HP = FOX_HEADS // 2
    return pl.pallas_call(
        functools.partial(_fox_kernel, tq=tq),
        grid=(B, HP, T // tq),
        in_specs=[pl.BlockSpec((None, tq, LANES), lambda b, h, i: (b, i, P2_FQ // LANES + h)),
                  pl.BlockSpec((None, T, LANES), lambda b, h, i: (b, 0, P2_FK // LANES + h)),
                  pl.BlockSpec((None, T, LANES), lambda b, h, i: (b, 0, P2_FV // LANES + h)),
                  pl.BlockSpec((None, None, 2, 1, T), lambda b, h, i: (b, h, 0, 0, 0))],
        out_specs=pl.BlockSpec((None, tq, LANES), lambda b, h, i: (b, i, h)),
        out_shape=jax.ShapeDtypeStruct((B, T, FOX_HEADS * HEAD_DIM), BF16),
        scratch_shapes=[pltpu.VMEM((2, T, LANES), BF16), pltpu.VMEM((2, T, LANES), BF16)],
        compiler_params=_cparams("parallel", "parallel", "arbitrary"),
        name="fox_attention",
    )(p2, p2, p2, cum)


def _readout_kernel(ocmp_ref, osel_ref, owin_ref, small_ref, oret_ref, ofox_ref, mg_ref, x_ref, g1_ref,
                    ex_ref, wn_ref, wr_ref, wf_ref, wo_ref, o_ref):
    W = NSA_OUT
    gs = _sigmoid(small_ref[...].astype(F32)).astype(BF16)
    ge = _dot(gs, ex_ref[...])
    onsa = (ge[:, :W] * ocmp_ref[...].astype(F32) + ge[:, W:2 * W] * osel_ref[...].astype(F32)
            + ge[:, 2 * W:] * owin_ref[...].astype(F32))
    D = D_MODEL
    merged = (_sigmoid(mg_ref[:, :D].astype(F32)) * _dot(onsa.astype(BF16), wn_ref[...])
              + _sigmoid(mg_ref[:, D:2 * D].astype(F32)) * _dot(oret_ref[...], wr_ref[...])
              + _sigmoid(mg_ref[:, 2 * D:].astype(F32)) * _dot(ofox_ref[...], wf_ref[...]))
    y = _dot(merged.astype(BF16), wo_ref[...])
    o_ref[...] = x_ref[...] + g1_ref[...] * y


def readout(o_cmp, o_sel, o_win, p2, o_ret, o_fox, x, mod_l, ex, wn, wr, wf, wo, T, *, tm=512):
    M, D = x.shape
    per_b = T // tm
    W = NSA_OUT
    row = lambda width, col=0: pl.BlockSpec((tm, width), lambda i: (i, col))
    full = lambda a: pl.BlockSpec(a.shape, lambda i: (0,) * a.ndim)
    return pl.pallas_call(
        _readout_kernel,
        grid=(M // tm,),
        in_specs=[row(W), row(W), row(W), row(LANES, P2_SMALL // LANES), row(512), row(512),
                  row(3 * D, 0), row(D),
                  pl.BlockSpec((None, None, 1, D), lambda i: (i // per_b, 2, 0, 0)),
                  full(ex), full(wn), full(wr), full(wf), full(wo)],
        out_specs=row(D),
        out_shape=jax.ShapeDtypeStruct((M, D), F32),
        compiler_params=_cparams("parallel"),
        name="mixer_readout",
    )(o_cmp, o_sel, o_win, p2, o_ret, o_fox, p2, x, mod_l, ex, wn, wr, wf, wo)


def nsa_gate_expand():
    ex = np.zeros((LANES, 3 * NSA_OUT), np.float32)
    for br in range(3):
        for h in range(NSA_HEADS):
            c0 = br * NSA_OUT + h * LANES
            ex[br * NSA_HEADS + h, c0:c0 + LANES] = 1.0
    return jnp.asarray(ex, BF16)


def pad_read_nsa(w):
    D = w.shape[1]
    w = w.reshape(NSA_HEADS, HEAD_DIM, D)
    z = jnp.zeros_like(w)
    g = (np.arange(NSA_HEADS) // NSA_HPG)[:, None, None]
    lo = jnp.where(g == 0, w, z)
    hi = jnp.where(g == 1, w, z)
    return jnp.concatenate([lo, hi], axis=1).reshape(NSA_OUT, D).astype(BF16)


def _ffn_kernel(*refs, gated):
    if gated:
        x_ref, nw_ref, sc_ref, sh_ref, g2_ref, gate_ref, w1_ref, w3_ref, w2_ref, o_ref, h_ref, acc_ref = refs
    else:
        x_ref, nw_ref, sc_ref, sh_ref, g2_ref, w1_ref, w3_ref, w2_ref, o_ref, h_ref, acc_ref = refs
    e = pl.program_id(1)
    f = pl.program_id(2)

    @pl.when((e == 0) & (f == 0))
    def _():
        h_ref[...] = _norm_mod(x_ref[...], nw_ref[...], sc_ref[...], sh_ref[...]).astype(BF16)
        acc_ref[...] = jnp.zeros_like(acc_ref)

    h = h_ref[...]
    u = _dot(h, w1_ref[...])
    v = _dot(h, w3_ref[...])
    a = (u * _sigmoid(u) * v).astype(BF16)
    y = _dot(a, w2_ref[...])
    if gated:
        gate = gate_ref[...]
        lane = lax.broadcasted_iota(jnp.int32, gate.shape, 1)
        y = y * jnp.sum(jnp.where(lane == e, gate, 0.0), axis=-1, keepdims=True)
    acc_ref[...] += y

    @pl.when((e == pl.num_programs(1) - 1) & (f == pl.num_programs(2) - 1))
    def _():
        o_ref[...] = x_ref[...] + g2_ref[...] * acc_ref[...]


def ffn(x, mod_l, nw, w1, w3, w2, gate, T, *, tm, tf):
    M, D = x.shape
    E, _, F = w1.shape
    per_b = T // tm
    gated = gate is not None
    modspec = lambda k: pl.BlockSpec((None, None, 1, D), lambda i, e, f: (i // per_b, k, 0, 0))
    in_specs = [pl.BlockSpec((tm, D), lambda i, e, f: (i, 0)),
                pl.BlockSpec((1, D), lambda i, e, f: (0, 0)),
                modspec(4), modspec(3), modspec(5)]
    args = [x, nw, mod_l, mod_l, mod_l]
    if gated:
        in_specs.append(pl.BlockSpec((tm, LANES), lambda i, e, f: (i, 0)))
        args.append(gate)
    in_specs += [pl.BlockSpec((None, D, tf), lambda i, e, f: (e, 0, f)),
                 pl.BlockSpec((None, D, tf), lambda i, e, f: (e, 0, f)),
                 pl.BlockSpec((None, tf, D), lambda i, e, f: (e, f, 0))]
    args += [w1, w3, w2]
    return pl.pallas_call(
        functools.partial(_ffn_kernel, gated=gated),
        grid=(M // tm, E, F // tf),
        in_specs=in_specs,
        out_specs=pl.BlockSpec((tm, D), lambda i, e, f: (i, 0)),
        out_shape=jax.ShapeDtypeStruct((M, D), F32),
        scratch_shapes=[pltpu.VMEM((tm, D), BF16), pltpu.VMEM((tm, D), F32)],
        compiler_params=_cparams("parallel", "arbitrary", "arbitrary"),
        name="ffn_gated" if gated else "ffn_dense",
    )(*args)


MOE_TC = 512
MOE_TS = 512


def _router_kernel(x_ref, nw_ref, sc_ref, sh_ref, wh_ref, wl_ref, h_ref, gate_ref, rank_ref, cnt_ref, carry_ref):
    @pl.when(pl.program_id(0) == 0)
    def _():
        carry_ref[...] = jnp.zeros_like(carry_ref)

    h = _norm_mod(x_ref[...], nw_ref[...], sc_ref[...], sh_ref[...])
    hh = h.astype(BF16)
    h_ref[...] = hh.astype(h_ref.dtype)
    hl = (h - hh.astype(F32)).astype(BF16)
    logits = _dot(hh, wh_ref[...]) + (_dot(hl, wh_ref[...]) + _dot(hh, wl_ref[...]))
    tm = logits.shape[0]
    lane = lax.broadcasted_iota(jnp.int32, logits.shape, 1)
    logits = jnp.where(lane < N_EXPERTS, logits, REMOVED)
    lane_f = lane.astype(F32)
    v1 = jnp.max(logits, axis=-1, keepdims=True)
    i1 = jnp.min(jnp.where(logits == v1, lane_f, float(LANES)), axis=-1, keepdims=True)
    rest = jnp.where(lane_f == i1, REMOVED, logits)
    v2 = jnp.max(rest, axis=-1, keepdims=True)
    i2 = jnp.min(jnp.where(rest == v2, lane_f, float(LANES)), axis=-1, keepdims=True)
    e2 = jnp.exp(v2 - v1)
    w1 = 1.0 / (1.0 + e2)
    w2 = e2 / (1.0 + e2)
    gate_ref[...] = jnp.where(lane_f == i1, w1, jnp.where(lane_f == i2, w2, 0.0))

    sel = jnp.where((lane_f == i1) | (lane_f == i2), 1.0, 0.0)
    ri = lax.broadcasted_iota(jnp.int32, (tm, tm), 0)
    ci = lax.broadcasted_iota(jnp.int32, (tm, tm), 1)
    before = jnp.where(ci < ri, 1.0, 0.0).astype(BF16)
    rank = _dot(before, sel.astype(BF16)) + carry_ref[0:1, :]
    rank_ref[...] = jnp.where(sel > 0.0, rank, -1.0)
    carry_ref[...] = carry_ref[...] + jnp.sum(sel, axis=0, keepdims=True)
    cnt_ref[...] = carry_ref[...]


def router(x, mod_l, nw, w_router, T):
    M, D = x.shape
    tm = MOE_TC
    per_b = T // tm
    wp = jnp.zeros((D, LANES), F32).at[:, :N_EXPERTS].set(w_router)
    wh = wp.astype(BF16)
    wl = (wp - wh.astype(F32)).astype(BF16)
    return pl.pallas_call(
        _router_kernel,
        grid=(M // tm,),
        in_specs=[pl.BlockSpec((tm, D), lambda i: (i, 0)),
                  pl.BlockSpec((1, D), lambda i: (0, 0))]
        + _mod_specs(T, tm, 4, 3, 1)
        + [pl.BlockSpec((D, LANES), lambda i: (0, 0)),
           pl.BlockSpec((D, LANES), lambda i: (0, 0))],
        out_specs=[pl.BlockSpec((tm, D), lambda i: (i, 0)),
                   pl.BlockSpec((tm, LANES), lambda i: (i, 0)),
                   pl.BlockSpec((tm, LANES), lambda i: (i, 0)),
                   pl.BlockSpec((8, LANES), lambda i: (0, 0))],
        out_shape=[jax.ShapeDtypeStruct((M, D), F32),
                   jax.ShapeDtypeStruct((M, LANES), F32),
                   jax.ShapeDtypeStruct((M, LANES), F32),
                   jax.ShapeDtypeStruct((8, LANES), F32)],
        scratch_shapes=[pltpu.VMEM((8, LANES), F32)],
        compiler_params=_cparams("arbitrary"),
        name="moe_router",
    )(x, nw, mod_l, mod_l, wh, wl)


def _count_le(sorted_vals, x):
    return jnp.sum(sorted_vals[None, :] <= x[:, None], axis=1, dtype=jnp.int32)


def _moe_up_kernel(e_r, total, x_ref, w1_ref, w3_ref, o_ref, w1b_ref, w3b_ref):
    r = pl.program_id(1)
    live = r < total[0]

    @pl.when(live & ((r == 0) | (e_r[r] != e_r[jnp.maximum(r - 1, 0)])))
    def _():
        w1b_ref[...] = w1_ref[...].astype(BF16)
        w3b_ref[...] = w3_ref[...].astype(BF16)

    @pl.when(live)
    def _():
        x = x_ref[...].astype(BF16)
        u = _dot(x, w1b_ref[...])
        v = _dot(x, w3b_ref[...])
        o_ref[...] = (u * _sigmoid(u) * v).astype(o_ref.dtype)


def moe_up(xs, w1, w3, tiles, rt, *, tf=1792):
    R, D = xs.shape
    ts = MOE_TS
    F = w1.shape[-1]
    live = lambda r, total: jnp.minimum(r, total[0] - 1)
    return pl.pallas_call(
        _moe_up_kernel,
        grid_spec=pltpu.PrefetchScalarGridSpec(
            num_scalar_prefetch=2,
            grid=(F // tf, rt),
            in_specs=[pl.BlockSpec((ts, D), lambda n, r, e, total: (live(r, total), 0)),
                      pl.BlockSpec((None, D, tf), lambda n, r, e, total: (e[live(r, total)], 0, n)),
                      pl.BlockSpec((None, D, tf), lambda n, r, e, total: (e[live(r, total)], 0, n))],
            out_specs=pl.BlockSpec((ts, tf), lambda n, r, e, total: (r, n)),
            scratch_shapes=[pltpu.VMEM((D, tf), BF16), pltpu.VMEM((D, tf), BF16)],
        ),
        out_shape=jax.ShapeDtypeStruct((R, F), BF16),
        compiler_params=_cparams("arbitrary", "arbitrary"),
        name="moe_up",
    )(tiles["e"], tiles["total"], xs, w1, w3)


def _moe_down_kernel(e_r, total, a_ref, w2_ref, o_ref, w2b_ref):
    r = pl.program_id(0)
    live = r < total[0]

    @pl.when(live & ((r == 0) | (e_r[r] != e_r[jnp.maximum(r - 1, 0)])))
    def _():
        w2b_ref[...] = w2_ref[...].astype(BF16)

    @pl.when(live)
    def _():
        o_ref[...] = _dot(a_ref[...], w2b_ref[...]).astype(o_ref.dtype)


def moe_down(a, w2, tiles, rt):
    R, F = a.shape
    ts = MOE_TS
    D = w2.shape[-1]
    live = lambda r, total: jnp.minimum(r, total[0] - 1)
    return pl.pallas_call(
        _moe_down_kernel,
        grid_spec=pltpu.PrefetchScalarGridSpec(
            num_scalar_prefetch=2,
            grid=(rt,),
            in_specs=[pl.BlockSpec((ts, F), lambda r, e, total: (live(r, total), 0)),
                      pl.BlockSpec((None, F, D), lambda r, e, total: (e[live(r, total)], 0, 0))],
            out_specs=pl.BlockSpec((ts, D), lambda r, e, total: (r, 0)),
            scratch_shapes=[pltpu.VMEM((F, D), BF16)],
        ),
        out_shape=jax.ShapeDtypeStruct((R, D), F32),
        compiler_params=_cparams("arbitrary"),
        name="moe_down",
    )(tiles["e"], tiles["total"], a, w2)


SC_WINDOW = 32


def _sc_mesh():
    return plsc.VectorSubcoreMesh(core_axis_name="core", subcore_axis_name="subcore")


def sc_scatter_rows2(x, idx_a, idx_b, n_out):
    n, d = x.shape
    steps = n // SC_WINDOW

    @pl.kernel(out_type=jax.ShapeDtypeStruct((n_out, d), x.dtype), mesh=_sc_mesh(), scratch_types=[])
    def kern(x_hbm, ia_hbm, ib_hbm, o_hbm):
        def body(x_vmem, ia_vmem, ib_vmem):
            pltpu.sync_copy(x_vmem, o_hbm.at[ia_vmem.at[0]])
            pltpu.sync_copy(x_vmem, o_hbm.at[ib_vmem.at[0]])

        pltpu.emit_pipeline(
            body,
            grid=(steps,),
            in_specs=[pl.BlockSpec((SC_WINDOW, d), index_map=lambda i: (i, 0)),
                      pl.BlockSpec((1, SC_WINDOW), index_map=lambda i: (i, 0)),
                      pl.BlockSpec((1, SC_WINDOW), index_map=lambda i: (i, 0))],
            out_specs=[],
            core_axis_name=("core", "subcore"),
            dimension_semantics=(pltpu.PARALLEL,),
        )(x_hbm, ia_hbm, ib_hbm)

    return kern(x, idx_a.reshape(steps, SC_WINDOW), idx_b.reshape(steps, SC_WINDOW))


def sc_gather_rows(x, idx):
    n = idx.shape[0]
    d = x.shape[1]
    steps = n // SC_WINDOW

    @pl.kernel(out_type=jax.ShapeDtypeStruct((n, d), x.dtype), mesh=_sc_mesh(), scratch_types=[])
    def kern(x_hbm, i_hbm, o_hbm):
        def body(i_vmem, o_vmem):
            pltpu.sync_copy(x_hbm.at[i_vmem.at[0]], o_vmem)

        pltpu.emit_pipeline(
            body,
            grid=(steps,),
            in_specs=[pl.BlockSpec((1, SC_WINDOW), index_map=lambda i: (i, 0))],
            out_specs=[pl.BlockSpec((SC_WINDOW, d), index_map=lambda i: (i, 0))],
            core_axis_name=("core", "subcore"),
            dimension_semantics=(pltpu.PARALLEL,),
        )(i_hbm, o_hbm)

    return kern(x, idx.reshape(steps, SC_WINDOW))


def _moe_finish_kernel(x_ref, g2_ref, ya_ref, yb_ref, gate_ref, rank_ref, nw_ref, o_ref, *, normalize):
    gate = gate_ref[...]
    chosen = rank_ref[...] >= 0.0
    lane = lax.broadcasted_iota(jnp.int32, gate.shape, 1).astype(F32)
    first = jnp.min(jnp.where(chosen, lane, float(LANES)), axis=-1, keepdims=True)
    last = jnp.max(jnp.where(chosen, lane, -1.0), axis=-1, keepdims=True)
    wa = jnp.sum(jnp.where(lane == first, gate, 0.0), axis=-1, keepdims=True)
    wb = jnp.sum(jnp.where(lane == last, gate, 0.0), axis=-1, keepdims=True)
    x = x_ref[...] + g2_ref[...] * (wa * ya_ref[...] + wb * yb_ref[...])
    if normalize:
        ms = jnp.mean(x * x, axis=-1, keepdims=True)
        x = x * lax.rsqrt(ms + NORM_EPS) * nw_ref[...]
    o_ref[...] = x


def moe_finish(x, mod_l, y2, gate, rank, norm_w, T, *, tm=512):
    M, D = x.shape
    per_b = T // tm
    normalize = norm_w is not None
    if norm_w is None:
        norm_w = jnp.ones((1, D), F32)
    return pl.pallas_call(
        functools.partial(_moe_finish_kernel, normalize=normalize),
        grid=(M // tm,),
        in_specs=[pl.BlockSpec((tm, D), lambda i: (i, 0)),
                  pl.BlockSpec((None, None, 1, D), lambda i: (i // per_b, 5, 0, 0)),
                  pl.BlockSpec((None, tm, D), lambda i: (0, i, 0)),
                  pl.BlockSpec((None, tm, D), lambda i: (1, i, 0)),
                  pl.BlockSpec((tm, LANES), lambda i: (i, 0)),
                  pl.BlockSpec((tm, LANES), lambda i: (i, 0)),
                  pl.BlockSpec((1, D), lambda i: (0, 0))],
        out_specs=pl.BlockSpec((tm, D), lambda i: (i, 0)),
        out_shape=jax.ShapeDtypeStruct((M, D), F32),
        compiler_params=_cparams("parallel"),
        name="moe_finish",
    )(x, mod_l, y2, y2, gate, rank, norm_w)


def moe_ffn(x, mod_l, nw, w_router, w1, w3, w2, T, norm_w=None):
    M = x.shape[0]
    ts = MOE_TS
    rt = (2 * M) // ts + N_EXPERTS
    h, gate, rank, cnt = router(x, mod_l, nw, w_router, T)
    i32 = jnp.int32
    counts = cnt[0, :N_EXPERTS].astype(i32)
    ntile = (counts + ts - 1) // ts
    tile_end = jnp.cumsum(ntile)
    row_off = (tile_end - ntile) * ts
    e_r = jnp.minimum(_count_le(tile_end, jnp.arange(rt, dtype=i32)), N_EXPERTS - 1)
    tiles = dict(e=e_r, total=tile_end[-1].reshape(1).astype(i32))
    rk = rank[:, :N_EXPERTS].astype(i32)
    pos = row_off[None, :] + rk
    pos_a = jnp.min(jnp.where(rk >= 0, pos, rt * ts), axis=1)
    pos_b = jnp.max(jnp.where(rk >= 0, pos, -1), axis=1)

    xs = sc_scatter_rows2(h, pos_a, pos_b, rt * ts)
    a = moe_up(xs, w1, w3, tiles, rt)
    y = moe_down(a, w2, tiles, rt)
    y2 = sc_gather_rows(y, jnp.concatenate([pos_a, pos_b])).reshape(2, M, -1)
    return moe_finish(x, mod_l, y2, gate, rank, norm_w, T)


def _final_norm_kernel(x_ref, w_ref, o_ref):
    x = x_ref[...]
    ms = jnp.mean(x * x, axis=-1, keepdims=True)
    o_ref[...] = x * lax.rsqrt(ms + NORM_EPS) * w_ref[...]


def final_norm(x, w, *, tm=1024):
    M, D = x.shape
    return pl.pallas_call(
        _final_norm_kernel,
        grid=(M // tm,),
        in_specs=[pl.BlockSpec((tm, D), lambda i: (i, 0)), pl.BlockSpec((1, D), lambda i: (0, 0))],
        out_specs=pl.BlockSpec((tm, D), lambda i: (i, 0)),
        out_shape=jax.ShapeDtypeStruct((M, D), F32),
        compiler_params=_cparams("parallel"),
        name="final_norm",
    )(x, w)


def nsa_constants(T):
    n_sel = T // SEL_LEN
    nsp = max(LANES, n_sel)
    ncp = T // CMP_STRIDE
    cmp_start = np.arange(ncp) * CMP_STRIDE
    sel_start = np.arange(nsp) * SEL_LEN
    ov = ((cmp_start[:, None] < sel_start[None, :] + SEL_LEN)
          & (cmp_start[:, None] + CMP_LEN > sel_start[None, :]))
    ov[(T - CMP_LEN) // CMP_STRIDE + 1:] = False
    ov[:, n_sel:] = False
    et_mat = ((np.arange(T)[:, None] // SEL_LEN) == np.arange(nsp)[None, :]) * SEL_BONUS
    return jnp.asarray(ov.T, BF16), jnp.asarray(et_mat, BF16)


def token_mixing(x, mod_l, lw, consts, B, T):
    M = B * T
    cos_t, sin_t, ov_t, e_mat, ret_consts, ex = consts
    p1 = proj_rope(x, mod_l, lw["norm_mix"], lw["w1"], cos_t, sin_t, p1_scales(), T).reshape(B, T, P1_COLS)
    p2 = proj_plain(x, mod_l, lw["norm_mix"], lw["w2"], T).reshape(B, T, P2_COLS)

    def group_rows(a):
        return a.reshape(B, T, NSA_GROUPS, HEAD_DIM).transpose(0, 2, 1, 3).reshape(
            B, NSA_GROUPS, T // CMP_STRIDE, CMP_STRIDE * HEAD_DIM)

    xr = jnp.stack([group_rows(p1[:, :, P1_NKC:P1_NKC + LANES]), group_rows(p2[:, :, P2_NVC:P2_NVC + LANES])])
    cmp_out = compress(xr, lw["cmp_pe"], lw["cmp_w1"], lw["cmp_w2"])
    cmp_out = cmp_out.transpose(0, 1, 3, 2, 4).reshape(2, B, T // CMP_STRIDE, LANES)
    o_cmp, sel = nsa_cmp_select(p1, cmp_out[0], cmp_out[1], ov_t, T)
    o_sel = nsa_selected(p1, nsa_value_augment(p2[:, :, P2_NVS:P2_NVS + LANES]), sel, e_mat, T)
    o_win = nsa_window(p1, p2, T)

    o_ret = retention(p1, p2, ret_consts, T)

    ff = p2[:, :, P2_SMALL + 3 * NSA_HEADS:P2_SMALL + 3 * NSA_HEADS + FOX_HEADS].astype(F32)
    ff = ff.transpose(0, 2, 1).reshape(B, FOX_HEADS, T // LANES, LANES)
    cum = fox_cum(ff, lw["fox_bias"]).reshape(B, FOX_HEADS // 2, 2, 1, T)
    o_fox = fox_attention(p2, cum, T)

    return readout(o_cmp.reshape(M, -1), o_sel.reshape(M, -1), o_win.reshape(M, -1), p2.reshape(M, P2_COLS),
                   o_ret.reshape(M, -1), o_fox.reshape(M, -1), x, mod_l, ex,
                   lw["wn"], lw["wr"], lw["wf"], lw["wo"], T)


def layer_weights(l, norm_mix, w_in, cmp_k_pe, cmp_k_w1, cmp_k_w2, cmp_v_pe, cmp_v_w1, cmp_v_w2, fox_f_bias,
                  w_read_nsa, w_read_ret, w_read_fox, w_out):
    w1, w2 = split_w_in(w_in[l])
    pe = jnp.stack([cmp_k_pe[l].reshape(1, -1), cmp_v_pe[l].reshape(1, -1)])
    pe = jnp.broadcast_to(pe, (2, 8, pe.shape[-1])).astype(BF16)
    return {
        "norm_mix": norm_mix[l].reshape(1, -1),
        "w1": w1, "w2": w2,
        "cmp_pe": pe,
        "cmp_w1": jnp.stack([cmp_k_w1[l], cmp_v_w1[l]]).astype(BF16),
        "cmp_w2": jnp.stack([cmp_k_w2[l], cmp_v_w2[l]]).astype(BF16),
        "fox_bias": jnp.broadcast_to(fox_f_bias[l][:, None, None], (FOX_HEADS, 1, LANES)),
        "wn": pad_read_nsa(w_read_nsa[l]),
        "wr": w_read_ret[l].astype(BF16),
        "wf": w_read_fox[l].astype(BF16),
        "wo": w_out[l].astype(BF16),
    }


def kernel(x, c, ada_w, ada_b, norm_mix, norm_ffn, w_in, cmp_k_pe, cmp_k_w1, cmp_k_w2, cmp_v_pe, cmp_v_w1,
           cmp_v_w2, fox_f_bias, w_read_nsa, w_read_ret, w_read_fox, w_out, ffn_w1, ffn_w3, ffn_w2, router_w,
           moe_w1, moe_w3, moe_w2, final_norm_w):
    B, T, D = x.shape
    M = B * T
    depth = ada_w.shape[0]
    mod = modulation(c, ada_w, ada_b)
    cos_t, sin_t = rope_tables(T)
    ov_t, e_mat = nsa_constants(T)
    consts = (cos_t, sin_t, ov_t, e_mat, retention_consts(), nsa_gate_expand())
    xs = x.reshape(M, D)
    for l in range(depth):
        lw = layer_weights(l, norm_mix, w_in, cmp_k_pe, cmp_k_w1, cmp_k_w2, cmp_v_pe, cmp_v_w1, cmp_v_w2,
                           fox_f_bias, w_read_nsa, w_read_ret, w_read_fox, w_out)
        xs = token_mixing(xs, mod[l], lw, consts, B, T)
        nf = norm_ffn[l].reshape(1, D)
        if l % 2 == 0:
            k = l // 2
            xs = ffn(xs, mod[l], nf, ffn_w1[k][None].astype(BF16), ffn_w3[k][None].astype(BF16),
                     ffn_w2[k][None].astype(BF16), None, T, tm=512, tf=D_FF // 2)
        else:
            k = l // 2
            fuse = final_norm_w.reshape(1, D) if l == depth - 1 else None
            xs = moe_ffn(xs, mod[l], nf, router_w[k], moe_w1[k], moe_w3[k], moe_w2[k], T, fuse)
    if depth % 2 == 1:
        xs = final_norm(xs, final_norm_w.reshape(1, D))
    return xs.reshape(B, T, D)
```

```python
import functools
import math

import jax
import jax.numpy as jnp
import numpy as np
from jax import lax
from jax.experimental import pallas as pl
from jax.experimental.pallas import tpu as pltpu
from jax.experimental.pallas import tpu_sc as plsc

F32 = jnp.float32
BF16 = jnp.bfloat16

D_MODEL = 1024
DEPTH = 2
HEAD_DIM = 64
ROPE_THETA = 10000.0
NORM_EPS = 1e-6
NEG_INF = -1e30
REMOVED = -3e38

NSA_HEADS = 8
NSA_GROUPS = 2
NSA_HPG = NSA_HEADS // NSA_GROUPS
CMP_LEN = 32
CMP_STRIDE = 16
CMP_HIDDEN = 256
SEL_LEN = 64
SEL_TOPN = 16
WINDOW = 512
FORCE_SCORE = 1e4
NSA_QBLOCK = 128

RET_HEADS = 4
RET_QK_DIM = 64
RET_V_DIM = 128
RET_CHUNK = 128

FOX_HEADS = 8
FOX_TQ = 1024
LOG2E = 1.4426950408889634

D_FF = 2816
N_EXPERTS = 8
D_FF_EXPERT = 3584

LANES = 128
VMEM_LIMIT = 56 * 1024 * 1024

P1_NQ = 0
P1_RQ = 512
P1_RK = 768
P1_NKC = 1024
P1_NKS = 1152
P1_NKW = 1280
P1_COLS = 1408
P2_MG = 0
P2_RV = 3072
P2_RG = 3584
P2_FQ = 4096
P2_FK = 4608
P2_FV = 5120
P2_NVC = 5632
P2_NVS = 5760
P2_NVW = 5888
P2_SMALL = 6016
P2_COLS = 6144
NSA_OUT = NSA_HEADS * LANES


def _layer_spec(w, l, nidx):
    zeros = (0,) * (w.ndim - 1)
    return pl.BlockSpec((None,) + w.shape[1:], lambda *_: (l,) + zeros)


def _cparams(*sem):
    return pltpu.CompilerParams(dimension_semantics=tuple(sem), vmem_limit_bytes=VMEM_LIMIT)


def _sigmoid(x):
    return 1.0 / (1.0 + jnp.exp(-x))


def _dot(a, b):
    return jnp.dot(a, b, preferred_element_type=F32)


def _dot_nt(a, b):
    return lax.dot_general(a, b, (((1,), (1,)), ((), ())), preferred_element_type=F32)


def _dot_tn(a, b):
    return lax.dot_general(a, b, (((0,), (0,)), ((), ())), preferred_element_type=F32)


def _split3(x):
    hi = x.astype(BF16)
    r1 = x - hi.astype(F32)
    mid = r1.astype(BF16)
    lo = (r1 - mid.astype(F32)).astype(BF16)
    return hi, mid, lo


def _norm_mod(x, nw, sc, sh):
    ms = jnp.mean(x * x, axis=-1, keepdims=True)
    y = x * lax.rsqrt(ms + NORM_EPS) * nw
    return y * (1.0 + sc) + sh


def _mod_kernel(c_ref, w_ref, b_ref, o_ref):
    c = c_ref[...]
    s = c * _sigmoid(c)
    o_ref[0] = _dot(s.astype(BF16), w_ref[0].astype(BF16)) + b_ref[0]


def modulation(c, ada_w, ada_b):
    B, D = c.shape
    depth = ada_w.shape[0]
    rows = 8
    c_pad = jnp.zeros((rows, D), F32).at[:B].set(c)
    out = pl.pallas_call(
        _mod_kernel,
        grid=(depth, 6),
        in_specs=[pl.BlockSpec((rows, D), lambda l, j: (0, 0)),
                  pl.BlockSpec((1, D, D), lambda l, j: (l, 0, j)),
                  pl.BlockSpec((1, 1, D), lambda l, j: (l, 0, j))],
        out_specs=pl.BlockSpec((1, rows, D), lambda l, j: (l, 0, j)),
        out_shape=jax.ShapeDtypeStruct((depth, rows, 6 * D), F32),
        compiler_params=_cparams("parallel", "parallel"),
        name="modulation",
    )(c_pad, ada_w, ada_b.reshape(depth, 1, 6 * D))
    return out[:, :B].reshape(depth, B, 6, 1, D)


def _proj_plain_kernel(x_ref, nw_ref, sc_ref, sh_ref, w_ref, o_ref, *, tn):
    h = _norm_mod(x_ref[...], nw_ref[...], sc_ref[...], sh_ref[...]).astype(BF16)
    for n in range(w_ref.shape[1] // tn):
        cols = slice(n * tn, (n + 1) * tn)
        o_ref[:, cols] = _dot(h, w_ref[:, cols]).astype(o_ref.dtype)


def _proj_rope_kernel(x_ref, nw_ref, sc_ref, sh_ref, w_ref, cos_ref, sin_ref, o_ref, *, scales):
    h = _norm_mod(x_ref[...], nw_ref[...], sc_ref[...], sh_ref[...]).astype(BF16)
    y = _dot(h, w_ref[...])
    cos = cos_ref[...]
    sin = sin_ref[...]
    lane = lax.broadcasted_iota(jnp.int32, cos.shape, 1)
    first_half = (lane % HEAD_DIM) < (HEAD_DIM // 2)
    for g, scale in enumerate(scales):
        yg = y[:, g * LANES:(g + 1) * LANES]
        rot = jnp.where(first_half, pltpu.roll(yg, LANES - HEAD_DIM // 2, 1),
                        pltpu.roll(yg, HEAD_DIM // 2, 1))
        r = yg * cos + rot * sin
        if scale != 1.0:
            r = r * scale
        o_ref[:, g * LANES:(g + 1) * LANES] = r.astype(o_ref.dtype)


def _mod_specs(T, tm, sc_idx, sh_idx, nargs):
    per_b = T // tm
    if nargs == 1:
        return [pl.BlockSpec((None, None, 1, D_MODEL), lambda i: (i // per_b, sc_idx, 0, 0)),
                pl.BlockSpec((None, None, 1, D_MODEL), lambda i: (i // per_b, sh_idx, 0, 0))]
    return [pl.BlockSpec((None, None, 1, D_MODEL), lambda i, j: (i // per_b, sc_idx, 0, 0)),
            pl.BlockSpec((None, None, 1, D_MODEL), lambda i, j: (i // per_b, sh_idx, 0, 0))]


def proj_plain(x, mod_l, nw, w, l, T, *, tm=512, tn=512):
    M, D = x.shape
    N = w.shape[-1]
    return pl.pallas_call(
        functools.partial(_proj_plain_kernel, tn=tn),
        grid=(M // tm,),
        in_specs=[pl.BlockSpec((tm, D), lambda i: (i, 0)),
                  pl.BlockSpec((1, D), lambda i: (0, 0))]
        + _mod_specs(T, tm, 1, 0, 1)
        + [_layer_spec(w, l, 1)],
        out_specs=pl.BlockSpec((tm, N), lambda i: (i, 0)),
        out_shape=jax.ShapeDtypeStruct((M, N), BF16),
        compiler_params=_cparams("parallel"),
        name="proj_plain",
    )(x, nw, mod_l, mod_l, w)


def proj_rope(x, mod_l, nw, w, l, cos, sin, scales, T, *, tm=512):
    M, D = x.shape
    N = w.shape[-1]
    per_b = T // tm
    return pl.pallas_call(
        functools.partial(_proj_rope_kernel, scales=scales),
        grid=(M // tm,),
        in_specs=[pl.BlockSpec((tm, D), lambda i: (i, 0)),
                  pl.BlockSpec((1, D), lambda i: (0, 0))]
        + _mod_specs(T, tm, 1, 0, 1)
        + [_layer_spec(w, l, 1),
           pl.BlockSpec((tm, LANES), lambda i: (i % per_b, 0)),
           pl.BlockSpec((tm, LANES), lambda i: (i % per_b, 0))],
        out_specs=pl.BlockSpec((tm, N), lambda i: (i, 0)),
        out_shape=jax.ShapeDtypeStruct((M, N), BF16),
        compiler_params=_cparams("parallel"),
        name="proj_rope",
    )(x, nw, mod_l, mod_l, w, cos, sin)


def rope_tables(T):
    d = HEAD_DIM
    pos = jnp.arange(T, dtype=F32)
    inv = ROPE_THETA ** (-jnp.arange(0, d, 2, dtype=F32) / d)
    ang = pos[:, None] * inv[None, :]
    cos = jnp.cos(ang)
    sin = jnp.sin(ang)
    cos_t = jnp.concatenate([cos, cos, cos, cos], axis=-1)
    sin_t = jnp.concatenate([-sin, sin, -sin, sin], axis=-1)
    return cos_t, sin_t


def split_w_in(w_in):
    sizes = [512, 128, 128, 128, 128, 128, 128, 24, 256, 256, 512, 512, 512, 512, 512, 8, 3072]
    offs = np.cumsum([0] + sizes)
    (nq, nkc, nvc, nks, nvs, nkw, nvw, ngate, rq, rk, rv, rg, fq, fk, fv, ff, mg) = [
        w_in[..., offs[i]:offs[i + 1]] for i in range(len(sizes))]
    fq_s = fq * (HEAD_DIM ** -0.5 * LOG2E)
    small = jnp.concatenate([ngate, ff, jnp.zeros(ngate.shape[:-1] + (LANES - 32,), w_in.dtype)], axis=-1)
    w1 = jnp.concatenate([nq, rq, rk, nkc, nks, nkw], axis=-1).astype(BF16)
    w2 = jnp.concatenate([mg, rv, rg, fq_s, fk, fv, nvc, nvs, nvw, small], axis=-1).astype(BF16)
    assert w1.shape[-1] == P1_COLS and w2.shape[-1] == P2_COLS
    return w1, w2


def p1_scales():
    s = [1.0] * (P1_COLS // LANES)
    for g in range(P1_NQ // LANES, P1_RQ // LANES):
        s[g] = HEAD_DIM ** -0.5 * LOG2E
    for g in range(P1_RK // LANES, P1_NKC // LANES):
        s[g] = RET_QK_DIM ** -0.5
    return tuple(s)


def _compress_kernel(x_ref, pe_ref, w1_ref, w2_ref, o_ref):
    r = x_ref[...]
    half = r.shape[1]
    w1 = w1_ref[...]
    a = _dot(r, w1[:half])
    b = _dot(r, w1[half:])
    pe = _dot(pe_ref[...], w1)[0:1]
    n = a.shape[0]
    hid = a + pltpu.roll(b, n - 1, 0) + pe
    hid = hid * _sigmoid(hid)
    o_ref[...] = _dot(hid.astype(BF16), w2_ref[...]).astype(o_ref.dtype)


def compress(xr, pe, w1, w2):
    _, B, G, R, W = xr.shape
    H = w1.shape[-1]
    return pl.pallas_call(
        _compress_kernel,
        grid=(2, B, G),
        in_specs=[pl.BlockSpec((None, None, None, R, W), lambda s, b, g: (s, b, g, 0, 0)),
                  pl.BlockSpec((None, 8, 2 * W), lambda s, b, g: (s, 0, 0)),
                  pl.BlockSpec((None, 2 * W, H), lambda s, b, g: (s, 0, 0)),
                  pl.BlockSpec((None, H, HEAD_DIM), lambda s, b, g: (s, 0, 0))],
        out_specs=pl.BlockSpec((None, None, None, R, HEAD_DIM), lambda s, b, g: (s, b, g, 0, 0)),
        out_shape=jax.ShapeDtypeStruct((2, B, G, R, HEAD_DIM), BF16),
        compiler_params=_cparams("parallel", "parallel", "parallel"),
        name="nsa_compress",
    )(xr, pe, w1, w2)


def _stack_heads(q_ref, g):
    tq = q_ref.shape[0]
    half = lax.broadcasted_iota(jnp.int32, (tq, LANES), 1) // HEAD_DIM
    rows = []
    for hh in range(NSA_HPG):
        h = NSA_HPG * g + hh
        x = q_ref[:, (h // 2) * LANES:(h // 2 + 1) * LANES].astype(F32)
        if h % 2 != g:
            x = pltpu.roll(x, HEAD_DIM, 1)
        rows.append(jnp.where(half == g, x, 0.0).astype(BF16))
    return jnp.concatenate(rows, axis=0)


def _store_heads(o_ref, g, o, tq):
    for hh in range(NSA_HPG):
        h = NSA_HPG * g + hh
        o_ref[:, h * LANES:(h + 1) * LANES] = o[hh * tq:(hh + 1) * tq].astype(o_ref.dtype)


CMP_CHUNK = 128


def _nsa_cmp_kernel(q_ref, kc_ref, vc_ref, ov_ref, o_ref, m_ref, imp_ref, *, tq, n_sel, top_n):
    t0 = pl.program_id(1) * tq
    ncp = kc_ref.shape[0]
    nsp = ov_ref.shape[0]
    rows = NSA_HPG * tq

    def attend(ncols):
        kc = kc_ref[0:ncols, :]
        vc = vc_ref[0:ncols, :]
        n_idx = lax.broadcasted_iota(jnp.int32, (rows, ncols), 1)
        t_idx = t0 + lax.broadcasted_iota(jnp.int32, (rows, ncols), 0) % tq
        valid = (n_idx * CMP_STRIDE + (CMP_LEN - 1)) <= t_idx
        for g in range(NSA_GROUPS):
            q = _stack_heads(q_ref, g)
            s = jnp.where(valid, _dot_nt(q, kc), NEG_INF)
            m = jnp.max(s, axis=-1, keepdims=True)
            e = jnp.exp2(s - m)
            l = jnp.sum(e, axis=-1, keepdims=True)
            p = e * jnp.where(m > 0.5 * NEG_INF, 1.0 / l, 0.0)
            _store_heads(o_ref, g, _dot(p.astype(BF16), vc), tq)
            psum = p[0:tq]
            for hh in range(1, NSA_HPG):
                psum = psum + p[hh * tq:(hh + 1) * tq]
            imp_ref[g] = _dot_nt(ov_ref[:, 0:ncols], psum.astype(BF16))

    n_live = jnp.maximum((t0 + tq - CMP_LEN) // CMP_STRIDE + 1, 1)
    n_chunks = jnp.minimum((n_live + CMP_CHUNK - 1) // CMP_CHUNK, ncp // CMP_CHUNK)
    for nc in range(1, ncp // CMP_CHUNK + 1):
        pl.when(n_chunks == nc)(functools.partial(attend, nc * CMP_CHUNK))

    j_idx = lax.broadcasted_iota(jnp.int32, (nsp, tq), 0)
    cur = (t0 + lax.broadcasted_iota(jnp.int32, (nsp, tq), 1)) // SEL_LEN
    forced = (j_idx == 0) | (j_idx == cur) | (j_idx == cur - 1)
    j_f = j_idx.astype(F32)
    for g in range(NSA_GROUPS):
        score = jnp.where(j_idx <= cur, imp_ref[g], NEG_INF)
        score = jnp.where(forced | (j_idx >= n_sel), REMOVED, score)
        sel = jnp.where(forced, 1.0, 0.0)
        for _ in range(max(top_n - 3, 0)):
            mx = jnp.max(score, axis=0, keepdims=True)
            idx = jnp.min(jnp.where(score == mx, j_f, float(nsp)), axis=0, keepdims=True)
            hit = j_f == idx
            sel = jnp.where(hit, 1.0, sel)
            score = jnp.where(hit, REMOVED, score)
        sel = jnp.where(j_idx <= cur, sel, 0.0)
        m_ref[g] = sel.T.astype(m_ref.dtype)


def nsa_cmp_select(p1, kc, vc, ov_t, T):
    B = p1.shape[0]
    tq = NSA_QBLOCK
    ncp = kc.shape[1]
    nsp = ov_t.shape[0]
    n_sel = T // SEL_LEN
    return pl.pallas_call(
        functools.partial(_nsa_cmp_kernel, tq=tq, n_sel=n_sel, top_n=min(SEL_TOPN, n_sel)),
        grid=(B, T // tq),
        in_specs=[pl.BlockSpec((None, tq, NSA_HEADS * HEAD_DIM), lambda b, i: (b, i, 0)),
                  pl.BlockSpec((None, ncp, LANES), lambda b, i: (b, 0, 0)),
                  pl.BlockSpec((None, ncp, LANES), lambda b, i: (b, 0, 0)),
                  pl.BlockSpec((nsp, ncp), lambda b, i: (0, 0))],
        out_specs=[pl.BlockSpec((None, tq, NSA_OUT), lambda b, i: (b, i, 0)),
                   pl.BlockSpec((None, NSA_GROUPS, tq, nsp), lambda b, i: (b, 0, i, 0))],
        out_shape=[jax.ShapeDtypeStruct((B, T, NSA_OUT), BF16),
                   jax.ShapeDtypeStruct((B, NSA_GROUPS, T, nsp), BF16)],
        scratch_shapes=[pltpu.VMEM((NSA_GROUPS, nsp, tq), F32)],
        compiler_params=_cparams("parallel", "parallel"),
        name="nsa_cmp_select",
    )(p1, kc, vc, ov_t)


SEL_BONUS = 8192.0
NSA_SEL_TQ = 256
NSA_SEL_TK = 1024


def _nsa_sel_kernel(q_ref, k_ref, v_ref, m_ref, et_ref, o_ref, *, tq, tk):
    t0 = pl.program_id(1) * tq
    n_tiles = (t0 + tq + tk - 1) // tk
    rows = NSA_HPG * tq
    for g in range(NSA_GROUPS):
        q = jnp.concatenate([_stack_heads(q_ref, g), jnp.concatenate([m_ref[g]] * NSA_HPG, axis=0)], axis=1)
        den = HEAD_DIM * (1 - g)

        def step(j, carry, masked, q=q, g=g):
            m, acc = carry
            start = pl.multiple_of(j * tk, tk)
            ks = jnp.concatenate([k_ref[pl.ds(start, tk), :], et_ref[pl.ds(start, tk), :]], axis=1)
            s = _dot_nt(q, ks)
            if masked:
                trow = t0 + lax.broadcasted_iota(jnp.int32, (rows, tk), 0) % tq
                kpos = start + lax.broadcasted_iota(jnp.int32, (rows, tk), 1)
                s = jnp.where(kpos <= trow, s, NEG_INF)
            m_new = jnp.maximum(m, jnp.max(s, axis=-1, keepdims=True))
            p = jnp.exp2(s - m_new)
            acc = jnp.exp2(m - m_new) * acc + _dot(p.astype(BF16), v_ref[g, pl.ds(start, tk), :])
            return m_new, acc

        init = (jnp.full((rows, 1), NEG_INF, F32), jnp.zeros((rows, LANES), F32))
        carry = lax.fori_loop(0, n_tiles - 1, functools.partial(step, masked=False), init)
        _, acc = step(n_tiles - 1, carry, True)
        _store_heads(o_ref, g, acc / acc[:, den:den + 1], tq)


def nsa_value_augment(v):
    ones = jnp.ones_like(v[..., :HEAD_DIM])
    return jnp.stack([jnp.concatenate([v[..., :HEAD_DIM], ones], axis=-1),
                      jnp.concatenate([ones, v[..., HEAD_DIM:]], axis=-1)], axis=1)


def nsa_selected(p1, v_aug, sel, et_mat, T, *, tq=NSA_SEL_TQ, tk=NSA_SEL_TK):
    B = p1.shape[0]
    nsp = sel.shape[-1]
    return pl.pallas_call(
        functools.partial(_nsa_sel_kernel, tq=tq, tk=tk),
        grid=(B, T // tq),
        in_specs=[pl.BlockSpec((None, tq, NSA_HEADS * HEAD_DIM), lambda b, i: (b, i, 0)),
                  pl.BlockSpec((None, T, LANES), lambda b, i: (b, 0, P1_NKS // LANES)),
                  pl.BlockSpec((None, NSA_GROUPS, T, LANES), lambda b, i: (b, 0, 0, 0)),
                  pl.BlockSpec((None, NSA_GROUPS, tq, nsp), lambda b, i: (b, 0, i, 0)),
                  pl.BlockSpec((T, nsp), lambda b, i: (0, 0))],
        out_specs=pl.BlockSpec((None, tq, NSA_OUT), lambda b, i: (b, i, 0)),
        out_shape=jax.ShapeDtypeStruct((B, T, NSA_OUT), BF16),
        compiler_params=_cparams("parallel", "parallel"),
        name="nsa_selected",
    )(p1, p1, v_aug, sel, et_mat)


def _nsa_win_kernel(q_ref, k_ref, v_ref, b_ref, o_ref, *, tq):
    t0 = pl.program_id(1) * tq
    span = WINDOW + tq
    start = pl.multiple_of(jnp.maximum(t0 - WINDOW, 0), tq)
    ks = k_ref[pl.ds(start, span), :]
    vs = v_ref[pl.ds(start, span), :]

    def run(bias):
        bias = jnp.concatenate([bias] * NSA_HPG, axis=0)
        for g in range(NSA_GROUPS):
            s = _dot_nt(_stack_heads(q_ref, g), ks) + bias
            m = jnp.max(s, axis=-1, keepdims=True)
            p = jnp.exp2(s - m)
            l = jnp.sum(p, axis=-1, keepdims=True)
            _store_heads(o_ref, g, _dot(p.astype(BF16), vs) / l, tq)

    @pl.when(t0 >= WINDOW)
    def _():
        run(b_ref[...])

    @pl.when(t0 < WINDOW)
    def _():
        row = lax.broadcasted_iota(jnp.int32, (tq, span), 0)
        col = lax.broadcasted_iota(jnp.int32, (tq, span), 1)
        run(jnp.where(col <= t0 + row, 0.0, NEG_INF))


def nsa_window(p1, p2, T):
    B = p1.shape[0]
    tq = NSA_QBLOCK
    span = WINDOW + tq
    r = np.arange(tq)[:, None]
    c = np.arange(span)[None, :]
    band = jnp.asarray(np.where((c > r) & (c <= r + WINDOW), 0.0, NEG_INF), F32)
    return pl.pallas_call(
        functools.partial(_nsa_win_kernel, tq=tq),
        grid=(B, T // tq),
        in_specs=[pl.BlockSpec((None, tq, NSA_HEADS * HEAD_DIM), lambda b, i: (b, i, 0)),
                  pl.BlockSpec((None, T, LANES), lambda b, i: (b, 0, P1_NKW // LANES)),
                  pl.BlockSpec((None, T, LANES), lambda b, i: (b, 0, P2_NVW // LANES)),
                  pl.BlockSpec((tq, span), lambda b, i: (0, 0))],
        out_specs=pl.BlockSpec((None, tq, NSA_OUT), lambda b, i: (b, i, 0)),
        out_shape=jax.ShapeDtypeStruct((B, T, NSA_OUT), BF16),
        compiler_params=_cparams("parallel", "parallel"),
        name="nsa_window",
    )(p1, p1, p2, band)


def _retention_kernel(q_ref, k_ref, v_ref, g_ref, din_ref, qd_ref, kd_ref, cd_ref, o_ref, st_ref):
    @pl.when(pl.program_id(0) == 0)
    def _():
        st_ref[...] = jnp.zeros_like(st_ref)

    B = q_ref.shape[0]
    half = lax.broadcasted_iota(jnp.int32, (q_ref.shape[1], LANES), 1) // HEAD_DIM
    for b in range(B):
        for h in range(RET_HEADS):
            lanes = slice(h * LANES, (h + 1) * LANES)
            pair = slice((h // 2) * LANES, (h // 2 + 1) * LANES)
            qh = jnp.where(half == h % 2, q_ref[b, :, pair], 0.0).astype(BF16)
            kp = k_ref[b, :, pair]
            vh = v_ref[b, :, lanes]
            st = st_ref[b, h]
            inner = _dot_nt(qh, kp) * din_ref[h]
            o = _dot(inner.astype(BF16), vh) + _dot(qh, st.astype(BF16)) * qd_ref[h]
            kd = (kp.astype(F32) * kd_ref[h]).astype(BF16)
            st_ref[b, h] = st * cd_ref[h, 0:1, :] + _dot_tn(kd, vh)
            mu = jnp.mean(o, axis=-1, keepdims=True)
            d = o - mu
            var = jnp.mean(d * d, axis=-1, keepdims=True)
            on = d * lax.rsqrt(var + NORM_EPS)
            gh = g_ref[b, :, lanes].astype(F32)
            o_ref[b, :, lanes] = (gh * _sigmoid(gh) * on).astype(o_ref.dtype)


def retention_consts():
    C = RET_CHUNK
    H = RET_HEADS
    log_g = jnp.log(1.0 - 2.0 ** (-5.0 - jnp.arange(H, dtype=F32)))
    n = jnp.arange(C, dtype=F32)
    diff = n[:, None] - n[None, :]
    causal = diff >= 0
    decay_in = jnp.where(causal[None], jnp.exp(jnp.where(causal, diff, 0.0)[None] * log_g[:, None, None]), 0.0)
    q_decay = jnp.exp((n[None, :] + 1.0) * log_g[:, None])
    k_decay = jnp.exp((C - 1.0 - n)[None, :] * log_g[:, None])
    chunk_decay = jnp.exp(C * log_g)
    qd = jnp.broadcast_to(q_decay[:, :, None], (H, C, LANES))
    kd = jnp.broadcast_to(k_decay[:, :, None], (H, C, LANES))
    cd = jnp.broadcast_to(chunk_decay[:, None, None], (H, 8, LANES))
    return decay_in, qd, kd, cd


def retention(p1, p2, consts, T):
    B = p1.shape[0]
    C = RET_CHUNK
    din, qd, kd, cd = consts
    W = RET_HEADS * LANES
    full = lambda shape: pl.BlockSpec(shape, lambda c: (0,) * len(shape))
    return pl.pallas_call(
        _retention_kernel,
        grid=(T // C,),
        in_specs=[pl.BlockSpec((B, C, W // 2), lambda c: (0, c, P1_RQ // (W // 2))),
                  pl.BlockSpec((B, C, W // 2), lambda c: (0, c, P1_RK // (W // 2))),
                  pl.BlockSpec((B, C, W), lambda c: (0, c, P2_RV // W)),
                  pl.BlockSpec((B, C, W), lambda c: (0, c, P2_RG // W)),
                  full(din.shape), full(qd.shape), full(kd.shape), full(cd.shape)],
        out_specs=pl.BlockSpec((B, C, W), lambda c: (0, c, 0)),
        out_shape=jax.ShapeDtypeStruct((B, T, W), BF16),
        scratch_shapes=[pltpu.VMEM((B, RET_HEADS, LANES, LANES), F32)],
        compiler_params=_cparams("arbitrary"),
        name="retention",
    )(p1, p1, p2, p2, din, qd, kd, cd)


def _fox_cum_kernel(f_ref, b_ref, o_ref):
    x = f_ref[...] + b_ref[...]
    ls = jnp.minimum(x, 0.0) - jnp.log1p(jnp.exp(-jnp.abs(x)))
    R = x.shape[0]
    ki = lax.broadcasted_iota(jnp.int32, (LANES, LANES), 0)
    ji = lax.broadcasted_iota(jnp.int32, (LANES, LANES), 1)
    upper = jnp.where(ki <= ji, 1.0, 0.0).astype(BF16)
    hi, mid, lo = _split3(ls)
    rowcum = _dot(hi, upper) + _dot(mid, upper) + _dot(lo, upper)
    tot = jnp.broadcast_to(rowcum[:, LANES - 1:LANES], (R, LANES))
    ri = lax.broadcasted_iota(jnp.int32, (R, R), 0)
    ci = lax.broadcasted_iota(jnp.int32, (R, R), 1)
    lower = jnp.where(ci < ri, 1.0, 0.0).astype(BF16)
    hi, mid, lo = _split3(tot)
    offs = _dot(lower, hi) + _dot(lower, mid) + _dot(lower, lo)
    o_ref[...] = (rowcum + offs) * LOG2E


def fox_cum(f_logit, bias):
    B, H, R, _ = f_logit.shape
    return pl.pallas_call(
        _fox_cum_kernel,
        grid=(B, H),
        in_specs=[pl.BlockSpec((None, None, R, LANES), lambda b, h: (b, h, 0, 0)),
                  pl.BlockSpec((None, 1, LANES), lambda b, h: (h, 0, 0))],
        out_specs=pl.BlockSpec((None, None, R, LANES), lambda b, h: (b, h, 0, 0)),
        out_shape=jax.ShapeDtypeStruct((B, H, R, LANES), F32),
        compiler_params=_cparams("parallel", "parallel"),
        name="fox_cum",
    )(f_logit, bias)


FOX_BIAS_LANES = 3


def _fox_kernel(q_ref, k_ref, v_ref, c_ref, o_ref, ka_ref, va_ref, *, tq):
    i = pl.program_id(2)
    tk = tq
    T = k_ref.shape[0]
    chunk = 512

    @pl.when(i == 0)
    def _():
        lane = lax.broadcasted_iota(jnp.int32, (chunk, LANES), 1)
        ri = lax.broadcasted_iota(jnp.int32, (16, LANES), 0)
        ci = lax.broadcasted_iota(jnp.int32, (16, LANES), 1)
        place = jnp.where((ci == ri + HEAD_DIM) & (ri < FOX_BIAS_LANES), 1.0, 0.0).astype(BF16)

        def build(c, _):
            c0 = pl.multiple_of(c * chunk, chunk)
            kp = k_ref[pl.ds(c0, chunk), :].astype(F32)
            vp = v_ref[pl.ds(c0, chunk), :].astype(F32)
            for hh in range(2):
                hi, mid, lo = _split3(-c_ref[hh, :, pl.ds(c0, chunk)])
                terms = jnp.concatenate([hi, mid, lo, jnp.zeros((13, chunk), BF16)], axis=0)
                bias = _dot_tn(terms, place)
                kh = kp if hh == 0 else pltpu.roll(kp, HEAD_DIM, 1)
                vh = vp if hh == 0 else pltpu.roll(vp, HEAD_DIM, 1)
                ka_ref[hh, pl.ds(c0, chunk), :] = jnp.where(lane < HEAD_DIM, kh, bias).astype(BF16)
                va_ref[hh, pl.ds(c0, chunk), :] = jnp.where(lane < HEAD_DIM, vh, 1.0).astype(BF16)
            return 0

        lax.fori_loop(0, T // chunk, build, 0)

    row = lax.broadcasted_iota(jnp.int32, (tq, tk), 0)
    col = lax.broadcasted_iota(jnp.int32, (tq, tk), 1)
    lane = lax.broadcasted_iota(jnp.int32, (tq, LANES), 1)
    ones_lanes = (lane >= HEAD_DIM) & (lane < HEAD_DIM + FOX_BIAS_LANES)
    qp = q_ref[...].astype(F32)
    qs = [jnp.where(lane < HEAD_DIM, qh, jnp.where(ones_lanes, 1.0, 0.0)).astype(BF16)
          for qh in (qp, pltpu.roll(qp, HEAD_DIM, 1))]

    def step(j, carry, masked):
        start = pl.multiple_of(j * tk, tk)
        out = []
        for hh in range(2):
            m, acc = carry[hh]
            s = _dot_nt(qs[hh], ka_ref[hh, pl.ds(start, tk), :])
            if masked:
                s = jnp.where(col <= row, s, NEG_INF)
            m_new = jnp.maximum(m, jnp.max(s, axis=-1, keepdims=True))
            p = jnp.exp2(s - m_new)
            acc = jnp.exp2(m - m_new) * acc + _dot(p.astype(BF16), va_ref[hh, pl.ds(start, tk), :])
            out.append((m_new, acc))
        return tuple(out)

    one = (jnp.full((tq, 1), NEG_INF, F32), jnp.zeros((tq, LANES), F32))
    carry = lax.fori_loop(0, i, functools.partial(step, masked=False), (one, one))
    (_, acc0), (_, acc1) = step(i, carry, True)
    o0 = acc0 / acc0[:, HEAD_DIM:HEAD_DIM + 1]
    o1 = acc1 / acc1[:, HEAD_DIM:HEAD_DIM + 1]
    o_ref[...] = jnp.where(lane < HEAD_DIM, o0, pltpu.roll(o1, HEAD_DIM, 1)).astype(o_ref.dtype)


def fox_attention(p2, cum, T, *, tq=FOX_TQ):
    B = p2.shape[0]
    HP = FOX_HEADS // 2
    return pl.pallas_call(
        functools.partial(_fox_kernel, tq=tq),
        grid=(B, HP, T // tq),
        in_specs=[pl.BlockSpec((None, tq, LANES), lambda b, h, i: (b, i, P2_FQ // LANES + h)),
                  pl.BlockSpec((None, T, LANES), lambda b, h, i: (b, 0, P2_FK // LANES + h)),
                  pl.BlockSpec((None, T, LANES), lambda b, h, i: (b, 0, P2_FV // LANES + h)),
                  pl.BlockSpec((None, None, 2, 1, T), lambda b, h, i: (b, h, 0, 0, 0))],
        out_specs=pl.BlockSpec((None, tq, LANES), lambda b, h, i: (b, i, h)),
        out_shape=jax.ShapeDtypeStruct((B, T, FOX_HEADS * HEAD_DIM), BF16),
        scratch_shapes=[pltpu.VMEM((2, T, LANES), BF16), pltpu.VMEM((2, T, LANES), BF16)],
        compiler_params=_cparams("parallel", "parallel", "arbitrary"),
        name="fox_attention",
    )(p2, p2, p2, cum)


def _readout_kernel(ocmp_ref, osel_ref, owin_ref, small_ref, oret_ref, ofox_ref, mg_ref, x_ref, g1_ref,
                    ex_ref, wn_ref, wr_ref, wf_ref, wo_ref, o_ref):
    W = NSA_OUT
    gs = _sigmoid(small_ref[...].astype(F32)).astype(BF16)
    ge = _dot(gs, ex_ref[...])
    onsa = (ge[:, :W] * ocmp_ref[...].astype(F32) + ge[:, W:2 * W] * osel_ref[...].astype(F32)
            + ge[:, 2 * W:] * owin_ref[...].astype(F32))
    D = D_MODEL
    merged = (_sigmoid(mg_ref[:, :D].astype(F32)) * _dot(onsa.astype(BF16), wn_ref[...])
              + _sigmoid(mg_ref[:, D:2 * D].astype(F32)) * _dot(oret_ref[...], wr_ref[...])
              + _sigmoid(mg_ref[:, 2 * D:].astype(F32)) * _dot(ofox_ref[...], wf_ref[...]))
    y = _dot(merged.astype(BF16), wo_ref[...])
    o_ref[...] = x_ref[...] + g1_ref[...] * y


def readout(o_cmp, o_sel, o_win, p2, o_ret, o_fox, x, mod_l, ex, wn, wr, wf, wo, l, T, *, tm=512):
    M, D = x.shape
    per_b = T // tm
    W = NSA_OUT
    row = lambda width, col=0: pl.BlockSpec((tm, width), lambda i: (i, col))
    full = lambda a: pl.BlockSpec(a.shape, lambda i: (0,) * a.ndim)
    return pl.pallas_call(
        _readout_kernel,
        grid=(M // tm,),
        in_specs=[row(W), row(W), row(W), row(LANES, P2_SMALL // LANES), row(512), row(512),
                  row(3 * D, 0), row(D),
                  pl.BlockSpec((None, None, 1, D), lambda i: (i // per_b, 2, 0, 0)),
                  full(ex), _layer_spec(wn, l, 1), _layer_spec(wr, l, 1), _layer_spec(wf, l, 1), _layer_spec(wo, l, 1)],
        out_specs=row(D),
        out_shape=jax.ShapeDtypeStruct((M, D), F32),
        compiler_params=_cparams("parallel"),
        name="mixer_readout",
    )(o_cmp, o_sel, o_win, p2, o_ret, o_fox, p2, x, mod_l, ex, wn, wr, wf, wo)


def nsa_gate_expand():
    ex = np.zeros((LANES, 3 * NSA_OUT), np.float32)
    for br in range(3):
        for h in range(NSA_HEADS):
            c0 = br * NSA_OUT + h * LANES
            ex[br * NSA_HEADS + h, c0:c0 + LANES] = 1.0
    return jnp.asarray(ex, BF16)


def pad_read_nsa(w):
    depth, _, D = w.shape
    w = w.reshape(depth, NSA_HEADS, HEAD_DIM, D)
    z = jnp.zeros_like(w)
    g = (np.arange(NSA_HEADS) // NSA_HPG)[None, :, None, None]
    lo = jnp.where(g == 0, w, z)
    hi = jnp.where(g == 1, w, z)
    return jnp.concatenate([lo, hi], axis=2).reshape(depth, NSA_OUT, D).astype(BF16)


FFN_CHUNK = 512


def _ffn_kernel(x_ref, nw_ref, sc_ref, sh_ref, g2_ref, w1_ref, w3_ref, w2_ref, o_ref):
    x = x_ref[...]
    h = _norm_mod(x, nw_ref[...], sc_ref[...], sh_ref[...]).astype(BF16)
    F = w1_ref.shape[1]
    y = None
    for c0 in range(0, F, FFN_CHUNK):
        cols = slice(c0, min(c0 + FFN_CHUNK, F))
        u = _dot(h, w1_ref[:, cols])
        v = _dot(h, w3_ref[:, cols])
        part = _dot((u * _sigmoid(u) * v).astype(BF16), w2_ref[cols, :])
        y = part if y is None else y + part
    o_ref[...] = x + g2_ref[...] * y


def ffn(x, mod_l, nw, w1, w3, w2, T, *, tm=512):
    M, D = x.shape
    F = w1.shape[1]
    per_b = T // tm
    modspec = lambda k: pl.BlockSpec((None, None, 1, D), lambda i: (i // per_b, k, 0, 0))
    full = lambda a: pl.BlockSpec(a.shape, lambda i: (0,) * a.ndim)
    return pl.pallas_call(
        _ffn_kernel,
        grid=(M // tm,),
        in_specs=[pl.BlockSpec((tm, D), lambda i: (i, 0)),
                  pl.BlockSpec((1, D), lambda i: (0, 0)),
                  modspec(4), modspec(3), modspec(5), full(w1), full(w3), full(w2)],
        out_specs=pl.BlockSpec((tm, D), lambda i: (i, 0)),
        out_shape=jax.ShapeDtypeStruct((M, D), F32),
        compiler_params=_cparams("parallel"),
        name="ffn_dense",
    )(x, nw, mod_l, mod_l, mod_l, w1, w3, w2)


MOE_TC = 512
MOE_TS = 512


def _router_kernel(x_ref, nw_ref, sc_ref, sh_ref, wh_ref, wl_ref, h_ref, gate_ref, rank_ref, cnt_ref, carry_ref):
    @pl.when(pl.program_id(0) == 0)
    def _():
        carry_ref[...] = jnp.zeros_like(carry_ref)

    h = _norm_mod(x_ref[...], nw_ref[...], sc_ref[...], sh_ref[...])
    hh = h.astype(BF16)
    h_ref[...] = hh.astype(h_ref.dtype)
    hl = (h - hh.astype(F32)).astype(BF16)
    logits = _dot(hh, wh_ref[...]) + (_dot(hl, wh_ref[...]) + _dot(hh, wl_ref[...]))
    tm = logits.shape[0]
    lane = lax.broadcasted_iota(jnp.int32, logits.shape, 1)
    logits = jnp.where(lane < N_EXPERTS, logits, REMOVED)
    lane_f = lane.astype(F32)
    v1 = jnp.max(logits, axis=-1, keepdims=True)
    i1 = jnp.min(jnp.where(logits == v1, lane_f, float(LANES)), axis=-1, keepdims=True)
    rest = jnp.where(lane_f == i1, REMOVED, logits)
    v2 = jnp.max(rest, axis=-1, keepdims=True)
    i2 = jnp.min(jnp.where(rest == v2, lane_f, float(LANES)), axis=-1, keepdims=True)
    e2 = jnp.exp(v2 - v1)
    w1 = 1.0 / (1.0 + e2)
    w2 = e2 / (1.0 + e2)
    gate_ref[...] = jnp.where(lane_f == i1, w1, jnp.where(lane_f == i2, w2, 0.0))

    sel = jnp.where((lane_f == i1) | (lane_f == i2), 1.0, 0.0)
    ri = lax.broadcasted_iota(jnp.int32, (tm, tm), 0)
    ci = lax.broadcasted_iota(jnp.int32, (tm, tm), 1)
    before = jnp.where(ci < ri, 1.0, 0.0).astype(BF16)
    rank = _dot(before, sel.astype(BF16)) + carry_ref[0:1, :]
    rank_ref[...] = jnp.where(sel > 0.0, rank, -1.0)
    carry_ref[...] = carry_ref[...] + jnp.sum(sel, axis=0, keepdims=True)
    cnt_ref[...] = carry_ref[...]


def router(x, mod_l, nw, w_router, T):
    M, D = x.shape
    tm = MOE_TC
    per_b = T // tm
    wp = jnp.zeros((D, LANES), F32).at[:, :N_EXPERTS].set(w_router)
    wh = wp.astype(BF16)
    wl = (wp - wh.astype(F32)).astype(BF16)
    return pl.pallas_call(
        _router_kernel,
        grid=(M // tm,),
        in_specs=[pl.BlockSpec((tm, D), lambda i: (i, 0)),
                  pl.BlockSpec((1, D), lambda i: (0, 0))]
        + _mod_specs(T, tm, 4, 3, 1)
        + [pl.BlockSpec((D, LANES), lambda i: (0, 0)),
           pl.BlockSpec((D, LANES), lambda i: (0, 0))],
        out_specs=[pl.BlockSpec((tm, D), lambda i: (i, 0)),
                   pl.BlockSpec((tm, LANES), lambda i: (i, 0)),
                   pl.BlockSpec((tm, LANES), lambda i: (i, 0)),
                   pl.BlockSpec((8, LANES), lambda i: (0, 0))],
        out_shape=[jax.ShapeDtypeStruct((M, D), F32),
                   jax.ShapeDtypeStruct((M, LANES), F32),
                   jax.ShapeDtypeStruct((M, LANES), F32),
                   jax.ShapeDtypeStruct((8, LANES), F32)],
        scratch_shapes=[pltpu.VMEM((8, LANES), F32)],
        compiler_params=_cparams("arbitrary"),
        name="moe_router",
    )(x, nw, mod_l, mod_l, wh, wl)


def _count_le(sorted_vals, x):
    return jnp.sum(sorted_vals[None, :] <= x[:, None], axis=1, dtype=jnp.int32)


def _moe_up_kernel(e_r, total, x_ref, w1_ref, w3_ref, o_ref, w1b_ref, w3b_ref):
    r = pl.program_id(1)
    live = r < total[0]

    @pl.when(live & ((r == 0) | (e_r[r] != e_r[jnp.maximum(r - 1, 0)])))
    def _():
        w1b_ref[...] = w1_ref[...].astype(BF16)
        w3b_ref[...] = w3_ref[...].astype(BF16)

    @pl.when(live)
    def _():
        x = x_ref[...].astype(BF16)
        u = _dot(x, w1b_ref[...])
        v = _dot(x, w3b_ref[...])
        o_ref[...] = (u * _sigmoid(u) * v).astype(o_ref.dtype)


def moe_up(xs, w1, w3, tiles, rt, *, tf=1792):
    R, D = xs.shape
    ts = MOE_TS
    F = w1.shape[-1]
    live = lambda r, total: jnp.minimum(r, total[0] - 1)
    return pl.pallas_call(
        _moe_up_kernel,
        grid_spec=pltpu.PrefetchScalarGridSpec(
            num_scalar_prefetch=2,
            grid=(F // tf, rt),
            in_specs=[pl.BlockSpec((ts, D), lambda n, r, e, total: (live(r, total), 0)),
                      pl.BlockSpec((None, D, tf), lambda n, r, e, total: (e[live(r, total)], 0, n)),
                      pl.BlockSpec((None, D, tf), lambda n, r, e, total: (e[live(r, total)], 0, n))],
            out_specs=pl.BlockSpec((ts, tf), lambda n, r, e, total: (r, n)),
            scratch_shapes=[pltpu.VMEM((D, tf), BF16), pltpu.VMEM((D, tf), BF16)],
        ),
        out_shape=jax.ShapeDtypeStruct((R, F), BF16),
        compiler_params=_cparams("arbitrary", "arbitrary"),
        name="moe_up",
    )(tiles["e"], tiles["total"], xs, w1, w3)


def _moe_down_kernel(e_r, total, a_ref, w2_ref, o_ref, w2b_ref):
    r = pl.program_id(0)
    live = r < total[0]

    @pl.when(live & ((r == 0) | (e_r[r] != e_r[jnp.maximum(r - 1, 0)])))
    def _():
        w2b_ref[...] = w2_ref[...].astype(BF16)

    @pl.when(live)
    def _():
        o_ref[...] = _dot(a_ref[...], w2b_ref[...]).astype(o_ref.dtype)


def moe_down(a, w2, tiles, rt):
    R, F = a.shape
    ts = MOE_TS
    D = w2.shape[-1]
    live = lambda r, total: jnp.minimum(r, total[0] - 1)
    return pl.pallas_call(
        _moe_down_kernel,
        grid_spec=pltpu.PrefetchScalarGridSpec(
            num_scalar_prefetch=2,
            grid=(rt,),
            in_specs=[pl.BlockSpec((ts, F), lambda r, e, total: (live(r, total), 0)),
                      pl.BlockSpec((None, F, D), lambda r, e, total: (e[live(r, total)], 0, 0))],
            out_specs=pl.BlockSpec((ts, D), lambda r, e, total: (r, 0)),
            scratch_shapes=[pltpu.VMEM((F, D), BF16)],
        ),
        out_shape=jax.ShapeDtypeStruct((R, D), F32),
        compiler_params=_cparams("arbitrary"),
        name="moe_down",
    )(tiles["e"], tiles["total"], a, w2)


SC_WINDOW = 32


def _sc_mesh():
    return plsc.VectorSubcoreMesh(core_axis_name="core", subcore_axis_name="subcore")


def sc_scatter_rows2(x, idx_a, idx_b, n_out):
    n, d = x.shape
    steps = n // SC_WINDOW

    @pl.kernel(out_type=jax.ShapeDtypeStruct((n_out, d), x.dtype), mesh=_sc_mesh(), scratch_types=[])
    def kern(x_hbm, ia_hbm, ib_hbm, o_hbm):
        def body(x_vmem, ia_vmem, ib_vmem):
            pltpu.sync_copy(x_vmem, o_hbm.at[ia_vmem.at[0]])
            pltpu.sync_copy(x_vmem, o_hbm.at[ib_vmem.at[0]])

        pltpu.emit_pipeline(
            body,
            grid=(steps,),
            in_specs=[pl.BlockSpec((SC_WINDOW, d), index_map=lambda i: (i, 0)),
                      pl.BlockSpec((1, SC_WINDOW), index_map=lambda i: (i, 0)),
                      pl.BlockSpec((1, SC_WINDOW), index_map=lambda i: (i, 0))],
            out_specs=[],
            core_axis_name=("core", "subcore"),
            dimension_semantics=(pltpu.PARALLEL,),
        )(x_hbm, ia_hbm, ib_hbm)

    return kern(x, idx_a.reshape(steps, SC_WINDOW), idx_b.reshape(steps, SC_WINDOW))


def sc_gather_rows(x, idx):
    n = idx.shape[0]
    d = x.shape[1]
    steps = n // SC_WINDOW

    @pl.kernel(out_type=jax.ShapeDtypeStruct((n, d), x.dtype), mesh=_sc_mesh(), scratch_types=[])
    def kern(x_hbm, i_hbm, o_hbm):
        def body(i_vmem, o_vmem):
            pltpu.sync_copy(x_hbm.at[i_vmem.at[0]], o_vmem)

        pltpu.emit_pipeline(
            body,
            grid=(steps,),
            in_specs=[pl.BlockSpec((1, SC_WINDOW), index_map=lambda i: (i, 0))],
            out_specs=[pl.BlockSpec((SC_WINDOW, d), index_map=lambda i: (i, 0))],
            core_axis_name=("core", "subcore"),
            dimension_semantics=(pltpu.PARALLEL,),
        )(i_hbm, o_hbm)

    return kern(x, idx.reshape(steps, SC_WINDOW))


def _moe_finish_kernel(x_ref, g2_ref, ya_ref, yb_ref, gate_ref, rank_ref, nw_ref, o_ref, *, normalize):
    gate = gate_ref[...]
    chosen = rank_ref[...] >= 0.0
    lane = lax.broadcasted_iota(jnp.int32, gate.shape, 1).astype(F32)
    first = jnp.min(jnp.where(chosen, lane, float(LANES)), axis=-1, keepdims=True)
    last = jnp.max(jnp.where(chosen, lane, -1.0), axis=-1, keepdims=True)
    wa = jnp.sum(jnp.where(lane == first, gate, 0.0), axis=-1, keepdims=True)
    wb = jnp.sum(jnp.where(lane == last, gate, 0.0), axis=-1, keepdims=True)
    x = x_ref[...] + g2_ref[...] * (wa * ya_ref[...] + wb * yb_ref[...])
    if normalize:
        ms = jnp.mean(x * x, axis=-1, keepdims=True)
        x = x * lax.rsqrt(ms + NORM_EPS) * nw_ref[...]
    o_ref[...] = x


def moe_finish(x, mod_l, y2, gate, rank, norm_w, T, *, tm=512):
    M, D = x.shape
    per_b = T // tm
    normalize = norm_w is not None
    if norm_w is None:
        norm_w = jnp.ones((1, D), F32)
    return pl.pallas_call(
        functools.partial(_moe_finish_kernel, normalize=normalize),
        grid=(M // tm,),
        in_specs=[pl.BlockSpec((tm, D), lambda i: (i, 0)),
                  pl.BlockSpec((None, None, 1, D), lambda i: (i // per_b, 5, 0, 0)),
                  pl.BlockSpec((None, tm, D), lambda i: (0, i, 0)),
                  pl.BlockSpec((None, tm, D), lambda i: (1, i, 0)),
                  pl.BlockSpec((tm, LANES), lambda i: (i, 0)),
                  pl.BlockSpec((tm, LANES), lambda i: (i, 0)),
                  pl.BlockSpec((1, D), lambda i: (0, 0))],
        out_specs=pl.BlockSpec((tm, D), lambda i: (i, 0)),
        out_shape=jax.ShapeDtypeStruct((M, D), F32),
        compiler_params=_cparams("parallel"),
        name="moe_finish",
    )(x, mod_l, y2, y2, gate, rank, norm_w)


def moe_ffn(x, mod_l, nw, w_router, w1, w3, w2, T, norm_w=None):
    M = x.shape[0]
    ts = MOE_TS
    rt = (2 * M) // ts + N_EXPERTS
    h, gate, rank, cnt = router(x, mod_l, nw, w_router, T)
    i32 = jnp.int32
    counts = cnt[0, :N_EXPERTS].astype(i32)
    ntile = (counts + ts - 1) // ts
    tile_end = jnp.cumsum(ntile)
    row_off = (tile_end - ntile) * ts
    e_r = jnp.minimum(_count_le(tile_end, jnp.arange(rt, dtype=i32)), N_EXPERTS - 1)
    tiles = dict(e=e_r, total=tile_end[-1].reshape(1).astype(i32))
    rk = rank[:, :N_EXPERTS].astype(i32)
    pos = row_off[None, :] + rk
    pos_a = jnp.min(jnp.where(rk >= 0, pos, rt * ts), axis=1)
    pos_b = jnp.max(jnp.where(rk >= 0, pos, -1), axis=1)

    xs = sc_scatter_rows2(h, pos_a, pos_b, rt * ts)
    a = moe_up(xs, w1, w3, tiles, rt)
    y = moe_down(a, w2, tiles, rt)
    y2 = sc_gather_rows(y, jnp.concatenate([pos_a, pos_b])).reshape(2, M, -1)
    return moe_finish(x, mod_l, y2, gate, rank, norm_w, T)


def _final_norm_kernel(x_ref, w_ref, o_ref):
    x = x_ref[...]
    ms = jnp.mean(x * x, axis=-1, keepdims=True)
    o_ref[...] = x * lax.rsqrt(ms + NORM_EPS) * w_ref[...]


def final_norm(x, w, *, tm=1024):
    M, D = x.shape
    return pl.pallas_call(
        _final_norm_kernel,
        grid=(M // tm,),
        in_specs=[pl.BlockSpec((tm, D), lambda i: (i, 0)), pl.BlockSpec((1, D), lambda i: (0, 0))],
        out_specs=pl.BlockSpec((tm, D), lambda i: (i, 0)),
        out_shape=jax.ShapeDtypeStruct((M, D), F32),
        compiler_params=_cparams("parallel"),
        name="final_norm",
    )(x, w)


def nsa_constants(T):
    n_sel = T // SEL_LEN
    nsp = max(LANES, n_sel)
    ncp = T // CMP_STRIDE
    cmp_start = np.arange(ncp) * CMP_STRIDE
    sel_start = np.arange(nsp) * SEL_LEN
    ov = ((cmp_start[:, None] < sel_start[None, :] + SEL_LEN)
          & (cmp_start[:, None] + CMP_LEN > sel_start[None, :]))
    ov[(T - CMP_LEN) // CMP_STRIDE + 1:] = False
    ov[:, n_sel:] = False
    et_mat = ((np.arange(T)[:, None] // SEL_LEN) == np.arange(nsp)[None, :]) * SEL_BONUS
    return jnp.asarray(ov.T, BF16), jnp.asarray(et_mat, BF16)


def token_mixing(x, mod_l, lw, consts, B, T):
    M = B * T
    cos_t, sin_t, ov_t, e_mat, ret_consts, ex = consts
    l = lw["layer"]
    p1 = proj_rope(x, mod_l, lw["norm_mix"], lw["w1"], l, cos_t, sin_t, p1_scales(), T).reshape(B, T, P1_COLS)
    p2 = proj_plain(x, mod_l, lw["norm_mix"], lw["w2"], l, T).reshape(B, T, P2_COLS)

    def group_rows(a):
        return a.reshape(B, T, NSA_GROUPS, HEAD_DIM).transpose(0, 2, 1, 3).reshape(
            B, NSA_GROUPS, T // CMP_STRIDE, CMP_STRIDE * HEAD_DIM)

    xr = jnp.stack([group_rows(p1[:, :, P1_NKC:P1_NKC + LANES]), group_rows(p2[:, :, P2_NVC:P2_NVC + LANES])])
    cmp_out = compress(xr, lw["cmp_pe"], lw["cmp_w1"], lw["cmp_w2"])
    cmp_out = cmp_out.transpose(0, 1, 3, 2, 4).reshape(2, B, T // CMP_STRIDE, LANES)
    o_cmp, sel = nsa_cmp_select(p1, cmp_out[0], cmp_out[1], ov_t, T)
    o_sel = nsa_selected(p1, nsa_value_augment(p2[:, :, P2_NVS:P2_NVS + LANES]), sel, e_mat, T)
    o_win = nsa_window(p1, p2, T)

    o_ret = retention(p1, p2, ret_consts, T)

    ff = p2[:, :, P2_SMALL + 3 * NSA_HEADS:P2_SMALL + 3 * NSA_HEADS + FOX_HEADS].astype(F32)
    ff = ff.transpose(0, 2, 1).reshape(B, FOX_HEADS, T // LANES, LANES)
    cum = fox_cum(ff, lw["fox_bias"]).reshape(B, FOX_HEADS // 2, 2, 1, T)
    o_fox = fox_attention(p2, cum, T)

    return readout(o_cmp.reshape(M, -1), o_sel.reshape(M, -1), o_win.reshape(M, -1), p2.reshape(M, P2_COLS),
                   o_ret.reshape(M, -1), o_fox.reshape(M, -1), x, mod_l, ex,
                   lw["wn"], lw["wr"], lw["wf"], lw["wo"], l, T)


def mixer_weights(norm_mix, w_in, cmp_k_pe, cmp_k_w1, cmp_k_w2, cmp_v_pe, cmp_v_w1, cmp_v_w2, fox_f_bias,
                  w_read_nsa, w_read_ret, w_read_fox, w_out):
    depth = w_in.shape[0]
    w1, w2 = split_w_in(w_in)
    pe = jnp.stack([cmp_k_pe.reshape(depth, 1, -1), cmp_v_pe.reshape(depth, 1, -1)], axis=1)
    pe = jnp.broadcast_to(pe, (depth, 2, 8, pe.shape[-1])).astype(BF16)
    shared = {
        "w1": w1, "w2": w2,
        "wn": pad_read_nsa(w_read_nsa),
        "wr": w_read_ret.astype(BF16),
        "wf": w_read_fox.astype(BF16),
        "wo": w_out.astype(BF16),
    }
    cmp_w1 = jnp.stack([cmp_k_w1, cmp_v_w1], axis=1).astype(BF16)
    cmp_w2 = jnp.stack([cmp_k_w2, cmp_v_w2], axis=1).astype(BF16)
    return [dict(shared, layer=l, norm_mix=norm_mix[l].reshape(1, -1), cmp_pe=pe[l], cmp_w1=cmp_w1[l], cmp_w2=cmp_w2[l],
                 fox_bias=jnp.broadcast_to(fox_f_bias[l][:, None, None], (FOX_HEADS, 1, LANES)))
            for l in range(depth)]


def kernel(x, c, ada_w, ada_b, norm_mix, norm_ffn, w_in, cmp_k_pe, cmp_k_w1, cmp_k_w2, cmp_v_pe, cmp_v_w1,
           cmp_v_w2, fox_f_bias, w_read_nsa, w_read_ret, w_read_fox, w_out, ffn_w1, ffn_w3, ffn_w2, router_w,
           moe_w1, moe_w3, moe_w2, final_norm_w):
    B, T, D = x.shape
    M = B * T
    depth = ada_w.shape[0]
    mod = modulation(c, ada_w, ada_b)
    cos_t, sin_t = rope_tables(T)
    ov_t, e_mat = nsa_constants(T)
    consts = (cos_t, sin_t, ov_t, e_mat, retention_consts(), nsa_gate_expand())
    xs = x.reshape(M, D)
    lws = mixer_weights(norm_mix, w_in, cmp_k_pe, cmp_k_w1, cmp_k_w2, cmp_v_pe, cmp_v_w1, cmp_v_w2,
                        fox_f_bias, w_read_nsa, w_read_ret, w_read_fox, w_out)
    for l in range(depth):
        xs = token_mixing(xs, mod[l], lws[l], consts, B, T)
        nf = norm_ffn[l].reshape(1, D)
        if l % 2 == 0:
            k = l // 2
            xs = ffn(xs, mod[l], nf, ffn_w1[k].astype(BF16), ffn_w3[k].astype(BF16), ffn_w2[k].astype(BF16), T)
        else:
            k = l // 2
            fuse = final_norm_w.reshape(1, D) if l == depth - 1 else None
            xs = moe_ffn(xs, mod[l], nf, router_w[k], moe_w1[k], moe_w3[k], moe_w2[k], T, fuse)
    if depth % 2 == 1:
        xs = final_norm(xs, final_norm_w.reshape(1, D))
    return xs.reshape(B, T, D)
```

```python
import functools
import math

import jax
import jax.numpy as jnp
import numpy as np
from jax import lax
from jax.experimental import pallas as pl
from jax.experimental.pallas import tpu as pltpu
from jax.experimental.pallas import tpu_sc as plsc

F32 = jnp.float32
BF16 = jnp.bfloat16

D_MODEL = 1024
DEPTH = 2
HEAD_DIM = 64
ROPE_THETA = 10000.0
NORM_EPS = 1e-6
NEG_INF = -1e30
REMOVED = -3e38

NSA_HEADS = 8
NSA_GROUPS = 2
NSA_HPG = NSA_HEADS // NSA_GROUPS
CMP_LEN = 32
CMP_STRIDE = 16
CMP_HIDDEN = 256
SEL_LEN = 64
SEL_TOPN = 16
WINDOW = 512
FORCE_SCORE = 1e4
NSA_QBLOCK = 128

RET_HEADS = 4
RET_QK_DIM = 64
RET_V_DIM = 128
RET_CHUNK = 128

FOX_HEADS = 8
FOX_TQ = 1024
LOG2E = 1.4426950408889634

D_FF = 2816
N_EXPERTS = 8
D_FF_EXPERT = 3584

LANES = 128
VMEM_LIMIT = 56 * 1024 * 1024

P1_NQ = 0
P1_RQ = 512
P1_RK = 768
P1_NKC = 1024
P1_NKS = 1152
P1_NKW = 1280
P1_COLS = 1408
P2_MG = 0
P2_RV = 3072
P2_RG = 3584
P2_FQ = 4096
P2_FK = 4608
P2_FV = 5120
P2_NVC = 5632
P2_NVS = 5760
P2_NVW = 5888
P2_SMALL = 6016
P2_COLS = 6144
NSA_OUT = NSA_HEADS * LANES


def _layer_spec(w, l, nidx):
    zeros = (0,) * (w.ndim - 1)
    return pl.BlockSpec((None,) + w.shape[1:], lambda *_: (l,) + zeros)


def _cparams(*sem):
    return pltpu.CompilerParams(dimension_semantics=tuple(sem), vmem_limit_bytes=VMEM_LIMIT)


def _sigmoid(x):
    return 1.0 / (1.0 + jnp.exp(-x))


def _dot(a, b):
    return jnp.dot(a, b, preferred_element_type=F32)


def _dot_nt(a, b):
    return lax.dot_general(a, b, (((1,), (1,)), ((), ())), preferred_element_type=F32)


def _dot_tn(a, b):
    return lax.dot_general(a, b, (((0,), (0,)), ((), ())), preferred_element_type=F32)


def _split3(x):
    hi = x.astype(BF16)
    r1 = x - hi.astype(F32)
    mid = r1.astype(BF16)
    lo = (r1 - mid.astype(F32)).astype(BF16)
    return hi, mid, lo


def _norm_mod(x, nw, sc, sh):
    ms = jnp.mean(x * x, axis=-1, keepdims=True)
    y = x * lax.rsqrt(ms + NORM_EPS) * nw
    return y * (1.0 + sc) + sh


def _mod_kernel(c_ref, w_ref, b_ref, o_ref):
    c = c_ref[...]
    s = c * _sigmoid(c)
    o_ref[0] = _dot(s.astype(BF16), w_ref[0].astype(BF16)) + b_ref[0]


def modulation(c, ada_w, ada_b):
    B, D = c.shape
    depth = ada_w.shape[0]
    rows = 8
    c_pad = jnp.zeros((rows, D), F32).at[:B].set(c)
    out = pl.pallas_call(
        _mod_kernel,
        grid=(depth, 6),
        in_specs=[pl.BlockSpec((rows, D), lambda l, j: (0, 0)),
                  pl.BlockSpec((1, D, D), lambda l, j: (l, 0, j)),
                  pl.BlockSpec((1, 1, D), lambda l, j: (l, 0, j))],
        out_specs=pl.BlockSpec((1, rows, D), lambda l, j: (l, 0, j)),
        out_shape=jax.ShapeDtypeStruct((depth, rows, 6 * D), F32),
        compiler_params=_cparams("parallel", "parallel"),
        name="modulation",
    )(c_pad, ada_w, ada_b.reshape(depth, 1, 6 * D))
    return out[:, :B].reshape(depth, B, 6, 1, D)


def _proj_plain_kernel(x_ref, nw_ref, sc_ref, sh_ref, w_ref, o_ref, *, tn):
    h = _norm_mod(x_ref[...], nw_ref[...], sc_ref[...], sh_ref[...]).astype(BF16)
    for n in range(w_ref.shape[1] // tn):
        cols = slice(n * tn, (n + 1) * tn)
        o_ref[:, cols] = _dot(h, w_ref[:, cols]).astype(o_ref.dtype)


def _proj_rope_kernel(x_ref, nw_ref, sc_ref, sh_ref, w_ref, cos_ref, sin_ref, o_ref, *, scales):
    h = _norm_mod(x_ref[...], nw_ref[...], sc_ref[...], sh_ref[...]).astype(BF16)
    y = _dot(h, w_ref[...])
    cos = cos_ref[...]
    sin = sin_ref[...]
    lane = lax.broadcasted_iota(jnp.int32, cos.shape, 1)
    first_half = (lane % HEAD_DIM) < (HEAD_DIM // 2)
    for g, scale in enumerate(scales):
        yg = y[:, g * LANES:(g + 1) * LANES]
        rot = jnp.where(first_half, pltpu.roll(yg, LANES - HEAD_DIM // 2, 1),
                        pltpu.roll(yg, HEAD_DIM // 2, 1))
        r = yg * cos + rot * sin
        if scale != 1.0:
            r = r * scale
        o_ref[:, g * LANES:(g + 1) * LANES] = r.astype(o_ref.dtype)


def _mod_specs(T, tm, sc_idx, sh_idx, nargs):
    per_b = T // tm
    if nargs == 1:
        return [pl.BlockSpec((None, None, 1, D_MODEL), lambda i: (i // per_b, sc_idx, 0, 0)),
                pl.BlockSpec((None, None, 1, D_MODEL), lambda i: (i // per_b, sh_idx, 0, 0))]
    return [pl.BlockSpec((None, None, 1, D_MODEL), lambda i, j: (i // per_b, sc_idx, 0, 0)),
            pl.BlockSpec((None, None, 1, D_MODEL), lambda i, j: (i // per_b, sh_idx, 0, 0))]


def proj_plain(x, mod_l, nw, w, l, T, *, tm=512, tn=512):
    M, D = x.shape
    N = w.shape[-1]
    return pl.pallas_call(
        functools.partial(_proj_plain_kernel, tn=tn),
        grid=(M // tm,),
        in_specs=[pl.BlockSpec((tm, D), lambda i: (i, 0)),
                  pl.BlockSpec((1, D), lambda i: (0, 0))]
        + _mod_specs(T, tm, 1, 0, 1)
        + [_layer_spec(w, l, 1)],
        out_specs=pl.BlockSpec((tm, N), lambda i: (i, 0)),
        out_shape=jax.ShapeDtypeStruct((M, N), BF16),
        compiler_params=_cparams("parallel"),
        name="proj_plain",
    )(x, nw, mod_l, mod_l, w)


def proj_rope(x, mod_l, nw, w, l, cos, sin, scales, T, *, tm=512):
    M, D = x.shape
    N = w.shape[-1]
    per_b = T // tm
    return pl.pallas_call(
        functools.partial(_proj_rope_kernel, scales=scales),
        grid=(M // tm,),
        in_specs=[pl.BlockSpec((tm, D), lambda i: (i, 0)),
                  pl.BlockSpec((1, D), lambda i: (0, 0))]
        + _mod_specs(T, tm, 1, 0, 1)
        + [_layer_spec(w, l, 1),
           pl.BlockSpec((tm, LANES), lambda i: (i % per_b, 0)),
           pl.BlockSpec((tm, LANES), lambda i: (i % per_b, 0))],
        out_specs=pl.BlockSpec((tm, N), lambda i: (i, 0)),
        out_shape=jax.ShapeDtypeStruct((M, N), BF16),
        compiler_params=_cparams("parallel"),
        name="proj_rope",
    )(x, nw, mod_l, mod_l, w, cos, sin)


def rope_tables(T):
    d = HEAD_DIM
    pos = jnp.arange(T, dtype=F32)
    inv = ROPE_THETA ** (-jnp.arange(0, d, 2, dtype=F32) / d)
    ang = pos[:, None] * inv[None, :]
    cos = jnp.cos(ang)
    sin = jnp.sin(ang)
    cos_t = jnp.concatenate([cos, cos, cos, cos], axis=-1)
    sin_t = jnp.concatenate([-sin, sin, -sin, sin], axis=-1)
    return cos_t, sin_t


def split_w_in(w_in):
    sizes = [512, 128, 128, 128, 128, 128, 128, 24, 256, 256, 512, 512, 512, 512, 512, 8, 3072]
    offs = np.cumsum([0] + sizes)
    wb = w_in.astype(BF16)
    (nq, nkc, nvc, nks, nvs, nkw, nvw, ngate, rq, rk, rv, rg, fq, fk, fv, ff, mg) = [
        wb[..., offs[i]:offs[i + 1]] for i in range(len(sizes))]
    small = jnp.concatenate([ngate, ff, jnp.zeros(ngate.shape[:-1] + (LANES - 32,), BF16)], axis=-1)
    w1 = jnp.concatenate([nq, rq, rk, nkc, nks, nkw], axis=-1)
    w2 = jnp.concatenate([mg, rv, rg, fq, fk, fv, nvc, nvs, nvw, small], axis=-1)
    assert w1.shape[-1] == P1_COLS and w2.shape[-1] == P2_COLS
    return w1, w2


def p1_scales():
    s = [1.0] * (P1_COLS // LANES)
    for g in range(P1_NQ // LANES, P1_RQ // LANES):
        s[g] = HEAD_DIM ** -0.5 * LOG2E
    for g in range(P1_RK // LANES, P1_NKC // LANES):
        s[g] = RET_QK_DIM ** -0.5
    return tuple(s)


def _compress_kernel(x_ref, pe_ref, w1_ref, w2_ref, o_ref):
    r = x_ref[...]
    half = r.shape[1]
    w1 = w1_ref[...]
    a = _dot(r, w1[:half])
    b = _dot(r, w1[half:])
    pe = _dot(pe_ref[...], w1)[0:1]
    n = a.shape[0]
    hid = a + pltpu.roll(b, n - 1, 0) + pe
    hid = hid * _sigmoid(hid)
    o_ref[...] = _dot(hid.astype(BF16), w2_ref[...]).astype(o_ref.dtype)


def compress(xr, pe, w1, w2):
    _, B, G, R, W = xr.shape
    H = w1.shape[-1]
    return pl.pallas_call(
        _compress_kernel,
        grid=(2, B, G),
        in_specs=[pl.BlockSpec((None, None, None, R, W), lambda s, b, g: (s, b, g, 0, 0)),
                  pl.BlockSpec((None, 8, 2 * W), lambda s, b, g: (s, 0, 0)),
                  pl.BlockSpec((None, 2 * W, H), lambda s, b, g: (s, 0, 0)),
                  pl.BlockSpec((None, H, HEAD_DIM), lambda s, b, g: (s, 0, 0))],
        out_specs=pl.BlockSpec((None, None, None, R, HEAD_DIM), lambda s, b, g: (s, b, g, 0, 0)),
        out_shape=jax.ShapeDtypeStruct((2, B, G, R, HEAD_DIM), BF16),
        compiler_params=_cparams("parallel", "parallel", "parallel"),
        name="nsa_compress",
    )(xr, pe, w1, w2)


def _stack_heads(q_ref, g):
    tq = q_ref.shape[0]
    half = lax.broadcasted_iota(jnp.int32, (tq, LANES), 1) // HEAD_DIM
    rows = []
    for hh in range(NSA_HPG):
        h = NSA_HPG * g + hh
        x = q_ref[:, (h // 2) * LANES:(h // 2 + 1) * LANES].astype(F32)
        if h % 2 != g:
            x = pltpu.roll(x, HEAD_DIM, 1)
        rows.append(jnp.where(half == g, x, 0.0).astype(BF16))
    return jnp.concatenate(rows, axis=0)


def _store_heads(o_ref, g, o, tq):
    for hh in range(NSA_HPG):
        h = NSA_HPG * g + hh
        o_ref[:, h * LANES:(h + 1) * LANES] = o[hh * tq:(hh + 1) * tq].astype(o_ref.dtype)


CMP_CHUNK = 128


def _nsa_cmp_kernel(q_ref, kc_ref, vc_ref, ov_ref, o_ref, m_ref, imp_ref, *, tq, n_sel, top_n):
    t0 = pl.program_id(1) * tq
    ncp = kc_ref.shape[0]
    nsp = ov_ref.shape[0]
    rows = NSA_HPG * tq

    def attend(ncols):
        kc = kc_ref[0:ncols, :]
        vc = vc_ref[0:ncols, :]
        n_idx = lax.broadcasted_iota(jnp.int32, (rows, ncols), 1)
        t_idx = t0 + lax.broadcasted_iota(jnp.int32, (rows, ncols), 0) % tq
        valid = (n_idx * CMP_STRIDE + (CMP_LEN - 1)) <= t_idx
        for g in range(NSA_GROUPS):
            q = _stack_heads(q_ref, g)
            s = jnp.where(valid, _dot_nt(q, kc), NEG_INF)
            m = jnp.max(s, axis=-1, keepdims=True)
            e = jnp.exp2(s - m)
            l = jnp.sum(e, axis=-1, keepdims=True)
            p = e * jnp.where(m > 0.5 * NEG_INF, 1.0 / l, 0.0)
            _store_heads(o_ref, g, _dot(p.astype(BF16), vc), tq)
            psum = p[0:tq]
            for hh in range(1, NSA_HPG):
                psum = psum + p[hh * tq:(hh + 1) * tq]
            imp_ref[g] = _dot_nt(ov_ref[:, 0:ncols], psum.astype(BF16))

    n_live = jnp.maximum((t0 + tq - CMP_LEN) // CMP_STRIDE + 1, 1)
    n_chunks = jnp.minimum((n_live + CMP_CHUNK - 1) // CMP_CHUNK, ncp // CMP_CHUNK)
    for nc in range(1, ncp // CMP_CHUNK + 1):
        pl.when(n_chunks == nc)(functools.partial(attend, nc * CMP_CHUNK))

    j_idx = lax.broadcasted_iota(jnp.int32, (nsp, tq), 0)
    cur = (t0 + lax.broadcasted_iota(jnp.int32, (nsp, tq), 1)) // SEL_LEN
    forced = (j_idx == 0) | (j_idx == cur) | (j_idx == cur - 1)
    j_f = j_idx.astype(F32)
    for g in range(NSA_GROUPS):
        score = jnp.where(j_idx <= cur, imp_ref[g], NEG_INF)
        score = jnp.where(forced | (j_idx >= n_sel), REMOVED, score)
        sel = jnp.where(forced, 1.0, 0.0)
        for _ in range(max(top_n - 3, 0)):
            mx = jnp.max(score, axis=0, keepdims=True)
            idx = jnp.min(jnp.where(score == mx, j_f, float(nsp)), axis=0, keepdims=True)
            hit = j_f == idx
            sel = jnp.where(hit, 1.0, sel)
            score = jnp.where(hit, REMOVED, score)
        sel = jnp.where(j_idx <= cur, sel, 0.0)
        m_ref[g] = sel.T.astype(m_ref.dtype)


def nsa_cmp_select(p1, kc, vc, ov_t, T):
    B = p1.shape[0]
    tq = NSA_QBLOCK
    ncp = kc.shape[1]
    nsp = ov_t.shape[0]
    n_sel = T // SEL_LEN
    return pl.pallas_call(
        functools.partial(_nsa_cmp_kernel, tq=tq, n_sel=n_sel, top_n=min(SEL_TOPN, n_sel)),
        grid=(B, T // tq),
        in_specs=[pl.BlockSpec((None, tq, NSA_HEADS * HEAD_DIM), lambda b, i: (b, i, 0)),
                  pl.BlockSpec((None, ncp, LANES), lambda b, i: (b, 0, 0)),
                  pl.BlockSpec((None, ncp, LANES), lambda b, i: (b, 0, 0)),
                  pl.BlockSpec((nsp, ncp), lambda b, i: (0, 0))],
        out_specs=[pl.BlockSpec((None, tq, NSA_OUT), lambda b, i: (b, i, 0)),
                   pl.BlockSpec((None, NSA_GROUPS, tq, nsp), lambda b, i: (b, 0, i, 0))],
        out_shape=[jax.ShapeDtypeStruct((B, T, NSA_OUT), BF16),
                   jax.ShapeDtypeStruct((B, NSA_GROUPS, T, nsp), BF16)],
        scratch_shapes=[pltpu.VMEM((NSA_GROUPS, nsp, tq), F32)],
        compiler_params=_cparams("parallel", "parallel"),
        name="nsa_cmp_select",
    )(p1, kc, vc, ov_t)


SEL_BONUS = 8192.0
NSA_SEL_TQ = 256
NSA_SEL_TK = 1024


def _nsa_sel_kernel(q_ref, k_ref, v_ref, m_ref, et_ref, o_ref, *, tq, tk):
    t0 = pl.program_id(1) * tq
    n_full = t0 // tk
    rows = NSA_HPG * tq

    def update(carry, q, ks, vs, mask=None):
        m, acc = carry
        s = _dot_nt(q, ks)
        if mask is not None:
            s = jnp.where(mask, s, NEG_INF)
        m_new = jnp.maximum(m, jnp.max(s, axis=-1, keepdims=True))
        p = jnp.exp2(s - m_new)
        return m_new, jnp.exp2(m - m_new) * acc + _dot(p.astype(BF16), vs)

    qs, carries = [], []
    for g in range(NSA_GROUPS):
        q = jnp.concatenate([_stack_heads(q_ref, g), jnp.concatenate([m_ref[g]] * NSA_HPG, axis=0)], axis=1)

        def step(j, carry, q=q, g=g):
            start = pl.multiple_of(j * tk, tk)
            ks = jnp.concatenate([k_ref[pl.ds(start, tk), :], et_ref[pl.ds(start, tk), :]], axis=1)
            return update(carry, q, ks, v_ref[g, pl.ds(start, tk), :])

        init = (jnp.full((rows, 1), NEG_INF, F32), jnp.zeros((rows, LANES), F32))
        qs.append(q)
        carries.append(lax.fori_loop(0, n_full, step, init))

    start = pl.multiple_of(n_full * tk, tk)

    def tail(nk):
        trow = t0 + lax.broadcasted_iota(jnp.int32, (rows, nk), 0) % tq
        causal = start + lax.broadcasted_iota(jnp.int32, (rows, nk), 1) <= trow
        ks = jnp.concatenate([k_ref[pl.ds(start, nk), :], et_ref[pl.ds(start, nk), :]], axis=1)
        for g in range(NSA_GROUPS):
            _, acc = update(carries[g], qs[g], ks, v_ref[g, pl.ds(start, nk), :], causal)
            den = HEAD_DIM * (1 - g)
            _store_heads(o_ref, g, acc / acc[:, den:den + 1], tq)

    which = (t0 - start) // tq
    for v in range(tk // tq):
        pl.when(which == v)(functools.partial(tail, (v + 1) * tq))


def nsa_value_augment(v):
    ones = jnp.ones_like(v[..., :HEAD_DIM])
    return jnp.stack([jnp.concatenate([v[..., :HEAD_DIM], ones], axis=-1),
                      jnp.concatenate([ones, v[..., HEAD_DIM:]], axis=-1)], axis=1)


def nsa_selected(p1, v_aug, sel, et_mat, T, *, tq=NSA_SEL_TQ, tk=NSA_SEL_TK):
    B = p1.shape[0]
    nsp = sel.shape[-1]
    return pl.pallas_call(
        functools.partial(_nsa_sel_kernel, tq=tq, tk=tk),
        grid=(B, T // tq),
        in_specs=[pl.BlockSpec((None, tq, NSA_HEADS * HEAD_DIM), lambda b, i: (b, i, 0)),
                  pl.BlockSpec((None, T, LANES), lambda b, i: (b, 0, P1_NKS // LANES)),
                  pl.BlockSpec((None, NSA_GROUPS, T, LANES), lambda b, i: (b, 0, 0, 0)),
                  pl.BlockSpec((None, NSA_GROUPS, tq, nsp), lambda b, i: (b, 0, i, 0)),
                  pl.BlockSpec((T, nsp), lambda b, i: (0, 0))],
        out_specs=pl.BlockSpec((None, tq, NSA_OUT), lambda b, i: (b, i, 0)),
        out_shape=jax.ShapeDtypeStruct((B, T, NSA_OUT), BF16),
        compiler_params=_cparams("parallel", "parallel"),
        name="nsa_selected",
    )(p1, p1, v_aug, sel, et_mat)


def _nsa_win_kernel(q_ref, k_ref, v_ref, b_ref, o_ref, *, tq):
    t0 = pl.program_id(1) * tq
    span = WINDOW + tq
    start = pl.multiple_of(jnp.maximum(t0 - WINDOW, 0), tq)
    ks = k_ref[pl.ds(start, span), :]
    vs = v_ref[pl.ds(start, span), :]

    def run(bias):
        bias = jnp.concatenate([bias] * NSA_HPG, axis=0)
        for g in range(NSA_GROUPS):
            s = _dot_nt(_stack_heads(q_ref, g), ks) + bias
            m = jnp.max(s, axis=-1, keepdims=True)
            p = jnp.exp2(s - m)
            l = jnp.sum(p, axis=-1, keepdims=True)
            _store_heads(o_ref, g, _dot(p.astype(BF16), vs) / l, tq)

    @pl.when(t0 >= WINDOW)
    def _():
        run(b_ref[...])

    @pl.when(t0 < WINDOW)
    def _():
        row = lax.broadcasted_iota(jnp.int32, (tq, span), 0)
        col = lax.broadcasted_iota(jnp.int32, (tq, span), 1)
        run(jnp.where(col <= t0 + row, 0.0, NEG_INF))


def nsa_window(p1, p2, T):
    B = p1.shape[0]
    tq = NSA_QBLOCK
    span = WINDOW + tq
    r = np.arange(tq)[:, None]
    c = np.arange(span)[None, :]
    band = jnp.asarray(np.where((c > r) & (c <= r + WINDOW), 0.0, NEG_INF), F32)
    return pl.pallas_call(
        functools.partial(_nsa_win_kernel, tq=tq),
        grid=(B, T // tq),
        in_specs=[pl.BlockSpec((None, tq, NSA_HEADS * HEAD_DIM), lambda b, i: (b, i, 0)),
                  pl.BlockSpec((None, T, LANES), lambda b, i: (b, 0, P1_NKW // LANES)),
                  pl.BlockSpec((None, T, LANES), lambda b, i: (b, 0, P2_NVW // LANES)),
                  pl.BlockSpec((tq, span), lambda b, i: (0, 0))],
        out_specs=pl.BlockSpec((None, tq, NSA_OUT), lambda b, i: (b, i, 0)),
        out_shape=jax.ShapeDtypeStruct((B, T, NSA_OUT), BF16),
        compiler_params=_cparams("parallel", "parallel"),
        name="nsa_window",
    )(p1, p1, p2, band)


def _retention_kernel(q_ref, k_ref, v_ref, g_ref, din_ref, qd_ref, kd_ref, cd_ref, o_ref, st_ref):
    @pl.when(pl.program_id(0) == 0)
    def _():
        st_ref[...] = jnp.zeros_like(st_ref)

    B = q_ref.shape[0]
    half = lax.broadcasted_iota(jnp.int32, (q_ref.shape[1], LANES), 1) // HEAD_DIM
    for b in range(B):
        for h in range(RET_HEADS):
            lanes = slice(h * LANES, (h + 1) * LANES)
            pair = slice((h // 2) * LANES, (h // 2 + 1) * LANES)
            qh = jnp.where(half == h % 2, q_ref[b, :, pair], 0.0).astype(BF16)
            kp = k_ref[b, :, pair]
            vh = v_ref[b, :, lanes]
            st = st_ref[b, h]
            inner = _dot_nt(qh, kp) * din_ref[h]
            o = _dot(inner.astype(BF16), vh) + _dot(qh, st.astype(BF16)) * qd_ref[h]
            kd = (kp.astype(F32) * kd_ref[h]).astype(BF16)
            st_ref[b, h] = st * cd_ref[h, 0:1, :] + _dot_tn(kd, vh)
            mu = jnp.mean(o, axis=-1, keepdims=True)
            d = o - mu
            var = jnp.mean(d * d, axis=-1, keepdims=True)
            on = d * lax.rsqrt(var + NORM_EPS)
            gh = g_ref[b, :, lanes].astype(F32)
            o_ref[b, :, lanes] = (gh * _sigmoid(gh) * on).astype(o_ref.dtype)


def retention_consts():
    C = RET_CHUNK
    H = RET_HEADS
    log_g = jnp.log(1.0 - 2.0 ** (-5.0 - jnp.arange(H, dtype=F32)))
    n = jnp.arange(C, dtype=F32)
    diff = n[:, None] - n[None, :]
    causal = diff >= 0
    decay_in = jnp.where(causal[None], jnp.exp(jnp.where(causal, diff, 0.0)[None] * log_g[:, None, None]), 0.0)
    q_decay = jnp.exp((n[None, :] + 1.0) * log_g[:, None])
    k_decay = jnp.exp((C - 1.0 - n)[None, :] * log_g[:, None])
    chunk_decay = jnp.exp(C * log_g)
    qd = jnp.broadcast_to(q_decay[:, :, None], (H, C, LANES))
    kd = jnp.broadcast_to(k_decay[:, :, None], (H, C, LANES))
    cd = jnp.broadcast_to(chunk_decay[:, None, None], (H, 8, LANES))
    return decay_in, qd, kd, cd


def retention(p1, p2, consts, T):
    B = p1.shape[0]
    C = RET_CHUNK
    din, qd, kd, cd = consts
    W = RET_HEADS * LANES
    full = lambda shape: pl.BlockSpec(shape, lambda c: (0,) * len(shape))
    return pl.pallas_call(
        _retention_kernel,
        grid=(T // C,),
        in_specs=[pl.BlockSpec((B, C, W // 2), lambda c: (0, c, P1_RQ // (W // 2))),
                  pl.BlockSpec((B, C, W // 2), lambda c: (0, c, P1_RK // (W // 2))),
                  pl.BlockSpec((B, C, W), lambda c: (0, c, P2_RV // W)),
                  pl.BlockSpec((B, C, W), lambda c: (0, c, P2_RG // W)),
                  full(din.shape), full(qd.shape), full(kd.shape), full(cd.shape)],
        out_specs=pl.BlockSpec((B, C, W), lambda c: (0, c, 0)),
        out_shape=jax.ShapeDtypeStruct((B, T, W), BF16),
        scratch_shapes=[pltpu.VMEM((B, RET_HEADS, LANES, LANES), F32)],
        compiler_params=_cparams("arbitrary"),
        name="retention",
    )(p1, p1, p2, p2, din, qd, kd, cd)


def _fox_cum_kernel(f_ref, b_ref, o_ref):
    x = f_ref[...] + b_ref[...]
    ls = jnp.minimum(x, 0.0) - jnp.log1p(jnp.exp(-jnp.abs(x)))
    R = x.shape[0]
    ki = lax.broadcasted_iota(jnp.int32, (LANES, LANES), 0)
    ji = lax.broadcasted_iota(jnp.int32, (LANES, LANES), 1)
    upper = jnp.where(ki <= ji, 1.0, 0.0).astype(BF16)
    hi, mid, lo = _split3(ls)
    rowcum = _dot(hi, upper) + _dot(mid, upper) + _dot(lo, upper)
    tot = jnp.broadcast_to(rowcum[:, LANES - 1:LANES], (R, LANES))
    ri = lax.broadcasted_iota(jnp.int32, (R, R), 0)
    ci = lax.broadcasted_iota(jnp.int32, (R, R), 1)
    lower = jnp.where(ci < ri, 1.0, 0.0).astype(BF16)
    hi, mid, lo = _split3(tot)
    offs = _dot(lower, hi) + _dot(lower, mid) + _dot(lower, lo)
    o_ref[...] = (rowcum + offs) * LOG2E


def fox_cum(f_logit, bias):
    B, H, R, _ = f_logit.shape
    return pl.pallas_call(
        _fox_cum_kernel,
        grid=(B, H),
        in_specs=[pl.BlockSpec((None, None, R, LANES), lambda b, h: (b, h, 0, 0)),
                  pl.BlockSpec((None, 1, LANES), lambda b, h: (h, 0, 0))],
        out_specs=pl.BlockSpec((None, None, R, LANES), lambda b, h: (b, h, 0, 0)),
        out_shape=jax.ShapeDtypeStruct((B, H, R, LANES), F32),
        compiler_params=_cparams("parallel", "parallel"),
        name="fox_cum",
    )(f_logit, bias)


FOX_BIAS_LANES = 3


def _fox_kernel(q_ref, k_ref, v_ref, c_ref, o_ref, ka_ref, va_ref, *, tq):
    i = pl.program_id(2)
    tk = tq
    T = k_ref.shape[0]
    chunk = 512

    @pl.when(i == 0)
    def _():
        lane = lax.broadcasted_iota(jnp.int32, (chunk, LANES), 1)
        ri = lax.broadcasted_iota(jnp.int32, (16, LANES), 0)
        ci = lax.broadcasted_iota(jnp.int32, (16, LANES), 1)
        place = jnp.where((ci == ri + HEAD_DIM) & (ri < FOX_BIAS_LANES), 1.0, 0.0).astype(BF16)

        def build(c, _):
            c0 = pl.multiple_of(c * chunk, chunk)
            kp = k_ref[pl.ds(c0, chunk), :].astype(F32)
            vp = v_ref[pl.ds(c0, chunk), :].astype(F32)
            for hh in range(2):
                hi, mid, lo = _split3(-c_ref[hh, :, pl.ds(c0, chunk)])
                terms = jnp.concatenate([hi, mid, lo, jnp.zeros((13, chunk), BF16)], axis=0)
                bias = _dot_tn(terms, place)
                kh = kp if hh == 0 else pltpu.roll(kp, HEAD_DIM, 1)
                vh = vp if hh == 0 else pltpu.roll(vp, HEAD_DIM, 1)
                ka_ref[hh, pl.ds(c0, chunk), :] = jnp.where(lane < HEAD_DIM, kh, bias).astype(BF16)
                va_ref[hh, pl.ds(c0, chunk), :] = jnp.where(lane < HEAD_DIM, vh, 1.0).astype(BF16)
            return 0

        lax.fori_loop(0, T // chunk, build, 0)

    row = lax.broadcasted_iota(jnp.int32, (tq, tk), 0)
    col = lax.broadcasted_iota(jnp.int32, (tq, tk), 1)
    lane = lax.broadcasted_iota(jnp.int32, (tq, LANES), 1)
    ones_lanes = (lane >= HEAD_DIM) & (lane < HEAD_DIM + FOX_BIAS_LANES)
    qp = q_ref[...].astype(F32) * (HEAD_DIM ** -0.5 * LOG2E)
    qs = [jnp.where(lane < HEAD_DIM, qh, jnp.where(ones_lanes, 1.0, 0.0)).astype(BF16)
          for qh in (qp, pltpu.roll(qp, HEAD_DIM, 1))]

    def step(j, carry, masked):
        start = pl.multiple_of(j * tk, tk)
        out = []
        for hh in range(2):
            m, acc = carry[hh]
            s = _dot_nt(qs[hh], ka_ref[hh, pl.ds(start, tk), :])
            if masked:
                s = jnp.where(col <= row, s, NEG_INF)
            m_new = jnp.maximum(m, jnp.max(s, axis=-1, keepdims=True))
            p = jnp.exp2(s - m_new)
            acc = jnp.exp2(m - m_new) * acc + _dot(p.astype(BF16), va_ref[hh, pl.ds(start, tk), :])
            out.append((m_new, acc))
        return tuple(out)

    one = (jnp.full((tq, 1), NEG_INF, F32), jnp.zeros((tq, LANES), F32))
    carry = lax.fori_loop(0, i, functools.partial(step, masked=False), (one, one))
    (_, acc0), (_, acc1) = step(i, carry, True)
    o0 = acc0 / acc0[:, HEAD_DIM:HEAD_DIM + 1]
    o1 = acc1 / acc1[:, HEAD_DIM:HEAD_DIM + 1]
    o_ref[...] = jnp.where(lane < HEAD_DIM, o0, pltpu.roll(o1, HEAD_DIM, 1)).astype(o_ref.dtype)


def fox_attention(p2, cum, T, *, tq=FOX_TQ):
    B = p2.shape[0]
    HP = FOX_HEADS // 2
    return pl.pallas_call(
        functools.partial(_fox_kernel, tq=tq),
        grid=(B, HP, T // tq),
        in_specs=[pl.BlockSpec((None, tq, LANES), lambda b, h, i: (b, i, P2_FQ // LANES + h)),
                  pl.BlockSpec((None, T, LANES), lambda b, h, i: (b, 0, P2_FK // LANES + h)),
                  pl.BlockSpec((None, T, LANES), lambda b, h, i: (b, 0, P2_FV // LANES + h)),
                  pl.BlockSpec((None, None, 2, 1, T), lambda b, h, i: (b, h, 0, 0, 0))],
        out_specs=pl.BlockSpec((None, tq, LANES), lambda b, h, i: (b, i, h)),
        out_shape=jax.ShapeDtypeStruct((B, T, FOX_HEADS * HEAD_DIM), BF16),
        scratch_shapes=[pltpu.VMEM((2, T, LANES), BF16), pltpu.VMEM((2, T, LANES), BF16)],
        compiler_params=_cparams("parallel", "parallel", "arbitrary"),
        name="fox_attention",
    )(p2, p2, p2, cum)


def _readout_kernel(ocmp_ref, osel_ref, owin_ref, small_ref, oret_ref, ofox_ref, mg_ref, x_ref, g1_ref,
                    ex_ref, wn_ref, wr_ref, wf_ref, wo_ref, o_ref):
    W = NSA_OUT
    gs = _sigmoid(small_ref[...].astype(F32)).astype(BF16)
    ge = _dot(gs, ex_ref[...])
    onsa = (ge[:, :W] * ocmp_ref[...].astype(F32) + ge[:, W:2 * W] * osel_ref[...].astype(F32)
            + ge[:, 2 * W:] * owin_ref[...].astype(F32))
    D = D_MODEL
    merged = (_sigmoid(mg_ref[:, :D].astype(F32)) * _dot(onsa.astype(BF16), wn_ref[...])
              + _sigmoid(mg_ref[:, D:2 * D].astype(F32)) * _dot(oret_ref[...], wr_ref[...])
              + _sigmoid(mg_ref[:, 2 * D:].astype(F32)) * _dot(ofox_ref[...], wf_ref[...]))
    y = _dot(merged.astype(BF16), wo_ref[...])
    o_ref[...] = x_ref[...] + g1_ref[...] * y


def readout(o_cmp, o_sel, o_win, p2, o_ret, o_fox, x, mod_l, ex, wn, wr, wf, wo, l, T, *, tm=512):
    M, D = x.shape
    per_b = T // tm
    W = NSA_OUT
    row = lambda width, col=0: pl.BlockSpec((tm, width), lambda i: (i, col))
    full = lambda a: pl.BlockSpec(a.shape, lambda i: (0,) * a.ndim)
    return pl.pallas_call(
        _readout_kernel,
        grid=(M // tm,),
        in_specs=[row(W), row(W), row(W), row(LANES, P2_SMALL // LANES), row(512), row(512),
                  row(3 * D, 0), row(D),
                  pl.BlockSpec((None, None, 1, D), lambda i: (i // per_b, 2, 0, 0)),
                  full(ex), _layer_spec(wn, l, 1), _layer_spec(wr, l, 1), _layer_spec(wf, l, 1), _layer_spec(wo, l, 1)],
        out_specs=row(D),
        out_shape=jax.ShapeDtypeStruct((M, D), F32),
        compiler_params=_cparams("parallel"),
        name="mixer_readout",
    )(o_cmp, o_sel, o_win, p2, o_ret, o_fox, p2, x, mod_l, ex, wn, wr, wf, wo)


def nsa_gate_expand():
    ex = np.zeros((LANES, 3 * NSA_OUT), np.float32)
    for br in range(3):
        for h in range(NSA_HEADS):
            c0 = br * NSA_OUT + h * LANES
            ex[br * NSA_HEADS + h, c0:c0 + LANES] = 1.0
    return jnp.asarray(ex, BF16)


def pad_read_nsa(w):
    depth, _, D = w.shape
    w = w.reshape(depth, NSA_HEADS, HEAD_DIM, D)
    z = jnp.zeros_like(w)
    g = (np.arange(NSA_HEADS) // NSA_HPG)[None, :, None, None]
    lo = jnp.where(g == 0, w, z)
    hi = jnp.where(g == 1, w, z)
    return jnp.concatenate([lo, hi], axis=2).reshape(depth, NSA_OUT, D).astype(BF16)


FFN_CHUNK = 512


def _ffn_kernel(x_ref, nw_ref, sc_ref, sh_ref, g2_ref, w1_ref, w3_ref, w2_ref, o_ref):
    x = x_ref[...]
    h = _norm_mod(x, nw_ref[...], sc_ref[...], sh_ref[...]).astype(BF16)
    F = w1_ref.shape[1]
    y = None
    for c0 in range(0, F, FFN_CHUNK):
        cols = slice(c0, min(c0 + FFN_CHUNK, F))
        u = _dot(h, w1_ref[:, cols])
        v = _dot(h, w3_ref[:, cols])
        part = _dot((u * _sigmoid(u) * v).astype(BF16), w2_ref[cols, :])
        y = part if y is None else y + part
    o_ref[...] = x + g2_ref[...] * y


def ffn(x, mod_l, nw, w1, w3, w2, T, *, tm=512):
    M, D = x.shape
    F = w1.shape[1]
    per_b = T // tm
    modspec = lambda k: pl.BlockSpec((None, None, 1, D), lambda i: (i // per_b, k, 0, 0))
    full = lambda a: pl.BlockSpec(a.shape, lambda i: (0,) * a.ndim)
    return pl.pallas_call(
        _ffn_kernel,
        grid=(M // tm,),
        in_specs=[pl.BlockSpec((tm, D), lambda i: (i, 0)),
                  pl.BlockSpec((1, D), lambda i: (0, 0)),
                  modspec(4), modspec(3), modspec(5), full(w1), full(w3), full(w2)],
        out_specs=pl.BlockSpec((tm, D), lambda i: (i, 0)),
        out_shape=jax.ShapeDtypeStruct((M, D), F32),
        compiler_params=_cparams("parallel"),
        name="ffn_dense",
    )(x, nw, mod_l, mod_l, mod_l, w1, w3, w2)


MOE_TC = 512
MOE_TS = 512


def _router_kernel(x_ref, nw_ref, sc_ref, sh_ref, wh_ref, wl_ref, h_ref, gate_ref, rank_ref, cnt_ref, carry_ref):
    @pl.when(pl.program_id(0) == 0)
    def _():
        carry_ref[...] = jnp.zeros_like(carry_ref)

    h = _norm_mod(x_ref[...], nw_ref[...], sc_ref[...], sh_ref[...])
    hh = h.astype(BF16)
    h_ref[...] = hh.astype(h_ref.dtype)
    hl = (h - hh.astype(F32)).astype(BF16)
    logits = _dot(hh, wh_ref[...]) + (_dot(hl, wh_ref[...]) + _dot(hh, wl_ref[...]))
    tm = logits.shape[0]
    lane = lax.broadcasted_iota(jnp.int32, logits.shape, 1)
    logits = jnp.where(lane < N_EXPERTS, logits, REMOVED)
    lane_f = lane.astype(F32)
    v1 = jnp.max(logits, axis=-1, keepdims=True)
    i1 = jnp.min(jnp.where(logits == v1, lane_f, float(LANES)), axis=-1, keepdims=True)
    rest = jnp.where(lane_f == i1, REMOVED, logits)
    v2 = jnp.max(rest, axis=-1, keepdims=True)
    i2 = jnp.min(jnp.where(rest == v2, lane_f, float(LANES)), axis=-1, keepdims=True)
    e2 = jnp.exp(v2 - v1)
    w1 = 1.0 / (1.0 + e2)
    w2 = e2 / (1.0 + e2)
    gate_ref[...] = jnp.where(lane_f == i1, w1, jnp.where(lane_f == i2, w2, 0.0))

    sel = jnp.where((lane_f == i1) | (lane_f == i2), 1.0, 0.0)
    ri = lax.broadcasted_iota(jnp.int32, (tm, tm), 0)
    ci = lax.broadcasted_iota(jnp.int32, (tm, tm), 1)
    before = jnp.where(ci < ri, 1.0, 0.0).astype(BF16)
    rank = _dot(before, sel.astype(BF16)) + carry_ref[0:1, :]
    rank_ref[...] = jnp.where(sel > 0.0, rank, -1.0)
    carry_ref[...] = carry_ref[...] + jnp.sum(sel, axis=0, keepdims=True)
    cnt_ref[...] = carry_ref[...]


def router(x, mod_l, nw, w_router, T):
    M, D = x.shape
    tm = MOE_TC
    per_b = T // tm
    wp = jnp.zeros((D, LANES), F32).at[:, :N_EXPERTS].set(w_router)
    wh = wp.astype(BF16)
    wl = (wp - wh.astype(F32)).astype(BF16)
    return pl.pallas_call(
        _router_kernel,
        grid=(M // tm,),
        in_specs=[pl.BlockSpec((tm, D), lambda i: (i, 0)),
                  pl.BlockSpec((1, D), lambda i: (0, 0))]
        + _mod_specs(T, tm, 4, 3, 1)
        + [pl.BlockSpec((D, LANES), lambda i: (0, 0)),
           pl.BlockSpec((D, LANES), lambda i: (0, 0))],
        out_specs=[pl.BlockSpec((tm, D), lambda i: (i, 0)),
                   pl.BlockSpec((tm, LANES), lambda i: (i, 0)),
                   pl.BlockSpec((tm, LANES), lambda i: (i, 0)),
                   pl.BlockSpec((8, LANES), lambda i: (0, 0))],
        out_shape=[jax.ShapeDtypeStruct((M, D), F32),
                   jax.ShapeDtypeStruct((M, LANES), F32),
                   jax.ShapeDtypeStruct((M, LANES), F32),
                   jax.ShapeDtypeStruct((8, LANES), F32)],
        scratch_shapes=[pltpu.VMEM((8, LANES), F32)],
        compiler_params=_cparams("arbitrary"),
        name="moe_router",
    )(x, nw, mod_l, mod_l, wh, wl)


def _count_le(sorted_vals, x):
    return jnp.sum(sorted_vals[None, :] <= x[:, None], axis=1, dtype=jnp.int32)


def _moe_up_kernel(e_r, total, x_ref, w1_ref, w3_ref, o_ref, w1b_ref, w3b_ref):
    r = pl.program_id(1)
    live = r < total[0]

    @pl.when(live & ((r == 0) | (e_r[r] != e_r[jnp.maximum(r - 1, 0)])))
    def _():
        w1b_ref[...] = w1_ref[...].astype(BF16)
        w3b_ref[...] = w3_ref[...].astype(BF16)

    @pl.when(live)
    def _():
        x = x_ref[...].astype(BF16)
        u = _dot(x, w1b_ref[...])
        v = _dot(x, w3b_ref[...])
        o_ref[...] = (u * _sigmoid(u) * v).astype(o_ref.dtype)


def moe_up(xs, w1, w3, tiles, rt, *, tf=1792):
    R, D = xs.shape
    ts = MOE_TS
    F = w1.shape[-1]
    live = lambda r, total: jnp.minimum(r, total[0] - 1)
    return pl.pallas_call(
        _moe_up_kernel,
        grid_spec=pltpu.PrefetchScalarGridSpec(
            num_scalar_prefetch=2,
            grid=(F // tf, rt),
            in_specs=[pl.BlockSpec((ts, D), lambda n, r, e, total: (live(r, total), 0)),
                      pl.BlockSpec((None, D, tf), lambda n, r, e, total: (e[live(r, total)], 0, n)),
                      pl.BlockSpec((None, D, tf), lambda n, r, e, total: (e[live(r, total)], 0, n))],
            out_specs=pl.BlockSpec((ts, tf), lambda n, r, e, total: (r, n)),
            scratch_shapes=[pltpu.VMEM((D, tf), BF16), pltpu.VMEM((D, tf), BF16)],
        ),
        out_shape=jax.ShapeDtypeStruct((R, F), BF16),
        compiler_params=_cparams("arbitrary", "arbitrary"),
        name="moe_up",
    )(tiles["e"], tiles["total"], xs, w1, w3)


def _moe_down_kernel(e_r, total, a_ref, w2_ref, o_ref, w2b_ref):
    r = pl.program_id(0)
    live = r < total[0]

    @pl.when(live & ((r == 0) | (e_r[r] != e_r[jnp.maximum(r - 1, 0)])))
    def _():
        w2b_ref[...] = w2_ref[...].astype(BF16)

    @pl.when(live)
    def _():
        o_ref[...] = _dot(a_ref[...], w2b_ref[...]).astype(o_ref.dtype)


def moe_down(a, w2, tiles, rt):
    R, F = a.shape
    ts = MOE_TS
    D = w2.shape[-1]
    live = lambda r, total: jnp.minimum(r, total[0] - 1)
    return pl.pallas_call(
        _moe_down_kernel,
        grid_spec=pltpu.PrefetchScalarGridSpec(
            num_scalar_prefetch=2,
            grid=(rt,),
            in_specs=[pl.BlockSpec((ts, F), lambda r, e, total: (live(r, total), 0)),
                      pl.BlockSpec((None, F, D), lambda r, e, total: (e[live(r, total)], 0, 0))],
            out_specs=pl.BlockSpec((ts, D), lambda r, e, total: (r, 0)),
            scratch_shapes=[pltpu.VMEM((F, D), BF16)],
        ),
        out_shape=jax.ShapeDtypeStruct((R, D), F32),
        compiler_params=_cparams("arbitrary"),
        name="moe_down",
    )(tiles["e"], tiles["total"], a, w2)


SC_WINDOW = 32


def _sc_mesh():
    return plsc.VectorSubcoreMesh(core_axis_name="core", subcore_axis_name="subcore")


def sc_scatter_rows2(x, idx_a, idx_b, n_out):
    n, d = x.shape
    steps = n // SC_WINDOW

    @pl.kernel(out_type=jax.ShapeDtypeStruct((n_out, d), x.dtype), mesh=_sc_mesh(), scratch_types=[])
    def kern(x_hbm, ia_hbm, ib_hbm, o_hbm):
        def body(x_vmem, ia_vmem, ib_vmem):
            pltpu.sync_copy(x_vmem, o_hbm.at[ia_vmem.at[0]])
            pltpu.sync_copy(x_vmem, o_hbm.at[ib_vmem.at[0]])

        pltpu.emit_pipeline(
            body,
            grid=(steps,),
            in_specs=[pl.BlockSpec((SC_WINDOW, d), index_map=lambda i: (i, 0)),
                      pl.BlockSpec((1, SC_WINDOW), index_map=lambda i: (i, 0)),
                      pl.BlockSpec((1, SC_WINDOW), index_map=lambda i: (i, 0))],
            out_specs=[],
            core_axis_name=("core", "subcore"),
            dimension_semantics=(pltpu.PARALLEL,),
        )(x_hbm, ia_hbm, ib_hbm)

    return kern(x, idx_a.reshape(steps, SC_WINDOW), idx_b.reshape(steps, SC_WINDOW))


def sc_gather_rows(x, idx):
    n = idx.shape[0]
    d = x.shape[1]
    steps = n // SC_WINDOW

    @pl.kernel(out_type=jax.ShapeDtypeStruct((n, d), x.dtype), mesh=_sc_mesh(), scratch_types=[])
    def kern(x_hbm, i_hbm, o_hbm):
        def body(i_vmem, o_vmem):
            pltpu.sync_copy(x_hbm.at[i_vmem.at[0]], o_vmem)

        pltpu.emit_pipeline(
            body,
            grid=(steps,),
            in_specs=[pl.BlockSpec((1, SC_WINDOW), index_map=lambda i: (i, 0))],
            out_specs=[pl.BlockSpec((SC_WINDOW, d), index_map=lambda i: (i, 0))],
            core_axis_name=("core", "subcore"),
            dimension_semantics=(pltpu.PARALLEL,),
        )(i_hbm, o_hbm)

    return kern(x, idx.reshape(steps, SC_WINDOW))


def _moe_finish_kernel(x_ref, g2_ref, ya_ref, yb_ref, gate_ref, rank_ref, nw_ref, o_ref, *, normalize):
    gate = gate_ref[...]
    chosen = rank_ref[...] >= 0.0
    lane = lax.broadcasted_iota(jnp.int32, gate.shape, 1).astype(F32)
    first = jnp.min(jnp.where(chosen, lane, float(LANES)), axis=-1, keepdims=True)
    last = jnp.max(jnp.where(chosen, lane, -1.0), axis=-1, keepdims=True)
    wa = jnp.sum(jnp.where(lane == first, gate, 0.0), axis=-1, keepdims=True)
    wb = jnp.sum(jnp.where(lane == last, gate, 0.0), axis=-1, keepdims=True)
    x = x_ref[...] + g2_ref[...] * (wa * ya_ref[...] + wb * yb_ref[...])
    if normalize:
        ms = jnp.mean(x * x, axis=-1, keepdims=True)
        x = x * lax.rsqrt(ms + NORM_EPS) * nw_ref[...]
    o_ref[...] = x


def moe_finish(x, mod_l, y2, gate, rank, norm_w, T, *, tm=512):
    M, D = x.shape
    per_b = T // tm
    normalize = norm_w is not None
    if norm_w is None:
        norm_w = jnp.ones((1, D), F32)
    return pl.pallas_call(
        functools.partial(_moe_finish_kernel, normalize=normalize),
        grid=(M // tm,),
        in_specs=[pl.BlockSpec((tm, D), lambda i: (i, 0)),
                  pl.BlockSpec((None, None, 1, D), lambda i: (i // per_b, 5, 0, 0)),
                  pl.BlockSpec((None, tm, D), lambda i: (0, i, 0)),
                  pl.BlockSpec((None, tm, D), lambda i: (1, i, 0)),
                  pl.BlockSpec((tm, LANES), lambda i: (i, 0)),
                  pl.BlockSpec((tm, LANES), lambda i: (i, 0)),
                  pl.BlockSpec((1, D), lambda i: (0, 0))],
        out_specs=pl.BlockSpec((tm, D), lambda i: (i, 0)),
        out_shape=jax.ShapeDtypeStruct((M, D), F32),
        compiler_params=_cparams("parallel"),
        name="moe_finish",
    )(x, mod_l, y2, y2, gate, rank, norm_w)


def moe_ffn(x, mod_l, nw, w_router, w1, w3, w2, T, norm_w=None):
    M = x.shape[0]
    ts = MOE_TS
    rt = (2 * M) // ts + N_EXPERTS
    h, gate, rank, cnt = router(x, mod_l, nw, w_router, T)
    i32 = jnp.int32
    counts = cnt[0, :N_EXPERTS].astype(i32)
    ntile = (counts + ts - 1) // ts
    tile_end = jnp.cumsum(ntile)
    row_off = (tile_end - ntile) * ts
    e_r = jnp.minimum(_count_le(tile_end, jnp.arange(rt, dtype=i32)), N_EXPERTS - 1)
    tiles = dict(e=e_r, total=tile_end[-1].reshape(1).astype(i32))
    rk = rank[:, :N_EXPERTS].astype(i32)
    pos = row_off[None, :] + rk
    pos_a = jnp.min(jnp.where(rk >= 0, pos, rt * ts), axis=1)
    pos_b = jnp.max(jnp.where(rk >= 0, pos, -1), axis=1)

    xs = sc_scatter_rows2(h, pos_a, pos_b, rt * ts)
    a = moe_up(xs, w1, w3, tiles, rt)
    y = moe_down(a, w2, tiles, rt)
    y2 = sc_gather_rows(y, jnp.concatenate([pos_a, pos_b])).reshape(2, M, -1)
    return moe_finish(x, mod_l, y2, gate, rank, norm_w, T)


def _final_norm_kernel(x_ref, w_ref, o_ref):
    x = x_ref[...]
    ms = jnp.mean(x * x, axis=-1, keepdims=True)
    o_ref[...] = x * lax.rsqrt(ms + NORM_EPS) * w_ref[...]


def final_norm(x, w, *, tm=1024):
    M, D = x.shape
    return pl.pallas_call(
        _final_norm_kernel,
        grid=(M // tm,),
        in_specs=[pl.BlockSpec((tm, D), lambda i: (i, 0)), pl.BlockSpec((1, D), lambda i: (0, 0))],
        out_specs=pl.BlockSpec((tm, D), lambda i: (i, 0)),
        out_shape=jax.ShapeDtypeStruct((M, D), F32),
        compiler_params=_cparams("parallel"),
        name="final_norm",
    )(x, w)


def nsa_constants(T):
    n_sel = T // SEL_LEN
    nsp = max(LANES, n_sel)
    ncp = T // CMP_STRIDE
    cmp_start = np.arange(ncp) * CMP_STRIDE
    sel_start = np.arange(nsp) * SEL_LEN
    ov = ((cmp_start[:, None] < sel_start[None, :] + SEL_LEN)
          & (cmp_start[:, None] + CMP_LEN > sel_start[None, :]))
    ov[(T - CMP_LEN) // CMP_STRIDE + 1:] = False
    ov[:, n_sel:] = False
    et_mat = ((np.arange(T)[:, None] // SEL_LEN) == np.arange(nsp)[None, :]) * SEL_BONUS
    return jnp.asarray(ov.T, BF16), jnp.asarray(et_mat, BF16)


def token_mixing(x, mod_l, lw, consts, B, T):
    M = B * T
    cos_t, sin_t, ov_t, e_mat, ret_consts, ex = consts
    l = lw["layer"]
    p1 = proj_rope(x, mod_l, lw["norm_mix"], lw["w1"], l, cos_t, sin_t, p1_scales(), T).reshape(B, T, P1_COLS)
    p2 = proj_plain(x, mod_l, lw["norm_mix"], lw["w2"], l, T).reshape(B, T, P2_COLS)

    def group_rows(a):
        return a.reshape(B, T, NSA_GROUPS, HEAD_DIM).transpose(0, 2, 1, 3).reshape(
            B, NSA_GROUPS, T // CMP_STRIDE, CMP_STRIDE * HEAD_DIM)

    xr = jnp.stack([group_rows(p1[:, :, P1_NKC:P1_NKC + LANES]), group_rows(p2[:, :, P2_NVC:P2_NVC + LANES])])
    cmp_out = compress(xr, lw["cmp_pe"], lw["cmp_w1"], lw["cmp_w2"])
    cmp_out = cmp_out.transpose(0, 1, 3, 2, 4).reshape(2, B, T // CMP_STRIDE, LANES)
    o_cmp, sel = nsa_cmp_select(p1, cmp_out[0], cmp_out[1], ov_t, T)
    o_sel = nsa_selected(p1, nsa_value_augment(p2[:, :, P2_NVS:P2_NVS + LANES]), sel, e_mat, T)
    o_win = nsa_window(p1, p2, T)

    o_ret = retention(p1, p2, ret_consts, T)

    ff = p2[:, :, P2_SMALL + 3 * NSA_HEADS:P2_SMALL + 3 * NSA_HEADS + FOX_HEADS].astype(F32)
    ff = ff.transpose(0, 2, 1).reshape(B, FOX_HEADS, T // LANES, LANES)
    cum = fox_cum(ff, lw["fox_bias"]).reshape(B, FOX_HEADS // 2, 2, 1, T)
    o_fox = fox_attention(p2, cum, T)

    return readout(o_cmp.reshape(M, -1), o_sel.reshape(M, -1), o_win.reshape(M, -1), p2.reshape(M, P2_COLS),
                   o_ret.reshape(M, -1), o_fox.reshape(M, -1), x, mod_l, ex,
                   lw["wn"], lw["wr"], lw["wf"], lw["wo"], l, T)


def mixer_weights(norm_mix, w_in, cmp_k_pe, cmp_k_w1, cmp_k_w2, cmp_v_pe, cmp_v_w1, cmp_v_w2, fox_f_bias,
                  w_read_nsa, w_read_ret, w_read_fox, w_out):
    depth = w_in.shape[0]
    w1, w2 = split_w_in(w_in)
    pe = jnp.stack([cmp_k_pe.reshape(depth, 1, -1), cmp_v_pe.reshape(depth, 1, -1)], axis=1)
    pe = jnp.broadcast_to(pe, (depth, 2, 8, pe.shape[-1])).astype(BF16)
    shared = {
        "w1": w1, "w2": w2,
        "wn": pad_read_nsa(w_read_nsa),
        "wr": w_read_ret.astype(BF16),
        "wf": w_read_fox.astype(BF16),
        "wo": w_out.astype(BF16),
    }
    cmp_w1 = jnp.stack([cmp_k_w1, cmp_v_w1], axis=1).astype(BF16)
    cmp_w2 = jnp.stack([cmp_k_w2, cmp_v_w2], axis=1).astype(BF16)
    return [dict(shared, layer=l, norm_mix=norm_mix[l].reshape(1, -1), cmp_pe=pe[l], cmp_w1=cmp_w1[l], cmp_w2=cmp_w2[l],
                 fox_bias=jnp.broadcast_to(fox_f_bias[l][:, None, None], (FOX_HEADS, 1, LANES)))
            for l in range(depth)]


def kernel(x, c, ada_w, ada_b, norm_mix, norm_ffn, w_in, cmp_k_pe, cmp_k_w1, cmp_k_w2, cmp_v_pe, cmp_v_w1,
           cmp_v_w2, fox_f_bias, w_read_nsa, w_read_ret, w_read_fox, w_out, ffn_w1, ffn_w3, ffn_w2, router_w,
           moe_w1, moe_w3, moe_w2, final_norm_w):
    B, T, D = x.shape
    M = B * T
    depth = ada_w.shape[0]
    mod = modulation(c, ada_w, ada_b)
    cos_t, sin_t = rope_tables(T)
    ov_t, e_mat = nsa_constants(T)
    consts = (cos_t, sin_t, ov_t, e_mat, retention_consts(), nsa_gate_expand())
    xs = x.reshape(M, D)
    lws = mixer_weights(norm_mix, w_in, cmp_k_pe, cmp_k_w1, cmp_k_w2, cmp_v_pe, cmp_v_w1, cmp_v_w2,
                        fox_f_bias, w_read_nsa, w_read_ret, w_read_fox, w_out)
    for l in range(depth):
        xs = token_mixing(xs, mod[l], lws[l], consts, B, T)
        nf = norm_ffn[l].reshape(1, D)
        if l % 2 == 0:
            k = l // 2
            xs = ffn(xs, mod[l], nf, ffn_w1[k].astype(BF16), ffn_w3[k].astype(BF16), ffn_w2[k].astype(BF16), T)
        else:
            k = l // 2
            fuse = final_norm_w.reshape(1, D) if l == depth - 1 else None
            xs = moe_ffn(xs, mod[l], nf, router_w[k], moe_w1[k], moe_w3[k], moe_w2[k], T, fuse)
    if depth % 2 == 1:
        xs = final_norm(xs, final_norm_w.reshape(1, D))
    return xs.reshape(B, T, D)
```

```python
import functools
import math

import jax
import jax.numpy as jnp
import numpy as np
from jax import lax
from jax.experimental import pallas as pl
from jax.experimental.pallas import tpu as pltpu
from jax.experimental.pallas import tpu_sc as plsc

F32 = jnp.float32
BF16 = jnp.bfloat16

D_MODEL = 1024
DEPTH = 2
HEAD_DIM = 64
ROPE_THETA = 10000.0
NORM_EPS = 1e-6
NEG_INF = -1e30
REMOVED = -3e38

NSA_HEADS = 8
NSA_GROUPS = 2
NSA_HPG = NSA_HEADS // NSA_GROUPS
CMP_LEN = 32
CMP_STRIDE = 16
CMP_HIDDEN = 256
SEL_LEN = 64
SEL_TOPN = 16
WINDOW = 512
FORCE_SCORE = 1e4
NSA_QBLOCK = 128

RET_HEADS = 4
RET_QK_DIM = 64
RET_V_DIM = 128
RET_CHUNK = 128

FOX_HEADS = 8
FOX_TQ = 1024
LOG2E = 1.4426950408889634

D_FF = 2816
N_EXPERTS = 8
D_FF_EXPERT = 3584

LANES = 128
VMEM_LIMIT = 56 * 1024 * 1024

P1_NQ = 0
P1_RQ = 512
P1_RK = 768
P1_NKC = 1024
P1_NKS = 1152
P1_NKW = 1280
P1_COLS = 1408
P2_MG = 0
P2_RV = 3072
P2_RG = 3584
P2_FQ = 4096
P2_FK = 4608
P2_FV = 5120
P2_NVC = 5632
P2_NVS = 5760
P2_NVW = 5888
P2_SMALL = 6016
P2_COLS = 6144
NSA_OUT = NSA_HEADS * LANES


def _layer_spec(w, l, nidx):
    zeros = (0,) * (w.ndim - 1)
    return pl.BlockSpec((None,) + w.shape[1:], lambda *_: (l,) + zeros)


def _cparams(*sem):
    return pltpu.CompilerParams(dimension_semantics=tuple(sem), vmem_limit_bytes=VMEM_LIMIT)


def _sigmoid(x):
    return 1.0 / (1.0 + jnp.exp(-x))


def _dot(a, b):
    return jnp.dot(a, b, preferred_element_type=F32)


def _dot_nt(a, b):
    return lax.dot_general(a, b, (((1,), (1,)), ((), ())), preferred_element_type=F32)


def _dot_tn(a, b):
    return lax.dot_general(a, b, (((0,), (0,)), ((), ())), preferred_element_type=F32)


def _split3(x):
    hi = x.astype(BF16)
    r1 = x - hi.astype(F32)
    mid = r1.astype(BF16)
    lo = (r1 - mid.astype(F32)).astype(BF16)
    return hi, mid, lo


def _pack_bf16_pairs(x):
    c = x.shape[1] // 2
    lo = pltpu.bitcast(x[:, :c].astype(BF16).astype(F32), jnp.uint32) >> 16
    hi = pltpu.bitcast(x[:, c:].astype(BF16).astype(F32), jnp.uint32) & jnp.uint32(0xFFFF0000)
    return hi | lo


def _unpack_bf16_pairs(u):
    lo = pltpu.bitcast(u << 16, F32)
    hi = pltpu.bitcast(u & jnp.uint32(0xFFFF0000), F32)
    return jnp.concatenate([lo, hi], axis=1)


def _norm_mod(x, nw, sc, sh):
    ms = jnp.mean(x * x, axis=-1, keepdims=True)
    y = x * lax.rsqrt(ms + NORM_EPS) * nw
    return y * (1.0 + sc) + sh


def _mod_kernel(c_ref, w_ref, b_ref, o_ref):
    c = c_ref[...]
    s = c * _sigmoid(c)
    o_ref[0] = _dot(s.astype(BF16), w_ref[0].astype(BF16)) + b_ref[0]


def modulation(c, ada_w, ada_b):
    B, D = c.shape
    depth = ada_w.shape[0]
    rows = 8
    c_pad = jnp.zeros((rows, D), F32).at[:B].set(c)
    out = pl.pallas_call(
        _mod_kernel,
        grid=(depth, 6),
        in_specs=[pl.BlockSpec((rows, D), lambda l, j: (0, 0)),
                  pl.BlockSpec((1, D, D), lambda l, j: (l, 0, j)),
                  pl.BlockSpec((1, 1, D), lambda l, j: (l, 0, j))],
        out_specs=pl.BlockSpec((1, rows, D), lambda l, j: (l, 0, j)),
        out_shape=jax.ShapeDtypeStruct((depth, rows, 6 * D), F32),
        compiler_params=_cparams("parallel", "parallel"),
        name="modulation",
    )(c_pad, ada_w, ada_b.reshape(depth, 1, 6 * D))
    return out[:, :B].reshape(depth, B, 6, 1, D)


def _proj_plain_kernel(x_ref, nw_ref, sc_ref, sh_ref, w_ref, o_ref, *, tn):
    h = _norm_mod(x_ref[...], nw_ref[...], sc_ref[...], sh_ref[...]).astype(BF16)
    for n in range(w_ref.shape[1] // tn):
        cols = slice(n * tn, (n + 1) * tn)
        o_ref[:, cols] = _dot(h, w_ref[:, cols]).astype(o_ref.dtype)


def _proj_rope_kernel(x_ref, nw_ref, sc_ref, sh_ref, w_ref, cos_ref, sin_ref, o_ref, *, scales):
    h = _norm_mod(x_ref[...], nw_ref[...], sc_ref[...], sh_ref[...]).astype(BF16)
    y = _dot(h, w_ref[...])
    cos = cos_ref[...]
    sin = sin_ref[...]
    lane = lax.broadcasted_iota(jnp.int32, cos.shape, 1)
    first_half = (lane % HEAD_DIM) < (HEAD_DIM // 2)
    for g, scale in enumerate(scales):
        yg = y[:, g * LANES:(g + 1) * LANES]
        rot = jnp.where(first_half, pltpu.roll(yg, LANES - HEAD_DIM // 2, 1),
                        pltpu.roll(yg, HEAD_DIM // 2, 1))
        r = yg * cos + rot * sin
        if scale != 1.0:
            r = r * scale
        o_ref[:, g * LANES:(g + 1) * LANES] = r.astype(o_ref.dtype)


def _mod_specs(T, tm, sc_idx, sh_idx, nargs):
    per_b = T // tm
    if nargs == 1:
        return [pl.BlockSpec((None, None, 1, D_MODEL), lambda i: (i // per_b, sc_idx, 0, 0)),
                pl.BlockSpec((None, None, 1, D_MODEL), lambda i: (i // per_b, sh_idx, 0, 0))]
    return [pl.BlockSpec((None, None, 1, D_MODEL), lambda i, j: (i // per_b, sc_idx, 0, 0)),
            pl.BlockSpec((None, None, 1, D_MODEL), lambda i, j: (i // per_b, sh_idx, 0, 0))]


def proj_plain(x, mod_l, nw, w, l, T, *, tm=512, tn=512):
    M, D = x.shape
    N = w.shape[-1]
    return pl.pallas_call(
        functools.partial(_proj_plain_kernel, tn=tn),
        grid=(M // tm,),
        in_specs=[pl.BlockSpec((tm, D), lambda i: (i, 0)),
                  pl.BlockSpec((1, D), lambda i: (0, 0))]
        + _mod_specs(T, tm, 1, 0, 1)
        + [_layer_spec(w, l, 1)],
        out_specs=pl.BlockSpec((tm, N), lambda i: (i, 0)),
        out_shape=jax.ShapeDtypeStruct((M, N), BF16),
        compiler_params=_cparams("parallel"),
        name="proj_plain",
    )(x, nw, mod_l, mod_l, w)


def proj_rope(x, mod_l, nw, w, l, cos, sin, scales, T, *, tm=512):
    M, D = x.shape
    N = w.shape[-1]
    per_b = T // tm
    return pl.pallas_call(
        functools.partial(_proj_rope_kernel, scales=scales),
        grid=(M // tm,),
        in_specs=[pl.BlockSpec((tm, D), lambda i: (i, 0)),
                  pl.BlockSpec((1, D), lambda i: (0, 0))]
        + _mod_specs(T, tm, 1, 0, 1)
        + [_layer_spec(w, l, 1),
           pl.BlockSpec((tm, LANES), lambda i: (i % per_b, 0)),
           pl.BlockSpec((tm, LANES), lambda i: (i % per_b, 0))],
        out_specs=pl.BlockSpec((tm, N), lambda i: (i, 0)),
        out_shape=jax.ShapeDtypeStruct((M, N), BF16),
        compiler_params=_cparams("parallel"),
        name="proj_rope",
    )(x, nw, mod_l, mod_l, w, cos, sin)


def rope_tables(T):
    d = HEAD_DIM
    pos = jnp.arange(T, dtype=F32)
    inv = ROPE_THETA ** (-jnp.arange(0, d, 2, dtype=F32) / d)
    ang = pos[:, None] * inv[None, :]
    cos = jnp.cos(ang)
    sin = jnp.sin(ang)
    cos_t = jnp.concatenate([cos, cos, cos, cos], axis=-1)
    sin_t = jnp.concatenate([-sin, sin, -sin, sin], axis=-1)
    return cos_t, sin_t


def split_w_in(w_in):
    sizes = [512, 128, 128, 128, 128, 128, 128, 24, 256, 256, 512, 512, 512, 512, 512, 8, 3072]
    offs = np.cumsum([0] + sizes)
    wb = w_in.astype(BF16)
    (nq, nkc, nvc, nks, nvs, nkw, nvw, ngate, rq, rk, rv, rg, fq, fk, fv, ff, mg) = [
        wb[..., offs[i]:offs[i + 1]] for i in range(len(sizes))]
    small = jnp.concatenate([ngate, ff, jnp.zeros(ngate.shape[:-1] + (LANES - 32,), BF16)], axis=-1)
    w1 = jnp.concatenate([nq, rq, rk, nkc, nks, nkw], axis=-1)
    w2 = jnp.concatenate([mg, rv, rg, fq, fk, fv, nvc, nvs, nvw, small], axis=-1)
    assert w1.shape[-1] == P1_COLS and w2.shape[-1] == P2_COLS
    return w1, w2


def p1_scales():
    s = [1.0] * (P1_COLS // LANES)
    for g in range(P1_NQ // LANES, P1_RQ // LANES):
        s[g] = HEAD_DIM ** -0.5 * LOG2E
    for g in range(P1_RK // LANES, P1_NKC // LANES):
        s[g] = RET_QK_DIM ** -0.5
    return tuple(s)


def _compress_kernel(x_ref, pe_ref, w1_ref, w2_ref, o_ref):
    r = x_ref[...]
    half = r.shape[1]
    w1 = w1_ref[...]
    a = _dot(r, w1[:half])
    b = _dot(r, w1[half:])
    pe = _dot(pe_ref[...], w1)[0:1]
    n = a.shape[0]
    hid = a + pltpu.roll(b, n - 1, 0) + pe
    hid = hid * _sigmoid(hid)
    o_ref[...] = _dot(hid.astype(BF16), w2_ref[...]).astype(o_ref.dtype)


def compress(xr, pe, w1, w2):
    _, B, G, R, W = xr.shape
    H = w1.shape[-1]
    return pl.pallas_call(
        _compress_kernel,
        grid=(2, B, G),
        in_specs=[pl.BlockSpec((None, None, None, R, W), lambda s, b, g: (s, b, g, 0, 0)),
                  pl.BlockSpec((None, 8, 2 * W), lambda s, b, g: (s, 0, 0)),
                  pl.BlockSpec((None, 2 * W, H), lambda s, b, g: (s, 0, 0)),
                  pl.BlockSpec((None, H, HEAD_DIM), lambda s, b, g: (s, 0, 0))],
        out_specs=pl.BlockSpec((None, None, None, R, HEAD_DIM), lambda s, b, g: (s, b, g, 0, 0)),
        out_shape=jax.ShapeDtypeStruct((2, B, G, R, HEAD_DIM), BF16),
        compiler_params=_cparams("parallel", "parallel", "parallel"),
        name="nsa_compress",
    )(xr, pe, w1, w2)


def _stack_heads(q_ref, g):
    tq = q_ref.shape[0]
    half = lax.broadcasted_iota(jnp.int32, (tq, LANES), 1) // HEAD_DIM
    rows = []
    for hh in range(NSA_HPG):
        h = NSA_HPG * g + hh
        x = q_ref[:, (h // 2) * LANES:(h // 2 + 1) * LANES].astype(F32)
        if h % 2 != g:
            x = pltpu.roll(x, HEAD_DIM, 1)
        rows.append(jnp.where(half == g, x, 0.0).astype(BF16))
    return jnp.concatenate(rows, axis=0)


def _store_heads(o_ref, g, o, tq):
    for hh in range(NSA_HPG):
        h = NSA_HPG * g + hh
        o_ref[:, h * LANES:(h + 1) * LANES] = o[hh * tq:(hh + 1) * tq].astype(o_ref.dtype)


CMP_CHUNK = 128


def _nsa_cmp_kernel(q_ref, kc_ref, vc_ref, ov_ref, o_ref, m_ref, imp_ref, *, tq, n_sel, top_n):
    t0 = pl.program_id(1) * tq
    ncp = kc_ref.shape[0]
    nsp = ov_ref.shape[0]
    rows = NSA_HPG * tq

    def attend(ncols):
        kc = kc_ref[0:ncols, :]
        vc = vc_ref[0:ncols, :]
        n_idx = lax.broadcasted_iota(jnp.int32, (rows, ncols), 1)
        t_idx = t0 + lax.broadcasted_iota(jnp.int32, (rows, ncols), 0) % tq
        valid = (n_idx * CMP_STRIDE + (CMP_LEN - 1)) <= t_idx
        for g in range(NSA_GROUPS):
            q = _stack_heads(q_ref, g)
            s = jnp.where(valid, _dot_nt(q, kc), NEG_INF)
            m = jnp.max(s, axis=-1, keepdims=True)
            e = jnp.exp2(s - m)
            l = jnp.sum(e, axis=-1, keepdims=True)
            p = e * jnp.where(m > 0.5 * NEG_INF, 1.0 / l, 0.0)
            _store_heads(o_ref, g, _dot(p.astype(BF16), vc), tq)
            psum = p[0:tq]
            for hh in range(1, NSA_HPG):
                psum = psum + p[hh * tq:(hh + 1) * tq]
            imp_ref[g] = _dot_nt(ov_ref[:, 0:ncols], psum.astype(BF16))

    n_live = jnp.maximum((t0 + tq - CMP_LEN) // CMP_STRIDE + 1, 1)
    n_chunks = jnp.minimum((n_live + CMP_CHUNK - 1) // CMP_CHUNK, ncp // CMP_CHUNK)
    for nc in range(1, ncp // CMP_CHUNK + 1):
        pl.when(n_chunks == nc)(functools.partial(attend, nc * CMP_CHUNK))

    j_idx = lax.broadcasted_iota(jnp.int32, (nsp, tq), 0)
    cur = (t0 + lax.broadcasted_iota(jnp.int32, (nsp, tq), 1)) // SEL_LEN
    forced = (j_idx == 0) | (j_idx == cur) | (j_idx == cur - 1)
    j_f = j_idx.astype(F32)
    for g in range(NSA_GROUPS):
        score = jnp.where(j_idx <= cur, imp_ref[g], NEG_INF)
        score = jnp.where(forced | (j_idx >= n_sel), REMOVED, score)
        sel = jnp.where(forced, 1.0, 0.0)
        for _ in range(max(top_n - 3, 0)):
            mx = jnp.max(score, axis=0, keepdims=True)
            idx = jnp.min(jnp.where(score == mx, j_f, float(nsp)), axis=0, keepdims=True)
            hit = j_f == idx
            sel = jnp.where(hit, 1.0, sel)
            score = jnp.where(hit, REMOVED, score)
        sel = jnp.where(j_idx <= cur, sel, 0.0)
        m_ref[g] = sel.T.astype(m_ref.dtype)


def nsa_cmp_select(p1, kc, vc, ov_t, T):
    B = p1.shape[0]
    tq = NSA_QBLOCK
    ncp = kc.shape[1]
    nsp = ov_t.shape[0]
    n_sel = T // SEL_LEN
    return pl.pallas_call(
        functools.partial(_nsa_cmp_kernel, tq=tq, n_sel=n_sel, top_n=min(SEL_TOPN, n_sel)),
        grid=(B, T // tq),
        in_specs=[pl.BlockSpec((None, tq, NSA_HEADS * HEAD_DIM), lambda b, i: (b, i, 0)),
                  pl.BlockSpec((None, ncp, LANES), lambda b, i: (b, 0, 0)),
                  pl.BlockSpec((None, ncp, LANES), lambda b, i: (b, 0, 0)),
                  pl.BlockSpec((nsp, ncp), lambda b, i: (0, 0))],
        out_specs=[pl.BlockSpec((None, tq, NSA_OUT), lambda b, i: (b, i, 0)),
                   pl.BlockSpec((None, NSA_GROUPS, tq, nsp), lambda b, i: (b, 0, i, 0))],
        out_shape=[jax.ShapeDtypeStruct((B, T, NSA_OUT), BF16),
                   jax.ShapeDtypeStruct((B, NSA_GROUPS, T, nsp), BF16)],
        scratch_shapes=[pltpu.VMEM((NSA_GROUPS, nsp, tq), F32)],
        compiler_params=_cparams("parallel", "parallel"),
        name="nsa_cmp_select",
    )(p1, kc, vc, ov_t)


SEL_BONUS = 8192.0
NSA_SEL_TQ = 256
NSA_SEL_TK = 1024


def _nsa_sel_kernel(q_ref, k_ref, v_ref, m_ref, et_ref, o_ref, *, tq, tk):
    t0 = pl.program_id(1) * tq
    n_full = t0 // tk
    rows = NSA_HPG * tq

    def update(carry, q, ks, vs, mask=None):
        m, acc = carry
        s = _dot_nt(q, ks)
        if mask is not None:
            s = jnp.where(mask, s, NEG_INF)
        m_new = jnp.maximum(m, jnp.max(s, axis=-1, keepdims=True))
        p = jnp.exp2(s - m_new)
        return m_new, jnp.exp2(m - m_new) * acc + _dot(p.astype(BF16), vs)

    qs, carries = [], []
    for g in range(NSA_GROUPS):
        q = jnp.concatenate([_stack_heads(q_ref, g), jnp.concatenate([m_ref[g]] * NSA_HPG, axis=0)], axis=1)

        def step(j, carry, q=q, g=g):
            start = pl.multiple_of(j * tk, tk)
            ks = jnp.concatenate([k_ref[pl.ds(start, tk), :], et_ref[pl.ds(start, tk), :]], axis=1)
            return update(carry, q, ks, v_ref[g, pl.ds(start, tk), :])

        init = (jnp.full((rows, 1), NEG_INF, F32), jnp.zeros((rows, LANES), F32))
        qs.append(q)
        carries.append(lax.fori_loop(0, n_full, step, init))

    start = pl.multiple_of(n_full * tk, tk)

    def tail(nk):
        trow = t0 + lax.broadcasted_iota(jnp.int32, (rows, nk), 0) % tq
        causal = start + lax.broadcasted_iota(jnp.int32, (rows, nk), 1) <= trow
        ks = jnp.concatenate([k_ref[pl.ds(start, nk), :], et_ref[pl.ds(start, nk), :]], axis=1)
        for g in range(NSA_GROUPS):
            _, acc = update(carries[g], qs[g], ks, v_ref[g, pl.ds(start, nk), :], causal)
            den = HEAD_DIM * (1 - g)
            _store_heads(o_ref, g, acc / acc[:, den:den + 1], tq)

    which = (t0 - start) // tq
    for v in range(tk // tq):
        pl.when(which == v)(functools.partial(tail, (v + 1) * tq))


def nsa_value_augment(v):
    ones = jnp.ones_like(v[..., :HEAD_DIM])
    return jnp.stack([jnp.concatenate([v[..., :HEAD_DIM], ones], axis=-1),
                      jnp.concatenate([ones, v[..., HEAD_DIM:]], axis=-1)], axis=1)


def nsa_selected(p1, v_aug, sel, et_mat, T, *, tq=NSA_SEL_TQ, tk=NSA_SEL_TK):
    B = p1.shape[0]
    nsp = sel.shape[-1]
    return pl.pallas_call(
        functools.partial(_nsa_sel_kernel, tq=tq, tk=tk),
        grid=(B, T // tq),
        in_specs=[pl.BlockSpec((None, tq, NSA_HEADS * HEAD_DIM), lambda b, i: (b, i, 0)),
                  pl.BlockSpec((None, T, LANES), lambda b, i: (b, 0, P1_NKS // LANES)),
                  pl.BlockSpec((None, NSA_GROUPS, T, LANES), lambda b, i: (b, 0, 0, 0)),
                  pl.BlockSpec((None, NSA_GROUPS, tq, nsp), lambda b, i: (b, 0, i, 0)),
                  pl.BlockSpec((T, nsp), lambda b, i: (0, 0))],
        out_specs=pl.BlockSpec((None, tq, NSA_OUT), lambda b, i: (b, i, 0)),
        out_shape=jax.ShapeDtypeStruct((B, T, NSA_OUT), BF16),
        compiler_params=_cparams("parallel", "parallel"),
        name="nsa_selected",
    )(p1, p1, v_aug, sel, et_mat)


def _nsa_win_kernel(q_ref, k_ref, v_ref, b_ref, o_ref, *, tq):
    t0 = pl.program_id(1) * tq
    span = WINDOW + tq
    start = pl.multiple_of(jnp.maximum(t0 - WINDOW, 0), tq)
    ks = k_ref[pl.ds(start, span), :]
    vs = v_ref[pl.ds(start, span), :]

    def run(bias):
        bias = jnp.concatenate([bias] * NSA_HPG, axis=0)
        for g in range(NSA_GROUPS):
            s = _dot_nt(_stack_heads(q_ref, g), ks) + bias
            m = jnp.max(s, axis=-1, keepdims=True)
            p = jnp.exp2(s - m)
            l = jnp.sum(p, axis=-1, keepdims=True)
            _store_heads(o_ref, g, _dot(p.astype(BF16), vs) / l, tq)

    @pl.when(t0 >= WINDOW)
    def _():
        run(b_ref[...])

    @pl.when(t0 < WINDOW)
    def _():
        row = lax.broadcasted_iota(jnp.int32, (tq, span), 0)
        col = lax.broadcasted_iota(jnp.int32, (tq, span), 1)
        run(jnp.where(col <= t0 + row, 0.0, NEG_INF))


def nsa_window(p1, p2, T):
    B = p1.shape[0]
    tq = NSA_QBLOCK
    span = WINDOW + tq
    r = np.arange(tq)[:, None]
    c = np.arange(span)[None, :]
    band = jnp.asarray(np.where((c > r) & (c <= r + WINDOW), 0.0, NEG_INF), F32)
    return pl.pallas_call(
        functools.partial(_nsa_win_kernel, tq=tq),
        grid=(B, T // tq),
        in_specs=[pl.BlockSpec((None, tq, NSA_HEADS * HEAD_DIM), lambda b, i: (b, i, 0)),
                  pl.BlockSpec((None, T, LANES), lambda b, i: (b, 0, P1_NKW // LANES)),
                  pl.BlockSpec((None, T, LANES), lambda b, i: (b, 0, P2_NVW // LANES)),
                  pl.BlockSpec((tq, span), lambda b, i: (0, 0))],
        out_specs=pl.BlockSpec((None, tq, NSA_OUT), lambda b, i: (b, i, 0)),
        out_shape=jax.ShapeDtypeStruct((B, T, NSA_OUT), BF16),
        compiler_params=_cparams("parallel", "parallel"),
        name="nsa_window",
    )(p1, p1, p2, band)


def _retention_kernel(q_ref, k_ref, v_ref, g_ref, din_ref, qd_ref, kd_ref, cd_ref, o_ref, st_ref):
    @pl.when(pl.program_id(0) == 0)
    def _():
        st_ref[...] = jnp.zeros_like(st_ref)

    B = q_ref.shape[0]
    half = lax.broadcasted_iota(jnp.int32, (q_ref.shape[1], LANES), 1) // HEAD_DIM
    for b in range(B):
        for h in range(RET_HEADS):
            lanes = slice(h * LANES, (h + 1) * LANES)
            pair = slice((h // 2) * LANES, (h // 2 + 1) * LANES)
            qh = jnp.where(half == h % 2, q_ref[b, :, pair], 0.0).astype(BF16)
            kp = k_ref[b, :, pair]
            vh = v_ref[b, :, lanes]
            st = st_ref[b, h]
            inner = _dot_nt(qh, kp) * din_ref[h]
            o = _dot(inner.astype(BF16), vh) + _dot(qh, st.astype(BF16)) * qd_ref[h]
            kd = (kp.astype(F32) * kd_ref[h]).astype(BF16)
            st_ref[b, h] = st * cd_ref[h, 0:1, :] + _dot_tn(kd, vh)
            mu = jnp.mean(o, axis=-1, keepdims=True)
            d = o - mu
            var = jnp.mean(d * d, axis=-1, keepdims=True)
            on = d * lax.rsqrt(var + NORM_EPS)
            gh = g_ref[b, :, lanes].astype(F32)
            o_ref[b, :, lanes] = (gh * _sigmoid(gh) * on).astype(o_ref.dtype)


def retention_consts():
    C = RET_CHUNK
    H = RET_HEADS
    log_g = jnp.log(1.0 - 2.0 ** (-5.0 - jnp.arange(H, dtype=F32)))
    n = jnp.arange(C, dtype=F32)
    diff = n[:, None] - n[None, :]
    causal = diff >= 0
    decay_in = jnp.where(causal[None], jnp.exp(jnp.where(causal, diff, 0.0)[None] * log_g[:, None, None]), 0.0)
    q_decay = jnp.exp((n[None, :] + 1.0) * log_g[:, None])
    k_decay = jnp.exp((C - 1.0 - n)[None, :] * log_g[:, None])
    chunk_decay = jnp.exp(C * log_g)
    qd = jnp.broadcast_to(q_decay[:, :, None], (H, C, LANES))
    kd = jnp.broadcast_to(k_decay[:, :, None], (H, C, LANES))
    cd = jnp.broadcast_to(chunk_decay[:, None, None], (H, 8, LANES))
    return decay_in, qd, kd, cd


def retention(p1, p2, consts, T):
    B = p1.shape[0]
    C = RET_CHUNK
    din, qd, kd, cd = consts
    W = RET_HEADS * LANES
    full = lambda shape: pl.BlockSpec(shape, lambda c: (0,) * len(shape))
    return pl.pallas_call(
        _retention_kernel,
        grid=(T // C,),
        in_specs=[pl.BlockSpec((B, C, W // 2), lambda c: (0, c, P1_RQ // (W // 2))),
                  pl.BlockSpec((B, C, W // 2), lambda c: (0, c, P1_RK // (W // 2))),
                  pl.BlockSpec((B, C, W), lambda c: (0, c, P2_RV // W)),
                  pl.BlockSpec((B, C, W), lambda c: (0, c, P2_RG // W)),
                  full(din.shape), full(qd.shape), full(kd.shape), full(cd.shape)],
        out_specs=pl.BlockSpec((B, C, W), lambda c: (0, c, 0)),
        out_shape=jax.ShapeDtypeStruct((B, T, W), BF16),
        scratch_shapes=[pltpu.VMEM((B, RET_HEADS, LANES, LANES), F32)],
        compiler_params=_cparams("arbitrary"),
        name="retention",
    )(p1, p1, p2, p2, din, qd, kd, cd)


def _fox_cum_kernel(f_ref, b_ref, o_ref):
    x = f_ref[...] + b_ref[...]
    ls = jnp.minimum(x, 0.0) - jnp.log1p(jnp.exp(-jnp.abs(x)))
    R = x.shape[0]
    ki = lax.broadcasted_iota(jnp.int32, (LANES, LANES), 0)
    ji = lax.broadcasted_iota(jnp.int32, (LANES, LANES), 1)
    upper = jnp.where(ki <= ji, 1.0, 0.0).astype(BF16)
    hi, mid, lo = _split3(ls)
    rowcum = _dot(hi, upper) + _dot(mid, upper) + _dot(lo, upper)
    tot = jnp.broadcast_to(rowcum[:, LANES - 1:LANES], (R, LANES))
    ri = lax.broadcasted_iota(jnp.int32, (R, R), 0)
    ci = lax.broadcasted_iota(jnp.int32, (R, R), 1)
    lower = jnp.where(ci < ri, 1.0, 0.0).astype(BF16)
    hi, mid, lo = _split3(tot)
    offs = _dot(lower, hi) + _dot(lower, mid) + _dot(lower, lo)
    o_ref[...] = (rowcum + offs) * LOG2E


def fox_cum(f_logit, bias):
    B, H, R, _ = f_logit.shape
    return pl.pallas_call(
        _fox_cum_kernel,
        grid=(B, H),
        in_specs=[pl.BlockSpec((None, None, R, LANES), lambda b, h: (b, h, 0, 0)),
                  pl.BlockSpec((None, 1, LANES), lambda b, h: (h, 0, 0))],
        out_specs=pl.BlockSpec((None, None, R, LANES), lambda b, h: (b, h, 0, 0)),
        out_shape=jax.ShapeDtypeStruct((B, H, R, LANES), F32),
        compiler_params=_cparams("parallel", "parallel"),
        name="fox_cum",
    )(f_logit, bias)


FOX_BIAS_LANES = 3


def _fox_kernel(q_ref, k_ref, v_ref, c_ref, o_ref, ka_ref, va_ref, *, tq):
    i = pl.program_id(2)
    tk = tq
    T = k_ref.shape[0]
    chunk = 512

    @pl.when(i == 0)
    def _():
        lane = lax.broadcasted_iota(jnp.int32, (chunk, LANES), 1)
        ri = lax.broadcasted_iota(jnp.int32, (16, LANES), 0)
        ci = lax.broadcasted_iota(jnp.int32, (16, LANES), 1)
        place = jnp.where((ci == ri + HEAD_DIM) & (ri < FOX_BIAS_LANES), 1.0, 0.0).astype(BF16)

        def build(c, _):
            c0 = pl.multiple_of(c * chunk, chunk)
            kp = k_ref[pl.ds(c0, chunk), :].astype(F32)
            vp = v_ref[pl.ds(c0, chunk), :].astype(F32)
            for hh in range(2):
                hi, mid, lo = _split3(-c_ref[hh, :, pl.ds(c0, chunk)])
                terms = jnp.concatenate([hi, mid, lo, jnp.zeros((13, chunk), BF16)], axis=0)
                bias = _dot_tn(terms, place)
                kh = kp if hh == 0 else pltpu.roll(kp, HEAD_DIM, 1)
                vh = vp if hh == 0 else pltpu.roll(vp, HEAD_DIM, 1)
                ka_ref[hh, pl.ds(c0, chunk), :] = jnp.where(lane < HEAD_DIM, kh, bias).astype(BF16)
                va_ref[hh, pl.ds(c0, chunk), :] = jnp.where(lane < HEAD_DIM, vh, 1.0).astype(BF16)
            return 0

        lax.fori_loop(0, T // chunk, build, 0)

    row = lax.broadcasted_iota(jnp.int32, (tq, tk), 0)
    col = lax.broadcasted_iota(jnp.int32, (tq, tk), 1)
    lane = lax.broadcasted_iota(jnp.int32, (tq, LANES), 1)
    ones_lanes = (lane >= HEAD_DIM) & (lane < HEAD_DIM + FOX_BIAS_LANES)
    qp = q_ref[...].astype(F32) * (HEAD_DIM ** -0.5 * LOG2E)
    qs = [jnp.where(lane < HEAD_DIM, qh, jnp.where(ones_lanes, 1.0, 0.0)).astype(BF16)
          for qh in (qp, pltpu.roll(qp, HEAD_DIM, 1))]

    def step(j, carry, masked):
        start = pl.multiple_of(j * tk, tk)
        out = []
        for hh in range(2):
            m, acc = carry[hh]
            s = _dot_nt(qs[hh], ka_ref[hh, pl.ds(start, tk), :])
            if masked:
                s = jnp.where(col <= row, s, NEG_INF)
            m_new = jnp.maximum(m, jnp.max(s, axis=-1, keepdims=True))
            p = jnp.exp2(s - m_new)
            acc = jnp.exp2(m - m_new) * acc + _dot(p.astype(BF16), va_ref[hh, pl.ds(start, tk), :])
            out.append((m_new, acc))
        return tuple(out)

    one = (jnp.full((tq, 1), NEG_INF, F32), jnp.zeros((tq, LANES), F32))
    carry = lax.fori_loop(0, i, functools.partial(step, masked=False), (one, one))
    (_, acc0), (_, acc1) = step(i, carry, True)
    o0 = acc0 / acc0[:, HEAD_DIM:HEAD_DIM + 1]
    o1 = acc1 / acc1[:, HEAD_DIM:HEAD_DIM + 1]
    o_ref[...] = jnp.where(lane < HEAD_DIM, o0, pltpu.roll(o1, HEAD_DIM, 1)).astype(o_ref.dtype)


def fox_attention(p2, cum, T, *, tq=FOX_TQ):
    B = p2.shape[0]
    HP = FOX_HEADS // 2
    return pl.pallas_call(
        functools.partial(_fox_kernel, tq=tq),
        grid=(B, HP, T // tq),
        in_specs=[pl.BlockSpec((None, tq, LANES), lambda b, h, i: (b, i, P2_FQ // LANES + h)),
                  pl.BlockSpec((None, T, LANES), lambda b, h, i: (b, 0, P2_FK // LANES + h)),
                  pl.BlockSpec((None, T, LANES), lambda b, h, i: (b, 0, P2_FV // LANES + h)),
                  pl.BlockSpec((None, None, 2, 1, T), lambda b, h, i: (b, h, 0, 0, 0))],
        out_specs=pl.BlockSpec((None, tq, LANES), lambda b, h, i: (b, i, h)),
        out_shape=jax.ShapeDtypeStruct((B, T, FOX_HEADS * HEAD_DIM), BF16),
        scratch_shapes=[pltpu.VMEM((2, T, LANES), BF16), pltpu.VMEM((2, T, LANES), BF16)],
        compiler_params=_cparams("parallel", "parallel", "arbitrary"),
        name="fox_attention",
    )(p2, p2, p2, cum)


def _readout_kernel(ocmp_ref, osel_ref, owin_ref, small_ref, oret_ref, ofox_ref, mg_ref, x_ref, g1_ref,
                    ex_ref, wn_ref, wr_ref, wf_ref, wo_ref, o_ref):
    W = NSA_OUT
    gs = _sigmoid(small_ref[...].astype(F32)).astype(BF16)
    ge = _dot(gs, ex_ref[...])
    onsa = (ge[:, :W] * ocmp_ref[...].astype(F32) + ge[:, W:2 * W] * osel_ref[...].astype(F32)
            + ge[:, 2 * W:] * owin_ref[...].astype(F32))
    D = D_MODEL
    merged = (_sigmoid(mg_ref[:, :D].astype(F32)) * _dot(onsa.astype(BF16), wn_ref[...])
              + _sigmoid(mg_ref[:, D:2 * D].astype(F32)) * _dot(oret_ref[...], wr_ref[...])
              + _sigmoid(mg_ref[:, 2 * D:].astype(F32)) * _dot(ofox_ref[...], wf_ref[...]))
    y = _dot(merged.astype(BF16), wo_ref[...])
    o_ref[...] = x_ref[...] + g1_ref[...] * y


def readout(o_cmp, o_sel, o_win, p2, o_ret, o_fox, x, mod_l, ex, wn, wr, wf, wo, l, T, *, tm=512):
    M, D = x.shape
    per_b = T // tm
    W = NSA_OUT
    row = lambda width, col=0: pl.BlockSpec((tm, width), lambda i: (i, col))
    full = lambda a: pl.BlockSpec(a.shape, lambda i: (0,) * a.ndim)
    return pl.pallas_call(
        _readout_kernel,
        grid=(M // tm,),
        in_specs=[row(W), row(W), row(W), row(LANES, P2_SMALL // LANES), row(512), row(512),
                  row(3 * D, 0), row(D),
                  pl.BlockSpec((None, None, 1, D), lambda i: (i // per_b, 2, 0, 0)),
                  full(ex), _layer_spec(wn, l, 1), _layer_spec(wr, l, 1), _layer_spec(wf, l, 1), _layer_spec(wo, l, 1)],
        out_specs=row(D),
        out_shape=jax.ShapeDtypeStruct((M, D), F32),
        compiler_params=_cparams("parallel"),
        name="mixer_readout",
    )(o_cmp, o_sel, o_win, p2, o_ret, o_fox, p2, x, mod_l, ex, wn, wr, wf, wo)


def nsa_gate_expand():
    ex = np.zeros((LANES, 3 * NSA_OUT), np.float32)
    for br in range(3):
        for h in range(NSA_HEADS):
            c0 = br * NSA_OUT + h * LANES
            ex[br * NSA_HEADS + h, c0:c0 + LANES] = 1.0
    return jnp.asarray(ex, BF16)


def pad_read_nsa(w):
    depth, _, D = w.shape
    w = w.reshape(depth, NSA_HEADS, HEAD_DIM, D)
    z = jnp.zeros_like(w)
    g = (np.arange(NSA_HEADS) // NSA_HPG)[None, :, None, None]
    lo = jnp.where(g == 0, w, z)
    hi = jnp.where(g == 1, w, z)
    return jnp.concatenate([lo, hi], axis=2).reshape(depth, NSA_OUT, D).astype(BF16)


FFN_CHUNK = 512


def _ffn_kernel(x_ref, nw_ref, sc_ref, sh_ref, g2_ref, w1_ref, w3_ref, w2_ref, o_ref):
    x = x_ref[...]
    h = _norm_mod(x, nw_ref[...], sc_ref[...], sh_ref[...]).astype(BF16)
    F = w1_ref.shape[1]
    y = None
    for c0 in range(0, F, FFN_CHUNK):
        cols = slice(c0, min(c0 + FFN_CHUNK, F))
        u = _dot(h, w1_ref[:, cols])
        v = _dot(h, w3_ref[:, cols])
        part = _dot((u * _sigmoid(u) * v).astype(BF16), w2_ref[cols, :])
        y = part if y is None else y + part
    o_ref[...] = x + g2_ref[...] * y


def ffn(x, mod_l, nw, w1, w3, w2, T, *, tm=512):
    M, D = x.shape
    F = w1.shape[1]
    per_b = T // tm
    modspec = lambda k: pl.BlockSpec((None, None, 1, D), lambda i: (i // per_b, k, 0, 0))
    full = lambda a: pl.BlockSpec(a.shape, lambda i: (0,) * a.ndim)
    return pl.pallas_call(
        _ffn_kernel,
        grid=(M // tm,),
        in_specs=[pl.BlockSpec((tm, D), lambda i: (i, 0)),
                  pl.BlockSpec((1, D), lambda i: (0, 0)),
                  modspec(4), modspec(3), modspec(5), full(w1), full(w3), full(w2)],
        out_specs=pl.BlockSpec((tm, D), lambda i: (i, 0)),
        out_shape=jax.ShapeDtypeStruct((M, D), F32),
        compiler_params=_cparams("parallel"),
        name="ffn_dense",
    )(x, nw, mod_l, mod_l, mod_l, w1, w3, w2)


MOE_TC = 512
MOE_TS = 512


def _router_kernel(x_ref, nw_ref, sc_ref, sh_ref, wh_ref, wl_ref, h_ref, gate_ref, rank_ref, cnt_ref, carry_ref):
    @pl.when(pl.program_id(0) == 0)
    def _():
        carry_ref[...] = jnp.zeros_like(carry_ref)

    h = _norm_mod(x_ref[...], nw_ref[...], sc_ref[...], sh_ref[...])
    hh = h.astype(BF16)
    h_ref[...] = _pack_bf16_pairs(hh.astype(F32))
    hl = (h - hh.astype(F32)).astype(BF16)
    logits = _dot(hh, wh_ref[...]) + (_dot(hl, wh_ref[...]) + _dot(hh, wl_ref[...]))
    tm = logits.shape[0]
    lane = lax.broadcasted_iota(jnp.int32, logits.shape, 1)
    logits = jnp.where(lane < N_EXPERTS, logits, REMOVED)
    lane_f = lane.astype(F32)
    v1 = jnp.max(logits, axis=-1, keepdims=True)
    i1 = jnp.min(jnp.where(logits == v1, lane_f, float(LANES)), axis=-1, keepdims=True)
    rest = jnp.where(lane_f == i1, REMOVED, logits)
    v2 = jnp.max(rest, axis=-1, keepdims=True)
    i2 = jnp.min(jnp.where(rest == v2, lane_f, float(LANES)), axis=-1, keepdims=True)
    e2 = jnp.exp(v2 - v1)
    w1 = 1.0 / (1.0 + e2)
    w2 = e2 / (1.0 + e2)
    gate_ref[...] = jnp.where(lane_f == i1, w1, jnp.where(lane_f == i2, w2, 0.0))

    sel = jnp.where((lane_f == i1) | (lane_f == i2), 1.0, 0.0)
    ri = lax.broadcasted_iota(jnp.int32, (tm, tm), 0)
    ci = lax.broadcasted_iota(jnp.int32, (tm, tm), 1)
    before = jnp.where(ci < ri, 1.0, 0.0).astype(BF16)
    rank = _dot(before, sel.astype(BF16)) + carry_ref[0:1, :]
    rank_ref[...] = jnp.where(sel > 0.0, rank, -1.0)
    carry_ref[...] = carry_ref[...] + jnp.sum(sel, axis=0, keepdims=True)
    cnt_ref[...] = carry_ref[...]


def router(x, mod_l, nw, w_router, T):
    M, D = x.shape
    tm = MOE_TC
    per_b = T // tm
    wp = jnp.zeros((D, LANES), F32).at[:, :N_EXPERTS].set(w_router)
    wh = wp.astype(BF16)
    wl = (wp - wh.astype(F32)).astype(BF16)
    return pl.pallas_call(
        _router_kernel,
        grid=(M // tm,),
        in_specs=[pl.BlockSpec((tm, D), lambda i: (i, 0)),
                  pl.BlockSpec((1, D), lambda i: (0, 0))]
        + _mod_specs(T, tm, 4, 3, 1)
        + [pl.BlockSpec((D, LANES), lambda i: (0, 0)),
           pl.BlockSpec((D, LANES), lambda i: (0, 0))],
        out_specs=[pl.BlockSpec((tm, D // 2), lambda i: (i, 0)),
                   pl.BlockSpec((tm, LANES), lambda i: (i, 0)),
                   pl.BlockSpec((tm, LANES), lambda i: (i, 0)),
                   pl.BlockSpec((8, LANES), lambda i: (0, 0))],
        out_shape=[jax.ShapeDtypeStruct((M, D // 2), jnp.uint32),
                   jax.ShapeDtypeStruct((M, LANES), F32),
                   jax.ShapeDtypeStruct((M, LANES), F32),
                   jax.ShapeDtypeStruct((8, LANES), F32)],
        scratch_shapes=[pltpu.VMEM((8, LANES), F32)],
        compiler_params=_cparams("arbitrary"),
        name="moe_router",
    )(x, nw, mod_l, mod_l, wh, wl)


def _count_le(sorted_vals, x):
    return jnp.sum(sorted_vals[None, :] <= x[:, None], axis=1, dtype=jnp.int32)


def _moe_up_kernel(e_r, total, x_ref, w1_ref, w3_ref, o_ref, w1b_ref, w3b_ref):
    r = pl.program_id(1)
    live = r < total[0]

    @pl.when(live & ((r == 0) | (e_r[r] != e_r[jnp.maximum(r - 1, 0)])))
    def _():
        w1b_ref[...] = w1_ref[...].astype(BF16)
        w3b_ref[...] = w3_ref[...].astype(BF16)

    @pl.when(live)
    def _():
        x = _unpack_bf16_pairs(x_ref[...]).astype(BF16)
        u = _dot(x, w1b_ref[...])
        v = _dot(x, w3b_ref[...])
        o_ref[...] = (u * _sigmoid(u) * v).astype(o_ref.dtype)


def moe_up(xs, w1, w3, tiles, rt, *, tf=1792):
    R = xs.shape[0]
    D = w1.shape[1]
    ts = MOE_TS
    F = w1.shape[-1]
    live = lambda r, total: jnp.minimum(r, total[0] - 1)
    return pl.pallas_call(
        _moe_up_kernel,
        grid_spec=pltpu.PrefetchScalarGridSpec(
            num_scalar_prefetch=2,
            grid=(F // tf, rt),
            in_specs=[pl.BlockSpec((ts, D // 2), lambda n, r, e, total: (live(r, total), 0)),
                      pl.BlockSpec((None, D, tf), lambda n, r, e, total: (e[live(r, total)], 0, n)),
                      pl.BlockSpec((None, D, tf), lambda n, r, e, total: (e[live(r, total)], 0, n))],
            out_specs=pl.BlockSpec((ts, tf), lambda n, r, e, total: (r, n)),
            scratch_shapes=[pltpu.VMEM((D, tf), BF16), pltpu.VMEM((D, tf), BF16)],
        ),
        out_shape=jax.ShapeDtypeStruct((R, F), BF16),
        compiler_params=_cparams("arbitrary", "arbitrary"),
        name="moe_up",
    )(tiles["e"], tiles["total"], xs, w1, w3)


def _moe_down_kernel(e_r, total, a_ref, w2_ref, o_ref, w2b_ref):
    r = pl.program_id(0)
    live = r < total[0]

    @pl.when(live & ((r == 0) | (e_r[r] != e_r[jnp.maximum(r - 1, 0)])))
    def _():
        w2b_ref[...] = w2_ref[...].astype(BF16)

    @pl.when(live)
    def _():
        o_ref[...] = _pack_bf16_pairs(_dot(a_ref[...], w2b_ref[...]))


def moe_down(a, w2, tiles, rt):
    R, F = a.shape
    ts = MOE_TS
    D = w2.shape[-1]
    live = lambda r, total: jnp.minimum(r, total[0] - 1)
    return pl.pallas_call(
        _moe_down_kernel,
        grid_spec=pltpu.PrefetchScalarGridSpec(
            num_scalar_prefetch=2,
            grid=(rt,),
            in_specs=[pl.BlockSpec((ts, F), lambda r, e, total: (live(r, total), 0)),
                      pl.BlockSpec((None, F, D), lambda r, e, total: (e[live(r, total)], 0, 0))],
            out_specs=pl.BlockSpec((ts, D // 2), lambda r, e, total: (r, 0)),
            scratch_shapes=[pltpu.VMEM((F, D), BF16)],
        ),
        out_shape=jax.ShapeDtypeStruct((R, D // 2), jnp.uint32),
        compiler_params=_cparams("arbitrary"),
        name="moe_down",
    )(tiles["e"], tiles["total"], a, w2)


SC_WINDOW = 64


def _sc_mesh():
    return plsc.VectorSubcoreMesh(core_axis_name="core", subcore_axis_name="subcore")


def sc_scatter_rows2(x, idx_a, idx_b, n_out):
    n, d = x.shape
    steps = n // SC_WINDOW

    @pl.kernel(out_type=jax.ShapeDtypeStruct((n_out, d), x.dtype), mesh=_sc_mesh(), scratch_types=[])
    def kern(x_hbm, ia_hbm, ib_hbm, o_hbm):
        def body(x_vmem, ia_vmem, ib_vmem):
            pltpu.sync_copy(x_vmem, o_hbm.at[ia_vmem.at[0]])
            pltpu.sync_copy(x_vmem, o_hbm.at[ib_vmem.at[0]])

        pltpu.emit_pipeline(
            body,
            grid=(steps,),
            in_specs=[pl.BlockSpec((SC_WINDOW, d), index_map=lambda i: (i, 0)),
                      pl.BlockSpec((1, SC_WINDOW), index_map=lambda i: (i, 0)),
                      pl.BlockSpec((1, SC_WINDOW), index_map=lambda i: (i, 0))],
            out_specs=[],
            core_axis_name=("core", "subcore"),
            dimension_semantics=(pltpu.PARALLEL,),
        )(x_hbm, ia_hbm, ib_hbm)

    return kern(x, idx_a.reshape(steps, SC_WINDOW), idx_b.reshape(steps, SC_WINDOW))


def sc_gather_rows(x, idx):
    n = idx.shape[0]
    d = x.shape[1]
    steps = n // SC_WINDOW

    @pl.kernel(out_type=jax.ShapeDtypeStruct((n, d), x.dtype), mesh=_sc_mesh(), scratch_types=[])
    def kern(x_hbm, i_hbm, o_hbm):
        def body(i_vmem, o_vmem):
            pltpu.sync_copy(x_hbm.at[i_vmem.at[0]], o_vmem)

        pltpu.emit_pipeline(
            body,
            grid=(steps,),
            in_specs=[pl.BlockSpec((1, SC_WINDOW), index_map=lambda i: (i, 0))],
            out_specs=[pl.BlockSpec((SC_WINDOW, d), index_map=lambda i: (i, 0))],
            core_axis_name=("core", "subcore"),
            dimension_semantics=(pltpu.PARALLEL,),
        )(i_hbm, o_hbm)

    return kern(x, idx.reshape(steps, SC_WINDOW))


def _moe_finish_kernel(x_ref, g2_ref, ya_ref, yb_ref, gate_ref, rank_ref, nw_ref, o_ref, *, normalize):
    gate = gate_ref[...]
    chosen = rank_ref[...] >= 0.0
    lane = lax.broadcasted_iota(jnp.int32, gate.shape, 1).astype(F32)
    first = jnp.min(jnp.where(chosen, lane, float(LANES)), axis=-1, keepdims=True)
    last = jnp.max(jnp.where(chosen, lane, -1.0), axis=-1, keepdims=True)
    wa = jnp.sum(jnp.where(lane == first, gate, 0.0), axis=-1, keepdims=True)
    wb = jnp.sum(jnp.where(lane == last, gate, 0.0), axis=-1, keepdims=True)
    x = x_ref[...] + g2_ref[...] * (wa * _unpack_bf16_pairs(ya_ref[...]) + wb * _unpack_bf16_pairs(yb_ref[...]))
    if normalize:
        ms = jnp.mean(x * x, axis=-1, keepdims=True)
        x = x * lax.rsqrt(ms + NORM_EPS) * nw_ref[...]
    o_ref[...] = x


def moe_finish(x, mod_l, y2, gate, rank, norm_w, T, *, tm=512):
    M, D = x.shape
    per_b = T // tm
    normalize = norm_w is not None
    if norm_w is None:
        norm_w = jnp.ones((1, D), F32)
    return pl.pallas_call(
        functools.partial(_moe_finish_kernel, normalize=normalize),
        grid=(M // tm,),
        in_specs=[pl.BlockSpec((tm, D), lambda i: (i, 0)),
                  pl.BlockSpec((None, None, 1, D), lambda i: (i // per_b, 5, 0, 0)),
                  pl.BlockSpec((None, tm, D // 2), lambda i: (0, i, 0)),
                  pl.BlockSpec((None, tm, D // 2), lambda i: (1, i, 0)),
                  pl.BlockSpec((tm, LANES), lambda i: (i, 0)),
                  pl.BlockSpec((tm, LANES), lambda i: (i, 0)),
                  pl.BlockSpec((1, D), lambda i: (0, 0))],
        out_specs=pl.BlockSpec((tm, D), lambda i: (i, 0)),
        out_shape=jax.ShapeDtypeStruct((M, D), F32),
        compiler_params=_cparams("parallel"),
        name="moe_finish",
    )(x, mod_l, y2, y2, gate, rank, norm_w)


def moe_ffn(x, mod_l, nw, w_router, w1, w3, w2, T, norm_w=None):
    M = x.shape[0]
    ts = MOE_TS
    rt = (2 * M) // ts + N_EXPERTS
    h, gate, rank, cnt = router(x, mod_l, nw, w_router, T)
    i32 = jnp.int32
    counts = cnt[0, :N_EXPERTS].astype(i32)
    ntile = (counts + ts - 1) // ts
    tile_end = jnp.cumsum(ntile)
    row_off = (tile_end - ntile) * ts
    e_r = jnp.minimum(_count_le(tile_end, jnp.arange(rt, dtype=i32)), N_EXPERTS - 1)
    tiles = dict(e=e_r, total=tile_end[-1].reshape(1).astype(i32))
    rk = rank[:, :N_EXPERTS].astype(i32)
    pos = row_off[None, :] + rk
    pos_a = jnp.min(jnp.where(rk >= 0, pos, rt * ts), axis=1)
    pos_b = jnp.max(jnp.where(rk >= 0, pos, -1), axis=1)

    xs = sc_scatter_rows2(h, pos_a, pos_b, rt * ts)
    a = moe_up(xs, w1, w3, tiles, rt)
    y = moe_down(a, w2, tiles, rt)
    y2 = sc_gather_rows(y, jnp.concatenate([pos_a, pos_b])).reshape(2, M, -1)
    return moe_finish(x, mod_l, y2, gate, rank, norm_w, T)


def _final_norm_kernel(x_ref, w_ref, o_ref):
    x = x_ref[...]
    ms = jnp.mean(x * x, axis=-1, keepdims=True)
    o_ref[...] = x * lax.rsqrt(ms + NORM_EPS) * w_ref[...]


def final_norm(x, w, *, tm=1024):
    M, D = x.shape
    return pl.pallas_call(
        _final_norm_kernel,
        grid=(M // tm,),
        in_specs=[pl.BlockSpec((tm, D), lambda i: (i, 0)), pl.BlockSpec((1, D), lambda i: (0, 0))],
        out_specs=pl.BlockSpec((tm, D), lambda i: (i, 0)),
        out_shape=jax.ShapeDtypeStruct((M, D), F32),
        compiler_params=_cparams("parallel"),
        name="final_norm",
    )(x, w)


def nsa_constants(T):
    n_sel = T // SEL_LEN
    nsp = max(LANES, n_sel)
    ncp = T // CMP_STRIDE
    cmp_start = np.arange(ncp) * CMP_STRIDE
    sel_start = np.arange(nsp) * SEL_LEN
    ov = ((cmp_start[:, None] < sel_start[None, :] + SEL_LEN)
          & (cmp_start[:, None] + CMP_LEN > sel_start[None, :]))
    ov[(T - CMP_LEN) // CMP_STRIDE + 1:] = False
    ov[:, n_sel:] = False
    et_mat = ((np.arange(T)[:, None] // SEL_LEN) == np.arange(nsp)[None, :]) * SEL_BONUS
    return jnp.asarray(ov.T, BF16), jnp.asarray(et_mat, BF16)


def token_mixing(x, mod_l, lw, consts, B, T):
    M = B * T
    cos_t, sin_t, ov_t, e_mat, ret_consts, ex = consts
    l = lw["layer"]
    p1 = proj_rope(x, mod_l, lw["norm_mix"], lw["w1"], l, cos_t, sin_t, p1_scales(), T).reshape(B, T, P1_COLS)
    p2 = proj_plain(x, mod_l, lw["norm_mix"], lw["w2"], l, T).reshape(B, T, P2_COLS)

    def group_rows(a):
        return a.reshape(B, T, NSA_GROUPS, HEAD_DIM).transpose(0, 2, 1, 3).reshape(
            B, NSA_GROUPS, T // CMP_STRIDE, CMP_STRIDE * HEAD_DIM)

    xr = jnp.stack([group_rows(p1[:, :, P1_NKC:P1_NKC + LANES]), group_rows(p2[:, :, P2_NVC:P2_NVC + LANES])])
    cmp_out = compress(xr, lw["cmp_pe"], lw["cmp_w1"], lw["cmp_w2"])
    cmp_out = cmp_out.transpose(0, 1, 3, 2, 4).reshape(2, B, T // CMP_STRIDE, LANES)
    o_cmp, sel = nsa_cmp_select(p1, cmp_out[0], cmp_out[1], ov_t, T)
    o_sel = nsa_selected(p1, nsa_value_augment(p2[:, :, P2_NVS:P2_NVS + LANES]), sel, e_mat, T)
    o_win = nsa_window(p1, p2, T)

    o_ret = retention(p1, p2, ret_consts, T)

    ff = p2[:, :, P2_SMALL + 3 * NSA_HEADS:P2_SMALL + 3 * NSA_HEADS + FOX_HEADS].astype(F32)
    ff = ff.transpose(0, 2, 1).reshape(B, FOX_HEADS, T // LANES, LANES)
    cum = fox_cum(ff, lw["fox_bias"]).reshape(B, FOX_HEADS // 2, 2, 1, T)
    o_fox = fox_attention(p2, cum, T)

    return readout(o_cmp.reshape(M, -1), o_sel.reshape(M, -1), o_win.reshape(M, -1), p2.reshape(M, P2_COLS),
                   o_ret.reshape(M, -1), o_fox.reshape(M, -1), x, mod_l, ex,
                   lw["wn"], lw["wr"], lw["wf"], lw["wo"], l, T)


def mixer_weights(norm_mix, w_in, cmp_k_pe, cmp_k_w1, cmp_k_w2, cmp_v_pe, cmp_v_w1, cmp_v_w2, fox_f_bias,
                  w_read_nsa, w_read_ret, w_read_fox, w_out):
    depth = w_in.shape[0]
    w1, w2 = split_w_in(w_in)
    pe = jnp.stack([cmp_k_pe.reshape(depth, 1, -1), cmp_v_pe.reshape(depth, 1, -1)], axis=1)
    pe = jnp.broadcast_to(pe, (depth, 2, 8, pe.shape[-1])).astype(BF16)
    shared = {
        "w1": w1, "w2": w2,
        "wn": pad_read_nsa(w_read_nsa),
        "wr": w_read_ret.astype(BF16),
        "wf": w_read_fox.astype(BF16),
        "wo": w_out.astype(BF16),
    }
    cmp_w1 = jnp.stack([cmp_k_w1, cmp_v_w1], axis=1).astype(BF16)
    cmp_w2 = jnp.stack([cmp_k_w2, cmp_v_w2], axis=1).astype(BF16)
    return [dict(shared, layer=l, norm_mix=norm_mix[l].reshape(1, -1), cmp_pe=pe[l], cmp_w1=cmp_w1[l], cmp_w2=cmp_w2[l],
                 fox_bias=jnp.broadcast_to(fox_f_bias[l][:, None, None], (FOX_HEADS, 1, LANES)))
            for l in range(depth)]


def kernel(x, c, ada_w, ada_b, norm_mix, norm_ffn, w_in, cmp_k_pe, cmp_k_w1, cmp_k_w2, cmp_v_pe, cmp_v_w1,
           cmp_v_w2, fox_f_bias, w_read_nsa, w_read_ret, w_read_fox, w_out, ffn_w1, ffn_w3, ffn_w2, router_w,
           moe_w1, moe_w3, moe_w2, final_norm_w):
    B, T, D = x.shape
    M = B * T
    depth = ada_w.shape[0]
    mod = modulation(c, ada_w, ada_b)
    cos_t, sin_t = rope_tables(T)
    ov_t, e_mat = nsa_constants(T)
    consts = (cos_t, sin_t, ov_t, e_mat, retention_consts(), nsa_gate_expand())
    xs = x.reshape(M, D)
    lws = mixer_weights(norm_mix, w_in, cmp_k_pe, cmp_k_w1, cmp_k_w2, cmp_v_pe, cmp_v_w1, cmp_v_w2,
                        fox_f_bias, w_read_nsa, w_read_ret, w_read_fox, w_out)
    for l in range(depth):
        xs = token_mixing(xs, mod[l], lws[l], consts, B, T)
        nf = norm_ffn[l].reshape(1, D)
        if l % 2 == 0:
            k = l // 2
            xs = ffn(xs, mod[l], nf, ffn_w1[k].astype(BF16), ffn_w3[k].astype(BF16), ffn_w2[k].astype(BF16), T)
        else:
            k = l // 2
            fuse = final_norm_w.reshape(1, D) if l == depth - 1 else None
            xs = moe_ffn(xs, mod[l], nf, router_w[k], moe_w1[k], moe_w3[k], moe_w2[k], T, fuse)
    if depth % 2 == 1:
        xs = final_norm(xs, final_norm_w.reshape(1, D))
    return xs.reshape(B, T, D)
```

```python
import functools
import math

import jax
import jax.numpy as jnp
import numpy as np
from jax import lax
from jax.experimental import pallas as pl
from jax.experimental.pallas import tpu as pltpu
from jax.experimental.pallas import tpu_sc as plsc

F32 = jnp.float32
BF16 = jnp.bfloat16

D_MODEL = 1024
DEPTH = 2
HEAD_DIM = 64
ROPE_THETA = 10000.0
NORM_EPS = 1e-6
NEG_INF = -1e30
REMOVED = -3e38

NSA_HEADS = 8
NSA_GROUPS = 2
NSA_HPG = NSA_HEADS // NSA_GROUPS
CMP_LEN = 32
CMP_STRIDE = 16
CMP_HIDDEN = 256
SEL_LEN = 64
SEL_TOPN = 16
WINDOW = 512
FORCE_SCORE = 1e4
NSA_QBLOCK = 128

RET_HEADS = 4
RET_QK_DIM = 64
RET_V_DIM = 128
RET_CHUNK = 128

FOX_HEADS = 8
FOX_TQ = 1024
LOG2E = 1.4426950408889634

D_FF = 2816
N_EXPERTS = 8
D_FF_EXPERT = 3584

LANES = 128
VMEM_LIMIT = 56 * 1024 * 1024

P1_NQ = 0
P1_RQ = 512
P1_RK = 768
P1_NKC = 1024
P1_NKS = 1152
P1_NKW = 1280
P1_COLS = 1408
P2_MG = 0
P2_RV = 3072
P2_RG = 3584
P2_FQ = 4096
P2_FK = 4608
P2_FV = 5120
P2_NVC = 5632
P2_NVS = 5760
P2_NVW = 5888
P2_SMALL = 6016
P2_COLS = 6144
NSA_OUT = NSA_HEADS * LANES


def _layer_spec(w, l, nidx):
    zeros = (0,) * (w.ndim - 1)
    return pl.BlockSpec((None,) + w.shape[1:], lambda *_: (l,) + zeros)


def _cparams(*sem):
    return pltpu.CompilerParams(dimension_semantics=tuple(sem), vmem_limit_bytes=VMEM_LIMIT)


def _sigmoid(x):
    return 1.0 / (1.0 + jnp.exp(-x))


def _dot(a, b):
    return jnp.dot(a, b, preferred_element_type=F32)


def _dot_nt(a, b):
    return lax.dot_general(a, b, (((1,), (1,)), ((), ())), preferred_element_type=F32)


def _dot_tn(a, b):
    return lax.dot_general(a, b, (((0,), (0,)), ((), ())), preferred_element_type=F32)


def _split3(x):
    hi = x.astype(BF16)
    r1 = x - hi.astype(F32)
    mid = r1.astype(BF16)
    lo = (r1 - mid.astype(F32)).astype(BF16)
    return hi, mid, lo


def _pack_bf16_pairs(x):
    c = x.shape[1] // 2
    lo = pltpu.bitcast(x[:, :c].astype(BF16).astype(F32), jnp.uint32) >> 16
    hi = pltpu.bitcast(x[:, c:].astype(BF16).astype(F32), jnp.uint32) & jnp.uint32(0xFFFF0000)
    return hi | lo


def _unpack_bf16_pairs(u):
    lo = pltpu.bitcast(u << 16, F32)
    hi = pltpu.bitcast(u & jnp.uint32(0xFFFF0000), F32)
    return jnp.concatenate([lo, hi], axis=1)


def _norm_mod(x, nw, sc, sh):
    ms = jnp.mean(x * x, axis=-1, keepdims=True)
    y = x * lax.rsqrt(ms + NORM_EPS) * nw
    return y * (1.0 + sc) + sh


def _mod_kernel(c_ref, w_ref, b_ref, o_ref):
    c = c_ref[...]
    s = c * _sigmoid(c)
    o_ref[0] = _dot(s.astype(BF16), w_ref[0].astype(BF16)) + b_ref[0]


def modulation(c, ada_w, ada_b):
    B, D = c.shape
    depth = ada_w.shape[0]
    rows = 8
    c_pad = jnp.zeros((rows, D), F32).at[:B].set(c)
    out = pl.pallas_call(
        _mod_kernel,
        grid=(depth, 6),
        in_specs=[pl.BlockSpec((rows, D), lambda l, j: (0, 0)),
                  pl.BlockSpec((1, D, D), lambda l, j: (l, 0, j)),
                  pl.BlockSpec((1, 1, D), lambda l, j: (l, 0, j))],
        out_specs=pl.BlockSpec((1, rows, D), lambda l, j: (l, 0, j)),
        out_shape=jax.ShapeDtypeStruct((depth, rows, 6 * D), F32),
        compiler_params=_cparams("parallel", "parallel"),
        name="modulation",
    )(c_pad, ada_w, ada_b.reshape(depth, 1, 6 * D))
    return out[:, :B].reshape(depth, B, 6, 1, D)


def _proj_plain_kernel(x_ref, nw_ref, sc_ref, sh_ref, w_ref, o_ref, *, tn):
    h = _norm_mod(x_ref[...], nw_ref[...], sc_ref[...], sh_ref[...]).astype(BF16)
    for n in range(w_ref.shape[1] // tn):
        cols = slice(n * tn, (n + 1) * tn)
        o_ref[:, cols] = _dot(h, w_ref[:, cols]).astype(o_ref.dtype)


def _proj_rope_kernel(x_ref, nw_ref, sc_ref, sh_ref, w_ref, cos_ref, sin_ref, o_ref, *, scales):
    h = _norm_mod(x_ref[...], nw_ref[...], sc_ref[...], sh_ref[...]).astype(BF16)
    y = _dot(h, w_ref[...])
    cos = cos_ref[...]
    sin = sin_ref[...]
    lane = lax.broadcasted_iota(jnp.int32, cos.shape, 1)
    first_half = (lane % HEAD_DIM) < (HEAD_DIM // 2)
    for g, scale in enumerate(scales):
        yg = y[:, g * LANES:(g + 1) * LANES]
        rot = jnp.where(first_half, pltpu.roll(yg, LANES - HEAD_DIM // 2, 1),
                        pltpu.roll(yg, HEAD_DIM // 2, 1))
        r = yg * cos + rot * sin
        if scale != 1.0:
            r = r * scale
        o_ref[:, g * LANES:(g + 1) * LANES] = r.astype(o_ref.dtype)


def _mod_specs(T, tm, sc_idx, sh_idx, nargs):
    per_b = T // tm
    if nargs == 1:
        return [pl.BlockSpec((None, None, 1, D_MODEL), lambda i: (i // per_b, sc_idx, 0, 0)),
                pl.BlockSpec((None, None, 1, D_MODEL), lambda i: (i // per_b, sh_idx, 0, 0))]
    return [pl.BlockSpec((None, None, 1, D_MODEL), lambda i, j: (i // per_b, sc_idx, 0, 0)),
            pl.BlockSpec((None, None, 1, D_MODEL), lambda i, j: (i // per_b, sh_idx, 0, 0))]


def proj_plain(x, mod_l, nw, w, l, T, *, tm=512, tn=512):
    M, D = x.shape
    N = w.shape[-1]
    return pl.pallas_call(
        functools.partial(_proj_plain_kernel, tn=tn),
        grid=(M // tm,),
        in_specs=[pl.BlockSpec((tm, D), lambda i: (i, 0)),
                  pl.BlockSpec((1, D), lambda i: (0, 0))]
        + _mod_specs(T, tm, 1, 0, 1)
        + [_layer_spec(w, l, 1)],
        out_specs=pl.BlockSpec((tm, N), lambda i: (i, 0)),
        out_shape=jax.ShapeDtypeStruct((M, N), BF16),
        compiler_params=_cparams("parallel"),
        name="proj_plain",
    )(x, nw, mod_l, mod_l, w)


def proj_rope(x, mod_l, nw, w, l, cos, sin, scales, T, *, tm=512):
    M, D = x.shape
    N = w.shape[-1]
    per_b = T // tm
    return pl.pallas_call(
        functools.partial(_proj_rope_kernel, scales=scales),
        grid=(M // tm,),
        in_specs=[pl.BlockSpec((tm, D), lambda i: (i, 0)),
                  pl.BlockSpec((1, D), lambda i: (0, 0))]
        + _mod_specs(T, tm, 1, 0, 1)
        + [_layer_spec(w, l, 1),
           pl.BlockSpec((tm, LANES), lambda i: (i % per_b, 0)),
           pl.BlockSpec((tm, LANES), lambda i: (i % per_b, 0))],
        out_specs=pl.BlockSpec((tm, N), lambda i: (i, 0)),
        out_shape=jax.ShapeDtypeStruct((M, N), BF16),
        compiler_params=_cparams("parallel"),
        name="proj_rope",
    )(x, nw, mod_l, mod_l, w, cos, sin)


def rope_tables(T):
    d = HEAD_DIM
    pos = jnp.arange(T, dtype=F32)
    inv = ROPE_THETA ** (-jnp.arange(0, d, 2, dtype=F32) / d)
    ang = pos[:, None] * inv[None, :]
    cos = jnp.cos(ang)
    sin = jnp.sin(ang)
    cos_t = jnp.concatenate([cos, cos, cos, cos], axis=-1)
    sin_t = jnp.concatenate([-sin, sin, -sin, sin], axis=-1)
    return cos_t, sin_t


def split_w_in(w_in):
    sizes = [512, 128, 128, 128, 128, 128, 128, 24, 256, 256, 512, 512, 512, 512, 512, 8, 3072]
    offs = np.cumsum([0] + sizes)
    wb = w_in.astype(BF16)
    (nq, nkc, nvc, nks, nvs, nkw, nvw, ngate, rq, rk, rv, rg, fq, fk, fv, ff, mg) = [
        wb[..., offs[i]:offs[i + 1]] for i in range(len(sizes))]
    small = jnp.concatenate([ngate, ff, jnp.zeros(ngate.shape[:-1] + (LANES - 32,), BF16)], axis=-1)
    w1 = jnp.concatenate([nq, rq, rk, nkc, nks, nkw], axis=-1)
    w2 = jnp.concatenate([mg, rv, rg, fq, fk, fv, nvc, nvs, nvw, small], axis=-1)
    assert w1.shape[-1] == P1_COLS and w2.shape[-1] == P2_COLS
    return w1, w2


def p1_scales():
    s = [1.0] * (P1_COLS // LANES)
    for g in range(P1_NQ // LANES, P1_RQ // LANES):
        s[g] = HEAD_DIM ** -0.5 * LOG2E
    for g in range(P1_RK // LANES, P1_NKC // LANES):
        s[g] = RET_QK_DIM ** -0.5
    return tuple(s)


def _compress_kernel(x_ref, pe_ref, w1_ref, w2_ref, o_ref):
    r = x_ref[...]
    half = r.shape[1]
    w1 = w1_ref[...]
    a = _dot(r, w1[:half])
    b = _dot(r, w1[half:])
    pe = _dot(pe_ref[...], w1)[0:1]
    n = a.shape[0]
    hid = a + pltpu.roll(b, n - 1, 0) + pe
    hid = hid * _sigmoid(hid)
    o_ref[...] = _dot(hid.astype(BF16), w2_ref[...]).astype(o_ref.dtype)


def compress(xr, pe, w1, w2):
    _, B, G, R, W = xr.shape
    H = w1.shape[-1]
    return pl.pallas_call(
        _compress_kernel,
        grid=(2, B, G),
        in_specs=[pl.BlockSpec((None, None, None, R, W), lambda s, b, g: (s, b, g, 0, 0)),
                  pl.BlockSpec((None, 8, 2 * W), lambda s, b, g: (s, 0, 0)),
                  pl.BlockSpec((None, 2 * W, H), lambda s, b, g: (s, 0, 0)),
                  pl.BlockSpec((None, H, HEAD_DIM), lambda s, b, g: (s, 0, 0))],
        out_specs=pl.BlockSpec((None, None, None, R, HEAD_DIM), lambda s, b, g: (s, b, g, 0, 0)),
        out_shape=jax.ShapeDtypeStruct((2, B, G, R, HEAD_DIM), BF16),
        compiler_params=_cparams("parallel", "parallel", "parallel"),
        name="nsa_compress",
    )(xr, pe, w1, w2)


def _stack_heads(q_ref, g):
    tq = q_ref.shape[0]
    half = lax.broadcasted_iota(jnp.int32, (tq, LANES), 1) // HEAD_DIM
    rows = []
    for hh in range(NSA_HPG):
        h = NSA_HPG * g + hh
        x = q_ref[:, (h // 2) * LANES:(h // 2 + 1) * LANES].astype(F32)
        if h % 2 != g:
            x = pltpu.roll(x, HEAD_DIM, 1)
        rows.append(jnp.where(half == g, x, 0.0).astype(BF16))
    return jnp.concatenate(rows, axis=0)


def _store_heads(o_ref, g, o, tq):
    for hh in range(NSA_HPG):
        h = NSA_HPG * g + hh
        o_ref[:, h * LANES:(h + 1) * LANES] = o[hh * tq:(hh + 1) * tq].astype(o_ref.dtype)


CMP_CHUNK = 128


def _nsa_cmp_kernel(q_ref, kc_ref, vc_ref, ov_ref, o_ref, m_ref, imp_ref, *, tq, n_sel, top_n):
    t0 = pl.program_id(1) * tq
    ncp = kc_ref.shape[0]
    nsp = ov_ref.shape[0]
    rows = NSA_HPG * tq

    def attend(ncols):
        kc = kc_ref[0:ncols, :]
        vc = vc_ref[0:ncols, :]
        n_idx = lax.broadcasted_iota(jnp.int32, (rows, ncols), 1)
        t_idx = t0 + lax.broadcasted_iota(jnp.int32, (rows, ncols), 0) % tq
        valid = (n_idx * CMP_STRIDE + (CMP_LEN - 1)) <= t_idx
        for g in range(NSA_GROUPS):
            q = _stack_heads(q_ref, g)
            s = jnp.where(valid, _dot_nt(q, kc), NEG_INF)
            m = jnp.max(s, axis=-1, keepdims=True)
            e = jnp.exp2(s - m)
            l = jnp.sum(e, axis=-1, keepdims=True)
            p = e * jnp.where(m > 0.5 * NEG_INF, 1.0 / l, 0.0)
            _store_heads(o_ref, g, _dot(p.astype(BF16), vc), tq)
            psum = p[0:tq]
            for hh in range(1, NSA_HPG):
                psum = psum + p[hh * tq:(hh + 1) * tq]
            imp_ref[g] = _dot_nt(ov_ref[:, 0:ncols], psum.astype(BF16))

    n_live = jnp.maximum((t0 + tq - CMP_LEN) // CMP_STRIDE + 1, 1)
    n_chunks = jnp.minimum((n_live + CMP_CHUNK - 1) // CMP_CHUNK, ncp // CMP_CHUNK)
    for nc in range(1, ncp // CMP_CHUNK + 1):
        pl.when(n_chunks == nc)(functools.partial(attend, nc * CMP_CHUNK))

    j_idx = lax.broadcasted_iota(jnp.int32, (nsp, tq), 0)
    cur = (t0 + lax.broadcasted_iota(jnp.int32, (nsp, tq), 1)) // SEL_LEN
    forced = (j_idx == 0) | (j_idx == cur) | (j_idx == cur - 1)
    j_f = j_idx.astype(F32)
    for g in range(NSA_GROUPS):
        score = jnp.where(j_idx <= cur, imp_ref[g], NEG_INF)
        score = jnp.where(forced | (j_idx >= n_sel), REMOVED, score)
        sel = jnp.where(forced, 1.0, 0.0)
        for _ in range(max(top_n - 3, 0)):
            mx = jnp.max(score, axis=0, keepdims=True)
            idx = jnp.min(jnp.where(score == mx, j_f, float(nsp)), axis=0, keepdims=True)
            hit = j_f == idx
            sel = jnp.where(hit, 1.0, sel)
            score = jnp.where(hit, REMOVED, score)
        sel = jnp.where(j_idx <= cur, sel, 0.0)
        m_ref[g] = sel.T.astype(m_ref.dtype)


SEL_BONUS = 8192.0
NSA_SEL_TQ = 256
NSA_SEL_TK = 1024


def _nsa_sel_kernel(q_ref, k_ref, v_ref, m_ref, et_ref, o_ref, *, tq, tk):
    t0 = pl.program_id(1) * tq
    n_full = t0 // tk
    rows = NSA_HPG * tq

    def update(carry, q, ks, vs, mask=None):
        m, acc = carry
        s = _dot_nt(q, ks)
        if mask is not None:
            s = jnp.where(mask, s, NEG_INF)
        m_new = jnp.maximum(m, jnp.max(s, axis=-1, keepdims=True))
        p = jnp.exp2(s - m_new)
        return m_new, jnp.exp2(m - m_new) * acc + _dot(p.astype(BF16), vs)

    qs, carries = [], []
    for g in range(NSA_GROUPS):
        q = jnp.concatenate([_stack_heads(q_ref, g), jnp.concatenate([m_ref[g]] * NSA_HPG, axis=0)], axis=1)

        def step(j, carry, q=q, g=g):
            start = pl.multiple_of(j * tk, tk)
            ks = jnp.concatenate([k_ref[pl.ds(start, tk), :], et_ref[pl.ds(start, tk), :]], axis=1)
            return update(carry, q, ks, v_ref[g, pl.ds(start, tk), :])

        init = (jnp.full((rows, 1), NEG_INF, F32), jnp.zeros((rows, LANES), F32))
        qs.append(q)
        carries.append(lax.fori_loop(0, n_full, step, init))

    start = pl.multiple_of(n_full * tk, tk)

    def tail(nk):
        trow = t0 + lax.broadcasted_iota(jnp.int32, (rows, nk), 0) % tq
        causal = start + lax.broadcasted_iota(jnp.int32, (rows, nk), 1) <= trow
        ks = jnp.concatenate([k_ref[pl.ds(start, nk), :], et_ref[pl.ds(start, nk), :]], axis=1)
        for g in range(NSA_GROUPS):
            _, acc = update(carries[g], qs[g], ks, v_ref[g, pl.ds(start, nk), :], causal)
            den = HEAD_DIM * (1 - g)
            _store_heads(o_ref, g, acc / acc[:, den:den + 1], tq)

    which = (t0 - start) // tq
    for v in range(tk // tq):
        pl.when(which == v)(functools.partial(tail, (v + 1) * tq))


def nsa_value_augment(v):
    ones = jnp.ones_like(v[..., :HEAD_DIM])
    return jnp.stack([jnp.concatenate([v[..., :HEAD_DIM], ones], axis=-1),
                      jnp.concatenate([ones, v[..., HEAD_DIM:]], axis=-1)], axis=1)


def nsa_selected(p1, v_aug, sel, et_mat, T, *, tq=NSA_SEL_TQ, tk=NSA_SEL_TK):
    B = p1.shape[0]
    nsp = sel.shape[-1]
    return pl.pallas_call(
        functools.partial(_nsa_sel_kernel, tq=tq, tk=tk),
        grid=(B, T // tq),
        in_specs=[pl.BlockSpec((None, tq, NSA_HEADS * HEAD_DIM), lambda b, i: (b, i, 0)),
                  pl.BlockSpec((None, T, LANES), lambda b, i: (b, 0, P1_NKS // LANES)),
                  pl.BlockSpec((None, NSA_GROUPS, T, LANES), lambda b, i: (b, 0, 0, 0)),
                  pl.BlockSpec((None, NSA_GROUPS, tq, nsp), lambda b, i: (b, 0, i, 0)),
                  pl.BlockSpec((T, nsp), lambda b, i: (0, 0))],
        out_specs=pl.BlockSpec((None, tq, NSA_OUT), lambda b, i: (b, i, 0)),
        out_shape=jax.ShapeDtypeStruct((B, T, NSA_OUT), BF16),
        compiler_params=_cparams("parallel", "parallel"),
        name="nsa_selected",
    )(p1, p1, v_aug, sel, et_mat)


def _nsa_win_kernel(q_ref, k_ref, v_ref, b_ref, o_ref, *, tq):
    t0 = pl.program_id(1) * tq
    span = WINDOW + tq
    start = pl.multiple_of(jnp.maximum(t0 - WINDOW, 0), tq)
    ks = k_ref[pl.ds(start, span), :]
    vs = v_ref[pl.ds(start, span), :]

    def run(bias):
        bias = jnp.concatenate([bias] * NSA_HPG, axis=0)
        for g in range(NSA_GROUPS):
            s = _dot_nt(_stack_heads(q_ref, g), ks) + bias
            m = jnp.max(s, axis=-1, keepdims=True)
            p = jnp.exp2(s - m)
            l = jnp.sum(p, axis=-1, keepdims=True)
            _store_heads(o_ref, g, _dot(p.astype(BF16), vs) / l, tq)

    @pl.when(t0 >= WINDOW)
    def _():
        run(b_ref[...])

    @pl.when(t0 < WINDOW)
    def _():
        row = lax.broadcasted_iota(jnp.int32, (tq, span), 0)
        col = lax.broadcasted_iota(jnp.int32, (tq, span), 1)
        run(jnp.where(col <= t0 + row, 0.0, NEG_INF))


def _nsa_cmp_win_kernel(q_ref, kc_ref, vc_ref, ov_ref, kw_ref, vw_ref, band_ref, ocmp_ref, m_ref, owin_ref, imp_ref,
                        *, tq, n_sel, top_n):
    _nsa_cmp_kernel(q_ref, kc_ref, vc_ref, ov_ref, ocmp_ref, m_ref, imp_ref, tq=tq, n_sel=n_sel, top_n=top_n)
    _nsa_win_kernel(q_ref, kw_ref, vw_ref, band_ref, owin_ref, tq=tq)


def nsa_cmp_select_window(p1, p2, kc, vc, ov_t, T):
    B = p1.shape[0]
    tq = NSA_QBLOCK
    ncp = kc.shape[1]
    nsp = ov_t.shape[0]
    n_sel = T // SEL_LEN
    span = WINDOW + tq
    r = np.arange(tq)[:, None]
    c = np.arange(span)[None, :]
    band = jnp.asarray(np.where((c > r) & (c <= r + WINDOW), 0.0, NEG_INF), F32)
    out_blk = pl.BlockSpec((None, tq, NSA_OUT), lambda b, i: (b, i, 0))
    return pl.pallas_call(
        functools.partial(_nsa_cmp_win_kernel, tq=tq, n_sel=n_sel, top_n=min(SEL_TOPN, n_sel)),
        grid=(B, T // tq),
        in_specs=[pl.BlockSpec((None, tq, NSA_HEADS * HEAD_DIM), lambda b, i: (b, i, 0)),
                  pl.BlockSpec((None, ncp, LANES), lambda b, i: (b, 0, 0)),
                  pl.BlockSpec((None, ncp, LANES), lambda b, i: (b, 0, 0)),
                  pl.BlockSpec((nsp, ncp), lambda b, i: (0, 0)),
                  pl.BlockSpec((None, T, LANES), lambda b, i: (b, 0, P1_NKW // LANES)),
                  pl.BlockSpec((None, T, LANES), lambda b, i: (b, 0, P2_NVW // LANES)),
                  pl.BlockSpec((tq, span), lambda b, i: (0, 0))],
        out_specs=[out_blk, pl.BlockSpec((None, NSA_GROUPS, tq, nsp), lambda b, i: (b, 0, i, 0)), out_blk],
        out_shape=[jax.ShapeDtypeStruct((B, T, NSA_OUT), BF16),
                   jax.ShapeDtypeStruct((B, NSA_GROUPS, T, nsp), BF16),
                   jax.ShapeDtypeStruct((B, T, NSA_OUT), BF16)],
        scratch_shapes=[pltpu.VMEM((NSA_GROUPS, nsp, tq), F32)],
        compiler_params=_cparams("parallel", "parallel"),
        name="nsa_cmp_select_window",
    )(p1, kc, vc, ov_t, p1, p2, band)


def _retention_kernel(q_ref, k_ref, v_ref, g_ref, din_ref, qd_ref, kd_ref, cd_ref, o_ref, st_ref):
    @pl.when(pl.program_id(0) == 0)
    def _():
        st_ref[...] = jnp.zeros_like(st_ref)

    B = q_ref.shape[0]
    C = RET_CHUNK
    half = lax.broadcasted_iota(jnp.int32, (C, LANES), 1) // HEAD_DIM
    for b in range(B):
        for h in range(RET_HEADS):
            lanes = slice(h * LANES, (h + 1) * LANES)
            pair = slice((h // 2) * LANES, (h // 2 + 1) * LANES)
            st = st_ref[b, h]
            for sub in range(q_ref.shape[1] // C):
                rows = slice(sub * C, (sub + 1) * C)
                qh = jnp.where(half == h % 2, q_ref[b, rows, pair], 0.0).astype(BF16)
                kp = k_ref[b, rows, pair]
                vh = v_ref[b, rows, lanes]
                inner = _dot_nt(qh, kp) * din_ref[h]
                o = _dot(inner.astype(BF16), vh) + _dot(qh, st.astype(BF16)) * qd_ref[h]
                kd = (kp.astype(F32) * kd_ref[h]).astype(BF16)
                st = st * cd_ref[h, 0:1, :] + _dot_tn(kd, vh)
                mu = jnp.mean(o, axis=-1, keepdims=True)
                d = o - mu
                var = jnp.mean(d * d, axis=-1, keepdims=True)
                on = d * lax.rsqrt(var + NORM_EPS)
                gh = g_ref[b, rows, lanes].astype(F32)
                o_ref[b, rows, lanes] = (gh * _sigmoid(gh) * on).astype(o_ref.dtype)
            st_ref[b, h] = st


def retention_consts():
    C = RET_CHUNK
    H = RET_HEADS
    log_g = jnp.log(1.0 - 2.0 ** (-5.0 - jnp.arange(H, dtype=F32)))
    n = jnp.arange(C, dtype=F32)
    diff = n[:, None] - n[None, :]
    causal = diff >= 0
    decay_in = jnp.where(causal[None], jnp.exp(jnp.where(causal, diff, 0.0)[None] * log_g[:, None, None]), 0.0)
    q_decay = jnp.exp((n[None, :] + 1.0) * log_g[:, None])
    k_decay = jnp.exp((C - 1.0 - n)[None, :] * log_g[:, None])
    chunk_decay = jnp.exp(C * log_g)
    qd = jnp.broadcast_to(q_decay[:, :, None], (H, C, LANES))
    kd = jnp.broadcast_to(k_decay[:, :, None], (H, C, LANES))
    cd = jnp.broadcast_to(chunk_decay[:, None, None], (H, 8, LANES))
    return decay_in, qd, kd, cd


RET_STEP = 4


def retention(p1, p2, consts, T):
    B = p1.shape[0]
    C = RET_CHUNK * RET_STEP
    din, qd, kd, cd = consts
    W = RET_HEADS * LANES
    full = lambda shape: pl.BlockSpec(shape, lambda c: (0,) * len(shape))
    return pl.pallas_call(
        _retention_kernel,
        grid=(T // C,),
        in_specs=[pl.BlockSpec((B, C, W // 2), lambda c: (0, c, P1_RQ // (W // 2))),
                  pl.BlockSpec((B, C, W // 2), lambda c: (0, c, P1_RK // (W // 2))),
                  pl.BlockSpec((B, C, W), lambda c: (0, c, P2_RV // W)),
                  pl.BlockSpec((B, C, W), lambda c: (0, c, P2_RG // W)),
                  full(din.shape), full(qd.shape), full(kd.shape), full(cd.shape)],
        out_specs=pl.BlockSpec((B, C, W), lambda c: (0, c, 0)),
        out_shape=jax.ShapeDtypeStruct((B, T, W), BF16),
        scratch_shapes=[pltpu.VMEM((B, RET_HEADS, LANES, LANES), F32)],
        compiler_params=_cparams("arbitrary"),
        name="retention",
    )(p1, p1, p2, p2, din, qd, kd, cd)


def _fox_cum_kernel(f_ref, b_ref, o_ref):
    x = f_ref[...] + b_ref[...]
    ls = jnp.minimum(x, 0.0) - jnp.log1p(jnp.exp(-jnp.abs(x)))
    R = x.shape[0]
    ki = lax.broadcasted_iota(jnp.int32, (LANES, LANES), 0)
    ji = lax.broadcasted_iota(jnp.int32, (LANES, LANES), 1)
    upper = jnp.where(ki <= ji, 1.0, 0.0).astype(BF16)
    hi, mid, lo = _split3(ls)
    rowcum = _dot(hi, upper) + _dot(mid, upper) + _dot(lo, upper)
    tot = jnp.broadcast_to(rowcum[:, LANES - 1:LANES], (R, LANES))
    ri = lax.broadcasted_iota(jnp.int32, (R, R), 0)
    ci = lax.broadcasted_iota(jnp.int32, (R, R), 1)
    lower = jnp.where(ci < ri, 1.0, 0.0).astype(BF16)
    hi, mid, lo = _split3(tot)
    offs = _dot(lower, hi) + _dot(lower, mid) + _dot(lower, lo)
    o_ref[...] = (rowcum + offs) * LOG2E


def fox_cum(f_logit, bias):
    B, H, R, _ = f_logit.shape
    return pl.pallas_call(
        _fox_cum_kernel,
        grid=(B, H),
        in_specs=[pl.BlockSpec((None, None, R, LANES), lambda b, h: (b, h, 0, 0)),
                  pl.BlockSpec((None, 1, LANES), lambda b, h: (h, 0, 0))],
        out_specs=pl.BlockSpec((None, None, R, LANES), lambda b, h: (b, h, 0, 0)),
        out_shape=jax.ShapeDtypeStruct((B, H, R, LANES), F32),
        compiler_params=_cparams("parallel", "parallel"),
        name="fox_cum",
    )(f_logit, bias)


FOX_BIAS_LANES = 3


def _fox_kernel(q_ref, k_ref, v_ref, c_ref, o_ref, ka_ref, va_ref, *, tq):
    i = pl.program_id(2)
    tk = tq
    T = k_ref.shape[0]
    chunk = 512

    @pl.when(i == 0)
    def _():
        lane = lax.broadcasted_iota(jnp.int32, (chunk, LANES), 1)
        ri = lax.broadcasted_iota(jnp.int32, (16, LANES), 0)
        ci = lax.broadcasted_iota(jnp.int32, (16, LANES), 1)
        place = jnp.where((ci == ri + HEAD_DIM) & (ri < FOX_BIAS_LANES), 1.0, 0.0).astype(BF16)

        def build(c, _):
            c0 = pl.multiple_of(c * chunk, chunk)
            kp = k_ref[pl.ds(c0, chunk), :].astype(F32)
            vp = v_ref[pl.ds(c0, chunk), :].astype(F32)
            for hh in range(2):
                hi, mid, lo = _split3(-c_ref[hh, :, pl.ds(c0, chunk)])
                terms = jnp.concatenate([hi, mid, lo, jnp.zeros((13, chunk), BF16)], axis=0)
                bias = _dot_tn(terms, place)
                kh = kp if hh == 0 else pltpu.roll(kp, HEAD_DIM, 1)
                vh = vp if hh == 0 else pltpu.roll(vp, HEAD_DIM, 1)
                ka_ref[hh, pl.ds(c0, chunk), :] = jnp.where(lane < HEAD_DIM, kh, bias).astype(BF16)
                va_ref[hh, pl.ds(c0, chunk), :] = jnp.where(lane < HEAD_DIM, vh, 1.0).astype(BF16)
            return 0

        lax.fori_loop(0, T // chunk, build, 0)

    lane = lax.broadcasted_iota(jnp.int32, (tq, LANES), 1)
    ones_lanes = (lane >= HEAD_DIM) & (lane < HEAD_DIM + FOX_BIAS_LANES)
    qp = q_ref[...].astype(F32) * (HEAD_DIM ** -0.5 * LOG2E)
    qs = [jnp.where(lane < HEAD_DIM, qh, jnp.where(ones_lanes, 1.0, 0.0)).astype(BF16)
          for qh in (qp, pltpu.roll(qp, HEAD_DIM, 1))]

    def update(hh, m, acc, q, start, size, mask=None):
        s = _dot_nt(q, ka_ref[hh, pl.ds(start, size), :])
        if mask is not None:
            s = jnp.where(mask, s, NEG_INF)
        m_new = jnp.maximum(m, jnp.max(s, axis=-1, keepdims=True))
        p = jnp.exp2(s - m_new)
        return m_new, jnp.exp2(m - m_new) * acc + _dot(p.astype(BF16), va_ref[hh, pl.ds(start, size), :])

    def step(j, carry):
        start = pl.multiple_of(j * tk, tk)
        return tuple(update(hh, *carry[hh], qs[hh], start, tk) for hh in range(2))

    one = (jnp.full((tq, 1), NEG_INF, F32), jnp.zeros((tq, LANES), F32))
    carry = lax.fori_loop(0, i, step, (one, one))

    half = tq // 2
    start = pl.multiple_of(i * tk, tk)
    row = lax.broadcasted_iota(jnp.int32, (tq, half), 0)
    col = lax.broadcasted_iota(jnp.int32, (tq, half), 1)
    accs = []
    for hh in range(2):
        m, acc = update(hh, *carry[hh], qs[hh], start, half, col <= row)
        _, low = update(hh, m[half:], acc[half:], qs[hh][half:], start + half, half, (col <= row)[:half])
        accs.append(jnp.concatenate([acc[:half], low], axis=0))
    acc0, acc1 = accs
    o0 = acc0 / acc0[:, HEAD_DIM:HEAD_DIM + 1]
    o1 = acc1 / acc1[:, HEAD_DIM:HEAD_DIM + 1]
    o_ref[...] = jnp.where(lane < HEAD_DIM, o0, pltpu.roll(o1, HEAD_DIM, 1)).astype(o_ref.dtype)


def fox_attention(p2, cum, T, *, tq=FOX_TQ):
    B = p2.shape[0]
    HP = FOX_HEADS // 2
    return pl.pallas_call(
        functools.partial(_fox_kernel, tq=tq),
        grid=(B, HP, T // tq),
        in_specs=[pl.BlockSpec((None, tq, LANES), lambda b, h, i: (b, i, P2_FQ // LANES + h)),
                  pl.BlockSpec((None, T, LANES), lambda b, h, i: (b, 0, P2_FK // LANES + h)),
                  pl.BlockSpec((None, T, LANES), lambda b, h, i: (b, 0, P2_FV // LANES + h)),
                  pl.BlockSpec((None, None, 2, 1, T), lambda b, h, i: (b, h, 0, 0, 0))],
        out_specs=pl.BlockSpec((None, tq, LANES), lambda b, h, i: (b, i, h)),
        out_shape=jax.ShapeDtypeStruct((B, T, FOX_HEADS * HEAD_DIM), BF16),
        scratch_shapes=[pltpu.VMEM((2, T, LANES), BF16), pltpu.VMEM((2, T, LANES), BF16)],
        compiler_params=_cparams("parallel", "parallel", "arbitrary"),
        name="fox_attention",
    )(p2, p2, p2, cum)


def _readout_kernel(ocmp_ref, osel_ref, owin_ref, small_ref, oret_ref, ofox_ref, mg_ref, x_ref, g1_ref,
                    ex_ref, wn_ref, wr_ref, wf_ref, wo_ref, o_ref):
    W = NSA_OUT
    gs = _sigmoid(small_ref[...].astype(F32)).astype(BF16)
    ge = _dot(gs, ex_ref[...])
    onsa = (ge[:, :W] * ocmp_ref[...].astype(F32) + ge[:, W:2 * W] * osel_ref[...].astype(F32)
            + ge[:, 2 * W:] * owin_ref[...].astype(F32))
    D = D_MODEL
    merged = (_sigmoid(mg_ref[:, :D].astype(F32)) * _dot(onsa.astype(BF16), wn_ref[...])
              + _sigmoid(mg_ref[:, D:2 * D].astype(F32)) * _dot(oret_ref[...], wr_ref[...])
              + _sigmoid(mg_ref[:, 2 * D:].astype(F32)) * _dot(ofox_ref[...], wf_ref[...]))
    y = _dot(merged.astype(BF16), wo_ref[...])
    o_ref[...] = x_ref[...] + g1_ref[...] * y


def readout(o_cmp, o_sel, o_win, p2, o_ret, o_fox, x, mod_l, ex, wn, wr, wf, wo, l, T, *, tm=512):
    M, D = x.shape
    per_b = T // tm
    W = NSA_OUT
    row = lambda width, col=0: pl.BlockSpec((tm, width), lambda i: (i, col))
    full = lambda a: pl.BlockSpec(a.shape, lambda i: (0,) * a.ndim)
    return pl.pallas_call(
        _readout_kernel,
        grid=(M // tm,),
        in_specs=[row(W), row(W), row(W), row(LANES, P2_SMALL // LANES), row(512), row(512),
                  row(3 * D, 0), row(D),
                  pl.BlockSpec((None, None, 1, D), lambda i: (i // per_b, 2, 0, 0)),
                  full(ex), _layer_spec(wn, l, 1), _layer_spec(wr, l, 1), _layer_spec(wf, l, 1), _layer_spec(wo, l, 1)],
        out_specs=row(D),
        out_shape=jax.ShapeDtypeStruct((M, D), F32),
        compiler_params=_cparams("parallel"),
        name="mixer_readout",
    )(o_cmp, o_sel, o_win, p2, o_ret, o_fox, p2, x, mod_l, ex, wn, wr, wf, wo)


def nsa_gate_expand():
    ex = np.zeros((LANES, 3 * NSA_OUT), np.float32)
    for br in range(3):
        for h in range(NSA_HEADS):
            c0 = br * NSA_OUT + h * LANES
            ex[br * NSA_HEADS + h, c0:c0 + LANES] = 1.0
    return jnp.asarray(ex, BF16)


def pad_read_nsa(w):
    depth, _, D = w.shape
    w = w.reshape(depth, NSA_HEADS, HEAD_DIM, D)
    z = jnp.zeros_like(w)
    g = (np.arange(NSA_HEADS) // NSA_HPG)[None, :, None, None]
    lo = jnp.where(g == 0, w, z)
    hi = jnp.where(g == 1, w, z)
    return jnp.concatenate([lo, hi], axis=2).reshape(depth, NSA_OUT, D).astype(BF16)


FFN_CHUNK = 512


def _ffn_kernel(x_ref, nw_ref, sc_ref, sh_ref, g2_ref, w1_ref, w3_ref, w2_ref, o_ref):
    x = x_ref[...]
    h = _norm_mod(x, nw_ref[...], sc_ref[...], sh_ref[...]).astype(BF16)
    F = w1_ref.shape[1]
    y = None
    for c0 in range(0, F, FFN_CHUNK):
        cols = slice(c0, min(c0 + FFN_CHUNK, F))
        u = _dot(h, w1_ref[:, cols])
        v = _dot(h, w3_ref[:, cols])
        part = _dot((u * _sigmoid(u) * v).astype(BF16), w2_ref[cols, :])
        y = part if y is None else y + part
    o_ref[...] = x + g2_ref[...] * y


def ffn(x, mod_l, nw, w1, w3, w2, T, *, tm=512):
    M, D = x.shape
    F = w1.shape[1]
    per_b = T // tm
    modspec = lambda k: pl.BlockSpec((None, None, 1, D), lambda i: (i // per_b, k, 0, 0))
    full = lambda a: pl.BlockSpec(a.shape, lambda i: (0,) * a.ndim)
    return pl.pallas_call(
        _ffn_kernel,
        grid=(M // tm,),
        in_specs=[pl.BlockSpec((tm, D), lambda i: (i, 0)),
                  pl.BlockSpec((1, D), lambda i: (0, 0)),
                  modspec(4), modspec(3), modspec(5), full(w1), full(w3), full(w2)],
        out_specs=pl.BlockSpec((tm, D), lambda i: (i, 0)),
        out_shape=jax.ShapeDtypeStruct((M, D), F32),
        compiler_params=_cparams("parallel"),
        name="ffn_dense",
    )(x, nw, mod_l, mod_l, mod_l, w1, w3, w2)


MOE_TC = 512
MOE_TS = 512


def _router_kernel(x_ref, nw_ref, sc_ref, sh_ref, wh_ref, wl_ref, h_ref, gate_ref, rank_ref, cnt_ref, carry_ref):
    @pl.when(pl.program_id(0) == 0)
    def _():
        carry_ref[...] = jnp.zeros_like(carry_ref)

    h = _norm_mod(x_ref[...], nw_ref[...], sc_ref[...], sh_ref[...])
    hh = h.astype(BF16)
    h_ref[...] = _pack_bf16_pairs(hh.astype(F32))
    hl = (h - hh.astype(F32)).astype(BF16)
    logits = _dot(hh, wh_ref[...]) + (_dot(hl, wh_ref[...]) + _dot(hh, wl_ref[...]))
    tm = logits.shape[0]
    lane = lax.broadcasted_iota(jnp.int32, logits.shape, 1)
    logits = jnp.where(lane < N_EXPERTS, logits, REMOVED)
    lane_f = lane.astype(F32)
    v1 = jnp.max(logits, axis=-1, keepdims=True)
    i1 = jnp.min(jnp.where(logits == v1, lane_f, float(LANES)), axis=-1, keepdims=True)
    rest = jnp.where(lane_f == i1, REMOVED, logits)
    v2 = jnp.max(rest, axis=-1, keepdims=True)
    i2 = jnp.min(jnp.where(rest == v2, lane_f, float(LANES)), axis=-1, keepdims=True)
    e2 = jnp.exp(v2 - v1)
    w1 = 1.0 / (1.0 + e2)
    w2 = e2 / (1.0 + e2)
    gate_ref[...] = jnp.where(lane_f == i1, w1, jnp.where(lane_f == i2, w2, 0.0))

    sel = jnp.where((lane_f == i1) | (lane_f == i2), 1.0, 0.0)
    ri = lax.broadcasted_iota(jnp.int32, (tm, tm), 0)
    ci = lax.broadcasted_iota(jnp.int32, (tm, tm), 1)
    before = jnp.where(ci < ri, 1.0, 0.0).astype(BF16)
    rank = _dot(before, sel.astype(BF16)) + carry_ref[0:1, :]
    rank_ref[...] = jnp.where(sel > 0.0, rank, -1.0)
    carry_ref[...] = carry_ref[...] + jnp.sum(sel, axis=0, keepdims=True)
    cnt_ref[...] = carry_ref[...]


def router(x, mod_l, nw, w_router, T):
    M, D = x.shape
    tm = MOE_TC
    per_b = T // tm
    wp = jnp.zeros((D, LANES), F32).at[:, :N_EXPERTS].set(w_router)
    wh = wp.astype(BF16)
    wl = (wp - wh.astype(F32)).astype(BF16)
    return pl.pallas_call(
        _router_kernel,
        grid=(M // tm,),
        in_specs=[pl.BlockSpec((tm, D), lambda i: (i, 0)),
                  pl.BlockSpec((1, D), lambda i: (0, 0))]
        + _mod_specs(T, tm, 4, 3, 1)
        + [pl.BlockSpec((D, LANES), lambda i: (0, 0)),
           pl.BlockSpec((D, LANES), lambda i: (0, 0))],
        out_specs=[pl.BlockSpec((tm, D // 2), lambda i: (i, 0)),
                   pl.BlockSpec((tm, LANES), lambda i: (i, 0)),
                   pl.BlockSpec((tm, LANES), lambda i: (i, 0)),
                   pl.BlockSpec((8, LANES), lambda i: (0, 0))],
        out_shape=[jax.ShapeDtypeStruct((M, D // 2), jnp.uint32),
                   jax.ShapeDtypeStruct((M, LANES), F32),
                   jax.ShapeDtypeStruct((M, LANES), F32),
                   jax.ShapeDtypeStruct((8, LANES), F32)],
        scratch_shapes=[pltpu.VMEM((8, LANES), F32)],
        compiler_params=_cparams("arbitrary"),
        name="moe_router",
    )(x, nw, mod_l, mod_l, wh, wl)


def _count_le(sorted_vals, x):
    return jnp.sum(sorted_vals[None, :] <= x[:, None], axis=1, dtype=jnp.int32)


def _moe_up_kernel(e_r, total, x_ref, w1_ref, w3_ref, o_ref, w1b_ref, w3b_ref):
    r = pl.program_id(1)
    live = r < total[0]

    @pl.when(live & ((r == 0) | (e_r[r] != e_r[jnp.maximum(r - 1, 0)])))
    def _():
        w1b_ref[...] = w1_ref[...].astype(BF16)
        w3b_ref[...] = w3_ref[...].astype(BF16)

    @pl.when(live)
    def _():
        x = _unpack_bf16_pairs(x_ref[...]).astype(BF16)
        u = _dot(x, w1b_ref[...])
        v = _dot(x, w3b_ref[...])
        o_ref[...] = (u * _sigmoid(u) * v).astype(o_ref.dtype)


def moe_up(xs, w1, w3, tiles, rt, *, tf=1792):
    R = xs.shape[0]
    D = w1.shape[1]
    ts = MOE_TS
    F = w1.shape[-1]
    live = lambda r, total: jnp.minimum(r, total[0] - 1)
    return pl.pallas_call(
        _moe_up_kernel,
        grid_spec=pltpu.PrefetchScalarGridSpec(
            num_scalar_prefetch=2,
            grid=(F // tf, rt),
            in_specs=[pl.BlockSpec((ts, D // 2), lambda n, r, e, total: (live(r, total), 0)),
                      pl.BlockSpec((None, D, tf), lambda n, r, e, total: (e[live(r, total)], 0, n)),
                      pl.BlockSpec((None, D, tf), lambda n, r, e, total: (e[live(r, total)], 0, n))],
            out_specs=pl.BlockSpec((ts, tf), lambda n, r, e, total: (r, n)),
            scratch_shapes=[pltpu.VMEM((D, tf), BF16), pltpu.VMEM((D, tf), BF16)],
        ),
        out_shape=jax.ShapeDtypeStruct((R, F), BF16),
        compiler_params=_cparams("arbitrary", "arbitrary"),
        name="moe_up",
    )(tiles["e"], tiles["total"], xs, w1, w3)


def _moe_down_kernel(e_r, total, a_ref, w2_ref, o_ref, w2b_ref):
    r = pl.program_id(0)
    live = r < total[0]

    @pl.when(live & ((r == 0) | (e_r[r] != e_r[jnp.maximum(r - 1, 0)])))
    def _():
        w2b_ref[...] = w2_ref[...].astype(BF16)

    @pl.when(live)
    def _():
        o_ref[...] = _pack_bf16_pairs(_dot(a_ref[...], w2b_ref[...]))


def moe_down(a, w2, tiles, rt):
    R, F = a.shape
    ts = MOE_TS
    D = w2.shape[-1]
    live = lambda r, total: jnp.minimum(r, total[0] - 1)
    return pl.pallas_call(
        _moe_down_kernel,
        grid_spec=pltpu.PrefetchScalarGridSpec(
            num_scalar_prefetch=2,
            grid=(rt,),
            in_specs=[pl.BlockSpec((ts, F), lambda r, e, total: (live(r, total), 0)),
                      pl.BlockSpec((None, F, D), lambda r, e, total: (e[live(r, total)], 0, 0))],
            out_specs=pl.BlockSpec((ts, D // 2), lambda r, e, total: (r, 0)),
            scratch_shapes=[pltpu.VMEM((F, D), BF16)],
        ),
        out_shape=jax.ShapeDtypeStruct((R, D // 2), jnp.uint32),
        compiler_params=_cparams("arbitrary"),
        name="moe_down",
    )(tiles["e"], tiles["total"], a, w2)


SC_WINDOW = 64


def _sc_mesh():
    return plsc.VectorSubcoreMesh(core_axis_name="core", subcore_axis_name="subcore")


def sc_scatter_rows2(x, idx_a, idx_b, n_out):
    n, d = x.shape
    steps = n // SC_WINDOW

    @pl.kernel(out_type=jax.ShapeDtypeStruct((n_out, d), x.dtype), mesh=_sc_mesh(), scratch_types=[])
    def kern(x_hbm, ia_hbm, ib_hbm, o_hbm):
        def body(x_vmem, ia_vmem, ib_vmem):
            pltpu.sync_copy(x_vmem, o_hbm.at[ia_vmem.at[0]])
            pltpu.sync_copy(x_vmem, o_hbm.at[ib_vmem.at[0]])

        pltpu.emit_pipeline(
            body,
            grid=(steps,),
            in_specs=[pl.BlockSpec((SC_WINDOW, d), index_map=lambda i: (i, 0)),
                      pl.BlockSpec((1, SC_WINDOW), index_map=lambda i: (i, 0)),
                      pl.BlockSpec((1, SC_WINDOW), index_map=lambda i: (i, 0))],
            out_specs=[],
            core_axis_name=("core", "subcore"),
            dimension_semantics=(pltpu.PARALLEL,),
        )(x_hbm, ia_hbm, ib_hbm)

    return kern(x, idx_a.reshape(steps, SC_WINDOW), idx_b.reshape(steps, SC_WINDOW))


def sc_gather_rows(x, idx):
    n = idx.shape[0]
    d = x.shape[1]
    steps = n // SC_WINDOW

    @pl.kernel(out_type=jax.ShapeDtypeStruct((n, d), x.dtype), mesh=_sc_mesh(), scratch_types=[])
    def kern(x_hbm, i_hbm, o_hbm):
        def body(i_vmem, o_vmem):
            pltpu.sync_copy(x_hbm.at[i_vmem.at[0]], o_vmem)

        pltpu.emit_pipeline(
            body,
            grid=(steps,),
            in_specs=[pl.BlockSpec((1, SC_WINDOW), index_map=lambda i: (i, 0))],
            out_specs=[pl.BlockSpec((SC_WINDOW, d), index_map=lambda i: (i, 0))],
            core_axis_name=("core", "subcore"),
            dimension_semantics=(pltpu.PARALLEL,),
        )(i_hbm, o_hbm)

    return kern(x, idx.reshape(steps, SC_WINDOW))


def _moe_finish_kernel(x_ref, g2_ref, ya_ref, yb_ref, gate_ref, rank_ref, nw_ref, o_ref, *, normalize):
    gate = gate_ref[...]
    chosen = rank_ref[...] >= 0.0
    lane = lax.broadcasted_iota(jnp.int32, gate.shape, 1).astype(F32)
    first = jnp.min(jnp.where(chosen, lane, float(LANES)), axis=-1, keepdims=True)
    last = jnp.max(jnp.where(chosen, lane, -1.0), axis=-1, keepdims=True)
    wa = jnp.sum(jnp.where(lane == first, gate, 0.0), axis=-1, keepdims=True)
    wb = jnp.sum(jnp.where(lane == last, gate, 0.0), axis=-1, keepdims=True)
    x = x_ref[...] + g2_ref[...] * (wa * _unpack_bf16_pairs(ya_ref[...]) + wb * _unpack_bf16_pairs(yb_ref[...]))
    if normalize:
        ms = jnp.mean(x * x, axis=-1, keepdims=True)
        x = x * lax.rsqrt(ms + NORM_EPS) * nw_ref[...]
    o_ref[...] = x


def moe_finish(x, mod_l, y2, gate, rank, norm_w, T, *, tm=512):
    M, D = x.shape
    per_b = T // tm
    normalize = norm_w is not None
    if norm_w is None:
        norm_w = jnp.ones((1, D), F32)
    return pl.pallas_call(
        functools.partial(_moe_finish_kernel, normalize=normalize),
        grid=(M // tm,),
        in_specs=[pl.BlockSpec((tm, D), lambda i: (i, 0)),
                  pl.BlockSpec((None, None, 1, D), lambda i: (i // per_b, 5, 0, 0)),
                  pl.BlockSpec((None, tm, D // 2), lambda i: (0, i, 0)),
                  pl.BlockSpec((None, tm, D // 2), lambda i: (1, i, 0)),
                  pl.BlockSpec((tm, LANES), lambda i: (i, 0)),
                  pl.BlockSpec((tm, LANES), lambda i: (i, 0)),
                  pl.BlockSpec((1, D), lambda i: (0, 0))],
        out_specs=pl.BlockSpec((tm, D), lambda i: (i, 0)),
        out_shape=jax.ShapeDtypeStruct((M, D), F32),
        compiler_params=_cparams("parallel"),
        name="moe_finish",
    )(x, mod_l, y2, y2, gate, rank, norm_w)


def moe_ffn(x, mod_l, nw, w_router, w1, w3, w2, T, norm_w=None):
    M = x.shape[0]
    ts = MOE_TS
    rt = (2 * M) // ts + N_EXPERTS
    h, gate, rank, cnt = router(x, mod_l, nw, w_router, T)
    i32 = jnp.int32
    counts = cnt[0, :N_EXPERTS].astype(i32)
    ntile = (counts + ts - 1) // ts
    tile_end = jnp.cumsum(ntile)
    row_off = (tile_end - ntile) * ts
    e_r = jnp.minimum(_count_le(tile_end, jnp.arange(rt, dtype=i32)), N_EXPERTS - 1)
    tiles = dict(e=e_r, total=tile_end[-1].reshape(1).astype(i32))
    rk = rank[:, :N_EXPERTS].astype(i32)
    pos = row_off[None, :] + rk
    pos_a = jnp.min(jnp.where(rk >= 0, pos, rt * ts), axis=1)
    pos_b = jnp.max(jnp.where(rk >= 0, pos, -1), axis=1)

    xs = sc_scatter_rows2(h, pos_a, pos_b, rt * ts)
    a = moe_up(xs, w1, w3, tiles, rt)
    y = moe_down(a, w2, tiles, rt)
    y2 = sc_gather_rows(y, jnp.concatenate([pos_a, pos_b])).reshape(2, M, -1)
    return moe_finish(x, mod_l, y2, gate, rank, norm_w, T)


def _final_norm_kernel(x_ref, w_ref, o_ref):
    x = x_ref[...]
    ms = jnp.mean(x * x, axis=-1, keepdims=True)
    o_ref[...] = x * lax.rsqrt(ms + NORM_EPS) * w_ref[...]


def final_norm(x, w, *, tm=1024):
    M, D = x.shape
    return pl.pallas_call(
        _final_norm_kernel,
        grid=(M // tm,),
        in_specs=[pl.BlockSpec((tm, D), lambda i: (i, 0)), pl.BlockSpec((1, D), lambda i: (0, 0))],
        out_specs=pl.BlockSpec((tm, D), lambda i: (i, 0)),
        out_shape=jax.ShapeDtypeStruct((M, D), F32),
        compiler_params=_cparams("parallel"),
        name="final_norm",
    )(x, w)


def nsa_constants(T):
    n_sel = T // SEL_LEN
    nsp = max(LANES, n_sel)
    ncp = T // CMP_STRIDE
    cmp_start = np.arange(ncp) * CMP_STRIDE
    sel_start = np.arange(nsp) * SEL_LEN
    ov = ((cmp_start[:, None] < sel_start[None, :] + SEL_LEN)
          & (cmp_start[:, None] + CMP_LEN > sel_start[None, :]))
    ov[(T - CMP_LEN) // CMP_STRIDE + 1:] = False
    ov[:, n_sel:] = False
    et_mat = ((np.arange(T)[:, None] // SEL_LEN) == np.arange(nsp)[None, :]) * SEL_BONUS
    return jnp.asarray(ov.T, BF16), jnp.asarray(et_mat, BF16)


def token_mixing(x, mod_l, lw, consts, B, T):
    M = B * T
    cos_t, sin_t, ov_t, e_mat, ret_consts, ex = consts
    l = lw["layer"]
    p1 = proj_rope(x, mod_l, lw["norm_mix"], lw["w1"], l, cos_t, sin_t, p1_scales(), T).reshape(B, T, P1_COLS)
    p2 = proj_plain(x, mod_l, lw["norm_mix"], lw["w2"], l, T).reshape(B, T, P2_COLS)

    def group_rows(a):
        return a.reshape(B, T, NSA_GROUPS, HEAD_DIM).transpose(0, 2, 1, 3).reshape(
            B, NSA_GROUPS, T // CMP_STRIDE, CMP_STRIDE * HEAD_DIM)

    xr = jnp.stack([group_rows(p1[:, :, P1_NKC:P1_NKC + LANES]), group_rows(p2[:, :, P2_NVC:P2_NVC + LANES])])
    cmp_out = compress(xr, lw["cmp_pe"], lw["cmp_w1"], lw["cmp_w2"])
    cmp_out = cmp_out.transpose(0, 1, 3, 2, 4).reshape(2, B, T // CMP_STRIDE, LANES)
    o_cmp, sel, o_win = nsa_cmp_select_window(p1, p2, cmp_out[0], cmp_out[1], ov_t, T)
    o_sel = nsa_selected(p1, nsa_value_augment(p2[:, :, P2_NVS:P2_NVS + LANES]), sel, e_mat, T)

    o_ret = retention(p1, p2, ret_consts, T)

    ff = p2[:, :, P2_SMALL + 3 * NSA_HEADS:P2_SMALL + 3 * NSA_HEADS + FOX_HEADS].astype(F32)
    ff = ff.transpose(0, 2, 1).reshape(B, FOX_HEADS, T // LANES, LANES)
    cum = fox_cum(ff, lw["fox_bias"]).reshape(B, FOX_HEADS // 2, 2, 1, T)
    o_fox = fox_attention(p2, cum, T)

    return readout(o_cmp.reshape(M, -1), o_sel.reshape(M, -1), o_win.reshape(M, -1), p2.reshape(M, P2_COLS),
                   o_ret.reshape(M, -1), o_fox.reshape(M, -1), x, mod_l, ex,
                   lw["wn"], lw["wr"], lw["wf"], lw["wo"], l, T)


def mixer_weights(norm_mix, w_in, cmp_k_pe, cmp_k_w1, cmp_k_w2, cmp_v_pe, cmp_v_w1, cmp_v_w2, fox_f_bias,
                  w_read_nsa, w_read_ret, w_read_fox, w_out):
    depth = w_in.shape[0]
    w1, w2 = split_w_in(w_in)
    pe = jnp.stack([cmp_k_pe.reshape(depth, 1, -1), cmp_v_pe.reshape(depth, 1, -1)], axis=1)
    pe = jnp.broadcast_to(pe, (depth, 2, 8, pe.shape[-1])).astype(BF16)
    shared = {
        "w1": w1, "w2": w2,
        "wn": pad_read_nsa(w_read_nsa),
        "wr": w_read_ret.astype(BF16),
        "wf": w_read_fox.astype(BF16),
        "wo": w_out.astype(BF16),
    }
    cmp_w1 = jnp.stack([cmp_k_w1, cmp_v_w1], axis=1).astype(BF16)
    cmp_w2 = jnp.stack([cmp_k_w2, cmp_v_w2], axis=1).astype(BF16)
    return [dict(shared, layer=l, norm_mix=norm_mix[l].reshape(1, -1), cmp_pe=pe[l], cmp_w1=cmp_w1[l], cmp_w2=cmp_w2[l],
                 fox_bias=jnp.broadcast_to(fox_f_bias[l][:, None, None], (FOX_HEADS, 1, LANES)))
            for l in range(depth)]


def kernel(x, c, ada_w, ada_b, norm_mix, norm_ffn, w_in, cmp_k_pe, cmp_k_w1, cmp_k_w2, cmp_v_pe, cmp_v_w1,
           cmp_v_w2, fox_f_bias, w_read_nsa, w_read_ret, w_read_fox, w_out, ffn_w1, ffn_w3, ffn_w2, router_w,
           moe_w1, moe_w3, moe_w2, final_norm_w):
    B, T, D = x.shape
    M = B * T
    depth = ada_w.shape[0]
    mod = modulation(c, ada_w, ada_b)
    cos_t, sin_t = rope_tables(T)
    ov_t, e_mat = nsa_constants(T)
    consts = (cos_t, sin_t, ov_t, e_mat, retention_consts(), nsa_gate_expand())
    xs = x.reshape(M, D)
    lws = mixer_weights(norm_mix, w_in, cmp_k_pe, cmp_k_w1, cmp_k_w2, cmp_v_pe, cmp_v_w1, cmp_v_w2,
                        fox_f_bias, w_read_nsa, w_read_ret, w_read_fox, w_out)
    for l in range(depth):
        xs = token_mixing(xs, mod[l], lws[l], consts, B, T)
        nf = norm_ffn[l].reshape(1, D)
        if l % 2 == 0:
            k = l // 2
            xs = ffn(xs, mod[l], nf, ffn_w1[k].astype(BF16), ffn_w3[k].astype(BF16), ffn_w2[k].astype(BF16), T)
        else:
            k = l // 2
            fuse = final_norm_w.reshape(1, D) if l == depth - 1 else None
            xs = moe_ffn(xs, mod[l], nf, router_w[k], moe_w1[k], moe_w3[k], moe_w2[k], T, fuse)
    if depth % 2 == 1:
        xs = final_norm(xs, final_norm_w.reshape(1, D))
    return xs.reshape(B, T, D)
```

```python
import functools
import math

import jax
import jax.numpy as jnp
import numpy as np
from jax import lax
from jax.experimental import pallas as pl
from jax.experimental.pallas import tpu as pltpu
from jax.experimental.pallas import tpu_sc as plsc

F32 = jnp.float32
BF16 = jnp.bfloat16

D_MODEL = 1024
DEPTH = 2
HEAD_DIM = 64
ROPE_THETA = 10000.0
NORM_EPS = 1e-6
NEG_INF = -1e30
REMOVED = -3e38

NSA_HEADS = 8
NSA_GROUPS = 2
NSA_HPG = NSA_HEADS // NSA_GROUPS
CMP_LEN = 32
CMP_STRIDE = 16
CMP_HIDDEN = 256
SEL_LEN = 64
SEL_TOPN = 16
WINDOW = 512
FORCE_SCORE = 1e4
NSA_QBLOCK = 256

RET_HEADS = 4
RET_QK_DIM = 64
RET_V_DIM = 128
RET_CHUNK = 128

FOX_HEADS = 8
FOX_TQ = 1024
LOG2E = 1.4426950408889634

D_FF = 2816
N_EXPERTS = 8
D_FF_EXPERT = 3584

LANES = 128
VMEM_LIMIT = 56 * 1024 * 1024

P1_NQ = 0
P1_RQ = 512
P1_RK = 768
P1_NKC = 1024
P1_NKS = 1152
P1_NKW = 1280
P1_COLS = 1408
P2_MG = 0
P2_RV = 3072
P2_RG = 3584
P2_FQ = 4096
P2_FK = 4608
P2_FV = 5120
P2_NVC = 5632
P2_NVS = 5760
P2_NVW = 5888
P2_SMALL = 6016
P2_COLS = 6144
NSA_OUT = NSA_HEADS * LANES


def _layer_spec(w, l, nidx):
    zeros = (0,) * (w.ndim - 1)
    return pl.BlockSpec((None,) + w.shape[1:], lambda *_: (l,) + zeros)


def _cparams(*sem):
    return pltpu.CompilerParams(dimension_semantics=tuple(sem), vmem_limit_bytes=VMEM_LIMIT)


def _sigmoid(x):
    return 1.0 / (1.0 + jnp.exp(-x))


def _dot(a, b):
    return jnp.dot(a, b, preferred_element_type=F32)


def _dot_nt(a, b):
    return lax.dot_general(a, b, (((1,), (1,)), ((), ())), preferred_element_type=F32)


def _dot_tn(a, b):
    return lax.dot_general(a, b, (((0,), (0,)), ((), ())), preferred_element_type=F32)


def _split3(x):
    hi = x.astype(BF16)
    r1 = x - hi.astype(F32)
    mid = r1.astype(BF16)
    lo = (r1 - mid.astype(F32)).astype(BF16)
    return hi, mid, lo


def _pack_bf16_pairs(x):
    c = x.shape[1] // 2
    lo = pltpu.bitcast(x[:, :c].astype(BF16).astype(F32), jnp.uint32) >> 16
    hi = pltpu.bitcast(x[:, c:].astype(BF16).astype(F32), jnp.uint32) & jnp.uint32(0xFFFF0000)
    return hi | lo


def _unpack_bf16_pairs(u):
    lo = pltpu.bitcast(u << 16, F32)
    hi = pltpu.bitcast(u & jnp.uint32(0xFFFF0000), F32)
    return jnp.concatenate([lo, hi], axis=1)


def _norm_mod(x, nw, sc, sh):
    ms = jnp.mean(x * x, axis=-1, keepdims=True)
    y = x * lax.rsqrt(ms + NORM_EPS) * nw
    return y * (1.0 + sc) + sh


def _mod_kernel(c_ref, w_ref, b_ref, o_ref):
    c = c_ref[...]
    s = c * _sigmoid(c)
    o_ref[0] = _dot(s.astype(BF16), w_ref[0].astype(BF16)) + b_ref[0]


def modulation(c, ada_w, ada_b):
    B, D = c.shape
    depth = ada_w.shape[0]
    rows = 8
    c_pad = jnp.zeros((rows, D), F32).at[:B].set(c)
    out = pl.pallas_call(
        _mod_kernel,
        grid=(depth, 6),
        in_specs=[pl.BlockSpec((rows, D), lambda l, j: (0, 0)),
                  pl.BlockSpec((1, D, D), lambda l, j: (l, 0, j)),
                  pl.BlockSpec((1, 1, D), lambda l, j: (l, 0, j))],
        out_specs=pl.BlockSpec((1, rows, D), lambda l, j: (l, 0, j)),
        out_shape=jax.ShapeDtypeStruct((depth, rows, 6 * D), F32),
        compiler_params=_cparams("parallel", "parallel"),
        name="modulation",
    )(c_pad, ada_w, ada_b.reshape(depth, 1, 6 * D))
    return out[:, :B].reshape(depth, B, 6, 1, D)


def _proj_plain_kernel(x_ref, nw_ref, sc_ref, sh_ref, w_ref, o_ref, *, tn):
    h = _norm_mod(x_ref[...], nw_ref[...], sc_ref[...], sh_ref[...]).astype(BF16)
    for n in range(w_ref.shape[1] // tn):
        cols = slice(n * tn, (n + 1) * tn)
        o_ref[:, cols] = _dot(h, w_ref[:, cols]).astype(o_ref.dtype)


def _proj_rope_kernel(x_ref, nw_ref, sc_ref, sh_ref, w_ref, cos_ref, sin_ref, o_ref, *, scales):
    h = _norm_mod(x_ref[...], nw_ref[...], sc_ref[...], sh_ref[...]).astype(BF16)
    y = _dot(h, w_ref[...])
    cos = cos_ref[...]
    sin = sin_ref[...]
    lane = lax.broadcasted_iota(jnp.int32, cos.shape, 1)
    first_half = (lane % HEAD_DIM) < (HEAD_DIM // 2)
    for g, scale in enumerate(scales):
        yg = y[:, g * LANES:(g + 1) * LANES]
        rot = jnp.where(first_half, pltpu.roll(yg, LANES - HEAD_DIM // 2, 1),
                        pltpu.roll(yg, HEAD_DIM // 2, 1))
        r = yg * cos + rot * sin
        if scale != 1.0:
            r = r * scale
        o_ref[:, g * LANES:(g + 1) * LANES] = r.astype(o_ref.dtype)


def _mod_specs(T, tm, sc_idx, sh_idx, nargs):
    per_b = T // tm
    if nargs == 1:
        return [pl.BlockSpec((None, None, 1, D_MODEL), lambda i: (i // per_b, sc_idx, 0, 0)),
                pl.BlockSpec((None, None, 1, D_MODEL), lambda i: (i // per_b, sh_idx, 0, 0))]
    return [pl.BlockSpec((None, None, 1, D_MODEL), lambda i, j: (i // per_b, sc_idx, 0, 0)),
            pl.BlockSpec((None, None, 1, D_MODEL), lambda i, j: (i // per_b, sh_idx, 0, 0))]


def proj_plain(x, mod_l, nw, w, l, T, *, tm=512, tn=512):
    M, D = x.shape
    N = w.shape[-1]
    return pl.pallas_call(
        functools.partial(_proj_plain_kernel, tn=tn),
        grid=(M // tm,),
        in_specs=[pl.BlockSpec((tm, D), lambda i: (i, 0)),
                  pl.BlockSpec((1, D), lambda i: (0, 0))]
        + _mod_specs(T, tm, 1, 0, 1)
        + [_layer_spec(w, l, 1)],
        out_specs=pl.BlockSpec((tm, N), lambda i: (i, 0)),
        out_shape=jax.ShapeDtypeStruct((M, N), BF16),
        compiler_params=_cparams("parallel"),
        name="proj_plain",
    )(x, nw, mod_l, mod_l, w)


def proj_rope(x, mod_l, nw, w, l, cos, sin, scales, T, *, tm=512):
    M, D = x.shape
    N = w.shape[-1]
    per_b = T // tm
    return pl.pallas_call(
        functools.partial(_proj_rope_kernel, scales=scales),
        grid=(M // tm,),
        in_specs=[pl.BlockSpec((tm, D), lambda i: (i, 0)),
                  pl.BlockSpec((1, D), lambda i: (0, 0))]
        + _mod_specs(T, tm, 1, 0, 1)
        + [_layer_spec(w, l, 1),
           pl.BlockSpec((tm, LANES), lambda i: (i % per_b, 0)),
           pl.BlockSpec((tm, LANES), lambda i: (i % per_b, 0))],
        out_specs=pl.BlockSpec((tm, N), lambda i: (i, 0)),
        out_shape=jax.ShapeDtypeStruct((M, N), BF16),
        compiler_params=_cparams("parallel"),
        name="proj_rope",
    )(x, nw, mod_l, mod_l, w, cos, sin)


def rope_tables(T):
    d = HEAD_DIM
    pos = jnp.arange(T, dtype=F32)
    inv = ROPE_THETA ** (-jnp.arange(0, d, 2, dtype=F32) / d)
    ang = pos[:, None] * inv[None, :]
    cos = jnp.cos(ang)
    sin = jnp.sin(ang)
    cos_t = jnp.concatenate([cos, cos, cos, cos], axis=-1)
    sin_t = jnp.concatenate([-sin, sin, -sin, sin], axis=-1)
    return cos_t, sin_t


def split_w_in(w_in):
    sizes = [512, 128, 128, 128, 128, 128, 128, 24, 256, 256, 512, 512, 512, 512, 512, 8, 3072]
    offs = np.cumsum([0] + sizes)
    wb = w_in.astype(BF16)
    (nq, nkc, nvc, nks, nvs, nkw, nvw, ngate, rq, rk, rv, rg, fq, fk, fv, ff, mg) = [
        wb[..., offs[i]:offs[i + 1]] for i in range(len(sizes))]
    small = jnp.concatenate([ngate, ff, jnp.zeros(ngate.shape[:-1] + (LANES - 32,), BF16)], axis=-1)
    w1 = jnp.concatenate([nq, rq, rk, nkc, nks, nkw], axis=-1)
    w2 = jnp.concatenate([mg, rv, rg, fq, fk, fv, nvc, nvs, nvw, small], axis=-1)
    assert w1.shape[-1] == P1_COLS and w2.shape[-1] == P2_COLS
    return w1, w2


def p1_scales():
    s = [1.0] * (P1_COLS // LANES)
    for g in range(P1_NQ // LANES, P1_RQ // LANES):
        s[g] = HEAD_DIM ** -0.5 * LOG2E
    for g in range(P1_RK // LANES, P1_NKC // LANES):
        s[g] = RET_QK_DIM ** -0.5
    return tuple(s)


def _compress_kernel(x_ref, pe_ref, w1_ref, w2_ref, o_ref):
    r = x_ref[...]
    half = r.shape[1]
    w1 = w1_ref[...]
    a = _dot(r, w1[:half])
    b = _dot(r, w1[half:])
    pe = _dot(pe_ref[...], w1)[0:1]
    n = a.shape[0]
    hid = a + pltpu.roll(b, n - 1, 0) + pe
    hid = hid * _sigmoid(hid)
    o_ref[...] = _dot(hid.astype(BF16), w2_ref[...]).astype(o_ref.dtype)


def compress(xr, pe, w1, w2):
    _, B, G, R, W = xr.shape
    H = w1.shape[-1]
    return pl.pallas_call(
        _compress_kernel,
        grid=(2, B, G),
        in_specs=[pl.BlockSpec((None, None, None, R, W), lambda s, b, g: (s, b, g, 0, 0)),
                  pl.BlockSpec((None, 8, 2 * W), lambda s, b, g: (s, 0, 0)),
                  pl.BlockSpec((None, 2 * W, H), lambda s, b, g: (s, 0, 0)),
                  pl.BlockSpec((None, H, HEAD_DIM), lambda s, b, g: (s, 0, 0))],
        out_specs=pl.BlockSpec((None, None, None, R, HEAD_DIM), lambda s, b, g: (s, b, g, 0, 0)),
        out_shape=jax.ShapeDtypeStruct((2, B, G, R, HEAD_DIM), BF16),
        compiler_params=_cparams("parallel", "parallel", "parallel"),
        name="nsa_compress",
    )(xr, pe, w1, w2)


def _stack_heads(q_ref, g):
    tq = q_ref.shape[0]
    half = lax.broadcasted_iota(jnp.int32, (tq, LANES), 1) // HEAD_DIM
    rows = []
    for hh in range(NSA_HPG):
        h = NSA_HPG * g + hh
        x = q_ref[:, (h // 2) * LANES:(h // 2 + 1) * LANES].astype(F32)
        if h % 2 != g:
            x = pltpu.roll(x, HEAD_DIM, 1)
        rows.append(jnp.where(half == g, x, 0.0).astype(BF16))
    return jnp.concatenate(rows, axis=0)


def _store_heads(o_ref, g, o, tq):
    for hh in range(NSA_HPG):
        h = NSA_HPG * g + hh
        o_ref[:, h * LANES:(h + 1) * LANES] = o[hh * tq:(hh + 1) * tq].astype(o_ref.dtype)


CMP_CHUNK = 128


def _nsa_cmp_kernel(q_ref, kc_ref, vc_ref, ov_ref, o_ref, m_ref, imp_ref, *, tq, n_sel, top_n):
    t0 = pl.program_id(1) * tq
    ncp = kc_ref.shape[0]
    nsp = ov_ref.shape[0]
    rows = NSA_HPG * tq

    def attend(ncols):
        kc = kc_ref[0:ncols, :]
        vc = vc_ref[0:ncols, :]
        n_idx = lax.broadcasted_iota(jnp.int32, (rows, ncols), 1)
        t_idx = t0 + lax.broadcasted_iota(jnp.int32, (rows, ncols), 0) % tq
        valid = (n_idx * CMP_STRIDE + (CMP_LEN - 1)) <= t_idx
        for g in range(NSA_GROUPS):
            q = _stack_heads(q_ref, g)
            s = jnp.where(valid, _dot_nt(q, kc), NEG_INF)
            m = jnp.max(s, axis=-1, keepdims=True)
            e = jnp.exp2(s - m)
            l = jnp.sum(e, axis=-1, keepdims=True)
            p = e * jnp.where(m > 0.5 * NEG_INF, 1.0 / l, 0.0)
            _store_heads(o_ref, g, _dot(p.astype(BF16), vc), tq)
            psum = p[0:tq]
            for hh in range(1, NSA_HPG):
                psum = psum + p[hh * tq:(hh + 1) * tq]
            imp_ref[g] = _dot_nt(ov_ref[:, 0:ncols], psum.astype(BF16))

    n_live = jnp.maximum((t0 + tq - CMP_LEN) // CMP_STRIDE + 1, 1)
    n_chunks = jnp.minimum((n_live + CMP_CHUNK - 1) // CMP_CHUNK, ncp // CMP_CHUNK)
    for nc in range(1, ncp // CMP_CHUNK + 1):
        pl.when(n_chunks == nc)(functools.partial(attend, nc * CMP_CHUNK))

    j_idx = lax.broadcasted_iota(jnp.int32, (nsp, tq), 0)
    cur = (t0 + lax.broadcasted_iota(jnp.int32, (nsp, tq), 1)) // SEL_LEN
    forced = (j_idx == 0) | (j_idx == cur) | (j_idx == cur - 1)
    j_f = j_idx.astype(F32)
    for g in range(NSA_GROUPS):
        score = jnp.where(j_idx <= cur, imp_ref[g], NEG_INF)
        score = jnp.where(forced | (j_idx >= n_sel), REMOVED, score)
        sel = jnp.where(forced, 1.0, 0.0)
        for _ in range(max(top_n - 3, 0)):
            mx = jnp.max(score, axis=0, keepdims=True)
            idx = jnp.min(jnp.where(score == mx, j_f, float(nsp)), axis=0, keepdims=True)
            hit = j_f == idx
            sel = jnp.where(hit, 1.0, sel)
            score = jnp.where(hit, REMOVED, score)
        sel = jnp.where(j_idx <= cur, sel, 0.0)
        m_ref[g] = sel.T.astype(m_ref.dtype)


SEL_BONUS = 8192.0
NSA_SEL_TQ = 256
NSA_SEL_TK = 1024


def _nsa_sel_kernel(q_ref, k_ref, v_ref, m_ref, et_ref, o_ref, *, tq, tk):
    t0 = pl.program_id(1) * tq
    n_full = t0 // tk
    rows = NSA_HPG * tq

    def update(carry, q, ks, vs, mask=None):
        m, acc = carry
        s = _dot_nt(q, ks)
        if mask is not None:
            s = jnp.where(mask, s, NEG_INF)
        m_new = jnp.maximum(m, jnp.max(s, axis=-1, keepdims=True))
        p = jnp.exp2(s - m_new)
        return m_new, jnp.exp2(m - m_new) * acc + _dot(p.astype(BF16), vs)

    qs, carries = [], []
    for g in range(NSA_GROUPS):
        q = jnp.concatenate([_stack_heads(q_ref, g), jnp.concatenate([m_ref[g]] * NSA_HPG, axis=0)], axis=1)

        def step(j, carry, q=q, g=g):
            start = pl.multiple_of(j * tk, tk)
            ks = jnp.concatenate([k_ref[pl.ds(start, tk), :], et_ref[pl.ds(start, tk), :]], axis=1)
            return update(carry, q, ks, v_ref[g, pl.ds(start, tk), :])

        init = (jnp.full((rows, 1), NEG_INF, F32), jnp.zeros((rows, LANES), F32))
        qs.append(q)
        carries.append(lax.fori_loop(0, n_full, step, init))

    start = pl.multiple_of(n_full * tk, tk)

    def tail(nk):
        trow = t0 + lax.broadcasted_iota(jnp.int32, (rows, nk), 0) % tq
        causal = start + lax.broadcasted_iota(jnp.int32, (rows, nk), 1) <= trow
        ks = jnp.concatenate([k_ref[pl.ds(start, nk), :], et_ref[pl.ds(start, nk), :]], axis=1)
        for g in range(NSA_GROUPS):
            _, acc = update(carries[g], qs[g], ks, v_ref[g, pl.ds(start, nk), :], causal)
            den = HEAD_DIM * (1 - g)
            _store_heads(o_ref, g, acc / acc[:, den:den + 1], tq)

    which = (t0 - start) // tq
    for v in range(tk // tq):
        pl.when(which == v)(functools.partial(tail, (v + 1) * tq))


def nsa_value_augment(v):
    ones = jnp.ones_like(v[..., :HEAD_DIM])
    return jnp.stack([jnp.concatenate([v[..., :HEAD_DIM], ones], axis=-1),
                      jnp.concatenate([ones, v[..., HEAD_DIM:]], axis=-1)], axis=1)


def nsa_selected(p1, v_aug, sel, et_mat, T, *, tq=NSA_SEL_TQ, tk=NSA_SEL_TK):
    B = p1.shape[0]
    nsp = sel.shape[-1]
    return pl.pallas_call(
        functools.partial(_nsa_sel_kernel, tq=tq, tk=tk),
        grid=(B, T // tq),
        in_specs=[pl.BlockSpec((None, tq, NSA_HEADS * HEAD_DIM), lambda b, i: (b, i, 0)),
                  pl.BlockSpec((None, T, LANES), lambda b, i: (b, 0, P1_NKS // LANES)),
                  pl.BlockSpec((None, NSA_GROUPS, T, LANES), lambda b, i: (b, 0, 0, 0)),
                  pl.BlockSpec((None, NSA_GROUPS, tq, nsp), lambda b, i: (b, 0, i, 0)),
                  pl.BlockSpec((T, nsp), lambda b, i: (0, 0))],
        out_specs=pl.BlockSpec((None, tq, NSA_OUT), lambda b, i: (b, i, 0)),
        out_shape=jax.ShapeDtypeStruct((B, T, NSA_OUT), BF16),
        compiler_params=_cparams("parallel", "parallel"),
        name="nsa_selected",
    )(p1, p1, v_aug, sel, et_mat)


def _nsa_win_kernel(q_ref, k_ref, v_ref, b_ref, o_ref, *, tq):
    t0 = pl.program_id(1) * tq
    span = WINDOW + tq
    start = pl.multiple_of(jnp.maximum(t0 - WINDOW, 0), tq)
    ks = k_ref[pl.ds(start, span), :]
    vs = v_ref[pl.ds(start, span), :]

    def run(bias):
        bias = jnp.concatenate([bias] * NSA_HPG, axis=0)
        for g in range(NSA_GROUPS):
            s = _dot_nt(_stack_heads(q_ref, g), ks) + bias
            m = jnp.max(s, axis=-1, keepdims=True)
            p = jnp.exp2(s - m)
            l = jnp.sum(p, axis=-1, keepdims=True)
            _store_heads(o_ref, g, _dot(p.astype(BF16), vs) / l, tq)

    @pl.when(t0 >= WINDOW)
    def _():
        run(b_ref[...])

    @pl.when(t0 < WINDOW)
    def _():
        row = lax.broadcasted_iota(jnp.int32, (tq, span), 0)
        col = lax.broadcasted_iota(jnp.int32, (tq, span), 1)
        run(jnp.where(col <= t0 + row, 0.0, NEG_INF))


def _nsa_cmp_win_kernel(q_ref, kc_ref, vc_ref, ov_ref, kw_ref, vw_ref, band_ref, ocmp_ref, m_ref, owin_ref, imp_ref,
                        *, tq, n_sel, top_n):
    _nsa_cmp_kernel(q_ref, kc_ref, vc_ref, ov_ref, ocmp_ref, m_ref, imp_ref, tq=tq, n_sel=n_sel, top_n=top_n)
    _nsa_win_kernel(q_ref, kw_ref, vw_ref, band_ref, owin_ref, tq=tq)


def nsa_cmp_select_window(p1, p2, kc, vc, ov_t, T):
    B = p1.shape[0]
    tq = NSA_QBLOCK
    ncp = kc.shape[1]
    nsp = ov_t.shape[0]
    n_sel = T // SEL_LEN
    span = WINDOW + tq
    r = np.arange(tq)[:, None]
    c = np.arange(span)[None, :]
    band = jnp.asarray(np.where((c > r) & (c <= r + WINDOW), 0.0, NEG_INF), F32)
    out_blk = pl.BlockSpec((None, tq, NSA_OUT), lambda b, i: (b, i, 0))
    return pl.pallas_call(
        functools.partial(_nsa_cmp_win_kernel, tq=tq, n_sel=n_sel, top_n=min(SEL_TOPN, n_sel)),
        grid=(B, T // tq),
        in_specs=[pl.BlockSpec((None, tq, NSA_HEADS * HEAD_DIM), lambda b, i: (b, i, 0)),
                  pl.BlockSpec((None, ncp, LANES), lambda b, i: (b, 0, 0)),
                  pl.BlockSpec((None, ncp, LANES), lambda b, i: (b, 0, 0)),
                  pl.BlockSpec((nsp, ncp), lambda b, i: (0, 0)),
                  pl.BlockSpec((None, T, LANES), lambda b, i: (b, 0, P1_NKW // LANES)),
                  pl.BlockSpec((None, T, LANES), lambda b, i: (b, 0, P2_NVW // LANES)),
                  pl.BlockSpec((tq, span), lambda b, i: (0, 0))],
        out_specs=[out_blk, pl.BlockSpec((None, NSA_GROUPS, tq, nsp), lambda b, i: (b, 0, i, 0)), out_blk],
        out_shape=[jax.ShapeDtypeStruct((B, T, NSA_OUT), BF16),
                   jax.ShapeDtypeStruct((B, NSA_GROUPS, T, nsp), BF16),
                   jax.ShapeDtypeStruct((B, T, NSA_OUT), BF16)],
        scratch_shapes=[pltpu.VMEM((NSA_GROUPS, nsp, tq), F32)],
        compiler_params=_cparams("parallel", "parallel"),
        name="nsa_cmp_select_window",
    )(p1, kc, vc, ov_t, p1, p2, band)


def _retention_kernel(q_ref, k_ref, v_ref, g_ref, din_ref, qd_ref, kd_ref, cd_ref, o_ref, st_ref):
    @pl.when(pl.program_id(0) == 0)
    def _():
        st_ref[...] = jnp.zeros_like(st_ref)

    B = q_ref.shape[0]
    C = RET_CHUNK
    half = lax.broadcasted_iota(jnp.int32, (C, LANES), 1) // HEAD_DIM
    for b in range(B):
        for h in range(RET_HEADS):
            lanes = slice(h * LANES, (h + 1) * LANES)
            pair = slice((h // 2) * LANES, (h // 2 + 1) * LANES)
            st = st_ref[b, h]
            for sub in range(q_ref.shape[1] // C):
                rows = slice(sub * C, (sub + 1) * C)
                qh = jnp.where(half == h % 2, q_ref[b, rows, pair], 0.0).astype(BF16)
                kp = k_ref[b, rows, pair]
                vh = v_ref[b, rows, lanes]
                inner = _dot_nt(qh, kp) * din_ref[h]
                o = _dot(inner.astype(BF16), vh) + _dot(qh, st.astype(BF16)) * qd_ref[h]
                kd = (kp.astype(F32) * kd_ref[h]).astype(BF16)
                st = st * cd_ref[h, 0:1, :] + _dot_tn(kd, vh)
                mu = jnp.mean(o, axis=-1, keepdims=True)
                d = o - mu
                var = jnp.mean(d * d, axis=-1, keepdims=True)
                on = d * lax.rsqrt(var + NORM_EPS)
                gh = g_ref[b, rows, lanes].astype(F32)
                o_ref[b, rows, lanes] = (gh * _sigmoid(gh) * on).astype(o_ref.dtype)
            st_ref[b, h] = st


def retention_consts():
    C = RET_CHUNK
    H = RET_HEADS
    log_g = jnp.log(1.0 - 2.0 ** (-5.0 - jnp.arange(H, dtype=F32)))
    n = jnp.arange(C, dtype=F32)
    diff = n[:, None] - n[None, :]
    causal = diff >= 0
    decay_in = jnp.where(causal[None], jnp.exp(jnp.where(causal, diff, 0.0)[None] * log_g[:, None, None]), 0.0)
    q_decay = jnp.exp((n[None, :] + 1.0) * log_g[:, None])
    k_decay = jnp.exp((C - 1.0 - n)[None, :] * log_g[:, None])
    chunk_decay = jnp.exp(C * log_g)
    qd = jnp.broadcast_to(q_decay[:, :, None], (H, C, LANES))
    kd = jnp.broadcast_to(k_decay[:, :, None], (H, C, LANES))
    cd = jnp.broadcast_to(chunk_decay[:, None, None], (H, 8, LANES))
    return decay_in, qd, kd, cd


RET_STEP = 4


def retention(p1, p2, consts, T):
    B = p1.shape[0]
    C = RET_CHUNK * RET_STEP
    din, qd, kd, cd = consts
    W = RET_HEADS * LANES
    full = lambda shape: pl.BlockSpec(shape, lambda c: (0,) * len(shape))
    return pl.pallas_call(
        _retention_kernel,
        grid=(T // C,),
        in_specs=[pl.BlockSpec((B, C, W // 2), lambda c: (0, c, P1_RQ // (W // 2))),
                  pl.BlockSpec((B, C, W // 2), lambda c: (0, c, P1_RK // (W // 2))),
                  pl.BlockSpec((B, C, W), lambda c: (0, c, P2_RV // W)),
                  pl.BlockSpec((B, C, W), lambda c: (0, c, P2_RG // W)),
                  full(din.shape), full(qd.shape), full(kd.shape), full(cd.shape)],
        out_specs=pl.BlockSpec((B, C, W), lambda c: (0, c, 0)),
        out_shape=jax.ShapeDtypeStruct((B, T, W), BF16),
        scratch_shapes=[pltpu.VMEM((B, RET_HEADS, LANES, LANES), F32)],
        compiler_params=_cparams("arbitrary"),
        name="retention",
    )(p1, p1, p2, p2, din, qd, kd, cd)


def _fox_cum_kernel(f_ref, b_ref, o_ref):
    x = f_ref[...] + b_ref[...]
    ls = jnp.minimum(x, 0.0) - jnp.log1p(jnp.exp(-jnp.abs(x)))
    R = x.shape[0]
    ki = lax.broadcasted_iota(jnp.int32, (LANES, LANES), 0)
    ji = lax.broadcasted_iota(jnp.int32, (LANES, LANES), 1)
    upper = jnp.where(ki <= ji, 1.0, 0.0).astype(BF16)
    hi, mid, lo = _split3(ls)
    rowcum = _dot(hi, upper) + _dot(mid, upper) + _dot(lo, upper)
    tot = jnp.broadcast_to(rowcum[:, LANES - 1:LANES], (R, LANES))
    ri = lax.broadcasted_iota(jnp.int32, (R, R), 0)
    ci = lax.broadcasted_iota(jnp.int32, (R, R), 1)
    lower = jnp.where(ci < ri, 1.0, 0.0).astype(BF16)
    hi, mid, lo = _split3(tot)
    offs = _dot(lower, hi) + _dot(lower, mid) + _dot(lower, lo)
    o_ref[...] = (rowcum + offs) * LOG2E


def fox_cum(f_logit, bias):
    B, H, R, _ = f_logit.shape
    return pl.pallas_call(
        _fox_cum_kernel,
        grid=(B, H),
        in_specs=[pl.BlockSpec((None, None, R, LANES), lambda b, h: (b, h, 0, 0)),
                  pl.BlockSpec((None, 1, LANES), lambda b, h: (h, 0, 0))],
        out_specs=pl.BlockSpec((None, None, R, LANES), lambda b, h: (b, h, 0, 0)),
        out_shape=jax.ShapeDtypeStruct((B, H, R, LANES), F32),
        compiler_params=_cparams("parallel", "parallel"),
        name="fox_cum",
    )(f_logit, bias)


FOX_BIAS_LANES = 3


def _fox_kernel(q_ref, k_ref, v_ref, c_ref, o_ref, ka_ref, va_ref, *, tq):
    i = pl.program_id(2)
    tk = tq
    T = k_ref.shape[0]
    chunk = 512

    @pl.when(i == 0)
    def _():
        lane = lax.broadcasted_iota(jnp.int32, (chunk, LANES), 1)
        ri = lax.broadcasted_iota(jnp.int32, (16, LANES), 0)
        ci = lax.broadcasted_iota(jnp.int32, (16, LANES), 1)
        place = jnp.where((ci == ri + HEAD_DIM) & (ri < FOX_BIAS_LANES), 1.0, 0.0).astype(BF16)

        def build(c, _):
            c0 = pl.multiple_of(c * chunk, chunk)
            kp = k_ref[pl.ds(c0, chunk), :].astype(F32)
            vp = v_ref[pl.ds(c0, chunk), :].astype(F32)
            for hh in range(2):
                hi, mid, lo = _split3(-c_ref[hh, :, pl.ds(c0, chunk)])
                terms = jnp.concatenate([hi, mid, lo, jnp.zeros((13, chunk), BF16)], axis=0)
                bias = _dot_tn(terms, place)
                kh = kp if hh == 0 else pltpu.roll(kp, HEAD_DIM, 1)
                vh = vp if hh == 0 else pltpu.roll(vp, HEAD_DIM, 1)
                ka_ref[hh, pl.ds(c0, chunk), :] = jnp.where(lane < HEAD_DIM, kh, bias).astype(BF16)
                va_ref[hh, pl.ds(c0, chunk), :] = jnp.where(lane < HEAD_DIM, vh, 1.0).astype(BF16)
            return 0

        lax.fori_loop(0, T // chunk, build, 0)

    lane = lax.broadcasted_iota(jnp.int32, (tq, LANES), 1)
    ones_lanes = (lane >= HEAD_DIM) & (lane < HEAD_DIM + FOX_BIAS_LANES)
    qp = q_ref[...].astype(F32) * (HEAD_DIM ** -0.5 * LOG2E)
    qs = [jnp.where(lane < HEAD_DIM, qh, jnp.where(ones_lanes, 1.0, 0.0)).astype(BF16)
          for qh in (qp, pltpu.roll(qp, HEAD_DIM, 1))]

    def update(hh, m, acc, q, start, size, mask=None):
        s = _dot_nt(q, ka_ref[hh, pl.ds(start, size), :])
        if mask is not None:
            s = jnp.where(mask, s, NEG_INF)
        m_new = jnp.maximum(m, jnp.max(s, axis=-1, keepdims=True))
        p = jnp.exp2(s - m_new)
        return m_new, jnp.exp2(m - m_new) * acc + _dot(p.astype(BF16), va_ref[hh, pl.ds(start, size), :])

    def step(j, carry):
        start = pl.multiple_of(j * tk, tk)
        return tuple(update(hh, *carry[hh], qs[hh], start, tk) for hh in range(2))

    one = (jnp.full((tq, 1), NEG_INF, F32), jnp.zeros((tq, LANES), F32))
    carry = lax.fori_loop(0, i, step, (one, one))

    half = tq // 2
    start = pl.multiple_of(i * tk, tk)
    row = lax.broadcasted_iota(jnp.int32, (tq, half), 0)
    col = lax.broadcasted_iota(jnp.int32, (tq, half), 1)
    accs = []
    for hh in range(2):
        m, acc = update(hh, *carry[hh], qs[hh], start, half, col <= row)
        _, low = update(hh, m[half:], acc[half:], qs[hh][half:], start + half, half, (col <= row)[:half])
        accs.append(jnp.concatenate([acc[:half], low], axis=0))
    acc0, acc1 = accs
    o0 = acc0 / acc0[:, HEAD_DIM:HEAD_DIM + 1]
    o1 = acc1 / acc1[:, HEAD_DIM:HEAD_DIM + 1]
    o_ref[...] = jnp.where(lane < HEAD_DIM, o0, pltpu.roll(o1, HEAD_DIM, 1)).astype(o_ref.dtype)


def fox_attention(p2, cum, T, *, tq=FOX_TQ):
    B = p2.shape[0]
    HP = FOX_HEADS // 2
    return pl.pallas_call(
        functools.partial(_fox_kernel, tq=tq),
        grid=(B, HP, T // tq),
        in_specs=[pl.BlockSpec((None, tq, LANES), lambda b, h, i: (b, i, P2_FQ // LANES + h)),
                  pl.BlockSpec((None, T, LANES), lambda b, h, i: (b, 0, P2_FK // LANES + h)),
                  pl.BlockSpec((None, T, LANES), lambda b, h, i: (b, 0, P2_FV // LANES + h)),
                  pl.BlockSpec((None, None, 2, 1, T), lambda b, h, i: (b, h, 0, 0, 0))],
        out_specs=pl.BlockSpec((None, tq, LANES), lambda b, h, i: (b, i, h)),
        out_shape=jax.ShapeDtypeStruct((B, T, FOX_HEADS * HEAD_DIM), BF16),
        scratch_shapes=[pltpu.VMEM((2, T, LANES), BF16), pltpu.VMEM((2, T, LANES), BF16)],
        compiler_params=_cparams("parallel", "parallel", "arbitrary"),
        name="fox_attention",
    )(p2, p2, p2, cum)


def _readout_kernel(ocmp_ref, osel_ref, owin_ref, small_ref, oret_ref, ofox_ref, mg_ref, x_ref, g1_ref,
                    ex_ref, wn_ref, wr_ref, wf_ref, wo_ref, o_ref):
    W = NSA_OUT
    gs = _sigmoid(small_ref[...].astype(F32)).astype(BF16)
    ge = _dot(gs, ex_ref[...])
    onsa = (ge[:, :W] * ocmp_ref[...].astype(F32) + ge[:, W:2 * W] * osel_ref[...].astype(F32)
            + ge[:, 2 * W:] * owin_ref[...].astype(F32))
    D = D_MODEL
    merged = (_sigmoid(mg_ref[:, :D].astype(F32)) * _dot(onsa.astype(BF16), wn_ref[...])
              + _sigmoid(mg_ref[:, D:2 * D].astype(F32)) * _dot(oret_ref[...], wr_ref[...])
              + _sigmoid(mg_ref[:, 2 * D:].astype(F32)) * _dot(ofox_ref[...], wf_ref[...]))
    y = _dot(merged.astype(BF16), wo_ref[...])
    o_ref[...] = x_ref[...] + g1_ref[...] * y


def readout(o_cmp, o_sel, o_win, p2, o_ret, o_fox, x, mod_l, ex, wn, wr, wf, wo, l, T, *, tm=512):
    M, D = x.shape
    per_b = T // tm
    W = NSA_OUT
    row = lambda width, col=0: pl.BlockSpec((tm, width), lambda i: (i, col))
    full = lambda a: pl.BlockSpec(a.shape, lambda i: (0,) * a.ndim)
    return pl.pallas_call(
        _readout_kernel,
        grid=(M // tm,),
        in_specs=[row(W), row(W), row(W), row(LANES, P2_SMALL // LANES), row(512), row(512),
                  row(3 * D, 0), row(D),
                  pl.BlockSpec((None, None, 1, D), lambda i: (i // per_b, 2, 0, 0)),
                  full(ex), _layer_spec(wn, l, 1), _layer_spec(wr, l, 1), _layer_spec(wf, l, 1), _layer_spec(wo, l, 1)],
        out_specs=row(D),
        out_shape=jax.ShapeDtypeStruct((M, D), F32),
        compiler_params=_cparams("parallel"),
        name="mixer_readout",
    )(o_cmp, o_sel, o_win, p2, o_ret, o_fox, p2, x, mod_l, ex, wn, wr, wf, wo)


def nsa_gate_expand():
    ex = np.zeros((LANES, 3 * NSA_OUT), np.float32)
    for br in range(3):
        for h in range(NSA_HEADS):
            c0 = br * NSA_OUT + h * LANES
            ex[br * NSA_HEADS + h, c0:c0 + LANES] = 1.0
    return jnp.asarray(ex, BF16)


def pad_read_nsa(w):
    depth, _, D = w.shape
    w = w.reshape(depth, NSA_HEADS, HEAD_DIM, D)
    z = jnp.zeros_like(w)
    g = (np.arange(NSA_HEADS) // NSA_HPG)[None, :, None, None]
    lo = jnp.where(g == 0, w, z)
    hi = jnp.where(g == 1, w, z)
    return jnp.concatenate([lo, hi], axis=2).reshape(depth, NSA_OUT, D).astype(BF16)


FFN_CHUNK = 512


def _ffn_kernel(x_ref, nw_ref, sc_ref, sh_ref, g2_ref, w1_ref, w3_ref, w2_ref, o_ref):
    x = x_ref[...]
    h = _norm_mod(x, nw_ref[...], sc_ref[...], sh_ref[...]).astype(BF16)
    F = w1_ref.shape[1]
    y = None
    for c0 in range(0, F, FFN_CHUNK):
        cols = slice(c0, min(c0 + FFN_CHUNK, F))
        u = _dot(h, w1_ref[:, cols])
        v = _dot(h, w3_ref[:, cols])
        part = _dot((u * _sigmoid(u) * v).astype(BF16), w2_ref[cols, :])
        y = part if y is None else y + part
    o_ref[...] = x + g2_ref[...] * y


def ffn(x, mod_l, nw, w1, w3, w2, T, *, tm=512):
    M, D = x.shape
    F = w1.shape[1]
    per_b = T // tm
    modspec = lambda k: pl.BlockSpec((None, None, 1, D), lambda i: (i // per_b, k, 0, 0))
    full = lambda a: pl.BlockSpec(a.shape, lambda i: (0,) * a.ndim)
    return pl.pallas_call(
        _ffn_kernel,
        grid=(M // tm,),
        in_specs=[pl.BlockSpec((tm, D), lambda i: (i, 0)),
                  pl.BlockSpec((1, D), lambda i: (0, 0)),
                  modspec(4), modspec(3), modspec(5), full(w1), full(w3), full(w2)],
        out_specs=pl.BlockSpec((tm, D), lambda i: (i, 0)),
        out_shape=jax.ShapeDtypeStruct((M, D), F32),
        compiler_params=_cparams("parallel"),
        name="ffn_dense",
    )(x, nw, mod_l, mod_l, mod_l, w1, w3, w2)


MOE_TC = 512
MOE_TS = 512


def _router_kernel(x_ref, nw_ref, sc_ref, sh_ref, wh_ref, wl_ref, h_ref, gate_ref, rank_ref, cnt_ref, carry_ref):
    @pl.when(pl.program_id(0) == 0)
    def _():
        carry_ref[...] = jnp.zeros_like(carry_ref)

    h = _norm_mod(x_ref[...], nw_ref[...], sc_ref[...], sh_ref[...])
    hh = h.astype(BF16)
    h_ref[...] = _pack_bf16_pairs(hh.astype(F32))
    hl = (h - hh.astype(F32)).astype(BF16)
    logits = _dot(hh, wh_ref[...]) + (_dot(hl, wh_ref[...]) + _dot(hh, wl_ref[...]))
    tm = logits.shape[0]
    lane = lax.broadcasted_iota(jnp.int32, logits.shape, 1)
    logits = jnp.where(lane < N_EXPERTS, logits, REMOVED)
    lane_f = lane.astype(F32)
    v1 = jnp.max(logits, axis=-1, keepdims=True)
    i1 = jnp.min(jnp.where(logits == v1, lane_f, float(LANES)), axis=-1, keepdims=True)
    rest = jnp.where(lane_f == i1, REMOVED, logits)
    v2 = jnp.max(rest, axis=-1, keepdims=True)
    i2 = jnp.min(jnp.where(rest == v2, lane_f, float(LANES)), axis=-1, keepdims=True)
    e2 = jnp.exp(v2 - v1)
    w1 = 1.0 / (1.0 + e2)
    w2 = e2 / (1.0 + e2)
    gate_ref[...] = jnp.where(lane_f == i1, w1, jnp.where(lane_f == i2, w2, 0.0))

    sel = jnp.where((lane_f == i1) | (lane_f == i2), 1.0, 0.0)
    ri = lax.broadcasted_iota(jnp.int32, (tm, tm), 0)
    ci = lax.broadcasted_iota(jnp.int32, (tm, tm), 1)
    before = jnp.where(ci < ri, 1.0, 0.0).astype(BF16)
    rank = _dot(before, sel.astype(BF16)) + carry_ref[0:1, :]
    rank_ref[...] = jnp.where(sel > 0.0, rank, -1.0)
    carry_ref[...] = carry_ref[...] + jnp.sum(sel, axis=0, keepdims=True)
    cnt_ref[...] = carry_ref[...]


def router(x, mod_l, nw, w_router, T):
    M, D = x.shape
    tm = MOE_TC
    per_b = T // tm
    wp = jnp.zeros((D, LANES), F32).at[:, :N_EXPERTS].set(w_router)
    wh = wp.astype(BF16)
    wl = (wp - wh.astype(F32)).astype(BF16)
    return pl.pallas_call(
        _router_kernel,
        grid=(M // tm,),
        in_specs=[pl.BlockSpec((tm, D), lambda i: (i, 0)),
                  pl.BlockSpec((1, D), lambda i: (0, 0))]
        + _mod_specs(T, tm, 4, 3, 1)
        + [pl.BlockSpec((D, LANES), lambda i: (0, 0)),
           pl.BlockSpec((D, LANES), lambda i: (0, 0))],
        out_specs=[pl.BlockSpec((tm, D // 2), lambda i: (i, 0)),
                   pl.BlockSpec((tm, LANES), lambda i: (i, 0)),
                   pl.BlockSpec((tm, LANES), lambda i: (i, 0)),
                   pl.BlockSpec((8, LANES), lambda i: (0, 0))],
        out_shape=[jax.ShapeDtypeStruct((M, D // 2), jnp.uint32),
                   jax.ShapeDtypeStruct((M, LANES), F32),
                   jax.ShapeDtypeStruct((M, LANES), F32),
                   jax.ShapeDtypeStruct((8, LANES), F32)],
        scratch_shapes=[pltpu.VMEM((8, LANES), F32)],
        compiler_params=_cparams("arbitrary"),
        name="moe_router",
    )(x, nw, mod_l, mod_l, wh, wl)


def _count_le(sorted_vals, x):
    return jnp.sum(sorted_vals[None, :] <= x[:, None], axis=1, dtype=jnp.int32)


def _moe_up_kernel(e_r, total, x_ref, w1_ref, w3_ref, o_ref, w1b_ref, w3b_ref):
    r = pl.program_id(1)
    live = r < total[0]

    @pl.when(live & ((r == 0) | (e_r[r] != e_r[jnp.maximum(r - 1, 0)])))
    def _():
        w1b_ref[...] = w1_ref[...].astype(BF16)
        w3b_ref[...] = w3_ref[...].astype(BF16)

    @pl.when(live)
    def _():
        x = _unpack_bf16_pairs(x_ref[...]).astype(BF16)
        u = _dot(x, w1b_ref[...])
        v = _dot(x, w3b_ref[...])
        o_ref[...] = (u * _sigmoid(u) * v).astype(o_ref.dtype)


def moe_up(xs, w1, w3, tiles, rt, *, tf=1792):
    R = xs.shape[0]
    D = w1.shape[1]
    ts = MOE_TS
    F = w1.shape[-1]
    live = lambda r, total: jnp.minimum(r, total[0] - 1)
    return pl.pallas_call(
        _moe_up_kernel,
        grid_spec=pltpu.PrefetchScalarGridSpec(
            num_scalar_prefetch=2,
            grid=(F // tf, rt),
            in_specs=[pl.BlockSpec((ts, D // 2), lambda n, r, e, total: (live(r, total), 0)),
                      pl.BlockSpec((None, D, tf), lambda n, r, e, total: (e[live(r, total)], 0, n)),
                      pl.BlockSpec((None, D, tf), lambda n, r, e, total: (e[live(r, total)], 0, n))],
            out_specs=pl.BlockSpec((ts, tf), lambda n, r, e, total: (r, n)),
            scratch_shapes=[pltpu.VMEM((D, tf), BF16), pltpu.VMEM((D, tf), BF16)],
        ),
        out_shape=jax.ShapeDtypeStruct((R, F), BF16),
        compiler_params=_cparams("arbitrary", "arbitrary"),
        name="moe_up",
    )(tiles["e"], tiles["total"], xs, w1, w3)


def _moe_down_kernel(e_r, total, a_ref, w2_ref, o_ref, w2b_ref):
    r = pl.program_id(0)
    live = r < total[0]

    @pl.when(live & ((r == 0) | (e_r[r] != e_r[jnp.maximum(r - 1, 0)])))
    def _():
        w2b_ref[...] = w2_ref[...].astype(BF16)

    @pl.when(live)
    def _():
        o_ref[...] = _pack_bf16_pairs(_dot(a_ref[...], w2b_ref[...]))


def moe_down(a, w2, tiles, rt):
    R, F = a.shape
    ts = MOE_TS
    D = w2.shape[-1]
    live = lambda r, total: jnp.minimum(r, total[0] - 1)
    return pl.pallas_call(
        _moe_down_kernel,
        grid_spec=pltpu.PrefetchScalarGridSpec(
            num_scalar_prefetch=2,
            grid=(rt,),
            in_specs=[pl.BlockSpec((ts, F), lambda r, e, total: (live(r, total), 0)),
                      pl.BlockSpec((None, F, D), lambda r, e, total: (e[live(r, total)], 0, 0))],
            out_specs=pl.BlockSpec((ts, D // 2), lambda r, e, total: (r, 0)),
            scratch_shapes=[pltpu.VMEM((F, D), BF16)],
        ),
        out_shape=jax.ShapeDtypeStruct((R, D // 2), jnp.uint32),
        compiler_params=_cparams("arbitrary"),
        name="moe_down",
    )(tiles["e"], tiles["total"], a, w2)


SC_WINDOW = 64


def _sc_mesh():
    return plsc.VectorSubcoreMesh(core_axis_name="core", subcore_axis_name="subcore")


def sc_scatter_rows2(x, idx_a, idx_b, n_out):
    n, d = x.shape
    steps = n // SC_WINDOW

    @pl.kernel(out_type=jax.ShapeDtypeStruct((n_out, d), x.dtype), mesh=_sc_mesh(), scratch_types=[])
    def kern(x_hbm, ia_hbm, ib_hbm, o_hbm):
        def body(x_vmem, ia_vmem, ib_vmem):
            pltpu.sync_copy(x_vmem, o_hbm.at[ia_vmem.at[0]])
            pltpu.sync_copy(x_vmem, o_hbm.at[ib_vmem.at[0]])

        pltpu.emit_pipeline(
            body,
            grid=(steps,),
            in_specs=[pl.BlockSpec((SC_WINDOW, d), index_map=lambda i: (i, 0)),
                      pl.BlockSpec((1, SC_WINDOW), index_map=lambda i: (i, 0)),
                      pl.BlockSpec((1, SC_WINDOW), index_map=lambda i: (i, 0))],
            out_specs=[],
            core_axis_name=("core", "subcore"),
            dimension_semantics=(pltpu.PARALLEL,),
        )(x_hbm, ia_hbm, ib_hbm)

    return kern(x, idx_a.reshape(steps, SC_WINDOW), idx_b.reshape(steps, SC_WINDOW))


def sc_gather_rows(x, idx):
    n = idx.shape[0]
    d = x.shape[1]
    steps = n // SC_WINDOW

    @pl.kernel(out_type=jax.ShapeDtypeStruct((n, d), x.dtype), mesh=_sc_mesh(), scratch_types=[])
    def kern(x_hbm, i_hbm, o_hbm):
        def body(i_vmem, o_vmem):
            pltpu.sync_copy(x_hbm.at[i_vmem.at[0]], o_vmem)

        pltpu.emit_pipeline(
            body,
            grid=(steps,),
            in_specs=[pl.BlockSpec((1, SC_WINDOW), index_map=lambda i: (i, 0))],
            out_specs=[pl.BlockSpec((SC_WINDOW, d), index_map=lambda i: (i, 0))],
            core_axis_name=("core", "subcore"),
            dimension_semantics=(pltpu.PARALLEL,),
        )(i_hbm, o_hbm)

    return kern(x, idx.reshape(steps, SC_WINDOW))


def _moe_finish_kernel(x_ref, g2_ref, ya_ref, yb_ref, gate_ref, rank_ref, nw_ref, o_ref, *, normalize):
    gate = gate_ref[...]
    chosen = rank_ref[...] >= 0.0
    lane = lax.broadcasted_iota(jnp.int32, gate.shape, 1).astype(F32)
    first = jnp.min(jnp.where(chosen, lane, float(LANES)), axis=-1, keepdims=True)
    last = jnp.max(jnp.where(chosen, lane, -1.0), axis=-1, keepdims=True)
    wa = jnp.sum(jnp.where(lane == first, gate, 0.0), axis=-1, keepdims=True)
    wb = jnp.sum(jnp.where(lane == last, gate, 0.0), axis=-1, keepdims=True)
    x = x_ref[...] + g2_ref[...] * (wa * _unpack_bf16_pairs(ya_ref[...]) + wb * _unpack_bf16_pairs(yb_ref[...]))
    if normalize:
        ms = jnp.mean(x * x, axis=-1, keepdims=True)
        x = x * lax.rsqrt(ms + NORM_EPS) * nw_ref[...]
    o_ref[...] = x


def moe_finish(x, mod_l, y2, gate, rank, norm_w, T, *, tm=512):
    M, D = x.shape
    per_b = T // tm
    normalize = norm_w is not None
    if norm_w is None:
        norm_w = jnp.ones((1, D), F32)
    return pl.pallas_call(
        functools.partial(_moe_finish_kernel, normalize=normalize),
        grid=(M // tm,),
        in_specs=[pl.BlockSpec((tm, D), lambda i: (i, 0)),
                  pl.BlockSpec((None, None, 1, D), lambda i: (i // per_b, 5, 0, 0)),
                  pl.BlockSpec((None, tm, D // 2), lambda i: (0, i, 0)),
                  pl.BlockSpec((None, tm, D // 2), lambda i: (1, i, 0)),
                  pl.BlockSpec((tm, LANES), lambda i: (i, 0)),
                  pl.BlockSpec((tm, LANES), lambda i: (i, 0)),
                  pl.BlockSpec((1, D), lambda i: (0, 0))],
        out_specs=pl.BlockSpec((tm, D), lambda i: (i, 0)),
        out_shape=jax.ShapeDtypeStruct((M, D), F32),
        compiler_params=_cparams("parallel"),
        name="moe_finish",
    )(x, mod_l, y2, y2, gate, rank, norm_w)


def moe_ffn(x, mod_l, nw, w_router, w1, w3, w2, T, norm_w=None):
    M = x.shape[0]
    ts = MOE_TS
    rt = (2 * M) // ts + N_EXPERTS
    h, gate, rank, cnt = router(x, mod_l, nw, w_router, T)
    i32 = jnp.int32
    counts = cnt[0, :N_EXPERTS].astype(i32)
    ntile = (counts + ts - 1) // ts
    tile_end = jnp.cumsum(ntile)
    row_off = (tile_end - ntile) * ts
    e_r = jnp.minimum(_count_le(tile_end, jnp.arange(rt, dtype=i32)), N_EXPERTS - 1)
    tiles = dict(e=e_r, total=tile_end[-1].reshape(1).astype(i32))
    rk = rank[:, :N_EXPERTS].astype(i32)
    pos = row_off[None, :] + rk
    pos_a = jnp.min(jnp.where(rk >= 0, pos, rt * ts), axis=1)
    pos_b = jnp.max(jnp.where(rk >= 0, pos, -1), axis=1)

    xs = sc_scatter_rows2(h, pos_a, pos_b, rt * ts)
    a = moe_up(xs, w1, w3, tiles, rt)
    y = moe_down(a, w2, tiles, rt)
    y2 = sc_gather_rows(y, jnp.concatenate([pos_a, pos_b])).reshape(2, M, -1)
    return moe_finish(x, mod_l, y2, gate, rank, norm_w, T)


def _final_norm_kernel(x_ref, w_ref, o_ref):
    x = x_ref[...]
    ms = jnp.mean(x * x, axis=-1, keepdims=True)
    o_ref[...] = x * lax.rsqrt(ms + NORM_EPS) * w_ref[...]


def final_norm(x, w, *, tm=1024):
    M, D = x.shape
    return pl.pallas_call(
        _final_norm_kernel,
        grid=(M // tm,),
        in_specs=[pl.BlockSpec((tm, D), lambda i: (i, 0)), pl.BlockSpec((1, D), lambda i: (0, 0))],
        out_specs=pl.BlockSpec((tm, D), lambda i: (i, 0)),
        out_shape=jax.ShapeDtypeStruct((M, D), F32),
        compiler_params=_cparams("parallel"),
        name="final_norm",
    )(x, w)


def nsa_constants(T):
    n_sel = T // SEL_LEN
    nsp = max(LANES, n_sel)
    ncp = T // CMP_STRIDE
    cmp_start = np.arange(ncp) * CMP_STRIDE
    sel_start = np.arange(nsp) * SEL_LEN
    ov = ((cmp_start[:, None] < sel_start[None, :] + SEL_LEN)
          & (cmp_start[:, None] + CMP_LEN > sel_start[None, :]))
    ov[(T - CMP_LEN) // CMP_STRIDE + 1:] = False
    ov[:, n_sel:] = False
    et_mat = ((np.arange(T)[:, None] // SEL_LEN) == np.arange(nsp)[None, :]) * SEL_BONUS
    return jnp.asarray(ov.T, BF16), jnp.asarray(et_mat, BF16)


def token_mixing(x, mod_l, lw, consts, B, T):
    M = B * T
    cos_t, sin_t, ov_t, e_mat, ret_consts, ex = consts
    l = lw["layer"]
    p1 = proj_rope(x, mod_l, lw["norm_mix"], lw["w1"], l, cos_t, sin_t, p1_scales(), T).reshape(B, T, P1_COLS)
    p2 = proj_plain(x, mod_l, lw["norm_mix"], lw["w2"], l, T).reshape(B, T, P2_COLS)

    def group_rows(a):
        return a.reshape(B, T, NSA_GROUPS, HEAD_DIM).transpose(0, 2, 1, 3).reshape(
            B, NSA_GROUPS, T // CMP_STRIDE, CMP_STRIDE * HEAD_DIM)

    xr = jnp.stack([group_rows(p1[:, :, P1_NKC:P1_NKC + LANES]), group_rows(p2[:, :, P2_NVC:P2_NVC + LANES])])
    cmp_out = compress(xr, lw["cmp_pe"], lw["cmp_w1"], lw["cmp_w2"])
    cmp_out = cmp_out.transpose(0, 1, 3, 2, 4).reshape(2, B, T // CMP_STRIDE, LANES)
    o_cmp, sel, o_win = nsa_cmp_select_window(p1, p2, cmp_out[0], cmp_out[1], ov_t, T)
    o_sel = nsa_selected(p1, nsa_value_augment(p2[:, :, P2_NVS:P2_NVS + LANES]), sel, e_mat, T)

    o_ret = retention(p1, p2, ret_consts, T)

    ff = p2[:, :, P2_SMALL + 3 * NSA_HEADS:P2_SMALL + 3 * NSA_HEADS + FOX_HEADS].astype(F32)
    ff = ff.transpose(0, 2, 1).reshape(B, FOX_HEADS, T // LANES, LANES)
    cum = fox_cum(ff, lw["fox_bias"]).reshape(B, FOX_HEADS // 2, 2, 1, T)
    o_fox = fox_attention(p2, cum, T)

    return readout(o_cmp.reshape(M, -1), o_sel.reshape(M, -1), o_win.reshape(M, -1), p2.reshape(M, P2_COLS),
                   o_ret.reshape(M, -1), o_fox.reshape(M, -1), x, mod_l, ex,
                   lw["wn"], lw["wr"], lw["wf"], lw["wo"], l, T)


def mixer_weights(norm_mix, w_in, cmp_k_pe, cmp_k_w1, cmp_k_w2, cmp_v_pe, cmp_v_w1, cmp_v_w2, fox_f_bias,
                  w_read_nsa, w_read_ret, w_read_fox, w_out):
    depth = w_in.shape[0]
    w1, w2 = split_w_in(w_in)
    pe = jnp.stack([cmp_k_pe.reshape(depth, 1, -1), cmp_v_pe.reshape(depth, 1, -1)], axis=1)
    pe = jnp.broadcast_to(pe, (depth, 2, 8, pe.shape[-1])).astype(BF16)
    shared = {
        "w1": w1, "w2": w2,
        "wn": pad_read_nsa(w_read_nsa),
        "wr": w_read_ret.astype(BF16),
        "wf": w_read_fox.astype(BF16),
        "wo": w_out.astype(BF16),
    }
    cmp_w1 = jnp.stack([cmp_k_w1, cmp_v_w1], axis=1).astype(BF16)
    cmp_w2 = jnp.stack([cmp_k_w2, cmp_v_w2], axis=1).astype(BF16)
    return [dict(shared, layer=l, norm_mix=norm_mix[l].reshape(1, -1), cmp_pe=pe[l], cmp_w1=cmp_w1[l], cmp_w2=cmp_w2[l],
                 fox_bias=jnp.broadcast_to(fox_f_bias[l][:, None, None], (FOX_HEADS, 1, LANES)))
            for l in range(depth)]


def kernel(x, c, ada_w, ada_b, norm_mix, norm_ffn, w_in, cmp_k_pe, cmp_k_w1, cmp_k_w2, cmp_v_pe, cmp_v_w1,
           cmp_v_w2, fox_f_bias, w_read_nsa, w_read_ret, w_read_fox, w_out, ffn_w1, ffn_w3, ffn_w2, router_w,
           moe_w1, moe_w3, moe_w2, final_norm_w):
    B, T, D = x.shape
    M = B * T
    depth = ada_w.shape[0]
    mod = modulation(c, ada_w, ada_b)
    cos_t, sin_t = rope_tables(T)
    ov_t, e_mat = nsa_constants(T)
    consts = (cos_t, sin_t, ov_t, e_mat, retention_consts(), nsa_gate_expand())
    xs = x.reshape(M, D)
    lws = mixer_weights(norm_mix, w_in, cmp_k_pe, cmp_k_w1, cmp_k_w2, cmp_v_pe, cmp_v_w1, cmp_v_w2,
                        fox_f_bias, w_read_nsa, w_read_ret, w_read_fox, w_out)
    for l in range(depth):
        xs = token_mixing(xs, mod[l], lws[l], consts, B, T)
        nf = norm_ffn[l].reshape(1, D)
        if l % 2 == 0:
            k = l // 2
            xs = ffn(xs, mod[l], nf, ffn_w1[k].astype(BF16), ffn_w3[k].astype(BF16), ffn_w2[k].astype(BF16), T)
        else:
            k = l // 2
            fuse = final_norm_w.reshape(1, D) if l == depth - 1 else None
            xs = moe_ffn(xs, mod[l], nf, router_w[k], moe_w1[k], moe_w3[k], moe_w2[k], T, fuse)
    if depth % 2 == 1:
        xs = final_norm(xs, final_norm_w.reshape(1, D))
    return xs.reshape(B, T, D)
```

```python
import functools

import jax
import jax.numpy as jnp
import numpy as np
from jax import lax
from jax.experimental import pallas as pl
from jax.experimental.pallas import tpu as pltpu
from jax.experimental.pallas import tpu_sc as plsc

F32 = jnp.float32
BF16 = jnp.bfloat16

D_MODEL = 1024
HEAD_DIM = 64
ROPE_THETA = 10000.0
NORM_EPS = 1e-6
NEG_INF = -1e30
REMOVED = -3e38

NSA_HEADS = 8
NSA_GROUPS = 2
NSA_HPG = NSA_HEADS // NSA_GROUPS
CMP_LEN = 32
CMP_STRIDE = 16
SEL_LEN = 64
SEL_TOPN = 16
WINDOW = 512
NSA_QBLOCK = 256

RET_HEADS = 4
RET_QK_DIM = 64
RET_CHUNK = 128

FOX_HEADS = 8
FOX_TQ = 1024
LOG2E = 1.4426950408889634

N_EXPERTS = 8

LANES = 128
VMEM_LIMIT = 56 * 1024 * 1024

P1_NQ = 0
P1_RQ = 512
P1_RK = 768
P1_NKC = 1024
P1_NKS = 1152
P1_NKW = 1280
P1_COLS = 1408
P2_MG = 0
P2_RV = 3072
P2_RG = 3584
P2_FQ = 4096
P2_FK = 4608
P2_FV = 5120
P2_NVC = 5632
P2_NVS = 5760
P2_NVW = 5888
P2_SMALL = 6016
P2_COLS = 6144
NSA_OUT = NSA_HEADS * HEAD_DIM


def _layer_spec(w, l):
    zeros = (0,) * (w.ndim - 1)
    return pl.BlockSpec((None,) + w.shape[1:], lambda *_: (l,) + zeros)


def _cparams(*sem):
    return pltpu.CompilerParams(dimension_semantics=tuple(sem), vmem_limit_bytes=VMEM_LIMIT)


def _sigmoid(x):
    return 1.0 / (1.0 + jnp.exp(-x))


def _dot(a, b):
    return jnp.dot(a, b, preferred_element_type=F32)


def _dot_nt(a, b):
    return lax.dot_general(a, b, (((1,), (1,)), ((), ())), preferred_element_type=F32)


def _dot_tn(a, b):
    return lax.dot_general(a, b, (((0,), (0,)), ((), ())), preferred_element_type=F32)


def _split3(x):
    hi = x.astype(BF16)
    r1 = x - hi.astype(F32)
    mid = r1.astype(BF16)
    lo = (r1 - mid.astype(F32)).astype(BF16)
    return hi, mid, lo


def _pack_bf16_pairs(x):
    c = x.shape[1] // 2
    lo = pltpu.bitcast(x[:, :c].astype(BF16).astype(F32), jnp.uint32) >> 16
    hi = pltpu.bitcast(x[:, c:].astype(BF16).astype(F32), jnp.uint32) & jnp.uint32(0xFFFF0000)
    return hi | lo


def _unpack_bf16_pairs(u):
    lo = pltpu.bitcast(u << 16, F32)
    hi = pltpu.bitcast(u & jnp.uint32(0xFFFF0000), F32)
    return jnp.concatenate([lo, hi], axis=1)


def _norm_mod(x, nw, sc, sh):
    ms = jnp.mean(x * x, axis=-1, keepdims=True)
    y = x * lax.rsqrt(ms + NORM_EPS) * nw
    return y * (1.0 + sc) + sh


def _mod_kernel(c_ref, w_ref, b_ref, o_ref):
    c = c_ref[...]
    s = c * _sigmoid(c)
    o_ref[0] = _dot(s.astype(BF16), w_ref[0].astype(BF16)) + b_ref[0]


def modulation(c, ada_w, ada_b):
    B, D = c.shape
    depth = ada_w.shape[0]
    rows = 8
    c_pad = jnp.zeros((rows, D), F32).at[:B].set(c)
    out = pl.pallas_call(
        _mod_kernel,
        grid=(depth, 6),
        in_specs=[pl.BlockSpec((rows, D), lambda l, j: (0, 0)),
                  pl.BlockSpec((1, D, D), lambda l, j: (l, 0, j)),
                  pl.BlockSpec((1, 1, D), lambda l, j: (l, 0, j))],
        out_specs=pl.BlockSpec((1, rows, D), lambda l, j: (l, 0, j)),
        out_shape=jax.ShapeDtypeStruct((depth, rows, 6 * D), F32),
        compiler_params=_cparams("parallel", "parallel"),
        name="modulation",
    )(c_pad, ada_w, ada_b.reshape(depth, 1, 6 * D))
    return out[:, :B].reshape(depth, B, 6, 1, D)


def _proj_plain_kernel(x_ref, nw_ref, sc_ref, sh_ref, w_ref, o_ref, *, tn):
    h = _norm_mod(x_ref[...], nw_ref[...], sc_ref[...], sh_ref[...]).astype(BF16)
    for n in range(w_ref.shape[1] // tn):
        cols = slice(n * tn, (n + 1) * tn)
        o_ref[:, cols] = _dot(h, w_ref[:, cols]).astype(o_ref.dtype)


def _proj_rope_kernel(x_ref, nw_ref, sc_ref, sh_ref, w_ref, cos_ref, sin_ref, o_ref, *, scales):
    h = _norm_mod(x_ref[...], nw_ref[...], sc_ref[...], sh_ref[...]).astype(BF16)
    y = _dot(h, w_ref[...])
    cos = cos_ref[...]
    sin = sin_ref[...]
    lane = lax.broadcasted_iota(jnp.int32, cos.shape, 1)
    first_half = (lane % HEAD_DIM) < (HEAD_DIM // 2)
    for g, scale in enumerate(scales):
        yg = y[:, g * LANES:(g + 1) * LANES]
        rot = jnp.where(first_half, pltpu.roll(yg, LANES - HEAD_DIM // 2, 1),
                        pltpu.roll(yg, HEAD_DIM // 2, 1))
        r = yg * cos + rot * sin
        if scale != 1.0:
            r = r * scale
        o_ref[:, g * LANES:(g + 1) * LANES] = r.astype(o_ref.dtype)


def _mod_specs(T, tm, sc_idx, sh_idx, nargs):
    per_b = T // tm
    if nargs == 1:
        return [pl.BlockSpec((None, None, 1, D_MODEL), lambda i: (i // per_b, sc_idx, 0, 0)),
                pl.BlockSpec((None, None, 1, D_MODEL), lambda i: (i // per_b, sh_idx, 0, 0))]
    return [pl.BlockSpec((None, None, 1, D_MODEL), lambda i, j: (i // per_b, sc_idx, 0, 0)),
            pl.BlockSpec((None, None, 1, D_MODEL), lambda i, j: (i // per_b, sh_idx, 0, 0))]


def proj_plain(x, mod_l, nw, w, l, T, *, tm=512, tn=512):
    M, D = x.shape
    N = w.shape[-1]
    return pl.pallas_call(
        functools.partial(_proj_plain_kernel, tn=tn),
        grid=(M // tm,),
        in_specs=[pl.BlockSpec((tm, D), lambda i: (i, 0)),
                  pl.BlockSpec((1, D), lambda i: (0, 0))]
        + _mod_specs(T, tm, 1, 0, 1)
        + [_layer_spec(w, l)],
        out_specs=pl.BlockSpec((tm, N), lambda i: (i, 0)),
        out_shape=jax.ShapeDtypeStruct((M, N), BF16),
        compiler_params=_cparams("parallel"),
        name="proj_plain",
    )(x, nw, mod_l, mod_l, w)


def proj_rope(x, mod_l, nw, w, l, cos, sin, scales, T, *, tm=512):
    M, D = x.shape
    N = w.shape[-1]
    per_b = T // tm
    return pl.pallas_call(
        functools.partial(_proj_rope_kernel, scales=scales),
        grid=(M // tm,),
        in_specs=[pl.BlockSpec((tm, D), lambda i: (i, 0)),
                  pl.BlockSpec((1, D), lambda i: (0, 0))]
        + _mod_specs(T, tm, 1, 0, 1)
        + [_layer_spec(w, l),
           pl.BlockSpec((tm, LANES), lambda i: (i % per_b, 0)),
           pl.BlockSpec((tm, LANES), lambda i: (i % per_b, 0))],
        out_specs=pl.BlockSpec((tm, N), lambda i: (i, 0)),
        out_shape=jax.ShapeDtypeStruct((M, N), BF16),
        compiler_params=_cparams("parallel"),
        name="proj_rope",
    )(x, nw, mod_l, mod_l, w, cos, sin)


def rope_tables(T):
    d = HEAD_DIM
    pos = jnp.arange(T, dtype=F32)
    inv = ROPE_THETA ** (-jnp.arange(0, d, 2, dtype=F32) / d)
    ang = pos[:, None] * inv[None, :]
    cos = jnp.cos(ang)
    sin = jnp.sin(ang)
    cos_t = jnp.concatenate([cos, cos, cos, cos], axis=-1)
    sin_t = jnp.concatenate([-sin, sin, -sin, sin], axis=-1)
    return cos_t, sin_t


def split_w_in(w_in):
    sizes = [512, 128, 128, 128, 128, 128, 128, 24, 256, 256, 512, 512, 512, 512, 512, 8, 3072]
    offs = np.cumsum([0] + sizes)
    wb = w_in.astype(BF16)
    (nq, nkc, nvc, nks, nvs, nkw, nvw, ngate, rq, rk, rv, rg, fq, fk, fv, ff, mg) = [
        wb[..., offs[i]:offs[i + 1]] for i in range(len(sizes))]
    small = jnp.concatenate([ngate, ff, jnp.zeros(ngate.shape[:-1] + (LANES - 32,), BF16)], axis=-1)
    w1 = jnp.concatenate([nq, rq, rk, nkc, nks, nkw], axis=-1)
    w2 = jnp.concatenate([mg, rv, rg, fq, fk, fv, nvc, nvs, nvw, small], axis=-1)
    assert w1.shape[-1] == P1_COLS and w2.shape[-1] == P2_COLS
    return w1, w2


def p1_scales():
    s = [1.0] * (P1_COLS // LANES)
    for g in range(P1_NQ // LANES, P1_RQ // LANES):
        s[g] = HEAD_DIM ** -0.5 * LOG2E
    for g in range(P1_RK // LANES, P1_NKC // LANES):
        s[g] = RET_QK_DIM ** -0.5
    return tuple(s)


def _compress_kernel(x_ref, pe_ref, w1_ref, w2_ref, o_ref):
    r = x_ref[...]
    half = r.shape[1]
    w1 = w1_ref[...]
    a = _dot(r, w1[:half])
    b = _dot(r, w1[half:])
    pe = _dot(pe_ref[...], w1)[0:1]
    n = a.shape[0]
    hid = a + pltpu.roll(b, n - 1, 0) + pe
    hid = hid * _sigmoid(hid)
    o_ref[...] = _dot(hid.astype(BF16), w2_ref[...]).astype(o_ref.dtype)


def compress(xr, pe, w1, w2):
    _, B, G, R, W = xr.shape
    H = w1.shape[-1]
    return pl.pallas_call(
        _compress_kernel,
        grid=(2, B, G),
        in_specs=[pl.BlockSpec((None, None, None, R, W), lambda s, b, g: (s, b, g, 0, 0)),
                  pl.BlockSpec((None, 8, 2 * W), lambda s, b, g: (s, 0, 0)),
                  pl.BlockSpec((None, 2 * W, H), lambda s, b, g: (s, 0, 0)),
                  pl.BlockSpec((None, H, HEAD_DIM), lambda s, b, g: (s, 0, 0))],
        out_specs=pl.BlockSpec((None, None, None, R, HEAD_DIM), lambda s, b, g: (s, b, g, 0, 0)),
        out_shape=jax.ShapeDtypeStruct((2, B, G, R, HEAD_DIM), BF16),
        compiler_params=_cparams("parallel", "parallel", "parallel"),
        name="nsa_compress",
    )(xr, pe, w1, w2)


def _stack_heads(q_ref, g):
    tq = q_ref.shape[0]
    half = lax.broadcasted_iota(jnp.int32, (tq, LANES), 1) // HEAD_DIM
    rows = []
    for hh in range(NSA_HPG):
        h = NSA_HPG * g + hh
        x = q_ref[:, (h // 2) * LANES:(h // 2 + 1) * LANES].astype(F32)
        if h % 2 != g:
            x = pltpu.roll(x, HEAD_DIM, 1)
        rows.append(jnp.where(half == g, x, 0.0).astype(BF16))
    return jnp.concatenate(rows, axis=0)


def _store_heads(o_ref, g, o, tq):
    low = lax.broadcasted_iota(jnp.int32, (tq, LANES), 1) < HEAD_DIM
    for pair in range(NSA_HPG // 2):
        even = o[(2 * pair) * tq:(2 * pair + 1) * tq]
        odd = o[(2 * pair + 1) * tq:(2 * pair + 2) * tq]
        if g == 0:
            blk = jnp.where(low, even, pltpu.roll(odd, HEAD_DIM, 1))
        else:
            blk = jnp.where(low, pltpu.roll(even, HEAD_DIM, 1), odd)
        col = (NSA_HPG // 2 * g + pair) * LANES
        o_ref[:, col:col + LANES] = blk.astype(o_ref.dtype)


CMP_CHUNK = 128


def _nsa_cmp_kernel(q_ref, kc_ref, vc_ref, ov_ref, o_ref, m_ref, imp_ref, *, tq, n_sel, top_n):
    t0 = pl.program_id(1) * tq
    ncp = kc_ref.shape[0]
    nsp = ov_ref.shape[0]
    rows = NSA_HPG * tq

    def attend(ncols):
        kc = kc_ref[0:ncols, :]
        vc = vc_ref[0:ncols, :]
        n_idx = lax.broadcasted_iota(jnp.int32, (rows, ncols), 1)
        t_idx = t0 + lax.broadcasted_iota(jnp.int32, (rows, ncols), 0) % tq
        valid = (n_idx * CMP_STRIDE + (CMP_LEN - 1)) <= t_idx
        for g in range(NSA_GROUPS):
            q = _stack_heads(q_ref, g)
            s = jnp.where(valid, _dot_nt(q, kc), NEG_INF)
            m = jnp.max(s, axis=-1, keepdims=True)
            e = jnp.exp2(s - m)
            l = jnp.sum(e, axis=-1, keepdims=True)
            p = e * jnp.where(m > 0.5 * NEG_INF, 1.0 / l, 0.0)
            _store_heads(o_ref, g, _dot(p.astype(BF16), vc), tq)
            psum = p[0:tq]
            for hh in range(1, NSA_HPG):
                psum = psum + p[hh * tq:(hh + 1) * tq]
            imp_ref[g] = _dot_nt(ov_ref[:, 0:ncols], psum.astype(BF16))

    n_live = jnp.maximum((t0 + tq - CMP_LEN) // CMP_STRIDE + 1, 1)
    n_chunks = jnp.minimum((n_live + CMP_CHUNK - 1) // CMP_CHUNK, ncp // CMP_CHUNK)
    for nc in range(1, ncp // CMP_CHUNK + 1):
        pl.when(n_chunks == nc)(functools.partial(attend, nc * CMP_CHUNK))

    j_idx = lax.broadcasted_iota(jnp.int32, (nsp, tq), 0)
    cur = (t0 + lax.broadcasted_iota(jnp.int32, (nsp, tq), 1)) // SEL_LEN
    forced = (j_idx == 0) | (j_idx == cur) | (j_idx == cur - 1)
    j_f = j_idx.astype(F32)
    for g in range(NSA_GROUPS):
        score = jnp.where(j_idx <= cur, imp_ref[g], NEG_INF)
        score = jnp.where(forced | (j_idx >= n_sel), REMOVED, score)
        sel = jnp.where(forced, 1.0, 0.0)
        for _ in range(max(top_n - 3, 0)):
            mx = jnp.max(score, axis=0, keepdims=True)
            idx = jnp.min(jnp.where(score == mx, j_f, float(nsp)), axis=0, keepdims=True)
            hit = j_f == idx
            sel = jnp.where(hit, 1.0, sel)
            score = jnp.where(hit, REMOVED, score)
        sel = jnp.where(j_idx <= cur, sel, 0.0)
        m_ref[g] = sel.T.astype(m_ref.dtype)


SEL_BONUS = 8192.0
NSA_SEL_TQ = 256
NSA_SEL_TK = 1024


def _nsa_sel_kernel(q_ref, k_ref, v_ref, m_ref, et_ref, o_ref, *, tq, tk):
    t0 = pl.program_id(1) * tq
    n_full = t0 // tk
    rows = NSA_HPG * tq

    def update(carry, q, ks, vs, mask=None):
        m, acc = carry
        s = _dot_nt(q, ks)
        if mask is not None:
            s = jnp.where(mask, s, NEG_INF)
        m_new = jnp.maximum(m, jnp.max(s, axis=-1, keepdims=True))
        p = jnp.exp2(s - m_new)
        return m_new, jnp.exp2(m - m_new) * acc + _dot(p.astype(BF16), vs)

    qs, carries = [], []
    for g in range(NSA_GROUPS):
        q = jnp.concatenate([_stack_heads(q_ref, g), jnp.concatenate([m_ref[g]] * NSA_HPG, axis=0)], axis=1)

        def step(j, carry, q=q, g=g):
            start = pl.multiple_of(j * tk, tk)
            ks = jnp.concatenate([k_ref[pl.ds(start, tk), :], et_ref[pl.ds(start, tk), :]], axis=1)
            return update(carry, q, ks, v_ref[g, pl.ds(start, tk), :])

        init = (jnp.full((rows, 1), NEG_INF, F32), jnp.zeros((rows, LANES), F32))
        qs.append(q)
        carries.append(lax.fori_loop(0, n_full, step, init))

    start = pl.multiple_of(n_full * tk, tk)

    def tail(nk):
        trow = t0 + lax.broadcasted_iota(jnp.int32, (rows, nk), 0) % tq
        causal = start + lax.broadcasted_iota(jnp.int32, (rows, nk), 1) <= trow
        ks = jnp.concatenate([k_ref[pl.ds(start, nk), :], et_ref[pl.ds(start, nk), :]], axis=1)
        for g in range(NSA_GROUPS):
            _, acc = update(carries[g], qs[g], ks, v_ref[g, pl.ds(start, nk), :], causal)
            den = HEAD_DIM * (1 - g)
            _store_heads(o_ref, g, acc / acc[:, den:den + 1], tq)

    which = (t0 - start) // tq
    for v in range(tk // tq):
        pl.when(which == v)(functools.partial(tail, (v + 1) * tq))


def nsa_value_augment(v):
    ones = jnp.ones_like(v[..., :HEAD_DIM])
    return jnp.stack([jnp.concatenate([v[..., :HEAD_DIM], ones], axis=-1),
                      jnp.concatenate([ones, v[..., HEAD_DIM:]], axis=-1)], axis=1)


def nsa_selected(p1, v_aug, sel, et_mat, T, *, tq=NSA_SEL_TQ, tk=NSA_SEL_TK):
    B = p1.shape[0]
    nsp = sel.shape[-1]
    return pl.pallas_call(
        functools.partial(_nsa_sel_kernel, tq=tq, tk=tk),
        grid=(B, T // tq),
        in_specs=[pl.BlockSpec((None, tq, NSA_HEADS * HEAD_DIM), lambda b, i: (b, i, 0)),
                  pl.BlockSpec((None, T, LANES), lambda b, i: (b, 0, P1_NKS // LANES)),
                  pl.BlockSpec((None, NSA_GROUPS, T, LANES), lambda b, i: (b, 0, 0, 0)),
                  pl.BlockSpec((None, NSA_GROUPS, tq, nsp), lambda b, i: (b, 0, i, 0)),
                  pl.BlockSpec((T, nsp), lambda b, i: (0, 0))],
        out_specs=pl.BlockSpec((None, tq, NSA_OUT), lambda b, i: (b, i, 0)),
        out_shape=jax.ShapeDtypeStruct((B, T, NSA_OUT), BF16),
        compiler_params=_cparams("parallel", "parallel"),
        name="nsa_selected",
    )(p1, p1, v_aug, sel, et_mat)


def _nsa_win_kernel(q_ref, k_ref, v_ref, b_ref, o_ref, *, tq):
    t0 = pl.program_id(1) * tq
    span = WINDOW + tq
    start = pl.multiple_of(jnp.maximum(t0 - WINDOW, 0), tq)
    ks = k_ref[pl.ds(start, span), :]
    vs = v_ref[pl.ds(start, span), :]

    def run(bias):
        bias = jnp.concatenate([bias] * NSA_HPG, axis=0)
        for g in range(NSA_GROUPS):
            s = _dot_nt(_stack_heads(q_ref, g), ks) + bias
            m = jnp.max(s, axis=-1, keepdims=True)
            p = jnp.exp2(s - m)
            l = jnp.sum(p, axis=-1, keepdims=True)
            _store_heads(o_ref, g, _dot(p.astype(BF16), vs) / l, tq)

    @pl.when(t0 >= WINDOW)
    def _():
        run(b_ref[...])

    @pl.when(t0 < WINDOW)
    def _():
        row = lax.broadcasted_iota(jnp.int32, (tq, span), 0)
        col = lax.broadcasted_iota(jnp.int32, (tq, span), 1)
        run(jnp.where(col <= t0 + row, 0.0, NEG_INF))


def _nsa_cmp_win_kernel(q_ref, kc_ref, vc_ref, ov_ref, kw_ref, vw_ref, band_ref, ocmp_ref, m_ref, owin_ref, imp_ref,
                        *, tq, n_sel, top_n):
    _nsa_cmp_kernel(q_ref, kc_ref, vc_ref, ov_ref, ocmp_ref, m_ref, imp_ref, tq=tq, n_sel=n_sel, top_n=top_n)
    _nsa_win_kernel(q_ref, kw_ref, vw_ref, band_ref, owin_ref, tq=tq)


def nsa_cmp_select_window(p1, p2, kc, vc, ov_t, T):
    B = p1.shape[0]
    tq = NSA_QBLOCK
    ncp = kc.shape[1]
    nsp = ov_t.shape[0]
    n_sel = T // SEL_LEN
    span = WINDOW + tq
    r = np.arange(tq)[:, None]
    c = np.arange(span)[None, :]
    band = jnp.asarray(np.where((c > r) & (c <= r + WINDOW), 0.0, NEG_INF), F32)
    out_blk = pl.BlockSpec((None, tq, NSA_OUT), lambda b, i: (b, i, 0))
    return pl.pallas_call(
        functools.partial(_nsa_cmp_win_kernel, tq=tq, n_sel=n_sel, top_n=min(SEL_TOPN, n_sel)),
        grid=(B, T // tq),
        in_specs=[pl.BlockSpec((None, tq, NSA_HEADS * HEAD_DIM), lambda b, i: (b, i, 0)),
                  pl.BlockSpec((None, ncp, LANES), lambda b, i: (b, 0, 0)),
                  pl.BlockSpec((None, ncp, LANES), lambda b, i: (b, 0, 0)),
                  pl.BlockSpec((nsp, ncp), lambda b, i: (0, 0)),
                  pl.BlockSpec((None, T, LANES), lambda b, i: (b, 0, P1_NKW // LANES)),
                  pl.BlockSpec((None, T, LANES), lambda b, i: (b, 0, P2_NVW // LANES)),
                  pl.BlockSpec((tq, span), lambda b, i: (0, 0))],
        out_specs=[out_blk, pl.BlockSpec((None, NSA_GROUPS, tq, nsp), lambda b, i: (b, 0, i, 0)), out_blk],
        out_shape=[jax.ShapeDtypeStruct((B, T, NSA_OUT), BF16),
                   jax.ShapeDtypeStruct((B, NSA_GROUPS, T, nsp), BF16),
                   jax.ShapeDtypeStruct((B, T, NSA_OUT), BF16)],
        scratch_shapes=[pltpu.VMEM((NSA_GROUPS, nsp, tq), F32)],
        compiler_params=_cparams("parallel", "parallel"),
        name="nsa_cmp_select_window",
    )(p1, kc, vc, ov_t, p1, p2, band)


def _retention_kernel(q_ref, k_ref, v_ref, g_ref, din_ref, qd_ref, kd_ref, cd_ref, o_ref, st_ref):
    @pl.when(pl.program_id(0) == 0)
    def _():
        st_ref[...] = jnp.zeros_like(st_ref)

    B = q_ref.shape[0]
    C = RET_CHUNK
    half = lax.broadcasted_iota(jnp.int32, (C, LANES), 1) // HEAD_DIM
    for b in range(B):
        for h in range(RET_HEADS):
            lanes = slice(h * LANES, (h + 1) * LANES)
            pair = slice((h // 2) * LANES, (h // 2 + 1) * LANES)
            st = st_ref[b, h]
            for sub in range(q_ref.shape[1] // C):
                rows = slice(sub * C, (sub + 1) * C)
                qh = jnp.where(half == h % 2, q_ref[b, rows, pair], 0.0).astype(BF16)
                kp = k_ref[b, rows, pair]
                vh = v_ref[b, rows, lanes]
                inner = _dot_nt(qh, kp) * din_ref[h]
                o = _dot(inner.astype(BF16), vh) + _dot(qh, st.astype(BF16)) * qd_ref[h]
                kd = (kp.astype(F32) * kd_ref[h]).astype(BF16)
                st = st * cd_ref[h, 0:1, :] + _dot_tn(kd, vh)
                mu = jnp.mean(o, axis=-1, keepdims=True)
                d = o - mu
                var = jnp.mean(d * d, axis=-1, keepdims=True)
                on = d * lax.rsqrt(var + NORM_EPS)
                gh = g_ref[b, rows, lanes].astype(F32)
                o_ref[b, rows, lanes] = (gh * _sigmoid(gh) * on).astype(o_ref.dtype)
            st_ref[b, h] = st


def retention_consts():
    C = RET_CHUNK
    H = RET_HEADS
    log_g = jnp.log(1.0 - 2.0 ** (-5.0 - jnp.arange(H, dtype=F32)))
    n = jnp.arange(C, dtype=F32)
    diff = n[:, None] - n[None, :]
    causal = diff >= 0
    decay_in = jnp.where(causal[None], jnp.exp(jnp.where(causal, diff, 0.0)[None] * log_g[:, None, None]), 0.0)
    q_decay = jnp.exp((n[None, :] + 1.0) * log_g[:, None])
    k_decay = jnp.exp((C - 1.0 - n)[None, :] * log_g[:, None])
    chunk_decay = jnp.exp(C * log_g)
    qd = jnp.broadcast_to(q_decay[:, :, None], (H, C, LANES))
    kd = jnp.broadcast_to(k_decay[:, :, None], (H, C, LANES))
    cd = jnp.broadcast_to(chunk_decay[:, None, None], (H, 8, LANES))
    return decay_in, qd, kd, cd


RET_STEP = 4


def retention(p1, p2, consts, T):
    B = p1.shape[0]
    C = RET_CHUNK * RET_STEP
    din, qd, kd, cd = consts
    W = RET_HEADS * LANES
    full = lambda shape: pl.BlockSpec(shape, lambda c: (0,) * len(shape))
    return pl.pallas_call(
        _retention_kernel,
        grid=(T // C,),
        in_specs=[pl.BlockSpec((B, C, W // 2), lambda c: (0, c, P1_RQ // (W // 2))),
                  pl.BlockSpec((B, C, W // 2), lambda c: (0, c, P1_RK // (W // 2))),
                  pl.BlockSpec((B, C, W), lambda c: (0, c, P2_RV // W)),
                  pl.BlockSpec((B, C, W), lambda c: (0, c, P2_RG // W)),
                  full(din.shape), full(qd.shape), full(kd.shape), full(cd.shape)],
        out_specs=pl.BlockSpec((B, C, W), lambda c: (0, c, 0)),
        out_shape=jax.ShapeDtypeStruct((B, T, W), BF16),
        scratch_shapes=[pltpu.VMEM((B, RET_HEADS, LANES, LANES), F32)],
        compiler_params=_cparams("arbitrary"),
        name="retention",
    )(p1, p1, p2, p2, din, qd, kd, cd)


def _fox_cum_kernel(f_ref, b_ref, o_ref):
    x = f_ref[...] + b_ref[...]
    ls = jnp.minimum(x, 0.0) - jnp.log1p(jnp.exp(-jnp.abs(x)))
    R = x.shape[0]
    ki = lax.broadcasted_iota(jnp.int32, (LANES, LANES), 0)
    ji = lax.broadcasted_iota(jnp.int32, (LANES, LANES), 1)
    upper = jnp.where(ki <= ji, 1.0, 0.0).astype(BF16)
    hi, mid, lo = _split3(ls)
    rowcum = _dot(hi, upper) + _dot(mid, upper) + _dot(lo, upper)
    tot = jnp.broadcast_to(rowcum[:, LANES - 1:LANES], (R, LANES))
    ri = lax.broadcasted_iota(jnp.int32, (R, R), 0)
    ci = lax.broadcasted_iota(jnp.int32, (R, R), 1)
    lower = jnp.where(ci < ri, 1.0, 0.0).astype(BF16)
    hi, mid, lo = _split3(tot)
    offs = _dot(lower, hi) + _dot(lower, mid) + _dot(lower, lo)
    o_ref[...] = (rowcum + offs) * LOG2E


def fox_cum(f_logit, bias):
    B, H, R, _ = f_logit.shape
    return pl.pallas_call(
        _fox_cum_kernel,
        grid=(B, H),
        in_specs=[pl.BlockSpec((None, None, R, LANES), lambda b, h: (b, h, 0, 0)),
                  pl.BlockSpec((None, 1, LANES), lambda b, h: (h, 0, 0))],
        out_specs=pl.BlockSpec((None, None, R, LANES), lambda b, h: (b, h, 0, 0)),
        out_shape=jax.ShapeDtypeStruct((B, H, R, LANES), F32),
        compiler_params=_cparams("parallel", "parallel"),
        name="fox_cum",
    )(f_logit, bias)


FOX_BIAS_LANES = 3


def _fox_kernel(q_ref, k_ref, v_ref, c_ref, o_ref, ka_ref, va_ref, *, tq):
    i = pl.program_id(2)
    tk = tq
    T = k_ref.shape[0]
    chunk = 512

    @pl.when(i == 0)
    def _():
        lane = lax.broadcasted_iota(jnp.int32, (chunk, LANES), 1)
        ri = lax.broadcasted_iota(jnp.int32, (16, LANES), 0)
        ci = lax.broadcasted_iota(jnp.int32, (16, LANES), 1)
        place = jnp.where((ci == ri + HEAD_DIM) & (ri < FOX_BIAS_LANES), 1.0, 0.0).astype(BF16)

        def build(c, _):
            c0 = pl.multiple_of(c * chunk, chunk)
            kp = k_ref[pl.ds(c0, chunk), :].astype(F32)
            vp = v_ref[pl.ds(c0, chunk), :].astype(F32)
            for hh in range(2):
                hi, mid, lo = _split3(-c_ref[hh, :, pl.ds(c0, chunk)])
                terms = jnp.concatenate([hi, mid, lo, jnp.zeros((13, chunk), BF16)], axis=0)
                bias = _dot_tn(terms, place)
                kh = kp if hh == 0 else pltpu.roll(kp, HEAD_DIM, 1)
                vh = vp if hh == 0 else pltpu.roll(vp, HEAD_DIM, 1)
                ka_ref[hh, pl.ds(c0, chunk), :] = jnp.where(lane < HEAD_DIM, kh, bias).astype(BF16)
                va_ref[hh, pl.ds(c0, chunk), :] = jnp.where(lane < HEAD_DIM, vh, 1.0).astype(BF16)
            return 0

        lax.fori_loop(0, T // chunk, build, 0)

    lane = lax.broadcasted_iota(jnp.int32, (tq, LANES), 1)
    ones_lanes = (lane >= HEAD_DIM) & (lane < HEAD_DIM + FOX_BIAS_LANES)
    qp = q_ref[...].astype(F32) * (HEAD_DIM ** -0.5 * LOG2E)
    qs = [jnp.where(lane < HEAD_DIM, qh, jnp.where(ones_lanes, 1.0, 0.0)).astype(BF16)
          for qh in (qp, pltpu.roll(qp, HEAD_DIM, 1))]

    def update(hh, m, acc, q, start, size, mask=None):
        s = _dot_nt(q, ka_ref[hh, pl.ds(start, size), :])
        if mask is not None:
            s = jnp.where(mask, s, NEG_INF)
        m_new = jnp.maximum(m, jnp.max(s, axis=-1, keepdims=True))
        p = jnp.exp2(s - m_new)
        return m_new, jnp.exp2(m - m_new) * acc + _dot(p.astype(BF16), va_ref[hh, pl.ds(start, size), :])

    def step(j, carry):
        start = pl.multiple_of(j * tk, tk)
        return tuple(update(hh, *carry[hh], qs[hh], start, tk) for hh in range(2))

    one = (jnp.full((tq, 1), NEG_INF, F32), jnp.zeros((tq, LANES), F32))
    carry = lax.fori_loop(0, i, step, (one, one))

    half = tq // 2
    start = pl.multiple_of(i * tk, tk)
    row = lax.broadcasted_iota(jnp.int32, (tq, half), 0)
    col = lax.broadcasted_iota(jnp.int32, (tq, half), 1)
    accs = []
    for hh in range(2):
        m, acc = update(hh, *carry[hh], qs[hh], start, half, col <= row)
        _, low = update(hh, m[half:], acc[half:], qs[hh][half:], start + half, half, (col <= row)[:half])
        accs.append(jnp.concatenate([acc[:half], low], axis=0))
    acc0, acc1 = accs
    o0 = acc0 / acc0[:, HEAD_DIM:HEAD_DIM + 1]
    o1 = acc1 / acc1[:, HEAD_DIM:HEAD_DIM + 1]
    o_ref[...] = jnp.where(lane < HEAD_DIM, o0, pltpu.roll(o1, HEAD_DIM, 1)).astype(o_ref.dtype)


def fox_attention(p2, cum, T, *, tq=FOX_TQ):
    B = p2.shape[0]
    HP = FOX_HEADS // 2
    return pl.pallas_call(
        functools.partial(_fox_kernel, tq=tq),
        grid=(B, HP, T // tq),
        in_specs=[pl.BlockSpec((None, tq, LANES), lambda b, h, i: (b, i, P2_FQ // LANES + h)),
                  pl.BlockSpec((None, T, LANES), lambda b, h, i: (b, 0, P2_FK // LANES + h)),
                  pl.BlockSpec((None, T, LANES), lambda b, h, i: (b, 0, P2_FV // LANES + h)),
                  pl.BlockSpec((None, None, 2, 1, T), lambda b, h, i: (b, h, 0, 0, 0))],
        out_specs=pl.BlockSpec((None, tq, LANES), lambda b, h, i: (b, i, h)),
        out_shape=jax.ShapeDtypeStruct((B, T, FOX_HEADS * HEAD_DIM), BF16),
        scratch_shapes=[pltpu.VMEM((2, T, LANES), BF16), pltpu.VMEM((2, T, LANES), BF16)],
        compiler_params=_cparams("parallel", "parallel", "arbitrary"),
        name="fox_attention",
    )(p2, p2, p2, cum)


def _readout_kernel(ocmp_ref, osel_ref, owin_ref, small_ref, oret_ref, ofox_ref, mg_ref, x_ref, g1_ref,
                    ex_ref, wn_ref, wr_ref, wf_ref, wo_ref, o_ref):
    W = NSA_OUT
    gs = _sigmoid(small_ref[...].astype(F32)).astype(BF16)
    ge = _dot(gs, ex_ref[...])
    onsa = (ge[:, :W] * ocmp_ref[...].astype(F32) + ge[:, W:2 * W] * osel_ref[...].astype(F32)
            + ge[:, 2 * W:] * owin_ref[...].astype(F32))
    D = D_MODEL
    merged = (_sigmoid(mg_ref[:, :D].astype(F32)) * _dot(onsa.astype(BF16), wn_ref[...])
              + _sigmoid(mg_ref[:, D:2 * D].astype(F32)) * _dot(oret_ref[...], wr_ref[...])
              + _sigmoid(mg_ref[:, 2 * D:].astype(F32)) * _dot(ofox_ref[...], wf_ref[...]))
    y = _dot(merged.astype(BF16), wo_ref[...])
    o_ref[...] = x_ref[...] + g1_ref[...] * y


def readout(o_cmp, o_sel, o_win, p2, o_ret, o_fox, x, mod_l, ex, wn, wr, wf, wo, l, T, *, tm=512):
    M, D = x.shape
    per_b = T // tm
    W = NSA_OUT
    row = lambda width, col=0: pl.BlockSpec((tm, width), lambda i: (i, col))
    full = lambda a: pl.BlockSpec(a.shape, lambda i: (0,) * a.ndim)
    return pl.pallas_call(
        _readout_kernel,
        grid=(M // tm,),
        in_specs=[row(W), row(W), row(W), row(LANES, P2_SMALL // LANES), row(512), row(512),
                  row(3 * D, 0), row(D),
                  pl.BlockSpec((None, None, 1, D), lambda i: (i // per_b, 2, 0, 0)),
                  full(ex), _layer_spec(wn, l), _layer_spec(wr, l), _layer_spec(wf, l), _layer_spec(wo, l)],
        out_specs=row(D),
        out_shape=jax.ShapeDtypeStruct((M, D), F32),
        compiler_params=_cparams("parallel"),
        name="mixer_readout",
    )(o_cmp, o_sel, o_win, p2, o_ret, o_fox, p2, x, mod_l, ex, wn, wr, wf, wo)


def nsa_gate_expand():
    ex = np.zeros((LANES, 3 * NSA_OUT), np.float32)
    for br in range(3):
        for h in range(NSA_HEADS):
            c0 = br * NSA_OUT + h * HEAD_DIM
            ex[br * NSA_HEADS + h, c0:c0 + HEAD_DIM] = 1.0
    return jnp.asarray(ex, BF16)


FFN_CHUNK = 512


def _ffn_kernel(x_ref, nw_ref, sc_ref, sh_ref, g2_ref, w1_ref, w3_ref, w2_ref, o_ref):
    x = x_ref[...]
    h = _norm_mod(x, nw_ref[...], sc_ref[...], sh_ref[...]).astype(BF16)
    F = w1_ref.shape[1]
    y = None
    for c0 in range(0, F, FFN_CHUNK):
        cols = slice(c0, min(c0 + FFN_CHUNK, F))
        u = _dot(h, w1_ref[:, cols])
        v = _dot(h, w3_ref[:, cols])
        part = _dot((u * _sigmoid(u) * v).astype(BF16), w2_ref[cols, :])
        y = part if y is None else y + part
    o_ref[...] = x + g2_ref[...] * y


def ffn(x, mod_l, nw, w1, w3, w2, T, *, tm=512):
    M, D = x.shape
    F = w1.shape[1]
    per_b = T // tm
    modspec = lambda k: pl.BlockSpec((None, None, 1, D), lambda i: (i // per_b, k, 0, 0))
    full = lambda a: pl.BlockSpec(a.shape, lambda i: (0,) * a.ndim)
    return pl.pallas_call(
        _ffn_kernel,
        grid=(M // tm,),
        in_specs=[pl.BlockSpec((tm, D), lambda i: (i, 0)),
                  pl.BlockSpec((1, D), lambda i: (0, 0)),
                  modspec(4), modspec(3), modspec(5), full(w1), full(w3), full(w2)],
        out_specs=pl.BlockSpec((tm, D), lambda i: (i, 0)),
        out_shape=jax.ShapeDtypeStruct((M, D), F32),
        compiler_params=_cparams("parallel"),
        name="ffn_dense",
    )(x, nw, mod_l, mod_l, mod_l, w1, w3, w2)


MOE_TC = 512
MOE_TS = 512


def _router_kernel(x_ref, nw_ref, sc_ref, sh_ref, wh_ref, wl_ref, h_ref, gate_ref, rank_ref, cnt_ref, carry_ref):
    @pl.when(pl.program_id(0) == 0)
    def _():
        carry_ref[...] = jnp.zeros_like(carry_ref)

    h = _norm_mod(x_ref[...], nw_ref[...], sc_ref[...], sh_ref[...])
    hh = h.astype(BF16)
    h_ref[...] = _pack_bf16_pairs(hh.astype(F32))
    hl = (h - hh.astype(F32)).astype(BF16)
    logits = _dot(hh, wh_ref[...]) + (_dot(hl, wh_ref[...]) + _dot(hh, wl_ref[...]))
    tm = logits.shape[0]
    lane = lax.broadcasted_iota(jnp.int32, logits.shape, 1)
    logits = jnp.where(lane < N_EXPERTS, logits, REMOVED)
    lane_f = lane.astype(F32)
    v1 = jnp.max(logits, axis=-1, keepdims=True)
    i1 = jnp.min(jnp.where(logits == v1, lane_f, float(LANES)), axis=-1, keepdims=True)
    rest = jnp.where(lane_f == i1, REMOVED, logits)
    v2 = jnp.max(rest, axis=-1, keepdims=True)
    i2 = jnp.min(jnp.where(rest == v2, lane_f, float(LANES)), axis=-1, keepdims=True)
    e2 = jnp.exp(v2 - v1)
    w1 = 1.0 / (1.0 + e2)
    w2 = e2 / (1.0 + e2)
    gate_ref[...] = jnp.where(lane_f == i1, w1, jnp.where(lane_f == i2, w2, 0.0))

    sel = jnp.where((lane_f == i1) | (lane_f == i2), 1.0, 0.0)
    ri = lax.broadcasted_iota(jnp.int32, (tm, tm), 0)
    ci = lax.broadcasted_iota(jnp.int32, (tm, tm), 1)
    before = jnp.where(ci < ri, 1.0, 0.0).astype(BF16)
    rank = _dot(before, sel.astype(BF16)) + carry_ref[0:1, :]
    rank_ref[...] = jnp.where(sel > 0.0, rank, -1.0)
    carry_ref[...] = carry_ref[...] + jnp.sum(sel, axis=0, keepdims=True)
    cnt_ref[...] = carry_ref[...]


def router(x, mod_l, nw, w_router, T):
    M, D = x.shape
    tm = MOE_TC
    per_b = T // tm
    wp = jnp.zeros((D, LANES), F32).at[:, :N_EXPERTS].set(w_router)
    wh = wp.astype(BF16)
    wl = (wp - wh.astype(F32)).astype(BF16)
    return pl.pallas_call(
        _router_kernel,
        grid=(M // tm,),
        in_specs=[pl.BlockSpec((tm, D), lambda i: (i, 0)),
                  pl.BlockSpec((1, D), lambda i: (0, 0))]
        + _mod_specs(T, tm, 4, 3, 1)
        + [pl.BlockSpec((D, LANES), lambda i: (0, 0)),
           pl.BlockSpec((D, LANES), lambda i: (0, 0))],
        out_specs=[pl.BlockSpec((tm, D // 2), lambda i: (i, 0)),
                   pl.BlockSpec((tm, LANES), lambda i: (i, 0)),
                   pl.BlockSpec((tm, LANES), lambda i: (i, 0)),
                   pl.BlockSpec((8, LANES), lambda i: (0, 0))],
        out_shape=[jax.ShapeDtypeStruct((M, D // 2), jnp.uint32),
                   jax.ShapeDtypeStruct((M, LANES), F32),
                   jax.ShapeDtypeStruct((M, LANES), F32),
                   jax.ShapeDtypeStruct((8, LANES), F32)],
        scratch_shapes=[pltpu.VMEM((8, LANES), F32)],
        compiler_params=_cparams("arbitrary"),
        name="moe_router",
    )(x, nw, mod_l, mod_l, wh, wl)


def _count_le(sorted_vals, x):
    return jnp.sum(sorted_vals[None, :] <= x[:, None], axis=1, dtype=jnp.int32)


def _moe_up_kernel(e_r, total, x_ref, w1_ref, w3_ref, o_ref, w1b_ref, w3b_ref):
    r = pl.program_id(1)
    live = r < total[0]

    @pl.when(live & ((r == 0) | (e_r[r] != e_r[jnp.maximum(r - 1, 0)])))
    def _():
        w1b_ref[...] = w1_ref[...].astype(BF16)
        w3b_ref[...] = w3_ref[...].astype(BF16)

    @pl.when(live)
    def _():
        x = _unpack_bf16_pairs(x_ref[...]).astype(BF16)
        u = _dot(x, w1b_ref[...])
        v = _dot(x, w3b_ref[...])
        o_ref[...] = (u * _sigmoid(u) * v).astype(o_ref.dtype)


def moe_up(xs, w1, w3, tiles, rt, *, tf=1792):
    R = xs.shape[0]
    D = w1.shape[1]
    ts = MOE_TS
    F = w1.shape[-1]
    live = lambda r, total: jnp.minimum(r, total[0] - 1)
    return pl.pallas_call(
        _moe_up_kernel,
        grid_spec=pltpu.PrefetchScalarGridSpec(
            num_scalar_prefetch=2,
            grid=(F // tf, rt),
            in_specs=[pl.BlockSpec((ts, D // 2), lambda n, r, e, total: (live(r, total), 0)),
                      pl.BlockSpec((None, D, tf), lambda n, r, e, total: (e[live(r, total)], 0, n)),
                      pl.BlockSpec((None, D, tf), lambda n, r, e, total: (e[live(r, total)], 0, n))],
            out_specs=pl.BlockSpec((ts, tf), lambda n, r, e, total: (r, n)),
            scratch_shapes=[pltpu.VMEM((D, tf), BF16), pltpu.VMEM((D, tf), BF16)],
        ),
        out_shape=jax.ShapeDtypeStruct((R, F), BF16),
        compiler_params=_cparams("arbitrary", "arbitrary"),
        name="moe_up",
    )(tiles["e"], tiles["total"], xs, w1, w3)


def _moe_down_kernel(e_r, total, a_ref, w2_ref, o_ref, w2b_ref):
    r = pl.program_id(0)
    live = r < total[0]

    @pl.when(live & ((r == 0) | (e_r[r] != e_r[jnp.maximum(r - 1, 0)])))
    def _():
        w2b_ref[...] = w2_ref[...].astype(BF16)

    @pl.when(live)
    def _():
        o_ref[...] = _pack_bf16_pairs(_dot(a_ref[...], w2b_ref[...]))


def moe_down(a, w2, tiles, rt):
    R, F = a.shape
    ts = MOE_TS
    D = w2.shape[-1]
    live = lambda r, total: jnp.minimum(r, total[0] - 1)
    return pl.pallas_call(
        _moe_down_kernel,
        grid_spec=pltpu.PrefetchScalarGridSpec(
            num_scalar_prefetch=2,
            grid=(rt,),
            in_specs=[pl.BlockSpec((ts, F), lambda r, e, total: (live(r, total), 0)),
                      pl.BlockSpec((None, F, D), lambda r, e, total: (e[live(r, total)], 0, 0))],
            out_specs=pl.BlockSpec((ts, D // 2), lambda r, e, total: (r, 0)),
            scratch_shapes=[pltpu.VMEM((F, D), BF16)],
        ),
        out_shape=jax.ShapeDtypeStruct((R, D // 2), jnp.uint32),
        compiler_params=_cparams("arbitrary"),
        name="moe_down",
    )(tiles["e"], tiles["total"], a, w2)


SC_WINDOW = 64


def _sc_mesh():
    return plsc.VectorSubcoreMesh(core_axis_name="core", subcore_axis_name="subcore")


def sc_scatter_rows2(x, idx_a, idx_b, n_out):
    n, d = x.shape
    steps = n // SC_WINDOW

    @pl.kernel(out_type=jax.ShapeDtypeStruct((n_out, d), x.dtype), mesh=_sc_mesh(), scratch_types=[])
    def kern(x_hbm, ia_hbm, ib_hbm, o_hbm):
        def body(x_vmem, ia_vmem, ib_vmem):
            pltpu.sync_copy(x_vmem, o_hbm.at[ia_vmem.at[0]])
            pltpu.sync_copy(x_vmem, o_hbm.at[ib_vmem.at[0]])

        pltpu.emit_pipeline(
            body,
            grid=(steps,),
            in_specs=[pl.BlockSpec((SC_WINDOW, d), index_map=lambda i: (i, 0)),
                      pl.BlockSpec((1, SC_WINDOW), index_map=lambda i: (i, 0)),
                      pl.BlockSpec((1, SC_WINDOW), index_map=lambda i: (i, 0))],
            out_specs=[],
            core_axis_name=("core", "subcore"),
            dimension_semantics=(pltpu.PARALLEL,),
        )(x_hbm, ia_hbm, ib_hbm)

    return kern(x, idx_a.reshape(steps, SC_WINDOW), idx_b.reshape(steps, SC_WINDOW))


def sc_gather_rows(x, idx):
    n = idx.shape[0]
    d = x.shape[1]
    steps = n // SC_WINDOW

    @pl.kernel(out_type=jax.ShapeDtypeStruct((n, d), x.dtype), mesh=_sc_mesh(), scratch_types=[])
    def kern(x_hbm, i_hbm, o_hbm):
        def body(i_vmem, o_vmem):
            pltpu.sync_copy(x_hbm.at[i_vmem.at[0]], o_vmem)

        pltpu.emit_pipeline(
            body,
            grid=(steps,),
            in_specs=[pl.BlockSpec((1, SC_WINDOW), index_map=lambda i: (i, 0))],
            out_specs=[pl.BlockSpec((SC_WINDOW, d), index_map=lambda i: (i, 0))],
            core_axis_name=("core", "subcore"),
            dimension_semantics=(pltpu.PARALLEL,),
        )(i_hbm, o_hbm)

    return kern(x, idx.reshape(steps, SC_WINDOW))


def _moe_finish_kernel(x_ref, g2_ref, ya_ref, yb_ref, gate_ref, rank_ref, nw_ref, o_ref, *, normalize):
    gate = gate_ref[...]
    chosen = rank_ref[...] >= 0.0
    lane = lax.broadcasted_iota(jnp.int32, gate.shape, 1).astype(F32)
    first = jnp.min(jnp.where(chosen, lane, float(LANES)), axis=-1, keepdims=True)
    last = jnp.max(jnp.where(chosen, lane, -1.0), axis=-1, keepdims=True)
    wa = jnp.sum(jnp.where(lane == first, gate, 0.0), axis=-1, keepdims=True)
    wb = jnp.sum(jnp.where(lane == last, gate, 0.0), axis=-1, keepdims=True)
    x = x_ref[...] + g2_ref[...] * (wa * _unpack_bf16_pairs(ya_ref[...]) + wb * _unpack_bf16_pairs(yb_ref[...]))
    if normalize:
        ms = jnp.mean(x * x, axis=-1, keepdims=True)
        x = x * lax.rsqrt(ms + NORM_EPS) * nw_ref[...]
    o_ref[...] = x


def moe_finish(x, mod_l, y2, gate, rank, norm_w, T, *, tm=512):
    M, D = x.shape
    per_b = T // tm
    normalize = norm_w is not None
    if norm_w is None:
        norm_w = jnp.ones((1, D), F32)
    return pl.pallas_call(
        functools.partial(_moe_finish_kernel, normalize=normalize),
        grid=(M // tm,),
        in_specs=[pl.BlockSpec((tm, D), lambda i: (i, 0)),
                  pl.BlockSpec((None, None, 1, D), lambda i: (i // per_b, 5, 0, 0)),
                  pl.BlockSpec((None, tm, D // 2), lambda i: (0, i, 0)),
                  pl.BlockSpec((None, tm, D // 2), lambda i: (1, i, 0)),
                  pl.BlockSpec((tm, LANES), lambda i: (i, 0)),
                  pl.BlockSpec((tm, LANES), lambda i: (i, 0)),
                  pl.BlockSpec((1, D), lambda i: (0, 0))],
        out_specs=pl.BlockSpec((tm, D), lambda i: (i, 0)),
        out_shape=jax.ShapeDtypeStruct((M, D), F32),
        compiler_params=_cparams("parallel"),
        name="moe_finish",
    )(x, mod_l, y2, y2, gate, rank, norm_w)


def moe_ffn(x, mod_l, nw, w_router, w1, w3, w2, T, norm_w=None):
    M = x.shape[0]
    ts = MOE_TS
    rt = (2 * M) // ts + N_EXPERTS
    h, gate, rank, cnt = router(x, mod_l, nw, w_router, T)
    i32 = jnp.int32
    counts = cnt[0, :N_EXPERTS].astype(i32)
    ntile = (counts + ts - 1) // ts
    tile_end = jnp.cumsum(ntile)
    row_off = (tile_end - ntile) * ts
    e_r = jnp.minimum(_count_le(tile_end, jnp.arange(rt, dtype=i32)), N_EXPERTS - 1)
    tiles = dict(e=e_r, total=tile_end[-1].reshape(1).astype(i32))
    rk = rank[:, :N_EXPERTS].astype(i32)
    pos = row_off[None, :] + rk
    pos_a = jnp.min(jnp.where(rk >= 0, pos, rt * ts), axis=1)
    pos_b = jnp.max(jnp.where(rk >= 0, pos, -1), axis=1)

    xs = sc_scatter_rows2(h, pos_a, pos_b, rt * ts)
    a = moe_up(xs, w1, w3, tiles, rt)
    y = moe_down(a, w2, tiles, rt)
    y2 = sc_gather_rows(y, jnp.concatenate([pos_a, pos_b])).reshape(2, M, -1)
    return moe_finish(x, mod_l, y2, gate, rank, norm_w, T)


def _final_norm_kernel(x_ref, w_ref, o_ref):
    x = x_ref[...]
    ms = jnp.mean(x * x, axis=-1, keepdims=True)
    o_ref[...] = x * lax.rsqrt(ms + NORM_EPS) * w_ref[...]


def final_norm(x, w, *, tm=1024):
    M, D = x.shape
    return pl.pallas_call(
        _final_norm_kernel,
        grid=(M // tm,),
        in_specs=[pl.BlockSpec((tm, D), lambda i: (i, 0)), pl.BlockSpec((1, D), lambda i: (0, 0))],
        out_specs=pl.BlockSpec((tm, D), lambda i: (i, 0)),
        out_shape=jax.ShapeDtypeStruct((M, D), F32),
        compiler_params=_cparams("parallel"),
        name="final_norm",
    )(x, w)


def nsa_constants(T):
    n_sel = T // SEL_LEN
    nsp = max(LANES, n_sel)
    ncp = T // CMP_STRIDE
    cmp_start = np.arange(ncp) * CMP_STRIDE
    sel_start = np.arange(nsp) * SEL_LEN
    ov = ((cmp_start[:, None] < sel_start[None, :] + SEL_LEN)
          & (cmp_start[:, None] + CMP_LEN > sel_start[None, :]))
    ov[(T - CMP_LEN) // CMP_STRIDE + 1:] = False
    ov[:, n_sel:] = False
    et_mat = ((np.arange(T)[:, None] // SEL_LEN) == np.arange(nsp)[None, :]) * SEL_BONUS
    return jnp.asarray(ov.T, BF16), jnp.asarray(et_mat, BF16)


def token_mixing(x, mod_l, lw, consts, B, T):
    M = B * T
    cos_t, sin_t, ov_t, e_mat, ret_consts, ex = consts
    l = lw["layer"]
    p1 = proj_rope(x, mod_l, lw["norm_mix"], lw["w1"], l, cos_t, sin_t, p1_scales(), T).reshape(B, T, P1_COLS)
    p2 = proj_plain(x, mod_l, lw["norm_mix"], lw["w2"], l, T).reshape(B, T, P2_COLS)

    def group_rows(a):
        return a.reshape(B, T, NSA_GROUPS, HEAD_DIM).transpose(0, 2, 1, 3).reshape(
            B, NSA_GROUPS, T // CMP_STRIDE, CMP_STRIDE * HEAD_DIM)

    xr = jnp.stack([group_rows(p1[:, :, P1_NKC:P1_NKC + LANES]), group_rows(p2[:, :, P2_NVC:P2_NVC + LANES])])
    cmp_out = compress(xr, lw["cmp_pe"], lw["cmp_w1"], lw["cmp_w2"])
    cmp_out = cmp_out.transpose(0, 1, 3, 2, 4).reshape(2, B, T // CMP_STRIDE, LANES)
    o_cmp, sel, o_win = nsa_cmp_select_window(p1, p2, cmp_out[0], cmp_out[1], ov_t, T)
    o_sel = nsa_selected(p1, nsa_value_augment(p2[:, :, P2_NVS:P2_NVS + LANES]), sel, e_mat, T)

    o_ret = retention(p1, p2, ret_consts, T)

    ff = p2[:, :, P2_SMALL + 3 * NSA_HEADS:P2_SMALL + 3 * NSA_HEADS + FOX_HEADS].astype(F32)
    ff = ff.transpose(0, 2, 1).reshape(B, FOX_HEADS, T // LANES, LANES)
    cum = fox_cum(ff, lw["fox_bias"]).reshape(B, FOX_HEADS // 2, 2, 1, T)
    o_fox = fox_attention(p2, cum, T)

    return readout(o_cmp.reshape(M, -1), o_sel.reshape(M, -1), o_win.reshape(M, -1), p2.reshape(M, P2_COLS),
                   o_ret.reshape(M, -1), o_fox.reshape(M, -1), x, mod_l, ex,
                   lw["wn"], lw["wr"], lw["wf"], lw["wo"], l, T)


def mixer_weights(norm_mix, w_in, cmp_k_pe, cmp_k_w1, cmp_k_w2, cmp_v_pe, cmp_v_w1, cmp_v_w2, fox_f_bias,
                  w_read_nsa, w_read_ret, w_read_fox, w_out):
    depth = w_in.shape[0]
    w1, w2 = split_w_in(w_in)
    pe = jnp.stack([cmp_k_pe.reshape(depth, 1, -1), cmp_v_pe.reshape(depth, 1, -1)], axis=1)
    pe = jnp.broadcast_to(pe, (depth, 2, 8, pe.shape[-1])).astype(BF16)
    shared = {
        "w1": w1, "w2": w2,
        "wn": w_read_nsa.astype(BF16),
        "wr": w_read_ret.astype(BF16),
        "wf": w_read_fox.astype(BF16),
        "wo": w_out.astype(BF16),
    }
    cmp_w1 = jnp.stack([cmp_k_w1, cmp_v_w1], axis=1).astype(BF16)
    cmp_w2 = jnp.stack([cmp_k_w2, cmp_v_w2], axis=1).astype(BF16)
    return [dict(shared, layer=l, norm_mix=norm_mix[l].reshape(1, -1), cmp_pe=pe[l], cmp_w1=cmp_w1[l], cmp_w2=cmp_w2[l],
                 fox_bias=jnp.broadcast_to(fox_f_bias[l][:, None, None], (FOX_HEADS, 1, LANES)))
            for l in range(depth)]


def kernel(x, c, ada_w, ada_b, norm_mix, norm_ffn, w_in, cmp_k_pe, cmp_k_w1, cmp_k_w2, cmp_v_pe, cmp_v_w1,
           cmp_v_w2, fox_f_bias, w_read_nsa, w_read_ret, w_read_fox, w_out, ffn_w1, ffn_w3, ffn_w2, router_w,
           moe_w1, moe_w3, moe_w2, final_norm_w):
    B, T, D = x.shape
    M = B * T
    depth = ada_w.shape[0]
    mod = modulation(c, ada_w, ada_b)
    cos_t, sin_t = rope_tables(T)
    ov_t, e_mat = nsa_constants(T)
    consts = (cos_t, sin_t, ov_t, e_mat, retention_consts(), nsa_gate_expand())
    xs = x.reshape(M, D)
    lws = mixer_weights(norm_mix, w_in, cmp_k_pe, cmp_k_w1, cmp_k_w2, cmp_v_pe, cmp_v_w1, cmp_v_w2,
                        fox_f_bias, w_read_nsa, w_read_ret, w_read_fox, w_out)
    for l in range(depth):
        xs = token_mixing(xs, mod[l], lws[l], consts, B, T)
        nf = norm_ffn[l].reshape(1, D)
        if l % 2 == 0:
            k = l // 2
            xs = ffn(xs, mod[l], nf, ffn_w1[k].astype(BF16), ffn_w3[k].astype(BF16), ffn_w2[k].astype(BF16), T)
        else:
            k = l // 2
            fuse = final_norm_w.reshape(1, D) if l == depth - 1 else None
            xs = moe_ffn(xs, mod[l], nf, router_w[k], moe_w1[k], moe_w3[k], moe_w2[k], T, fuse)
    if depth % 2 == 1:
        xs = final_norm(xs, final_norm_w.reshape(1, D))
    return xs.reshape(B, T, D)
```

```python
import functools

import jax
import jax.numpy as jnp
import numpy as np
from jax import lax
from jax.experimental import pallas as pl
from jax.experimental.pallas import tpu as pltpu
from jax.experimental.pallas import tpu_sc as plsc

F32 = jnp.float32
BF16 = jnp.bfloat16

D_MODEL = 1024
HEAD_DIM = 64
ROPE_THETA = 10000.0
NORM_EPS = 1e-6
NEG_INF = -1e30
REMOVED = -3e38

NSA_HEADS = 8
NSA_GROUPS = 2
NSA_HPG = NSA_HEADS // NSA_GROUPS
CMP_LEN = 32
CMP_STRIDE = 16
SEL_LEN = 64
SEL_TOPN = 16
WINDOW = 512
NSA_QBLOCK = 256

RET_HEADS = 4
RET_QK_DIM = 64
RET_CHUNK = 128

FOX_HEADS = 8
FOX_TQ = 1024
LOG2E = 1.4426950408889634

N_EXPERTS = 8

LANES = 128
VMEM_LIMIT = 56 * 1024 * 1024

P1_NQ = 0
P1_RQ = 512
P1_RK = 768
P1_NKC = 1024
P1_NKS = 1152
P1_NKW = 1280
P1_COLS = 1408
P2_MG = 0
P2_RV = 3072
P2_RG = 3584
P2_FQ = 4096
P2_FK = 4608
P2_FV = 5120
P2_NVC = 5632
P2_NVS = 5760
P2_NVW = 5888
P2_SMALL = 6016
P2_COLS = 6144
NSA_OUT = NSA_HEADS * HEAD_DIM


def _layer_spec(w, l):
    zeros = (0,) * (w.ndim - 1)
    return pl.BlockSpec((None,) + w.shape[1:], lambda *_: (l,) + zeros)


def _cparams(*sem):
    return pltpu.CompilerParams(dimension_semantics=tuple(sem), vmem_limit_bytes=VMEM_LIMIT)


def _sigmoid(x):
    return 1.0 / (1.0 + jnp.exp(-x))


def _dot(a, b):
    return jnp.dot(a, b, preferred_element_type=F32)


def _dot_nt(a, b):
    return lax.dot_general(a, b, (((1,), (1,)), ((), ())), preferred_element_type=F32)


def _dot_tn(a, b):
    return lax.dot_general(a, b, (((0,), (0,)), ((), ())), preferred_element_type=F32)


def _split3(x):
    hi = x.astype(BF16)
    r1 = x - hi.astype(F32)
    mid = r1.astype(BF16)
    lo = (r1 - mid.astype(F32)).astype(BF16)
    return hi, mid, lo


def _pack_bf16_pairs(x):
    c = x.shape[1] // 2
    lo = pltpu.bitcast(x[:, :c].astype(BF16).astype(F32), jnp.uint32) >> 16
    hi = pltpu.bitcast(x[:, c:].astype(BF16).astype(F32), jnp.uint32) & jnp.uint32(0xFFFF0000)
    return hi | lo


def _unpack_bf16_pairs(u):
    lo = pltpu.bitcast(u << 16, F32)
    hi = pltpu.bitcast(u & jnp.uint32(0xFFFF0000), F32)
    return jnp.concatenate([lo, hi], axis=1)


def _norm_mod(x, nw, sc, sh):
    ms = jnp.mean(x * x, axis=-1, keepdims=True)
    y = x * lax.rsqrt(ms + NORM_EPS) * nw
    return y * (1.0 + sc) + sh


def _mod_kernel(c_ref, w_ref, b_ref, o_ref):
    c = c_ref[...]
    s = c * _sigmoid(c)
    o_ref[0] = _dot(s.astype(BF16), w_ref[0].astype(BF16)) + b_ref[0]


def modulation(c, ada_w, ada_b):
    B, D = c.shape
    depth = ada_w.shape[0]
    rows = 8
    c_pad = jnp.zeros((rows, D), F32).at[:B].set(c)
    out = pl.pallas_call(
        _mod_kernel,
        grid=(depth, 6),
        in_specs=[pl.BlockSpec((rows, D), lambda l, j: (0, 0)),
                  pl.BlockSpec((1, D, D), lambda l, j: (l, 0, j)),
                  pl.BlockSpec((1, 1, D), lambda l, j: (l, 0, j))],
        out_specs=pl.BlockSpec((1, rows, D), lambda l, j: (l, 0, j)),
        out_shape=jax.ShapeDtypeStruct((depth, rows, 6 * D), F32),
        compiler_params=_cparams("parallel", "parallel"),
        name="modulation",
    )(c_pad, ada_w, ada_b.reshape(depth, 1, 6 * D))
    return out[:, :B].reshape(depth, B, 6, 1, D)


def _proj_plain_kernel(x_ref, nw_ref, sc_ref, sh_ref, w_ref, o_ref, *, tn):
    h = _norm_mod(x_ref[...], nw_ref[...], sc_ref[...], sh_ref[...]).astype(BF16)
    for n in range(w_ref.shape[1] // tn):
        cols = slice(n * tn, (n + 1) * tn)
        o_ref[:, cols] = _dot(h, w_ref[:, cols]).astype(o_ref.dtype)


def _proj_rope_kernel(x_ref, nw_ref, sc_ref, sh_ref, w_ref, cos_ref, sin_ref, o_ref, *, scales):
    h = _norm_mod(x_ref[...], nw_ref[...], sc_ref[...], sh_ref[...]).astype(BF16)
    y = _dot(h, w_ref[...])
    cos = cos_ref[...]
    sin = sin_ref[...]
    lane = lax.broadcasted_iota(jnp.int32, cos.shape, 1)
    first_half = (lane % HEAD_DIM) < (HEAD_DIM // 2)
    for g, scale in enumerate(scales):
        yg = y[:, g * LANES:(g + 1) * LANES]
        rot = jnp.where(first_half, pltpu.roll(yg, LANES - HEAD_DIM // 2, 1),
                        pltpu.roll(yg, HEAD_DIM // 2, 1))
        r = yg * cos + rot * sin
        if scale != 1.0:
            r = r * scale
        o_ref[:, g * LANES:(g + 1) * LANES] = r.astype(o_ref.dtype)


def _mod_specs(T, tm, sc_idx, sh_idx, nargs):
    per_b = T // tm
    if nargs == 1:
        return [pl.BlockSpec((None, None, 1, D_MODEL), lambda i: (i // per_b, sc_idx, 0, 0)),
                pl.BlockSpec((None, None, 1, D_MODEL), lambda i: (i // per_b, sh_idx, 0, 0))]
    return [pl.BlockSpec((None, None, 1, D_MODEL), lambda i, j: (i // per_b, sc_idx, 0, 0)),
            pl.BlockSpec((None, None, 1, D_MODEL), lambda i, j: (i // per_b, sh_idx, 0, 0))]


def proj_plain(x, mod_l, nw, w, l, T, *, tm=512, tn=512):
    M, D = x.shape
    N = w.shape[-1]
    return pl.pallas_call(
        functools.partial(_proj_plain_kernel, tn=tn),
        grid=(M // tm,),
        in_specs=[pl.BlockSpec((tm, D), lambda i: (i, 0)),
                  pl.BlockSpec((1, D), lambda i: (0, 0))]
        + _mod_specs(T, tm, 1, 0, 1)
        + [_layer_spec(w, l)],
        out_specs=pl.BlockSpec((tm, N), lambda i: (i, 0)),
        out_shape=jax.ShapeDtypeStruct((M, N), BF16),
        compiler_params=_cparams("parallel"),
        name="proj_plain",
    )(x, nw, mod_l, mod_l, w)


def proj_rope(x, mod_l, nw, w, l, cos, sin, scales, T, *, tm=512):
    M, D = x.shape
    N = w.shape[-1]
    per_b = T // tm
    return pl.pallas_call(
        functools.partial(_proj_rope_kernel, scales=scales),
        grid=(M // tm,),
        in_specs=[pl.BlockSpec((tm, D), lambda i: (i, 0)),
                  pl.BlockSpec((1, D), lambda i: (0, 0))]
        + _mod_specs(T, tm, 1, 0, 1)
        + [_layer_spec(w, l),
           pl.BlockSpec((tm, LANES), lambda i: (i % per_b, 0)),
           pl.BlockSpec((tm, LANES), lambda i: (i % per_b, 0))],
        out_specs=pl.BlockSpec((tm, N), lambda i: (i, 0)),
        out_shape=jax.ShapeDtypeStruct((M, N), BF16),
        compiler_params=_cparams("parallel"),
        name="proj_rope",
    )(x, nw, mod_l, mod_l, w, cos, sin)


def rope_tables(T):
    d = HEAD_DIM
    pos = jnp.arange(T, dtype=F32)
    inv = ROPE_THETA ** (-jnp.arange(0, d, 2, dtype=F32) / d)
    ang = pos[:, None] * inv[None, :]
    cos = jnp.cos(ang)
    sin = jnp.sin(ang)
    cos_t = jnp.concatenate([cos, cos, cos, cos], axis=-1)
    sin_t = jnp.concatenate([-sin, sin, -sin, sin], axis=-1)
    return cos_t, sin_t


def split_w_in(w_in):
    sizes = [512, 128, 128, 128, 128, 128, 128, 24, 256, 256, 512, 512, 512, 512, 512, 8, 3072]
    offs = np.cumsum([0] + sizes)
    wb = w_in.astype(BF16)
    (nq, nkc, nvc, nks, nvs, nkw, nvw, ngate, rq, rk, rv, rg, fq, fk, fv, ff, mg) = [
        wb[..., offs[i]:offs[i + 1]] for i in range(len(sizes))]
    small = jnp.concatenate([ngate, ff, jnp.zeros(ngate.shape[:-1] + (LANES - 32,), BF16)], axis=-1)
    w1 = jnp.concatenate([nq, rq, rk, nkc, nks, nkw], axis=-1)
    w2 = jnp.concatenate([mg, rv, rg, fq, fk, fv, nvc, nvs, nvw, small], axis=-1)
    assert w1.shape[-1] == P1_COLS and w2.shape[-1] == P2_COLS
    return w1, w2


def p1_scales():
    s = [1.0] * (P1_COLS // LANES)
    for g in range(P1_NQ // LANES, P1_RQ // LANES):
        s[g] = HEAD_DIM ** -0.5 * LOG2E
    for g in range(P1_RK // LANES, P1_NKC // LANES):
        s[g] = RET_QK_DIM ** -0.5
    return tuple(s)


def _compress_kernel(x_ref, pe_ref, w1_ref, w2_ref, o_ref):
    r = x_ref[...]
    half = r.shape[1]
    w1 = w1_ref[...]
    a = _dot(r, w1[:half])
    b = _dot(r, w1[half:])
    pe = _dot(pe_ref[...], w1)[0:1]
    n = a.shape[0]
    hid = a + pltpu.roll(b, n - 1, 0) + pe
    hid = hid * _sigmoid(hid)
    o_ref[...] = _dot(hid.astype(BF16), w2_ref[...]).astype(o_ref.dtype)


def compress(xr, pe, w1, w2):
    _, B, G, R, W = xr.shape
    H = w1.shape[-1]
    return pl.pallas_call(
        _compress_kernel,
        grid=(2, B, G),
        in_specs=[pl.BlockSpec((None, None, None, R, W), lambda s, b, g: (s, b, g, 0, 0)),
                  pl.BlockSpec((None, 8, 2 * W), lambda s, b, g: (s, 0, 0)),
                  pl.BlockSpec((None, 2 * W, H), lambda s, b, g: (s, 0, 0)),
                  pl.BlockSpec((None, H, HEAD_DIM), lambda s, b, g: (s, 0, 0))],
        out_specs=pl.BlockSpec((None, None, None, R, HEAD_DIM), lambda s, b, g: (s, b, g, 0, 0)),
        out_shape=jax.ShapeDtypeStruct((2, B, G, R, HEAD_DIM), BF16),
        compiler_params=_cparams("parallel", "parallel", "parallel"),
        name="nsa_compress",
    )(xr, pe, w1, w2)


def _stack_heads(q_ref, g):
    tq = q_ref.shape[0]
    half = lax.broadcasted_iota(jnp.int32, (tq, LANES), 1) // HEAD_DIM
    rows = []
    for hh in range(NSA_HPG):
        h = NSA_HPG * g + hh
        x = q_ref[:, (h // 2) * LANES:(h // 2 + 1) * LANES].astype(F32)
        if h % 2 != g:
            x = pltpu.roll(x, HEAD_DIM, 1)
        rows.append(jnp.where(half == g, x, 0.0).astype(BF16))
    return jnp.concatenate(rows, axis=0)


def _store_heads(o_ref, g, o, tq):
    low = lax.broadcasted_iota(jnp.int32, (tq, LANES), 1) < HEAD_DIM
    for pair in range(NSA_HPG // 2):
        even = o[(2 * pair) * tq:(2 * pair + 1) * tq]
        odd = o[(2 * pair + 1) * tq:(2 * pair + 2) * tq]
        if g == 0:
            blk = jnp.where(low, even, pltpu.roll(odd, HEAD_DIM, 1))
        else:
            blk = jnp.where(low, pltpu.roll(even, HEAD_DIM, 1), odd)
        col = (NSA_HPG // 2 * g + pair) * LANES
        o_ref[:, col:col + LANES] = blk.astype(o_ref.dtype)


CMP_CHUNK = 128


def _nsa_cmp_kernel(q_ref, kc_ref, vc_ref, ov_ref, o_ref, m_ref, imp_ref, *, tq, n_sel, top_n):
    t0 = pl.program_id(1) * tq
    ncp = kc_ref.shape[0]
    nsp = ov_ref.shape[0]
    rows = NSA_HPG * tq

    def attend(ncols):
        kc = kc_ref[0:ncols, :]
        vc = vc_ref[0:ncols, :]
        n_idx = lax.broadcasted_iota(jnp.int32, (rows, ncols), 1)
        t_idx = t0 + lax.broadcasted_iota(jnp.int32, (rows, ncols), 0) % tq
        valid = (n_idx * CMP_STRIDE + (CMP_LEN - 1)) <= t_idx
        for g in range(NSA_GROUPS):
            q = _stack_heads(q_ref, g)
            s = jnp.where(valid, _dot_nt(q, kc), NEG_INF)
            m = jnp.max(s, axis=-1, keepdims=True)
            e = jnp.exp2(s - m)
            l = jnp.sum(e, axis=-1, keepdims=True)
            p = e * jnp.where(m > 0.5 * NEG_INF, 1.0 / l, 0.0)
            _store_heads(o_ref, g, _dot(p.astype(BF16), vc), tq)
            psum = p[0:tq]
            for hh in range(1, NSA_HPG):
                psum = psum + p[hh * tq:(hh + 1) * tq]
            imp_ref[g] = _dot_nt(ov_ref[:, 0:ncols], psum.astype(BF16))

    n_live = jnp.maximum((t0 + tq - CMP_LEN) // CMP_STRIDE + 1, 1)
    n_chunks = jnp.minimum((n_live + CMP_CHUNK - 1) // CMP_CHUNK, ncp // CMP_CHUNK)
    for nc in range(1, ncp // CMP_CHUNK + 1):
        pl.when(n_chunks == nc)(functools.partial(attend, nc * CMP_CHUNK))

    j_idx = lax.broadcasted_iota(jnp.int32, (nsp, tq), 0)
    cur = (t0 + lax.broadcasted_iota(jnp.int32, (nsp, tq), 1)) // SEL_LEN
    forced = (j_idx == 0) | (j_idx == cur) | (j_idx == cur - 1)
    j_f = j_idx.astype(F32)
    for g in range(NSA_GROUPS):
        score = jnp.where(j_idx <= cur, imp_ref[g], NEG_INF)
        score = jnp.where(forced | (j_idx >= n_sel), REMOVED, score)
        sel = jnp.where(forced, 1.0, 0.0)
        for _ in range(max(top_n - 3, 0)):
            mx = jnp.max(score, axis=0, keepdims=True)
            idx = jnp.min(jnp.where(score == mx, j_f, float(nsp)), axis=0, keepdims=True)
            hit = j_f == idx
            sel = jnp.where(hit, 1.0, sel)
            score = jnp.where(hit, REMOVED, score)
        sel = jnp.where(j_idx <= cur, sel, 0.0)
        m_ref[g] = sel.T.astype(m_ref.dtype)


SEL_BONUS = 32768.0
NSA_SEL_TQ = 256
NSA_SEL_TK = 1024


def _nsa_sel_kernel(q_ref, k_ref, v_ref, m_ref, et_ref, o_ref, *, tq, tk):
    t0 = pl.program_id(1) * tq
    n_full = t0 // tk
    rows = NSA_HPG * tq

    def update(carry, q, ks, vs, mask=None):
        m, acc = carry
        s = _dot_nt(q, ks)
        if mask is not None:
            s = jnp.where(mask, s, NEG_INF)
        m_new = jnp.maximum(m, jnp.max(s, axis=-1, keepdims=True))
        p = jnp.exp2(s - m_new)
        return m_new, jnp.exp2(m - m_new) * acc + _dot(p.astype(BF16), vs)

    qs, carries = [], []
    for g in range(NSA_GROUPS):
        q = jnp.concatenate([_stack_heads(q_ref, g), jnp.concatenate([m_ref[g]] * NSA_HPG, axis=0)], axis=1)

        def step(j, carry, q=q, g=g):
            start = pl.multiple_of(j * tk, tk)
            ks = jnp.concatenate([k_ref[pl.ds(start, tk), :], et_ref[pl.ds(start, tk), :]], axis=1)
            return update(carry, q, ks, v_ref[g, pl.ds(start, tk), :])

        init = (jnp.full((rows, 1), NEG_INF, F32), jnp.zeros((rows, LANES), F32))
        qs.append(q)
        carries.append(lax.fori_loop(0, n_full, step, init))

    start = pl.multiple_of(n_full * tk, tk)

    def tail(nk):
        trow = t0 + lax.broadcasted_iota(jnp.int32, (rows, nk), 0) % tq
        causal = start + lax.broadcasted_iota(jnp.int32, (rows, nk), 1) <= trow
        ks = jnp.concatenate([k_ref[pl.ds(start, nk), :], et_ref[pl.ds(start, nk), :]], axis=1)
        for g in range(NSA_GROUPS):
            _, acc = update(carries[g], qs[g], ks, v_ref[g, pl.ds(start, nk), :], causal)
            den = HEAD_DIM * (1 - g)
            _store_heads(o_ref, g, acc / acc[:, den:den + 1], tq)

    which = (t0 - start) // tq
    for v in range(tk // tq):
        pl.when(which == v)(functools.partial(tail, (v + 1) * tq))


def nsa_value_augment(v):
    ones = jnp.ones_like(v[..., :HEAD_DIM])
    return jnp.stack([jnp.concatenate([v[..., :HEAD_DIM], ones], axis=-1),
                      jnp.concatenate([ones, v[..., HEAD_DIM:]], axis=-1)], axis=1)


def nsa_selected(p1, v_aug, sel, et_mat, T, *, tq=NSA_SEL_TQ, tk=NSA_SEL_TK):
    B = p1.shape[0]
    nsp = sel.shape[-1]
    return pl.pallas_call(
        functools.partial(_nsa_sel_kernel, tq=tq, tk=tk),
        grid=(B, T // tq),
        in_specs=[pl.BlockSpec((None, tq, NSA_HEADS * HEAD_DIM), lambda b, i: (b, i, 0)),
                  pl.BlockSpec((None, T, LANES), lambda b, i: (b, 0, P1_NKS // LANES)),
                  pl.BlockSpec((None, NSA_GROUPS, T, LANES), lambda b, i: (b, 0, 0, 0)),
                  pl.BlockSpec((None, NSA_GROUPS, tq, nsp), lambda b, i: (b, 0, i, 0)),
                  pl.BlockSpec((T, nsp), lambda b, i: (0, 0))],
        out_specs=pl.BlockSpec((None, tq, NSA_OUT), lambda b, i: (b, i, 0)),
        out_shape=jax.ShapeDtypeStruct((B, T, NSA_OUT), BF16),
        compiler_params=_cparams("parallel", "parallel"),
        name="nsa_selected",
    )(p1, p1, v_aug, sel, et_mat)


def _nsa_win_kernel(q_ref, k_ref, v_ref, b_ref, o_ref, *, tq):
    t0 = pl.program_id(1) * tq
    span = WINDOW + tq
    start = pl.multiple_of(jnp.maximum(t0 - WINDOW, 0), tq)
    ks = k_ref[pl.ds(start, span), :]
    vs = v_ref[pl.ds(start, span), :]

    def run(bias):
        bias = jnp.concatenate([bias] * NSA_HPG, axis=0)
        for g in range(NSA_GROUPS):
            s = _dot_nt(_stack_heads(q_ref, g), ks) + bias
            m = jnp.max(s, axis=-1, keepdims=True)
            p = jnp.exp2(s - m)
            l = jnp.sum(p, axis=-1, keepdims=True)
            _store_heads(o_ref, g, _dot(p.astype(BF16), vs) / l, tq)

    @pl.when(t0 >= WINDOW)
    def _():
        run(b_ref[...])

    @pl.when(t0 < WINDOW)
    def _():
        row = lax.broadcasted_iota(jnp.int32, (tq, span), 0)
        col = lax.broadcasted_iota(jnp.int32, (tq, span), 1)
        run(jnp.where(col <= t0 + row, 0.0, NEG_INF))


def _nsa_cmp_win_kernel(q_ref, kc_ref, vc_ref, ov_ref, kw_ref, vw_ref, band_ref, ocmp_ref, m_ref, owin_ref, imp_ref,
                        *, tq, n_sel, top_n):
    _nsa_cmp_kernel(q_ref, kc_ref, vc_ref, ov_ref, ocmp_ref, m_ref, imp_ref, tq=tq, n_sel=n_sel, top_n=top_n)
    _nsa_win_kernel(q_ref, kw_ref, vw_ref, band_ref, owin_ref, tq=tq)


def nsa_cmp_select_window(p1, p2, kc, vc, ov_t, T):
    B = p1.shape[0]
    tq = NSA_QBLOCK
    ncp = kc.shape[1]
    nsp = ov_t.shape[0]
    n_sel = T // SEL_LEN
    span = WINDOW + tq
    r = np.arange(tq)[:, None]
    c = np.arange(span)[None, :]
    band = jnp.asarray(np.where((c > r) & (c <= r + WINDOW), 0.0, NEG_INF), F32)
    out_blk = pl.BlockSpec((None, tq, NSA_OUT), lambda b, i: (b, i, 0))
    return pl.pallas_call(
        functools.partial(_nsa_cmp_win_kernel, tq=tq, n_sel=n_sel, top_n=min(SEL_TOPN, n_sel)),
        grid=(B, T // tq),
        in_specs=[pl.BlockSpec((None, tq, NSA_HEADS * HEAD_DIM), lambda b, i: (b, i, 0)),
                  pl.BlockSpec((None, ncp, LANES), lambda b, i: (b, 0, 0)),
                  pl.BlockSpec((None, ncp, LANES), lambda b, i: (b, 0, 0)),
                  pl.BlockSpec((nsp, ncp), lambda b, i: (0, 0)),
                  pl.BlockSpec((None, T, LANES), lambda b, i: (b, 0, P1_NKW // LANES)),
                  pl.BlockSpec((None, T, LANES), lambda b, i: (b, 0, P2_NVW // LANES)),
                  pl.BlockSpec((tq, span), lambda b, i: (0, 0))],
        out_specs=[out_blk, pl.BlockSpec((None, NSA_GROUPS, tq, nsp), lambda b, i: (b, 0, i, 0)), out_blk],
        out_shape=[jax.ShapeDtypeStruct((B, T, NSA_OUT), BF16),
                   jax.ShapeDtypeStruct((B, NSA_GROUPS, T, nsp), BF16),
                   jax.ShapeDtypeStruct((B, T, NSA_OUT), BF16)],
        scratch_shapes=[pltpu.VMEM((NSA_GROUPS, nsp, tq), F32)],
        compiler_params=_cparams("parallel", "parallel"),
        name="nsa_cmp_select_window",
    )(p1, kc, vc, ov_t, p1, p2, band)


def _retention_kernel(q_ref, k_ref, v_ref, g_ref, din_ref, qd_ref, kd_ref, cd_ref, o_ref, st_ref):
    @pl.when(pl.program_id(0) == 0)
    def _():
        st_ref[...] = jnp.zeros_like(st_ref)

    B = q_ref.shape[0]
    C = RET_CHUNK
    half = lax.broadcasted_iota(jnp.int32, (C, LANES), 1) // HEAD_DIM
    for b in range(B):
        for h in range(RET_HEADS):
            lanes = slice(h * LANES, (h + 1) * LANES)
            pair = slice((h // 2) * LANES, (h // 2 + 1) * LANES)
            st = st_ref[b, h]
            for sub in range(q_ref.shape[1] // C):
                rows = slice(sub * C, (sub + 1) * C)
                qh = jnp.where(half == h % 2, q_ref[b, rows, pair], 0.0).astype(BF16)
                kp = k_ref[b, rows, pair]
                vh = v_ref[b, rows, lanes]
                inner = _dot_nt(qh, kp) * din_ref[h]
                o = _dot(inner.astype(BF16), vh) + _dot(qh, st.astype(BF16)) * qd_ref[h]
                kd = (kp.astype(F32) * kd_ref[h]).astype(BF16)
                st = st * cd_ref[h, 0:1, :] + _dot_tn(kd, vh)
                mu = jnp.mean(o, axis=-1, keepdims=True)
                d = o - mu
                var = jnp.mean(d * d, axis=-1, keepdims=True)
                on = d * lax.rsqrt(var + NORM_EPS)
                gh = g_ref[b, rows, lanes].astype(F32)
                o_ref[b, rows, lanes] = (gh * _sigmoid(gh) * on).astype(o_ref.dtype)
            st_ref[b, h] = st


def retention_consts():
    C = RET_CHUNK
    H = RET_HEADS
    log_g = jnp.log(1.0 - 2.0 ** (-5.0 - jnp.arange(H, dtype=F32)))
    n = jnp.arange(C, dtype=F32)
    diff = n[:, None] - n[None, :]
    causal = diff >= 0
    decay_in = jnp.where(causal[None], jnp.exp(jnp.where(causal, diff, 0.0)[None] * log_g[:, None, None]), 0.0)
    q_decay = jnp.exp((n[None, :] + 1.0) * log_g[:, None])
    k_decay = jnp.exp((C - 1.0 - n)[None, :] * log_g[:, None])
    chunk_decay = jnp.exp(C * log_g)
    qd = jnp.broadcast_to(q_decay[:, :, None], (H, C, LANES))
    kd = jnp.broadcast_to(k_decay[:, :, None], (H, C, LANES))
    cd = jnp.broadcast_to(chunk_decay[:, None, None], (H, 8, LANES))
    return decay_in, qd, kd, cd


RET_STEP = 4


def retention(p1, p2, consts, T):
    B = p1.shape[0]
    C = RET_CHUNK * RET_STEP
    din, qd, kd, cd = consts
    W = RET_HEADS * LANES
    full = lambda shape: pl.BlockSpec(shape, lambda c: (0,) * len(shape))
    return pl.pallas_call(
        _retention_kernel,
        grid=(T // C,),
        in_specs=[pl.BlockSpec((B, C, W // 2), lambda c: (0, c, P1_RQ // (W // 2))),
                  pl.BlockSpec((B, C, W // 2), lambda c: (0, c, P1_RK // (W // 2))),
                  pl.BlockSpec((B, C, W), lambda c: (0, c, P2_RV // W)),
                  pl.BlockSpec((B, C, W), lambda c: (0, c, P2_RG // W)),
                  full(din.shape), full(qd.shape), full(kd.shape), full(cd.shape)],
        out_specs=pl.BlockSpec((B, C, W), lambda c: (0, c, 0)),
        out_shape=jax.ShapeDtypeStruct((B, T, W), BF16),
        scratch_shapes=[pltpu.VMEM((B, RET_HEADS, LANES, LANES), F32)],
        compiler_params=_cparams("arbitrary"),
        name="retention",
    )(p1, p1, p2, p2, din, qd, kd, cd)


def _fox_cum_kernel(f_ref, b_ref, o_ref):
    x = f_ref[...] + b_ref[...]
    ls = jnp.minimum(x, 0.0) - jnp.log1p(jnp.exp(-jnp.abs(x)))
    R = x.shape[0]
    ki = lax.broadcasted_iota(jnp.int32, (LANES, LANES), 0)
    ji = lax.broadcasted_iota(jnp.int32, (LANES, LANES), 1)
    upper = jnp.where(ki <= ji, 1.0, 0.0).astype(BF16)
    hi, mid, lo = _split3(ls)
    rowcum = _dot(hi, upper) + _dot(mid, upper) + _dot(lo, upper)
    tot = jnp.broadcast_to(rowcum[:, LANES - 1:LANES], (R, LANES))
    ri = lax.broadcasted_iota(jnp.int32, (R, R), 0)
    ci = lax.broadcasted_iota(jnp.int32, (R, R), 1)
    lower = jnp.where(ci < ri, 1.0, 0.0).astype(BF16)
    hi, mid, lo = _split3(tot)
    offs = _dot(lower, hi) + _dot(lower, mid) + _dot(lower, lo)
    o_ref[...] = (rowcum + offs) * LOG2E


def fox_cum(f_logit, bias):
    B, H, R, _ = f_logit.shape
    return pl.pallas_call(
        _fox_cum_kernel,
        grid=(B, H),
        in_specs=[pl.BlockSpec((None, None, R, LANES), lambda b, h: (b, h, 0, 0)),
                  pl.BlockSpec((None, 1, LANES), lambda b, h: (h, 0, 0))],
        out_specs=pl.BlockSpec((None, None, R, LANES), lambda b, h: (b, h, 0, 0)),
        out_shape=jax.ShapeDtypeStruct((B, H, R, LANES), F32),
        compiler_params=_cparams("parallel", "parallel"),
        name="fox_cum",
    )(f_logit, bias)


FOX_BIAS_LANES = 3


def _fox_kernel(q_ref, k_ref, v_ref, c_ref, o_ref, ka_ref, va_ref, *, tq):
    i = pl.program_id(2)
    tk = tq
    T = k_ref.shape[0]
    chunk = 512

    @pl.when(i == 0)
    def _():
        lane = lax.broadcasted_iota(jnp.int32, (chunk, LANES), 1)
        ri = lax.broadcasted_iota(jnp.int32, (16, LANES), 0)
        ci = lax.broadcasted_iota(jnp.int32, (16, LANES), 1)
        place = jnp.where((ci == ri + HEAD_DIM) & (ri < FOX_BIAS_LANES), 1.0, 0.0).astype(BF16)

        def build(c, _):
            c0 = pl.multiple_of(c * chunk, chunk)
            kp = k_ref[pl.ds(c0, chunk), :].astype(F32)
            vp = v_ref[pl.ds(c0, chunk), :].astype(F32)
            for hh in range(2):
                hi, mid, lo = _split3(-c_ref[hh, :, pl.ds(c0, chunk)])
                terms = jnp.concatenate([hi, mid, lo, jnp.zeros((13, chunk), BF16)], axis=0)
                bias = _dot_tn(terms, place)
                kh = kp if hh == 0 else pltpu.roll(kp, HEAD_DIM, 1)
                vh = vp if hh == 0 else pltpu.roll(vp, HEAD_DIM, 1)
                ka_ref[hh, pl.ds(c0, chunk), :] = jnp.where(lane < HEAD_DIM, kh, bias).astype(BF16)
                va_ref[hh, pl.ds(c0, chunk), :] = jnp.where(lane < HEAD_DIM, vh, 1.0).astype(BF16)
            return 0

        lax.fori_loop(0, T // chunk, build, 0)

    lane = lax.broadcasted_iota(jnp.int32, (tq, LANES), 1)
    ones_lanes = (lane >= HEAD_DIM) & (lane < HEAD_DIM + FOX_BIAS_LANES)
    qp = q_ref[...].astype(F32) * (HEAD_DIM ** -0.5 * LOG2E)
    qs = [jnp.where(lane < HEAD_DIM, qh, jnp.where(ones_lanes, 1.0, 0.0)).astype(BF16)
          for qh in (qp, pltpu.roll(qp, HEAD_DIM, 1))]

    def update(hh, m, acc, q, start, size, mask=None):
        s = _dot_nt(q, ka_ref[hh, pl.ds(start, size), :])
        if mask is not None:
            s = jnp.where(mask, s, NEG_INF)
        m_new = jnp.maximum(m, jnp.max(s, axis=-1, keepdims=True))
        p = jnp.exp2(s - m_new)
        return m_new, jnp.exp2(m - m_new) * acc + _dot(p.astype(BF16), va_ref[hh, pl.ds(start, size), :])

    def step(j, carry):
        start = pl.multiple_of(j * tk, tk)
        return tuple(update(hh, *carry[hh], qs[hh], start, tk) for hh in range(2))

    one = (jnp.full((tq, 1), NEG_INF, F32), jnp.zeros((tq, LANES), F32))
    carry = lax.fori_loop(0, i, step, (one, one))

    half = tq // 2
    start = pl.multiple_of(i * tk, tk)
    row = lax.broadcasted_iota(jnp.int32, (tq, half), 0)
    col = lax.broadcasted_iota(jnp.int32, (tq, half), 1)
    accs = []
    for hh in range(2):
        m, acc = update(hh, *carry[hh], qs[hh], start, half, col <= row)
        _, low = update(hh, m[half:], acc[half:], qs[hh][half:], start + half, half, (col <= row)[:half])
        accs.append(jnp.concatenate([acc[:half], low], axis=0))
    acc0, acc1 = accs
    o0 = acc0 / acc0[:, HEAD_DIM:HEAD_DIM + 1]
    o1 = acc1 / acc1[:, HEAD_DIM:HEAD_DIM + 1]
    o_ref[...] = jnp.where(lane < HEAD_DIM, o0, pltpu.roll(o1, HEAD_DIM, 1)).astype(o_ref.dtype)


def fox_attention(p2, cum, T, *, tq=FOX_TQ):
    B = p2.shape[0]
    HP = FOX_HEADS // 2
    return pl.pallas_call(
        functools.partial(_fox_kernel, tq=tq),
        grid=(B, HP, T // tq),
        in_specs=[pl.BlockSpec((None, tq, LANES), lambda b, h, i: (b, i, P2_FQ // LANES + h)),
                  pl.BlockSpec((None, T, LANES), lambda b, h, i: (b, 0, P2_FK // LANES + h)),
                  pl.BlockSpec((None, T, LANES), lambda b, h, i: (b, 0, P2_FV // LANES + h)),
                  pl.BlockSpec((None, None, 2, 1, T), lambda b, h, i: (b, h, 0, 0, 0))],
        out_specs=pl.BlockSpec((None, tq, LANES), lambda b, h, i: (b, i, h)),
        out_shape=jax.ShapeDtypeStruct((B, T, FOX_HEADS * HEAD_DIM), BF16),
        scratch_shapes=[pltpu.VMEM((2, T, LANES), BF16), pltpu.VMEM((2, T, LANES), BF16)],
        compiler_params=_cparams("parallel", "parallel", "arbitrary"),
        name="fox_attention",
    )(p2, p2, p2, cum)


def _readout_kernel(ocmp_ref, osel_ref, owin_ref, small_ref, oret_ref, ofox_ref, mg_ref, x_ref, g1_ref,
                    ex_ref, wn_ref, wr_ref, wf_ref, wo_ref, o_ref):
    W = NSA_OUT
    gs = _sigmoid(small_ref[...].astype(F32)).astype(BF16)
    ge = _dot(gs, ex_ref[...])
    onsa = (ge[:, :W] * ocmp_ref[...].astype(F32) + ge[:, W:2 * W] * osel_ref[...].astype(F32)
            + ge[:, 2 * W:] * owin_ref[...].astype(F32))
    D = D_MODEL
    merged = (_sigmoid(mg_ref[:, :D].astype(F32)) * _dot(onsa.astype(BF16), wn_ref[...])
              + _sigmoid(mg_ref[:, D:2 * D].astype(F32)) * _dot(oret_ref[...], wr_ref[...])
              + _sigmoid(mg_ref[:, 2 * D:].astype(F32)) * _dot(ofox_ref[...], wf_ref[...]))
    y = _dot(merged.astype(BF16), wo_ref[...])
    o_ref[...] = x_ref[...] + g1_ref[...] * y


def readout(o_cmp, o_sel, o_win, p2, o_ret, o_fox, x, mod_l, ex, wn, wr, wf, wo, l, T, *, tm=512):
    M, D = x.shape
    per_b = T // tm
    W = NSA_OUT
    row = lambda width, col=0: pl.BlockSpec((tm, width), lambda i: (i, col))
    full = lambda a: pl.BlockSpec(a.shape, lambda i: (0,) * a.ndim)
    return pl.pallas_call(
        _readout_kernel,
        grid=(M // tm,),
        in_specs=[row(W), row(W), row(W), row(LANES, P2_SMALL // LANES), row(512), row(512),
                  row(3 * D, 0), row(D),
                  pl.BlockSpec((None, None, 1, D), lambda i: (i // per_b, 2, 0, 0)),
                  full(ex), _layer_spec(wn, l), _layer_spec(wr, l), _layer_spec(wf, l), _layer_spec(wo, l)],
        out_specs=row(D),
        out_shape=jax.ShapeDtypeStruct((M, D), F32),
        compiler_params=_cparams("parallel"),
        name="mixer_readout",
    )(o_cmp, o_sel, o_win, p2, o_ret, o_fox, p2, x, mod_l, ex, wn, wr, wf, wo)


def nsa_gate_expand():
    ex = np.zeros((LANES, 3 * NSA_OUT), np.float32)
    for br in range(3):
        for h in range(NSA_HEADS):
            c0 = br * NSA_OUT + h * HEAD_DIM
            ex[br * NSA_HEADS + h, c0:c0 + HEAD_DIM] = 1.0
    return jnp.asarray(ex, BF16)


FFN_CHUNK = 512


def _ffn_kernel(x_ref, nw_ref, sc_ref, sh_ref, g2_ref, w1_ref, w3_ref, w2_ref, o_ref):
    x = x_ref[...]
    h = _norm_mod(x, nw_ref[...], sc_ref[...], sh_ref[...]).astype(BF16)
    F = w1_ref.shape[1]
    y = None
    for c0 in range(0, F, FFN_CHUNK):
        cols = slice(c0, min(c0 + FFN_CHUNK, F))
        u = _dot(h, w1_ref[:, cols])
        v = _dot(h, w3_ref[:, cols])
        part = _dot((u * _sigmoid(u) * v).astype(BF16), w2_ref[cols, :])
        y = part if y is None else y + part
    o_ref[...] = x + g2_ref[...] * y


def ffn(x, mod_l, nw, w1, w3, w2, T, *, tm=512):
    M, D = x.shape
    F = w1.shape[1]
    per_b = T // tm
    modspec = lambda k: pl.BlockSpec((None, None, 1, D), lambda i: (i // per_b, k, 0, 0))
    full = lambda a: pl.BlockSpec(a.shape, lambda i: (0,) * a.ndim)
    return pl.pallas_call(
        _ffn_kernel,
        grid=(M // tm,),
        in_specs=[pl.BlockSpec((tm, D), lambda i: (i, 0)),
                  pl.BlockSpec((1, D), lambda i: (0, 0)),
                  modspec(4), modspec(3), modspec(5), full(w1), full(w3), full(w2)],
        out_specs=pl.BlockSpec((tm, D), lambda i: (i, 0)),
        out_shape=jax.ShapeDtypeStruct((M, D), F32),
        compiler_params=_cparams("parallel"),
        name="ffn_dense",
    )(x, nw, mod_l, mod_l, mod_l, w1, w3, w2)


MOE_TC = 512
MOE_TS = 512


def _router_kernel(x_ref, nw_ref, sc_ref, sh_ref, wh_ref, wl_ref, h_ref, gate_ref, rank_ref, cnt_ref, carry_ref):
    @pl.when(pl.program_id(0) == 0)
    def _():
        carry_ref[...] = jnp.zeros_like(carry_ref)

    h = _norm_mod(x_ref[...], nw_ref[...], sc_ref[...], sh_ref[...])
    hh = h.astype(BF16)
    h_ref[...] = _pack_bf16_pairs(hh.astype(F32))
    hl = (h - hh.astype(F32)).astype(BF16)
    logits = _dot(hh, wh_ref[...]) + (_dot(hl, wh_ref[...]) + _dot(hh, wl_ref[...]))
    tm = logits.shape[0]
    lane = lax.broadcasted_iota(jnp.int32, logits.shape, 1)
    logits = jnp.where(lane < N_EXPERTS, logits, REMOVED)
    lane_f = lane.astype(F32)
    v1 = jnp.max(logits, axis=-1, keepdims=True)
    i1 = jnp.min(jnp.where(logits == v1, lane_f, float(LANES)), axis=-1, keepdims=True)
    rest = jnp.where(lane_f == i1, REMOVED, logits)
    v2 = jnp.max(rest, axis=-1, keepdims=True)
    i2 = jnp.min(jnp.where(rest == v2, lane_f, float(LANES)), axis=-1, keepdims=True)
    e2 = jnp.exp(v2 - v1)
    w1 = 1.0 / (1.0 + e2)
    w2 = e2 / (1.0 + e2)
    gate_ref[...] = jnp.where(lane_f == i1, w1, jnp.where(lane_f == i2, w2, 0.0))

    sel = jnp.where((lane_f == i1) | (lane_f == i2), 1.0, 0.0)
    ri = lax.broadcasted_iota(jnp.int32, (tm, tm), 0)
    ci = lax.broadcasted_iota(jnp.int32, (tm, tm), 1)
    before = jnp.where(ci < ri, 1.0, 0.0).astype(BF16)
    rank = _dot(before, sel.astype(BF16)) + carry_ref[0:1, :]
    rank_ref[...] = jnp.where(sel > 0.0, rank, -1.0)
    carry_ref[...] = carry_ref[...] + jnp.sum(sel, axis=0, keepdims=True)
    cnt_ref[...] = carry_ref[...]


def router(x, mod_l, nw, w_router, T):
    M, D = x.shape
    tm = MOE_TC
    per_b = T // tm
    wp = jnp.zeros((D, LANES), F32).at[:, :N_EXPERTS].set(w_router)
    wh = wp.astype(BF16)
    wl = (wp - wh.astype(F32)).astype(BF16)
    return pl.pallas_call(
        _router_kernel,
        grid=(M // tm,),
        in_specs=[pl.BlockSpec((tm, D), lambda i: (i, 0)),
                  pl.BlockSpec((1, D), lambda i: (0, 0))]
        + _mod_specs(T, tm, 4, 3, 1)
        + [pl.BlockSpec((D, LANES), lambda i: (0, 0)),
           pl.BlockSpec((D, LANES), lambda i: (0, 0))],
        out_specs=[pl.BlockSpec((tm, D // 2), lambda i: (i, 0)),
                   pl.BlockSpec((tm, LANES), lambda i: (i, 0)),
                   pl.BlockSpec((tm, LANES), lambda i: (i, 0)),
                   pl.BlockSpec((8, LANES), lambda i: (0, 0))],
        out_shape=[jax.ShapeDtypeStruct((M, D // 2), jnp.uint32),
                   jax.ShapeDtypeStruct((M, LANES), F32),
                   jax.ShapeDtypeStruct((M, LANES), F32),
                   jax.ShapeDtypeStruct((8, LANES), F32)],
        scratch_shapes=[pltpu.VMEM((8, LANES), F32)],
        compiler_params=_cparams("arbitrary"),
        name="moe_router",
    )(x, nw, mod_l, mod_l, wh, wl)


def _count_le(sorted_vals, x):
    return jnp.sum(sorted_vals[None, :] <= x[:, None], axis=1, dtype=jnp.int32)


def _moe_up_kernel(e_r, total, x_ref, w1_ref, w3_ref, o_ref, w1b_ref, w3b_ref):
    r = pl.program_id(1)
    live = r < total[0]

    @pl.when(live & ((r == 0) | (e_r[r] != e_r[jnp.maximum(r - 1, 0)])))
    def _():
        w1b_ref[...] = w1_ref[...].astype(BF16)
        w3b_ref[...] = w3_ref[...].astype(BF16)

    @pl.when(live)
    def _():
        x = _unpack_bf16_pairs(x_ref[...]).astype(BF16)
        u = _dot(x, w1b_ref[...])
        v = _dot(x, w3b_ref[...])
        o_ref[...] = (u * _sigmoid(u) * v).astype(o_ref.dtype)


def moe_up(xs, w1, w3, tiles, rt, *, tf=1792):
    R = xs.shape[0]
    D = w1.shape[1]
    ts = MOE_TS
    F = w1.shape[-1]
    live = lambda r, total: jnp.minimum(r, total[0] - 1)
    return pl.pallas_call(
        _moe_up_kernel,
        grid_spec=pltpu.PrefetchScalarGridSpec(
            num_scalar_prefetch=2,
            grid=(F // tf, rt),
            in_specs=[pl.BlockSpec((ts, D // 2), lambda n, r, e, total: (live(r, total), 0)),
                      pl.BlockSpec((None, D, tf), lambda n, r, e, total: (e[live(r, total)], 0, n)),
                      pl.BlockSpec((None, D, tf), lambda n, r, e, total: (e[live(r, total)], 0, n))],
            out_specs=pl.BlockSpec((ts, tf), lambda n, r, e, total: (r, n)),
            scratch_shapes=[pltpu.VMEM((D, tf), BF16), pltpu.VMEM((D, tf), BF16)],
        ),
        out_shape=jax.ShapeDtypeStruct((R, F), BF16),
        compiler_params=_cparams("arbitrary", "arbitrary"),
        name="moe_up",
    )(tiles["e"], tiles["total"], xs, w1, w3)


def _moe_down_kernel(e_r, total, a_ref, w2_ref, o_ref, w2b_ref):
    r = pl.program_id(0)
    live = r < total[0]

    @pl.when(live & ((r == 0) | (e_r[r] != e_r[jnp.maximum(r - 1, 0)])))
    def _():
        w2b_ref[...] = w2_ref[...].astype(BF16)

    @pl.when(live)
    def _():
        o_ref[...] = _pack_bf16_pairs(_dot(a_ref[...], w2b_ref[...]))


def moe_down(a, w2, tiles, rt):
    R, F = a.shape
    ts = MOE_TS
    D = w2.shape[-1]
    live = lambda r, total: jnp.minimum(r, total[0] - 1)
    return pl.pallas_call(
        _moe_down_kernel,
        grid_spec=pltpu.PrefetchScalarGridSpec(
            num_scalar_prefetch=2,
            grid=(rt,),
            in_specs=[pl.BlockSpec((ts, F), lambda r, e, total: (live(r, total), 0)),
                      pl.BlockSpec((None, F, D), lambda r, e, total: (e[live(r, total)], 0, 0))],
            out_specs=pl.BlockSpec((ts, D // 2), lambda r, e, total: (r, 0)),
            scratch_shapes=[pltpu.VMEM((F, D), BF16)],
        ),
        out_shape=jax.ShapeDtypeStruct((R, D // 2), jnp.uint32),
        compiler_params=_cparams("arbitrary"),
        name="moe_down",
    )(tiles["e"], tiles["total"], a, w2)


SC_WINDOW = 64


def _sc_mesh():
    return plsc.VectorSubcoreMesh(core_axis_name="core", subcore_axis_name="subcore")


def sc_scatter_rows2(x, idx_a, idx_b, n_out):
    n, d = x.shape
    steps = n // SC_WINDOW

    @pl.kernel(out_type=jax.ShapeDtypeStruct((n_out, d), x.dtype), mesh=_sc_mesh(), scratch_types=[])
    def kern(x_hbm, ia_hbm, ib_hbm, o_hbm):
        def body(x_vmem, ia_vmem, ib_vmem):
            pltpu.sync_copy(x_vmem, o_hbm.at[ia_vmem.at[0]])
            pltpu.sync_copy(x_vmem, o_hbm.at[ib_vmem.at[0]])

        pltpu.emit_pipeline(
            body,
            grid=(steps,),
            in_specs=[pl.BlockSpec((SC_WINDOW, d), index_map=lambda i: (i, 0)),
                      pl.BlockSpec((1, SC_WINDOW), index_map=lambda i: (i, 0)),
                      pl.BlockSpec((1, SC_WINDOW), index_map=lambda i: (i, 0))],
            out_specs=[],
            core_axis_name=("core", "subcore"),
            dimension_semantics=(pltpu.PARALLEL,),
        )(x_hbm, ia_hbm, ib_hbm)

    return kern(x, idx_a.reshape(steps, SC_WINDOW), idx_b.reshape(steps, SC_WINDOW))


def sc_gather_rows(x, idx):
    n = idx.shape[0]
    d = x.shape[1]
    steps = n // SC_WINDOW

    @pl.kernel(out_type=jax.ShapeDtypeStruct((n, d), x.dtype), mesh=_sc_mesh(), scratch_types=[])
    def kern(x_hbm, i_hbm, o_hbm):
        def body(i_vmem, o_vmem):
            pltpu.sync_copy(x_hbm.at[i_vmem.at[0]], o_vmem)

        pltpu.emit_pipeline(
            body,
            grid=(steps,),
            in_specs=[pl.BlockSpec((1, SC_WINDOW), index_map=lambda i: (i, 0))],
            out_specs=[pl.BlockSpec((SC_WINDOW, d), index_map=lambda i: (i, 0))],
            core_axis_name=("core", "subcore"),
            dimension_semantics=(pltpu.PARALLEL,),
        )(i_hbm, o_hbm)

    return kern(x, idx.reshape(steps, SC_WINDOW))


def _moe_finish_kernel(x_ref, g2_ref, ya_ref, yb_ref, gate_ref, rank_ref, nw_ref, o_ref, *, normalize):
    gate = gate_ref[...]
    chosen = rank_ref[...] >= 0.0
    lane = lax.broadcasted_iota(jnp.int32, gate.shape, 1).astype(F32)
    first = jnp.min(jnp.where(chosen, lane, float(LANES)), axis=-1, keepdims=True)
    last = jnp.max(jnp.where(chosen, lane, -1.0), axis=-1, keepdims=True)
    wa = jnp.sum(jnp.where(lane == first, gate, 0.0), axis=-1, keepdims=True)
    wb = jnp.sum(jnp.where(lane == last, gate, 0.0), axis=-1, keepdims=True)
    x = x_ref[...] + g2_ref[...] * (wa * _unpack_bf16_pairs(ya_ref[...]) + wb * _unpack_bf16_pairs(yb_ref[...]))
    if normalize:
        ms = jnp.mean(x * x, axis=-1, keepdims=True)
        x = x * lax.rsqrt(ms + NORM_EPS) * nw_ref[...]
    o_ref[...] = x


def moe_finish(x, mod_l, y2, gate, rank, norm_w, T, *, tm=512):
    M, D = x.shape
    per_b = T // tm
    normalize = norm_w is not None
    if norm_w is None:
        norm_w = jnp.ones((1, D), F32)
    return pl.pallas_call(
        functools.partial(_moe_finish_kernel, normalize=normalize),
        grid=(M // tm,),
        in_specs=[pl.BlockSpec((tm, D), lambda i: (i, 0)),
                  pl.BlockSpec((None, None, 1, D), lambda i: (i // per_b, 5, 0, 0)),
                  pl.BlockSpec((None, tm, D // 2), lambda i: (0, i, 0)),
                  pl.BlockSpec((None, tm, D // 2), lambda i: (1, i, 0)),
                  pl.BlockSpec((tm, LANES), lambda i: (i, 0)),
                  pl.BlockSpec((tm, LANES), lambda i: (i, 0)),
                  pl.BlockSpec((1, D), lambda i: (0, 0))],
        out_specs=pl.BlockSpec((tm, D), lambda i: (i, 0)),
        out_shape=jax.ShapeDtypeStruct((M, D), F32),
        compiler_params=_cparams("parallel"),
        name="moe_finish",
    )(x, mod_l, y2, y2, gate, rank, norm_w)


def moe_ffn(x, mod_l, nw, w_router, w1, w3, w2, T, norm_w=None):
    M = x.shape[0]
    ts = MOE_TS
    rt = (2 * M) // ts + N_EXPERTS
    h, gate, rank, cnt = router(x, mod_l, nw, w_router, T)
    i32 = jnp.int32
    counts = cnt[0, :N_EXPERTS].astype(i32)
    ntile = (counts + ts - 1) // ts
    tile_end = jnp.cumsum(ntile)
    row_off = (tile_end - ntile) * ts
    e_r = jnp.minimum(_count_le(tile_end, jnp.arange(rt, dtype=i32)), N_EXPERTS - 1)
    tiles = dict(e=e_r, total=tile_end[-1].reshape(1).astype(i32))
    rk = rank[:, :N_EXPERTS].astype(i32)
    pos = row_off[None, :] + rk
    pos_a = jnp.min(jnp.where(rk >= 0, pos, rt * ts), axis=1)
    pos_b = jnp.max(jnp.where(rk >= 0, pos, -1), axis=1)

    xs = sc_scatter_rows2(h, pos_a, pos_b, rt * ts)
    a = moe_up(xs, w1, w3, tiles, rt)
    y = moe_down(a, w2, tiles, rt)
    y2 = sc_gather_rows(y, jnp.concatenate([pos_a, pos_b])).reshape(2, M, -1)
    return moe_finish(x, mod_l, y2, gate, rank, norm_w, T)


def _final_norm_kernel(x_ref, w_ref, o_ref):
    x = x_ref[...]
    ms = jnp.mean(x * x, axis=-1, keepdims=True)
    o_ref[...] = x * lax.rsqrt(ms + NORM_EPS) * w_ref[...]


def final_norm(x, w, *, tm=1024):
    M, D = x.shape
    return pl.pallas_call(
        _final_norm_kernel,
        grid=(M // tm,),
        in_specs=[pl.BlockSpec((tm, D), lambda i: (i, 0)), pl.BlockSpec((1, D), lambda i: (0, 0))],
        out_specs=pl.BlockSpec((tm, D), lambda i: (i, 0)),
        out_shape=jax.ShapeDtypeStruct((M, D), F32),
        compiler_params=_cparams("parallel"),
        name="final_norm",
    )(x, w)


def nsa_constants(T):
    n_sel = T // SEL_LEN
    nsp = max(LANES, n_sel)
    ncp = T // CMP_STRIDE
    cmp_start = np.arange(ncp) * CMP_STRIDE
    sel_start = np.arange(nsp) * SEL_LEN
    ov = ((cmp_start[:, None] < sel_start[None, :] + SEL_LEN)
          & (cmp_start[:, None] + CMP_LEN > sel_start[None, :]))
    ov[(T - CMP_LEN) // CMP_STRIDE + 1:] = False
    ov[:, n_sel:] = False
    et_mat = ((np.arange(T)[:, None] // SEL_LEN) == np.arange(nsp)[None, :]) * SEL_BONUS
    return jnp.asarray(ov.T, BF16), jnp.asarray(et_mat, BF16)


def token_mixing(x, mod_l, lw, consts, B, T):
    M = B * T
    cos_t, sin_t, ov_t, e_mat, ret_consts, ex = consts
    l = lw["layer"]
    p1 = proj_rope(x, mod_l, lw["norm_mix"], lw["w1"], l, cos_t, sin_t, p1_scales(), T).reshape(B, T, P1_COLS)
    p2 = proj_plain(x, mod_l, lw["norm_mix"], lw["w2"], l, T).reshape(B, T, P2_COLS)

    def group_rows(a):
        return a.reshape(B, T, NSA_GROUPS, HEAD_DIM).transpose(0, 2, 1, 3).reshape(
            B, NSA_GROUPS, T // CMP_STRIDE, CMP_STRIDE * HEAD_DIM)

    xr = jnp.stack([group_rows(p1[:, :, P1_NKC:P1_NKC + LANES]), group_rows(p2[:, :, P2_NVC:P2_NVC + LANES])])
    cmp_out = compress(xr, lw["cmp_pe"], lw["cmp_w1"], lw["cmp_w2"])
    cmp_out = cmp_out.transpose(0, 1, 3, 2, 4).reshape(2, B, T // CMP_STRIDE, LANES)
    o_cmp, sel, o_win = nsa_cmp_select_window(p1, p2, cmp_out[0], cmp_out[1], ov_t, T)
    o_sel = nsa_selected(p1, nsa_value_augment(p2[:, :, P2_NVS:P2_NVS + LANES]), sel, e_mat, T)

    o_ret = retention(p1, p2, ret_consts, T)

    ff = p2[:, :, P2_SMALL + 3 * NSA_HEADS:P2_SMALL + 3 * NSA_HEADS + FOX_HEADS].astype(F32)
    ff = ff.transpose(0, 2, 1).reshape(B, FOX_HEADS, T // LANES, LANES)
    cum = fox_cum(ff, lw["fox_bias"]).reshape(B, FOX_HEADS // 2, 2, 1, T)
    o_fox = fox_attention(p2, cum, T)

    return readout(o_cmp.reshape(M, -1), o_sel.reshape(M, -1), o_win.reshape(M, -1), p2.reshape(M, P2_COLS),
                   o_ret.reshape(M, -1), o_fox.reshape(M, -1), x, mod_l, ex,
                   lw["wn"], lw["wr"], lw["wf"], lw["wo"], l, T)


def mixer_weights(norm_mix, w_in, cmp_k_pe, cmp_k_w1, cmp_k_w2, cmp_v_pe, cmp_v_w1, cmp_v_w2, fox_f_bias,
                  w_read_nsa, w_read_ret, w_read_fox, w_out):
    depth = w_in.shape[0]
    w1, w2 = split_w_in(w_in)
    pe = jnp.stack([cmp_k_pe.reshape(depth, 1, -1), cmp_v_pe.reshape(depth, 1, -1)], axis=1)
    pe = jnp.broadcast_to(pe, (depth, 2, 8, pe.shape[-1])).astype(BF16)
    shared = {
        "w1": w1, "w2": w2,
        "wn": w_read_nsa.astype(BF16),
        "wr": w_read_ret.astype(BF16),
        "wf": w_read_fox.astype(BF16),
        "wo": w_out.astype(BF16),
    }
    cmp_w1 = jnp.stack([cmp_k_w1, cmp_v_w1], axis=1).astype(BF16)
    cmp_w2 = jnp.stack([cmp_k_w2, cmp_v_w2], axis=1).astype(BF16)
    return [dict(shared, layer=l, norm_mix=norm_mix[l].reshape(1, -1), cmp_pe=pe[l], cmp_w1=cmp_w1[l], cmp_w2=cmp_w2[l],
                 fox_bias=jnp.broadcast_to(fox_f_bias[l][:, None, None], (FOX_HEADS, 1, LANES)))
            for l in range(depth)]


def kernel(x, c, ada_w, ada_b, norm_mix, norm_ffn, w_in, cmp_k_pe, cmp_k_w1, cmp_k_w2, cmp_v_pe, cmp_v_w1,
           cmp_v_w2, fox_f_bias, w_read_nsa, w_read_ret, w_read_fox, w_out, ffn_w1, ffn_w3, ffn_w2, router_w,
           moe_w1, moe_w3, moe_w2, final_norm_w):
    B, T, D = x.shape
    M = B * T
    depth = ada_w.shape[0]
    mod = modulation(c, ada_w, ada_b)
    cos_t, sin_t = rope_tables(T)
    ov_t, e_mat = nsa_constants(T)
    consts = (cos_t, sin_t, ov_t, e_mat, retention_consts(), nsa_gate_expand())
    xs = x.reshape(M, D)
    lws = mixer_weights(norm_mix, w_in, cmp_k_pe, cmp_k_w1, cmp_k_w2, cmp_v_pe, cmp_v_w1, cmp_v_w2,
                        fox_f_bias, w_read_nsa, w_read_ret, w_read_fox, w_out)
    for l in range(depth):
        xs = token_mixing(xs, mod[l], lws[l], consts, B, T)
        nf = norm_ffn[l].reshape(1, D)
        if l % 2 == 0:
            k = l // 2
            xs = ffn(xs, mod[l], nf, ffn_w1[k].astype(BF16), ffn_w3[k].astype(BF16), ffn_w2[k].astype(BF16), T)
        else:
            k = l // 2
            fuse = final_norm_w.reshape(1, D) if l == depth - 1 else None
            xs = moe_ffn(xs, mod[l], nf, router_w[k], moe_w1[k], moe_w3[k], moe_w2[k], T, fuse)
    if depth % 2 == 1:
        xs = final_norm(xs, final_norm_w.reshape(1, D))
    return xs.reshape(B, T, D)
```

```python
import functools

import jax
import jax.numpy as jnp
import numpy as np
from jax import lax
from jax.experimental import pallas as pl
from jax.experimental.pallas import tpu as pltpu
from jax.experimental.pallas import tpu_sc as plsc

F32 = jnp.float32
BF16 = jnp.bfloat16

D_MODEL = 1024
HEAD_DIM = 64
ROPE_THETA = 10000.0
NORM_EPS = 1e-6
NEG_INF = -1e30
REMOVED = -3e38

NSA_HEADS = 8
NSA_GROUPS = 2
NSA_HPG = NSA_HEADS // NSA_GROUPS
CMP_LEN = 32
CMP_STRIDE = 16
SEL_LEN = 64
SEL_TOPN = 16
WINDOW = 512
NSA_QBLOCK = 256

RET_HEADS = 4
RET_QK_DIM = 64
RET_CHUNK = 128

FOX_HEADS = 8
FOX_TQ = 1024
LOG2E = 1.4426950408889634

N_EXPERTS = 8

LANES = 128
VMEM_LIMIT = 56 * 1024 * 1024

P1_NQ = 0
P1_RQ = 512
P1_RK = 768
P1_NKC = 1024
P1_NKS = 1152
P1_NKW = 1280
P1_COLS = 1408
P2_MG = 0
P2_RV = 3072
P2_RG = 3584
P2_FQ = 4096
P2_FK = 4608
P2_FV = 5120
P2_NVC = 5632
P2_NVS = 5760
P2_NVW = 5888
P2_SMALL = 6016
P2_COLS = 6144
NSA_OUT = NSA_HEADS * HEAD_DIM


def _layer_spec(w, l):
    zeros = (0,) * (w.ndim - 1)
    return pl.BlockSpec((None,) + w.shape[1:], lambda *_: (l,) + zeros)


def _cparams(*sem):
    return pltpu.CompilerParams(dimension_semantics=tuple(sem), vmem_limit_bytes=VMEM_LIMIT)


def _sigmoid(x):
    return 1.0 / (1.0 + jnp.exp(-x))


def _dot(a, b):
    return jnp.dot(a, b, preferred_element_type=F32)


def _dot_nt(a, b):
    return lax.dot_general(a, b, (((1,), (1,)), ((), ())), preferred_element_type=F32)


def _dot_tn(a, b):
    return lax.dot_general(a, b, (((0,), (0,)), ((), ())), preferred_element_type=F32)


def _split3(x):
    hi = x.astype(BF16)
    r1 = x - hi.astype(F32)
    mid = r1.astype(BF16)
    lo = (r1 - mid.astype(F32)).astype(BF16)
    return hi, mid, lo


def _pack_bf16_pairs(x):
    c = x.shape[1] // 2
    lo = pltpu.bitcast(x[:, :c].astype(BF16).astype(F32), jnp.uint32) >> 16
    hi = pltpu.bitcast(x[:, c:].astype(BF16).astype(F32), jnp.uint32) & jnp.uint32(0xFFFF0000)
    return hi | lo


def _unpack_bf16_pairs(u):
    lo = pltpu.bitcast(u << 16, F32)
    hi = pltpu.bitcast(u & jnp.uint32(0xFFFF0000), F32)
    return jnp.concatenate([lo, hi], axis=1)


def _norm_mod(x, nw, sc, sh):
    ms = jnp.mean(x * x, axis=-1, keepdims=True)
    y = x * lax.rsqrt(ms + NORM_EPS) * nw
    return y * (1.0 + sc) + sh


def _mod_kernel(c_ref, w_ref, b_ref, o_ref):
    c = c_ref[...]
    s = c * _sigmoid(c)
    o_ref[0] = _dot(s.astype(BF16), w_ref[0].astype(BF16)) + b_ref[0]


def modulation(c, ada_w, ada_b):
    B, D = c.shape
    depth = ada_w.shape[0]
    rows = 8
    c_pad = jnp.zeros((rows, D), F32).at[:B].set(c)
    out = pl.pallas_call(
        _mod_kernel,
        grid=(depth, 6),
        in_specs=[pl.BlockSpec((rows, D), lambda l, j: (0, 0)),
                  pl.BlockSpec((1, D, D), lambda l, j: (l, 0, j)),
                  pl.BlockSpec((1, 1, D), lambda l, j: (l, 0, j))],
        out_specs=pl.BlockSpec((1, rows, D), lambda l, j: (l, 0, j)),
        out_shape=jax.ShapeDtypeStruct((depth, rows, 6 * D), F32),
        compiler_params=_cparams("parallel", "parallel"),
        name="modulation",
    )(c_pad, ada_w, ada_b.reshape(depth, 1, 6 * D))
    return out[:, :B].reshape(depth, B, 6, 1, D)


def _proj_plain_kernel(x_ref, nw_ref, sc_ref, sh_ref, w_ref, o_ref, *, tn):
    h = _norm_mod(x_ref[...], nw_ref[...], sc_ref[...], sh_ref[...]).astype(BF16)
    for n in range(w_ref.shape[0] // tn):
        cols = slice(n * tn, (n + 1) * tn)
        o_ref[:, cols] = _dot_nt(h, w_ref[cols, :]).astype(o_ref.dtype)


def _proj_rope_kernel(x_ref, nw_ref, sc_ref, sh_ref, w_ref, cos_ref, sin_ref, o_ref, *, scales):
    h = _norm_mod(x_ref[...], nw_ref[...], sc_ref[...], sh_ref[...]).astype(BF16)
    y = _dot_nt(h, w_ref[...])
    cos = cos_ref[...]
    sin = sin_ref[...]
    lane = lax.broadcasted_iota(jnp.int32, cos.shape, 1)
    first_half = (lane % HEAD_DIM) < (HEAD_DIM // 2)
    for g, scale in enumerate(scales):
        yg = y[:, g * LANES:(g + 1) * LANES]
        rot = jnp.where(first_half, pltpu.roll(yg, LANES - HEAD_DIM // 2, 1),
                        pltpu.roll(yg, HEAD_DIM // 2, 1))
        r = yg * cos + rot * sin
        if scale != 1.0:
            r = r * scale
        o_ref[:, g * LANES:(g + 1) * LANES] = r.astype(o_ref.dtype)


def _mod_specs(T, tm, sc_idx, sh_idx, nargs):
    per_b = T // tm
    if nargs == 1:
        return [pl.BlockSpec((None, None, 1, D_MODEL), lambda i: (i // per_b, sc_idx, 0, 0)),
                pl.BlockSpec((None, None, 1, D_MODEL), lambda i: (i // per_b, sh_idx, 0, 0))]
    return [pl.BlockSpec((None, None, 1, D_MODEL), lambda i, j: (i // per_b, sc_idx, 0, 0)),
            pl.BlockSpec((None, None, 1, D_MODEL), lambda i, j: (i // per_b, sh_idx, 0, 0))]


def proj_plain(x, mod_l, nw, w, l, T, *, tm=512, tn=512):
    M, D = x.shape
    N = w.shape[1]
    return pl.pallas_call(
        functools.partial(_proj_plain_kernel, tn=tn),
        grid=(M // tm,),
        in_specs=[pl.BlockSpec((tm, D), lambda i: (i, 0)),
                  pl.BlockSpec((1, D), lambda i: (0, 0))]
        + _mod_specs(T, tm, 1, 0, 1)
        + [_layer_spec(w, l)],
        out_specs=pl.BlockSpec((tm, N), lambda i: (i, 0)),
        out_shape=jax.ShapeDtypeStruct((M, N), BF16),
        compiler_params=_cparams("parallel"),
        name="proj_plain",
    )(x, nw, mod_l, mod_l, w)


def proj_rope(x, mod_l, nw, w, l, cos, sin, scales, T, *, tm=512):
    M, D = x.shape
    N = w.shape[1]
    per_b = T // tm
    return pl.pallas_call(
        functools.partial(_proj_rope_kernel, scales=scales),
        grid=(M // tm,),
        in_specs=[pl.BlockSpec((tm, D), lambda i: (i, 0)),
                  pl.BlockSpec((1, D), lambda i: (0, 0))]
        + _mod_specs(T, tm, 1, 0, 1)
        + [_layer_spec(w, l),
           pl.BlockSpec((tm, LANES), lambda i: (i % per_b, 0)),
           pl.BlockSpec((tm, LANES), lambda i: (i % per_b, 0))],
        out_specs=pl.BlockSpec((tm, N), lambda i: (i, 0)),
        out_shape=jax.ShapeDtypeStruct((M, N), BF16),
        compiler_params=_cparams("parallel"),
        name="proj_rope",
    )(x, nw, mod_l, mod_l, w, cos, sin)


def rope_tables(T):
    d = HEAD_DIM
    pos = jnp.arange(T, dtype=F32)
    inv = ROPE_THETA ** (-jnp.arange(0, d, 2, dtype=F32) / d)
    ang = pos[:, None] * inv[None, :]
    cos = jnp.cos(ang)
    sin = jnp.sin(ang)
    cos_t = jnp.concatenate([cos, cos, cos, cos], axis=-1)
    sin_t = jnp.concatenate([-sin, sin, -sin, sin], axis=-1)
    return cos_t, sin_t


def split_w_in(w_in):
    sizes = [512, 128, 128, 128, 128, 128, 128, 24, 256, 256, 512, 512, 512, 512, 512, 8, 3072]
    offs = np.cumsum([0] + sizes)
    wb = jnp.swapaxes(w_in, 1, 2).astype(BF16)
    (nq, nkc, nvc, nks, nvs, nkw, nvw, ngate, rq, rk, rv, rg, fq, fk, fv, ff, mg) = [
        wb[:, offs[i]:offs[i + 1], :] for i in range(len(sizes))]
    small = jnp.concatenate([ngate, ff, jnp.zeros((wb.shape[0], LANES - 32, wb.shape[2]), BF16)], axis=1)
    w1 = jnp.concatenate([nq, rq, rk, nkc, nks, nkw], axis=1)
    w2 = jnp.concatenate([mg, rv, rg, fq, fk, fv, nvc, nvs, nvw, small], axis=1)
    assert w1.shape[1] == P1_COLS and w2.shape[1] == P2_COLS
    return w1, w2


def p1_scales():
    s = [1.0] * (P1_COLS // LANES)
    for g in range(P1_NQ // LANES, P1_RQ // LANES):
        s[g] = HEAD_DIM ** -0.5 * LOG2E
    for g in range(P1_RK // LANES, P1_NKC // LANES):
        s[g] = RET_QK_DIM ** -0.5
    return tuple(s)


def _compress_kernel(x_ref, pe_ref, w1_ref, w2_ref, o_ref):
    r = x_ref[...]
    half = r.shape[1]
    w1 = w1_ref[...]
    a = _dot(r, w1[:half])
    b = _dot(r, w1[half:])
    pe = _dot(pe_ref[...], w1)[0:1]
    n = a.shape[0]
    hid = a + pltpu.roll(b, n - 1, 0) + pe
    hid = hid * _sigmoid(hid)
    o_ref[...] = _dot(hid.astype(BF16), w2_ref[...]).astype(o_ref.dtype)


def compress(xr, pe, w1, w2):
    _, B, G, R, W = xr.shape
    H = w1.shape[-1]
    return pl.pallas_call(
        _compress_kernel,
        grid=(2, B, G),
        in_specs=[pl.BlockSpec((None, None, None, R, W), lambda s, b, g: (s, b, g, 0, 0)),
                  pl.BlockSpec((None, 8, 2 * W), lambda s, b, g: (s, 0, 0)),
                  pl.BlockSpec((None, 2 * W, H), lambda s, b, g: (s, 0, 0)),
                  pl.BlockSpec((None, H, HEAD_DIM), lambda s, b, g: (s, 0, 0))],
        out_specs=pl.BlockSpec((None, None, None, R, HEAD_DIM), lambda s, b, g: (s, b, g, 0, 0)),
        out_shape=jax.ShapeDtypeStruct((2, B, G, R, HEAD_DIM), BF16),
        compiler_params=_cparams("parallel", "parallel", "parallel"),
        name="nsa_compress",
    )(xr, pe, w1, w2)


def _stack_heads(q_ref, g):
    tq = q_ref.shape[0]
    half = lax.broadcasted_iota(jnp.int32, (tq, LANES), 1) // HEAD_DIM
    rows = []
    for hh in range(NSA_HPG):
        h = NSA_HPG * g + hh
        x = q_ref[:, (h // 2) * LANES:(h // 2 + 1) * LANES].astype(F32)
        if h % 2 != g:
            x = pltpu.roll(x, HEAD_DIM, 1)
        rows.append(jnp.where(half == g, x, 0.0).astype(BF16))
    return jnp.concatenate(rows, axis=0)


def _store_heads(o_ref, g, o, tq):
    low = lax.broadcasted_iota(jnp.int32, (tq, LANES), 1) < HEAD_DIM
    for pair in range(NSA_HPG // 2):
        even = o[(2 * pair) * tq:(2 * pair + 1) * tq]
        odd = o[(2 * pair + 1) * tq:(2 * pair + 2) * tq]
        if g == 0:
            blk = jnp.where(low, even, pltpu.roll(odd, HEAD_DIM, 1))
        else:
            blk = jnp.where(low, pltpu.roll(even, HEAD_DIM, 1), odd)
        col = (NSA_HPG // 2 * g + pair) * LANES
        o_ref[:, col:col + LANES] = blk.astype(o_ref.dtype)


CMP_CHUNK = 128


def _nsa_cmp_kernel(q_ref, kc_ref, vc_ref, ov_ref, o_ref, m_ref, imp_ref, *, tq, n_sel, top_n):
    t0 = pl.program_id(1) * tq
    ncp = kc_ref.shape[0]
    nsp = ov_ref.shape[0]
    rows = NSA_HPG * tq

    def attend(ncols):
        kc = kc_ref[0:ncols, :]
        vc = vc_ref[0:ncols, :]
        n_idx = lax.broadcasted_iota(jnp.int32, (rows, ncols), 1)
        t_idx = t0 + lax.broadcasted_iota(jnp.int32, (rows, ncols), 0) % tq
        valid = (n_idx * CMP_STRIDE + (CMP_LEN - 1)) <= t_idx
        for g in range(NSA_GROUPS):
            q = _stack_heads(q_ref, g)
            s = jnp.where(valid, _dot_nt(q, kc), NEG_INF)
            m = jnp.max(s, axis=-1, keepdims=True)
            e = jnp.exp2(s - m)
            l = jnp.sum(e, axis=-1, keepdims=True)
            p = e * jnp.where(m > 0.5 * NEG_INF, 1.0 / l, 0.0)
            _store_heads(o_ref, g, _dot(p.astype(BF16), vc), tq)
            psum = p[0:tq]
            for hh in range(1, NSA_HPG):
                psum = psum + p[hh * tq:(hh + 1) * tq]
            imp_ref[g] = _dot_nt(ov_ref[:, 0:ncols], psum.astype(BF16))

    n_live = jnp.maximum((t0 + tq - CMP_LEN) // CMP_STRIDE + 1, 1)
    n_chunks = jnp.minimum((n_live + CMP_CHUNK - 1) // CMP_CHUNK, ncp // CMP_CHUNK)
    for nc in range(1, ncp // CMP_CHUNK + 1):
        pl.when(n_chunks == nc)(functools.partial(attend, nc * CMP_CHUNK))

    j_idx = lax.broadcasted_iota(jnp.int32, (nsp, tq), 0)
    cur = (t0 + lax.broadcasted_iota(jnp.int32, (nsp, tq), 1)) // SEL_LEN
    forced = (j_idx == 0) | (j_idx == cur) | (j_idx == cur - 1)
    j_f = j_idx.astype(F32)
    for g in range(NSA_GROUPS):
        score = jnp.where(j_idx <= cur, imp_ref[g], NEG_INF)
        score = jnp.where(forced | (j_idx >= n_sel), REMOVED, score)
        sel = jnp.where(forced, 1.0, 0.0)
        for _ in range(max(top_n - 3, 0)):
            mx = jnp.max(score, axis=0, keepdims=True)
            idx = jnp.min(jnp.where(score == mx, j_f, float(nsp)), axis=0, keepdims=True)
            hit = j_f == idx
            sel = jnp.where(hit, 1.0, sel)
            score = jnp.where(hit, REMOVED, score)
        sel = jnp.where(j_idx <= cur, sel, 0.0)
        m_ref[g] = sel.T.astype(m_ref.dtype)


SEL_BONUS = 32768.0
NSA_SEL_TQ = 256
NSA_SEL_TK = 1024


def _nsa_sel_kernel(q_ref, k_ref, v_ref, m_ref, et_ref, o_ref, *, tq, tk):
    t0 = pl.program_id(1) * tq
    n_full = t0 // tk
    rows = NSA_HPG * tq

    def update(carry, q, ks, vs, mask=None):
        m, acc = carry
        s = _dot_nt(q, ks)
        if mask is not None:
            s = jnp.where(mask, s, NEG_INF)
        m_new = jnp.maximum(m, jnp.max(s, axis=-1, keepdims=True))
        p = jnp.exp2(s - m_new)
        return m_new, jnp.exp2(m - m_new) * acc + _dot(p.astype(BF16), vs)

    qs, carries = [], []
    for g in range(NSA_GROUPS):
        q = jnp.concatenate([_stack_heads(q_ref, g), jnp.concatenate([m_ref[g]] * NSA_HPG, axis=0)], axis=1)

        def step(j, carry, q=q, g=g):
            start = pl.multiple_of(j * tk, tk)
            ks = jnp.concatenate([k_ref[pl.ds(start, tk), :], et_ref[pl.ds(start, tk), :]], axis=1)
            return update(carry, q, ks, v_ref[g, pl.ds(start, tk), :])

        init = (jnp.full((rows, 1), NEG_INF, F32), jnp.zeros((rows, LANES), F32))
        qs.append(q)
        carries.append(lax.fori_loop(0, n_full, step, init))

    start = pl.multiple_of(n_full * tk, tk)

    def tail(nk):
        trow = t0 + lax.broadcasted_iota(jnp.int32, (rows, nk), 0) % tq
        causal = start + lax.broadcasted_iota(jnp.int32, (rows, nk), 1) <= trow
        ks = jnp.concatenate([k_ref[pl.ds(start, nk), :], et_ref[pl.ds(start, nk), :]], axis=1)
        for g in range(NSA_GROUPS):
            _, acc = update(carries[g], qs[g], ks, v_ref[g, pl.ds(start, nk), :], causal)
            den = HEAD_DIM * (1 - g)
            _store_heads(o_ref, g, acc / acc[:, den:den + 1], tq)

    which = (t0 - start) // tq
    for v in range(tk // tq):
        pl.when(which == v)(functools.partial(tail, (v + 1) * tq))


def nsa_value_augment(v):
    ones = jnp.ones_like(v[..., :HEAD_DIM])
    return jnp.stack([jnp.concatenate([v[..., :HEAD_DIM], ones], axis=-1),
                      jnp.concatenate([ones, v[..., HEAD_DIM:]], axis=-1)], axis=1)


def nsa_selected(p1, v_aug, sel, et_mat, T, *, tq=NSA_SEL_TQ, tk=NSA_SEL_TK):
    B = p1.shape[0]
    nsp = sel.shape[-1]
    return pl.pallas_call(
        functools.partial(_nsa_sel_kernel, tq=tq, tk=tk),
        grid=(B, T // tq),
        in_specs=[pl.BlockSpec((None, tq, NSA_HEADS * HEAD_DIM), lambda b, i: (b, i, 0)),
                  pl.BlockSpec((None, T, LANES), lambda b, i: (b, 0, P1_NKS // LANES)),
                  pl.BlockSpec((None, NSA_GROUPS, T, LANES), lambda b, i: (b, 0, 0, 0)),
                  pl.BlockSpec((None, NSA_GROUPS, tq, nsp), lambda b, i: (b, 0, i, 0)),
                  pl.BlockSpec((T, nsp), lambda b, i: (0, 0))],
        out_specs=pl.BlockSpec((None, tq, NSA_OUT), lambda b, i: (b, i, 0)),
        out_shape=jax.ShapeDtypeStruct((B, T, NSA_OUT), BF16),
        compiler_params=_cparams("parallel", "parallel"),
        name="nsa_selected",
    )(p1, p1, v_aug, sel, et_mat)


def _nsa_win_kernel(q_ref, k_ref, v_ref, b_ref, o_ref, *, tq):
    t0 = pl.program_id(1) * tq
    span = WINDOW + tq
    start = pl.multiple_of(jnp.maximum(t0 - WINDOW, 0), tq)
    ks = k_ref[pl.ds(start, span), :]
    vs = v_ref[pl.ds(start, span), :]

    def run(bias):
        bias = jnp.concatenate([bias] * NSA_HPG, axis=0)
        for g in range(NSA_GROUPS):
            s = _dot_nt(_stack_heads(q_ref, g), ks) + bias
            m = jnp.max(s, axis=-1, keepdims=True)
            p = jnp.exp2(s - m)
            l = jnp.sum(p, axis=-1, keepdims=True)
            _store_heads(o_ref, g, _dot(p.astype(BF16), vs) / l, tq)

    @pl.when(t0 >= WINDOW)
    def _():
        run(b_ref[...])

    @pl.when(t0 < WINDOW)
    def _():
        row = lax.broadcasted_iota(jnp.int32, (tq, span), 0)
        col = lax.broadcasted_iota(jnp.int32, (tq, span), 1)
        run(jnp.where(col <= t0 + row, 0.0, NEG_INF))


def _nsa_cmp_win_kernel(q_ref, kc_ref, vc_ref, ov_ref, kw_ref, vw_ref, band_ref, ocmp_ref, m_ref, owin_ref, imp_ref,
                        *, tq, n_sel, top_n):
    _nsa_cmp_kernel(q_ref, kc_ref, vc_ref, ov_ref, ocmp_ref, m_ref, imp_ref, tq=tq, n_sel=n_sel, top_n=top_n)
    _nsa_win_kernel(q_ref, kw_ref, vw_ref, band_ref, owin_ref, tq=tq)


def nsa_cmp_select_window(p1, p2, kc, vc, ov_t, T):
    B = p1.shape[0]
    tq = NSA_QBLOCK
    ncp = kc.shape[1]
    nsp = ov_t.shape[0]
    n_sel = T // SEL_LEN
    span = WINDOW + tq
    r = np.arange(tq)[:, None]
    c = np.arange(span)[None, :]
    band = jnp.asarray(np.where((c > r) & (c <= r + WINDOW), 0.0, NEG_INF), F32)
    out_blk = pl.BlockSpec((None, tq, NSA_OUT), lambda b, i: (b, i, 0))
    return pl.pallas_call(
        functools.partial(_nsa_cmp_win_kernel, tq=tq, n_sel=n_sel, top_n=min(SEL_TOPN, n_sel)),
        grid=(B, T // tq),
        in_specs=[pl.BlockSpec((None, tq, NSA_HEADS * HEAD_DIM), lambda b, i: (b, i, 0)),
                  pl.BlockSpec((None, ncp, LANES), lambda b, i: (b, 0, 0)),
                  pl.BlockSpec((None, ncp, LANES), lambda b, i: (b, 0, 0)),
                  pl.BlockSpec((nsp, ncp), lambda b, i: (0, 0)),
                  pl.BlockSpec((None, T, LANES), lambda b, i: (b, 0, P1_NKW // LANES)),
                  pl.BlockSpec((None, T, LANES), lambda b, i: (b, 0, P2_NVW // LANES)),
                  pl.BlockSpec((tq, span), lambda b, i: (0, 0))],
        out_specs=[out_blk, pl.BlockSpec((None, NSA_GROUPS, tq, nsp), lambda b, i: (b, 0, i, 0)), out_blk],
        out_shape=[jax.ShapeDtypeStruct((B, T, NSA_OUT), BF16),
                   jax.ShapeDtypeStruct((B, NSA_GROUPS, T, nsp), BF16),
                   jax.ShapeDtypeStruct((B, T, NSA_OUT), BF16)],
        scratch_shapes=[pltpu.VMEM((NSA_GROUPS, nsp, tq), F32)],
        compiler_params=_cparams("parallel", "parallel"),
        name="nsa_cmp_select_window",
    )(p1, kc, vc, ov_t, p1, p2, band)


def _retention_kernel(q_ref, k_ref, v_ref, g_ref, din_ref, qd_ref, kd_ref, cd_ref, o_ref, st_ref):
    @pl.when(pl.program_id(0) == 0)
    def _():
        st_ref[...] = jnp.zeros_like(st_ref)

    B = q_ref.shape[0]
    C = RET_CHUNK
    half = lax.broadcasted_iota(jnp.int32, (C, LANES), 1) // HEAD_DIM
    for b in range(B):
        for h in range(RET_HEADS):
            lanes = slice(h * LANES, (h + 1) * LANES)
            pair = slice((h // 2) * LANES, (h // 2 + 1) * LANES)
            st = st_ref[b, h]
            for sub in range(q_ref.shape[1] // C):
                rows = slice(sub * C, (sub + 1) * C)
                qh = jnp.where(half == h % 2, q_ref[b, rows, pair], 0.0).astype(BF16)
                kp = k_ref[b, rows, pair]
                vh = v_ref[b, rows, lanes]
                inner = _dot_nt(qh, kp) * din_ref[h]
                o = _dot(inner.astype(BF16), vh) + _dot(qh, st.astype(BF16)) * qd_ref[h]
                kd = (kp.astype(F32) * kd_ref[h]).astype(BF16)
                st = st * cd_ref[h, 0:1, :] + _dot_tn(kd, vh)
                mu = jnp.mean(o, axis=-1, keepdims=True)
                d = o - mu
                var = jnp.mean(d * d, axis=-1, keepdims=True)
                on = d * lax.rsqrt(var + NORM_EPS)
                gh = g_ref[b, rows, lanes].astype(F32)
                o_ref[b, rows, lanes] = (gh * _sigmoid(gh) * on).astype(o_ref.dtype)
            st_ref[b, h] = st


def retention_consts():
    C = RET_CHUNK
    H = RET_HEADS
    log_g = jnp.log(1.0 - 2.0 ** (-5.0 - jnp.arange(H, dtype=F32)))
    n = jnp.arange(C, dtype=F32)
    diff = n[:, None] - n[None, :]
    causal = diff >= 0
    decay_in = jnp.where(causal[None], jnp.exp(jnp.where(causal, diff, 0.0)[None] * log_g[:, None, None]), 0.0)
    q_decay = jnp.exp((n[None, :] + 1.0) * log_g[:, None])
    k_decay = jnp.exp((C - 1.0 - n)[None, :] * log_g[:, None])
    chunk_decay = jnp.exp(C * log_g)
    qd = jnp.broadcast_to(q_decay[:, :, None], (H, C, LANES))
    kd = jnp.broadcast_to(k_decay[:, :, None], (H, C, LANES))
    cd = jnp.broadcast_to(chunk_decay[:, None, None], (H, 8, LANES))
    return decay_in, qd, kd, cd


RET_STEP = 4


def retention(p1, p2, consts, T):
    B = p1.shape[0]
    C = RET_CHUNK * RET_STEP
    din, qd, kd, cd = consts
    W = RET_HEADS * LANES
    full = lambda shape: pl.BlockSpec(shape, lambda c: (0,) * len(shape))
    return pl.pallas_call(
        _retention_kernel,
        grid=(T // C,),
        in_specs=[pl.BlockSpec((B, C, W // 2), lambda c: (0, c, P1_RQ // (W // 2))),
                  pl.BlockSpec((B, C, W // 2), lambda c: (0, c, P1_RK // (W // 2))),
                  pl.BlockSpec((B, C, W), lambda c: (0, c, P2_RV // W)),
                  pl.BlockSpec((B, C, W), lambda c: (0, c, P2_RG // W)),
                  full(din.shape), full(qd.shape), full(kd.shape), full(cd.shape)],
        out_specs=pl.BlockSpec((B, C, W), lambda c: (0, c, 0)),
        out_shape=jax.ShapeDtypeStruct((B, T, W), BF16),
        scratch_shapes=[pltpu.VMEM((B, RET_HEADS, LANES, LANES), F32)],
        compiler_params=_cparams("arbitrary"),
        name="retention",
    )(p1, p1, p2, p2, din, qd, kd, cd)


def _fox_cum_kernel(f_ref, b_ref, o_ref):
    x = f_ref[...] + b_ref[...]
    ls = jnp.minimum(x, 0.0) - jnp.log1p(jnp.exp(-jnp.abs(x)))
    R = x.shape[0]
    ki = lax.broadcasted_iota(jnp.int32, (LANES, LANES), 0)
    ji = lax.broadcasted_iota(jnp.int32, (LANES, LANES), 1)
    upper = jnp.where(ki <= ji, 1.0, 0.0).astype(BF16)
    hi, mid, lo = _split3(ls)
    rowcum = _dot(hi, upper) + _dot(mid, upper) + _dot(lo, upper)
    tot = jnp.broadcast_to(rowcum[:, LANES - 1:LANES], (R, LANES))
    ri = lax.broadcasted_iota(jnp.int32, (R, R), 0)
    ci = lax.broadcasted_iota(jnp.int32, (R, R), 1)
    lower = jnp.where(ci < ri, 1.0, 0.0).astype(BF16)
    hi, mid, lo = _split3(tot)
    offs = _dot(lower, hi) + _dot(lower, mid) + _dot(lower, lo)
    o_ref[...] = (rowcum + offs) * LOG2E


def fox_cum(f_logit, bias):
    B, H, R, _ = f_logit.shape
    return pl.pallas_call(
        _fox_cum_kernel,
        grid=(B, H),
        in_specs=[pl.BlockSpec((None, None, R, LANES), lambda b, h: (b, h, 0, 0)),
                  pl.BlockSpec((None, 1, LANES), lambda b, h: (h, 0, 0))],
        out_specs=pl.BlockSpec((None, None, R, LANES), lambda b, h: (b, h, 0, 0)),
        out_shape=jax.ShapeDtypeStruct((B, H, R, LANES), F32),
        compiler_params=_cparams("parallel", "parallel"),
        name="fox_cum",
    )(f_logit, bias)


FOX_BIAS_LANES = 3


def _fox_kernel(q_ref, k_ref, v_ref, c_ref, o_ref, ka_ref, va_ref, *, tq):
    i = pl.program_id(2)
    tk = tq
    T = k_ref.shape[0]
    chunk = 512

    @pl.when(i == 0)
    def _():
        lane = lax.broadcasted_iota(jnp.int32, (chunk, LANES), 1)
        ri = lax.broadcasted_iota(jnp.int32, (16, LANES), 0)
        ci = lax.broadcasted_iota(jnp.int32, (16, LANES), 1)
        place = jnp.where((ci == ri + HEAD_DIM) & (ri < FOX_BIAS_LANES), 1.0, 0.0).astype(BF16)

        def build(c, _):
            c0 = pl.multiple_of(c * chunk, chunk)
            kp = k_ref[pl.ds(c0, chunk), :].astype(F32)
            vp = v_ref[pl.ds(c0, chunk), :].astype(F32)
            for hh in range(2):
                hi, mid, lo = _split3(-c_ref[hh, :, pl.ds(c0, chunk)])
                terms = jnp.concatenate([hi, mid, lo, jnp.zeros((13, chunk), BF16)], axis=0)
                bias = _dot_tn(terms, place)
                kh = kp if hh == 0 else pltpu.roll(kp, HEAD_DIM, 1)
                vh = vp if hh == 0 else pltpu.roll(vp, HEAD_DIM, 1)
                ka_ref[hh, pl.ds(c0, chunk), :] = jnp.where(lane < HEAD_DIM, kh, bias).astype(BF16)
                va_ref[hh, pl.ds(c0, chunk), :] = jnp.where(lane < HEAD_DIM, vh, 1.0).astype(BF16)
            return 0

        lax.fori_loop(0, T // chunk, build, 0)

    lane = lax.broadcasted_iota(jnp.int32, (tq, LANES), 1)
    ones_lanes = (lane >= HEAD_DIM) & (lane < HEAD_DIM + FOX_BIAS_LANES)
    qp = q_ref[...].astype(F32) * (HEAD_DIM ** -0.5 * LOG2E)
    qs = [jnp.where(lane < HEAD_DIM, qh, jnp.where(ones_lanes, 1.0, 0.0)).astype(BF16)
          for qh in (qp, pltpu.roll(qp, HEAD_DIM, 1))]

    def update(hh, m, acc, q, start, size, mask=None):
        s = _dot_nt(q, ka_ref[hh, pl.ds(start, size), :])
        if mask is not None:
            s = jnp.where(mask, s, NEG_INF)
        m_new = jnp.maximum(m, jnp.max(s, axis=-1, keepdims=True))
        p = jnp.exp2(s - m_new)
        return m_new, jnp.exp2(m - m_new) * acc + _dot(p.astype(BF16), va_ref[hh, pl.ds(start, size), :])

    def step(j, carry):
        start = pl.multiple_of(j * tk, tk)
        return tuple(update(hh, *carry[hh], qs[hh], start, tk) for hh in range(2))

    one = (jnp.full((tq, 1), NEG_INF, F32), jnp.zeros((tq, LANES), F32))
    carry = lax.fori_loop(0, i, step, (one, one))

    half = tq // 2
    start = pl.multiple_of(i * tk, tk)
    row = lax.broadcasted_iota(jnp.int32, (tq, half), 0)
    col = lax.broadcasted_iota(jnp.int32, (tq, half), 1)
    accs = []
    for hh in range(2):
        m, acc = update(hh, *carry[hh], qs[hh], start, half, col <= row)
        _, low = update(hh, m[half:], acc[half:], qs[hh][half:], start + half, half, (col <= row)[:half])
        accs.append(jnp.concatenate([acc[:half], low], axis=0))
    acc0, acc1 = accs
    o0 = acc0 / acc0[:, HEAD_DIM:HEAD_DIM + 1]
    o1 = acc1 / acc1[:, HEAD_DIM:HEAD_DIM + 1]
    o_ref[...] = jnp.where(lane < HEAD_DIM, o0, pltpu.roll(o1, HEAD_DIM, 1)).astype(o_ref.dtype)


def fox_attention(p2, cum, T, *, tq=FOX_TQ):
    B = p2.shape[0]
    HP = FOX_HEADS // 2
    return pl.pallas_call(
        functools.partial(_fox_kernel, tq=tq),
        grid=(B, HP, T // tq),
        in_specs=[pl.BlockSpec((None, tq, LANES), lambda b, h, i: (b, i, P2_FQ // LANES + h)),
                  pl.BlockSpec((None, T, LANES), lambda b, h, i: (b, 0, P2_FK // LANES + h)),
                  pl.BlockSpec((None, T, LANES), lambda b, h, i: (b, 0, P2_FV // LANES + h)),
                  pl.BlockSpec((None, None, 2, 1, T), lambda b, h, i: (b, h, 0, 0, 0))],
        out_specs=pl.BlockSpec((None, tq, LANES), lambda b, h, i: (b, i, h)),
        out_shape=jax.ShapeDtypeStruct((B, T, FOX_HEADS * HEAD_DIM), BF16),
        scratch_shapes=[pltpu.VMEM((2, T, LANES), BF16), pltpu.VMEM((2, T, LANES), BF16)],
        compiler_params=_cparams("parallel", "parallel", "arbitrary"),
        name="fox_attention",
    )(p2, p2, p2, cum)


def _readout_kernel(ocmp_ref, osel_ref, owin_ref, small_ref, oret_ref, ofox_ref, mg_ref, x_ref, g1_ref,
                    ex_ref, wn_ref, wr_ref, wf_ref, wo_ref, o_ref):
    W = NSA_OUT
    gs = _sigmoid(small_ref[...].astype(F32)).astype(BF16)
    ge = _dot(gs, ex_ref[...])
    onsa = (ge[:, :W] * ocmp_ref[...].astype(F32) + ge[:, W:2 * W] * osel_ref[...].astype(F32)
            + ge[:, 2 * W:] * owin_ref[...].astype(F32))
    D = D_MODEL
    merged = (_sigmoid(mg_ref[:, :D].astype(F32)) * _dot(onsa.astype(BF16), wn_ref[...])
              + _sigmoid(mg_ref[:, D:2 * D].astype(F32)) * _dot(oret_ref[...], wr_ref[...])
              + _sigmoid(mg_ref[:, 2 * D:].astype(F32)) * _dot(ofox_ref[...], wf_ref[...]))
    y = _dot(merged.astype(BF16), wo_ref[...])
    o_ref[...] = x_ref[...] + g1_ref[...] * y


def readout(o_cmp, o_sel, o_win, p2, o_ret, o_fox, x, mod_l, ex, wn, wr, wf, wo, l, T, *, tm=512):
    M, D = x.shape
    per_b = T // tm
    W = NSA_OUT
    row = lambda width, col=0: pl.BlockSpec((tm, width), lambda i: (i, col))
    full = lambda a: pl.BlockSpec(a.shape, lambda i: (0,) * a.ndim)
    return pl.pallas_call(
        _readout_kernel,
        grid=(M // tm,),
        in_specs=[row(W), row(W), row(W), row(LANES, P2_SMALL // LANES), row(512), row(512),
                  row(3 * D, 0), row(D),
                  pl.BlockSpec((None, None, 1, D), lambda i: (i // per_b, 2, 0, 0)),
                  full(ex), _layer_spec(wn, l), _layer_spec(wr, l), _layer_spec(wf, l), _layer_spec(wo, l)],
        out_specs=row(D),
        out_shape=jax.ShapeDtypeStruct((M, D), F32),
        compiler_params=_cparams("parallel"),
        name="mixer_readout",
    )(o_cmp, o_sel, o_win, p2, o_ret, o_fox, p2, x, mod_l, ex, wn, wr, wf, wo)


def nsa_gate_expand():
    ex = np.zeros((LANES, 3 * NSA_OUT), np.float32)
    for br in range(3):
        for h in range(NSA_HEADS):
            c0 = br * NSA_OUT + h * HEAD_DIM
            ex[br * NSA_HEADS + h, c0:c0 + HEAD_DIM] = 1.0
    return jnp.asarray(ex, BF16)


FFN_CHUNK = 512


def _ffn_kernel(x_ref, nw_ref, sc_ref, sh_ref, g2_ref, w1_ref, w3_ref, w2_ref, o_ref):
    x = x_ref[...]
    h = _norm_mod(x, nw_ref[...], sc_ref[...], sh_ref[...]).astype(BF16)
    F = w1_ref.shape[1]
    y = None
    for c0 in range(0, F, FFN_CHUNK):
        cols = slice(c0, min(c0 + FFN_CHUNK, F))
        u = _dot(h, w1_ref[:, cols])
        v = _dot(h, w3_ref[:, cols])
        part = _dot((u * _sigmoid(u) * v).astype(BF16), w2_ref[cols, :])
        y = part if y is None else y + part
    o_ref[...] = x + g2_ref[...] * y


def ffn(x, mod_l, nw, w1, w3, w2, T, *, tm=512):
    M, D = x.shape
    F = w1.shape[1]
    per_b = T // tm
    modspec = lambda k: pl.BlockSpec((None, None, 1, D), lambda i: (i // per_b, k, 0, 0))
    full = lambda a: pl.BlockSpec(a.shape, lambda i: (0,) * a.ndim)
    return pl.pallas_call(
        _ffn_kernel,
        grid=(M // tm,),
        in_specs=[pl.BlockSpec((tm, D), lambda i: (i, 0)),
                  pl.BlockSpec((1, D), lambda i: (0, 0)),
                  modspec(4), modspec(3), modspec(5), full(w1), full(w3), full(w2)],
        out_specs=pl.BlockSpec((tm, D), lambda i: (i, 0)),
        out_shape=jax.ShapeDtypeStruct((M, D), F32),
        compiler_params=_cparams("parallel"),
        name="ffn_dense",
    )(x, nw, mod_l, mod_l, mod_l, w1, w3, w2)


MOE_TC = 512
MOE_TS = 512


def _router_kernel(x_ref, nw_ref, sc_ref, sh_ref, wh_ref, wl_ref, h_ref, gate_ref, rank_ref, cnt_ref, carry_ref):
    @pl.when(pl.program_id(0) == 0)
    def _():
        carry_ref[...] = jnp.zeros_like(carry_ref)

    h = _norm_mod(x_ref[...], nw_ref[...], sc_ref[...], sh_ref[...])
    hh = h.astype(BF16)
    h_ref[...] = _pack_bf16_pairs(hh.astype(F32))
    hl = (h - hh.astype(F32)).astype(BF16)
    logits = _dot(hh, wh_ref[...]) + (_dot(hl, wh_ref[...]) + _dot(hh, wl_ref[...]))
    tm = logits.shape[0]
    lane = lax.broadcasted_iota(jnp.int32, logits.shape, 1)
    logits = jnp.where(lane < N_EXPERTS, logits, REMOVED)
    lane_f = lane.astype(F32)
    v1 = jnp.max(logits, axis=-1, keepdims=True)
    i1 = jnp.min(jnp.where(logits == v1, lane_f, float(LANES)), axis=-1, keepdims=True)
    rest = jnp.where(lane_f == i1, REMOVED, logits)
    v2 = jnp.max(rest, axis=-1, keepdims=True)
    i2 = jnp.min(jnp.where(rest == v2, lane_f, float(LANES)), axis=-1, keepdims=True)
    e2 = jnp.exp(v2 - v1)
    w1 = 1.0 / (1.0 + e2)
    w2 = e2 / (1.0 + e2)
    gate_ref[...] = jnp.where(lane_f == i1, w1, jnp.where(lane_f == i2, w2, 0.0))

    sel = jnp.where((lane_f == i1) | (lane_f == i2), 1.0, 0.0)
    ri = lax.broadcasted_iota(jnp.int32, (tm, tm), 0)
    ci = lax.broadcasted_iota(jnp.int32, (tm, tm), 1)
    before = jnp.where(ci < ri, 1.0, 0.0).astype(BF16)
    rank = _dot(before, sel.astype(BF16)) + carry_ref[0:1, :]
    rank_ref[...] = jnp.where(sel > 0.0, rank, -1.0)
    carry_ref[...] = carry_ref[...] + jnp.sum(sel, axis=0, keepdims=True)
    cnt_ref[...] = carry_ref[...]


def router(x, mod_l, nw, w_router, T):
    M, D = x.shape
    tm = MOE_TC
    per_b = T // tm
    wp = jnp.zeros((D, LANES), F32).at[:, :N_EXPERTS].set(w_router)
    wh = wp.astype(BF16)
    wl = (wp - wh.astype(F32)).astype(BF16)
    return pl.pallas_call(
        _router_kernel,
        grid=(M // tm,),
        in_specs=[pl.BlockSpec((tm, D), lambda i: (i, 0)),
                  pl.BlockSpec((1, D), lambda i: (0, 0))]
        + _mod_specs(T, tm, 4, 3, 1)
        + [pl.BlockSpec((D, LANES), lambda i: (0, 0)),
           pl.BlockSpec((D, LANES), lambda i: (0, 0))],
        out_specs=[pl.BlockSpec((tm, D // 2), lambda i: (i, 0)),
                   pl.BlockSpec((tm, LANES), lambda i: (i, 0)),
                   pl.BlockSpec((tm, LANES), lambda i: (i, 0)),
                   pl.BlockSpec((8, LANES), lambda i: (0, 0))],
        out_shape=[jax.ShapeDtypeStruct((M, D // 2), jnp.uint32),
                   jax.ShapeDtypeStruct((M, LANES), F32),
                   jax.ShapeDtypeStruct((M, LANES), F32),
                   jax.ShapeDtypeStruct((8, LANES), F32)],
        scratch_shapes=[pltpu.VMEM((8, LANES), F32)],
        compiler_params=_cparams("arbitrary"),
        name="moe_router",
    )(x, nw, mod_l, mod_l, wh, wl)


def _count_le(sorted_vals, x):
    return jnp.sum(sorted_vals[None, :] <= x[:, None], axis=1, dtype=jnp.int32)


def _moe_up_kernel(e_r, total, x_ref, w1_ref, w3_ref, o_ref, w1b_ref, w3b_ref):
    r = pl.program_id(1)
    live = r < total[0]

    @pl.when(live & ((r == 0) | (e_r[r] != e_r[jnp.maximum(r - 1, 0)])))
    def _():
        w1b_ref[...] = w1_ref[...].astype(BF16)
        w3b_ref[...] = w3_ref[...].astype(BF16)

    @pl.when(live)
    def _():
        x = _unpack_bf16_pairs(x_ref[...]).astype(BF16)
        u = _dot(x, w1b_ref[...])
        v = _dot(x, w3b_ref[...])
        o_ref[...] = (u * _sigmoid(u) * v).astype(o_ref.dtype)


def moe_up(xs, w1, w3, tiles, rt, *, tf=1792):
    R = xs.shape[0]
    D = w1.shape[1]
    ts = MOE_TS
    F = w1.shape[-1]
    live = lambda r, total: jnp.minimum(r, total[0] - 1)
    return pl.pallas_call(
        _moe_up_kernel,
        grid_spec=pltpu.PrefetchScalarGridSpec(
            num_scalar_prefetch=2,
            grid=(F // tf, rt),
            in_specs=[pl.BlockSpec((ts, D // 2), lambda n, r, e, total: (live(r, total), 0)),
                      pl.BlockSpec((None, D, tf), lambda n, r, e, total: (e[live(r, total)], 0, n)),
                      pl.BlockSpec((None, D, tf), lambda n, r, e, total: (e[live(r, total)], 0, n))],
            out_specs=pl.BlockSpec((ts, tf), lambda n, r, e, total: (r, n)),
            scratch_shapes=[pltpu.VMEM((D, tf), BF16), pltpu.VMEM((D, tf), BF16)],
        ),
        out_shape=jax.ShapeDtypeStruct((R, F), BF16),
        compiler_params=_cparams("arbitrary", "arbitrary"),
        name="moe_up",
    )(tiles["e"], tiles["total"], xs, w1, w3)


def _moe_down_kernel(e_r, total, a_ref, w2_ref, o_ref, w2b_ref):
    r = pl.program_id(0)
    live = r < total[0]

    @pl.when(live & ((r == 0) | (e_r[r] != e_r[jnp.maximum(r - 1, 0)])))
    def _():
        w2b_ref[...] = w2_ref[...].astype(BF16)

    @pl.when(live)
    def _():
        o_ref[...] = _pack_bf16_pairs(_dot(a_ref[...], w2b_ref[...]))


def moe_down(a, w2, tiles, rt):
    R, F = a.shape
    ts = MOE_TS
    D = w2.shape[-1]
    live = lambda r, total: jnp.minimum(r, total[0] - 1)
    return pl.pallas_call(
        _moe_down_kernel,
        grid_spec=pltpu.PrefetchScalarGridSpec(
            num_scalar_prefetch=2,
            grid=(rt,),
            in_specs=[pl.BlockSpec((ts, F), lambda r, e, total: (live(r, total), 0)),
                      pl.BlockSpec((None, F, D), lambda r, e, total: (e[live(r, total)], 0, 0))],
            out_specs=pl.BlockSpec((ts, D // 2), lambda r, e, total: (r, 0)),
            scratch_shapes=[pltpu.VMEM((F, D), BF16)],
        ),
        out_shape=jax.ShapeDtypeStruct((R, D // 2), jnp.uint32),
        compiler_params=_cparams("arbitrary"),
        name="moe_down",
    )(tiles["e"], tiles["total"], a, w2)


SC_WINDOW = 64


def _sc_mesh():
    return plsc.VectorSubcoreMesh(core_axis_name="core", subcore_axis_name="subcore")


def sc_scatter_rows2(x, idx_a, idx_b, n_out):
    n, d = x.shape
    steps = n // SC_WINDOW

    @pl.kernel(out_type=jax.ShapeDtypeStruct((n_out, d), x.dtype), mesh=_sc_mesh(), scratch_types=[])
    def kern(x_hbm, ia_hbm, ib_hbm, o_hbm):
        def body(x_vmem, ia_vmem, ib_vmem):
            pltpu.sync_copy(x_vmem, o_hbm.at[ia_vmem.at[0]])
            pltpu.sync_copy(x_vmem, o_hbm.at[ib_vmem.at[0]])

        pltpu.emit_pipeline(
            body,
            grid=(steps,),
            in_specs=[pl.BlockSpec((SC_WINDOW, d), index_map=lambda i: (i, 0)),
                      pl.BlockSpec((1, SC_WINDOW), index_map=lambda i: (i, 0)),
                      pl.BlockSpec((1, SC_WINDOW), index_map=lambda i: (i, 0))],
            out_specs=[],
            core_axis_name=("core", "subcore"),
            dimension_semantics=(pltpu.PARALLEL,),
        )(x_hbm, ia_hbm, ib_hbm)

    return kern(x, idx_a.reshape(steps, SC_WINDOW), idx_b.reshape(steps, SC_WINDOW))


def sc_gather_rows(x, idx):
    n = idx.shape[0]
    d = x.shape[1]
    steps = n // SC_WINDOW

    @pl.kernel(out_type=jax.ShapeDtypeStruct((n, d), x.dtype), mesh=_sc_mesh(), scratch_types=[])
    def kern(x_hbm, i_hbm, o_hbm):
        def body(i_vmem, o_vmem):
            pltpu.sync_copy(x_hbm.at[i_vmem.at[0]], o_vmem)

        pltpu.emit_pipeline(
            body,
            grid=(steps,),
            in_specs=[pl.BlockSpec((1, SC_WINDOW), index_map=lambda i: (i, 0))],
            out_specs=[pl.BlockSpec((SC_WINDOW, d), index_map=lambda i: (i, 0))],
            core_axis_name=("core", "subcore"),
            dimension_semantics=(pltpu.PARALLEL,),
        )(i_hbm, o_hbm)

    return kern(x, idx.reshape(steps, SC_WINDOW))


def _moe_finish_kernel(x_ref, g2_ref, ya_ref, yb_ref, gate_ref, rank_ref, nw_ref, o_ref, *, normalize):
    gate = gate_ref[...]
    chosen = rank_ref[...] >= 0.0
    lane = lax.broadcasted_iota(jnp.int32, gate.shape, 1).astype(F32)
    first = jnp.min(jnp.where(chosen, lane, float(LANES)), axis=-1, keepdims=True)
    last = jnp.max(jnp.where(chosen, lane, -1.0), axis=-1, keepdims=True)
    wa = jnp.sum(jnp.where(lane == first, gate, 0.0), axis=-1, keepdims=True)
    wb = jnp.sum(jnp.where(lane == last, gate, 0.0), axis=-1, keepdims=True)
    x = x_ref[...] + g2_ref[...] * (wa * _unpack_bf16_pairs(ya_ref[...]) + wb * _unpack_bf16_pairs(yb_ref[...]))
    if normalize:
        ms = jnp.mean(x * x, axis=-1, keepdims=True)
        x = x * lax.rsqrt(ms + NORM_EPS) * nw_ref[...]
    o_ref[...] = x


def moe_finish(x, mod_l, y2, gate, rank, norm_w, T, *, tm=512):
    M, D = x.shape
    per_b = T // tm
    normalize = norm_w is not None
    if norm_w is None:
        norm_w = jnp.ones((1, D), F32)
    return pl.pallas_call(
        functools.partial(_moe_finish_kernel, normalize=normalize),
        grid=(M // tm,),
        in_specs=[pl.BlockSpec((tm, D), lambda i: (i, 0)),
                  pl.BlockSpec((None, None, 1, D), lambda i: (i // per_b, 5, 0, 0)),
                  pl.BlockSpec((None, tm, D // 2), lambda i: (0, i, 0)),
                  pl.BlockSpec((None, tm, D // 2), lambda i: (1, i, 0)),
                  pl.BlockSpec((tm, LANES), lambda i: (i, 0)),
                  pl.BlockSpec((tm, LANES), lambda i: (i, 0)),
                  pl.BlockSpec((1, D), lambda i: (0, 0))],
        out_specs=pl.BlockSpec((tm, D), lambda i: (i, 0)),
        out_shape=jax.ShapeDtypeStruct((M, D), F32),
        compiler_params=_cparams("parallel"),
        name="moe_finish",
    )(x, mod_l, y2, y2, gate, rank, norm_w)


def moe_ffn(x, mod_l, nw, w_router, w1, w3, w2, T, norm_w=None):
    M = x.shape[0]
    ts = MOE_TS
    rt = (2 * M) // ts + N_EXPERTS
    h, gate, rank, cnt = router(x, mod_l, nw, w_router, T)
    i32 = jnp.int32
    counts = cnt[0, :N_EXPERTS].astype(i32)
    ntile = (counts + ts - 1) // ts
    tile_end = jnp.cumsum(ntile)
    row_off = (tile_end - ntile) * ts
    e_r = jnp.minimum(_count_le(tile_end, jnp.arange(rt, dtype=i32)), N_EXPERTS - 1)
    tiles = dict(e=e_r, total=tile_end[-1].reshape(1).astype(i32))
    rk = rank[:, :N_EXPERTS].astype(i32)
    pos = row_off[None, :] + rk
    pos_a = jnp.min(jnp.where(rk >= 0, pos, rt * ts), axis=1)
    pos_b = jnp.max(jnp.where(rk >= 0, pos, -1), axis=1)

    xs = sc_scatter_rows2(h, pos_a, pos_b, rt * ts)
    a = moe_up(xs, w1, w3, tiles, rt)
    y = moe_down(a, w2, tiles, rt)
    y2 = sc_gather_rows(y, jnp.concatenate([pos_a, pos_b])).reshape(2, M, -1)
    return moe_finish(x, mod_l, y2, gate, rank, norm_w, T)


def _final_norm_kernel(x_ref, w_ref, o_ref):
    x = x_ref[...]
    ms = jnp.mean(x * x, axis=-1, keepdims=True)
    o_ref[...] = x * lax.rsqrt(ms + NORM_EPS) * w_ref[...]


def final_norm(x, w, *, tm=1024):
    M, D = x.shape
    return pl.pallas_call(
        _final_norm_kernel,
        grid=(M // tm,),
        in_specs=[pl.BlockSpec((tm, D), lambda i: (i, 0)), pl.BlockSpec((1, D), lambda i: (0, 0))],
        out_specs=pl.BlockSpec((tm, D), lambda i: (i, 0)),
        out_shape=jax.ShapeDtypeStruct((M, D), F32),
        compiler_params=_cparams("parallel"),
        name="final_norm",
    )(x, w)


def nsa_constants(T):
    n_sel = T // SEL_LEN
    nsp = max(LANES, n_sel)
    ncp = T // CMP_STRIDE
    cmp_start = np.arange(ncp) * CMP_STRIDE
    sel_start = np.arange(nsp) * SEL_LEN
    ov = ((cmp_start[:, None] < sel_start[None, :] + SEL_LEN)
          & (cmp_start[:, None] + CMP_LEN > sel_start[None, :]))
    ov[(T - CMP_LEN) // CMP_STRIDE + 1:] = False
    ov[:, n_sel:] = False
    et_mat = ((np.arange(T)[:, None] // SEL_LEN) == np.arange(nsp)[None, :]) * SEL_BONUS
    return jnp.asarray(ov.T, BF16), jnp.asarray(et_mat, BF16)


def token_mixing(x, mod_l, lw, consts, B, T):
    M = B * T
    cos_t, sin_t, ov_t, e_mat, ret_consts, ex = consts
    l = lw["layer"]
    p1 = proj_rope(x, mod_l, lw["norm_mix"], lw["w1"], l, cos_t, sin_t, p1_scales(), T).reshape(B, T, P1_COLS)
    p2 = proj_plain(x, mod_l, lw["norm_mix"], lw["w2"], l, T).reshape(B, T, P2_COLS)

    def group_rows(a):
        return a.reshape(B, T, NSA_GROUPS, HEAD_DIM).transpose(0, 2, 1, 3).reshape(
            B, NSA_GROUPS, T // CMP_STRIDE, CMP_STRIDE * HEAD_DIM)

    xr = jnp.stack([group_rows(p1[:, :, P1_NKC:P1_NKC + LANES]), group_rows(p2[:, :, P2_NVC:P2_NVC + LANES])])
    cmp_out = compress(xr, lw["cmp_pe"], lw["cmp_w1"], lw["cmp_w2"])
    cmp_out = cmp_out.transpose(0, 1, 3, 2, 4).reshape(2, B, T // CMP_STRIDE, LANES)
    o_cmp, sel, o_win = nsa_cmp_select_window(p1, p2, cmp_out[0], cmp_out[1], ov_t, T)
    o_sel = nsa_selected(p1, nsa_value_augment(p2[:, :, P2_NVS:P2_NVS + LANES]), sel, e_mat, T)

    o_ret = retention(p1, p2, ret_consts, T)

    ff = p2[:, :, P2_SMALL + 3 * NSA_HEADS:P2_SMALL + 3 * NSA_HEADS + FOX_HEADS].astype(F32)
    ff = ff.transpose(0, 2, 1).reshape(B, FOX_HEADS, T // LANES, LANES)
    cum = fox_cum(ff, lw["fox_bias"]).reshape(B, FOX_HEADS // 2, 2, 1, T)
    o_fox = fox_attention(p2, cum, T)

    return readout(o_cmp.reshape(M, -1), o_sel.reshape(M, -1), o_win.reshape(M, -1), p2.reshape(M, P2_COLS),
                   o_ret.reshape(M, -1), o_fox.reshape(M, -1), x, mod_l, ex,
                   lw["wn"], lw["wr"], lw["wf"], lw["wo"], l, T)


def mixer_weights(norm_mix, w_in, cmp_k_pe, cmp_k_w1, cmp_k_w2, cmp_v_pe, cmp_v_w1, cmp_v_w2, fox_f_bias,
                  w_read_nsa, w_read_ret, w_read_fox, w_out):
    depth = w_in.shape[0]
    w1, w2 = split_w_in(w_in)
    pe = jnp.stack([cmp_k_pe.reshape(depth, 1, -1), cmp_v_pe.reshape(depth, 1, -1)], axis=1)
    pe = jnp.broadcast_to(pe, (depth, 2, 8, pe.shape[-1])).astype(BF16)
    shared = {
        "w1": w1, "w2": w2,
        "wn": w_read_nsa.astype(BF16),
        "wr": w_read_ret.astype(BF16),
        "wf": w_read_fox.astype(BF16),
        "wo": w_out.astype(BF16),
    }
    cmp_w1 = jnp.stack([cmp_k_w1, cmp_v_w1], axis=1).astype(BF16)
    cmp_w2 = jnp.stack([cmp_k_w2, cmp_v_w2], axis=1).astype(BF16)
    return [dict(shared, layer=l, norm_mix=norm_mix[l].reshape(1, -1), cmp_pe=pe[l], cmp_w1=cmp_w1[l], cmp_w2=cmp_w2[l],
                 fox_bias=jnp.broadcast_to(fox_f_bias[l][:, None, None], (FOX_HEADS, 1, LANES)))
            for l in range(depth)]


def kernel(x, c, ada_w, ada_b, norm_mix, norm_ffn, w_in, cmp_k_pe, cmp_k_w1, cmp_k_w2, cmp_v_pe, cmp_v_w1,
           cmp_v_w2, fox_f_bias, w_read_nsa, w_read_ret, w_read_fox, w_out, ffn_w1, ffn_w3, ffn_w2, router_w,
           moe_w1, moe_w3, moe_w2, final_norm_w):
    B, T, D = x.shape
    M = B * T
    depth = ada_w.shape[0]
    mod = modulation(c, ada_w, ada_b)
    cos_t, sin_t = rope_tables(T)
    ov_t, e_mat = nsa_constants(T)
    consts = (cos_t, sin_t, ov_t, e_mat, retention_consts(), nsa_gate_expand())
    xs = x.reshape(M, D)
    lws = mixer_weights(norm_mix, w_in, cmp_k_pe, cmp_k_w1, cmp_k_w2, cmp_v_pe, cmp_v_w1, cmp_v_w2,
                        fox_f_bias, w_read_nsa, w_read_ret, w_read_fox, w_out)
    for l in range(depth):
        xs = token_mixing(xs, mod[l], lws[l], consts, B, T)
        nf = norm_ffn[l].reshape(1, D)
        if l % 2 == 0:
            k = l // 2
            xs = ffn(xs, mod[l], nf, ffn_w1[k].astype(BF16), ffn_w3[k].astype(BF16), ffn_w2[k].astype(BF16), T)
        else:
            k = l // 2
            fuse = final_norm_w.reshape(1, D) if l == depth - 1 else None
            xs = moe_ffn(xs, mod[l], nf, router_w[k], moe_w1[k], moe_w3[k], moe_w2[k], T, fuse)
    if depth % 2 == 1:
        xs = final_norm(xs, final_norm_w.reshape(1, D))
    return xs.reshape(B, T, D)
```

```python
import functools

import jax
import jax.numpy as jnp
import numpy as np
from jax import lax
from jax.experimental import pallas as pl
from jax.experimental.pallas import tpu as pltpu
from jax.experimental.pallas import tpu_sc as plsc

F32 = jnp.float32
BF16 = jnp.bfloat16

D_MODEL = 1024
HEAD_DIM = 64
ROPE_THETA = 10000.0
NORM_EPS = 1e-6
NEG_INF = -1e30
REMOVED = -3e38

NSA_HEADS = 8
NSA_GROUPS = 2
NSA_HPG = NSA_HEADS // NSA_GROUPS
CMP_LEN = 32
CMP_STRIDE = 16
SEL_LEN = 64
SEL_TOPN = 16
WINDOW = 512
NSA_QBLOCK = 256

RET_HEADS = 4
RET_QK_DIM = 64
RET_CHUNK = 128

FOX_HEADS = 8
FOX_TQ = 1024
LOG2E = 1.4426950408889634

N_EXPERTS = 8

LANES = 128
VMEM_LIMIT = 56 * 1024 * 1024

P1_NQ = 0
P1_RQ = 512
P1_RK = 768
P1_NKC = 1024
P1_NKS = 1152
P1_NKW = 1280
P1_COLS = 1408
P2_MG = 0
P2_RV = 3072
P2_RG = 3584
P2_FQ = 4096
P2_FK = 4608
P2_FV = 5120
P2_NVC = 5632
P2_NVS = 5760
P2_NVW = 5888
P2_SMALL = 6016
P2_COLS = 6144
NSA_OUT = NSA_HEADS * HEAD_DIM


def _layer_spec(w, l):
    zeros = (0,) * (w.ndim - 1)
    return pl.BlockSpec((None,) + w.shape[1:], lambda *_: (l,) + zeros)


def _cparams(*sem):
    return pltpu.CompilerParams(dimension_semantics=tuple(sem), vmem_limit_bytes=VMEM_LIMIT)


def _sigmoid(x):
    return 1.0 / (1.0 + jnp.exp(-x))


def _dot(a, b):
    return jnp.dot(a, b, preferred_element_type=F32)


def _dot_nt(a, b):
    return lax.dot_general(a, b, (((1,), (1,)), ((), ())), preferred_element_type=F32)


def _dot_tn(a, b):
    return lax.dot_general(a, b, (((0,), (0,)), ((), ())), preferred_element_type=F32)


def _split3(x):
    hi = x.astype(BF16)
    r1 = x - hi.astype(F32)
    mid = r1.astype(BF16)
    lo = (r1 - mid.astype(F32)).astype(BF16)
    return hi, mid, lo


def _pack_bf16_pairs(x):
    c = x.shape[1] // 2
    lo = pltpu.bitcast(x[:, :c].astype(BF16).astype(F32), jnp.uint32) >> 16
    hi = pltpu.bitcast(x[:, c:].astype(BF16).astype(F32), jnp.uint32) & jnp.uint32(0xFFFF0000)
    return hi | lo


def _unpack_bf16_pairs(u):
    lo = pltpu.bitcast(u << 16, F32)
    hi = pltpu.bitcast(u & jnp.uint32(0xFFFF0000), F32)
    return jnp.concatenate([lo, hi], axis=1)


def _norm_mod(x, nw, sc, sh):
    ms = jnp.mean(x * x, axis=-1, keepdims=True)
    y = x * lax.rsqrt(ms + NORM_EPS) * nw
    return y * (1.0 + sc) + sh


def _mod_kernel(c_ref, w_ref, b_ref, o_ref):
    c = c_ref[...]
    s = c * _sigmoid(c)
    o_ref[0] = _dot(s.astype(BF16), w_ref[0].astype(BF16)) + b_ref[0]


def modulation(c, ada_w, ada_b):
    B, D = c.shape
    depth = ada_w.shape[0]
    rows = 8
    c_pad = jnp.zeros((rows, D), F32).at[:B].set(c)
    out = pl.pallas_call(
        _mod_kernel,
        grid=(depth, 6),
        in_specs=[pl.BlockSpec((rows, D), lambda l, j: (0, 0)),
                  pl.BlockSpec((1, D, D), lambda l, j: (l, 0, j)),
                  pl.BlockSpec((1, 1, D), lambda l, j: (l, 0, j))],
        out_specs=pl.BlockSpec((1, rows, D), lambda l, j: (l, 0, j)),
        out_shape=jax.ShapeDtypeStruct((depth, rows, 6 * D), F32),
        compiler_params=_cparams("parallel", "parallel"),
        name="modulation",
    )(c_pad, ada_w, ada_b.reshape(depth, 1, 6 * D))
    return out[:, :B].reshape(depth, B, 6, 1, D)


def _proj_plain_kernel(x_ref, nw_ref, sc_ref, sh_ref, w_ref, o_ref, *, tn):
    h = _norm_mod(x_ref[...], nw_ref[...], sc_ref[...], sh_ref[...]).astype(BF16)
    for n in range(w_ref.shape[0] // tn):
        cols = slice(n * tn, (n + 1) * tn)
        o_ref[:, cols] = _dot_nt(h, w_ref[cols, :]).astype(o_ref.dtype)


def _proj_rope_kernel(x_ref, nw_ref, sc_ref, sh_ref, w_ref, cos_ref, sin_ref, o_ref, *, scales):
    h = _norm_mod(x_ref[...], nw_ref[...], sc_ref[...], sh_ref[...]).astype(BF16)
    y = _dot_nt(h, w_ref[...])
    cos = cos_ref[...]
    sin = sin_ref[...]
    lane = lax.broadcasted_iota(jnp.int32, cos.shape, 1)
    first_half = (lane % HEAD_DIM) < (HEAD_DIM // 2)
    for g, scale in enumerate(scales):
        yg = y[:, g * LANES:(g + 1) * LANES]
        rot = jnp.where(first_half, pltpu.roll(yg, LANES - HEAD_DIM // 2, 1),
                        pltpu.roll(yg, HEAD_DIM // 2, 1))
        r = yg * cos + rot * sin
        if scale != 1.0:
            r = r * scale
        o_ref[:, g * LANES:(g + 1) * LANES] = r.astype(o_ref.dtype)


def _mod_specs(T, tm, sc_idx, sh_idx, nargs):
    per_b = T // tm
    if nargs == 1:
        return [pl.BlockSpec((None, None, 1, D_MODEL), lambda i: (i // per_b, sc_idx, 0, 0)),
                pl.BlockSpec((None, None, 1, D_MODEL), lambda i: (i // per_b, sh_idx, 0, 0))]
    return [pl.BlockSpec((None, None, 1, D_MODEL), lambda i, j: (i // per_b, sc_idx, 0, 0)),
            pl.BlockSpec((None, None, 1, D_MODEL), lambda i, j: (i // per_b, sh_idx, 0, 0))]


def proj_plain(x, mod_l, nw, w, l, T, *, tm=512, tn=512):
    M, D = x.shape
    N = w.shape[1]
    return pl.pallas_call(
        functools.partial(_proj_plain_kernel, tn=tn),
        grid=(M // tm,),
        in_specs=[pl.BlockSpec((tm, D), lambda i: (i, 0)),
                  pl.BlockSpec((1, D), lambda i: (0, 0))]
        + _mod_specs(T, tm, 1, 0, 1)
        + [_layer_spec(w, l)],
        out_specs=pl.BlockSpec((tm, N), lambda i: (i, 0)),
        out_shape=jax.ShapeDtypeStruct((M, N), BF16),
        compiler_params=_cparams("parallel"),
        name="proj_plain",
    )(x, nw, mod_l, mod_l, w)


def proj_rope(x, mod_l, nw, w, l, cos, sin, scales, T, *, tm=512):
    M, D = x.shape
    N = w.shape[1]
    per_b = T // tm
    return pl.pallas_call(
        functools.partial(_proj_rope_kernel, scales=scales),
        grid=(M // tm,),
        in_specs=[pl.BlockSpec((tm, D), lambda i: (i, 0)),
                  pl.BlockSpec((1, D), lambda i: (0, 0))]
        + _mod_specs(T, tm, 1, 0, 1)
        + [_layer_spec(w, l),
           pl.BlockSpec((tm, LANES), lambda i: (i % per_b, 0)),
           pl.BlockSpec((tm, LANES), lambda i: (i % per_b, 0))],
        out_specs=pl.BlockSpec((tm, N), lambda i: (i, 0)),
        out_shape=jax.ShapeDtypeStruct((M, N), BF16),
        compiler_params=_cparams("parallel"),
        name="proj_rope",
    )(x, nw, mod_l, mod_l, w, cos, sin)


def rope_tables(T):
    d = HEAD_DIM
    inv = ROPE_THETA ** (-np.arange(0, d, 2, dtype=np.float64) / d)
    ang = np.arange(T, dtype=np.float64)[:, None] * inv[None, :]
    cos = np.cos(ang)
    sin = np.sin(ang)
    cos_t = np.concatenate([cos, cos, cos, cos], axis=-1)
    sin_t = np.concatenate([-sin, sin, -sin, sin], axis=-1)
    return jnp.asarray(cos_t, F32), jnp.asarray(sin_t, F32)


def split_w_in(w_in):
    sizes = [512, 128, 128, 128, 128, 128, 128, 24, 256, 256, 512, 512, 512, 512, 512, 8, 3072]
    offs = np.cumsum([0] + sizes)
    wb = jnp.swapaxes(w_in, 1, 2).astype(BF16)
    (nq, nkc, nvc, nks, nvs, nkw, nvw, ngate, rq, rk, rv, rg, fq, fk, fv, ff, mg) = [
        wb[:, offs[i]:offs[i + 1], :] for i in range(len(sizes))]
    small = jnp.concatenate([ngate, ff, jnp.zeros((wb.shape[0], LANES - 32, wb.shape[2]), BF16)], axis=1)
    w1 = jnp.concatenate([nq, rq, rk, nkc, nks, nkw], axis=1)
    w2 = jnp.concatenate([mg, rv, rg, fq, fk, fv, nvc, nvs, nvw, small], axis=1)
    assert w1.shape[1] == P1_COLS and w2.shape[1] == P2_COLS
    return w1, w2


def p1_scales():
    s = [1.0] * (P1_COLS // LANES)
    for g in range(P1_NQ // LANES, P1_RQ // LANES):
        s[g] = HEAD_DIM ** -0.5 * LOG2E
    for g in range(P1_RK // LANES, P1_NKC // LANES):
        s[g] = RET_QK_DIM ** -0.5
    return tuple(s)


def _compress_kernel(x_ref, pe_ref, w1_ref, w2_ref, o_ref):
    r = x_ref[...]
    half = r.shape[1]
    w1 = w1_ref[...]
    a = _dot(r, w1[:half])
    b = _dot(r, w1[half:])
    pe = _dot(pe_ref[...], w1)[0:1]
    n = a.shape[0]
    hid = a + pltpu.roll(b, n - 1, 0) + pe
    hid = hid * _sigmoid(hid)
    o_ref[...] = _dot(hid.astype(BF16), w2_ref[...]).astype(o_ref.dtype)


def compress(xr, pe, w1, w2):
    _, B, G, R, W = xr.shape
    H = w1.shape[-1]
    return pl.pallas_call(
        _compress_kernel,
        grid=(2, B, G),
        in_specs=[pl.BlockSpec((None, None, None, R, W), lambda s, b, g: (s, b, g, 0, 0)),
                  pl.BlockSpec((None, 8, 2 * W), lambda s, b, g: (s, 0, 0)),
                  pl.BlockSpec((None, 2 * W, H), lambda s, b, g: (s, 0, 0)),
                  pl.BlockSpec((None, H, HEAD_DIM), lambda s, b, g: (s, 0, 0))],
        out_specs=pl.BlockSpec((None, None, None, R, HEAD_DIM), lambda s, b, g: (s, b, g, 0, 0)),
        out_shape=jax.ShapeDtypeStruct((2, B, G, R, HEAD_DIM), BF16),
        compiler_params=_cparams("parallel", "parallel", "parallel"),
        name="nsa_compress",
    )(xr, pe, w1, w2)


def _stack_heads(q_ref, g):
    tq = q_ref.shape[0]
    half = lax.broadcasted_iota(jnp.int32, (tq, LANES), 1) // HEAD_DIM
    rows = []
    for hh in range(NSA_HPG):
        h = NSA_HPG * g + hh
        x = q_ref[:, (h // 2) * LANES:(h // 2 + 1) * LANES].astype(F32)
        if h % 2 != g:
            x = pltpu.roll(x, HEAD_DIM, 1)
        rows.append(jnp.where(half == g, x, 0.0).astype(BF16))
    return jnp.concatenate(rows, axis=0)


def _store_heads(o_ref, g, o, tq):
    low = lax.broadcasted_iota(jnp.int32, (tq, LANES), 1) < HEAD_DIM
    for pair in range(NSA_HPG // 2):
        even = o[(2 * pair) * tq:(2 * pair + 1) * tq]
        odd = o[(2 * pair + 1) * tq:(2 * pair + 2) * tq]
        if g == 0:
            blk = jnp.where(low, even, pltpu.roll(odd, HEAD_DIM, 1))
        else:
            blk = jnp.where(low, pltpu.roll(even, HEAD_DIM, 1), odd)
        col = (NSA_HPG // 2 * g + pair) * LANES
        o_ref[:, col:col + LANES] = blk.astype(o_ref.dtype)


CMP_CHUNK = 128


def _nsa_cmp_kernel(q_ref, kc_ref, vc_ref, ov_ref, o_ref, m_ref, imp_ref, *, tq, n_sel, top_n):
    t0 = pl.program_id(1) * tq
    ncp = kc_ref.shape[0]
    nsp = ov_ref.shape[0]
    rows = NSA_HPG * tq

    def attend(ncols):
        kc = kc_ref[0:ncols, :]
        vc = vc_ref[0:ncols, :]
        n_idx = lax.broadcasted_iota(jnp.int32, (rows, ncols), 1)
        t_idx = t0 + lax.broadcasted_iota(jnp.int32, (rows, ncols), 0) % tq
        valid = (n_idx * CMP_STRIDE + (CMP_LEN - 1)) <= t_idx
        for g in range(NSA_GROUPS):
            q = _stack_heads(q_ref, g)
            s = jnp.where(valid, _dot_nt(q, kc), NEG_INF)
            m = jnp.max(s, axis=-1, keepdims=True)
            e = jnp.exp2(s - m)
            l = jnp.sum(e, axis=-1, keepdims=True)
            p = e * jnp.where(m > 0.5 * NEG_INF, 1.0 / l, 0.0)
            _store_heads(o_ref, g, _dot(p.astype(BF16), vc), tq)
            psum = p[0:tq]
            for hh in range(1, NSA_HPG):
                psum = psum + p[hh * tq:(hh + 1) * tq]
            imp_ref[g] = _dot_nt(ov_ref[:, 0:ncols], psum.astype(BF16))

    n_live = jnp.maximum((t0 + tq - CMP_LEN) // CMP_STRIDE + 1, 1)
    n_chunks = jnp.minimum((n_live + CMP_CHUNK - 1) // CMP_CHUNK, ncp // CMP_CHUNK)
    for nc in range(1, ncp // CMP_CHUNK + 1):
        pl.when(n_chunks == nc)(functools.partial(attend, nc * CMP_CHUNK))

    j_idx = lax.broadcasted_iota(jnp.int32, (nsp, tq), 0)
    cur = (t0 + lax.broadcasted_iota(jnp.int32, (nsp, tq), 1)) // SEL_LEN
    forced = (j_idx == 0) | (j_idx == cur) | (j_idx == cur - 1)
    j_f = j_idx.astype(F32)
    for g in range(NSA_GROUPS):
        score = jnp.where(j_idx <= cur, imp_ref[g], NEG_INF)
        score = jnp.where(forced | (j_idx >= n_sel), REMOVED, score)
        sel = jnp.where(forced, 1.0, 0.0)
        for _ in range(max(top_n - 3, 0)):
            mx = jnp.max(score, axis=0, keepdims=True)
            idx = jnp.min(jnp.where(score == mx, j_f, float(nsp)), axis=0, keepdims=True)
            hit = j_f == idx
            sel = jnp.where(hit, 1.0, sel)
            score = jnp.where(hit, REMOVED, score)
        sel = jnp.where(j_idx <= cur, sel, 0.0)
        m_ref[g] = sel.T.astype(m_ref.dtype)


SEL_BONUS = 32768.0
NSA_SEL_TQ = 256
NSA_SEL_TK = 1024


def _nsa_sel_kernel(q_ref, k_ref, v_ref, m_ref, et_ref, o_ref, *, tq, tk):
    t0 = pl.program_id(1) * tq
    n_full = t0 // tk
    rows = NSA_HPG * tq

    def update(carry, q, ks, vs, mask=None):
        m, acc = carry
        s = _dot_nt(q, ks)
        if mask is not None:
            s = jnp.where(mask, s, NEG_INF)
        m_new = jnp.maximum(m, jnp.max(s, axis=-1, keepdims=True))
        p = jnp.exp2(s - m_new)
        return m_new, jnp.exp2(m - m_new) * acc + _dot(p.astype(BF16), vs)

    qs, carries = [], []
    for g in range(NSA_GROUPS):
        q = jnp.concatenate([_stack_heads(q_ref, g), jnp.concatenate([m_ref[g]] * NSA_HPG, axis=0)], axis=1)

        def step(j, carry, q=q, g=g):
            start = pl.multiple_of(j * tk, tk)
            ks = jnp.concatenate([k_ref[pl.ds(start, tk), :], et_ref[pl.ds(start, tk), :]], axis=1)
            return update(carry, q, ks, v_ref[g, pl.ds(start, tk), :])

        init = (jnp.full((rows, 1), NEG_INF, F32), jnp.zeros((rows, LANES), F32))
        qs.append(q)
        carries.append(lax.fori_loop(0, n_full, step, init))

    start = pl.multiple_of(n_full * tk, tk)

    def tail(nk):
        trow = t0 + lax.broadcasted_iota(jnp.int32, (rows, nk), 0) % tq
        causal = start + lax.broadcasted_iota(jnp.int32, (rows, nk), 1) <= trow
        ks = jnp.concatenate([k_ref[pl.ds(start, nk), :], et_ref[pl.ds(start, nk), :]], axis=1)
        for g in range(NSA_GROUPS):
            _, acc = update(carries[g], qs[g], ks, v_ref[g, pl.ds(start, nk), :], causal)
            den = HEAD_DIM * (1 - g)
            _store_heads(o_ref, g, acc / acc[:, den:den + 1], tq)

    which = (t0 - start) // tq
    for v in range(tk // tq):
        pl.when(which == v)(functools.partial(tail, (v + 1) * tq))


def nsa_value_augment(v):
    ones = jnp.ones_like(v[..., :HEAD_DIM])
    return jnp.stack([jnp.concatenate([v[..., :HEAD_DIM], ones], axis=-1),
                      jnp.concatenate([ones, v[..., HEAD_DIM:]], axis=-1)], axis=1)


def nsa_selected(p1, v_aug, sel, et_mat, T, *, tq=NSA_SEL_TQ, tk=NSA_SEL_TK):
    B = p1.shape[0]
    nsp = sel.shape[-1]
    return pl.pallas_call(
        functools.partial(_nsa_sel_kernel, tq=tq, tk=tk),
        grid=(B, T // tq),
        in_specs=[pl.BlockSpec((None, tq, NSA_HEADS * HEAD_DIM), lambda b, i: (b, i, 0)),
                  pl.BlockSpec((None, T, LANES), lambda b, i: (b, 0, P1_NKS // LANES)),
                  pl.BlockSpec((None, NSA_GROUPS, T, LANES), lambda b, i: (b, 0, 0, 0)),
                  pl.BlockSpec((None, NSA_GROUPS, tq, nsp), lambda b, i: (b, 0, i, 0)),
                  pl.BlockSpec((T, nsp), lambda b, i: (0, 0))],
        out_specs=pl.BlockSpec((None, tq, NSA_OUT), lambda b, i: (b, i, 0)),
        out_shape=jax.ShapeDtypeStruct((B, T, NSA_OUT), BF16),
        compiler_params=_cparams("parallel", "parallel"),
        name="nsa_selected",
    )(p1, p1, v_aug, sel, et_mat)


def _nsa_win_kernel(q_ref, k_ref, v_ref, b_ref, o_ref, *, tq):
    t0 = pl.program_id(1) * tq
    span = WINDOW + tq
    start = pl.multiple_of(jnp.maximum(t0 - WINDOW, 0), tq)
    ks = k_ref[pl.ds(start, span), :]
    vs = v_ref[pl.ds(start, span), :]

    def run(bias):
        bias = jnp.concatenate([bias] * NSA_HPG, axis=0)
        for g in range(NSA_GROUPS):
            s = _dot_nt(_stack_heads(q_ref, g), ks) + bias
            m = jnp.max(s, axis=-1, keepdims=True)
            p = jnp.exp2(s - m)
            l = jnp.sum(p, axis=-1, keepdims=True)
            _store_heads(o_ref, g, _dot(p.astype(BF16), vs) / l, tq)

    @pl.when(t0 >= WINDOW)
    def _():
        run(b_ref[...])

    @pl.when(t0 < WINDOW)
    def _():
        row = lax.broadcasted_iota(jnp.int32, (tq, span), 0)
        col = lax.broadcasted_iota(jnp.int32, (tq, span), 1)
        run(jnp.where(col <= t0 + row, 0.0, NEG_INF))


def _nsa_cmp_win_kernel(q_ref, kc_ref, vc_ref, ov_ref, kw_ref, vw_ref, band_ref, ocmp_ref, m_ref, owin_ref, imp_ref,
                        *, tq, n_sel, top_n):
    _nsa_cmp_kernel(q_ref, kc_ref, vc_ref, ov_ref, ocmp_ref, m_ref, imp_ref, tq=tq, n_sel=n_sel, top_n=top_n)
    _nsa_win_kernel(q_ref, kw_ref, vw_ref, band_ref, owin_ref, tq=tq)


def nsa_cmp_select_window(p1, p2, kc, vc, ov_t, T):
    B = p1.shape[0]
    tq = NSA_QBLOCK
    ncp = kc.shape[1]
    nsp = ov_t.shape[0]
    n_sel = T // SEL_LEN
    span = WINDOW + tq
    r = np.arange(tq)[:, None]
    c = np.arange(span)[None, :]
    band = jnp.asarray(np.where((c > r) & (c <= r + WINDOW), 0.0, NEG_INF), F32)
    out_blk = pl.BlockSpec((None, tq, NSA_OUT), lambda b, i: (b, i, 0))
    return pl.pallas_call(
        functools.partial(_nsa_cmp_win_kernel, tq=tq, n_sel=n_sel, top_n=min(SEL_TOPN, n_sel)),
        grid=(B, T // tq),
        in_specs=[pl.BlockSpec((None, tq, NSA_HEADS * HEAD_DIM), lambda b, i: (b, i, 0)),
                  pl.BlockSpec((None, ncp, LANES), lambda b, i: (b, 0, 0)),
                  pl.BlockSpec((None, ncp, LANES), lambda b, i: (b, 0, 0)),
                  pl.BlockSpec((nsp, ncp), lambda b, i: (0, 0)),
                  pl.BlockSpec((None, T, LANES), lambda b, i: (b, 0, P1_NKW // LANES)),
                  pl.BlockSpec((None, T, LANES), lambda b, i: (b, 0, P2_NVW // LANES)),
                  pl.BlockSpec((tq, span), lambda b, i: (0, 0))],
        out_specs=[out_blk, pl.BlockSpec((None, NSA_GROUPS, tq, nsp), lambda b, i: (b, 0, i, 0)), out_blk],
        out_shape=[jax.ShapeDtypeStruct((B, T, NSA_OUT), BF16),
                   jax.ShapeDtypeStruct((B, NSA_GROUPS, T, nsp), BF16),
                   jax.ShapeDtypeStruct((B, T, NSA_OUT), BF16)],
        scratch_shapes=[pltpu.VMEM((NSA_GROUPS, nsp, tq), F32)],
        compiler_params=_cparams("parallel", "parallel"),
        name="nsa_cmp_select_window",
    )(p1, kc, vc, ov_t, p1, p2, band)


def _retention_kernel(q_ref, k_ref, v_ref, g_ref, din_ref, qd_ref, kd_ref, cd_ref, o_ref, st_ref):
    @pl.when(pl.program_id(0) == 0)
    def _():
        st_ref[...] = jnp.zeros_like(st_ref)

    B = q_ref.shape[0]
    C = RET_CHUNK
    half = lax.broadcasted_iota(jnp.int32, (C, LANES), 1) // HEAD_DIM
    for b in range(B):
        for h in range(RET_HEADS):
            lanes = slice(h * LANES, (h + 1) * LANES)
            pair = slice((h // 2) * LANES, (h // 2 + 1) * LANES)
            st = st_ref[b, h]
            for sub in range(q_ref.shape[1] // C):
                rows = slice(sub * C, (sub + 1) * C)
                qh = jnp.where(half == h % 2, q_ref[b, rows, pair], 0.0).astype(BF16)
                kp = k_ref[b, rows, pair]
                vh = v_ref[b, rows, lanes]
                inner = _dot_nt(qh, kp) * din_ref[h]
                o = _dot(inner.astype(BF16), vh) + _dot(qh, st.astype(BF16)) * qd_ref[h]
                kd = (kp.astype(F32) * kd_ref[h]).astype(BF16)
                st = st * cd_ref[h, 0:1, :] + _dot_tn(kd, vh)
                mu = jnp.mean(o, axis=-1, keepdims=True)
                d = o - mu
                var = jnp.mean(d * d, axis=-1, keepdims=True)
                on = d * lax.rsqrt(var + NORM_EPS)
                gh = g_ref[b, rows, lanes].astype(F32)
                o_ref[b, rows, lanes] = (gh * _sigmoid(gh) * on).astype(o_ref.dtype)
            st_ref[b, h] = st


def retention_consts():
    C = RET_CHUNK
    H = RET_HEADS
    log_g = np.log(1.0 - 2.0 ** (-5.0 - np.arange(H, dtype=np.float64)))
    n = np.arange(C, dtype=np.float64)
    diff = n[:, None] - n[None, :]
    causal = diff >= 0
    decay_in = np.where(causal[None], np.exp(np.where(causal, diff, 0.0)[None] * log_g[:, None, None]), 0.0)
    q_decay = np.exp((n[None, :] + 1.0) * log_g[:, None])
    k_decay = np.exp((C - 1.0 - n)[None, :] * log_g[:, None])
    chunk_decay = np.exp(C * log_g)
    qd = np.broadcast_to(q_decay[:, :, None], (H, C, LANES))
    kd = np.broadcast_to(k_decay[:, :, None], (H, C, LANES))
    cd = np.broadcast_to(chunk_decay[:, None, None], (H, 8, LANES))
    return tuple(jnp.asarray(a, F32) for a in (decay_in, qd, kd, cd))


RET_STEP = 4


def retention(p1, p2, consts, T):
    B = p1.shape[0]
    C = RET_CHUNK * RET_STEP
    din, qd, kd, cd = consts
    W = RET_HEADS * LANES
    full = lambda shape: pl.BlockSpec(shape, lambda c: (0,) * len(shape))
    return pl.pallas_call(
        _retention_kernel,
        grid=(T // C,),
        in_specs=[pl.BlockSpec((B, C, W // 2), lambda c: (0, c, P1_RQ // (W // 2))),
                  pl.BlockSpec((B, C, W // 2), lambda c: (0, c, P1_RK // (W // 2))),
                  pl.BlockSpec((B, C, W), lambda c: (0, c, P2_RV // W)),
                  pl.BlockSpec((B, C, W), lambda c: (0, c, P2_RG // W)),
                  full(din.shape), full(qd.shape), full(kd.shape), full(cd.shape)],
        out_specs=pl.BlockSpec((B, C, W), lambda c: (0, c, 0)),
        out_shape=jax.ShapeDtypeStruct((B, T, W), BF16),
        scratch_shapes=[pltpu.VMEM((B, RET_HEADS, LANES, LANES), F32)],
        compiler_params=_cparams("arbitrary"),
        name="retention",
    )(p1, p1, p2, p2, din, qd, kd, cd)


def _fox_cum_kernel(f_ref, b_ref, o_ref):
    x = f_ref[...] + b_ref[...]
    ls = jnp.minimum(x, 0.0) - jnp.log1p(jnp.exp(-jnp.abs(x)))
    R = x.shape[0]
    ki = lax.broadcasted_iota(jnp.int32, (LANES, LANES), 0)
    ji = lax.broadcasted_iota(jnp.int32, (LANES, LANES), 1)
    upper = jnp.where(ki <= ji, 1.0, 0.0).astype(BF16)
    hi, mid, lo = _split3(ls)
    rowcum = _dot(hi, upper) + _dot(mid, upper) + _dot(lo, upper)
    tot = jnp.broadcast_to(rowcum[:, LANES - 1:LANES], (R, LANES))
    ri = lax.broadcasted_iota(jnp.int32, (R, R), 0)
    ci = lax.broadcasted_iota(jnp.int32, (R, R), 1)
    lower = jnp.where(ci < ri, 1.0, 0.0).astype(BF16)
    hi, mid, lo = _split3(tot)
    offs = _dot(lower, hi) + _dot(lower, mid) + _dot(lower, lo)
    o_ref[...] = (rowcum + offs) * LOG2E


def fox_cum(f_logit, bias):
    B, H, R, _ = f_logit.shape
    return pl.pallas_call(
        _fox_cum_kernel,
        grid=(B, H),
        in_specs=[pl.BlockSpec((None, None, R, LANES), lambda b, h: (b, h, 0, 0)),
                  pl.BlockSpec((None, 1, LANES), lambda b, h: (h, 0, 0))],
        out_specs=pl.BlockSpec((None, None, R, LANES), lambda b, h: (b, h, 0, 0)),
        out_shape=jax.ShapeDtypeStruct((B, H, R, LANES), F32),
        compiler_params=_cparams("parallel", "parallel"),
        name="fox_cum",
    )(f_logit, bias)


FOX_BIAS_LANES = 3


def _fox_kernel(q_ref, k_ref, v_ref, c_ref, o_ref, ka_ref, va_ref, *, tq):
    i = pl.program_id(2)
    tk = tq
    T = k_ref.shape[0]
    chunk = 512

    @pl.when(i == 0)
    def _():
        lane = lax.broadcasted_iota(jnp.int32, (chunk, LANES), 1)
        ri = lax.broadcasted_iota(jnp.int32, (16, LANES), 0)
        ci = lax.broadcasted_iota(jnp.int32, (16, LANES), 1)
        place = jnp.where((ci == ri + HEAD_DIM) & (ri < FOX_BIAS_LANES), 1.0, 0.0).astype(BF16)

        def build(c, _):
            c0 = pl.multiple_of(c * chunk, chunk)
            kp = k_ref[pl.ds(c0, chunk), :].astype(F32)
            vp = v_ref[pl.ds(c0, chunk), :].astype(F32)
            for hh in range(2):
                hi, mid, lo = _split3(-c_ref[hh, :, pl.ds(c0, chunk)])
                terms = jnp.concatenate([hi, mid, lo, jnp.zeros((13, chunk), BF16)], axis=0)
                bias = _dot_tn(terms, place)
                kh = kp if hh == 0 else pltpu.roll(kp, HEAD_DIM, 1)
                vh = vp if hh == 0 else pltpu.roll(vp, HEAD_DIM, 1)
                ka_ref[hh, pl.ds(c0, chunk), :] = jnp.where(lane < HEAD_DIM, kh, bias).astype(BF16)
                va_ref[hh, pl.ds(c0, chunk), :] = jnp.where(lane < HEAD_DIM, vh, 1.0).astype(BF16)
            return 0

        lax.fori_loop(0, T // chunk, build, 0)

    lane = lax.broadcasted_iota(jnp.int32, (tq, LANES), 1)
    ones_lanes = (lane >= HEAD_DIM) & (lane < HEAD_DIM + FOX_BIAS_LANES)
    qp = q_ref[...].astype(F32) * (HEAD_DIM ** -0.5 * LOG2E)
    qs = [jnp.where(lane < HEAD_DIM, qh, jnp.where(ones_lanes, 1.0, 0.0)).astype(BF16)
          for qh in (qp, pltpu.roll(qp, HEAD_DIM, 1))]

    def update(hh, m, acc, q, start, size, mask=None):
        s = _dot_nt(q, ka_ref[hh, pl.ds(start, size), :])
        if mask is not None:
            s = jnp.where(mask, s, NEG_INF)
        m_new = jnp.maximum(m, jnp.max(s, axis=-1, keepdims=True))
        p = jnp.exp2(s - m_new)
        return m_new, jnp.exp2(m - m_new) * acc + _dot(p.astype(BF16), va_ref[hh, pl.ds(start, size), :])

    def step(j, carry):
        start = pl.multiple_of(j * tk, tk)
        return tuple(update(hh, *carry[hh], qs[hh], start, tk) for hh in range(2))

    one = (jnp.full((tq, 1), NEG_INF, F32), jnp.zeros((tq, LANES), F32))
    carry = lax.fori_loop(0, i, step, (one, one))

    half = tq // 2
    start = pl.multiple_of(i * tk, tk)
    row = lax.broadcasted_iota(jnp.int32, (tq, half), 0)
    col = lax.broadcasted_iota(jnp.int32, (tq, half), 1)
    accs = []
    for hh in range(2):
        m, acc = update(hh, *carry[hh], qs[hh], start, half, col <= row)
        _, low = update(hh, m[half:], acc[half:], qs[hh][half:], start + half, half, (col <= row)[:half])
        accs.append(jnp.concatenate([acc[:half], low], axis=0))
    acc0, acc1 = accs
    o0 = acc0 / acc0[:, HEAD_DIM:HEAD_DIM + 1]
    o1 = acc1 / acc1[:, HEAD_DIM:HEAD_DIM + 1]
    o_ref[...] = jnp.where(lane < HEAD_DIM, o0, pltpu.roll(o1, HEAD_DIM, 1)).astype(o_ref.dtype)


def fox_attention(p2, cum, T, *, tq=FOX_TQ):
    B = p2.shape[0]
    HP = FOX_HEADS // 2
    return pl.pallas_call(
        functools.partial(_fox_kernel, tq=tq),
        grid=(B, HP, T // tq),
        in_specs=[pl.BlockSpec((None, tq, LANES), lambda b, h, i: (b, i, P2_FQ // LANES + h)),
                  pl.BlockSpec((None, T, LANES), lambda b, h, i: (b, 0, P2_FK // LANES + h)),
                  pl.BlockSpec((None, T, LANES), lambda b, h, i: (b, 0, P2_FV // LANES + h)),
                  pl.BlockSpec((None, None, 2, 1, T), lambda b, h, i: (b, h, 0, 0, 0))],
        out_specs=pl.BlockSpec((None, tq, LANES), lambda b, h, i: (b, i, h)),
        out_shape=jax.ShapeDtypeStruct((B, T, FOX_HEADS * HEAD_DIM), BF16),
        scratch_shapes=[pltpu.VMEM((2, T, LANES), BF16), pltpu.VMEM((2, T, LANES), BF16)],
        compiler_params=_cparams("parallel", "parallel", "arbitrary"),
        name="fox_attention",
    )(p2, p2, p2, cum)


def _readout_kernel(ocmp_ref, osel_ref, owin_ref, small_ref, oret_ref, ofox_ref, mg_ref, x_ref, g1_ref,
                    ex_ref, wn_ref, wr_ref, wf_ref, wo_ref, o_ref):
    W = NSA_OUT
    gs = _sigmoid(small_ref[...].astype(F32)).astype(BF16)
    ge = _dot(gs, ex_ref[...])
    onsa = (ge[:, :W] * ocmp_ref[...].astype(F32) + ge[:, W:2 * W] * osel_ref[...].astype(F32)
            + ge[:, 2 * W:] * owin_ref[...].astype(F32))
    D = D_MODEL
    merged = (_sigmoid(mg_ref[:, :D].astype(F32)) * _dot(onsa.astype(BF16), wn_ref[...])
              + _sigmoid(mg_ref[:, D:2 * D].astype(F32)) * _dot(oret_ref[...], wr_ref[...])
              + _sigmoid(mg_ref[:, 2 * D:].astype(F32)) * _dot(ofox_ref[...], wf_ref[...]))
    y = _dot(merged.astype(BF16), wo_ref[...])
    o_ref[...] = x_ref[...] + g1_ref[...] * y


def readout(o_cmp, o_sel, o_win, p2, o_ret, o_fox, x, mod_l, ex, wn, wr, wf, wo, l, T, *, tm=512):
    M, D = x.shape
    per_b = T // tm
    W = NSA_OUT
    row = lambda width, col=0: pl.BlockSpec((tm, width), lambda i: (i, col))
    full = lambda a: pl.BlockSpec(a.shape, lambda i: (0,) * a.ndim)
    return pl.pallas_call(
        _readout_kernel,
        grid=(M // tm,),
        in_specs=[row(W), row(W), row(W), row(LANES, P2_SMALL // LANES), row(512), row(512),
                  row(3 * D, 0), row(D),
                  pl.BlockSpec((None, None, 1, D), lambda i: (i // per_b, 2, 0, 0)),
                  full(ex), _layer_spec(wn, l), _layer_spec(wr, l), _layer_spec(wf, l), _layer_spec(wo, l)],
        out_specs=row(D),
        out_shape=jax.ShapeDtypeStruct((M, D), F32),
        compiler_params=_cparams("parallel"),
        name="mixer_readout",
    )(o_cmp, o_sel, o_win, p2, o_ret, o_fox, p2, x, mod_l, ex, wn, wr, wf, wo)


def nsa_gate_expand():
    ex = np.zeros((LANES, 3 * NSA_OUT), np.float32)
    for br in range(3):
        for h in range(NSA_HEADS):
            c0 = br * NSA_OUT + h * HEAD_DIM
            ex[br * NSA_HEADS + h, c0:c0 + HEAD_DIM] = 1.0
    return jnp.asarray(ex, BF16)


FFN_CHUNK = 512


def _ffn_kernel(x_ref, nw_ref, sc_ref, sh_ref, g2_ref, w1_ref, w3_ref, w2_ref, o_ref):
    x = x_ref[...]
    h = _norm_mod(x, nw_ref[...], sc_ref[...], sh_ref[...]).astype(BF16)
    F = w1_ref.shape[1]
    y = None
    for c0 in range(0, F, FFN_CHUNK):
        cols = slice(c0, min(c0 + FFN_CHUNK, F))
        u = _dot(h, w1_ref[:, cols])
        v = _dot(h, w3_ref[:, cols])
        part = _dot((u * _sigmoid(u) * v).astype(BF16), w2_ref[cols, :])
        y = part if y is None else y + part
    o_ref[...] = x + g2_ref[...] * y


def ffn(x, mod_l, nw, w1, w3, w2, T, *, tm=512):
    M, D = x.shape
    F = w1.shape[1]
    per_b = T // tm
    modspec = lambda k: pl.BlockSpec((None, None, 1, D), lambda i: (i // per_b, k, 0, 0))
    full = lambda a: pl.BlockSpec(a.shape, lambda i: (0,) * a.ndim)
    return pl.pallas_call(
        _ffn_kernel,
        grid=(M // tm,),
        in_specs=[pl.BlockSpec((tm, D), lambda i: (i, 0)),
                  pl.BlockSpec((1, D), lambda i: (0, 0)),
                  modspec(4), modspec(3), modspec(5), full(w1), full(w3), full(w2)],
        out_specs=pl.BlockSpec((tm, D), lambda i: (i, 0)),
        out_shape=jax.ShapeDtypeStruct((M, D), F32),
        compiler_params=_cparams("parallel"),
        name="ffn_dense",
    )(x, nw, mod_l, mod_l, mod_l, w1, w3, w2)


MOE_TC = 512
MOE_TS = 512


def _router_kernel(x_ref, nw_ref, sc_ref, sh_ref, wh_ref, wl_ref, h_ref, gate_ref, rank_ref, cnt_ref, carry_ref):
    @pl.when(pl.program_id(0) == 0)
    def _():
        carry_ref[...] = jnp.zeros_like(carry_ref)

    h = _norm_mod(x_ref[...], nw_ref[...], sc_ref[...], sh_ref[...])
    hh = h.astype(BF16)
    h_ref[...] = _pack_bf16_pairs(hh.astype(F32))
    hl = (h - hh.astype(F32)).astype(BF16)
    logits = _dot(hh, wh_ref[...]) + (_dot(hl, wh_ref[...]) + _dot(hh, wl_ref[...]))
    tm = logits.shape[0]
    lane = lax.broadcasted_iota(jnp.int32, logits.shape, 1)
    logits = jnp.where(lane < N_EXPERTS, logits, REMOVED)
    lane_f = lane.astype(F32)
    v1 = jnp.max(logits, axis=-1, keepdims=True)
    i1 = jnp.min(jnp.where(logits == v1, lane_f, float(LANES)), axis=-1, keepdims=True)
    rest = jnp.where(lane_f == i1, REMOVED, logits)
    v2 = jnp.max(rest, axis=-1, keepdims=True)
    i2 = jnp.min(jnp.where(rest == v2, lane_f, float(LANES)), axis=-1, keepdims=True)
    e2 = jnp.exp(v2 - v1)
    w1 = 1.0 / (1.0 + e2)
    w2 = e2 / (1.0 + e2)
    gate_ref[...] = jnp.where(lane_f == i1, w1, jnp.where(lane_f == i2, w2, 0.0))

    sel = jnp.where((lane_f == i1) | (lane_f == i2), 1.0, 0.0)
    ri = lax.broadcasted_iota(jnp.int32, (tm, tm), 0)
    ci = lax.broadcasted_iota(jnp.int32, (tm, tm), 1)
    before = jnp.where(ci < ri, 1.0, 0.0).astype(BF16)
    rank = _dot(before, sel.astype(BF16)) + carry_ref[0:1, :]
    rank_ref[...] = jnp.where(sel > 0.0, rank, -1.0)
    carry_ref[...] = carry_ref[...] + jnp.sum(sel, axis=0, keepdims=True)
    cnt_ref[...] = carry_ref[...]


def router(x, mod_l, nw, w_router, T):
    M, D = x.shape
    tm = MOE_TC
    per_b = T // tm
    wp = jnp.zeros((D, LANES), F32).at[:, :N_EXPERTS].set(w_router)
    wh = wp.astype(BF16)
    wl = (wp - wh.astype(F32)).astype(BF16)
    return pl.pallas_call(
        _router_kernel,
        grid=(M // tm,),
        in_specs=[pl.BlockSpec((tm, D), lambda i: (i, 0)),
                  pl.BlockSpec((1, D), lambda i: (0, 0))]
        + _mod_specs(T, tm, 4, 3, 1)
        + [pl.BlockSpec((D, LANES), lambda i: (0, 0)),
           pl.BlockSpec((D, LANES), lambda i: (0, 0))],
        out_specs=[pl.BlockSpec((tm, D // 2), lambda i: (i, 0)),
                   pl.BlockSpec((tm, LANES), lambda i: (i, 0)),
                   pl.BlockSpec((tm, LANES), lambda i: (i, 0)),
                   pl.BlockSpec((8, LANES), lambda i: (0, 0))],
        out_shape=[jax.ShapeDtypeStruct((M, D // 2), jnp.uint32),
                   jax.ShapeDtypeStruct((M, LANES), F32),
                   jax.ShapeDtypeStruct((M, LANES), F32),
                   jax.ShapeDtypeStruct((8, LANES), F32)],
        scratch_shapes=[pltpu.VMEM((8, LANES), F32)],
        compiler_params=_cparams("arbitrary"),
        name="moe_router",
    )(x, nw, mod_l, mod_l, wh, wl)


def _count_le(sorted_vals, x):
    return jnp.sum(sorted_vals[None, :] <= x[:, None], axis=1, dtype=jnp.int32)


def _moe_up_kernel(e_r, total, x_ref, w1_ref, w3_ref, o_ref, w1b_ref, w3b_ref):
    r = pl.program_id(1)
    live = r < total[0]

    @pl.when(live & ((r == 0) | (e_r[r] != e_r[jnp.maximum(r - 1, 0)])))
    def _():
        w1b_ref[...] = w1_ref[...].astype(BF16)
        w3b_ref[...] = w3_ref[...].astype(BF16)

    @pl.when(live)
    def _():
        x = _unpack_bf16_pairs(x_ref[...]).astype(BF16)
        u = _dot(x, w1b_ref[...])
        v = _dot(x, w3b_ref[...])
        o_ref[...] = (u * _sigmoid(u) * v).astype(o_ref.dtype)


def moe_up(xs, w1, w3, tiles, rt, *, tf=1792):
    R = xs.shape[0]
    D = w1.shape[1]
    ts = MOE_TS
    F = w1.shape[-1]
    live = lambda r, total: jnp.minimum(r, total[0] - 1)
    return pl.pallas_call(
        _moe_up_kernel,
        grid_spec=pltpu.PrefetchScalarGridSpec(
            num_scalar_prefetch=2,
            grid=(F // tf, rt),
            in_specs=[pl.BlockSpec((ts, D // 2), lambda n, r, e, total: (live(r, total), 0)),
                      pl.BlockSpec((None, D, tf), lambda n, r, e, total: (e[live(r, total)], 0, n)),
                      pl.BlockSpec((None, D, tf), lambda n, r, e, total: (e[live(r, total)], 0, n))],
            out_specs=pl.BlockSpec((ts, tf), lambda n, r, e, total: (r, n)),
            scratch_shapes=[pltpu.VMEM((D, tf), BF16), pltpu.VMEM((D, tf), BF16)],
        ),
        out_shape=jax.ShapeDtypeStruct((R, F), BF16),
        compiler_params=_cparams("arbitrary", "arbitrary"),
        name="moe_up",
    )(tiles["e"], tiles["total"], xs, w1, w3)


def _moe_down_kernel(e_r, total, a_ref, w2_ref, o_ref, w2b_ref):
    r = pl.program_id(0)
    live = r < total[0]

    @pl.when(live & ((r == 0) | (e_r[r] != e_r[jnp.maximum(r - 1, 0)])))
    def _():
        w2b_ref[...] = w2_ref[...].astype(BF16)

    @pl.when(live)
    def _():
        o_ref[...] = _pack_bf16_pairs(_dot(a_ref[...], w2b_ref[...]))


def moe_down(a, w2, tiles, rt):
    R, F = a.shape
    ts = MOE_TS
    D = w2.shape[-1]
    live = lambda r, total: jnp.minimum(r, total[0] - 1)
    return pl.pallas_call(
        _moe_down_kernel,
        grid_spec=pltpu.PrefetchScalarGridSpec(
            num_scalar_prefetch=2,
            grid=(rt,),
            in_specs=[pl.BlockSpec((ts, F), lambda r, e, total: (live(r, total), 0)),
                      pl.BlockSpec((None, F, D), lambda r, e, total: (e[live(r, total)], 0, 0))],
            out_specs=pl.BlockSpec((ts, D // 2), lambda r, e, total: (r, 0)),
            scratch_shapes=[pltpu.VMEM((F, D), BF16)],
        ),
        out_shape=jax.ShapeDtypeStruct((R, D // 2), jnp.uint32),
        compiler_params=_cparams("arbitrary"),
        name="moe_down",
    )(tiles["e"], tiles["total"], a, w2)


SC_WINDOW = 64


def _sc_mesh():
    return plsc.VectorSubcoreMesh(core_axis_name="core", subcore_axis_name="subcore")


def sc_scatter_rows2(x, idx_a, idx_b, n_out):
    n, d = x.shape
    steps = n // SC_WINDOW

    @pl.kernel(out_type=jax.ShapeDtypeStruct((n_out, d), x.dtype), mesh=_sc_mesh(), scratch_types=[])
    def kern(x_hbm, ia_hbm, ib_hbm, o_hbm):
        def body(x_vmem, ia_vmem, ib_vmem):
            pltpu.sync_copy(x_vmem, o_hbm.at[ia_vmem.at[0]])
            pltpu.sync_copy(x_vmem, o_hbm.at[ib_vmem.at[0]])

        pltpu.emit_pipeline(
            body,
            grid=(steps,),
            in_specs=[pl.BlockSpec((SC_WINDOW, d), index_map=lambda i: (i, 0)),
                      pl.BlockSpec((1, SC_WINDOW), index_map=lambda i: (i, 0)),
                      pl.BlockSpec((1, SC_WINDOW), index_map=lambda i: (i, 0))],
            out_specs=[],
            core_axis_name=("core", "subcore"),
            dimension_semantics=(pltpu.PARALLEL,),
        )(x_hbm, ia_hbm, ib_hbm)

    return kern(x, idx_a.reshape(steps, SC_WINDOW), idx_b.reshape(steps, SC_WINDOW))


def sc_gather_rows(x, idx):
    n = idx.shape[0]
    d = x.shape[1]
    steps = n // SC_WINDOW

    @pl.kernel(out_type=jax.ShapeDtypeStruct((n, d), x.dtype), mesh=_sc_mesh(), scratch_types=[])
    def kern(x_hbm, i_hbm, o_hbm):
        def body(i_vmem, o_vmem):
            pltpu.sync_copy(x_hbm.at[i_vmem.at[0]], o_vmem)

        pltpu.emit_pipeline(
            body,
            grid=(steps,),
            in_specs=[pl.BlockSpec((1, SC_WINDOW), index_map=lambda i: (i, 0))],
            out_specs=[pl.BlockSpec((SC_WINDOW, d), index_map=lambda i: (i, 0))],
            core_axis_name=("core", "subcore"),
            dimension_semantics=(pltpu.PARALLEL,),
        )(i_hbm, o_hbm)

    return kern(x, idx.reshape(steps, SC_WINDOW))


def _moe_finish_kernel(x_ref, g2_ref, ya_ref, yb_ref, gate_ref, rank_ref, nw_ref, o_ref, *, normalize):
    gate = gate_ref[...]
    chosen = rank_ref[...] >= 0.0
    lane = lax.broadcasted_iota(jnp.int32, gate.shape, 1).astype(F32)
    first = jnp.min(jnp.where(chosen, lane, float(LANES)), axis=-1, keepdims=True)
    last = jnp.max(jnp.where(chosen, lane, -1.0), axis=-1, keepdims=True)
    wa = jnp.sum(jnp.where(lane == first, gate, 0.0), axis=-1, keepdims=True)
    wb = jnp.sum(jnp.where(lane == last, gate, 0.0), axis=-1, keepdims=True)
    x = x_ref[...] + g2_ref[...] * (wa * _unpack_bf16_pairs(ya_ref[...]) + wb * _unpack_bf16_pairs(yb_ref[...]))
    if normalize:
        ms = jnp.mean(x * x, axis=-1, keepdims=True)
        x = x * lax.rsqrt(ms + NORM_EPS) * nw_ref[...]
    o_ref[...] = x


def moe_finish(x, mod_l, y2, gate, rank, norm_w, T, *, tm=512):
    M, D = x.shape
    per_b = T // tm
    normalize = norm_w is not None
    if norm_w is None:
        norm_w = jnp.ones((1, D), F32)
    return pl.pallas_call(
        functools.partial(_moe_finish_kernel, normalize=normalize),
        grid=(M // tm,),
        in_specs=[pl.BlockSpec((tm, D), lambda i: (i, 0)),
                  pl.BlockSpec((None, None, 1, D), lambda i: (i // per_b, 5, 0, 0)),
                  pl.BlockSpec((None, tm, D // 2), lambda i: (0, i, 0)),
                  pl.BlockSpec((None, tm, D // 2), lambda i: (1, i, 0)),
                  pl.BlockSpec((tm, LANES), lambda i: (i, 0)),
                  pl.BlockSpec((tm, LANES), lambda i: (i, 0)),
                  pl.BlockSpec((1, D), lambda i: (0, 0))],
        out_specs=pl.BlockSpec((tm, D), lambda i: (i, 0)),
        out_shape=jax.ShapeDtypeStruct((M, D), F32),
        compiler_params=_cparams("parallel"),
        name="moe_finish",
    )(x, mod_l, y2, y2, gate, rank, norm_w)


def moe_ffn(x, mod_l, nw, w_router, w1, w3, w2, T, norm_w=None):
    M = x.shape[0]
    ts = MOE_TS
    rt = (2 * M) // ts + N_EXPERTS
    h, gate, rank, cnt = router(x, mod_l, nw, w_router, T)
    i32 = jnp.int32
    counts = cnt[0, :N_EXPERTS].astype(i32)
    ntile = (counts + ts - 1) // ts
    tile_end = jnp.cumsum(ntile)
    row_off = (tile_end - ntile) * ts
    e_r = jnp.minimum(_count_le(tile_end, jnp.arange(rt, dtype=i32)), N_EXPERTS - 1)
    tiles = dict(e=e_r, total=tile_end[-1].reshape(1).astype(i32))
    rk = rank[:, :N_EXPERTS].astype(i32)
    pos = row_off[None, :] + rk
    pos_a = jnp.min(jnp.where(rk >= 0, pos, rt * ts), axis=1)
    pos_b = jnp.max(jnp.where(rk >= 0, pos, -1), axis=1)

    xs = sc_scatter_rows2(h, pos_a, pos_b, rt * ts)
    a = moe_up(xs, w1, w3, tiles, rt)
    y = moe_down(a, w2, tiles, rt)
    y2 = sc_gather_rows(y, jnp.concatenate([pos_a, pos_b])).reshape(2, M, -1)
    return moe_finish(x, mod_l, y2, gate, rank, norm_w, T)


def _final_norm_kernel(x_ref, w_ref, o_ref):
    x = x_ref[...]
    ms = jnp.mean(x * x, axis=-1, keepdims=True)
    o_ref[...] = x * lax.rsqrt(ms + NORM_EPS) * w_ref[...]


def final_norm(x, w, *, tm=1024):
    M, D = x.shape
    return pl.pallas_call(
        _final_norm_kernel,
        grid=(M // tm,),
        in_specs=[pl.BlockSpec((tm, D), lambda i: (i, 0)), pl.BlockSpec((1, D), lambda i: (0, 0))],
        out_specs=pl.BlockSpec((tm, D), lambda i: (i, 0)),
        out_shape=jax.ShapeDtypeStruct((M, D), F32),
        compiler_params=_cparams("parallel"),
        name="final_norm",
    )(x, w)


def nsa_constants(T):
    n_sel = T // SEL_LEN
    nsp = max(LANES, n_sel)
    ncp = T // CMP_STRIDE
    cmp_start = np.arange(ncp) * CMP_STRIDE
    sel_start = np.arange(nsp) * SEL_LEN
    ov = ((cmp_start[:, None] < sel_start[None, :] + SEL_LEN)
          & (cmp_start[:, None] + CMP_LEN > sel_start[None, :]))
    ov[(T - CMP_LEN) // CMP_STRIDE + 1:] = False
    ov[:, n_sel:] = False
    et_mat = ((np.arange(T)[:, None] // SEL_LEN) == np.arange(nsp)[None, :]) * SEL_BONUS
    return jnp.asarray(ov.T, BF16), jnp.asarray(et_mat, BF16)


def token_mixing(x, mod_l, lw, consts, B, T):
    M = B * T
    cos_t, sin_t, ov_t, e_mat, ret_consts, ex = consts
    l = lw["layer"]
    p1 = proj_rope(x, mod_l, lw["norm_mix"], lw["w1"], l, cos_t, sin_t, p1_scales(), T).reshape(B, T, P1_COLS)
    p2 = proj_plain(x, mod_l, lw["norm_mix"], lw["w2"], l, T).reshape(B, T, P2_COLS)

    def group_rows(a):
        return a.reshape(B, T, NSA_GROUPS, HEAD_DIM).transpose(0, 2, 1, 3).reshape(
            B, NSA_GROUPS, T // CMP_STRIDE, CMP_STRIDE * HEAD_DIM)

    xr = jnp.stack([group_rows(p1[:, :, P1_NKC:P1_NKC + LANES]), group_rows(p2[:, :, P2_NVC:P2_NVC + LANES])])
    cmp_out = compress(xr, lw["cmp_pe"], lw["cmp_w1"], lw["cmp_w2"])
    cmp_out = cmp_out.transpose(0, 1, 3, 2, 4).reshape(2, B, T // CMP_STRIDE, LANES)
    o_cmp, sel, o_win = nsa_cmp_select_window(p1, p2, cmp_out[0], cmp_out[1], ov_t, T)
    o_sel = nsa_selected(p1, nsa_value_augment(p2[:, :, P2_NVS:P2_NVS + LANES]), sel, e_mat, T)

    o_ret = retention(p1, p2, ret_consts, T)

    ff = p2[:, :, P2_SMALL + 3 * NSA_HEADS:P2_SMALL + 3 * NSA_HEADS + FOX_HEADS].astype(F32)
    ff = ff.transpose(0, 2, 1).reshape(B, FOX_HEADS, T // LANES, LANES)
    cum = fox_cum(ff, lw["fox_bias"]).reshape(B, FOX_HEADS // 2, 2, 1, T)
    o_fox = fox_attention(p2, cum, T)

    return readout(o_cmp.reshape(M, -1), o_sel.reshape(M, -1), o_win.reshape(M, -1), p2.reshape(M, P2_COLS),
                   o_ret.reshape(M, -1), o_fox.reshape(M, -1), x, mod_l, ex,
                   lw["wn"], lw["wr"], lw["wf"], lw["wo"], l, T)


def mixer_weights(norm_mix, w_in, cmp_k_pe, cmp_k_w1, cmp_k_w2, cmp_v_pe, cmp_v_w1, cmp_v_w2, fox_f_bias,
                  w_read_nsa, w_read_ret, w_read_fox, w_out):
    depth = w_in.shape[0]
    w1, w2 = split_w_in(w_in)
    pe = jnp.stack([cmp_k_pe.reshape(depth, 1, -1), cmp_v_pe.reshape(depth, 1, -1)], axis=1)
    pe = jnp.broadcast_to(pe, (depth, 2, 8, pe.shape[-1])).astype(BF16)
    shared = {
        "w1": w1, "w2": w2,
        "wn": w_read_nsa.astype(BF16),
        "wr": w_read_ret.astype(BF16),
        "wf": w_read_fox.astype(BF16),
        "wo": w_out.astype(BF16),
    }
    cmp_w1 = jnp.stack([cmp_k_w1, cmp_v_w1], axis=1).astype(BF16)
    cmp_w2 = jnp.stack([cmp_k_w2, cmp_v_w2], axis=1).astype(BF16)
    return [dict(shared, layer=l, norm_mix=norm_mix[l].reshape(1, -1), cmp_pe=pe[l], cmp_w1=cmp_w1[l], cmp_w2=cmp_w2[l],
                 fox_bias=jnp.broadcast_to(fox_f_bias[l][:, None, None], (FOX_HEADS, 1, LANES)))
            for l in range(depth)]


def kernel(x, c, ada_w, ada_b, norm_mix, norm_ffn, w_in, cmp_k_pe, cmp_k_w1, cmp_k_w2, cmp_v_pe, cmp_v_w1,
           cmp_v_w2, fox_f_bias, w_read_nsa, w_read_ret, w_read_fox, w_out, ffn_w1, ffn_w3, ffn_w2, router_w,
           moe_w1, moe_w3, moe_w2, final_norm_w):
    B, T, D = x.shape
    M = B * T
    depth = ada_w.shape[0]
    mod = modulation(c, ada_w, ada_b)
    cos_t, sin_t = rope_tables(T)
    ov_t, e_mat = nsa_constants(T)
    consts = (cos_t, sin_t, ov_t, e_mat, retention_consts(), nsa_gate_expand())
    xs = x.reshape(M, D)
    lws = mixer_weights(norm_mix, w_in, cmp_k_pe, cmp_k_w1, cmp_k_w2, cmp_v_pe, cmp_v_w1, cmp_v_w2,
                        fox_f_bias, w_read_nsa, w_read_ret, w_read_fox, w_out)
    for l in range(depth):
        xs = token_mixing(xs, mod[l], lws[l], consts, B, T)
        nf = norm_ffn[l].reshape(1, D)
        if l % 2 == 0:
            k = l // 2
            xs = ffn(xs, mod[l], nf, ffn_w1[k].astype(BF16), ffn_w3[k].astype(BF16), ffn_w2[k].astype(BF16), T)
        else:
            k = l // 2
            fuse = final_norm_w.reshape(1, D) if l == depth - 1 else None
            xs = moe_ffn(xs, mod[l], nf, router_w[k], moe_w1[k], moe_w3[k], moe_w2[k], T, fuse)
    if depth % 2 == 1:
        xs = final_norm(xs, final_norm_w.reshape(1, D))
    return xs.reshape(B, T, D)
```

```python
import functools

import jax
import jax.numpy as jnp
import numpy as np
from jax import lax
from jax.experimental import pallas as pl
from jax.experimental.pallas import tpu as pltpu
from jax.experimental.pallas import tpu_sc as plsc

F32 = jnp.float32
BF16 = jnp.bfloat16

D_MODEL = 1024
HEAD_DIM = 64
ROPE_THETA = 10000.0
NORM_EPS = 1e-6
NEG_INF = -1e30
REMOVED = -3e38

NSA_HEADS = 8
NSA_GROUPS = 2
NSA_HPG = NSA_HEADS // NSA_GROUPS
CMP_LEN = 32
CMP_STRIDE = 16
SEL_LEN = 64
SEL_TOPN = 16
WINDOW = 512
NSA_QBLOCK = 256

RET_HEADS = 4
RET_QK_DIM = 64
RET_CHUNK = 128

FOX_HEADS = 8
FOX_TQ = 1024
LOG2E = 1.4426950408889634

N_EXPERTS = 8

LANES = 128
VMEM_LIMIT = 56 * 1024 * 1024

P1_NQ = 0
P1_RQ = 512
P1_RK = 768
P1_NKC = 1024
P1_NKS = 1152
P1_NKW = 1280
P1_COLS = 1408
P2_MG = 0
P2_RV = 3072
P2_RG = 3584
P2_FQ = 4096
P2_FK = 4608
P2_FV = 5120
P2_NVC = 5632
P2_NVS = 5760
P2_NVW = 5888
P2_SMALL = 6016
P2_COLS = 6144
NSA_OUT = NSA_HEADS * HEAD_DIM


def _layer_spec(w, l):
    zeros = (0,) * (w.ndim - 1)
    return pl.BlockSpec((None,) + w.shape[1:], lambda *_: (l,) + zeros)


def _cparams(*sem):
    return pltpu.CompilerParams(dimension_semantics=tuple(sem), vmem_limit_bytes=VMEM_LIMIT)


def _sigmoid(x):
    return 1.0 / (1.0 + jnp.exp(-x))


def _dot(a, b):
    return jnp.dot(a, b, preferred_element_type=F32)


def _dot_nt(a, b):
    return lax.dot_general(a, b, (((1,), (1,)), ((), ())), preferred_element_type=F32)


def _dot_tn(a, b):
    return lax.dot_general(a, b, (((0,), (0,)), ((), ())), preferred_element_type=F32)


def _split3(x):
    hi = x.astype(BF16)
    r1 = x - hi.astype(F32)
    mid = r1.astype(BF16)
    lo = (r1 - mid.astype(F32)).astype(BF16)
    return hi, mid, lo


def _pack_bf16_pairs(x):
    c = x.shape[1] // 2
    lo = pltpu.bitcast(x[:, :c].astype(BF16).astype(F32), jnp.uint32) >> 16
    hi = pltpu.bitcast(x[:, c:].astype(BF16).astype(F32), jnp.uint32) & jnp.uint32(0xFFFF0000)
    return hi | lo


def _unpack_bf16_pairs(u):
    lo = pltpu.bitcast(u << 16, F32)
    hi = pltpu.bitcast(u & jnp.uint32(0xFFFF0000), F32)
    return jnp.concatenate([lo, hi], axis=1)


def _norm_mod(x, nw, sc, sh):
    ms = jnp.mean(x * x, axis=-1, keepdims=True)
    y = x * lax.rsqrt(ms + NORM_EPS) * nw
    return y * (1.0 + sc) + sh


def _mod_kernel(c_ref, w_ref, b_ref, o_ref):
    c = c_ref[...]
    s = c * _sigmoid(c)
    o_ref[0] = _dot(s.astype(BF16), w_ref[0].astype(BF16)) + b_ref[0]


def modulation(c, ada_w, ada_b):
    B, D = c.shape
    depth = ada_w.shape[0]
    rows = 8
    c_pad = jnp.zeros((rows, D), F32).at[:B].set(c)
    out = pl.pallas_call(
        _mod_kernel,
        grid=(depth, 6),
        in_specs=[pl.BlockSpec((rows, D), lambda l, j: (0, 0)),
                  pl.BlockSpec((1, D, D), lambda l, j: (l, 0, j)),
                  pl.BlockSpec((1, 1, D), lambda l, j: (l, 0, j))],
        out_specs=pl.BlockSpec((1, rows, D), lambda l, j: (l, 0, j)),
        out_shape=jax.ShapeDtypeStruct((depth, rows, 6 * D), F32),
        compiler_params=_cparams("parallel", "parallel"),
        name="modulation",
    )(c_pad, ada_w, ada_b.reshape(depth, 1, 6 * D))
    return out[:, :B].reshape(depth, B, 6, 1, D)


def _proj_plain_kernel(x_ref, nw_ref, sc_ref, sh_ref, w_ref, o_ref, *, tn):
    h = _norm_mod(x_ref[...], nw_ref[...], sc_ref[...], sh_ref[...]).astype(BF16)
    for n in range(w_ref.shape[0] // tn):
        cols = slice(n * tn, (n + 1) * tn)
        o_ref[:, cols] = _dot_nt(h, w_ref[cols, :]).astype(o_ref.dtype)


def _proj_rope_kernel(x_ref, nw_ref, sc_ref, sh_ref, w_ref, cos_ref, sin_ref, o_ref, *, scales):
    h = _norm_mod(x_ref[...], nw_ref[...], sc_ref[...], sh_ref[...]).astype(BF16)
    y = _dot_nt(h, w_ref[...])
    cos = cos_ref[...]
    sin = sin_ref[...]
    lane = lax.broadcasted_iota(jnp.int32, cos.shape, 1)
    first_half = (lane % HEAD_DIM) < (HEAD_DIM // 2)
    for g, scale in enumerate(scales):
        yg = y[:, g * LANES:(g + 1) * LANES]
        rot = jnp.where(first_half, pltpu.roll(yg, LANES - HEAD_DIM // 2, 1),
                        pltpu.roll(yg, HEAD_DIM // 2, 1))
        r = yg * cos + rot * sin
        if scale != 1.0:
            r = r * scale
        o_ref[:, g * LANES:(g + 1) * LANES] = r.astype(o_ref.dtype)


def _mod_specs(T, tm, sc_idx, sh_idx, nargs):
    per_b = T // tm
    if nargs == 1:
        return [pl.BlockSpec((None, None, 1, D_MODEL), lambda i: (i // per_b, sc_idx, 0, 0)),
                pl.BlockSpec((None, None, 1, D_MODEL), lambda i: (i // per_b, sh_idx, 0, 0))]
    return [pl.BlockSpec((None, None, 1, D_MODEL), lambda i, j: (i // per_b, sc_idx, 0, 0)),
            pl.BlockSpec((None, None, 1, D_MODEL), lambda i, j: (i // per_b, sh_idx, 0, 0))]


def proj_plain(x, mod_l, nw, w, l, T, *, tm=512, tn=512):
    M, D = x.shape
    N = w.shape[1]
    return pl.pallas_call(
        functools.partial(_proj_plain_kernel, tn=tn),
        grid=(M // tm,),
        in_specs=[pl.BlockSpec((tm, D), lambda i: (i, 0)),
                  pl.BlockSpec((1, D), lambda i: (0, 0))]
        + _mod_specs(T, tm, 1, 0, 1)
        + [_layer_spec(w, l)],
        out_specs=pl.BlockSpec((tm, N), lambda i: (i, 0)),
        out_shape=jax.ShapeDtypeStruct((M, N), BF16),
        compiler_params=_cparams("parallel"),
        name="proj_plain",
    )(x, nw, mod_l, mod_l, w)


def proj_rope(x, mod_l, nw, w, l, cos, sin, scales, T, *, tm=512):
    M, D = x.shape
    N = w.shape[1]
    per_b = T // tm
    return pl.pallas_call(
        functools.partial(_proj_rope_kernel, scales=scales),
        grid=(M // tm,),
        in_specs=[pl.BlockSpec((tm, D), lambda i: (i, 0)),
                  pl.BlockSpec((1, D), lambda i: (0, 0))]
        + _mod_specs(T, tm, 1, 0, 1)
        + [_layer_spec(w, l),
           pl.BlockSpec((tm, LANES), lambda i: (i % per_b, 0)),
           pl.BlockSpec((tm, LANES), lambda i: (i % per_b, 0))],
        out_specs=pl.BlockSpec((tm, N), lambda i: (i, 0)),
        out_shape=jax.ShapeDtypeStruct((M, N), BF16),
        compiler_params=_cparams("parallel"),
        name="proj_rope",
    )(x, nw, mod_l, mod_l, w, cos, sin)


def rope_tables(T):
    d = HEAD_DIM
    inv = ROPE_THETA ** (-np.arange(0, d, 2, dtype=np.float64) / d)
    ang = np.arange(T, dtype=np.float64)[:, None] * inv[None, :]
    cos = np.cos(ang)
    sin = np.sin(ang)
    cos_t = np.concatenate([cos, cos, cos, cos], axis=-1)
    sin_t = np.concatenate([-sin, sin, -sin, sin], axis=-1)
    return jnp.asarray(cos_t, F32), jnp.asarray(sin_t, F32)


def split_w_in(w_in):
    sizes = [512, 128, 128, 128, 128, 128, 128, 24, 256, 256, 512, 512, 512, 512, 512, 8, 3072]
    offs = np.cumsum([0] + sizes)
    wb = jnp.swapaxes(w_in, 1, 2).astype(BF16)
    (nq, nkc, nvc, nks, nvs, nkw, nvw, ngate, rq, rk, rv, rg, fq, fk, fv, ff, mg) = [
        wb[:, offs[i]:offs[i + 1], :] for i in range(len(sizes))]
    small = jnp.concatenate([ngate, ff, jnp.zeros((wb.shape[0], LANES - 32, wb.shape[2]), BF16)], axis=1)
    w1 = jnp.concatenate([nq, rq, rk, nkc, nks, nkw], axis=1)
    w2 = jnp.concatenate([mg, rv, rg, fq, fk, fv, nvc, nvs, nvw, small], axis=1)
    assert w1.shape[1] == P1_COLS and w2.shape[1] == P2_COLS
    return w1, w2


def p1_scales():
    s = [1.0] * (P1_COLS // LANES)
    for g in range(P1_NQ // LANES, P1_RQ // LANES):
        s[g] = HEAD_DIM ** -0.5 * LOG2E
    for g in range(P1_RK // LANES, P1_NKC // LANES):
        s[g] = RET_QK_DIM ** -0.5
    return tuple(s)


def _compress_kernel(x_ref, pe_ref, w1_ref, w2_ref, o_ref):
    r = x_ref[...]
    half = r.shape[1]
    w1 = w1_ref[...]
    a = _dot(r, w1[:half])
    b = _dot(r, w1[half:])
    pe = _dot(pe_ref[...], w1)[0:1]
    n = a.shape[0]
    hid = a + pltpu.roll(b, n - 1, 0) + pe
    hid = hid * _sigmoid(hid)
    o_ref[...] = _dot(hid.astype(BF16), w2_ref[...]).astype(o_ref.dtype)


def compress(xr, pe, w1, w2):
    _, B, G, R, W = xr.shape
    H = w1.shape[-1]
    return pl.pallas_call(
        _compress_kernel,
        grid=(2, B, G),
        in_specs=[pl.BlockSpec((None, None, None, R, W), lambda s, b, g: (s, b, g, 0, 0)),
                  pl.BlockSpec((None, 8, 2 * W), lambda s, b, g: (s, 0, 0)),
                  pl.BlockSpec((None, 2 * W, H), lambda s, b, g: (s, 0, 0)),
                  pl.BlockSpec((None, H, HEAD_DIM), lambda s, b, g: (s, 0, 0))],
        out_specs=pl.BlockSpec((None, None, None, R, HEAD_DIM), lambda s, b, g: (s, b, g, 0, 0)),
        out_shape=jax.ShapeDtypeStruct((2, B, G, R, HEAD_DIM), BF16),
        compiler_params=_cparams("parallel", "parallel", "parallel"),
        name="nsa_compress",
    )(xr, pe, w1, w2)


def _stack_heads(q_ref, g):
    tq = q_ref.shape[0]
    half = lax.broadcasted_iota(jnp.int32, (tq, LANES), 1) // HEAD_DIM
    rows = []
    for hh in range(NSA_HPG):
        h = NSA_HPG * g + hh
        x = q_ref[:, (h // 2) * LANES:(h // 2 + 1) * LANES].astype(F32)
        if h % 2 != g:
            x = pltpu.roll(x, HEAD_DIM, 1)
        rows.append(jnp.where(half == g, x, 0.0).astype(BF16))
    return jnp.concatenate(rows, axis=0)


def _store_heads(o_ref, g, o, tq):
    low = lax.broadcasted_iota(jnp.int32, (tq, LANES), 1) < HEAD_DIM
    for pair in range(NSA_HPG // 2):
        even = o[(2 * pair) * tq:(2 * pair + 1) * tq]
        odd = o[(2 * pair + 1) * tq:(2 * pair + 2) * tq]
        if g == 0:
            blk = jnp.where(low, even, pltpu.roll(odd, HEAD_DIM, 1))
        else:
            blk = jnp.where(low, pltpu.roll(even, HEAD_DIM, 1), odd)
        col = (NSA_HPG // 2 * g + pair) * LANES
        o_ref[:, col:col + LANES] = blk.astype(o_ref.dtype)


CMP_CHUNK = 128


def _nsa_cmp_kernel(q_ref, kc_ref, vc_ref, ov_ref, o_ref, m_ref, imp_ref, *, tq, n_sel, top_n):
    t0 = pl.program_id(1) * tq
    ncp = kc_ref.shape[0]
    nsp = ov_ref.shape[0]
    rows = NSA_HPG * tq

    def attend(ncols):
        kc = kc_ref[0:ncols, :]
        vc = vc_ref[0:ncols, :]
        n_idx = lax.broadcasted_iota(jnp.int32, (rows, ncols), 1)
        t_idx = t0 + lax.broadcasted_iota(jnp.int32, (rows, ncols), 0) % tq
        valid = (n_idx * CMP_STRIDE + (CMP_LEN - 1)) <= t_idx
        for g in range(NSA_GROUPS):
            q = _stack_heads(q_ref, g)
            s = jnp.where(valid, _dot_nt(q, kc), NEG_INF)
            m = jnp.max(s, axis=-1, keepdims=True)
            e = jnp.exp2(s - m)
            l = jnp.sum(e, axis=-1, keepdims=True)
            p = e * jnp.where(m > 0.5 * NEG_INF, 1.0 / l, 0.0)
            _store_heads(o_ref, g, _dot(p.astype(BF16), vc), tq)
            psum = p[0:tq]
            for hh in range(1, NSA_HPG):
                psum = psum + p[hh * tq:(hh + 1) * tq]
            imp_ref[g] = _dot_nt(ov_ref[:, 0:ncols], psum.astype(BF16))

    n_live = jnp.maximum((t0 + tq - CMP_LEN) // CMP_STRIDE + 1, 1)
    n_chunks = jnp.minimum((n_live + CMP_CHUNK - 1) // CMP_CHUNK, ncp // CMP_CHUNK)
    for nc in range(1, ncp // CMP_CHUNK + 1):
        pl.when(n_chunks == nc)(functools.partial(attend, nc * CMP_CHUNK))

    j_idx = lax.broadcasted_iota(jnp.int32, (nsp, tq), 0)
    cur = (t0 + lax.broadcasted_iota(jnp.int32, (nsp, tq), 1)) // SEL_LEN
    forced = (j_idx == 0) | (j_idx == cur) | (j_idx == cur - 1)
    j_f = j_idx.astype(F32)
    for g in range(NSA_GROUPS):
        score = jnp.where(j_idx <= cur, imp_ref[g], NEG_INF)
        score = jnp.where(forced | (j_idx >= n_sel), REMOVED, score)
        sel = jnp.where(forced, 1.0, 0.0)
        for _ in range(max(top_n - 3, 0)):
            mx = jnp.max(score, axis=0, keepdims=True)
            idx = jnp.min(jnp.where(score == mx, j_f, float(nsp)), axis=0, keepdims=True)
            hit = j_f == idx
            sel = jnp.where(hit, 1.0, sel)
            score = jnp.where(hit, REMOVED, score)
        sel = jnp.where(j_idx <= cur, sel, 0.0)
        m_ref[g] = sel.T.astype(m_ref.dtype)


SEL_BONUS = 32768.0
NSA_SEL_TQ = 256
NSA_SEL_TK = 1024


def _nsa_sel_kernel(q_ref, k_ref, vin_ref, m_ref, et_ref, o_ref, v_ref, *, tq, tk):
    @pl.when(pl.program_id(1) == 0)
    def _():
        v = vin_ref[...]
        low = lax.broadcasted_iota(jnp.int32, v.shape, 1) < HEAD_DIM
        one = jnp.ones_like(v)
        v_ref[0] = jnp.where(low, v, one)
        v_ref[1] = jnp.where(low, one, v)

    t0 = pl.program_id(1) * tq
    n_full = t0 // tk
    rows = NSA_HPG * tq

    def update(carry, q, ks, vs, mask=None):
        m, acc = carry
        s = _dot_nt(q, ks)
        if mask is not None:
            s = jnp.where(mask, s, NEG_INF)
        m_new = jnp.maximum(m, jnp.max(s, axis=-1, keepdims=True))
        p = jnp.exp2(s - m_new)
        return m_new, jnp.exp2(m - m_new) * acc + _dot(p.astype(BF16), vs)

    qs, carries = [], []
    for g in range(NSA_GROUPS):
        q = jnp.concatenate([_stack_heads(q_ref, g), jnp.concatenate([m_ref[g]] * NSA_HPG, axis=0)], axis=1)

        def step(j, carry, q=q, g=g):
            start = pl.multiple_of(j * tk, tk)
            ks = jnp.concatenate([k_ref[pl.ds(start, tk), :], et_ref[pl.ds(start, tk), :]], axis=1)
            return update(carry, q, ks, v_ref[g, pl.ds(start, tk), :])

        init = (jnp.full((rows, 1), NEG_INF, F32), jnp.zeros((rows, LANES), F32))
        qs.append(q)
        carries.append(lax.fori_loop(0, n_full, step, init))

    start = pl.multiple_of(n_full * tk, tk)

    def tail(nk):
        trow = t0 + lax.broadcasted_iota(jnp.int32, (rows, nk), 0) % tq
        causal = start + lax.broadcasted_iota(jnp.int32, (rows, nk), 1) <= trow
        ks = jnp.concatenate([k_ref[pl.ds(start, nk), :], et_ref[pl.ds(start, nk), :]], axis=1)
        for g in range(NSA_GROUPS):
            _, acc = update(carries[g], qs[g], ks, v_ref[g, pl.ds(start, nk), :], causal)
            den = HEAD_DIM * (1 - g)
            _store_heads(o_ref, g, acc / acc[:, den:den + 1], tq)

    which = (t0 - start) // tq
    for v in range(tk // tq):
        pl.when(which == v)(functools.partial(tail, (v + 1) * tq))


def nsa_selected(p1, p2, sel, et_mat, T, *, tq=NSA_SEL_TQ, tk=NSA_SEL_TK):
    B = p1.shape[0]
    nsp = sel.shape[-1]
    return pl.pallas_call(
        functools.partial(_nsa_sel_kernel, tq=tq, tk=tk),
        grid=(B, T // tq),
        in_specs=[pl.BlockSpec((None, tq, NSA_HEADS * HEAD_DIM), lambda b, i: (b, i, 0)),
                  pl.BlockSpec((None, T, LANES), lambda b, i: (b, 0, P1_NKS // LANES)),
                  pl.BlockSpec((None, T, LANES), lambda b, i: (b, 0, P2_NVS // LANES)),
                  pl.BlockSpec((None, NSA_GROUPS, tq, nsp), lambda b, i: (b, 0, i, 0)),
                  pl.BlockSpec((T, nsp), lambda b, i: (0, 0))],
        out_specs=pl.BlockSpec((None, tq, NSA_OUT), lambda b, i: (b, i, 0)),
        out_shape=jax.ShapeDtypeStruct((B, T, NSA_OUT), BF16),
        scratch_shapes=[pltpu.VMEM((NSA_GROUPS, T, LANES), BF16)],
        compiler_params=_cparams("parallel", "arbitrary"),
        name="nsa_selected",
    )(p1, p1, p2, sel, et_mat)


def _nsa_win_kernel(q_ref, k_ref, v_ref, b_ref, o_ref, *, tq):
    t0 = pl.program_id(1) * tq
    span = WINDOW + tq
    start = pl.multiple_of(jnp.maximum(t0 - WINDOW, 0), tq)
    ks = k_ref[pl.ds(start, span), :]
    vs = v_ref[pl.ds(start, span), :]

    def run(bias):
        bias = jnp.concatenate([bias] * NSA_HPG, axis=0)
        for g in range(NSA_GROUPS):
            s = _dot_nt(_stack_heads(q_ref, g), ks) + bias
            m = jnp.max(s, axis=-1, keepdims=True)
            p = jnp.exp2(s - m)
            l = jnp.sum(p, axis=-1, keepdims=True)
            _store_heads(o_ref, g, _dot(p.astype(BF16), vs) / l, tq)

    @pl.when(t0 >= WINDOW)
    def _():
        run(b_ref[...])

    @pl.when(t0 < WINDOW)
    def _():
        row = lax.broadcasted_iota(jnp.int32, (tq, span), 0)
        col = lax.broadcasted_iota(jnp.int32, (tq, span), 1)
        run(jnp.where(col <= t0 + row, 0.0, NEG_INF))


def _nsa_cmp_win_kernel(q_ref, kc_ref, vc_ref, ov_ref, kw_ref, vw_ref, band_ref, ocmp_ref, m_ref, owin_ref, imp_ref,
                        *, tq, n_sel, top_n):
    _nsa_cmp_kernel(q_ref, kc_ref, vc_ref, ov_ref, ocmp_ref, m_ref, imp_ref, tq=tq, n_sel=n_sel, top_n=top_n)
    _nsa_win_kernel(q_ref, kw_ref, vw_ref, band_ref, owin_ref, tq=tq)


def nsa_cmp_select_window(p1, p2, kc, vc, ov_t, T):
    B = p1.shape[0]
    tq = NSA_QBLOCK
    ncp = kc.shape[1]
    nsp = ov_t.shape[0]
    n_sel = T // SEL_LEN
    span = WINDOW + tq
    r = np.arange(tq)[:, None]
    c = np.arange(span)[None, :]
    band = jnp.asarray(np.where((c > r) & (c <= r + WINDOW), 0.0, NEG_INF), F32)
    out_blk = pl.BlockSpec((None, tq, NSA_OUT), lambda b, i: (b, i, 0))
    return pl.pallas_call(
        functools.partial(_nsa_cmp_win_kernel, tq=tq, n_sel=n_sel, top_n=min(SEL_TOPN, n_sel)),
        grid=(B, T // tq),
        in_specs=[pl.BlockSpec((None, tq, NSA_HEADS * HEAD_DIM), lambda b, i: (b, i, 0)),
                  pl.BlockSpec((None, ncp, LANES), lambda b, i: (b, 0, 0)),
                  pl.BlockSpec((None, ncp, LANES), lambda b, i: (b, 0, 0)),
                  pl.BlockSpec((nsp, ncp), lambda b, i: (0, 0)),
                  pl.BlockSpec((None, T, LANES), lambda b, i: (b, 0, P1_NKW // LANES)),
                  pl.BlockSpec((None, T, LANES), lambda b, i: (b, 0, P2_NVW // LANES)),
                  pl.BlockSpec((tq, span), lambda b, i: (0, 0))],
        out_specs=[out_blk, pl.BlockSpec((None, NSA_GROUPS, tq, nsp), lambda b, i: (b, 0, i, 0)), out_blk],
        out_shape=[jax.ShapeDtypeStruct((B, T, NSA_OUT), BF16),
                   jax.ShapeDtypeStruct((B, NSA_GROUPS, T, nsp), BF16),
                   jax.ShapeDtypeStruct((B, T, NSA_OUT), BF16)],
        scratch_shapes=[pltpu.VMEM((NSA_GROUPS, nsp, tq), F32)],
        compiler_params=_cparams("parallel", "parallel"),
        name="nsa_cmp_select_window",
    )(p1, kc, vc, ov_t, p1, p2, band)


def _retention_kernel(q_ref, k_ref, v_ref, g_ref, din_ref, qd_ref, kd_ref, cd_ref, o_ref, st_ref):
    @pl.when(pl.program_id(0) == 0)
    def _():
        st_ref[...] = jnp.zeros_like(st_ref)

    B = q_ref.shape[0]
    C = RET_CHUNK
    half = lax.broadcasted_iota(jnp.int32, (C, LANES), 1) // HEAD_DIM
    for b in range(B):
        for h in range(RET_HEADS):
            lanes = slice(h * LANES, (h + 1) * LANES)
            pair = slice((h // 2) * LANES, (h // 2 + 1) * LANES)
            st = st_ref[b, h]
            for sub in range(q_ref.shape[1] // C):
                rows = slice(sub * C, (sub + 1) * C)
                qh = jnp.where(half == h % 2, q_ref[b, rows, pair], 0.0).astype(BF16)
                kp = k_ref[b, rows, pair]
                vh = v_ref[b, rows, lanes]
                inner = _dot_nt(qh, kp) * din_ref[h]
                o = _dot(inner.astype(BF16), vh) + _dot(qh, st.astype(BF16)) * qd_ref[h]
                kd = (kp.astype(F32) * kd_ref[h]).astype(BF16)
                st = st * cd_ref[h, 0:1, :] + _dot_tn(kd, vh)
                mu = jnp.mean(o, axis=-1, keepdims=True)
                d = o - mu
                var = jnp.mean(d * d, axis=-1, keepdims=True)
                on = d * lax.rsqrt(var + NORM_EPS)
                gh = g_ref[b, rows, lanes].astype(F32)
                o_ref[b, rows, lanes] = (gh * _sigmoid(gh) * on).astype(o_ref.dtype)
            st_ref[b, h] = st


def retention_consts():
    C = RET_CHUNK
    H = RET_HEADS
    log_g = np.log(1.0 - 2.0 ** (-5.0 - np.arange(H, dtype=np.float64)))
    n = np.arange(C, dtype=np.float64)
    diff = n[:, None] - n[None, :]
    causal = diff >= 0
    decay_in = np.where(causal[None], np.exp(np.where(causal, diff, 0.0)[None] * log_g[:, None, None]), 0.0)
    q_decay = np.exp((n[None, :] + 1.0) * log_g[:, None])
    k_decay = np.exp((C - 1.0 - n)[None, :] * log_g[:, None])
    chunk_decay = np.exp(C * log_g)
    qd = np.broadcast_to(q_decay[:, :, None], (H, C, LANES))
    kd = np.broadcast_to(k_decay[:, :, None], (H, C, LANES))
    cd = np.broadcast_to(chunk_decay[:, None, None], (H, 8, LANES))
    return tuple(jnp.asarray(a, F32) for a in (decay_in, qd, kd, cd))


RET_STEP = 4


def retention(p1, p2, consts, T):
    B = p1.shape[0]
    C = RET_CHUNK * RET_STEP
    din, qd, kd, cd = consts
    W = RET_HEADS * LANES
    full = lambda shape: pl.BlockSpec(shape, lambda c: (0,) * len(shape))
    return pl.pallas_call(
        _retention_kernel,
        grid=(T // C,),
        in_specs=[pl.BlockSpec((B, C, W // 2), lambda c: (0, c, P1_RQ // (W // 2))),
                  pl.BlockSpec((B, C, W // 2), lambda c: (0, c, P1_RK // (W // 2))),
                  pl.BlockSpec((B, C, W), lambda c: (0, c, P2_RV // W)),
                  pl.BlockSpec((B, C, W), lambda c: (0, c, P2_RG // W)),
                  full(din.shape), full(qd.shape), full(kd.shape), full(cd.shape)],
        out_specs=pl.BlockSpec((B, C, W), lambda c: (0, c, 0)),
        out_shape=jax.ShapeDtypeStruct((B, T, W), BF16),
        scratch_shapes=[pltpu.VMEM((B, RET_HEADS, LANES, LANES), F32)],
        compiler_params=_cparams("arbitrary"),
        name="retention",
    )(p1, p1, p2, p2, din, qd, kd, cd)


def _fox_cum_kernel(f_ref, b_ref, o_ref):
    x = f_ref[...] + b_ref[...]
    ls = jnp.minimum(x, 0.0) - jnp.log1p(jnp.exp(-jnp.abs(x)))
    R = x.shape[0]
    ki = lax.broadcasted_iota(jnp.int32, (LANES, LANES), 0)
    ji = lax.broadcasted_iota(jnp.int32, (LANES, LANES), 1)
    upper = jnp.where(ki <= ji, 1.0, 0.0).astype(BF16)
    hi, mid, lo = _split3(ls)
    rowcum = _dot(hi, upper) + _dot(mid, upper) + _dot(lo, upper)
    tot = jnp.broadcast_to(rowcum[:, LANES - 1:LANES], (R, LANES))
    ri = lax.broadcasted_iota(jnp.int32, (R, R), 0)
    ci = lax.broadcasted_iota(jnp.int32, (R, R), 1)
    lower = jnp.where(ci < ri, 1.0, 0.0).astype(BF16)
    hi, mid, lo = _split3(tot)
    offs = _dot(lower, hi) + _dot(lower, mid) + _dot(lower, lo)
    o_ref[...] = (rowcum + offs) * LOG2E


def fox_cum(f_logit, bias):
    B, H, R, _ = f_logit.shape
    return pl.pallas_call(
        _fox_cum_kernel,
        grid=(B, H),
        in_specs=[pl.BlockSpec((None, None, R, LANES), lambda b, h: (b, h, 0, 0)),
                  pl.BlockSpec((None, 1, LANES), lambda b, h: (h, 0, 0))],
        out_specs=pl.BlockSpec((None, None, R, LANES), lambda b, h: (b, h, 0, 0)),
        out_shape=jax.ShapeDtypeStruct((B, H, R, LANES), F32),
        compiler_params=_cparams("parallel", "parallel"),
        name="fox_cum",
    )(f_logit, bias)


FOX_BIAS_LANES = 3


def _fox_kernel(q_ref, k_ref, v_ref, c_ref, o_ref, ka_ref, va_ref, *, tq):
    i = pl.program_id(2)
    tk = tq
    T = k_ref.shape[0]
    chunk = 512

    @pl.when(i == 0)
    def _():
        lane = lax.broadcasted_iota(jnp.int32, (chunk, LANES), 1)
        ri = lax.broadcasted_iota(jnp.int32, (16, LANES), 0)
        ci = lax.broadcasted_iota(jnp.int32, (16, LANES), 1)
        place = jnp.where((ci == ri + HEAD_DIM) & (ri < FOX_BIAS_LANES), 1.0, 0.0).astype(BF16)

        def build(c, _):
            c0 = pl.multiple_of(c * chunk, chunk)
            kp = k_ref[pl.ds(c0, chunk), :].astype(F32)
            vp = v_ref[pl.ds(c0, chunk), :].astype(F32)
            for hh in range(2):
                hi, mid, lo = _split3(-c_ref[hh, :, pl.ds(c0, chunk)])
                terms = jnp.concatenate([hi, mid, lo, jnp.zeros((13, chunk), BF16)], axis=0)
                bias = _dot_tn(terms, place)
                kh = kp if hh == 0 else pltpu.roll(kp, HEAD_DIM, 1)
                vh = vp if hh == 0 else pltpu.roll(vp, HEAD_DIM, 1)
                ka_ref[hh, pl.ds(c0, chunk), :] = jnp.where(lane < HEAD_DIM, kh, bias).astype(BF16)
                va_ref[hh, pl.ds(c0, chunk), :] = jnp.where(lane < HEAD_DIM, vh, 1.0).astype(BF16)
            return 0

        lax.fori_loop(0, T // chunk, build, 0)

    lane = lax.broadcasted_iota(jnp.int32, (tq, LANES), 1)
    ones_lanes = (lane >= HEAD_DIM) & (lane < HEAD_DIM + FOX_BIAS_LANES)
    qp = q_ref[...].astype(F32) * (HEAD_DIM ** -0.5 * LOG2E)
    qs = [jnp.where(lane < HEAD_DIM, qh, jnp.where(ones_lanes, 1.0, 0.0)).astype(BF16)
          for qh in (qp, pltpu.roll(qp, HEAD_DIM, 1))]

    def update(hh, m, acc, q, start, size, mask=None):
        s = _dot_nt(q, ka_ref[hh, pl.ds(start, size), :])
        if mask is not None:
            s = jnp.where(mask, s, NEG_INF)
        m_new = jnp.maximum(m, jnp.max(s, axis=-1, keepdims=True))
        p = jnp.exp2(s - m_new)
        return m_new, jnp.exp2(m - m_new) * acc + _dot(p.astype(BF16), va_ref[hh, pl.ds(start, size), :])

    def step(j, carry):
        start = pl.multiple_of(j * tk, tk)
        return tuple(update(hh, *carry[hh], qs[hh], start, tk) for hh in range(2))

    one = (jnp.full((tq, 1), NEG_INF, F32), jnp.zeros((tq, LANES), F32))
    carry = lax.fori_loop(0, i, step, (one, one))

    half = tq // 2
    start = pl.multiple_of(i * tk, tk)
    row = lax.broadcasted_iota(jnp.int32, (tq, half), 0)
    col = lax.broadcasted_iota(jnp.int32, (tq, half), 1)
    accs = []
    for hh in range(2):
        m, acc = update(hh, *carry[hh], qs[hh], start, half, col <= row)
        _, low = update(hh, m[half:], acc[half:], qs[hh][half:], start + half, half, (col <= row)[:half])
        accs.append(jnp.concatenate([acc[:half], low], axis=0))
    acc0, acc1 = accs
    o0 = acc0 / acc0[:, HEAD_DIM:HEAD_DIM + 1]
    o1 = acc1 / acc1[:, HEAD_DIM:HEAD_DIM + 1]
    o_ref[...] = jnp.where(lane < HEAD_DIM, o0, pltpu.roll(o1, HEAD_DIM, 1)).astype(o_ref.dtype)


def fox_attention(p2, cum, T, *, tq=FOX_TQ):
    B = p2.shape[0]
    HP = FOX_HEADS // 2
    return pl.pallas_call(
        functools.partial(_fox_kernel, tq=tq),
        grid=(B, HP, T // tq),
        in_specs=[pl.BlockSpec((None, tq, LANES), lambda b, h, i: (b, i, P2_FQ // LANES + h)),
                  pl.BlockSpec((None, T, LANES), lambda b, h, i: (b, 0, P2_FK // LANES + h)),
                  pl.BlockSpec((None, T, LANES), lambda b, h, i: (b, 0, P2_FV // LANES + h)),
                  pl.BlockSpec((None, None, 2, 1, T), lambda b, h, i: (b, h, 0, 0, 0))],
        out_specs=pl.BlockSpec((None, tq, LANES), lambda b, h, i: (b, i, h)),
        out_shape=jax.ShapeDtypeStruct((B, T, FOX_HEADS * HEAD_DIM), BF16),
        scratch_shapes=[pltpu.VMEM((2, T, LANES), BF16), pltpu.VMEM((2, T, LANES), BF16)],
        compiler_params=_cparams("parallel", "parallel", "arbitrary"),
        name="fox_attention",
    )(p2, p2, p2, cum)


def _readout_kernel(ocmp_ref, osel_ref, owin_ref, small_ref, oret_ref, ofox_ref, mg_ref, x_ref, g1_ref,
                    ex_ref, wn_ref, wr_ref, wf_ref, wo_ref, o_ref):
    W = NSA_OUT
    gs = _sigmoid(small_ref[...].astype(F32)).astype(BF16)
    ge = _dot(gs, ex_ref[...])
    onsa = (ge[:, :W] * ocmp_ref[...].astype(F32) + ge[:, W:2 * W] * osel_ref[...].astype(F32)
            + ge[:, 2 * W:] * owin_ref[...].astype(F32))
    D = D_MODEL
    merged = (_sigmoid(mg_ref[:, :D].astype(F32)) * _dot(onsa.astype(BF16), wn_ref[...])
              + _sigmoid(mg_ref[:, D:2 * D].astype(F32)) * _dot(oret_ref[...], wr_ref[...])
              + _sigmoid(mg_ref[:, 2 * D:].astype(F32)) * _dot(ofox_ref[...], wf_ref[...]))
    y = _dot(merged.astype(BF16), wo_ref[...])
    o_ref[...] = x_ref[...] + g1_ref[...] * y


def readout(o_cmp, o_sel, o_win, p2, o_ret, o_fox, x, mod_l, ex, wn, wr, wf, wo, l, T, *, tm=512):
    M, D = x.shape
    per_b = T // tm
    W = NSA_OUT
    row = lambda width, col=0: pl.BlockSpec((tm, width), lambda i: (i, col))
    full = lambda a: pl.BlockSpec(a.shape, lambda i: (0,) * a.ndim)
    return pl.pallas_call(
        _readout_kernel,
        grid=(M // tm,),
        in_specs=[row(W), row(W), row(W), row(LANES, P2_SMALL // LANES), row(512), row(512),
                  row(3 * D, 0), row(D),
                  pl.BlockSpec((None, None, 1, D), lambda i: (i // per_b, 2, 0, 0)),
                  full(ex), _layer_spec(wn, l), _layer_spec(wr, l), _layer_spec(wf, l), _layer_spec(wo, l)],
        out_specs=row(D),
        out_shape=jax.ShapeDtypeStruct((M, D), F32),
        compiler_params=_cparams("parallel"),
        name="mixer_readout",
    )(o_cmp, o_sel, o_win, p2, o_ret, o_fox, p2, x, mod_l, ex, wn, wr, wf, wo)


def nsa_gate_expand():
    ex = np.zeros((LANES, 3 * NSA_OUT), np.float32)
    for br in range(3):
        for h in range(NSA_HEADS):
            c0 = br * NSA_OUT + h * HEAD_DIM
            ex[br * NSA_HEADS + h, c0:c0 + HEAD_DIM] = 1.0
    return jnp.asarray(ex, BF16)


FFN_CHUNK = 512


def _ffn_kernel(x_ref, nw_ref, sc_ref, sh_ref, g2_ref, w1_ref, w3_ref, w2_ref, o_ref):
    x = x_ref[...]
    h = _norm_mod(x, nw_ref[...], sc_ref[...], sh_ref[...]).astype(BF16)
    F = w1_ref.shape[1]
    y = None
    for c0 in range(0, F, FFN_CHUNK):
        cols = slice(c0, min(c0 + FFN_CHUNK, F))
        u = _dot(h, w1_ref[:, cols])
        v = _dot(h, w3_ref[:, cols])
        part = _dot((u * _sigmoid(u) * v).astype(BF16), w2_ref[cols, :])
        y = part if y is None else y + part
    o_ref[...] = x + g2_ref[...] * y


def ffn(x, mod_l, nw, w1, w3, w2, T, *, tm=512):
    M, D = x.shape
    F = w1.shape[1]
    per_b = T // tm
    modspec = lambda k: pl.BlockSpec((None, None, 1, D), lambda i: (i // per_b, k, 0, 0))
    full = lambda a: pl.BlockSpec(a.shape, lambda i: (0,) * a.ndim)
    return pl.pallas_call(
        _ffn_kernel,
        grid=(M // tm,),
        in_specs=[pl.BlockSpec((tm, D), lambda i: (i, 0)),
                  pl.BlockSpec((1, D), lambda i: (0, 0)),
                  modspec(4), modspec(3), modspec(5), full(w1), full(w3), full(w2)],
        out_specs=pl.BlockSpec((tm, D), lambda i: (i, 0)),
        out_shape=jax.ShapeDtypeStruct((M, D), F32),
        compiler_params=_cparams("parallel"),
        name="ffn_dense",
    )(x, nw, mod_l, mod_l, mod_l, w1, w3, w2)


MOE_TC = 512
MOE_TS = 512


def _router_kernel(x_ref, nw_ref, sc_ref, sh_ref, wh_ref, wl_ref, h_ref, gate_ref, rank_ref, cnt_ref, carry_ref):
    @pl.when(pl.program_id(0) == 0)
    def _():
        carry_ref[...] = jnp.zeros_like(carry_ref)

    h = _norm_mod(x_ref[...], nw_ref[...], sc_ref[...], sh_ref[...])
    hh = h.astype(BF16)
    h_ref[...] = _pack_bf16_pairs(hh.astype(F32))
    hl = (h - hh.astype(F32)).astype(BF16)
    logits = _dot(hh, wh_ref[...]) + (_dot(hl, wh_ref[...]) + _dot(hh, wl_ref[...]))
    tm = logits.shape[0]
    lane = lax.broadcasted_iota(jnp.int32, logits.shape, 1)
    logits = jnp.where(lane < N_EXPERTS, logits, REMOVED)
    lane_f = lane.astype(F32)
    v1 = jnp.max(logits, axis=-1, keepdims=True)
    i1 = jnp.min(jnp.where(logits == v1, lane_f, float(LANES)), axis=-1, keepdims=True)
    rest = jnp.where(lane_f == i1, REMOVED, logits)
    v2 = jnp.max(rest, axis=-1, keepdims=True)
    i2 = jnp.min(jnp.where(rest == v2, lane_f, float(LANES)), axis=-1, keepdims=True)
    e2 = jnp.exp(v2 - v1)
    w1 = 1.0 / (1.0 + e2)
    w2 = e2 / (1.0 + e2)
    gate_ref[...] = jnp.where(lane_f == i1, w1, jnp.where(lane_f == i2, w2, 0.0))

    sel = jnp.where((lane_f == i1) | (lane_f == i2), 1.0, 0.0)
    ri = lax.broadcasted_iota(jnp.int32, (tm, tm), 0)
    ci = lax.broadcasted_iota(jnp.int32, (tm, tm), 1)
    before = jnp.where(ci < ri, 1.0, 0.0).astype(BF16)
    rank = _dot(before, sel.astype(BF16)) + carry_ref[0:1, :]
    rank_ref[...] = jnp.where(sel > 0.0, rank, -1.0)
    carry_ref[...] = carry_ref[...] + jnp.sum(sel, axis=0, keepdims=True)
    cnt_ref[...] = carry_ref[...]


def router(x, mod_l, nw, w_router, T):
    M, D = x.shape
    tm = MOE_TC
    per_b = T // tm
    wp = jnp.zeros((D, LANES), F32).at[:, :N_EXPERTS].set(w_router)
    wh = wp.astype(BF16)
    wl = (wp - wh.astype(F32)).astype(BF16)
    return pl.pallas_call(
        _router_kernel,
        grid=(M // tm,),
        in_specs=[pl.BlockSpec((tm, D), lambda i: (i, 0)),
                  pl.BlockSpec((1, D), lambda i: (0, 0))]
        + _mod_specs(T, tm, 4, 3, 1)
        + [pl.BlockSpec((D, LANES), lambda i: (0, 0)),
           pl.BlockSpec((D, LANES), lambda i: (0, 0))],
        out_specs=[pl.BlockSpec((tm, D // 2), lambda i: (i, 0)),
                   pl.BlockSpec((tm, LANES), lambda i: (i, 0)),
                   pl.BlockSpec((tm, LANES), lambda i: (i, 0)),
                   pl.BlockSpec((8, LANES), lambda i: (0, 0))],
        out_shape=[jax.ShapeDtypeStruct((M, D // 2), jnp.uint32),
                   jax.ShapeDtypeStruct((M, LANES), F32),
                   jax.ShapeDtypeStruct((M, LANES), F32),
                   jax.ShapeDtypeStruct((8, LANES), F32)],
        scratch_shapes=[pltpu.VMEM((8, LANES), F32)],
        compiler_params=_cparams("arbitrary"),
        name="moe_router",
    )(x, nw, mod_l, mod_l, wh, wl)


def _count_le(sorted_vals, x):
    return jnp.sum(sorted_vals[None, :] <= x[:, None], axis=1, dtype=jnp.int32)


def _moe_up_kernel(e_r, total, x_ref, w1_ref, w3_ref, o_ref, w1b_ref, w3b_ref):
    r = pl.program_id(1)
    live = r < total[0]

    @pl.when(live & ((r == 0) | (e_r[r] != e_r[jnp.maximum(r - 1, 0)])))
    def _():
        w1b_ref[...] = w1_ref[...].astype(BF16)
        w3b_ref[...] = w3_ref[...].astype(BF16)

    @pl.when(live)
    def _():
        x = _unpack_bf16_pairs(x_ref[...]).astype(BF16)
        u = _dot(x, w1b_ref[...])
        v = _dot(x, w3b_ref[...])
        o_ref[...] = (u * _sigmoid(u) * v).astype(o_ref.dtype)


def moe_up(xs, w1, w3, tiles, rt, *, tf=1792):
    R = xs.shape[0]
    D = w1.shape[1]
    ts = MOE_TS
    F = w1.shape[-1]
    live = lambda r, total: jnp.minimum(r, total[0] - 1)
    return pl.pallas_call(
        _moe_up_kernel,
        grid_spec=pltpu.PrefetchScalarGridSpec(
            num_scalar_prefetch=2,
            grid=(F // tf, rt),
            in_specs=[pl.BlockSpec((ts, D // 2), lambda n, r, e, total: (live(r, total), 0)),
                      pl.BlockSpec((None, D, tf), lambda n, r, e, total: (e[live(r, total)], 0, n)),
                      pl.BlockSpec((None, D, tf), lambda n, r, e, total: (e[live(r, total)], 0, n))],
            out_specs=pl.BlockSpec((ts, tf), lambda n, r, e, total: (r, n)),
            scratch_shapes=[pltpu.VMEM((D, tf), BF16), pltpu.VMEM((D, tf), BF16)],
        ),
        out_shape=jax.ShapeDtypeStruct((R, F), BF16),
        compiler_params=_cparams("arbitrary", "arbitrary"),
        name="moe_up",
    )(tiles["e"], tiles["total"], xs, w1, w3)


def _moe_down_kernel(e_r, total, a_ref, w2_ref, o_ref, w2b_ref):
    r = pl.program_id(0)
    live = r < total[0]

    @pl.when(live & ((r == 0) | (e_r[r] != e_r[jnp.maximum(r - 1, 0)])))
    def _():
        w2b_ref[...] = w2_ref[...].astype(BF16)

    @pl.when(live)
    def _():
        o_ref[...] = _pack_bf16_pairs(_dot(a_ref[...], w2b_ref[...]))


def moe_down(a, w2, tiles, rt):
    R, F = a.shape
    ts = MOE_TS
    D = w2.shape[-1]
    live = lambda r, total: jnp.minimum(r, total[0] - 1)
    return pl.pallas_call(
        _moe_down_kernel,
        grid_spec=pltpu.PrefetchScalarGridSpec(
            num_scalar_prefetch=2,
            grid=(rt,),
            in_specs=[pl.BlockSpec((ts, F), lambda r, e, total: (live(r, total), 0)),
                      pl.BlockSpec((None, F, D), lambda r, e, total: (e[live(r, total)], 0, 0))],
            out_specs=pl.BlockSpec((ts, D // 2), lambda r, e, total: (r, 0)),
            scratch_shapes=[pltpu.VMEM((F, D), BF16)],
        ),
        out_shape=jax.ShapeDtypeStruct((R, D // 2), jnp.uint32),
        compiler_params=_cparams("arbitrary"),
        name="moe_down",
    )(tiles["e"], tiles["total"], a, w2)


SC_WINDOW = 64


def _sc_mesh():
    return plsc.VectorSubcoreMesh(core_axis_name="core", subcore_axis_name="subcore")


def sc_scatter_rows2(x, idx_a, idx_b, n_out):
    n, d = x.shape
    steps = n // SC_WINDOW

    @pl.kernel(out_type=jax.ShapeDtypeStruct((n_out, d), x.dtype), mesh=_sc_mesh(), scratch_types=[])
    def kern(x_hbm, ia_hbm, ib_hbm, o_hbm):
        def body(x_vmem, ia_vmem, ib_vmem):
            pltpu.sync_copy(x_vmem, o_hbm.at[ia_vmem.at[0]])
            pltpu.sync_copy(x_vmem, o_hbm.at[ib_vmem.at[0]])

        pltpu.emit_pipeline(
            body,
            grid=(steps,),
            in_specs=[pl.BlockSpec((SC_WINDOW, d), index_map=lambda i: (i, 0)),
                      pl.BlockSpec((1, SC_WINDOW), index_map=lambda i: (i, 0)),
                      pl.BlockSpec((1, SC_WINDOW), index_map=lambda i: (i, 0))],
            out_specs=[],
            core_axis_name=("core", "subcore"),
            dimension_semantics=(pltpu.PARALLEL,),
        )(x_hbm, ia_hbm, ib_hbm)

    return kern(x, idx_a.reshape(steps, SC_WINDOW), idx_b.reshape(steps, SC_WINDOW))


def sc_gather_rows(x, idx):
    n = idx.shape[0]
    d = x.shape[1]
    steps = n // SC_WINDOW

    @pl.kernel(out_type=jax.ShapeDtypeStruct((n, d), x.dtype), mesh=_sc_mesh(), scratch_types=[])
    def kern(x_hbm, i_hbm, o_hbm):
        def body(i_vmem, o_vmem):
            pltpu.sync_copy(x_hbm.at[i_vmem.at[0]], o_vmem)

        pltpu.emit_pipeline(
            body,
            grid=(steps,),
            in_specs=[pl.BlockSpec((1, SC_WINDOW), index_map=lambda i: (i, 0))],
            out_specs=[pl.BlockSpec((SC_WINDOW, d), index_map=lambda i: (i, 0))],
            core_axis_name=("core", "subcore"),
            dimension_semantics=(pltpu.PARALLEL,),
        )(i_hbm, o_hbm)

    return kern(x, idx.reshape(steps, SC_WINDOW))


def _moe_finish_kernel(x_ref, g2_ref, ya_ref, yb_ref, gate_ref, rank_ref, nw_ref, o_ref, *, normalize):
    gate = gate_ref[...]
    chosen = rank_ref[...] >= 0.0
    lane = lax.broadcasted_iota(jnp.int32, gate.shape, 1).astype(F32)
    first = jnp.min(jnp.where(chosen, lane, float(LANES)), axis=-1, keepdims=True)
    last = jnp.max(jnp.where(chosen, lane, -1.0), axis=-1, keepdims=True)
    wa = jnp.sum(jnp.where(lane == first, gate, 0.0), axis=-1, keepdims=True)
    wb = jnp.sum(jnp.where(lane == last, gate, 0.0), axis=-1, keepdims=True)
    x = x_ref[...] + g2_ref[...] * (wa * _unpack_bf16_pairs(ya_ref[...]) + wb * _unpack_bf16_pairs(yb_ref[...]))
    if normalize:
        ms = jnp.mean(x * x, axis=-1, keepdims=True)
        x = x * lax.rsqrt(ms + NORM_EPS) * nw_ref[...]
    o_ref[...] = x


def moe_finish(x, mod_l, y2, gate, rank, norm_w, T, *, tm=512):
    M, D = x.shape
    per_b = T // tm
    normalize = norm_w is not None
    if norm_w is None:
        norm_w = jnp.ones((1, D), F32)
    return pl.pallas_call(
        functools.partial(_moe_finish_kernel, normalize=normalize),
        grid=(M // tm,),
        in_specs=[pl.BlockSpec((tm, D), lambda i: (i, 0)),
                  pl.BlockSpec((None, None, 1, D), lambda i: (i // per_b, 5, 0, 0)),
                  pl.BlockSpec((None, tm, D // 2), lambda i: (0, i, 0)),
                  pl.BlockSpec((None, tm, D // 2), lambda i: (1, i, 0)),
                  pl.BlockSpec((tm, LANES), lambda i: (i, 0)),
                  pl.BlockSpec((tm, LANES), lambda i: (i, 0)),
                  pl.BlockSpec((1, D), lambda i: (0, 0))],
        out_specs=pl.BlockSpec((tm, D), lambda i: (i, 0)),
        out_shape=jax.ShapeDtypeStruct((M, D), F32),
        compiler_params=_cparams("parallel"),
        name="moe_finish",
    )(x, mod_l, y2, y2, gate, rank, norm_w)


def moe_ffn(x, mod_l, nw, w_router, w1, w3, w2, T, norm_w=None):
    M = x.shape[0]
    ts = MOE_TS
    rt = (2 * M) // ts + N_EXPERTS
    h, gate, rank, cnt = router(x, mod_l, nw, w_router, T)
    i32 = jnp.int32
    counts = cnt[0, :N_EXPERTS].astype(i32)
    ntile = (counts + ts - 1) // ts
    tile_end = jnp.cumsum(ntile)
    row_off = (tile_end - ntile) * ts
    e_r = jnp.minimum(_count_le(tile_end, jnp.arange(rt, dtype=i32)), N_EXPERTS - 1)
    tiles = dict(e=e_r, total=tile_end[-1].reshape(1).astype(i32))
    rk = rank[:, :N_EXPERTS].astype(i32)
    pos = row_off[None, :] + rk
    pos_a = jnp.min(jnp.where(rk >= 0, pos, rt * ts), axis=1)
    pos_b = jnp.max(jnp.where(rk >= 0, pos, -1), axis=1)

    xs = sc_scatter_rows2(h, pos_a, pos_b, rt * ts)
    a = moe_up(xs, w1, w3, tiles, rt)
    y = moe_down(a, w2, tiles, rt)
    y2 = sc_gather_rows(y, jnp.concatenate([pos_a, pos_b])).reshape(2, M, -1)
    return moe_finish(x, mod_l, y2, gate, rank, norm_w, T)


def _final_norm_kernel(x_ref, w_ref, o_ref):
    x = x_ref[...]
    ms = jnp.mean(x * x, axis=-1, keepdims=True)
    o_ref[...] = x * lax.rsqrt(ms + NORM_EPS) * w_ref[...]


def final_norm(x, w, *, tm=1024):
    M, D = x.shape
    return pl.pallas_call(
        _final_norm_kernel,
        grid=(M // tm,),
        in_specs=[pl.BlockSpec((tm, D), lambda i: (i, 0)), pl.BlockSpec((1, D), lambda i: (0, 0))],
        out_specs=pl.BlockSpec((tm, D), lambda i: (i, 0)),
        out_shape=jax.ShapeDtypeStruct((M, D), F32),
        compiler_params=_cparams("parallel"),
        name="final_norm",
    )(x, w)


def nsa_constants(T):
    n_sel = T // SEL_LEN
    nsp = max(LANES, n_sel)
    ncp = T // CMP_STRIDE
    cmp_start = np.arange(ncp) * CMP_STRIDE
    sel_start = np.arange(nsp) * SEL_LEN
    ov = ((cmp_start[:, None] < sel_start[None, :] + SEL_LEN)
          & (cmp_start[:, None] + CMP_LEN > sel_start[None, :]))
    ov[(T - CMP_LEN) // CMP_STRIDE + 1:] = False
    ov[:, n_sel:] = False
    et_mat = ((np.arange(T)[:, None] // SEL_LEN) == np.arange(nsp)[None, :]) * SEL_BONUS
    return jnp.asarray(ov.T, BF16), jnp.asarray(et_mat, BF16)


def token_mixing(x, mod_l, lw, consts, B, T):
    M = B * T
    cos_t, sin_t, ov_t, e_mat, ret_consts, ex = consts
    l = lw["layer"]
    p1 = proj_rope(x, mod_l, lw["norm_mix"], lw["w1"], l, cos_t, sin_t, p1_scales(), T).reshape(B, T, P1_COLS)
    p2 = proj_plain(x, mod_l, lw["norm_mix"], lw["w2"], l, T).reshape(B, T, P2_COLS)

    def group_rows(a):
        return a.reshape(B, T, NSA_GROUPS, HEAD_DIM).transpose(0, 2, 1, 3).reshape(
            B, NSA_GROUPS, T // CMP_STRIDE, CMP_STRIDE * HEAD_DIM)

    xr = jnp.stack([group_rows(p1[:, :, P1_NKC:P1_NKC + LANES]), group_rows(p2[:, :, P2_NVC:P2_NVC + LANES])])
    cmp_out = compress(xr, lw["cmp_pe"], lw["cmp_w1"], lw["cmp_w2"])
    cmp_out = cmp_out.transpose(0, 1, 3, 2, 4).reshape(2, B, T // CMP_STRIDE, LANES)
    o_cmp, sel, o_win = nsa_cmp_select_window(p1, p2, cmp_out[0], cmp_out[1], ov_t, T)
    o_sel = nsa_selected(p1, p2, sel, e_mat, T)

    o_ret = retention(p1, p2, ret_consts, T)

    ff = p2[:, :, P2_SMALL + 3 * NSA_HEADS:P2_SMALL + 3 * NSA_HEADS + FOX_HEADS].astype(F32)
    ff = ff.transpose(0, 2, 1).reshape(B, FOX_HEADS, T // LANES, LANES)
    cum = fox_cum(ff, lw["fox_bias"]).reshape(B, FOX_HEADS // 2, 2, 1, T)
    o_fox = fox_attention(p2, cum, T)

    return readout(o_cmp.reshape(M, -1), o_sel.reshape(M, -1), o_win.reshape(M, -1), p2.reshape(M, P2_COLS),
                   o_ret.reshape(M, -1), o_fox.reshape(M, -1), x, mod_l, ex,
                   lw["wn"], lw["wr"], lw["wf"], lw["wo"], l, T)


def mixer_weights(norm_mix, w_in, cmp_k_pe, cmp_k_w1, cmp_k_w2, cmp_v_pe, cmp_v_w1, cmp_v_w2, fox_f_bias,
                  w_read_nsa, w_read_ret, w_read_fox, w_out):
    depth = w_in.shape[0]
    w1, w2 = split_w_in(w_in)
    pe = jnp.stack([cmp_k_pe.reshape(depth, 1, -1), cmp_v_pe.reshape(depth, 1, -1)], axis=1)
    pe = jnp.broadcast_to(pe, (depth, 2, 8, pe.shape[-1])).astype(BF16)
    shared = {
        "w1": w1, "w2": w2,
        "wn": w_read_nsa.astype(BF16),
        "wr": w_read_ret.astype(BF16),
        "wf": w_read_fox.astype(BF16),
        "wo": w_out.astype(BF16),
    }
    cmp_w1 = jnp.stack([cmp_k_w1, cmp_v_w1], axis=1).astype(BF16)
    cmp_w2 = jnp.stack([cmp_k_w2, cmp_v_w2], axis=1).astype(BF16)
    return [dict(shared, layer=l, norm_mix=norm_mix[l].reshape(1, -1), cmp_pe=pe[l], cmp_w1=cmp_w1[l], cmp_w2=cmp_w2[l],
                 fox_bias=jnp.broadcast_to(fox_f_bias[l][:, None, None], (FOX_HEADS, 1, LANES)))
            for l in range(depth)]


def kernel(x, c, ada_w, ada_b, norm_mix, norm_ffn, w_in, cmp_k_pe, cmp_k_w1, cmp_k_w2, cmp_v_pe, cmp_v_w1,
           cmp_v_w2, fox_f_bias, w_read_nsa, w_read_ret, w_read_fox, w_out, ffn_w1, ffn_w3, ffn_w2, router_w,
           moe_w1, moe_w3, moe_w2, final_norm_w):
    B, T, D = x.shape
    M = B * T
    depth = ada_w.shape[0]
    mod = modulation(c, ada_w, ada_b)
    cos_t, sin_t = rope_tables(T)
    ov_t, e_mat = nsa_constants(T)
    consts = (cos_t, sin_t, ov_t, e_mat, retention_consts(), nsa_gate_expand())
    xs = x.reshape(M, D)
    lws = mixer_weights(norm_mix, w_in, cmp_k_pe, cmp_k_w1, cmp_k_w2, cmp_v_pe, cmp_v_w1, cmp_v_w2,
                        fox_f_bias, w_read_nsa, w_read_ret, w_read_fox, w_out)
    for l in range(depth):
        xs = token_mixing(xs, mod[l], lws[l], consts, B, T)
        nf = norm_ffn[l].reshape(1, D)
        if l % 2 == 0:
            k = l // 2
            xs = ffn(xs, mod[l], nf, ffn_w1[k].astype(BF16), ffn_w3[k].astype(BF16), ffn_w2[k].astype(BF16), T)
        else:
            k = l // 2
            fuse = final_norm_w.reshape(1, D) if l == depth - 1 else None
            xs = moe_ffn(xs, mod[l], nf, router_w[k], moe_w1[k], moe_w3[k], moe_w2[k], T, fuse)
    if depth % 2 == 1:
        xs = final_norm(xs, final_norm_w.reshape(1, D))
    return xs.reshape(B, T, D)
```

```python
import functools

import jax
import jax.numpy as jnp
import numpy as np
from jax import lax
from jax.experimental import pallas as pl
from jax.experimental.pallas import tpu as pltpu
from jax.experimental.pallas import tpu_sc as plsc

F32 = jnp.float32
BF16 = jnp.bfloat16

D_MODEL = 1024
HEAD_DIM = 64
ROPE_THETA = 10000.0
NORM_EPS = 1e-6
NEG_INF = -1e30
REMOVED = -3e38

NSA_HEADS = 8
NSA_GROUPS = 2
NSA_HPG = NSA_HEADS // NSA_GROUPS
CMP_LEN = 32
CMP_STRIDE = 16
SEL_LEN = 64
SEL_TOPN = 16
WINDOW = 512
NSA_QBLOCK = 256

RET_HEADS = 4
RET_QK_DIM = 64
RET_CHUNK = 128

FOX_HEADS = 8
FOX_TQ = 1024
LOG2E = 1.4426950408889634

N_EXPERTS = 8

LANES = 128
VMEM_LIMIT = 56 * 1024 * 1024

P1_NQ = 0
P1_RQ = 512
P1_RK = 768
P1_NKC = 1024
P1_NKS = 1152
P1_NKW = 1280
P1_COLS = 1408
P2_MG = 0
P2_RV = 3072
P2_RG = 3584
P2_FQ = 4096
P2_FK = 4608
P2_FV = 5120
P2_NVC = 5632
P2_NVS = 5760
P2_NVW = 5888
P2_SMALL = 6016
P2_COLS = 6144
NSA_OUT = NSA_HEADS * HEAD_DIM


def _layer_spec(w, l):
    zeros = (0,) * (w.ndim - 1)
    return pl.BlockSpec((None,) + w.shape[1:], lambda *_: (l,) + zeros)


def _cparams(*sem):
    return pltpu.CompilerParams(dimension_semantics=tuple(sem), vmem_limit_bytes=VMEM_LIMIT)


def _sigmoid(x):
    return 1.0 / (1.0 + jnp.exp(-x))


def _dot(a, b):
    return jnp.dot(a, b, preferred_element_type=F32)


def _dot_nt(a, b):
    return lax.dot_general(a, b, (((1,), (1,)), ((), ())), preferred_element_type=F32)


def _dot_tn(a, b):
    return lax.dot_general(a, b, (((0,), (0,)), ((), ())), preferred_element_type=F32)


def _split3(x):
    hi = x.astype(BF16)
    r1 = x - hi.astype(F32)
    mid = r1.astype(BF16)
    lo = (r1 - mid.astype(F32)).astype(BF16)
    return hi, mid, lo


def _pack_bf16_pairs(x):
    c = x.shape[1] // 2
    lo = pltpu.bitcast(x[:, :c].astype(BF16).astype(F32), jnp.uint32) >> 16
    hi = pltpu.bitcast(x[:, c:].astype(BF16).astype(F32), jnp.uint32) & jnp.uint32(0xFFFF0000)
    return hi | lo


def _unpack_bf16_pairs(u):
    lo = pltpu.bitcast(u << 16, F32)
    hi = pltpu.bitcast(u & jnp.uint32(0xFFFF0000), F32)
    return jnp.concatenate([lo, hi], axis=1)


def _norm_mod(x, nw, sc, sh):
    ms = jnp.mean(x * x, axis=-1, keepdims=True)
    y = x * lax.rsqrt(ms + NORM_EPS) * nw
    return y * (1.0 + sc) + sh


def _mod_kernel(c_ref, w_ref, b_ref, o_ref):
    c = c_ref[...]
    s = c * _sigmoid(c)
    o_ref[0] = _dot(s.astype(BF16), w_ref[0].astype(BF16)) + b_ref[0]


def modulation(c, ada_w, ada_b):
    B, D = c.shape
    depth = ada_w.shape[0]
    rows = 8
    c_pad = jnp.zeros((rows, D), F32).at[:B].set(c)
    out = pl.pallas_call(
        _mod_kernel,
        grid=(depth, 6),
        in_specs=[pl.BlockSpec((rows, D), lambda l, j: (0, 0)),
                  pl.BlockSpec((1, D, D), lambda l, j: (l, 0, j)),
                  pl.BlockSpec((1, 1, D), lambda l, j: (l, 0, j))],
        out_specs=pl.BlockSpec((1, rows, D), lambda l, j: (l, 0, j)),
        out_shape=jax.ShapeDtypeStruct((depth, rows, 6 * D), F32),
        compiler_params=_cparams("parallel", "parallel"),
        name="modulation",
    )(c_pad, ada_w, ada_b.reshape(depth, 1, 6 * D))
    return out[:, :B].reshape(depth, B, 6, 1, D)


def _proj_plain_kernel(x_ref, nw_ref, sc_ref, sh_ref, w_ref, o_ref, *, tn):
    h = _norm_mod(x_ref[...], nw_ref[...], sc_ref[...], sh_ref[...]).astype(BF16)
    for n in range(w_ref.shape[0] // tn):
        cols = slice(n * tn, (n + 1) * tn)
        o_ref[:, cols] = _dot_nt(h, w_ref[cols, :]).astype(o_ref.dtype)


def _proj_rope_kernel(x_ref, nw_ref, sc_ref, sh_ref, w_ref, cos_ref, sin_ref, o_ref, *, scales):
    h = _norm_mod(x_ref[...], nw_ref[...], sc_ref[...], sh_ref[...]).astype(BF16)
    y = _dot_nt(h, w_ref[...])
    cos = cos_ref[...]
    sin = sin_ref[...]
    lane = lax.broadcasted_iota(jnp.int32, cos.shape, 1)
    first_half = (lane % HEAD_DIM) < (HEAD_DIM // 2)
    for g, scale in enumerate(scales):
        yg = y[:, g * LANES:(g + 1) * LANES]
        rot = jnp.where(first_half, pltpu.roll(yg, LANES - HEAD_DIM // 2, 1),
                        pltpu.roll(yg, HEAD_DIM // 2, 1))
        r = yg * cos + rot * sin
        if scale != 1.0:
            r = r * scale
        o_ref[:, g * LANES:(g + 1) * LANES] = r.astype(o_ref.dtype)


def _mod_specs(T, tm, sc_idx, sh_idx, nargs):
    per_b = T // tm
    if nargs == 1:
        return [pl.BlockSpec((None, None, 1, D_MODEL), lambda i: (i // per_b, sc_idx, 0, 0)),
                pl.BlockSpec((None, None, 1, D_MODEL), lambda i: (i // per_b, sh_idx, 0, 0))]
    return [pl.BlockSpec((None, None, 1, D_MODEL), lambda i, j: (i // per_b, sc_idx, 0, 0)),
            pl.BlockSpec((None, None, 1, D_MODEL), lambda i, j: (i // per_b, sh_idx, 0, 0))]


def proj_plain(x, mod_l, nw, w, l, T, *, tm=512, tn=512):
    M, D = x.shape
    N = w.shape[1]
    return pl.pallas_call(
        functools.partial(_proj_plain_kernel, tn=tn),
        grid=(M // tm,),
        in_specs=[pl.BlockSpec((tm, D), lambda i: (i, 0)),
                  pl.BlockSpec((1, D), lambda i: (0, 0))]
        + _mod_specs(T, tm, 1, 0, 1)
        + [_layer_spec(w, l)],
        out_specs=pl.BlockSpec((tm, N), lambda i: (i, 0)),
        out_shape=jax.ShapeDtypeStruct((M, N), BF16),
        compiler_params=_cparams("parallel"),
        name="proj_plain",
    )(x, nw, mod_l, mod_l, w)


def proj_rope(x, mod_l, nw, w, l, cos, sin, scales, T, *, tm=512):
    M, D = x.shape
    N = w.shape[1]
    per_b = T // tm
    return pl.pallas_call(
        functools.partial(_proj_rope_kernel, scales=scales),
        grid=(M // tm,),
        in_specs=[pl.BlockSpec((tm, D), lambda i: (i, 0)),
                  pl.BlockSpec((1, D), lambda i: (0, 0))]
        + _mod_specs(T, tm, 1, 0, 1)
        + [_layer_spec(w, l),
           pl.BlockSpec((tm, LANES), lambda i: (i % per_b, 0)),
           pl.BlockSpec((tm, LANES), lambda i: (i % per_b, 0))],
        out_specs=pl.BlockSpec((tm, N), lambda i: (i, 0)),
        out_shape=jax.ShapeDtypeStruct((M, N), BF16),
        compiler_params=_cparams("parallel"),
        name="proj_rope",
    )(x, nw, mod_l, mod_l, w, cos, sin)


def rope_tables(T):
    d = HEAD_DIM
    inv = ROPE_THETA ** (-np.arange(0, d, 2, dtype=np.float64) / d)
    ang = np.arange(T, dtype=np.float64)[:, None] * inv[None, :]
    cos = np.cos(ang)
    sin = np.sin(ang)
    cos_t = np.concatenate([cos, cos, cos, cos], axis=-1)
    sin_t = np.concatenate([-sin, sin, -sin, sin], axis=-1)
    return jnp.asarray(cos_t, F32), jnp.asarray(sin_t, F32)


def split_w_in(w_in):
    sizes = [512, 128, 128, 128, 128, 128, 128, 24, 256, 256, 512, 512, 512, 512, 512, 8, 3072]
    offs = np.cumsum([0] + sizes)
    wb = jnp.swapaxes(w_in, 1, 2).astype(BF16)
    (nq, nkc, nvc, nks, nvs, nkw, nvw, ngate, rq, rk, rv, rg, fq, fk, fv, ff, mg) = [
        wb[:, offs[i]:offs[i + 1], :] for i in range(len(sizes))]
    small = jnp.concatenate([ngate, ff, jnp.zeros((wb.shape[0], LANES - 32, wb.shape[2]), BF16)], axis=1)
    w1 = jnp.concatenate([nq, rq, rk, nkc, nks, nkw], axis=1)
    w2 = jnp.concatenate([mg, rv, rg, fq, fk, fv, nvc, nvs, nvw, small], axis=1)
    assert w1.shape[1] == P1_COLS and w2.shape[1] == P2_COLS
    return w1, w2


def p1_scales():
    s = [1.0] * (P1_COLS // LANES)
    for g in range(P1_NQ // LANES, P1_RQ // LANES):
        s[g] = HEAD_DIM ** -0.5 * LOG2E
    for g in range(P1_RK // LANES, P1_NKC // LANES):
        s[g] = RET_QK_DIM ** -0.5
    return tuple(s)


def _compress_kernel(x_ref, pe_ref, w1_ref, w2_ref, o_ref):
    r = x_ref[...]
    half = r.shape[1]
    w1 = w1_ref[...]
    a = _dot(r, w1[:half])
    b = _dot(r, w1[half:])
    pe = _dot(pe_ref[...], w1)[0:1]
    n = a.shape[0]
    hid = a + pltpu.roll(b, n - 1, 0) + pe
    hid = hid * _sigmoid(hid)
    o_ref[...] = _dot(hid.astype(BF16), w2_ref[...]).astype(o_ref.dtype)


def compress(xr, pe, w1, w2):
    _, B, R, W = xr.shape
    G = w1.shape[1]
    H = w1.shape[-1]
    return pl.pallas_call(
        _compress_kernel,
        grid=(2, B, G),
        in_specs=[pl.BlockSpec((None, None, R, W), lambda s, b, g: (s, b, 0, 0)),
                  pl.BlockSpec((None, None, 8, 2 * W), lambda s, b, g: (s, g, 0, 0)),
                  pl.BlockSpec((None, None, 2 * W, H), lambda s, b, g: (s, g, 0, 0)),
                  pl.BlockSpec((None, H, HEAD_DIM), lambda s, b, g: (s, 0, 0))],
        out_specs=pl.BlockSpec((None, None, None, R, HEAD_DIM), lambda s, b, g: (s, b, g, 0, 0)),
        out_shape=jax.ShapeDtypeStruct((2, B, G, R, HEAD_DIM), BF16),
        compiler_params=_cparams("parallel", "parallel", "parallel"),
        name="nsa_compress",
    )(xr, pe, w1, w2)


def _stack_heads(q_ref, g):
    tq = q_ref.shape[0]
    half = lax.broadcasted_iota(jnp.int32, (tq, LANES), 1) // HEAD_DIM
    rows = []
    for hh in range(NSA_HPG):
        h = NSA_HPG * g + hh
        x = q_ref[:, (h // 2) * LANES:(h // 2 + 1) * LANES].astype(F32)
        if h % 2 != g:
            x = pltpu.roll(x, HEAD_DIM, 1)
        rows.append(jnp.where(half == g, x, 0.0).astype(BF16))
    return jnp.concatenate(rows, axis=0)


def _store_heads(o_ref, g, o, tq):
    low = lax.broadcasted_iota(jnp.int32, (tq, LANES), 1) < HEAD_DIM
    for pair in range(NSA_HPG // 2):
        even = o[(2 * pair) * tq:(2 * pair + 1) * tq]
        odd = o[(2 * pair + 1) * tq:(2 * pair + 2) * tq]
        if g == 0:
            blk = jnp.where(low, even, pltpu.roll(odd, HEAD_DIM, 1))
        else:
            blk = jnp.where(low, pltpu.roll(even, HEAD_DIM, 1), odd)
        col = (NSA_HPG // 2 * g + pair) * LANES
        o_ref[:, col:col + LANES] = blk.astype(o_ref.dtype)


CMP_CHUNK = 128


def _nsa_cmp_kernel(q_ref, kc_ref, vc_ref, ov_ref, o_ref, m_ref, imp_ref, *, tq, n_sel, top_n):
    t0 = pl.program_id(1) * tq
    ncp = kc_ref.shape[0]
    nsp = ov_ref.shape[0]
    rows = NSA_HPG * tq

    def attend(ncols):
        kc = kc_ref[0:ncols, :]
        vc = vc_ref[0:ncols, :]
        n_idx = lax.broadcasted_iota(jnp.int32, (rows, ncols), 1)
        t_idx = t0 + lax.broadcasted_iota(jnp.int32, (rows, ncols), 0) % tq
        valid = (n_idx * CMP_STRIDE + (CMP_LEN - 1)) <= t_idx
        for g in range(NSA_GROUPS):
            q = _stack_heads(q_ref, g)
            s = jnp.where(valid, _dot_nt(q, kc), NEG_INF)
            m = jnp.max(s, axis=-1, keepdims=True)
            e = jnp.exp2(s - m)
            l = jnp.sum(e, axis=-1, keepdims=True)
            p = e * jnp.where(m > 0.5 * NEG_INF, 1.0 / l, 0.0)
            _store_heads(o_ref, g, _dot(p.astype(BF16), vc), tq)
            psum = p[0:tq]
            for hh in range(1, NSA_HPG):
                psum = psum + p[hh * tq:(hh + 1) * tq]
            imp_ref[g] = _dot_nt(ov_ref[:, 0:ncols], psum.astype(BF16))

    n_live = jnp.maximum((t0 + tq - CMP_LEN) // CMP_STRIDE + 1, 1)
    n_chunks = jnp.minimum((n_live + CMP_CHUNK - 1) // CMP_CHUNK, ncp // CMP_CHUNK)
    for nc in range(1, ncp // CMP_CHUNK + 1):
        pl.when(n_chunks == nc)(functools.partial(attend, nc * CMP_CHUNK))

    j_idx = lax.broadcasted_iota(jnp.int32, (nsp, tq), 0)
    cur = (t0 + lax.broadcasted_iota(jnp.int32, (nsp, tq), 1)) // SEL_LEN
    forced = (j_idx == 0) | (j_idx == cur) | (j_idx == cur - 1)
    j_f = j_idx.astype(F32)
    for g in range(NSA_GROUPS):
        score = jnp.where(j_idx <= cur, imp_ref[g], NEG_INF)
        score = jnp.where(forced | (j_idx >= n_sel), REMOVED, score)
        sel = jnp.where(forced, 1.0, 0.0)
        for _ in range(max(top_n - 3, 0)):
            mx = jnp.max(score, axis=0, keepdims=True)
            idx = jnp.min(jnp.where(score == mx, j_f, float(nsp)), axis=0, keepdims=True)
            hit = j_f == idx
            sel = jnp.where(hit, 1.0, sel)
            score = jnp.where(hit, REMOVED, score)
        sel = jnp.where(j_idx <= cur, sel, 0.0)
        m_ref[g] = sel.T.astype(m_ref.dtype)


SEL_BONUS = 32768.0
NSA_SEL_TQ = 256
NSA_SEL_TK = 1024


def _nsa_sel_kernel(q_ref, k_ref, vin_ref, m_ref, et_ref, o_ref, v_ref, *, tq, tk):
    @pl.when(pl.program_id(1) == 0)
    def _():
        v = vin_ref[...]
        low = lax.broadcasted_iota(jnp.int32, v.shape, 1) < HEAD_DIM
        one = jnp.ones_like(v)
        v_ref[0] = jnp.where(low, v, one)
        v_ref[1] = jnp.where(low, one, v)

    t0 = pl.program_id(1) * tq
    n_full = t0 // tk
    rows = NSA_HPG * tq

    def update(carry, q, ks, vs, mask=None):
        m, acc = carry
        s = _dot_nt(q, ks)
        if mask is not None:
            s = jnp.where(mask, s, NEG_INF)
        m_new = jnp.maximum(m, jnp.max(s, axis=-1, keepdims=True))
        p = jnp.exp2(s - m_new)
        return m_new, jnp.exp2(m - m_new) * acc + _dot(p.astype(BF16), vs)

    qs, carries = [], []
    for g in range(NSA_GROUPS):
        q = jnp.concatenate([_stack_heads(q_ref, g), jnp.concatenate([m_ref[g]] * NSA_HPG, axis=0)], axis=1)

        def step(j, carry, q=q, g=g):
            start = pl.multiple_of(j * tk, tk)
            ks = jnp.concatenate([k_ref[pl.ds(start, tk), :], et_ref[pl.ds(start, tk), :]], axis=1)
            return update(carry, q, ks, v_ref[g, pl.ds(start, tk), :])

        init = (jnp.full((rows, 1), NEG_INF, F32), jnp.zeros((rows, LANES), F32))
        qs.append(q)
        carries.append(lax.fori_loop(0, n_full, step, init))

    start = pl.multiple_of(n_full * tk, tk)

    def tail(nk):
        trow = t0 + lax.broadcasted_iota(jnp.int32, (rows, nk), 0) % tq
        causal = start + lax.broadcasted_iota(jnp.int32, (rows, nk), 1) <= trow
        ks = jnp.concatenate([k_ref[pl.ds(start, nk), :], et_ref[pl.ds(start, nk), :]], axis=1)
        for g in range(NSA_GROUPS):
            _, acc = update(carries[g], qs[g], ks, v_ref[g, pl.ds(start, nk), :], causal)
            den = HEAD_DIM * (1 - g)
            _store_heads(o_ref, g, acc / acc[:, den:den + 1], tq)

    which = (t0 - start) // tq
    for v in range(tk // tq):
        pl.when(which == v)(functools.partial(tail, (v + 1) * tq))


def nsa_selected(p1, p2, sel, et_mat, T, *, tq=NSA_SEL_TQ, tk=NSA_SEL_TK):
    B = p1.shape[0]
    nsp = sel.shape[-1]
    return pl.pallas_call(
        functools.partial(_nsa_sel_kernel, tq=tq, tk=tk),
        grid=(B, T // tq),
        in_specs=[pl.BlockSpec((None, tq, NSA_HEADS * HEAD_DIM), lambda b, i: (b, i, 0)),
                  pl.BlockSpec((None, T, LANES), lambda b, i: (b, 0, P1_NKS // LANES)),
                  pl.BlockSpec((None, T, LANES), lambda b, i: (b, 0, P2_NVS // LANES)),
                  pl.BlockSpec((None, NSA_GROUPS, tq, nsp), lambda b, i: (b, 0, i, 0)),
                  pl.BlockSpec((T, nsp), lambda b, i: (0, 0))],
        out_specs=pl.BlockSpec((None, tq, NSA_OUT), lambda b, i: (b, i, 0)),
        out_shape=jax.ShapeDtypeStruct((B, T, NSA_OUT), BF16),
        scratch_shapes=[pltpu.VMEM((NSA_GROUPS, T, LANES), BF16)],
        compiler_params=_cparams("parallel", "arbitrary"),
        name="nsa_selected",
    )(p1, p1, p2, sel, et_mat)


def _nsa_win_kernel(q_ref, k_ref, v_ref, b_ref, o_ref, *, tq):
    t0 = pl.program_id(1) * tq
    span = WINDOW + tq
    start = pl.multiple_of(jnp.maximum(t0 - WINDOW, 0), tq)
    ks = k_ref[pl.ds(start, span), :]
    vs = v_ref[pl.ds(start, span), :]

    def run(bias):
        bias = jnp.concatenate([bias] * NSA_HPG, axis=0)
        for g in range(NSA_GROUPS):
            s = _dot_nt(_stack_heads(q_ref, g), ks) + bias
            m = jnp.max(s, axis=-1, keepdims=True)
            p = jnp.exp2(s - m)
            l = jnp.sum(p, axis=-1, keepdims=True)
            _store_heads(o_ref, g, _dot(p.astype(BF16), vs) / l, tq)

    @pl.when(t0 >= WINDOW)
    def _():
        run(b_ref[...])

    @pl.when(t0 < WINDOW)
    def _():
        row = lax.broadcasted_iota(jnp.int32, (tq, span), 0)
        col = lax.broadcasted_iota(jnp.int32, (tq, span), 1)
        run(jnp.where(col <= t0 + row, 0.0, NEG_INF))


def _nsa_cmp_win_kernel(q_ref, kc_ref, vc_ref, ov_ref, kw_ref, vw_ref, band_ref, ocmp_ref, m_ref, owin_ref, imp_ref,
                        *, tq, n_sel, top_n):
    _nsa_cmp_kernel(q_ref, kc_ref, vc_ref, ov_ref, ocmp_ref, m_ref, imp_ref, tq=tq, n_sel=n_sel, top_n=top_n)
    _nsa_win_kernel(q_ref, kw_ref, vw_ref, band_ref, owin_ref, tq=tq)


def nsa_cmp_select_window(p1, p2, kc, vc, ov_t, T):
    B = p1.shape[0]
    tq = NSA_QBLOCK
    ncp = kc.shape[1]
    nsp = ov_t.shape[0]
    n_sel = T // SEL_LEN
    span = WINDOW + tq
    r = np.arange(tq)[:, None]
    c = np.arange(span)[None, :]
    band = jnp.asarray(np.where((c > r) & (c <= r + WINDOW), 0.0, NEG_INF), F32)
    out_blk = pl.BlockSpec((None, tq, NSA_OUT), lambda b, i: (b, i, 0))
    return pl.pallas_call(
        functools.partial(_nsa_cmp_win_kernel, tq=tq, n_sel=n_sel, top_n=min(SEL_TOPN, n_sel)),
        grid=(B, T // tq),
        in_specs=[pl.BlockSpec((None, tq, NSA_HEADS * HEAD_DIM), lambda b, i: (b, i, 0)),
                  pl.BlockSpec((None, ncp, LANES), lambda b, i: (b, 0, 0)),
                  pl.BlockSpec((None, ncp, LANES), lambda b, i: (b, 0, 0)),
                  pl.BlockSpec((nsp, ncp), lambda b, i: (0, 0)),
                  pl.BlockSpec((None, T, LANES), lambda b, i: (b, 0, P1_NKW // LANES)),
                  pl.BlockSpec((None, T, LANES), lambda b, i: (b, 0, P2_NVW // LANES)),
                  pl.BlockSpec((tq, span), lambda b, i: (0, 0))],
        out_specs=[out_blk, pl.BlockSpec((None, NSA_GROUPS, tq, nsp), lambda b, i: (b, 0, i, 0)), out_blk],
        out_shape=[jax.ShapeDtypeStruct((B, T, NSA_OUT), BF16),
                   jax.ShapeDtypeStruct((B, NSA_GROUPS, T, nsp), BF16),
                   jax.ShapeDtypeStruct((B, T, NSA_OUT), BF16)],
        scratch_shapes=[pltpu.VMEM((NSA_GROUPS, nsp, tq), F32)],
        compiler_params=_cparams("parallel", "parallel"),
        name="nsa_cmp_select_window",
    )(p1, kc, vc, ov_t, p1, p2, band)


def _retention_kernel(q_ref, k_ref, v_ref, g_ref, din_ref, qd_ref, kd_ref, cd_ref, o_ref, st_ref):
    @pl.when(pl.program_id(0) == 0)
    def _():
        st_ref[...] = jnp.zeros_like(st_ref)

    B = q_ref.shape[0]
    C = RET_CHUNK
    half = lax.broadcasted_iota(jnp.int32, (C, LANES), 1) // HEAD_DIM
    for b in range(B):
        for h in range(RET_HEADS):
            lanes = slice(h * LANES, (h + 1) * LANES)
            pair = slice((h // 2) * LANES, (h // 2 + 1) * LANES)
            st = st_ref[b, h]
            for sub in range(q_ref.shape[1] // C):
                rows = slice(sub * C, (sub + 1) * C)
                qh = jnp.where(half == h % 2, q_ref[b, rows, pair], 0.0).astype(BF16)
                kp = k_ref[b, rows, pair]
                vh = v_ref[b, rows, lanes]
                inner = _dot_nt(qh, kp) * din_ref[h]
                o = _dot(inner.astype(BF16), vh) + _dot(qh, st.astype(BF16)) * qd_ref[h]
                kd = (kp.astype(F32) * kd_ref[h]).astype(BF16)
                st = st * cd_ref[h, 0:1, :] + _dot_tn(kd, vh)
                mu = jnp.mean(o, axis=-1, keepdims=True)
                d = o - mu
                var = jnp.mean(d * d, axis=-1, keepdims=True)
                on = d * lax.rsqrt(var + NORM_EPS)
                gh = g_ref[b, rows, lanes].astype(F32)
                o_ref[b, rows, lanes] = (gh * _sigmoid(gh) * on).astype(o_ref.dtype)
            st_ref[b, h] = st


def retention_consts():
    C = RET_CHUNK
    H = RET_HEADS
    log_g = np.log(1.0 - 2.0 ** (-5.0 - np.arange(H, dtype=np.float64)))
    n = np.arange(C, dtype=np.float64)
    diff = n[:, None] - n[None, :]
    causal = diff >= 0
    decay_in = np.where(causal[None], np.exp(np.where(causal, diff, 0.0)[None] * log_g[:, None, None]), 0.0)
    q_decay = np.exp((n[None, :] + 1.0) * log_g[:, None])
    k_decay = np.exp((C - 1.0 - n)[None, :] * log_g[:, None])
    chunk_decay = np.exp(C * log_g)
    qd = np.broadcast_to(q_decay[:, :, None], (H, C, LANES))
    kd = np.broadcast_to(k_decay[:, :, None], (H, C, LANES))
    cd = np.broadcast_to(chunk_decay[:, None, None], (H, 8, LANES))
    return tuple(jnp.asarray(a, F32) for a in (decay_in, qd, kd, cd))


RET_STEP = 4


def retention(p1, p2, consts, T):
    B = p1.shape[0]
    C = RET_CHUNK * RET_STEP
    din, qd, kd, cd = consts
    W = RET_HEADS * LANES
    full = lambda shape: pl.BlockSpec(shape, lambda c: (0,) * len(shape))
    return pl.pallas_call(
        _retention_kernel,
        grid=(T // C,),
        in_specs=[pl.BlockSpec((B, C, W // 2), lambda c: (0, c, P1_RQ // (W // 2))),
                  pl.BlockSpec((B, C, W // 2), lambda c: (0, c, P1_RK // (W // 2))),
                  pl.BlockSpec((B, C, W), lambda c: (0, c, P2_RV // W)),
                  pl.BlockSpec((B, C, W), lambda c: (0, c, P2_RG // W)),
                  full(din.shape), full(qd.shape), full(kd.shape), full(cd.shape)],
        out_specs=pl.BlockSpec((B, C, W), lambda c: (0, c, 0)),
        out_shape=jax.ShapeDtypeStruct((B, T, W), BF16),
        scratch_shapes=[pltpu.VMEM((B, RET_HEADS, LANES, LANES), F32)],
        compiler_params=_cparams("arbitrary"),
        name="retention",
    )(p1, p1, p2, p2, din, qd, kd, cd)


def _fox_cum_kernel(f_ref, b_ref, o_ref):
    x = f_ref[...] + b_ref[...]
    ls = jnp.minimum(x, 0.0) - jnp.log1p(jnp.exp(-jnp.abs(x)))
    R = x.shape[0]
    ki = lax.broadcasted_iota(jnp.int32, (LANES, LANES), 0)
    ji = lax.broadcasted_iota(jnp.int32, (LANES, LANES), 1)
    upper = jnp.where(ki <= ji, 1.0, 0.0).astype(BF16)
    hi, mid, lo = _split3(ls)
    rowcum = _dot(hi, upper) + _dot(mid, upper) + _dot(lo, upper)
    tot = jnp.broadcast_to(rowcum[:, LANES - 1:LANES], (R, LANES))
    ri = lax.broadcasted_iota(jnp.int32, (R, R), 0)
    ci = lax.broadcasted_iota(jnp.int32, (R, R), 1)
    lower = jnp.where(ci < ri, 1.0, 0.0).astype(BF16)
    hi, mid, lo = _split3(tot)
    offs = _dot(lower, hi) + _dot(lower, mid) + _dot(lower, lo)
    o_ref[...] = (rowcum + offs) * LOG2E


def fox_cum(f_logit, bias):
    B, H, R, _ = f_logit.shape
    return pl.pallas_call(
        _fox_cum_kernel,
        grid=(B, H),
        in_specs=[pl.BlockSpec((None, None, R, LANES), lambda b, h: (b, h, 0, 0)),
                  pl.BlockSpec((None, 1, LANES), lambda b, h: (h, 0, 0))],
        out_specs=pl.BlockSpec((None, None, R, LANES), lambda b, h: (b, h, 0, 0)),
        out_shape=jax.ShapeDtypeStruct((B, H, R, LANES), F32),
        compiler_params=_cparams("parallel", "parallel"),
        name="fox_cum",
    )(f_logit, bias)


FOX_BIAS_LANES = 3


def _fox_kernel(q_ref, k_ref, v_ref, c_ref, o_ref, ka_ref, va_ref, *, tq):
    i = pl.program_id(2)
    tk = tq
    T = k_ref.shape[0]
    chunk = 512

    @pl.when(i == 0)
    def _():
        lane = lax.broadcasted_iota(jnp.int32, (chunk, LANES), 1)
        ri = lax.broadcasted_iota(jnp.int32, (16, LANES), 0)
        ci = lax.broadcasted_iota(jnp.int32, (16, LANES), 1)
        place = jnp.where((ci == ri + HEAD_DIM) & (ri < FOX_BIAS_LANES), 1.0, 0.0).astype(BF16)

        def build(c, _):
            c0 = pl.multiple_of(c * chunk, chunk)
            kp = k_ref[pl.ds(c0, chunk), :].astype(F32)
            vp = v_ref[pl.ds(c0, chunk), :].astype(F32)
            for hh in range(2):
                hi, mid, lo = _split3(-c_ref[hh, :, pl.ds(c0, chunk)])
                terms = jnp.concatenate([hi, mid, lo, jnp.zeros((13, chunk), BF16)], axis=0)
                bias = _dot_tn(terms, place)
                kh = kp if hh == 0 else pltpu.roll(kp, HEAD_DIM, 1)
                vh = vp if hh == 0 else pltpu.roll(vp, HEAD_DIM, 1)
                ka_ref[hh, pl.ds(c0, chunk), :] = jnp.where(lane < HEAD_DIM, kh, bias).astype(BF16)
                va_ref[hh, pl.ds(c0, chunk), :] = jnp.where(lane < HEAD_DIM, vh, 1.0).astype(BF16)
            return 0

        lax.fori_loop(0, T // chunk, build, 0)

    lane = lax.broadcasted_iota(jnp.int32, (tq, LANES), 1)
    ones_lanes = (lane >= HEAD_DIM) & (lane < HEAD_DIM + FOX_BIAS_LANES)
    qp = q_ref[...].astype(F32) * (HEAD_DIM ** -0.5 * LOG2E)
    qs = [jnp.where(lane < HEAD_DIM, qh, jnp.where(ones_lanes, 1.0, 0.0)).astype(BF16)
          for qh in (qp, pltpu.roll(qp, HEAD_DIM, 1))]

    def update(hh, m, acc, q, start, size, mask=None):
        s = _dot_nt(q, ka_ref[hh, pl.ds(start, size), :])
        if mask is not None:
            s = jnp.where(mask, s, NEG_INF)
        m_new = jnp.maximum(m, jnp.max(s, axis=-1, keepdims=True))
        p = jnp.exp2(s - m_new)
        return m_new, jnp.exp2(m - m_new) * acc + _dot(p.astype(BF16), va_ref[hh, pl.ds(start, size), :])

    def step(j, carry):
        start = pl.multiple_of(j * tk, tk)
        return tuple(update(hh, *carry[hh], qs[hh], start, tk) for hh in range(2))

    one = (jnp.full((tq, 1), NEG_INF, F32), jnp.zeros((tq, LANES), F32))
    carry = lax.fori_loop(0, i, step, (one, one))

    half = tq // 2
    start = pl.multiple_of(i * tk, tk)
    row = lax.broadcasted_iota(jnp.int32, (tq, half), 0)
    col = lax.broadcasted_iota(jnp.int32, (tq, half), 1)
    accs = []
    for hh in range(2):
        m, acc = update(hh, *carry[hh], qs[hh], start, half, col <= row)
        _, low = update(hh, m[half:], acc[half:], qs[hh][half:], start + half, half, (col <= row)[:half])
        accs.append(jnp.concatenate([acc[:half], low], axis=0))
    acc0, acc1 = accs
    o0 = acc0 / acc0[:, HEAD_DIM:HEAD_DIM + 1]
    o1 = acc1 / acc1[:, HEAD_DIM:HEAD_DIM + 1]
    o_ref[...] = jnp.where(lane < HEAD_DIM, o0, pltpu.roll(o1, HEAD_DIM, 1)).astype(o_ref.dtype)


def fox_attention(p2, cum, T, *, tq=FOX_TQ):
    B = p2.shape[0]
    HP = FOX_HEADS // 2
    return pl.pallas_call(
        functools.partial(_fox_kernel, tq=tq),
        grid=(B, HP, T // tq),
        in_specs=[pl.BlockSpec((None, tq, LANES), lambda b, h, i: (b, i, P2_FQ // LANES + h)),
                  pl.BlockSpec((None, T, LANES), lambda b, h, i: (b, 0, P2_FK // LANES + h)),
                  pl.BlockSpec((None, T, LANES), lambda b, h, i: (b, 0, P2_FV // LANES + h)),
                  pl.BlockSpec((None, None, 2, 1, T), lambda b, h, i: (b, h, 0, 0, 0))],
        out_specs=pl.BlockSpec((None, tq, LANES), lambda b, h, i: (b, i, h)),
        out_shape=jax.ShapeDtypeStruct((B, T, FOX_HEADS * HEAD_DIM), BF16),
        scratch_shapes=[pltpu.VMEM((2, T, LANES), BF16), pltpu.VMEM((2, T, LANES), BF16)],
        compiler_params=_cparams("parallel", "parallel", "arbitrary"),
        name="fox_attention",
    )(p2, p2, p2, cum)


def _readout_kernel(ocmp_ref, osel_ref, owin_ref, small_ref, oret_ref, ofox_ref, mg_ref, x_ref, g1_ref,
                    ex_ref, wn_ref, wr_ref, wf_ref, wo_ref, o_ref):
    W = NSA_OUT
    gs = _sigmoid(small_ref[...].astype(F32)).astype(BF16)
    ge = _dot(gs, ex_ref[...])
    onsa = (ge[:, :W] * ocmp_ref[...].astype(F32) + ge[:, W:2 * W] * osel_ref[...].astype(F32)
            + ge[:, 2 * W:] * owin_ref[...].astype(F32))
    D = D_MODEL
    merged = (_sigmoid(mg_ref[:, :D].astype(F32)) * _dot(onsa.astype(BF16), wn_ref[...])
              + _sigmoid(mg_ref[:, D:2 * D].astype(F32)) * _dot(oret_ref[...], wr_ref[...])
              + _sigmoid(mg_ref[:, 2 * D:].astype(F32)) * _dot(ofox_ref[...], wf_ref[...]))
    y = _dot(merged.astype(BF16), wo_ref[...])
    o_ref[...] = x_ref[...] + g1_ref[...] * y


def readout(o_cmp, o_sel, o_win, p2, o_ret, o_fox, x, mod_l, ex, wn, wr, wf, wo, l, T, *, tm=512):
    M, D = x.shape
    per_b = T // tm
    W = NSA_OUT
    row = lambda width, col=0: pl.BlockSpec((tm, width), lambda i: (i, col))
    full = lambda a: pl.BlockSpec(a.shape, lambda i: (0,) * a.ndim)
    return pl.pallas_call(
        _readout_kernel,
        grid=(M // tm,),
        in_specs=[row(W), row(W), row(W), row(LANES, P2_SMALL // LANES), row(512), row(512),
                  row(3 * D, 0), row(D),
                  pl.BlockSpec((None, None, 1, D), lambda i: (i // per_b, 2, 0, 0)),
                  full(ex), _layer_spec(wn, l), _layer_spec(wr, l), _layer_spec(wf, l), _layer_spec(wo, l)],
        out_specs=row(D),
        out_shape=jax.ShapeDtypeStruct((M, D), F32),
        compiler_params=_cparams("parallel"),
        name="mixer_readout",
    )(o_cmp, o_sel, o_win, p2, o_ret, o_fox, p2, x, mod_l, ex, wn, wr, wf, wo)


def nsa_gate_expand():
    ex = np.zeros((LANES, 3 * NSA_OUT), np.float32)
    for br in range(3):
        for h in range(NSA_HEADS):
            c0 = br * NSA_OUT + h * HEAD_DIM
            ex[br * NSA_HEADS + h, c0:c0 + HEAD_DIM] = 1.0
    return jnp.asarray(ex, BF16)


FFN_CHUNK = 512


def _ffn_kernel(x_ref, nw_ref, sc_ref, sh_ref, g2_ref, w1_ref, w3_ref, w2_ref, o_ref):
    x = x_ref[...]
    h = _norm_mod(x, nw_ref[...], sc_ref[...], sh_ref[...]).astype(BF16)
    F = w1_ref.shape[1]
    y = None
    for c0 in range(0, F, FFN_CHUNK):
        cols = slice(c0, min(c0 + FFN_CHUNK, F))
        u = _dot(h, w1_ref[:, cols])
        v = _dot(h, w3_ref[:, cols])
        part = _dot((u * _sigmoid(u) * v).astype(BF16), w2_ref[cols, :])
        y = part if y is None else y + part
    o_ref[...] = x + g2_ref[...] * y


def ffn(x, mod_l, nw, w1, w3, w2, T, *, tm=512):
    M, D = x.shape
    F = w1.shape[1]
    per_b = T // tm
    modspec = lambda k: pl.BlockSpec((None, None, 1, D), lambda i: (i // per_b, k, 0, 0))
    full = lambda a: pl.BlockSpec(a.shape, lambda i: (0,) * a.ndim)
    return pl.pallas_call(
        _ffn_kernel,
        grid=(M // tm,),
        in_specs=[pl.BlockSpec((tm, D), lambda i: (i, 0)),
                  pl.BlockSpec((1, D), lambda i: (0, 0)),
                  modspec(4), modspec(3), modspec(5), full(w1), full(w3), full(w2)],
        out_specs=pl.BlockSpec((tm, D), lambda i: (i, 0)),
        out_shape=jax.ShapeDtypeStruct((M, D), F32),
        compiler_params=_cparams("parallel"),
        name="ffn_dense",
    )(x, nw, mod_l, mod_l, mod_l, w1, w3, w2)


MOE_TC = 512
MOE_TS = 512


def _router_kernel(x_ref, nw_ref, sc_ref, sh_ref, wh_ref, wl_ref, h_ref, gate_ref, rank_ref, cnt_ref, carry_ref):
    @pl.when(pl.program_id(0) == 0)
    def _():
        carry_ref[...] = jnp.zeros_like(carry_ref)

    h = _norm_mod(x_ref[...], nw_ref[...], sc_ref[...], sh_ref[...])
    hh = h.astype(BF16)
    h_ref[...] = _pack_bf16_pairs(hh.astype(F32))
    hl = (h - hh.astype(F32)).astype(BF16)
    logits = _dot(hh, wh_ref[...]) + (_dot(hl, wh_ref[...]) + _dot(hh, wl_ref[...]))
    tm = logits.shape[0]
    lane = lax.broadcasted_iota(jnp.int32, logits.shape, 1)
    logits = jnp.where(lane < N_EXPERTS, logits, REMOVED)
    lane_f = lane.astype(F32)
    v1 = jnp.max(logits, axis=-1, keepdims=True)
    i1 = jnp.min(jnp.where(logits == v1, lane_f, float(LANES)), axis=-1, keepdims=True)
    rest = jnp.where(lane_f == i1, REMOVED, logits)
    v2 = jnp.max(rest, axis=-1, keepdims=True)
    i2 = jnp.min(jnp.where(rest == v2, lane_f, float(LANES)), axis=-1, keepdims=True)
    e2 = jnp.exp(v2 - v1)
    w1 = 1.0 / (1.0 + e2)
    w2 = e2 / (1.0 + e2)
    gate_ref[...] = jnp.where(lane_f == i1, w1, jnp.where(lane_f == i2, w2, 0.0))

    sel = jnp.where((lane_f == i1) | (lane_f == i2), 1.0, 0.0)
    ri = lax.broadcasted_iota(jnp.int32, (tm, tm), 0)
    ci = lax.broadcasted_iota(jnp.int32, (tm, tm), 1)
    before = jnp.where(ci < ri, 1.0, 0.0).astype(BF16)
    rank = _dot(before, sel.astype(BF16)) + carry_ref[0:1, :]
    rank_ref[...] = jnp.where(sel > 0.0, rank, -1.0)
    carry_ref[...] = carry_ref[...] + jnp.sum(sel, axis=0, keepdims=True)
    cnt_ref[...] = carry_ref[...]


def router(x, mod_l, nw, w_router, T):
    M, D = x.shape
    tm = MOE_TC
    per_b = T // tm
    wp = jnp.zeros((D, LANES), F32).at[:, :N_EXPERTS].set(w_router)
    wh = wp.astype(BF16)
    wl = (wp - wh.astype(F32)).astype(BF16)
    return pl.pallas_call(
        _router_kernel,
        grid=(M // tm,),
        in_specs=[pl.BlockSpec((tm, D), lambda i: (i, 0)),
                  pl.BlockSpec((1, D), lambda i: (0, 0))]
        + _mod_specs(T, tm, 4, 3, 1)
        + [pl.BlockSpec((D, LANES), lambda i: (0, 0)),
           pl.BlockSpec((D, LANES), lambda i: (0, 0))],
        out_specs=[pl.BlockSpec((tm, D // 2), lambda i: (i, 0)),
                   pl.BlockSpec((tm, LANES), lambda i: (i, 0)),
                   pl.BlockSpec((tm, LANES), lambda i: (i, 0)),
                   pl.BlockSpec((8, LANES), lambda i: (0, 0))],
        out_shape=[jax.ShapeDtypeStruct((M, D // 2), jnp.uint32),
                   jax.ShapeDtypeStruct((M, LANES), F32),
                   jax.ShapeDtypeStruct((M, LANES), F32),
                   jax.ShapeDtypeStruct((8, LANES), F32)],
        scratch_shapes=[pltpu.VMEM((8, LANES), F32)],
        compiler_params=_cparams("arbitrary"),
        name="moe_router",
    )(x, nw, mod_l, mod_l, wh, wl)


def _count_le(sorted_vals, x):
    return jnp.sum(sorted_vals[None, :] <= x[:, None], axis=1, dtype=jnp.int32)


def _moe_up_kernel(e_r, total, x_ref, w1_ref, w3_ref, o_ref, w1b_ref, w3b_ref):
    r = pl.program_id(1)
    live = r < total[0]

    @pl.when(live & ((r == 0) | (e_r[r] != e_r[jnp.maximum(r - 1, 0)])))
    def _():
        w1b_ref[...] = w1_ref[...].astype(BF16)
        w3b_ref[...] = w3_ref[...].astype(BF16)

    @pl.when(live)
    def _():
        x = _unpack_bf16_pairs(x_ref[...]).astype(BF16)
        u = _dot(x, w1b_ref[...])
        v = _dot(x, w3b_ref[...])
        o_ref[...] = (u * _sigmoid(u) * v).astype(o_ref.dtype)


def moe_up(xs, w1, w3, tiles, rt, *, tf=1792):
    R = xs.shape[0]
    D = w1.shape[1]
    ts = MOE_TS
    F = w1.shape[-1]
    live = lambda r, total: jnp.minimum(r, total[0] - 1)
    return pl.pallas_call(
        _moe_up_kernel,
        grid_spec=pltpu.PrefetchScalarGridSpec(
            num_scalar_prefetch=2,
            grid=(F // tf, rt),
            in_specs=[pl.BlockSpec((ts, D // 2), lambda n, r, e, total: (live(r, total), 0)),
                      pl.BlockSpec((None, D, tf), lambda n, r, e, total: (e[live(r, total)], 0, n)),
                      pl.BlockSpec((None, D, tf), lambda n, r, e, total: (e[live(r, total)], 0, n))],
            out_specs=pl.BlockSpec((ts, tf), lambda n, r, e, total: (r, n)),
            scratch_shapes=[pltpu.VMEM((D, tf), BF16), pltpu.VMEM((D, tf), BF16)],
        ),
        out_shape=jax.ShapeDtypeStruct((R, F), BF16),
        compiler_params=_cparams("arbitrary", "arbitrary"),
        name="moe_up",
    )(tiles["e"], tiles["total"], xs, w1, w3)


def _moe_down_kernel(e_r, total, a_ref, w2_ref, o_ref, w2b_ref):
    r = pl.program_id(0)
    live = r < total[0]

    @pl.when(live & ((r == 0) | (e_r[r] != e_r[jnp.maximum(r - 1, 0)])))
    def _():
        w2b_ref[...] = w2_ref[...].astype(BF16)

    @pl.when(live)
    def _():
        o_ref[...] = _pack_bf16_pairs(_dot(a_ref[...], w2b_ref[...]))


def moe_down(a, w2, tiles, rt):
    R, F = a.shape
    ts = MOE_TS
    D = w2.shape[-1]
    live = lambda r, total: jnp.minimum(r, total[0] - 1)
    return pl.pallas_call(
        _moe_down_kernel,
        grid_spec=pltpu.PrefetchScalarGridSpec(
            num_scalar_prefetch=2,
            grid=(rt,),
            in_specs=[pl.BlockSpec((ts, F), lambda r, e, total: (live(r, total), 0)),
                      pl.BlockSpec((None, F, D), lambda r, e, total: (e[live(r, total)], 0, 0))],
            out_specs=pl.BlockSpec((ts, D // 2), lambda r, e, total: (r, 0)),
            scratch_shapes=[pltpu.VMEM((F, D), BF16)],
        ),
        out_shape=jax.ShapeDtypeStruct((R, D // 2), jnp.uint32),
        compiler_params=_cparams("arbitrary"),
        name="moe_down",
    )(tiles["e"], tiles["total"], a, w2)


SC_WINDOW = 64


def _sc_mesh():
    return plsc.VectorSubcoreMesh(core_axis_name="core", subcore_axis_name="subcore")


def sc_scatter_rows2(x, idx_a, idx_b, n_out):
    n, d = x.shape
    steps = n // SC_WINDOW

    @pl.kernel(out_type=jax.ShapeDtypeStruct((n_out, d), x.dtype), mesh=_sc_mesh(), scratch_types=[])
    def kern(x_hbm, ia_hbm, ib_hbm, o_hbm):
        def body(x_vmem, ia_vmem, ib_vmem):
            pltpu.sync_copy(x_vmem, o_hbm.at[ia_vmem.at[0]])
            pltpu.sync_copy(x_vmem, o_hbm.at[ib_vmem.at[0]])

        pltpu.emit_pipeline(
            body,
            grid=(steps,),
            in_specs=[pl.BlockSpec((SC_WINDOW, d), index_map=lambda i: (i, 0)),
                      pl.BlockSpec((1, SC_WINDOW), index_map=lambda i: (i, 0)),
                      pl.BlockSpec((1, SC_WINDOW), index_map=lambda i: (i, 0))],
            out_specs=[],
            core_axis_name=("core", "subcore"),
            dimension_semantics=(pltpu.PARALLEL,),
        )(x_hbm, ia_hbm, ib_hbm)

    return kern(x, idx_a.reshape(steps, SC_WINDOW), idx_b.reshape(steps, SC_WINDOW))


def sc_gather_rows(x, idx):
    n = idx.shape[0]
    d = x.shape[1]
    steps = n // SC_WINDOW

    @pl.kernel(out_type=jax.ShapeDtypeStruct((n, d), x.dtype), mesh=_sc_mesh(), scratch_types=[])
    def kern(x_hbm, i_hbm, o_hbm):
        def body(i_vmem, o_vmem):
            pltpu.sync_copy(x_hbm.at[i_vmem.at[0]], o_vmem)

        pltpu.emit_pipeline(
            body,
            grid=(steps,),
            in_specs=[pl.BlockSpec((1, SC_WINDOW), index_map=lambda i: (i, 0))],
            out_specs=[pl.BlockSpec((SC_WINDOW, d), index_map=lambda i: (i, 0))],
            core_axis_name=("core", "subcore"),
            dimension_semantics=(pltpu.PARALLEL,),
        )(i_hbm, o_hbm)

    return kern(x, idx.reshape(steps, SC_WINDOW))


def _moe_finish_kernel(x_ref, g2_ref, ya_ref, yb_ref, gate_ref, rank_ref, nw_ref, o_ref, *, normalize):
    gate = gate_ref[...]
    chosen = rank_ref[...] >= 0.0
    lane = lax.broadcasted_iota(jnp.int32, gate.shape, 1).astype(F32)
    first = jnp.min(jnp.where(chosen, lane, float(LANES)), axis=-1, keepdims=True)
    last = jnp.max(jnp.where(chosen, lane, -1.0), axis=-1, keepdims=True)
    wa = jnp.sum(jnp.where(lane == first, gate, 0.0), axis=-1, keepdims=True)
    wb = jnp.sum(jnp.where(lane == last, gate, 0.0), axis=-1, keepdims=True)
    x = x_ref[...] + g2_ref[...] * (wa * _unpack_bf16_pairs(ya_ref[...]) + wb * _unpack_bf16_pairs(yb_ref[...]))
    if normalize:
        ms = jnp.mean(x * x, axis=-1, keepdims=True)
        x = x * lax.rsqrt(ms + NORM_EPS) * nw_ref[...]
    o_ref[...] = x


def moe_finish(x, mod_l, y2, gate, rank, norm_w, T, *, tm=512):
    M, D = x.shape
    per_b = T // tm
    normalize = norm_w is not None
    if norm_w is None:
        norm_w = jnp.ones((1, D), F32)
    return pl.pallas_call(
        functools.partial(_moe_finish_kernel, normalize=normalize),
        grid=(M // tm,),
        in_specs=[pl.BlockSpec((tm, D), lambda i: (i, 0)),
                  pl.BlockSpec((None, None, 1, D), lambda i: (i // per_b, 5, 0, 0)),
                  pl.BlockSpec((None, tm, D // 2), lambda i: (0, i, 0)),
                  pl.BlockSpec((None, tm, D // 2), lambda i: (1, i, 0)),
                  pl.BlockSpec((tm, LANES), lambda i: (i, 0)),
                  pl.BlockSpec((tm, LANES), lambda i: (i, 0)),
                  pl.BlockSpec((1, D), lambda i: (0, 0))],
        out_specs=pl.BlockSpec((tm, D), lambda i: (i, 0)),
        out_shape=jax.ShapeDtypeStruct((M, D), F32),
        compiler_params=_cparams("parallel"),
        name="moe_finish",
    )(x, mod_l, y2, y2, gate, rank, norm_w)


def moe_ffn(x, mod_l, nw, w_router, w1, w3, w2, T, norm_w=None):
    M = x.shape[0]
    ts = MOE_TS
    rt = (2 * M) // ts + N_EXPERTS
    h, gate, rank, cnt = router(x, mod_l, nw, w_router, T)
    i32 = jnp.int32
    counts = cnt[0, :N_EXPERTS].astype(i32)
    ntile = (counts + ts - 1) // ts
    tile_end = jnp.cumsum(ntile)
    row_off = (tile_end - ntile) * ts
    e_r = jnp.minimum(_count_le(tile_end, jnp.arange(rt, dtype=i32)), N_EXPERTS - 1)
    tiles = dict(e=e_r, total=tile_end[-1].reshape(1).astype(i32))
    rk = rank[:, :N_EXPERTS].astype(i32)
    pos = row_off[None, :] + rk
    pos_a = jnp.min(jnp.where(rk >= 0, pos, rt * ts), axis=1)
    pos_b = jnp.max(jnp.where(rk >= 0, pos, -1), axis=1)

    xs = sc_scatter_rows2(h, pos_a, pos_b, rt * ts)
    a = moe_up(xs, w1, w3, tiles, rt)
    y = moe_down(a, w2, tiles, rt)
    y2 = sc_gather_rows(y, jnp.concatenate([pos_a, pos_b])).reshape(2, M, -1)
    return moe_finish(x, mod_l, y2, gate, rank, norm_w, T)


def _final_norm_kernel(x_ref, w_ref, o_ref):
    x = x_ref[...]
    ms = jnp.mean(x * x, axis=-1, keepdims=True)
    o_ref[...] = x * lax.rsqrt(ms + NORM_EPS) * w_ref[...]


def final_norm(x, w, *, tm=1024):
    M, D = x.shape
    return pl.pallas_call(
        _final_norm_kernel,
        grid=(M // tm,),
        in_specs=[pl.BlockSpec((tm, D), lambda i: (i, 0)), pl.BlockSpec((1, D), lambda i: (0, 0))],
        out_specs=pl.BlockSpec((tm, D), lambda i: (i, 0)),
        out_shape=jax.ShapeDtypeStruct((M, D), F32),
        compiler_params=_cparams("parallel"),
        name="final_norm",
    )(x, w)


def nsa_constants(T):
    n_sel = T // SEL_LEN
    nsp = max(LANES, n_sel)
    ncp = T // CMP_STRIDE
    cmp_start = np.arange(ncp) * CMP_STRIDE
    sel_start = np.arange(nsp) * SEL_LEN
    ov = ((cmp_start[:, None] < sel_start[None, :] + SEL_LEN)
          & (cmp_start[:, None] + CMP_LEN > sel_start[None, :]))
    ov[(T - CMP_LEN) // CMP_STRIDE + 1:] = False
    ov[:, n_sel:] = False
    et_mat = ((np.arange(T)[:, None] // SEL_LEN) == np.arange(nsp)[None, :]) * SEL_BONUS
    return jnp.asarray(ov.T, BF16), jnp.asarray(et_mat, BF16)


def token_mixing(x, mod_l, lw, consts, B, T):
    M = B * T
    cos_t, sin_t, ov_t, e_mat, ret_consts, ex = consts
    l = lw["layer"]
    p1 = proj_rope(x, mod_l, lw["norm_mix"], lw["w1"], l, cos_t, sin_t, p1_scales(), T).reshape(B, T, P1_COLS)
    p2 = proj_plain(x, mod_l, lw["norm_mix"], lw["w2"], l, T).reshape(B, T, P2_COLS)

    rows16 = lambda a: a.reshape(B, T // CMP_STRIDE, CMP_STRIDE * LANES)
    xr = jnp.stack([rows16(p1[:, :, P1_NKC:P1_NKC + LANES]), rows16(p2[:, :, P2_NVC:P2_NVC + LANES])])
    cmp_out = compress(xr, lw["cmp_pe"], lw["cmp_w1"], lw["cmp_w2"])
    cmp_out = cmp_out.transpose(0, 1, 3, 2, 4).reshape(2, B, T // CMP_STRIDE, LANES)
    o_cmp, sel, o_win = nsa_cmp_select_window(p1, p2, cmp_out[0], cmp_out[1], ov_t, T)
    o_sel = nsa_selected(p1, p2, sel, e_mat, T)

    o_ret = retention(p1, p2, ret_consts, T)

    ff = p2[:, :, P2_SMALL + 3 * NSA_HEADS:P2_SMALL + 3 * NSA_HEADS + FOX_HEADS].astype(F32)
    ff = ff.transpose(0, 2, 1).reshape(B, FOX_HEADS, T // LANES, LANES)
    cum = fox_cum(ff, lw["fox_bias"]).reshape(B, FOX_HEADS // 2, 2, 1, T)
    o_fox = fox_attention(p2, cum, T)

    return readout(o_cmp.reshape(M, -1), o_sel.reshape(M, -1), o_win.reshape(M, -1), p2.reshape(M, P2_COLS),
                   o_ret.reshape(M, -1), o_fox.reshape(M, -1), x, mod_l, ex,
                   lw["wn"], lw["wr"], lw["wf"], lw["wo"], l, T)


def mixer_weights(norm_mix, w_in, cmp_k_pe, cmp_k_w1, cmp_k_w2, cmp_v_pe, cmp_v_w1, cmp_v_w2, fox_f_bias,
                  w_read_nsa, w_read_ret, w_read_fox, w_out):
    depth = w_in.shape[0]
    w1, w2 = split_w_in(w_in)
    pe = jnp.stack([cmp_k_pe.reshape(depth, 1, -1), cmp_v_pe.reshape(depth, 1, -1)], axis=1)
    pe = jnp.broadcast_to(pe, (depth, 2, 8, pe.shape[-1])).astype(BF16)
    shared = {
        "w1": w1, "w2": w2,
        "wn": w_read_nsa.astype(BF16),
        "wr": w_read_ret.astype(BF16),
        "wf": w_read_fox.astype(BF16),
        "wo": w_out.astype(BF16),
    }
    def per_group(a, axis):
        a = jnp.moveaxis(a, axis, -1)
        a = a.reshape(a.shape[:-1] + (CMP_LEN, HEAD_DIM))
        z = jnp.zeros_like(a)
        both = jnp.stack([jnp.concatenate([a, z], axis=-1), jnp.concatenate([z, a], axis=-1)], axis=0)
        both = both.reshape(both.shape[:-2] + (CMP_LEN * LANES,))
        return jnp.moveaxis(jnp.moveaxis(both, -1, axis + 1), 0, axis)

    pe = jnp.swapaxes(per_group(pe, 3), 2, 3)
    cmp_w1 = per_group(jnp.stack([cmp_k_w1, cmp_v_w1], axis=1).astype(BF16), 2)
    cmp_w2 = jnp.stack([cmp_k_w2, cmp_v_w2], axis=1).astype(BF16)
    return [dict(shared, layer=l, norm_mix=norm_mix[l].reshape(1, -1), cmp_pe=pe[l], cmp_w1=cmp_w1[l], cmp_w2=cmp_w2[l],
                 fox_bias=jnp.broadcast_to(fox_f_bias[l][:, None, None], (FOX_HEADS, 1, LANES)))
            for l in range(depth)]


def kernel(x, c, ada_w, ada_b, norm_mix, norm_ffn, w_in, cmp_k_pe, cmp_k_w1, cmp_k_w2, cmp_v_pe, cmp_v_w1,
           cmp_v_w2, fox_f_bias, w_read_nsa, w_read_ret, w_read_fox, w_out, ffn_w1, ffn_w3, ffn_w2, router_w,
           moe_w1, moe_w3, moe_w2, final_norm_w):
    B, T, D = x.shape
    M = B * T
    depth = ada_w.shape[0]
    mod = modulation(c, ada_w, ada_b)
    cos_t, sin_t = rope_tables(T)
    ov_t, e_mat = nsa_constants(T)
    consts = (cos_t, sin_t, ov_t, e_mat, retention_consts(), nsa_gate_expand())
    xs = x.reshape(M, D)
    lws = mixer_weights(norm_mix, w_in, cmp_k_pe, cmp_k_w1, cmp_k_w2, cmp_v_pe, cmp_v_w1, cmp_v_w2,
                        fox_f_bias, w_read_nsa, w_read_ret, w_read_fox, w_out)
    for l in range(depth):
        xs = token_mixing(xs, mod[l], lws[l], consts, B, T)
        nf = norm_ffn[l].reshape(1, D)
        if l % 2 == 0:
            k = l // 2
            xs = ffn(xs, mod[l], nf, ffn_w1[k].astype(BF16), ffn_w3[k].astype(BF16), ffn_w2[k].astype(BF16), T)
        else:
            k = l // 2
            fuse = final_norm_w.reshape(1, D) if l == depth - 1 else None
            xs = moe_ffn(xs, mod[l], nf, router_w[k], moe_w1[k], moe_w3[k], moe_w2[k], T, fuse)
    if depth % 2 == 1:
        xs = final_norm(xs, final_norm_w.reshape(1, D))
    return xs.reshape(B, T, D)
```
